```python
import jax, jax.numpy as jnp
from jax import lax
import numpy as np

D_MODEL = 1024
BATCH = 32
SEQ = 2048
DEPTH = 1

MLA_HEADS = 4
QK_NOPE_DIM = 128
QK_ROPE_DIM = 64
QK_DIM = QK_NOPE_DIM + QK_ROPE_DIM
V_HEAD_DIM = 128
Q_LORA_RANK = 512
KV_LORA_RANK = 256
MLA_WIDTH = MLA_HEADS * V_HEAD_DIM
ROPE_THETA = 10000.0
Q_BLOCK = 128
POOL_WINDOWS = (2, 4, 8, 16)
POOL_GROUPS = len(POOL_WINDOWS)
POOL_WIDTH = D_MODEL - MLA_WIDTH
POOL_GROUP_DIM = POOL_WIDTH // POOL_GROUPS
MIX_WIDTH = MLA_WIDTH + POOL_WIDTH
IN_SPLITS = (Q_LORA_RANK, KV_LORA_RANK, QK_ROPE_DIM, MLA_WIDTH, POOL_WIDTH, POOL_WIDTH)
IN_WIDTH = sum(IN_SPLITS)
RMS_EPS = 1e-6
LN_EPS = 1e-5

kernel_name = "hybrid_mla_multiscale_pool_deepnorm"


def rms_norm(x, g):
    xf = x.astype(jnp.float32)
    inv = lax.rsqrt(jnp.mean(xf * xf, axis=-1, keepdims=True) + RMS_EPS)
    return (xf * inv).astype(x.dtype) * g


def layer_norm(x, g, b):
    xf = x.astype(jnp.float32)
    mu = jnp.mean(xf, axis=-1, keepdims=True)
    var = jnp.mean(jnp.square(xf - mu), axis=-1, keepdims=True)
    return ((xf - mu) * lax.rsqrt(var + LN_EPS)).astype(x.dtype) * g + b


def rope_cos_sin(positions, dtype):
    half = QK_ROPE_DIM // 2
    inv_freq = ROPE_THETA ** (-jnp.arange(half, dtype=jnp.float32) / half)
    ang = positions.astype(jnp.float32)[..., None] * inv_freq
    return jnp.cos(ang).astype(dtype), jnp.sin(ang).astype(dtype)


def apply_rope(t, cos, sin):
    t1, t2 = jnp.split(t, 2, axis=-1)
    return jnp.concatenate([t1 * cos - t2 * sin, t1 * sin + t2 * cos], axis=-1)


def mla_branch(x_q, x_kv, k_rope_raw, positions, q_norm_g, w_uq, kv_norm_g, w_ukv):
    B, S, _ = x_q.shape
    cos, sin = rope_cos_sin(positions, x_q.dtype)
    q = (rms_norm(x_q, q_norm_g) @ w_uq).reshape(B, S, MLA_HEADS, QK_DIM)
    q_nope, q_rope = q[..., :QK_NOPE_DIM], q[..., QK_NOPE_DIM:]
    q_rope = apply_rope(q_rope, cos[:, :, None, :], sin[:, :, None, :])
    kv = (rms_norm(x_kv, kv_norm_g) @ w_ukv).reshape(B, S, MLA_HEADS, QK_NOPE_DIM + V_HEAD_DIM)
    k_nope, v = kv[..., :QK_NOPE_DIM], kv[..., QK_NOPE_DIM:]
    k_rope = apply_rope(k_rope_raw, cos, sin)
    scale = QK_DIM ** -0.5
    nb = S // Q_BLOCK
    qn_b = q_nope.reshape(B, nb, Q_BLOCK, MLA_HEADS, QK_NOPE_DIM).transpose(1, 0, 2, 3, 4)
    qr_b = q_rope.reshape(B, nb, Q_BLOCK, MLA_HEADS, QK_ROPE_DIM).transpose(1, 0, 2, 3, 4)
    pos_b = positions.reshape(B, nb, Q_BLOCK).transpose(1, 0, 2)
    neg = jnp.finfo(jnp.float32).min

    def attend(args):
        qn, qr, pq = args
        s = (jnp.einsum('bqhd,bkhd->bhqk', qn, k_nope)
             + jnp.einsum('bqhr,bkr->bhqk', qr, k_rope)).astype(jnp.float32) * scale
        mask = pq[:, None, :, None] >= positions[:, None, None, :]
        p = jax.nn.softmax(jnp.where(mask, s, neg), axis=-1)
        return jnp.einsum('bhqk,bkhd->bqhd', p.astype(v.dtype), v)

    o = lax.map(attend, (qn_b, qr_b, pos_b))
    return o.transpose(1, 0, 2, 3, 4).reshape(B, S, MLA_WIDTH)


def pool_branch(u, pool_w, pool_scale):
    B, S, _ = u.shape
    uf = u.astype(jnp.float32).reshape(B, S, POOL_GROUPS, POOL_GROUP_DIM)
    cs = jnp.concatenate([jnp.zeros((B, 1, POOL_GROUPS, POOL_GROUP_DIM), jnp.float32),
                          jnp.cumsum(uf, axis=1)], axis=1)
    hi = jnp.arange(S) + 1
    means = []
    for g, w in enumerate(POOL_WINDOWS):
        lo = jnp.maximum(hi - w, 0)
        cnt = (hi - lo).astype(jnp.float32)[None, :, None]
        means.append((cs[:, hi, g] - cs[:, lo, g]) / cnt)
    pooled = jnp.stack(means, axis=2) - uf
    mixed = jnp.einsum('bsgc,gcd->bsgd', pooled.astype(u.dtype), pool_w)
    return mixed.reshape(B, S, POOL_WIDTH) * pool_scale


def _fwd_setup_inputs(seed: int = 0) -> dict:
    key = jax.random.key(seed)
    ks = jax.random.split(key, 12)
    beta = (8.0 * DEPTH) ** -0.25
    nrm = jax.random.normal
    return {
        "x": nrm(ks[0], (BATCH, SEQ, D_MODEL), jnp.float32),
        "positions": jnp.broadcast_to(jnp.arange(SEQ, dtype=jnp.int32), (BATCH, SEQ)),
        "w_in": nrm(ks[1], (D_MODEL, IN_WIDTH), jnp.float32) * D_MODEL ** -0.5,
        "q_norm_g": 1.0 + 0.05 * nrm(ks[2], (Q_LORA_RANK,), jnp.float32),
        "w_uq": nrm(ks[3], (Q_LORA_RANK, MLA_HEADS * QK_DIM), jnp.float32) * Q_LORA_RANK ** -0.5,
        "kv_norm_g": 1.0 + 0.05 * nrm(ks[4], (KV_LORA_RANK,), jnp.float32),
        "w_ukv": nrm(ks[5], (KV_LORA_RANK, MLA_HEADS * (QK_NOPE_DIM + V_HEAD_DIM)), jnp.float32) * KV_LORA_RANK ** -0.5,
        "pool_w": nrm(ks[6], (POOL_GROUPS, POOL_GROUP_DIM, POOL_GROUP_DIM), jnp.float32) * POOL_GROUP_DIM ** -0.5,
        "pool_scale": 1.0 + 0.1 * nrm(ks[7], (POOL_WIDTH,), jnp.float32),
        "w_out": nrm(ks[8], (MIX_WIDTH, D_MODEL), jnp.float32) * (MIX_WIDTH ** -0.5) * beta,
        "ln_g": 1.0 + 0.05 * nrm(ks[9], (DEPTH, D_MODEL), jnp.float32),
        "ln_b": 0.02 * nrm(ks[10], (DEPTH, D_MODEL), jnp.float32),
    }


def _fwd_reference(x, positions, w_in, q_norm_g, w_uq, kv_norm_g, w_ukv, pool_w, pool_scale, w_out, ln_g, ln_b):
    alpha = (2.0 * DEPTH) ** 0.25
    splits = [int(c) for c in np.cumsum(IN_SPLITS)[:-1]]
    for layer in range(DEPTH):
        h = x @ w_in
        x_q, x_kv, k_rope_raw, gate_a, u_pool, gate_b = jnp.split(h, splits, axis=-1)
        y_a = mla_branch(x_q, x_kv, k_rope_raw, positions, q_norm_g, w_uq, kv_norm_g, w_ukv) * jax.nn.silu(gate_a)
        y_b = pool_branch(u_pool, pool_w, pool_scale) * jax.nn.silu(gate_b)
        mix = jnp.concatenate([y_a, y_b], axis=-1) @ w_out
        x = layer_norm(alpha * x + mix, ln_g[layer], ln_b[layer])
    return x


import jax as _jax
import jax.numpy as _jnp

TWIN_FORMAT = 'train_step'
FWD_PARAMS = ['x', 'positions', 'w_in', 'q_norm_g', 'w_uq', 'kv_norm_g', 'w_ukv', 'pool_w', 'pool_scale', 'w_out', 'ln_g', 'ln_b']
TWIN_WEIGHTS = ['w_in', 'q_norm_g', 'w_uq', 'kv_norm_g', 'w_ukv', 'pool_w', 'pool_scale', 'w_out', 'ln_g', 'ln_b']
TWIN_DIFF_INPUT = 'x'
TWIN_INPUTS = ['x', 'positions', 'w_in', 'q_norm_g', 'w_uq', 'kv_norm_g', 'w_ukv', 'pool_w', 'pool_scale', 'w_out', 'ln_g', 'ln_b', 'loss_target', 'm_w_in', 'm_q_norm_g', 'm_w_uq', 'm_kv_norm_g', 'm_w_ukv', 'm_pool_w', 'm_pool_scale', 'm_w_out', 'm_ln_g', 'm_ln_b', 'v_w_in', 'v_q_norm_g', 'v_w_uq', 'v_kv_norm_g', 'v_w_ukv', 'v_pool_w', 'v_pool_scale', 'v_w_out', 'v_ln_g', 'v_ln_b']
TWIN_OUTPUTS = ['loss', 'grad_x', 'grad_w_in', 'grad_q_norm_g', 'grad_w_uq', 'grad_kv_norm_g', 'grad_w_ukv', 'grad_pool_w', 'grad_pool_scale', 'grad_w_out', 'grad_ln_g', 'grad_ln_b', 'delta_w_in', 'delta_q_norm_g', 'delta_w_uq', 'delta_kv_norm_g', 'delta_w_ukv', 'delta_pool_w', 'delta_pool_scale', 'delta_w_out', 'delta_ln_g', 'delta_ln_b', 'new_m_w_in', 'new_m_q_norm_g', 'new_m_w_uq', 'new_m_kv_norm_g', 'new_m_w_ukv', 'new_m_pool_w', 'new_m_pool_scale', 'new_m_w_out', 'new_m_ln_g', 'new_m_ln_b', 'new_v_w_in', 'new_v_q_norm_g', 'new_v_w_uq', 'new_v_kv_norm_g', 'new_v_w_ukv', 'new_v_pool_w', 'new_v_pool_scale', 'new_v_w_out', 'new_v_ln_g', 'new_v_ln_b']
TWIN_LEAF_KINDS = {'loss': 'loss', 'grad_x': 'grad_x', 'grad_w_in': 'grad_w', 'grad_q_norm_g': 'grad_w', 'grad_w_uq': 'grad_w', 'grad_kv_norm_g': 'grad_w', 'grad_w_ukv': 'grad_w', 'grad_pool_w': 'grad_w', 'grad_pool_scale': 'grad_w', 'grad_w_out': 'grad_w', 'grad_ln_g': 'grad_w', 'grad_ln_b': 'grad_w', 'delta_w_in': 'delta_w', 'delta_q_norm_g': 'delta_w', 'delta_w_uq': 'delta_w', 'delta_kv_norm_g': 'delta_w', 'delta_w_ukv': 'delta_w', 'delta_pool_w': 'delta_w', 'delta_pool_scale': 'delta_w', 'delta_w_out': 'delta_w', 'delta_ln_g': 'delta_w', 'delta_ln_b': 'delta_w', 'new_m_w_in': 'new_m', 'new_m_q_norm_g': 'new_m', 'new_m_w_uq': 'new_m', 'new_m_kv_norm_g': 'new_m', 'new_m_w_ukv': 'new_m', 'new_m_pool_w': 'new_m', 'new_m_pool_scale': 'new_m', 'new_m_w_out': 'new_m', 'new_m_ln_g': 'new_m', 'new_m_ln_b': 'new_m', 'new_v_w_in': 'new_v', 'new_v_q_norm_g': 'new_v', 'new_v_w_uq': 'new_v', 'new_v_kv_norm_g': 'new_v', 'new_v_w_ukv': 'new_v', 'new_v_pool_w': 'new_v', 'new_v_pool_scale': 'new_v', 'new_v_w_out': 'new_v', 'new_v_ln_g': 'new_v', 'new_v_ln_b': 'new_v'}


def _forward(args):
    return _fwd_reference(*[args[k] for k in FWD_PARAMS])


def _output_shape():
    out = _jax.eval_shape(lambda: _forward(_fwd_setup_inputs(0)))
    return out.shape, out.dtype

N_MICROBATCH = 1
ADAM_LR = 0.001
ADAM_B1 = 0.9
ADAM_B2 = 0.999
ADAM_EPS = 1e-08
ADAM_WD = 0.01
ADAM_STEP = 10
PER_EXAMPLE_BATCH_AXIS = {'x': 0, 'positions': 0, 'loss_target': 0}
SHARED_INPUTS = []
_WEIGHT_DTYPES = {'w_in': _jnp.float32, 'q_norm_g': _jnp.float32, 'w_uq': _jnp.float32, 'kv_norm_g': _jnp.float32, 'w_ukv': _jnp.float32, 'pool_w': _jnp.float32, 'pool_scale': _jnp.float32, 'w_out': _jnp.float32, 'ln_g': _jnp.float32, 'ln_b': _jnp.float32}
MOMENT_SCALE = {'w_in': 4.495257e-02, 'q_norm_g': 1.344303e-02, 'w_uq': 1.104970e-02, 'kv_norm_g': 3.034760e-02, 'w_ukv': 1.372495e-02, 'pool_w': 6.680379e-02, 'pool_scale': 6.895978e-02, 'w_out': 7.880702e-02, 'ln_g': 6.397332e+01, 'ln_b': 7.447745e-01}


def _to_microbatches(a, axis):
    t = _jnp.moveaxis(a, axis, 0)
    t = t.reshape((N_MICROBATCH, t.shape[0] // N_MICROBATCH) + t.shape[1:])
    return _jnp.moveaxis(t, 1, axis + 1)


def setup_inputs(seed: int = 0) -> dict:
    inp = _fwd_setup_inputs(seed)
    key = _jax.random.fold_in(_jax.random.key(seed), 7919)
    shape, _ = _output_shape()
    out = dict(inp)
    out["loss_target"] = _jax.random.normal(_jax.random.fold_in(key, 0), shape, _jnp.float32)
    for i, name in enumerate(TWIN_WEIGHTS):
        w = inp[name].astype(_jnp.float32)
        if MOMENT_SCALE is None:
            s = _jnp.sqrt(_jnp.mean(_jnp.square(w)) + 1e-30)
        else:
            s = MOMENT_SCALE[name]
        km, kv = _jax.random.split(_jax.random.fold_in(key, i + 1))
        out[name] = w
        out["m_" + name] = s * _jax.random.normal(km, w.shape, _jnp.float32)
        out["v_" + name] = (s * s) * _jax.random.uniform(kv, w.shape, _jnp.float32, 0.5, 1.5)
    if N_MICROBATCH > 1:
        for name, axis in PER_EXAMPLE_BATCH_AXIS.items():
            out[name] = _to_microbatches(out[name], axis)
    return {'x': out['x'], 'positions': out['positions'], 'w_in': out['w_in'], 'q_norm_g': out['q_norm_g'], 'w_uq': out['w_uq'], 'kv_norm_g': out['kv_norm_g'], 'w_ukv': out['w_ukv'], 'pool_w': out['pool_w'], 'pool_scale': out['pool_scale'], 'w_out': out['w_out'], 'ln_g': out['ln_g'], 'ln_b': out['ln_b'], 'loss_target': out['loss_target'], 'm_w_in': out['m_w_in'], 'm_q_norm_g': out['m_q_norm_g'], 'm_w_uq': out['m_w_uq'], 'm_kv_norm_g': out['m_kv_norm_g'], 'm_w_ukv': out['m_w_ukv'], 'm_pool_w': out['m_pool_w'], 'm_pool_scale': out['m_pool_scale'], 'm_w_out': out['m_w_out'], 'm_ln_g': out['m_ln_g'], 'm_ln_b': out['m_ln_b'], 'v_w_in': out['v_w_in'], 'v_q_norm_g': out['v_q_norm_g'], 'v_w_uq': out['v_w_uq'], 'v_kv_norm_g': out['v_kv_norm_g'], 'v_w_ukv': out['v_w_ukv'], 'v_pool_w': out['v_pool_w'], 'v_pool_scale': out['v_pool_scale'], 'v_w_out': out['v_w_out'], 'v_ln_g': out['v_ln_g'], 'v_ln_b': out['v_ln_b']}


def _loss(weights, diff, rest, loss_target):
    with _jax.named_scope("forward"):
        args = {**rest, TWIN_DIFF_INPUT: diff, **{k: w.astype(_WEIGHT_DTYPES[k]) for k, w in weights.items()}}
        y = _forward(args)
    with _jax.named_scope("loss_head"):
        err = _jnp.square(y.astype(_jnp.float32) - loss_target)
        return 0.5 * _jnp.sum(_jnp.mean(err, axis=-1)) if err.ndim else 0.5 * err


def _adamw(w, g, m, v):
    m = ADAM_B1 * m + (1.0 - ADAM_B1) * g
    v = ADAM_B2 * v + (1.0 - ADAM_B2) * _jnp.square(g)
    m_hat = m / (1.0 - ADAM_B1 ** ADAM_STEP)
    v_hat = v / (1.0 - ADAM_B2 ** ADAM_STEP)
    delta = -ADAM_LR * (m_hat / (_jnp.sqrt(v_hat) + ADAM_EPS) + ADAM_WD * w)
    return delta, m, v


def reference(x, positions, w_in, q_norm_g, w_uq, kv_norm_g, w_ukv, pool_w, pool_scale, w_out, ln_g, ln_b, loss_target, m_w_in, m_q_norm_g, m_w_uq, m_kv_norm_g, m_w_ukv, m_pool_w, m_pool_scale, m_w_out, m_ln_g, m_ln_b, v_w_in, v_q_norm_g, v_w_uq, v_kv_norm_g, v_w_ukv, v_pool_w, v_pool_scale, v_w_out, v_ln_g, v_ln_b):
    given = dict(x=x, positions=positions, w_in=w_in, q_norm_g=q_norm_g, w_uq=w_uq, kv_norm_g=kv_norm_g, w_ukv=w_ukv, pool_w=pool_w, pool_scale=pool_scale, w_out=w_out, ln_g=ln_g, ln_b=ln_b, loss_target=loss_target, m_w_in=m_w_in, m_q_norm_g=m_q_norm_g, m_w_uq=m_w_uq, m_kv_norm_g=m_kv_norm_g, m_w_ukv=m_w_ukv, m_pool_w=m_pool_w, m_pool_scale=m_pool_scale, m_w_out=m_w_out, m_ln_g=m_ln_g, m_ln_b=m_ln_b, v_w_in=v_w_in, v_q_norm_g=v_q_norm_g, v_w_uq=v_w_uq, v_kv_norm_g=v_kv_norm_g, v_w_ukv=v_w_ukv, v_pool_w=v_pool_w, v_pool_scale=v_pool_scale, v_w_out=v_w_out, v_ln_g=v_ln_g, v_ln_b=v_ln_b)
    weights = {n: given[n] for n in TWIN_WEIGHTS}
    shared = {n: given[n] for n in SHARED_INPUTS}
    per_example = {n: given[n] for n in ['x', 'positions']}
    grad_fn = _jax.value_and_grad(_loss, argnums=(0, 1))

    def one_microbatch(ex, loss_target):
        ex = dict(ex)
        diff = ex.pop(TWIN_DIFF_INPUT)
        return grad_fn(weights, diff, {**shared, **ex}, loss_target)

    if N_MICROBATCH == 1:
        loss, (grad_w, grad_x) = one_microbatch(per_example, given["loss_target"])
    else:
        def body(carry, xs):
            loss_sum, grad_sum = carry
            l_k, (gw_k, gx_k) = one_microbatch(xs[0], xs[1])
            with _jax.named_scope("update"):
                return (loss_sum + l_k, _jax.tree.map(_jnp.add, grad_sum, gw_k)), gx_k

        init = (_jnp.zeros((), _jnp.float32), _jax.tree.map(_jnp.zeros_like, weights))
        (loss, grad_w), grad_x = _jax.lax.scan(body, init, (per_example, given["loss_target"]))
    with _jax.named_scope("update"):
        delta_w, new_m, new_v = {}, {}, {}
        for n in TWIN_WEIGHTS:
            delta_w[n], new_m[n], new_v[n] = _adamw(weights[n], grad_w[n], given["m_" + n], given["v_" + n])
    return (loss, grad_x, *[grad_w[n] for n in TWIN_WEIGHTS], *[delta_w[n] for n in TWIN_WEIGHTS],
            *[new_m[n] for n in TWIN_WEIGHTS], *[new_v[n] for n in TWIN_WEIGHTS])
```

```python
import functools

import jax
import jax.numpy as jnp
from jax import lax
from jax.experimental import pallas as pl
from jax.experimental.pallas import tpu as pltpu

F32 = jnp.float32
BF16 = jnp.bfloat16
MESH = pl.DeviceIdType.MESH

N_DEV = 8
D_MODEL = 1024
HEADS = 4
NOPE = 128
ROPE = 64
V_DIM = 128
QK_DIM = NOPE + ROPE
HEAD_PAD = 256
Q_LORA = 512
KV_LORA = 256
MLA_W = HEADS * V_DIM
POOL_W = 512
POOL_G = 4
POOL_C = 128
POOL_HALO = 16
IN_W = 2368
IN_WP = 2432
C_XQ, C_XKV, C_KR, C_GA, C_U, C_GB = 0, 512, 768, 896, 1408, 1920
ROPE_THETA = 10000.0
RMS_EPS = 1e-6
LN_EPS = 1e-5
ALPHA = 2.0 ** 0.25
SCALE = QK_DIM ** -0.5
NEG = float(jnp.finfo(jnp.float32).min)

ADAM_LR = 0.001
ADAM_B1 = 0.9
ADAM_B2 = 0.999
ADAM_EPS = 1e-08
ADAM_WD = 0.01
ADAM_STEP = 10

R_IN, R_UQ, R_UKV, R_OUT = 296, 48, 32, 128
SLAB_ROWS = 512
O_UQ, O_UKV, O_OUT = R_IN, R_IN + R_UQ, R_IN + R_UQ + R_UKV

V7X_VMEM_BYTES = 64 * 1024 * 1024


def _params(vmem_mb, semantics=None):
    assert vmem_mb * 1024 * 1024 < V7X_VMEM_BYTES
    return pltpu.CompilerParams(vmem_limit_bytes=vmem_mb * 1024 * 1024, dimension_semantics=semantics)


def _dot(a, b):
    return jnp.dot(a, b, preferred_element_type=F32)


def _dot_nt(a, b):
    return lax.dot_general(a, b, (((1,), (1,)), ((), ())), preferred_element_type=F32)


def _dot_tn(a, b):
    return lax.dot_general(a, b, (((0,), (0,)), ((), ())), preferred_element_type=F32)


def _rope_fwd(t, c, sa, sb):
    return t * c + pltpu.roll(t, 96, 1) * sa + pltpu.roll(t, 32, 1) * sb


def _rope_bwd(d, c, sa, sb):
    return d * c + pltpu.roll(d * sa, 32, 1) + pltpu.roll(d * sb, 96, 1)


def _silu_parts(g):
    sig = jax.nn.sigmoid(g)
    return g * sig, sig * (1.0 + g * (1.0 - sig))


def _pool_cnt(row0, rows):
    t = row0 + lax.broadcasted_iota(jnp.int32, (rows, POOL_W), 0)
    w = 2 << (lax.broadcasted_iota(jnp.int32, (rows, POOL_W), 1) // POOL_C)
    return jnp.minimum(t + 1, w).astype(F32)


def _pick_groups(s2, s4, s8, s16, lo, hi):
    return jnp.concatenate([s2[lo:hi, 0:128], s4[lo:hi, 128:256], s8[lo:hi, 256:384], s16[lo:hi, 384:512]], axis=1)


def _all_gather(shard, name, reduce):
    m_per, n = shard.shape

    def body(x_ref, *rest):
        if reduce:
            sum_ref, out_ref, send_sems, recv_sems, local_sem = rest
        else:
            out_ref, send_sems, recv_sems, local_sem = rest
        x, y, c = lax.axis_index("x"), lax.axis_index("y"), lax.axis_index("c")
        me, sibling = (x, y, c), (x, y, 1 - c)
        chips = [(1 - x, y), (x, 1 - y), (1 - x, 1 - y)]

        def rows(px, py, pc):
            return out_ref.at[pl.ds((4 * px + 2 * py + pc) * m_per, m_per), :]

        def copy(k, block, to, src=None):
            return pltpu.make_async_remote_copy(
                src_ref=rows(*block) if src is None else src, dst_ref=rows(*block),
                send_sem=send_sems.at[k], recv_sem=recv_sems.at[k], device_id=to, device_id_type=MESH)

        mine = pltpu.make_async_copy(x_ref, rows(*me), local_sem)
        mine.start()
        first = [copy(0, me, sibling, src=x_ref)]
        first += [copy(1 + j, me, (*chip, c), src=x_ref) for j, chip in enumerate(chips)]
        for cp in first:
            cp.start()
        passed = [copy(4 + j, (*chip, c), sibling) for j, chip in enumerate(chips)]
        for j, chip in enumerate(chips):
            copy(1 + j, (*chip, c), me).wait_recv()
            passed[j].start()
        copy(0, sibling, me).wait_recv()
        for j, chip in enumerate(chips):
            copy(4 + j, (*chip, 1 - c), me).wait_recv()
        for cp in first + passed:
            cp.wait_send()
        mine.wait()
        if reduce:
            acc = out_ref[pl.ds(0, m_per), :]
            for d in range(1, N_DEV):
                acc = acc + out_ref[pl.ds(d * m_per, m_per), :]
            sum_ref[...] = acc

    gathered = jax.ShapeDtypeStruct((N_DEV * m_per, n), shard.dtype)
    vmem = pl.BlockSpec(memory_space=pltpu.VMEM)
    sems = [pltpu.SemaphoreType.DMA((7,)), pltpu.SemaphoreType.DMA((7,)), pltpu.SemaphoreType.DMA]
    if reduce:
        return pl.pallas_call(
            body, name=name, out_shape=jax.ShapeDtypeStruct((m_per, n), shard.dtype),
            in_specs=[vmem], out_specs=vmem,
            scratch_shapes=[pltpu.VMEM(gathered.shape, gathered.dtype)] + sems,
            compiler_params=_params(32))(shard)
    return pl.pallas_call(
        body, name=name, out_shape=gathered, in_specs=[vmem], out_specs=vmem, scratch_shapes=sems,
        compiler_params=_params(32))(shard)


def _reduce_scatter(gslab, name):
    _, rr, ww = gslab.shape
    chunk = 64

    def body(g_hbm, out_ref, own_ref, recv1_ref, sendb_ref, recv2_ref, ld_sems, s1_send, s1_recv, s2_send, s2_recv):
        x, y, c = lax.axis_index("x"), lax.axis_index("y"), lax.axis_index("c")
        sibling = (x, y, 1 - c)
        chips = [(1 - x, y), (x, 1 - y), (1 - x, 1 - y)]
        loads = [pltpu.make_async_copy(g_hbm.at[2 * k + c], own_ref.at[k], ld_sems.at[k]) for k in range(4)]
        stage1 = [pltpu.make_async_remote_copy(
            src_ref=g_hbm.at[2 * k + (1 - c)], dst_ref=recv1_ref.at[k], send_sem=s1_send.at[k],
            recv_sem=s1_recv.at[k], device_id=sibling, device_id_type=MESH) for k in range(4)]
        for cp in loads + stage1:
            cp.start()
        for cp in loads:
            cp.wait()
        for cp in stage1:
            cp.wait_recv()

        stage2 = []
        for r, (cx, cy) in enumerate(chips):
            kk = 2 * cx + cy

            def pack(i, carry, r=r, kk=kk):
                rows = pl.ds(pl.multiple_of(i * chunk, chunk), chunk)
                sendb_ref[r, rows, :] = (own_ref[kk, rows, :] + recv1_ref[kk, rows, :]).astype(BF16)
                return carry

            lax.fori_loop(0, rr // chunk, pack, 0)
            cp = pltpu.make_async_remote_copy(
                src_ref=sendb_ref.at[r], dst_ref=recv2_ref.at[r], send_sem=s2_send.at[r],
                recv_sem=s2_recv.at[r], device_id=(cx, cy, c), device_id_type=MESH)
            cp.start()
            stage2.append(cp)
        for cp in stage2:
            cp.wait_recv()
        mine = 2 * x + y

        def total(i, carry):
            rows = pl.ds(pl.multiple_of(i * chunk, chunk), chunk)
            acc = own_ref[mine, rows, :] + recv1_ref[mine, rows, :]
            for r in range(3):
                acc = acc + recv2_ref[r, rows, :].astype(F32)
            out_ref[rows, :] = acc
            return carry

        lax.fori_loop(0, rr // chunk, total, 0)
        for cp in stage1 + stage2:
            cp.wait_send()

    return pl.pallas_call(
        body, name=name, out_shape=jax.ShapeDtypeStruct((rr, ww), F32),
        in_specs=[pl.BlockSpec(memory_space=pl.ANY)], out_specs=pl.BlockSpec(memory_space=pltpu.VMEM),
        scratch_shapes=[
            pltpu.VMEM((4, rr, ww), F32), pltpu.VMEM((4, rr, ww), F32),
            pltpu.VMEM((3, rr, ww), BF16), pltpu.VMEM((3, rr, ww), BF16),
            pltpu.SemaphoreType.DMA((4,)), pltpu.SemaphoreType.DMA((4,)), pltpu.SemaphoreType.DMA((4,)),
            pltpu.SemaphoreType.DMA((3,)), pltpu.SemaphoreType.DMA((3,))],
        compiler_params=_params(48))(gslab)


def _fwd_in(x2, w1, gq, gkv, wq, wkv, rc, rsa, rsb, tm):
    tt = x2.shape[0]

    def body(x_ref, w1_ref, gq_ref, gkv_ref, wq_ref, wkv_ref, c_ref, sa_ref, sb_ref,
             xq_ref, xkv_ref, ga_ref, u_ref, gb_ref, qn_ref, kvn_ref, qcat_ref, kcat_ref, v_ref):
        h = _dot(x_ref[...].astype(BF16), w1_ref[...])
        xq = h[:, C_XQ:C_XKV]
        xkv = h[:, C_XKV:C_KR]
        xq_ref[...] = xq
        xkv_ref[...] = xkv
        ga_ref[...] = h[:, C_GA:C_U]
        u_ref[...] = h[:, C_U:C_GB]
        gb_ref[...] = h[:, C_GB:IN_WP]
        c, sa, sb = c_ref[...], sa_ref[...], sb_ref[...]
        qn = (xq * lax.rsqrt(jnp.mean(xq * xq, axis=-1, keepdims=True) + RMS_EPS) * gq_ref[...]).astype(BF16)
        kvn = (xkv * lax.rsqrt(jnp.mean(xkv * xkv, axis=-1, keepdims=True) + RMS_EPS) * gkv_ref[...]).astype(BF16)
        qn_ref[...] = qn
        kvn_ref[...] = kvn
        q = _dot(qn, wq_ref[...])
        kv = _dot(kvn, wkv_ref[...])
        kr = _rope_fwd(h[:, C_KR:C_GA], c, sa, sb).astype(BF16)
        for hd in range(HEADS):
            lo = hd * HEAD_PAD
            qcat_ref[:, lo:lo + NOPE] = q[:, lo:lo + NOPE].astype(BF16)
            qcat_ref[:, lo + NOPE:lo + HEAD_PAD] = _rope_fwd(q[:, lo + NOPE:lo + HEAD_PAD], c, sa, sb).astype(BF16)
            kcat_ref[:, lo:lo + NOPE] = kv[:, hd * NOPE:(hd + 1) * NOPE].astype(BF16)
            kcat_ref[:, lo + NOPE:lo + HEAD_PAD] = kr
        v_ref[...] = kv[:, MLA_W:].astype(BF16)

    def tile(w):
        return pl.BlockSpec((tm, w), lambda i: (i, 0))

    def whole(a):
        return pl.BlockSpec(a.shape, lambda i: (0, 0))

    outs = [(Q_LORA, F32), (KV_LORA, F32), (MLA_W, F32), (POOL_W, F32), (POOL_W, F32),
            (Q_LORA, BF16), (KV_LORA, BF16), (HEADS * HEAD_PAD, BF16), (HEADS * HEAD_PAD, BF16), (MLA_W, BF16)]
    return pl.pallas_call(
        body, name="fwd_in", grid=(tt // tm,),
        in_specs=[tile(D_MODEL), whole(w1), whole(gq), whole(gkv), whole(wq), whole(wkv), tile(128), tile(128), tile(128)],
        out_specs=[tile(w) for w, _ in outs],
        out_shape=[jax.ShapeDtypeStruct((tt, w), dt) for w, dt in outs],
        compiler_params=_params(48, ("arbitrary",)))(x2, w1, gq, gkv, wq, wkv, rc, rsa, rsb)


def _attn_fwd(qcat, kcat, v, pos_col, pos_blk, qmax, kmin, tq, tk):
    bb, ss, _ = qcat.shape
    nq, nk = ss // tq, ss // tk

    def body(qmax_ref, kmin_ref, q_ref, k_ref, v_ref, pc_ref, pb_ref, o_ref, lse_ref, m_s, l_s, acc_s):
        b, qi = pl.program_id(0), pl.program_id(2)
        m_s[...] = jnp.full(m_s.shape, NEG, F32)
        l_s[...] = jnp.zeros(l_s.shape, F32)
        acc_s[...] = jnp.zeros(acc_s.shape, F32)
        q = q_ref[...]
        pq = pc_ref[...]
        reach = qmax_ref[b * nq + qi]

        def step(ki, carry):
            @pl.when(reach >= kmin_ref[b * nk + ki])
            def _():
                rows = pl.ds(pl.multiple_of(ki * tk, tk), tk)
                s = _dot_nt(q, k_ref[rows, :]) * SCALE
                s = jnp.where(pq >= pb_ref[ki], s, NEG)
                m_prev = m_s[...]
                m_new = jnp.maximum(m_prev, jnp.max(s, axis=-1, keepdims=True))
                p = jnp.exp(s - m_new)
                a = jnp.exp(m_prev - m_new)
                l_s[...] = a * l_s[...] + jnp.sum(p, axis=-1, keepdims=True)
                acc_s[...] = a * acc_s[...] + _dot(p.astype(BF16), v_ref[rows, :])
                m_s[...] = m_new
            return carry

        lax.fori_loop(0, nk, step, 0)
        l = l_s[...]
        o_ref[...] = acc_s[...] / l
        lse_ref[...] = m_s[...] + jnp.log(l)

    grid_spec = pltpu.PrefetchScalarGridSpec(
        num_scalar_prefetch=2, grid=(bb, HEADS, nq),
        in_specs=[
            pl.BlockSpec((None, tq, HEAD_PAD), lambda b, h, i, *_: (b, i, h)),
            pl.BlockSpec((None, ss, HEAD_PAD), lambda b, h, i, *_: (b, 0, h)),
            pl.BlockSpec((None, ss, V_DIM), lambda b, h, i, *_: (b, 0, h)),
            pl.BlockSpec((None, tq, 1), lambda b, h, i, *_: (b, i, 0)),
            pl.BlockSpec((None, nk, 1, tk), lambda b, h, i, *_: (b, 0, 0, 0)),
        ],
        out_specs=[
            pl.BlockSpec((None, tq, V_DIM), lambda b, h, i, *_: (b, i, h)),
            pl.BlockSpec((None, None, tq, 1), lambda b, h, i, *_: (b, h, i, 0)),
        ],
        scratch_shapes=[pltpu.VMEM((tq, 1), F32), pltpu.VMEM((tq, 1), F32), pltpu.VMEM((tq, V_DIM), F32)])
    return pl.pallas_call(
        body, name="attn_fwd", grid_spec=grid_spec,
        out_shape=[jax.ShapeDtypeStruct((bb, ss, MLA_W), F32), jax.ShapeDtypeStruct((bb, HEADS, ss, 1), F32)],
        compiler_params=_params(32, ("arbitrary", "arbitrary", "arbitrary")))(qmax, kmin, qcat, kcat, v, pos_col, pos_blk)


def _post(x, tgt, o, ga, u, gb, w_out, pool_wb, pool_scale, ln_g, ln_b, ts):
    bb, ss, _ = x.shape
    nt = ss // ts
    hb = ts // POOL_HALO

    def body(x_ref, tgt_ref, o_ref, ga_ref, u_ref, uh_ref, gb_ref, wo_ref, pw_ref, ps_ref, lg_ref, lb_ref,
             dz_ref, ycat_ref, do_ref, dga_ref, dgb_ref, dpc_ref, loss_ref, glg_ref, glb_ref, gps_ref, gpw_ref):
        step = pl.program_id(0)
        j = step % nt

        @pl.when(step == 0)
        def _():
            for r in (loss_ref, glg_ref, glb_ref, gps_ref, gpw_ref):
                r[...] = jnp.zeros(r.shape, F32)

        o, ga, u, gb = o_ref[...], ga_ref[...], u_ref[...], gb_ref[...]
        sa, dsa = _silu_parts(ga)
        sb, dsb = _silu_parts(gb)
        ext = jnp.concatenate([jnp.where(j > 0, uh_ref[...], 0.0), u], axis=0)
        s2 = ext + pltpu.roll(ext, 1, 0)
        s4 = s2 + pltpu.roll(s2, 2, 0)
        s8 = s4 + pltpu.roll(s4, 4, 0)
        s16 = s8 + pltpu.roll(s8, 8, 0)
        cnt = _pool_cnt(j * ts, ts)
        pooled = (_pick_groups(s2, s4, s8, s16, POOL_HALO, POOL_HALO + ts) / cnt - u).astype(BF16)
        mixed = jnp.concatenate(
            [_dot(pooled[:, g * POOL_C:(g + 1) * POOL_C], pw_ref[g]) for g in range(POOL_G)], axis=1)
        ps = ps_ref[...]
        scaled = mixed * ps
        ycat = jnp.concatenate([o * sa, scaled * sb], axis=1).astype(BF16)
        ycat_ref[...] = ycat
        z = ALPHA * x_ref[...] + _dot(ycat, wo_ref[...])
        mu = jnp.mean(z, axis=-1, keepdims=True)
        zc = z - mu
        rstd = lax.rsqrt(jnp.mean(zc * zc, axis=-1, keepdims=True) + LN_EPS)
        xhat = zc * rstd
        lg = lg_ref[...]
        diff = xhat * lg + lb_ref[...] - tgt_ref[...]
        loss_ref[...] += jnp.sum(diff * diff) * (0.5 / D_MODEL)
        dy = diff * (1.0 / D_MODEL)
        glb_ref[...] += jnp.sum(dy, axis=0, keepdims=True)
        glg_ref[...] += jnp.sum(dy * xhat, axis=0, keepdims=True)
        dxh = dy * lg
        dz = rstd * (dxh - jnp.mean(dxh, axis=-1, keepdims=True) - xhat * jnp.mean(dxh * xhat, axis=-1, keepdims=True))
        dz_ref[...] = dz
        dycat = _dot_nt(dz.astype(BF16), wo_ref[...])
        dya, dyb = dycat[:, :MLA_W], dycat[:, MLA_W:]
        do_ref[...] = (dya * sa).astype(BF16)
        dga_ref[...] = (dya * o * dsa).astype(BF16)
        dgb_ref[...] = (dyb * scaled * dsb).astype(BF16)
        dscaled = dyb * sb
        gps_ref[...] += jnp.sum(dscaled * mixed, axis=0, keepdims=True)
        dmixed = (dscaled * ps).astype(BF16)
        dpooled = []
        for g in range(POOL_G):
            cols = slice(g * POOL_C, (g + 1) * POOL_C)
            gpw_ref[g] += _dot_tn(pooled[:, cols], dmixed[:, cols])
            dpooled.append(_dot_nt(dmixed[:, cols], pw_ref[g]))
        dpc_ref[...] = jnp.concatenate(dpooled, axis=1) / cnt

    def tile(w):
        return pl.BlockSpec((None, ts, w), lambda i: (i // nt, i % nt, 0))

    def whole(a):
        nd = a.ndim
        return pl.BlockSpec(a.shape, lambda i: (0,) * nd)

    halo = pl.BlockSpec((None, POOL_HALO, POOL_W), lambda i: (i // nt, jnp.maximum((i % nt) * hb - 1, 0), 0))
    acc_shapes = [(8, 128), (1, D_MODEL), (1, D_MODEL), (1, POOL_W), (POOL_G, POOL_C, POOL_C)]
    tile_outs = [(D_MODEL, F32), (D_MODEL, BF16), (MLA_W, BF16), (MLA_W, BF16), (POOL_W, BF16), (POOL_W, F32)]
    return pl.pallas_call(
        body, name="post", grid=(bb * nt,),
        in_specs=[tile(D_MODEL), tile(D_MODEL), tile(MLA_W), tile(MLA_W), tile(POOL_W), halo, tile(POOL_W),
                  whole(w_out), whole(pool_wb), whole(pool_scale), whole(ln_g), whole(ln_b)],
        out_specs=[tile(w) for w, _ in tile_outs]
        + [pl.BlockSpec(s, lambda i, n=len(s): (0,) * n) for s in acc_shapes],
        out_shape=[jax.ShapeDtypeStruct((bb, ss, w), dt) for w, dt in tile_outs]
        + [jax.ShapeDtypeStruct(s, F32) for s in acc_shapes],
        compiler_params=_params(48, ("arbitrary",)))(x, tgt, o, ga, u, u, gb, w_out, pool_wb, pool_scale, ln_g, ln_b)


def _attn_bwd(qcat, kcat, v, do, o, lse, pos_col, pos_blk, qmax, kmin, tq, tk):
    bb, ss, _ = qcat.shape
    nq, nk = ss // tq, ss // tk

    def body(qmax_ref, kmin_ref, q_ref, k_ref, v_ref, do_ref, o_ref, lse_ref, pc_ref, pb_ref,
             dq_ref, dk_ref, dv_ref, delta_s, dk_s, dv_s):
        b = pl.program_id(0)
        delta_s[...] = jnp.sum(do_ref[...].astype(F32) * o_ref[...], axis=-1, keepdims=True)
        dq_ref[...] = jnp.zeros(dq_ref.shape, F32)

        def kv_step(ki, carry):
            krows = pl.ds(pl.multiple_of(ki * tk, tk), tk)
            k = k_ref[krows, :]
            vv = v_ref[krows, :]
            pk = pb_ref[ki]
            first = kmin_ref[b * nk + ki]
            dk_s[...] = jnp.zeros(dk_s.shape, F32)
            dv_s[...] = jnp.zeros(dv_s.shape, F32)

            def q_step(qi, c2):
                @pl.when(qmax_ref[b * nq + qi] >= first)
                def _():
                    qrows = pl.ds(pl.multiple_of(qi * tq, tq), tq)
                    q = q_ref[qrows, :]
                    dd = do_ref[qrows, :]
                    s = _dot_nt(q, k) * SCALE
                    s = jnp.where(pc_ref[qrows, :] >= pk, s, NEG)
                    p = jnp.exp(s - lse_ref[qrows, :])
                    dv_s[...] += _dot_tn(p.astype(BF16), dd)
                    dp = _dot_nt(dd, vv)
                    ds = (p * (dp - delta_s[qrows, :]) * SCALE).astype(BF16)
                    dk_s[...] += _dot_tn(ds, q)
                    dq_ref[qrows, :] += _dot(ds, k)
                return c2

            lax.fori_loop(0, nq, q_step, 0)
            dk_ref[krows, :] = dk_s[...]
            dv_ref[krows, :] = dv_s[...].astype(BF16)
            return carry

        lax.fori_loop(0, nk, kv_step, 0)

    def per_head(w):
        return pl.BlockSpec((None, ss, w), lambda b, h, *_: (b, 0, h))

    grid_spec = pltpu.PrefetchScalarGridSpec(
        num_scalar_prefetch=2, grid=(bb, HEADS),
        in_specs=[per_head(HEAD_PAD), per_head(HEAD_PAD), per_head(V_DIM), per_head(V_DIM), per_head(V_DIM),
                  pl.BlockSpec((None, None, ss, 1), lambda b, h, *_: (b, h, 0, 0)),
                  pl.BlockSpec((None, ss, 1), lambda b, h, *_: (b, 0, 0)),
                  pl.BlockSpec((None, nk, 1, tk), lambda b, h, *_: (b, 0, 0, 0))],
        out_specs=[per_head(HEAD_PAD), per_head(HEAD_PAD), per_head(V_DIM)],
        scratch_shapes=[pltpu.VMEM((ss, 1), F32), pltpu.VMEM((tk, HEAD_PAD), F32), pltpu.VMEM((tk, V_DIM), F32)])
    return pl.pallas_call(
        body, name="attn_bwd", grid_spec=grid_spec,
        out_shape=[jax.ShapeDtypeStruct((bb, ss, HEADS * HEAD_PAD), F32),
                   jax.ShapeDtypeStruct((bb, ss, HEADS * HEAD_PAD), F32),
                   jax.ShapeDtypeStruct((bb, ss, MLA_W), BF16)],
        compiler_params=_params(56, ("arbitrary", "arbitrary")))(qmax, kmin, qcat, kcat, v, do, o, lse, pos_col, pos_blk)


def _bwd_mid(dqcat, dkcat, dv, xq, xkv, rc, rsa, rsb, wq, wkv, gq, gkv, dga, dgb, dpc, dz, w1, ts):
    bb, ss, _ = dz.shape
    nt = ss // ts
    hb = ts // POOL_HALO

    def body(dq_ref, dk_ref, dv_ref, xq_ref, xkv_ref, c_ref, sa_ref, sb_ref, wq_ref, wkv_ref, gq_ref, gkv_ref,
             dga_ref, dgb_ref, dpc_ref, dph_ref, dz_ref, w1_ref,
             gx_ref, dh_ref, dqp_ref, dkv_ref, ggq_ref, ggkv_ref):
        step = pl.program_id(0)
        j = step % nt

        @pl.when(step == 0)
        def _():
            ggq_ref[...] = jnp.zeros(ggq_ref.shape, F32)
            ggkv_ref[...] = jnp.zeros(ggkv_ref.shape, F32)

        c, sa, sb = c_ref[...], sa_ref[...], sb_ref[...]
        dq, dk = dq_ref[...], dk_ref[...]
        dkr = jnp.zeros((ts, HEAD_PAD - NOPE), F32)
        for hd in range(HEADS):
            lo = hd * HEAD_PAD
            dqp_ref[:, lo:lo + NOPE] = dq[:, lo:lo + NOPE].astype(BF16)
            dqp_ref[:, lo + NOPE:lo + HEAD_PAD] = _rope_bwd(dq[:, lo + NOPE:lo + HEAD_PAD], c, sa, sb).astype(BF16)
            dkv_ref[:, hd * NOPE:(hd + 1) * NOPE] = dk[:, lo:lo + NOPE].astype(BF16)
            dkr = dkr + dk[:, lo + NOPE:lo + HEAD_PAD]
        dkv_ref[:, MLA_W:] = dv_ref[...]
        dh_ref[:, C_KR:C_GA] = _rope_bwd(dkr, c, sa, sb).astype(BF16)

        def rms_bwd(x, g, dn):
            inv = lax.rsqrt(jnp.mean(x * x, axis=-1, keepdims=True) + RMS_EPS)
            xh = x * inv
            dxh = dn * g
            return inv * (dxh - xh * jnp.mean(dxh * xh, axis=-1, keepdims=True)), jnp.sum(dn * xh, axis=0, keepdims=True)

        dxq, ggq = rms_bwd(xq_ref[...], gq_ref[...], _dot_nt(dqp_ref[...], wq_ref[...]))
        dxkv, ggkv = rms_bwd(xkv_ref[...], gkv_ref[...], _dot_nt(dkv_ref[...], wkv_ref[...]))
        ggq_ref[...] += ggq
        ggkv_ref[...] += ggkv
        dh_ref[:, C_XQ:C_XKV] = dxq.astype(BF16)
        dh_ref[:, C_XKV:C_KR] = dxkv.astype(BF16)
        dh_ref[:, C_GA:C_U] = dga_ref[...]
        dh_ref[:, C_GB:IN_WP] = dgb_ref[...]
        dpc = dpc_ref[...]
        n = ts + POOL_HALO
        ext = jnp.concatenate([dpc, jnp.where(j < nt - 1, dph_ref[...], 0.0)], axis=0)
        r2 = ext + pltpu.roll(ext, n - 1, 0)
        r4 = r2 + pltpu.roll(r2, n - 2, 0)
        r8 = r4 + pltpu.roll(r4, n - 4, 0)
        r16 = r8 + pltpu.roll(r8, n - 8, 0)
        du = _pick_groups(r2, r4, r8, r16, 0, ts) - dpc * _pool_cnt(j * ts, ts)
        dh_ref[:, C_U:C_GB] = du.astype(BF16)
        gx_ref[...] = ALPHA * dz_ref[...] + _dot_nt(dh_ref[...], w1_ref[...])

    def tile(w):
        return pl.BlockSpec((None, ts, w), lambda i: (i // nt, i % nt, 0))

    def whole(a):
        return pl.BlockSpec(a.shape, lambda i: (0, 0))

    halo = pl.BlockSpec((None, POOL_HALO, POOL_W),
                        lambda i: (i // nt, jnp.minimum((i % nt + 1) * hb, ss // POOL_HALO - 1), 0))
    return pl.pallas_call(
        body, name="bwd_mid", grid=(bb * nt,),
        in_specs=[tile(HEADS * HEAD_PAD), tile(HEADS * HEAD_PAD), tile(MLA_W), tile(Q_LORA), tile(KV_LORA),
                  tile(128), tile(128), tile(128), whole(wq), whole(wkv), whole(gq), whole(gkv),
                  tile(MLA_W), tile(POOL_W), tile(POOL_W), halo, tile(D_MODEL), whole(w1)],
        out_specs=[tile(D_MODEL), tile(IN_WP), tile(HEADS * HEAD_PAD), tile(2 * MLA_W),
                   pl.BlockSpec((1, Q_LORA), lambda i: (0, 0)), pl.BlockSpec((1, KV_LORA), lambda i: (0, 0))],
        out_shape=[jax.ShapeDtypeStruct((bb, ss, D_MODEL), F32), jax.ShapeDtypeStruct((bb, ss, IN_WP), BF16),
                   jax.ShapeDtypeStruct((bb, ss, HEADS * HEAD_PAD), BF16), jax.ShapeDtypeStruct((bb, ss, 2 * MLA_W), BF16),
                   jax.ShapeDtypeStruct((1, Q_LORA), F32), jax.ShapeDtypeStruct((1, KV_LORA), F32)],
        compiler_params=_params(48, ("arbitrary",)))(
            dqcat, dkcat, dv, xq, xkv, rc, rsa, rsb, wq, wkv, gq, gkv, dga, dgb, dpc, dpc, dz, w1)


def _grad_w(a, b, bt, name):
    tt, m = a.shape
    n = b.shape[1]

    def body(a_ref, b_ref, out_ref):
        @pl.when(pl.program_id(0) == 0)
        def _():
            out_ref[...] = jnp.zeros(out_ref.shape, F32)

        out_ref[...] += _dot_tn(a_ref[...].astype(BF16), b_ref[...].astype(BF16))

    return pl.pallas_call(
        body, name=name, grid=(tt // bt,),
        in_specs=[pl.BlockSpec((bt, m), lambda i: (i, 0)), pl.BlockSpec((bt, n), lambda i: (i, 0))],
        out_specs=pl.BlockSpec((m, n), lambda i: (0, 0)),
        out_shape=jax.ShapeDtypeStruct((m, n), F32),
        compiler_params=_params(48, ("arbitrary",)))(a, b)


def _adamw(triples):
    n = len(triples)

    def body(*refs):
        ins, outs = refs[:4 * n], refs[4 * n:]
        for i in range(n):
            w, g, m, v = (r[...] for r in ins[4 * i:4 * i + 4])
            m = ADAM_B1 * m + (1.0 - ADAM_B1) * g
            v = ADAM_B2 * v + (1.0 - ADAM_B2) * jnp.square(g)
            m_hat = m / (1.0 - ADAM_B1 ** ADAM_STEP)
            v_hat = v / (1.0 - ADAM_B2 ** ADAM_STEP)
            outs[3 * i][...] = -ADAM_LR * (m_hat / (jnp.sqrt(v_hat) + ADAM_EPS) + ADAM_WD * w)
            outs[3 * i + 1][...] = m
            outs[3 * i + 2][...] = v

    flat = [a for t in triples for a in t]
    vmem = pl.BlockSpec(memory_space=pltpu.VMEM)
    res = pl.pallas_call(
        body, name="adamw", in_specs=[vmem] * len(flat), out_specs=[vmem] * (3 * n),
        out_shape=[jax.ShapeDtypeStruct(t[0].shape, F32) for t in triples for _ in range(3)],
        compiler_params=_params(48))(*flat)
    return [tuple(res[3 * i:3 * i + 3]) for i in range(n)]


def _shard_slab(w_in, w_uq, w_ukv, w_out):
    parts = [w_in.reshape(R_IN, 1024), w_uq.reshape(R_UQ, 1024), w_ukv.reshape(R_UKV, 1024), w_out,
             jnp.zeros((SLAB_ROWS - O_OUT - R_OUT, 1024), F32)]
    return jnp.concatenate(parts, axis=0)


def _unpack_weights(slabs):
    def cols(lo, rows, k, per):
        return slabs[:, lo:lo + rows].reshape(N_DEV, k, per).transpose(1, 0, 2).reshape(k, N_DEV * per)

    w_in = cols(0, R_IN, D_MODEL, IN_W // N_DEV)
    w1 = jnp.concatenate([w_in[:, :768 + ROPE], jnp.zeros((D_MODEL, 128 - ROPE), BF16), w_in[:, 768 + ROPE:]], axis=1)
    w_uq = cols(O_UQ, R_UQ, Q_LORA, HEADS * QK_DIM // N_DEV).reshape(Q_LORA, HEADS, QK_DIM)
    wq = jnp.pad(w_uq, ((0, 0), (0, 0), (0, HEAD_PAD - QK_DIM))).reshape(Q_LORA, HEADS * HEAD_PAD)
    w_ukv = cols(O_UKV, R_UKV, KV_LORA, 1024 // N_DEV)
    wkv = w_ukv.reshape(KV_LORA, HEADS, 2, NOPE).transpose(0, 2, 1, 3).reshape(KV_LORA, 2 * MLA_W)
    w_out = slabs[:, O_OUT:O_OUT + R_OUT].reshape(D_MODEL, D_MODEL)
    return w1, wq, wkv, w_out


def _grad_slab(g_w1, g_wq, g_wkv, g_wout):
    def rows(g, per):
        k = g.shape[0]
        return g.reshape(k, N_DEV, per).transpose(1, 0, 2).reshape(N_DEV, k * per // 1024, 1024)

    g_in = jnp.concatenate([g_w1[:, :768 + ROPE], g_w1[:, C_GA:]], axis=1)
    g_uq = g_wq.reshape(Q_LORA, HEADS, HEAD_PAD)[:, :, :QK_DIM].reshape(Q_LORA, HEADS * QK_DIM)
    g_ukv = g_wkv.reshape(KV_LORA, 2, HEADS, NOPE).transpose(0, 2, 1, 3).reshape(KV_LORA, 1024)
    parts = [rows(g_in, IN_W // N_DEV), rows(g_uq, HEADS * QK_DIM // N_DEV), rows(g_ukv, 1024 // N_DEV),
             g_wout.reshape(N_DEV, R_OUT, 1024), jnp.zeros((N_DEV, SLAB_ROWS - O_OUT - R_OUT, 1024), F32)]
    return jnp.concatenate(parts, axis=1)


def _rope_tables(positions):
    half = ROPE // 2
    inv_freq = ROPE_THETA ** (-jnp.arange(half, dtype=F32) / half)
    ang = positions.astype(F32)[..., None] * inv_freq
    cos, sin = jnp.cos(ang), jnp.sin(ang)
    zero = jnp.zeros_like(cos)
    rc = jnp.concatenate([cos, cos, zero, zero], axis=-1)
    rsa = jnp.concatenate([-sin, zero, zero, zero], axis=-1)
    rsb = jnp.concatenate([zero, sin, zero, zero], axis=-1)
    return rc, rsa, rsb


def _pad_rows(a, rows):
    return jnp.pad(a, ((0, rows - a.shape[0]), (0, 0)))


def kernel(x, positions, w_in, q_norm_g, w_uq, kv_norm_g, w_ukv, pool_w, pool_scale, w_out, ln_g, ln_b, loss_target, m_w_in, m_q_norm_g, m_w_uq, m_kv_norm_g, m_w_ukv, m_pool_w, m_pool_scale, m_w_out, m_ln_g, m_ln_b, v_w_in, v_q_norm_g, v_w_uq, v_kv_norm_g, v_w_ukv, v_pool_w, v_pool_scale, v_w_out, v_ln_g, v_ln_b):
    bb, ss, _ = x.shape
    tt = bb * ss
    tile = min(256, ss)
    nblk = ss // tile

    slab = _shard_slab(w_in, w_uq, w_ukv, w_out).astype(BF16)
    slabs = _all_gather(slab, "gather_weights", reduce=False).reshape(N_DEV, SLAB_ROWS, 1024)
    w1, wq, wkv, wo = _unpack_weights(slabs)

    gq, gkv = q_norm_g.reshape(1, Q_LORA), kv_norm_g.reshape(1, KV_LORA)
    ps = pool_scale.reshape(1, POOL_W)
    rc, rsa, rsb = (t.reshape(tt, 128) for t in _rope_tables(positions))
    pos_col = positions.reshape(bb, ss, 1)
    pos_blk = positions.reshape(bb, nblk, 1, tile)
    qmax = jnp.max(positions.reshape(bb, nblk, tile), axis=-1).reshape(-1)
    kmin = jnp.min(positions.reshape(bb, nblk, tile), axis=-1).reshape(-1)

    x2 = x.reshape(tt, D_MODEL)
    xq, xkv, ga, u, gb, qn, kvn, qcat, kcat, v = _fwd_in(x2, w1, gq, gkv, wq, wkv, rc, rsa, rsb, tile)
    as3 = lambda a: a.reshape(bb, ss, a.shape[-1])
    qcat, kcat, v = as3(qcat), as3(kcat), as3(v)
    o, lse = _attn_fwd(qcat, kcat, v, pos_col, pos_blk, qmax, kmin, tile, tile)
    dz, ycat, do, dga, dgb, dpc, loss_p, g_lng, g_lnb, g_ps, g_pw = _post(
        x, loss_target, o, as3(ga), as3(u), as3(gb), wo, pool_w.astype(BF16), ps, ln_g, ln_b, tile)

    dqcat, dkcat, dv = _attn_bwd(qcat, kcat, v, do, o, lse, pos_col, pos_blk, qmax, kmin, tile, tile)
    grad_x, dh, dqp, dkv, g_gq, g_gkv = _bwd_mid(
        dqcat, dkcat, dv, as3(xq), as3(xkv), as3(rc), as3(rsa), as3(rsb), wq, wkv, gq, gkv, dga, dgb, dpc, dz, w1, tile)
    bt = min(512, tt)
    g_w1 = _grad_w(x2, dh.reshape(tt, IN_WP), bt, "grad_w_in")
    g_wq = _grad_w(qn, dqp.reshape(tt, HEADS * HEAD_PAD), bt, "grad_w_uq")
    g_wkv = _grad_w(kvn, dkv.reshape(tt, 2 * MLA_W), bt, "grad_w_ukv")
    g_wo = _grad_w(ycat.reshape(tt, D_MODEL), dz.reshape(tt, D_MODEL), bt, "grad_w_out")

    g_big = _reduce_scatter(_grad_slab(g_w1, g_wq, g_wkv, g_wo), "reduce_scatter_grads")
    small = jnp.concatenate(
        [_pad_rows(g_lng.reshape(8, 128), 8), _pad_rows(g_lnb.reshape(8, 128), 8), _pad_rows(g_gq.reshape(4, 128), 8),
         _pad_rows(g_gkv.reshape(2, 128), 8), _pad_rows(g_ps.reshape(4, 128), 8), g_pw.reshape(POOL_G * POOL_C, 128),
         loss_p], axis=0)
    small = _all_gather(small, "all_reduce_small", reduce=True)
    loss = small[40 + POOL_G * POOL_C, 0]
    grads = {
        "w_in": g_big[:R_IN].reshape(D_MODEL, IN_W // N_DEV),
        "q_norm_g": small[16:20].reshape(1, Q_LORA),
        "w_uq": g_big[O_UQ:O_UKV].reshape(Q_LORA, HEADS * QK_DIM // N_DEV),
        "kv_norm_g": small[24:26].reshape(1, KV_LORA),
        "w_ukv": g_big[O_UKV:O_OUT].reshape(KV_LORA, 1024 // N_DEV),
        "pool_w": small[40:40 + POOL_G * POOL_C],
        "pool_scale": small[32:36].reshape(1, POOL_W),
        "w_out": g_big[O_OUT:O_OUT + R_OUT],
        "ln_g": small[0:8].reshape(1, D_MODEL),
        "ln_b": small[8:16].reshape(1, D_MODEL),
    }

    names = ["w_in", "q_norm_g", "w_uq", "kv_norm_g", "w_ukv", "pool_w", "pool_scale", "w_out", "ln_g", "ln_b"]
    weights = dict(w_in=w_in, q_norm_g=q_norm_g, w_uq=w_uq, kv_norm_g=kv_norm_g, w_ukv=w_ukv, pool_w=pool_w,
                   pool_scale=pool_scale, w_out=w_out, ln_g=ln_g, ln_b=ln_b)
    moms = dict(w_in=(m_w_in, v_w_in), q_norm_g=(m_q_norm_g, v_q_norm_g), w_uq=(m_w_uq, v_w_uq),
                kv_norm_g=(m_kv_norm_g, v_kv_norm_g), w_ukv=(m_w_ukv, v_w_ukv), pool_w=(m_pool_w, v_pool_w),
                pool_scale=(m_pool_scale, v_pool_scale), w_out=(m_w_out, v_w_out), ln_g=(m_ln_g, v_ln_g),
                ln_b=(m_ln_b, v_ln_b))
    as2 = lambda a, n: a.reshape(grads[n].shape)
    upd = _adamw([(as2(weights[n], n), grads[n], as2(moms[n][0], n), as2(moms[n][1], n)) for n in names])
    shaped = lambda a, n: a.reshape(weights[n].shape)
    return (loss, grad_x,
            *[shaped(grads[n], n) for n in names],
            *[shaped(upd[i][0], n) for i, n in enumerate(names)],
            *[shaped(upd[i][1], n) for i, n in enumerate(names)],
            *[shaped(upd[i][2], n) for i, n in enumerate(names)])
```

```python
import functools

import jax
import jax.numpy as jnp
from jax import lax
from jax.experimental import pallas as pl
from jax.experimental.pallas import tpu as pltpu

F32 = jnp.float32
BF16 = jnp.bfloat16
MESH = pl.DeviceIdType.MESH

N_DEV = 8
D_MODEL = 1024
HEADS = 4
NOPE = 128
ROPE = 64
V_DIM = 128
QK_DIM = NOPE + ROPE
HEAD_PAD = 256
Q_LORA = 512
KV_LORA = 256
MLA_W = HEADS * V_DIM
POOL_W = 512
POOL_G = 4
POOL_C = 128
POOL_HALO = 16
IN_W = 2368
IN_WP = 2432
C_XQ, C_XKV, C_KR, C_GA, C_U, C_GB = 0, 512, 768, 896, 1408, 1920
ROPE_THETA = 10000.0
RMS_EPS = 1e-6
LN_EPS = 1e-5
ALPHA = 2.0 ** 0.25
SCALE = QK_DIM ** -0.5
NEG = float(jnp.finfo(jnp.float32).min)

ADAM_LR = 0.001
ADAM_B1 = 0.9
ADAM_B2 = 0.999
ADAM_EPS = 1e-08
ADAM_WD = 0.01
ADAM_STEP = 10

R_IN, R_UQ, R_UKV, R_OUT = 296, 48, 32, 128
SLAB_ROWS = 512
O_UQ, O_UKV, O_OUT = R_IN, R_IN + R_UQ, R_IN + R_UQ + R_UKV

V7X_VMEM_BYTES = 64 * 1024 * 1024


def _params(vmem_mb, semantics=None):
    assert vmem_mb * 1024 * 1024 < V7X_VMEM_BYTES
    return pltpu.CompilerParams(vmem_limit_bytes=vmem_mb * 1024 * 1024, dimension_semantics=semantics)


def _dot(a, b):
    return jnp.dot(a, b, preferred_element_type=F32)


def _dot_nt(a, b):
    return lax.dot_general(a, b, (((1,), (1,)), ((), ())), preferred_element_type=F32)


def _dot_tn(a, b):
    return lax.dot_general(a, b, (((0,), (0,)), ((), ())), preferred_element_type=F32)


def _rope_fwd(t, c, sa, sb):
    return t * c + pltpu.roll(t, 96, 1) * sa + pltpu.roll(t, 32, 1) * sb


def _rope_bwd(d, c, sa, sb):
    return d * c + pltpu.roll(d * sa, 32, 1) + pltpu.roll(d * sb, 96, 1)


def _silu_parts(g):
    sig = jax.nn.sigmoid(g)
    return g * sig, sig * (1.0 + g * (1.0 - sig))


def _pool_cnt(row0, rows):
    t = row0 + lax.broadcasted_iota(jnp.int32, (rows, POOL_W), 0)
    w = 2 << (lax.broadcasted_iota(jnp.int32, (rows, POOL_W), 1) // POOL_C)
    return jnp.minimum(t + 1, w).astype(F32)


def _pick_groups(s2, s4, s8, s16, lo, hi):
    return jnp.concatenate([s2[lo:hi, 0:128], s4[lo:hi, 128:256], s8[lo:hi, 256:384], s16[lo:hi, 384:512]], axis=1)


def _all_gather(shard, name, reduce):
    m_per, n = shard.shape

    def body(x_ref, *rest):
        if reduce:
            sum_ref, out_ref, send_sems, recv_sems, local_sem = rest
        else:
            out_ref, send_sems, recv_sems, local_sem = rest
        x, y, c = lax.axis_index("x"), lax.axis_index("y"), lax.axis_index("c")
        me, sibling = (x, y, c), (x, y, 1 - c)
        chips = [(1 - x, y), (x, 1 - y), (1 - x, 1 - y)]

        def rows(px, py, pc):
            return out_ref.at[pl.ds((4 * px + 2 * py + pc) * m_per, m_per), :]

        def copy(k, block, to, src=None):
            return pltpu.make_async_remote_copy(
                src_ref=rows(*block) if src is None else src, dst_ref=rows(*block),
                send_sem=send_sems.at[k], recv_sem=recv_sems.at[k], device_id=to, device_id_type=MESH)

        mine = pltpu.make_async_copy(x_ref, rows(*me), local_sem)
        mine.start()
        first = [copy(0, me, sibling, src=x_ref)]
        first += [copy(1 + j, me, (*chip, c), src=x_ref) for j, chip in enumerate(chips)]
        for cp in first:
            cp.start()
        passed = [copy(4 + j, (*chip, c), sibling) for j, chip in enumerate(chips)]
        for j, chip in enumerate(chips):
            copy(1 + j, (*chip, c), me).wait_recv()
            passed[j].start()
        copy(0, sibling, me).wait_recv()
        for j, chip in enumerate(chips):
            copy(4 + j, (*chip, 1 - c), me).wait_recv()
        for cp in first + passed:
            cp.wait_send()
        mine.wait()
        if reduce:
            acc = out_ref[pl.ds(0, m_per), :]
            for d in range(1, N_DEV):
                acc = acc + out_ref[pl.ds(d * m_per, m_per), :]
            sum_ref[...] = acc

    gathered = jax.ShapeDtypeStruct((N_DEV * m_per, n), shard.dtype)
    vmem = pl.BlockSpec(memory_space=pltpu.VMEM)
    sems = [pltpu.SemaphoreType.DMA((7,)), pltpu.SemaphoreType.DMA((7,)), pltpu.SemaphoreType.DMA]
    if reduce:
        return pl.pallas_call(
            body, name=name, out_shape=jax.ShapeDtypeStruct((m_per, n), shard.dtype),
            in_specs=[vmem], out_specs=vmem,
            scratch_shapes=[pltpu.VMEM(gathered.shape, gathered.dtype)] + sems,
            compiler_params=_params(32))(shard)
    return pl.pallas_call(
        body, name=name, out_shape=gathered, in_specs=[vmem], out_specs=vmem, scratch_shapes=sems,
        compiler_params=_params(32))(shard)


def _reduce_scatter(gslab, name):
    _, rr, ww = gslab.shape
    chunk = 64

    def body(g_hbm, out_ref, own_ref, recv1_ref, sendb_ref, recv2_ref, ld_sems, s1_send, s1_recv, s2_send, s2_recv):
        x, y, c = lax.axis_index("x"), lax.axis_index("y"), lax.axis_index("c")
        sibling = (x, y, 1 - c)
        chips = [(1 - x, y), (x, 1 - y), (1 - x, 1 - y)]
        loads = [pltpu.make_async_copy(g_hbm.at[2 * k + c], own_ref.at[k], ld_sems.at[k]) for k in range(4)]
        stage1 = [pltpu.make_async_remote_copy(
            src_ref=g_hbm.at[2 * k + (1 - c)], dst_ref=recv1_ref.at[k], send_sem=s1_send.at[k],
            recv_sem=s1_recv.at[k], device_id=sibling, device_id_type=MESH) for k in range(4)]
        for cp in loads + stage1:
            cp.start()
        for cp in loads:
            cp.wait()
        for cp in stage1:
            cp.wait_recv()

        stage2 = []
        for r, (cx, cy) in enumerate(chips):
            kk = 2 * cx + cy

            def pack(i, carry, r=r, kk=kk):
                rows = pl.ds(pl.multiple_of(i * chunk, chunk), chunk)
                sendb_ref[r, rows, :] = (own_ref[kk, rows, :] + recv1_ref[kk, rows, :]).astype(BF16)
                return carry

            lax.fori_loop(0, rr // chunk, pack, 0)
            cp = pltpu.make_async_remote_copy(
                src_ref=sendb_ref.at[r], dst_ref=recv2_ref.at[r], send_sem=s2_send.at[r],
                recv_sem=s2_recv.at[r], device_id=(cx, cy, c), device_id_type=MESH)
            cp.start()
            stage2.append(cp)
        for cp in stage2:
            cp.wait_recv()
        mine = 2 * x + y

        def total(i, carry):
            rows = pl.ds(pl.multiple_of(i * chunk, chunk), chunk)
            acc = own_ref[mine, rows, :] + recv1_ref[mine, rows, :]
            for r in range(3):
                acc = acc + recv2_ref[r, rows, :].astype(F32)
            out_ref[rows, :] = acc
            return carry

        lax.fori_loop(0, rr // chunk, total, 0)
        for cp in stage1 + stage2:
            cp.wait_send()

    return pl.pallas_call(
        body, name=name, out_shape=jax.ShapeDtypeStruct((rr, ww), F32),
        in_specs=[pl.BlockSpec(memory_space=pl.ANY)], out_specs=pl.BlockSpec(memory_space=pltpu.VMEM),
        scratch_shapes=[
            pltpu.VMEM((4, rr, ww), F32), pltpu.VMEM((4, rr, ww), F32),
            pltpu.VMEM((3, rr, ww), BF16), pltpu.VMEM((3, rr, ww), BF16),
            pltpu.SemaphoreType.DMA((4,)), pltpu.SemaphoreType.DMA((4,)), pltpu.SemaphoreType.DMA((4,)),
            pltpu.SemaphoreType.DMA((3,)), pltpu.SemaphoreType.DMA((3,))],
        compiler_params=_params(48))(gslab)


def _fwd_in(x2, w1, gq, gkv, wq, wkv, rc, rsa, rsb, tm):
    tt = x2.shape[0]

    def body(x_ref, w1_ref, gq_ref, gkv_ref, wq_ref, wkv_ref, c_ref, sa_ref, sb_ref,
             xq_ref, xkv_ref, ga_ref, u_ref, gb_ref, qn_ref, kvn_ref, qcat_ref, kcat_ref, v_ref):
        h = _dot(x_ref[...].astype(BF16), w1_ref[...])
        xq = h[:, C_XQ:C_XKV]
        xkv = h[:, C_XKV:C_KR]
        xq_ref[...] = xq
        xkv_ref[...] = xkv
        ga_ref[...] = h[:, C_GA:C_U]
        u_ref[...] = h[:, C_U:C_GB]
        gb_ref[...] = h[:, C_GB:IN_WP]
        c, sa, sb = c_ref[...], sa_ref[...], sb_ref[...]
        qn = (xq * lax.rsqrt(jnp.mean(xq * xq, axis=-1, keepdims=True) + RMS_EPS) * gq_ref[...]).astype(BF16)
        kvn = (xkv * lax.rsqrt(jnp.mean(xkv * xkv, axis=-1, keepdims=True) + RMS_EPS) * gkv_ref[...]).astype(BF16)
        qn_ref[...] = qn
        kvn_ref[...] = kvn
        q = _dot(qn, wq_ref[...])
        kv = _dot(kvn, wkv_ref[...])
        kr = _rope_fwd(h[:, C_KR:C_GA], c, sa, sb).astype(BF16)
        for hd in range(HEADS):
            lo = hd * HEAD_PAD
            qcat_ref[:, lo:lo + NOPE] = q[:, lo:lo + NOPE].astype(BF16)
            qcat_ref[:, lo + NOPE:lo + HEAD_PAD] = _rope_fwd(q[:, lo + NOPE:lo + HEAD_PAD], c, sa, sb).astype(BF16)
            kcat_ref[:, lo:lo + NOPE] = kv[:, hd * NOPE:(hd + 1) * NOPE].astype(BF16)
            kcat_ref[:, lo + NOPE:lo + HEAD_PAD] = kr
        v_ref[...] = kv[:, MLA_W:].astype(BF16)

    def tile(w):
        return pl.BlockSpec((tm, w), lambda i: (i, 0))

    def whole(a):
        return pl.BlockSpec(a.shape, lambda i: (0, 0))

    outs = [(Q_LORA, F32), (KV_LORA, F32), (MLA_W, F32), (POOL_W, F32), (POOL_W, F32),
            (Q_LORA, BF16), (KV_LORA, BF16), (HEADS * HEAD_PAD, BF16), (HEADS * HEAD_PAD, BF16), (MLA_W, BF16)]
    return pl.pallas_call(
        body, name="fwd_in", grid=(tt // tm,),
        in_specs=[tile(D_MODEL), whole(w1), whole(gq), whole(gkv), whole(wq), whole(wkv), tile(128), tile(128), tile(128)],
        out_specs=[tile(w) for w, _ in outs],
        out_shape=[jax.ShapeDtypeStruct((tt, w), dt) for w, dt in outs],
        compiler_params=_params(48, ("arbitrary",)))(x2, w1, gq, gkv, wq, wkv, rc, rsa, rsb)


def _attn_fwd(qcat, kcat, v, pos_col, pos_blk, qmax, kmin, tq, tk):
    bb, ss, _ = qcat.shape
    nq, nk = ss // tq, ss // tk
    lanes = 128

    def body(qmax_ref, kmin_ref, q_ref, k_ref, v_ref, pc_ref, pb_ref, o_ref, lse_ref, m_s, acc_s):
        b, qi = pl.program_id(0), pl.program_id(2)
        m_s[...] = jnp.full(m_s.shape, NEG, F32)
        acc_s[...] = jnp.zeros(acc_s.shape, F32)
        q = q_ref[...]
        pq = pc_ref[...]
        reach = qmax_ref[b * nq + qi]
        ones = jnp.ones((tk, lanes), BF16)

        def step(ki, carry):
            @pl.when(reach >= kmin_ref[b * nk + ki])
            def _():
                rows = pl.ds(pl.multiple_of(ki * tk, tk), tk)
                s = _dot_nt(q, k_ref[rows, :]) * SCALE
                s = jnp.where(pq >= pb_ref[ki], s, NEG)
                m_prev = m_s[...]
                m_new = jnp.maximum(m_prev, jnp.max(s, axis=-1, keepdims=True))
                p = jnp.exp(s - jnp.tile(m_new, (1, tk // lanes)))
                a = jnp.exp(m_prev - m_new)
                pv = _dot(p.astype(BF16), jnp.concatenate([v_ref[rows, :], ones], axis=1))
                acc_s[...] = jnp.tile(a, (1, 2)) * acc_s[...] + pv
                m_s[...] = m_new
            return carry

        lax.fori_loop(0, nk, step, 0)
        acc = acc_s[...]
        l = acc[:, V_DIM:]
        o_ref[...] = acc[:, :V_DIM] / l
        lse_ref[...] = (m_s[...] + jnp.log(l)).T[0:1, :]

    grid_spec = pltpu.PrefetchScalarGridSpec(
        num_scalar_prefetch=2, grid=(bb, HEADS, nq),
        in_specs=[
            pl.BlockSpec((None, tq, HEAD_PAD), lambda b, h, i, *_: (b, i, h)),
            pl.BlockSpec((None, ss, HEAD_PAD), lambda b, h, i, *_: (b, 0, h)),
            pl.BlockSpec((None, ss, V_DIM), lambda b, h, i, *_: (b, 0, h)),
            pl.BlockSpec((None, tq, 1), lambda b, h, i, *_: (b, i, 0)),
            pl.BlockSpec((None, nk, 1, tk), lambda b, h, i, *_: (b, 0, 0, 0)),
        ],
        out_specs=[
            pl.BlockSpec((None, tq, V_DIM), lambda b, h, i, *_: (b, i, h)),
            pl.BlockSpec((None, None, 1, tq), lambda b, h, i, *_: (b, h, 0, i)),
        ],
        scratch_shapes=[pltpu.VMEM((tq, lanes), F32), pltpu.VMEM((tq, 2 * V_DIM), F32)])
    return pl.pallas_call(
        body, name="attn_fwd", grid_spec=grid_spec,
        out_shape=[jax.ShapeDtypeStruct((bb, ss, MLA_W), F32), jax.ShapeDtypeStruct((bb, HEADS, 1, ss), F32)],
        compiler_params=_params(48, ("arbitrary", "arbitrary", "arbitrary")))(qmax, kmin, qcat, kcat, v, pos_col, pos_blk)


def _post(x, tgt, o, ga, u, gb, w_out, pool_wb, pool_scale, ln_g, ln_b, ts):
    bb, ss, _ = x.shape
    nt = ss // ts
    hb = ts // POOL_HALO

    def body(x_ref, tgt_ref, o_ref, ga_ref, u_ref, uh_ref, gb_ref, wo_ref, pw_ref, ps_ref, lg_ref, lb_ref,
             dz_ref, ycat_ref, do_ref, dga_ref, dgb_ref, dpc_ref, dl_ref, loss_ref, glg_ref, glb_ref, gps_ref, gpw_ref):
        step = pl.program_id(0)
        j = step % nt

        @pl.when(step == 0)
        def _():
            for r in (loss_ref, glg_ref, glb_ref, gps_ref, gpw_ref):
                r[...] = jnp.zeros(r.shape, F32)

        o, ga, u, gb = o_ref[...], ga_ref[...], u_ref[...], gb_ref[...]
        sa, dsa = _silu_parts(ga)
        sb, dsb = _silu_parts(gb)
        ext = jnp.concatenate([jnp.where(j > 0, uh_ref[...], 0.0), u], axis=0)
        s2 = ext + pltpu.roll(ext, 1, 0)
        s4 = s2 + pltpu.roll(s2, 2, 0)
        s8 = s4 + pltpu.roll(s4, 4, 0)
        s16 = s8 + pltpu.roll(s8, 8, 0)
        cnt = _pool_cnt(j * ts, ts)
        pooled = (_pick_groups(s2, s4, s8, s16, POOL_HALO, POOL_HALO + ts) / cnt - u).astype(BF16)
        mixed = jnp.concatenate(
            [_dot(pooled[:, g * POOL_C:(g + 1) * POOL_C], pw_ref[g]) for g in range(POOL_G)], axis=1)
        ps = ps_ref[...]
        scaled = mixed * ps
        ycat = jnp.concatenate([o * sa, scaled * sb], axis=1).astype(BF16)
        ycat_ref[...] = ycat
        z = ALPHA * x_ref[...] + _dot(ycat, wo_ref[...])
        mu = jnp.mean(z, axis=-1, keepdims=True)
        zc = z - mu
        rstd = lax.rsqrt(jnp.mean(zc * zc, axis=-1, keepdims=True) + LN_EPS)
        xhat = zc * rstd
        lg = lg_ref[...]
        diff = xhat * lg + lb_ref[...] - tgt_ref[...]
        loss_ref[...] += jnp.sum(diff * diff) * (0.5 / D_MODEL)
        dy = diff * (1.0 / D_MODEL)
        glb_ref[...] += jnp.sum(dy, axis=0, keepdims=True)
        glg_ref[...] += jnp.sum(dy * xhat, axis=0, keepdims=True)
        dxh = dy * lg
        dz = rstd * (dxh - jnp.mean(dxh, axis=-1, keepdims=True) - xhat * jnp.mean(dxh * xhat, axis=-1, keepdims=True))
        dz_ref[...] = dz
        dycat = _dot_nt(dz.astype(BF16), wo_ref[...])
        dya, dyb = dycat[:, :MLA_W], dycat[:, MLA_W:]
        do = dya * sa
        do_ref[...] = do.astype(BF16)
        doo = do * o
        for hd in range(HEADS):
            dl_ref[hd] = jnp.sum(doo[:, hd * V_DIM:(hd + 1) * V_DIM].T, axis=0, keepdims=True)
        dga_ref[...] = (dya * o * dsa).astype(BF16)
        dgb_ref[...] = (dyb * scaled * dsb).astype(BF16)
        dscaled = dyb * sb
        gps_ref[...] += jnp.sum(dscaled * mixed, axis=0, keepdims=True)
        dmixed = (dscaled * ps).astype(BF16)
        dpooled = []
        for g in range(POOL_G):
            cols = slice(g * POOL_C, (g + 1) * POOL_C)
            gpw_ref[g] += _dot_tn(pooled[:, cols], dmixed[:, cols])
            dpooled.append(_dot_nt(dmixed[:, cols], pw_ref[g]))
        dpc_ref[...] = jnp.concatenate(dpooled, axis=1) / cnt

    def tile(w):
        return pl.BlockSpec((None, ts, w), lambda i: (i // nt, i % nt, 0))

    def whole(a):
        nd = a.ndim
        return pl.BlockSpec(a.shape, lambda i: (0,) * nd)

    halo = pl.BlockSpec((None, POOL_HALO, POOL_W), lambda i: (i // nt, jnp.maximum((i % nt) * hb - 1, 0), 0))
    acc_shapes = [(8, 128), (1, D_MODEL), (1, D_MODEL), (1, POOL_W), (POOL_G, POOL_C, POOL_C)]
    tile_outs = [(D_MODEL, F32), (D_MODEL, BF16), (MLA_W, BF16), (MLA_W, BF16), (POOL_W, BF16), (POOL_W, F32)]
    return pl.pallas_call(
        body, name="post", grid=(bb * nt,),
        in_specs=[tile(D_MODEL), tile(D_MODEL), tile(MLA_W), tile(MLA_W), tile(POOL_W), halo, tile(POOL_W),
                  whole(w_out), whole(pool_wb), whole(pool_scale), whole(ln_g), whole(ln_b)],
        out_specs=[tile(w) for w, _ in tile_outs]
        + [pl.BlockSpec((None, HEADS, 1, ts), lambda i: (i // nt, 0, 0, i % nt))]
        + [pl.BlockSpec(s, lambda i, n=len(s): (0,) * n) for s in acc_shapes],
        out_shape=[jax.ShapeDtypeStruct((bb, ss, w), dt) for w, dt in tile_outs]
        + [jax.ShapeDtypeStruct((bb, HEADS, 1, ss), F32)]
        + [jax.ShapeDtypeStruct(s, F32) for s in acc_shapes],
        compiler_params=_params(48, ("arbitrary",)))(x, tgt, o, ga, u, u, gb, w_out, pool_wb, pool_scale, ln_g, ln_b)


def _attn_bwd(qcat, kcat, v, do, lse, dl, pos_col, pos_blk, qmax, kmin, tq, tk):
    bb, ss, _ = qcat.shape
    nq, nk = ss // tq, ss // tk

    def body(qmax_ref, kmin_ref, q_ref, k_ref, v_ref, do_ref, lse_ref, dl_ref, pc_ref, pb_ref,
             dq_ref, dk_ref, dv_ref, dqt_s, dk_s, dv_s):
        b = pl.program_id(0)
        dqt_s[...] = jnp.zeros(dqt_s.shape, F32)

        def kv_step(ki, carry):
            krows = pl.ds(pl.multiple_of(ki * tk, tk), tk)
            k = k_ref[krows, :]
            kt = k.T
            vv = v_ref[krows, :]
            pk = pc_ref[krows, :]
            first = kmin_ref[b * nk + ki]
            dk_s[...] = jnp.zeros(dk_s.shape, F32)
            dv_s[...] = jnp.zeros(dv_s.shape, F32)

            def q_step(qi, c2):
                @pl.when(qmax_ref[b * nq + qi] >= first)
                def _():
                    qrows = pl.ds(pl.multiple_of(qi * tq, tq), tq)
                    q = q_ref[qrows, :]
                    dd = do_ref[qrows, :]
                    st = _dot_nt(k, q) * SCALE
                    st = jnp.where(pb_ref[qi] >= pk, st, NEG)
                    pt = jnp.exp(st - lse_ref[qi])
                    dv_s[...] += _dot(pt.astype(BF16), dd)
                    dpt = _dot_nt(vv, dd)
                    dst = (pt * (dpt - dl_ref[qi]) * SCALE).astype(BF16)
                    dk_s[...] += _dot(dst, q)
                    dqt_s[qi] += _dot(kt, dst)
                return c2

            lax.fori_loop(0, nq, q_step, 0)
            dk_ref[krows, :] = dk_s[...]
            dv_ref[krows, :] = dv_s[...].astype(BF16)
            return carry

        lax.fori_loop(0, nk, kv_step, 0)
        for qi in range(nq):
            dq_ref[qi * tq:(qi + 1) * tq, :] = dqt_s[qi].T

    def per_head(w):
        return pl.BlockSpec((None, ss, w), lambda b, h, *_: (b, 0, h))

    def rows_of_head():
        return pl.BlockSpec((None, None, nq, 1, tq), lambda b, h, *_: (b, h, 0, 0, 0))

    grid_spec = pltpu.PrefetchScalarGridSpec(
        num_scalar_prefetch=2, grid=(bb, HEADS),
        in_specs=[per_head(HEAD_PAD), per_head(HEAD_PAD), per_head(V_DIM), per_head(V_DIM),
                  rows_of_head(), rows_of_head(),
                  pl.BlockSpec((None, ss, 1), lambda b, h, *_: (b, 0, 0)),
                  pl.BlockSpec((None, nq, 1, tq), lambda b, h, *_: (b, 0, 0, 0))],
        out_specs=[per_head(HEAD_PAD), per_head(HEAD_PAD), per_head(V_DIM)],
        scratch_shapes=[pltpu.VMEM((nq, HEAD_PAD, tq), F32), pltpu.VMEM((tk, HEAD_PAD), F32),
                        pltpu.VMEM((tk, V_DIM), F32)])
    return pl.pallas_call(
        body, name="attn_bwd", grid_spec=grid_spec,
        out_shape=[jax.ShapeDtypeStruct((bb, ss, HEADS * HEAD_PAD), F32),
                   jax.ShapeDtypeStruct((bb, ss, HEADS * HEAD_PAD), F32),
                   jax.ShapeDtypeStruct((bb, ss, MLA_W), BF16)],
        compiler_params=_params(56, ("arbitrary", "arbitrary")))(qmax, kmin, qcat, kcat, v, do, lse, dl, pos_col, pos_blk)


def _bwd_mid(dqcat, dkcat, dv, xq, xkv, rc, rsa, rsb, wq, wkv, gq, gkv, dga, dgb, dpc, dz, w1, ts):
    bb, ss, _ = dz.shape
    nt = ss // ts
    hb = ts // POOL_HALO

    def body(dq_ref, dk_ref, dv_ref, xq_ref, xkv_ref, c_ref, sa_ref, sb_ref, wq_ref, wkv_ref, gq_ref, gkv_ref,
             dga_ref, dgb_ref, dpc_ref, dph_ref, dz_ref, w1_ref,
             gx_ref, dh_ref, dqp_ref, dkv_ref, ggq_ref, ggkv_ref):
        step = pl.program_id(0)
        j = step % nt

        @pl.when(step == 0)
        def _():
            ggq_ref[...] = jnp.zeros(ggq_ref.shape, F32)
            ggkv_ref[...] = jnp.zeros(ggkv_ref.shape, F32)

        c, sa, sb = c_ref[...], sa_ref[...], sb_ref[...]
        dq, dk = dq_ref[...], dk_ref[...]
        dkr = jnp.zeros((ts, HEAD_PAD - NOPE), F32)
        for hd in range(HEADS):
            lo = hd * HEAD_PAD
            dqp_ref[:, lo:lo + NOPE] = dq[:, lo:lo + NOPE].astype(BF16)
            dqp_ref[:, lo + NOPE:lo + HEAD_PAD] = _rope_bwd(dq[:, lo + NOPE:lo + HEAD_PAD], c, sa, sb).astype(BF16)
            dkv_ref[:, hd * NOPE:(hd + 1) * NOPE] = dk[:, lo:lo + NOPE].astype(BF16)
            dkr = dkr + dk[:, lo + NOPE:lo + HEAD_PAD]
        dkv_ref[:, MLA_W:] = dv_ref[...]
        dh_ref[:, C_KR:C_GA] = _rope_bwd(dkr, c, sa, sb).astype(BF16)

        def rms_bwd(x, g, dn):
            inv = lax.rsqrt(jnp.mean(x * x, axis=-1, keepdims=True) + RMS_EPS)
            xh = x * inv
            dxh = dn * g
            return inv * (dxh - xh * jnp.mean(dxh * xh, axis=-1, keepdims=True)), jnp.sum(dn * xh, axis=0, keepdims=True)

        dxq, ggq = rms_bwd(xq_ref[...], gq_ref[...], _dot_nt(dqp_ref[...], wq_ref[...]))
        dxkv, ggkv = rms_bwd(xkv_ref[...], gkv_ref[...], _dot_nt(dkv_ref[...], wkv_ref[...]))
        ggq_ref[...] += ggq
        ggkv_ref[...] += ggkv
        dh_ref[:, C_XQ:C_XKV] = dxq.astype(BF16)
        dh_ref[:, C_XKV:C_KR] = dxkv.astype(BF16)
        dh_ref[:, C_GA:C_U] = dga_ref[...]
        dh_ref[:, C_GB:IN_WP] = dgb_ref[...]
        dpc = dpc_ref[...]
        n = ts + POOL_HALO
        ext = jnp.concatenate([dpc, jnp.where(j < nt - 1, dph_ref[...], 0.0)], axis=0)
        r2 = ext + pltpu.roll(ext, n - 1, 0)
        r4 = r2 + pltpu.roll(r2, n - 2, 0)
        r8 = r4 + pltpu.roll(r4, n - 4, 0)
        r16 = r8 + pltpu.roll(r8, n - 8, 0)
        du = _pick_groups(r2, r4, r8, r16, 0, ts) - dpc * _pool_cnt(j * ts, ts)
        dh_ref[:, C_U:C_GB] = du.astype(BF16)
        gx_ref[...] = ALPHA * dz_ref[...] + _dot_nt(dh_ref[...], w1_ref[...])

    def tile(w):
        return pl.BlockSpec((None, ts, w), lambda i: (i // nt, i % nt, 0))

    def whole(a):
        return pl.BlockSpec(a.shape, lambda i: (0, 0))

    halo = pl.BlockSpec((None, POOL_HALO, POOL_W),
                        lambda i: (i // nt, jnp.minimum((i % nt + 1) * hb, ss // POOL_HALO - 1), 0))
    return pl.pallas_call(
        body, name="bwd_mid", grid=(bb * nt,),
        in_specs=[tile(HEADS * HEAD_PAD), tile(HEADS * HEAD_PAD), tile(MLA_W), tile(Q_LORA), tile(KV_LORA),
                  tile(128), tile(128), tile(128), whole(wq), whole(wkv), whole(gq), whole(gkv),
                  tile(MLA_W), tile(POOL_W), tile(POOL_W), halo, tile(D_MODEL), whole(w1)],
        out_specs=[tile(D_MODEL), tile(IN_WP), tile(HEADS * HEAD_PAD), tile(2 * MLA_W),
                   pl.BlockSpec((1, Q_LORA), lambda i: (0, 0)), pl.BlockSpec((1, KV_LORA), lambda i: (0, 0))],
        out_shape=[jax.ShapeDtypeStruct((bb, ss, D_MODEL), F32), jax.ShapeDtypeStruct((bb, ss, IN_WP), BF16),
                   jax.ShapeDtypeStruct((bb, ss, HEADS * HEAD_PAD), BF16), jax.ShapeDtypeStruct((bb, ss, 2 * MLA_W), BF16),
                   jax.ShapeDtypeStruct((1, Q_LORA), F32), jax.ShapeDtypeStruct((1, KV_LORA), F32)],
        compiler_params=_params(48, ("arbitrary",)))(
            dqcat, dkcat, dv, xq, xkv, rc, rsa, rsb, wq, wkv, gq, gkv, dga, dgb, dpc, dpc, dz, w1)


def _grad_w(a, b, bt, name):
    tt, m = a.shape
    n = b.shape[1]

    def body(a_ref, b_ref, out_ref):
        @pl.when(pl.program_id(0) == 0)
        def _():
            out_ref[...] = jnp.zeros(out_ref.shape, F32)

        out_ref[...] += _dot_tn(a_ref[...].astype(BF16), b_ref[...].astype(BF16))

    return pl.pallas_call(
        body, name=name, grid=(tt // bt,),
        in_specs=[pl.BlockSpec((bt, m), lambda i: (i, 0)), pl.BlockSpec((bt, n), lambda i: (i, 0))],
        out_specs=pl.BlockSpec((m, n), lambda i: (0, 0)),
        out_shape=jax.ShapeDtypeStruct((m, n), F32),
        compiler_params=_params(48, ("arbitrary",)))(a, b)


def _adamw(triples):
    n = len(triples)

    def body(*refs):
        ins, outs = refs[:4 * n], refs[4 * n:]
        for i in range(n):
            w, g, m, v = (r[...] for r in ins[4 * i:4 * i + 4])
            m = ADAM_B1 * m + (1.0 - ADAM_B1) * g
            v = ADAM_B2 * v + (1.0 - ADAM_B2) * jnp.square(g)
            m_hat = m / (1.0 - ADAM_B1 ** ADAM_STEP)
            v_hat = v / (1.0 - ADAM_B2 ** ADAM_STEP)
            outs[3 * i][...] = -ADAM_LR * (m_hat / (jnp.sqrt(v_hat) + ADAM_EPS) + ADAM_WD * w)
            outs[3 * i + 1][...] = m
            outs[3 * i + 2][...] = v

    flat = [a for t in triples for a in t]
    vmem = pl.BlockSpec(memory_space=pltpu.VMEM)
    res = pl.pallas_call(
        body, name="adamw", in_specs=[vmem] * len(flat), out_specs=[vmem] * (3 * n),
        out_shape=[jax.ShapeDtypeStruct(t[0].shape, F32) for t in triples for _ in range(3)],
        compiler_params=_params(48))(*flat)
    return [tuple(res[3 * i:3 * i + 3]) for i in range(n)]


def _shard_slab(w_in, w_uq, w_ukv, w_out):
    parts = [w_in.reshape(R_IN, 1024), w_uq.reshape(R_UQ, 1024), w_ukv.reshape(R_UKV, 1024), w_out,
             jnp.zeros((SLAB_ROWS - O_OUT - R_OUT, 1024), F32)]
    return jnp.concatenate(parts, axis=0)


def _unpack_weights(slabs):
    def cols(lo, rows, k, per):
        return slabs[:, lo:lo + rows].reshape(N_DEV, k, per).transpose(1, 0, 2).reshape(k, N_DEV * per)

    w_in = cols(0, R_IN, D_MODEL, IN_W // N_DEV)
    w1 = jnp.concatenate([w_in[:, :768 + ROPE], jnp.zeros((D_MODEL, 128 - ROPE), BF16), w_in[:, 768 + ROPE:]], axis=1)
    w_uq = cols(O_UQ, R_UQ, Q_LORA, HEADS * QK_DIM // N_DEV).reshape(Q_LORA, HEADS, QK_DIM)
    wq = jnp.pad(w_uq, ((0, 0), (0, 0), (0, HEAD_PAD - QK_DIM))).reshape(Q_LORA, HEADS * HEAD_PAD)
    w_ukv = cols(O_UKV, R_UKV, KV_LORA, 1024 // N_DEV)
    wkv = w_ukv.reshape(KV_LORA, HEADS, 2, NOPE).transpose(0, 2, 1, 3).reshape(KV_LORA, 2 * MLA_W)
    w_out = slabs[:, O_OUT:O_OUT + R_OUT].reshape(D_MODEL, D_MODEL)
    return w1, wq, wkv, w_out


def _grad_slab(g_w1, g_wq, g_wkv, g_wout):
    def rows(g, per):
        k = g.shape[0]
        return g.reshape(k, N_DEV, per).transpose(1, 0, 2).reshape(N_DEV, k * per // 1024, 1024)

    g_in = jnp.concatenate([g_w1[:, :768 + ROPE], g_w1[:, C_GA:]], axis=1)
    g_uq = g_wq.reshape(Q_LORA, HEADS, HEAD_PAD)[:, :, :QK_DIM].reshape(Q_LORA, HEADS * QK_DIM)
    g_ukv = g_wkv.reshape(KV_LORA, 2, HEADS, NOPE).transpose(0, 2, 1, 3).reshape(KV_LORA, 1024)
    parts = [rows(g_in, IN_W // N_DEV), rows(g_uq, HEADS * QK_DIM // N_DEV), rows(g_ukv, 1024 // N_DEV),
             g_wout.reshape(N_DEV, R_OUT, 1024), jnp.zeros((N_DEV, SLAB_ROWS - O_OUT - R_OUT, 1024), F32)]
    return jnp.concatenate(parts, axis=1)


def _rope_tables(positions):
    half = ROPE // 2
    inv_freq = ROPE_THETA ** (-jnp.arange(half, dtype=F32) / half)
    ang = positions.astype(F32)[..., None] * inv_freq
    cos, sin = jnp.cos(ang), jnp.sin(ang)
    zero = jnp.zeros_like(cos)
    rc = jnp.concatenate([cos, cos, zero, zero], axis=-1)
    rsa = jnp.concatenate([-sin, zero, zero, zero], axis=-1)
    rsb = jnp.concatenate([zero, sin, zero, zero], axis=-1)
    return rc, rsa, rsb


def _pad_rows(a, rows):
    return jnp.pad(a, ((0, rows - a.shape[0]), (0, 0)))


def kernel(x, positions, w_in, q_norm_g, w_uq, kv_norm_g, w_ukv, pool_w, pool_scale, w_out, ln_g, ln_b, loss_target, m_w_in, m_q_norm_g, m_w_uq, m_kv_norm_g, m_w_ukv, m_pool_w, m_pool_scale, m_w_out, m_ln_g, m_ln_b, v_w_in, v_q_norm_g, v_w_uq, v_kv_norm_g, v_w_ukv, v_pool_w, v_pool_scale, v_w_out, v_ln_g, v_ln_b):
    bb, ss, _ = x.shape
    tt = bb * ss
    tile = min(256, ss)
    atile = min(512, ss)
    nblk = ss // atile

    slab = _shard_slab(w_in, w_uq, w_ukv, w_out).astype(BF16)
    slabs = _all_gather(slab, "gather_weights", reduce=False).reshape(N_DEV, SLAB_ROWS, 1024)
    w1, wq, wkv, wo = _unpack_weights(slabs)

    gq, gkv = q_norm_g.reshape(1, Q_LORA), kv_norm_g.reshape(1, KV_LORA)
    ps = pool_scale.reshape(1, POOL_W)
    rc, rsa, rsb = (t.reshape(tt, 128) for t in _rope_tables(positions))
    pos_col = positions.reshape(bb, ss, 1)
    pos_blk = positions.reshape(bb, nblk, 1, atile)
    qmax = jnp.max(positions.reshape(bb, nblk, atile), axis=-1).reshape(-1)
    kmin = jnp.min(positions.reshape(bb, nblk, atile), axis=-1).reshape(-1)

    x2 = x.reshape(tt, D_MODEL)
    xq, xkv, ga, u, gb, qn, kvn, qcat, kcat, v = _fwd_in(x2, w1, gq, gkv, wq, wkv, rc, rsa, rsb, tile)
    as3 = lambda a: a.reshape(bb, ss, a.shape[-1])
    qcat, kcat, v = as3(qcat), as3(kcat), as3(v)
    o, lse = _attn_fwd(qcat, kcat, v, pos_col, pos_blk, qmax, kmin, atile, atile)
    dz, ycat, do, dga, dgb, dpc, dl, loss_p, g_lng, g_lnb, g_ps, g_pw = _post(
        x, loss_target, o, as3(ga), as3(u), as3(gb), wo, pool_w.astype(BF16), ps, ln_g, ln_b, tile)

    rows5 = lambda a: a.reshape(bb, HEADS, nblk, 1, atile)
    dqcat, dkcat, dv = _attn_bwd(qcat, kcat, v, do, rows5(lse), rows5(dl), pos_col, pos_blk, qmax, kmin, atile, atile)
    grad_x, dh, dqp, dkv, g_gq, g_gkv = _bwd_mid(
        dqcat, dkcat, dv, as3(xq), as3(xkv), as3(rc), as3(rsa), as3(rsb), wq, wkv, gq, gkv, dga, dgb, dpc, dz, w1, tile)
    bt = min(512, tt)
    g_w1 = _grad_w(x2, dh.reshape(tt, IN_WP), bt, "grad_w_in")
    g_wq = _grad_w(qn, dqp.reshape(tt, HEADS * HEAD_PAD), bt, "grad_w_uq")
    g_wkv = _grad_w(kvn, dkv.reshape(tt, 2 * MLA_W), bt, "grad_w_ukv")
    g_wo = _grad_w(ycat.reshape(tt, D_MODEL), dz.reshape(tt, D_MODEL), bt, "grad_w_out")

    g_big = _reduce_scatter(_grad_slab(g_w1, g_wq, g_wkv, g_wo), "reduce_scatter_grads")
    small = jnp.concatenate(
        [_pad_rows(g_lng.reshape(8, 128), 8), _pad_rows(g_lnb.reshape(8, 128), 8), _pad_rows(g_gq.reshape(4, 128), 8),
         _pad_rows(g_gkv.reshape(2, 128), 8), _pad_rows(g_ps.reshape(4, 128), 8), g_pw.reshape(POOL_G * POOL_C, 128),
         loss_p], axis=0)
    small = _all_gather(small, "all_reduce_small", reduce=True)
    loss = small[40 + POOL_G * POOL_C, 0]
    grads = {
        "w_in": g_big[:R_IN].reshape(D_MODEL, IN_W // N_DEV),
        "q_norm_g": small[16:20].reshape(1, Q_LORA),
        "w_uq": g_big[O_UQ:O_UKV].reshape(Q_LORA, HEADS * QK_DIM // N_DEV),
        "kv_norm_g": small[24:26].reshape(1, KV_LORA),
        "w_ukv": g_big[O_UKV:O_OUT].reshape(KV_LORA, 1024 // N_DEV),
        "pool_w": small[40:40 + POOL_G * POOL_C],
        "pool_scale": small[32:36].reshape(1, POOL_W),
        "w_out": g_big[O_OUT:O_OUT + R_OUT],
        "ln_g": small[0:8].reshape(1, D_MODEL),
        "ln_b": small[8:16].reshape(1, D_MODEL),
    }

    names = ["w_in", "q_norm_g", "w_uq", "kv_norm_g", "w_ukv", "pool_w", "pool_scale", "w_out", "ln_g", "ln_b"]
    weights = dict(w_in=w_in, q_norm_g=q_norm_g, w_uq=w_uq, kv_norm_g=kv_norm_g, w_ukv=w_ukv, pool_w=pool_w,
                   pool_scale=pool_scale, w_out=w_out, ln_g=ln_g, ln_b=ln_b)
    moms = dict(w_in=(m_w_in, v_w_in), q_norm_g=(m_q_norm_g, v_q_norm_g), w_uq=(m_w_uq, v_w_uq),
                kv_norm_g=(m_kv_norm_g, v_kv_norm_g), w_ukv=(m_w_ukv, v_w_ukv), pool_w=(m_pool_w, v_pool_w),
                pool_scale=(m_pool_scale, v_pool_scale), w_out=(m_w_out, v_w_out), ln_g=(m_ln_g, v_ln_g),
                ln_b=(m_ln_b, v_ln_b))
    as2 = lambda a, n: a.reshape(grads[n].shape)
    upd = _adamw([(as2(weights[n], n), grads[n], as2(moms[n][0], n), as2(moms[n][1], n)) for n in names])
    shaped = lambda a, n: a.reshape(weights[n].shape)
    return (loss, grad_x,
            *[shaped(grads[n], n) for n in names],
            *[shaped(upd[i][0], n) for i, n in enumerate(names)],
            *[shaped(upd[i][1], n) for i, n in enumerate(names)],
            *[shaped(upd[i][2], n) for i, n in enumerate(names)])
```

```python
import functools

import jax
import jax.numpy as jnp
from jax import lax
from jax.experimental import pallas as pl
from jax.experimental.pallas import tpu as pltpu

F32 = jnp.float32
BF16 = jnp.bfloat16
MESH = pl.DeviceIdType.MESH

N_DEV = 8
D_MODEL = 1024
HEADS = 4
NOPE = 128
ROPE = 64
V_DIM = 128
QK_DIM = NOPE + ROPE
HEAD_PAD = 256
Q_LORA = 512
KV_LORA = 256
MLA_W = HEADS * V_DIM
POOL_W = 512
POOL_G = 4
POOL_C = 128
POOL_HALO = 16
IN_W = 2368
IN_WP = 2432
C_XQ, C_XKV, C_KR, C_GA, C_U, C_GB = 0, 512, 768, 896, 1408, 1920
ROPE_THETA = 10000.0
RMS_EPS = 1e-6
LN_EPS = 1e-5
ALPHA = 2.0 ** 0.25
SCALE = QK_DIM ** -0.5
NEG = float(jnp.finfo(jnp.float32).min)

ADAM_LR = 0.001
ADAM_B1 = 0.9
ADAM_B2 = 0.999
ADAM_EPS = 1e-08
ADAM_WD = 0.01
ADAM_STEP = 10

R_IN, R_UQ, R_UKV, R_OUT = 296, 48, 32, 128
SLAB_ROWS = 512
O_UQ, O_UKV, O_OUT = R_IN, R_IN + R_UQ, R_IN + R_UQ + R_UKV

V7X_VMEM_BYTES = 64 * 1024 * 1024
ATTN_FWD_HEADS_PER_STEP = 4
ATTN_BWD_HEADS_PER_STEP = 2


def _params(vmem_mb, semantics=None):
    assert vmem_mb * 1024 * 1024 < V7X_VMEM_BYTES
    return pltpu.CompilerParams(vmem_limit_bytes=vmem_mb * 1024 * 1024, dimension_semantics=semantics)


def _dot(a, b):
    return jnp.dot(a, b, preferred_element_type=F32)


def _dot_nt(a, b):
    return lax.dot_general(a, b, (((1,), (1,)), ((), ())), preferred_element_type=F32)


def _dot_tn(a, b):
    return lax.dot_general(a, b, (((0,), (0,)), ((), ())), preferred_element_type=F32)


def _rope_fwd(t, c, sa, sb):
    return t * c + pltpu.roll(t, 96, 1) * sa + pltpu.roll(t, 32, 1) * sb


def _rope_bwd(d, c, sa, sb):
    return d * c + pltpu.roll(d * sa, 32, 1) + pltpu.roll(d * sb, 96, 1)


def _silu_parts(g):
    sig = jax.nn.sigmoid(g)
    return g * sig, sig * (1.0 + g * (1.0 - sig))


def _pool_cnt(row0, rows):
    t = row0 + lax.broadcasted_iota(jnp.int32, (rows, POOL_W), 0)
    w = 2 << (lax.broadcasted_iota(jnp.int32, (rows, POOL_W), 1) // POOL_C)
    return jnp.minimum(t + 1, w).astype(F32)


def _pick_groups(s2, s4, s8, s16, lo, hi):
    return jnp.concatenate([s2[lo:hi, 0:128], s4[lo:hi, 128:256], s8[lo:hi, 256:384], s16[lo:hi, 384:512]], axis=1)


def _all_gather(shard, name, reduce):
    m_per, n = shard.shape

    def body(x_ref, *rest):
        if reduce:
            sum_ref, out_ref, send_sems, recv_sems, local_sem = rest
        else:
            out_ref, send_sems, recv_sems, local_sem = rest
        x, y, c = lax.axis_index("x"), lax.axis_index("y"), lax.axis_index("c")
        me, sibling = (x, y, c), (x, y, 1 - c)
        chips = [(1 - x, y), (x, 1 - y), (1 - x, 1 - y)]

        def rows(px, py, pc):
            return out_ref.at[pl.ds((4 * px + 2 * py + pc) * m_per, m_per), :]

        def copy(k, block, to, src=None):
            return pltpu.make_async_remote_copy(
                src_ref=rows(*block) if src is None else src, dst_ref=rows(*block),
                send_sem=send_sems.at[k], recv_sem=recv_sems.at[k], device_id=to, device_id_type=MESH)

        mine = pltpu.make_async_copy(x_ref, rows(*me), local_sem)
        mine.start()
        first = [copy(0, me, sibling, src=x_ref)]
        first += [copy(1 + j, me, (*chip, c), src=x_ref) for j, chip in enumerate(chips)]
        for cp in first:
            cp.start()
        passed = [copy(4 + j, (*chip, c), sibling) for j, chip in enumerate(chips)]
        for j, chip in enumerate(chips):
            copy(1 + j, (*chip, c), me).wait_recv()
            passed[j].start()
        copy(0, sibling, me).wait_recv()
        for j, chip in enumerate(chips):
            copy(4 + j, (*chip, 1 - c), me).wait_recv()
        for cp in first + passed:
            cp.wait_send()
        mine.wait()
        if reduce:
            acc = out_ref[pl.ds(0, m_per), :]
            for d in range(1, N_DEV):
                acc = acc + out_ref[pl.ds(d * m_per, m_per), :]
            sum_ref[...] = acc

    gathered = jax.ShapeDtypeStruct((N_DEV * m_per, n), shard.dtype)
    vmem = pl.BlockSpec(memory_space=pltpu.VMEM)
    sems = [pltpu.SemaphoreType.DMA((7,)), pltpu.SemaphoreType.DMA((7,)), pltpu.SemaphoreType.DMA]
    if reduce:
        return pl.pallas_call(
            body, name=name, out_shape=jax.ShapeDtypeStruct((m_per, n), shard.dtype),
            in_specs=[vmem], out_specs=vmem,
            scratch_shapes=[pltpu.VMEM(gathered.shape, gathered.dtype)] + sems,
            compiler_params=_params(32))(shard)
    return pl.pallas_call(
        body, name=name, out_shape=gathered, in_specs=[vmem], out_specs=vmem, scratch_shapes=sems,
        compiler_params=_params(32))(shard)


def _reduce_scatter(gslab, name):
    _, rr, ww = gslab.shape
    chunk = 64

    def body(g_hbm, out_ref, own_ref, recv1_ref, sendb_ref, recv2_ref, ld_sems, s1_send, s1_recv, s2_send, s2_recv):
        x, y, c = lax.axis_index("x"), lax.axis_index("y"), lax.axis_index("c")
        sibling = (x, y, 1 - c)
        chips = [(1 - x, y), (x, 1 - y), (1 - x, 1 - y)]
        loads = [pltpu.make_async_copy(g_hbm.at[2 * k + c], own_ref.at[k], ld_sems.at[k]) for k in range(4)]
        stage1 = [pltpu.make_async_remote_copy(
            src_ref=g_hbm.at[2 * k + (1 - c)], dst_ref=recv1_ref.at[k], send_sem=s1_send.at[k],
            recv_sem=s1_recv.at[k], device_id=sibling, device_id_type=MESH) for k in range(4)]
        for cp in loads + stage1:
            cp.start()
        for cp in loads:
            cp.wait()
        for cp in stage1:
            cp.wait_recv()

        stage2 = []
        for r, (cx, cy) in enumerate(chips):
            kk = 2 * cx + cy

            def pack(i, carry, r=r, kk=kk):
                rows = pl.ds(pl.multiple_of(i * chunk, chunk), chunk)
                sendb_ref[r, rows, :] = (own_ref[kk, rows, :] + recv1_ref[kk, rows, :]).astype(BF16)
                return carry

            lax.fori_loop(0, rr // chunk, pack, 0)
            cp = pltpu.make_async_remote_copy(
                src_ref=sendb_ref.at[r], dst_ref=recv2_ref.at[r], send_sem=s2_send.at[r],
                recv_sem=s2_recv.at[r], device_id=(cx, cy, c), device_id_type=MESH)
            cp.start()
            stage2.append(cp)
        for cp in stage2:
            cp.wait_recv()
        mine = 2 * x + y

        def total(i, carry):
            rows = pl.ds(pl.multiple_of(i * chunk, chunk), chunk)
            acc = own_ref[mine, rows, :] + recv1_ref[mine, rows, :]
            for r in range(3):
                acc = acc + recv2_ref[r, rows, :].astype(F32)
            out_ref[rows, :] = acc
            return carry

        lax.fori_loop(0, rr // chunk, total, 0)
        for cp in stage1 + stage2:
            cp.wait_send()

    return pl.pallas_call(
        body, name=name, out_shape=jax.ShapeDtypeStruct((rr, ww), F32),
        in_specs=[pl.BlockSpec(memory_space=pl.ANY)], out_specs=pl.BlockSpec(memory_space=pltpu.VMEM),
        scratch_shapes=[
            pltpu.VMEM((4, rr, ww), F32), pltpu.VMEM((4, rr, ww), F32),
            pltpu.VMEM((3, rr, ww), BF16), pltpu.VMEM((3, rr, ww), BF16),
            pltpu.SemaphoreType.DMA((4,)), pltpu.SemaphoreType.DMA((4,)), pltpu.SemaphoreType.DMA((4,)),
            pltpu.SemaphoreType.DMA((3,)), pltpu.SemaphoreType.DMA((3,))],
        compiler_params=_params(48))(gslab)


def _fwd_in(x2, w1, gq, gkv, wq, wkv, pos, rope_rows, tm):
    tt = x2.shape[0]

    def body(x_ref, w1_ref, gq_ref, gkv_ref, wq_ref, wkv_ref, pos_ref, rr_ref,
             xq_ref, xkv_ref, ga_ref, u_ref, gb_ref, qn_ref, kvn_ref, qcat_ref, kcat_ref, v_ref,
             c_ref, sa_ref, sb_ref):
        ang = pos_ref[...].astype(F32) * rr_ref[0:1, :]
        cos, sin = jnp.cos(ang), jnp.sin(ang)
        c, sa, sb = cos * rr_ref[1:2, :], sin * rr_ref[2:3, :], sin * rr_ref[3:4, :]
        c_ref[...] = c
        sa_ref[...] = sa
        sb_ref[...] = sb
        h = _dot(x_ref[...].astype(BF16), w1_ref[...])
        xq = h[:, C_XQ:C_XKV]
        xkv = h[:, C_XKV:C_KR]
        xq_ref[...] = xq
        xkv_ref[...] = xkv
        ga_ref[...] = h[:, C_GA:C_U]
        u_ref[...] = h[:, C_U:C_GB]
        gb_ref[...] = h[:, C_GB:IN_WP]
        qn =(xq * lax.rsqrt(jnp.mean(xq * xq, axis=-1, keepdims=True) + RMS_EPS) * gq_ref[...]).astype(BF16)
        kvn = (xkv * lax.rsqrt(jnp.mean(xkv * xkv, axis=-1, keepdims=True) + RMS_EPS) * gkv_ref[...]).astype(BF16)
        qn_ref[...] = qn
        kvn_ref[...] = kvn
        q = _dot(qn, wq_ref[...])
        kv = _dot(kvn, wkv_ref[...])
        kr = _rope_fwd(h[:, C_KR:C_GA], c, sa, sb).astype(BF16)
        for hd in range(HEADS):
            lo = hd * HEAD_PAD
            qcat_ref[:, lo:lo + NOPE] = q[:, lo:lo + NOPE].astype(BF16)
            qcat_ref[:, lo + NOPE:lo + HEAD_PAD] = _rope_fwd(q[:, lo + NOPE:lo + HEAD_PAD], c, sa, sb).astype(BF16)
            kcat_ref[:, lo:lo + NOPE] = kv[:, hd * NOPE:(hd + 1) * NOPE].astype(BF16)
            kcat_ref[:, lo + NOPE:lo + HEAD_PAD] = kr
        v_ref[...] = kv[:, MLA_W:].astype(BF16)

    def tile(w):
        return pl.BlockSpec((tm, w), lambda i: (i, 0))

    def whole(a):
        return pl.BlockSpec(a.shape, lambda i: (0, 0))

    outs = [(Q_LORA, F32), (KV_LORA, F32), (MLA_W, F32), (POOL_W, F32), (POOL_W, F32),
            (Q_LORA, BF16), (KV_LORA, BF16), (HEADS * HEAD_PAD, BF16), (HEADS * HEAD_PAD, BF16), (MLA_W, BF16),
            (128, F32), (128, F32), (128, F32)]
    return pl.pallas_call(
        body, name="fwd_in", grid=(tt // tm,),
        in_specs=[tile(D_MODEL), whole(w1), whole(gq), whole(gkv), whole(wq), whole(wkv), tile(1), whole(rope_rows)],
        out_specs=[tile(w) for w, _ in outs],
        out_shape=[jax.ShapeDtypeStruct((tt, w), dt) for w, dt in outs],
        compiler_params=_params(48, ("arbitrary",)))(x2, w1, gq, gkv, wq, wkv, pos, rope_rows)


def _attn_fwd(qcat, kcat, v, pos_col, pos_blk, qmax, kmin, tq, tk):
    bb, ss, _ = qcat.shape
    nq, nk = ss // tq, ss // tk
    lanes = 128
    hps = ATTN_FWD_HEADS_PER_STEP

    def body(qmax_ref, kmin_ref, q_ref, k_ref, v_ref, pc_ref, pb_ref, o_ref, lse_ref, m_s, acc_s):
        b, qi = pl.program_id(0), pl.program_id(2)
        m_s[...] = jnp.full(m_s.shape, NEG, F32)
        acc_s[...] = jnp.zeros(acc_s.shape, F32)
        pq = pc_ref[...]
        reach = qmax_ref[b * nq + qi]
        ones = jnp.ones((tk, lanes), BF16)

        def step(ki, carry):
            @pl.when(reach >= kmin_ref[b * nk + ki])
            def _():
                rows = pl.ds(pl.multiple_of(ki * tk, tk), tk)
                mask = pq >= pb_ref[ki]
                for hd in range(hps):
                    qk = slice(hd * HEAD_PAD, (hd + 1) * HEAD_PAD)
                    s = _dot_nt(q_ref[:, qk], k_ref[rows, qk]) * SCALE
                    s = jnp.where(mask, s, NEG)
                    m_prev = m_s[hd]
                    m_new = jnp.maximum(m_prev, jnp.max(s, axis=-1, keepdims=True))
                    p = jnp.exp(s - jnp.tile(m_new, (1, tk // lanes)))
                    a = jnp.exp(m_prev - m_new)
                    vv = jnp.concatenate([v_ref[rows, hd * V_DIM:(hd + 1) * V_DIM], ones], axis=1)
                    acc_s[hd] = jnp.tile(a, (1, 2)) * acc_s[hd] + _dot(p.astype(BF16), vv)
                    m_s[hd] = m_new
            return carry

        lax.fori_loop(0, nk, step, 0)
        for hd in range(hps):
            acc = acc_s[hd]
            l = acc[:, V_DIM:]
            o_ref[:, hd * V_DIM:(hd + 1) * V_DIM] = acc[:, :V_DIM] / l
            lse_ref[hd] = (m_s[hd] + jnp.log(l)).T[0:1, :]

    grid_spec = pltpu.PrefetchScalarGridSpec(
        num_scalar_prefetch=2, grid=(bb, HEADS // hps, nq),
        in_specs=[
            pl.BlockSpec((None, tq, hps * HEAD_PAD), lambda b, h, i, *_: (b, i, h)),
            pl.BlockSpec((None, ss, hps * HEAD_PAD), lambda b, h, i, *_: (b, 0, h)),
            pl.BlockSpec((None, ss, hps * V_DIM), lambda b, h, i, *_: (b, 0, h)),
            pl.BlockSpec((None, tq, 1), lambda b, h, i, *_: (b, i, 0)),
            pl.BlockSpec((None, nk, 1, tk), lambda b, h, i, *_: (b, 0, 0, 0)),
        ],
        out_specs=[
            pl.BlockSpec((None, tq, hps * V_DIM), lambda b, h, i, *_: (b, i, h)),
            pl.BlockSpec((None, hps, 1, tq), lambda b, h, i, *_: (b, h, 0, i)),
        ],
        scratch_shapes=[pltpu.VMEM((hps, tq, lanes), F32), pltpu.VMEM((hps, tq, 2 * V_DIM), F32)])
    return pl.pallas_call(
        body, name="attn_fwd", grid_spec=grid_spec,
        out_shape=[jax.ShapeDtypeStruct((bb, ss, MLA_W), F32), jax.ShapeDtypeStruct((bb, HEADS, 1, ss), F32)],
        compiler_params=_params(48, ("arbitrary", "arbitrary", "arbitrary")))(qmax, kmin, qcat, kcat, v, pos_col, pos_blk)


def _post(x, tgt, o, ga, u, gb, w_out, pool_wb, pool_scale, ln_g, ln_b, ts):
    bb, ss, _ = x.shape
    nt = ss // ts
    hb = ts // POOL_HALO

    def body(x_ref, tgt_ref, o_ref, ga_ref, u_ref, uh_ref, gb_ref, wo_ref, pw_ref, ps_ref, lg_ref, lb_ref,
             dz_ref, ycat_ref, do_ref, dga_ref, dgb_ref, dpc_ref, dl_ref, loss_ref, glg_ref, glb_ref, gps_ref, gpw_ref):
        step = pl.program_id(0)
        j = step % nt

        @pl.when(step == 0)
        def _():
            for r in (loss_ref, glg_ref, glb_ref, gps_ref, gpw_ref):
                r[...] = jnp.zeros(r.shape, F32)

        o, ga, u, gb = o_ref[...], ga_ref[...], u_ref[...], gb_ref[...]
        sa, dsa = _silu_parts(ga)
        sb, dsb = _silu_parts(gb)
        ext = jnp.concatenate([jnp.where(j > 0, uh_ref[...], 0.0), u], axis=0)
        s2 = ext + pltpu.roll(ext, 1, 0)
        s4 = s2 + pltpu.roll(s2, 2, 0)
        s8 = s4 + pltpu.roll(s4, 4, 0)
        s16 = s8 + pltpu.roll(s8, 8, 0)
        cnt = _pool_cnt(j * ts, ts)
        pooled = (_pick_groups(s2, s4, s8, s16, POOL_HALO, POOL_HALO + ts) / cnt - u).astype(BF16)
        mixed = jnp.concatenate(
            [_dot(pooled[:, g * POOL_C:(g + 1) * POOL_C], pw_ref[g]) for g in range(POOL_G)], axis=1)
        ps = ps_ref[...]
        scaled = mixed * ps
        ycat = jnp.concatenate([o * sa, scaled * sb], axis=1).astype(BF16)
        ycat_ref[...] = ycat
        z = ALPHA * x_ref[...] + _dot(ycat, wo_ref[...])
        mu = jnp.mean(z, axis=-1, keepdims=True)
        zc = z - mu
        rstd = lax.rsqrt(jnp.mean(zc * zc, axis=-1, keepdims=True) + LN_EPS)
        xhat = zc * rstd
        lg = lg_ref[...]
        diff = xhat * lg + lb_ref[...] - tgt_ref[...]
        loss_ref[...] += jnp.sum(diff * diff) * (0.5 / D_MODEL)
        dy = diff * (1.0 / D_MODEL)
        glb_ref[...] += jnp.sum(dy, axis=0, keepdims=True)
        glg_ref[...] += jnp.sum(dy * xhat, axis=0, keepdims=True)
        dxh = dy * lg
        dz = rstd * (dxh - jnp.mean(dxh, axis=-1, keepdims=True) - xhat * jnp.mean(dxh * xhat, axis=-1, keepdims=True))
        dz_ref[...] = dz
        dycat = _dot_nt(dz.astype(BF16), wo_ref[...])
        dya, dyb = dycat[:, :MLA_W], dycat[:, MLA_W:]
        do = dya * sa
        do_ref[...] = do.astype(BF16)
        doo = do * o
        for hd in range(HEADS):
            dl_ref[hd] = jnp.sum(doo[:, hd * V_DIM:(hd + 1) * V_DIM].T, axis=0, keepdims=True)
        dga_ref[...] = (dya * o * dsa).astype(BF16)
        dgb_ref[...] = (dyb * scaled * dsb).astype(BF16)
        dscaled = dyb * sb
        gps_ref[...] += jnp.sum(dscaled * mixed, axis=0, keepdims=True)
        dmixed = (dscaled * ps).astype(BF16)
        dpooled = []
        for g in range(POOL_G):
            cols = slice(g * POOL_C, (g + 1) * POOL_C)
            gpw_ref[g] += _dot_tn(pooled[:, cols], dmixed[:, cols])
            dpooled.append(_dot_nt(dmixed[:, cols], pw_ref[g]))
        dpc_ref[...] = jnp.concatenate(dpooled, axis=1) / cnt

    def tile(w):
        return pl.BlockSpec((None, ts, w), lambda i: (i // nt, i % nt, 0))

    def whole(a):
        nd = a.ndim
        return pl.BlockSpec(a.shape, lambda i: (0,) * nd)

    halo = pl.BlockSpec((None, POOL_HALO, POOL_W), lambda i: (i // nt, jnp.maximum((i % nt) * hb - 1, 0), 0))
    acc_shapes = [(8, 128), (1, D_MODEL), (1, D_MODEL), (1, POOL_W), (POOL_G, POOL_C, POOL_C)]
    tile_outs = [(D_MODEL, F32), (D_MODEL, BF16), (MLA_W, BF16), (MLA_W, BF16), (POOL_W, BF16), (POOL_W, F32)]
    return pl.pallas_call(
        body, name="post", grid=(bb * nt,),
        in_specs=[tile(D_MODEL), tile(D_MODEL), tile(MLA_W), tile(MLA_W), tile(POOL_W), halo, tile(POOL_W),
                  whole(w_out), whole(pool_wb), whole(pool_scale), whole(ln_g), whole(ln_b)],
        out_specs=[tile(w) for w, _ in tile_outs]
        + [pl.BlockSpec((None, HEADS, 1, ts), lambda i: (i // nt, 0, 0, i % nt))]
        + [pl.BlockSpec(s, lambda i, n=len(s): (0,) * n) for s in acc_shapes],
        out_shape=[jax.ShapeDtypeStruct((bb, ss, w), dt) for w, dt in tile_outs]
        + [jax.ShapeDtypeStruct((bb, HEADS, 1, ss), F32)]
        + [jax.ShapeDtypeStruct(s, F32) for s in acc_shapes],
        compiler_params=_params(48, ("arbitrary",)))(x, tgt, o, ga, u, u, gb, w_out, pool_wb, pool_scale, ln_g, ln_b)


def _attn_bwd(qcat, kcat, v, do, lse, dl, pos_col, pos_blk, qmax, kmin, tq, tk):
    bb, ss, _ = qcat.shape
    nq, nk = ss // tq, ss // tk
    hps = ATTN_BWD_HEADS_PER_STEP

    def body(qmax_ref, kmin_ref, q_ref, k_ref, v_ref, do_ref, lse_ref, dl_ref, pc_ref, pb_ref,
             dq_ref, dk_ref, dv_ref, dqt_s, dk_s, dv_s):
        b = pl.program_id(0)
        dqt_s[...] = jnp.zeros(dqt_s.shape, F32)

        def kv_step(ki, carry):
            krows = pl.ds(pl.multiple_of(ki * tk, tk), tk)
            kts = [k_ref[krows, hd * HEAD_PAD:(hd + 1) * HEAD_PAD].T for hd in range(hps)]
            pk = pc_ref[krows, :]
            first = kmin_ref[b * nk + ki]
            dk_s[...] = jnp.zeros(dk_s.shape, F32)
            dv_s[...] = jnp.zeros(dv_s.shape, F32)

            def q_step(qi, c2):
                @pl.when(qmax_ref[b * nq + qi] >= first)
                def _():
                    qrows = pl.ds(pl.multiple_of(qi * tq, tq), tq)
                    mask = pb_ref[qi] >= pk
                    for hd in range(hps):
                        qk = slice(hd * HEAD_PAD, (hd + 1) * HEAD_PAD)
                        vs = slice(hd * V_DIM, (hd + 1) * V_DIM)
                        q = q_ref[qrows, qk]
                        dd = do_ref[qrows, vs]
                        st = _dot_nt(k_ref[krows, qk], q) * SCALE
                        st = jnp.where(mask, st, NEG)
                        pt = jnp.exp(st - lse_ref[hd, qi])
                        dv_s[:, vs] += _dot(pt.astype(BF16), dd)
                        dpt = _dot_nt(v_ref[krows, vs], dd)
                        dst = (pt * (dpt - dl_ref[hd, qi]) * SCALE).astype(BF16)
                        dk_s[:, qk] += _dot(dst, q)
                        dqt_s[hd, qi] += _dot(kts[hd], dst)
                return c2

            lax.fori_loop(0, nq, q_step, 0)
            dk_ref[krows, :] = dk_s[...]
            dv_ref[krows, :] = dv_s[...].astype(BF16)
            return carry

        lax.fori_loop(0, nk, kv_step, 0)
        for hd in range(hps):
            for qi in range(nq):
                dq_ref[qi * tq:(qi + 1) * tq, hd * HEAD_PAD:(hd + 1) * HEAD_PAD] = dqt_s[hd, qi].T

    def per_head(w):
        return pl.BlockSpec((None, ss, hps * w), lambda b, h, *_: (b, 0, h))

    def rows_of_head():
        return pl.BlockSpec((None, hps, nq, 1, tq), lambda b, h, *_: (b, h, 0, 0, 0))

    grid_spec = pltpu.PrefetchScalarGridSpec(
        num_scalar_prefetch=2, grid=(bb, HEADS // hps),
        in_specs=[per_head(HEAD_PAD), per_head(HEAD_PAD), per_head(V_DIM), per_head(V_DIM),
                  rows_of_head(), rows_of_head(),
                  pl.BlockSpec((None, ss, 1), lambda b, h, *_: (b, 0, 0)),
                  pl.BlockSpec((None, nq, 1, tq), lambda b, h, *_: (b, 0, 0, 0))],
        out_specs=[per_head(HEAD_PAD), per_head(HEAD_PAD), per_head(V_DIM)],
        scratch_shapes=[pltpu.VMEM((hps, nq, HEAD_PAD, tq), F32), pltpu.VMEM((tk, hps * HEAD_PAD), F32),
                        pltpu.VMEM((tk, hps * V_DIM), F32)])
    return pl.pallas_call(
        body, name="attn_bwd", grid_spec=grid_spec,
        out_shape=[jax.ShapeDtypeStruct((bb, ss, HEADS * HEAD_PAD), F32),
                   jax.ShapeDtypeStruct((bb, ss, HEADS * HEAD_PAD), F32),
                   jax.ShapeDtypeStruct((bb, ss, MLA_W), BF16)],
        compiler_params=_params(56, ("arbitrary", "arbitrary")))(qmax, kmin, qcat, kcat, v, do, lse, dl, pos_col, pos_blk)


def _bwd_mid(dqcat, dkcat, dv, xq, xkv, rc, rsa, rsb, wq, wkv, gq, gkv, dga, dgb, dpc, dz, w1, ts):
    bb, ss, _ = dz.shape
    nt = ss // ts
    hb = ts // POOL_HALO

    def body(dq_ref, dk_ref, dv_ref, xq_ref, xkv_ref, c_ref, sa_ref, sb_ref, wq_ref, wkv_ref, gq_ref, gkv_ref,
             dga_ref, dgb_ref, dpc_ref, dph_ref, dz_ref, w1_ref,
             gx_ref, dh_ref, dqp_ref, dkv_ref, ggq_ref, ggkv_ref):
        step = pl.program_id(0)
        j = step % nt

        @pl.when(step == 0)
        def _():
            ggq_ref[...] = jnp.zeros(ggq_ref.shape, F32)
            ggkv_ref[...] = jnp.zeros(ggkv_ref.shape, F32)

        c, sa, sb = c_ref[...], sa_ref[...], sb_ref[...]
        dq, dk = dq_ref[...], dk_ref[...]
        dkr = jnp.zeros((ts, HEAD_PAD - NOPE), F32)
        for hd in range(HEADS):
            lo = hd * HEAD_PAD
            dqp_ref[:, lo:lo + NOPE] = dq[:, lo:lo + NOPE].astype(BF16)
            dqp_ref[:, lo + NOPE:lo + HEAD_PAD] = _rope_bwd(dq[:, lo + NOPE:lo + HEAD_PAD], c, sa, sb).astype(BF16)
            dkv_ref[:, hd * NOPE:(hd + 1) * NOPE] = dk[:, lo:lo + NOPE].astype(BF16)
            dkr = dkr + dk[:, lo + NOPE:lo + HEAD_PAD]
        dkv_ref[:, MLA_W:] = dv_ref[...]
        dh_ref[:, C_KR:C_GA] = _rope_bwd(dkr, c, sa, sb).astype(BF16)

        def rms_bwd(x, g, dn):
            inv = lax.rsqrt(jnp.mean(x * x, axis=-1, keepdims=True) + RMS_EPS)
            xh = x * inv
            dxh = dn * g
            return inv * (dxh - xh * jnp.mean(dxh * xh, axis=-1, keepdims=True)), jnp.sum(dn * xh, axis=0, keepdims=True)

        dxq, ggq = rms_bwd(xq_ref[...], gq_ref[...], _dot_nt(dqp_ref[...], wq_ref[...]))
        dxkv, ggkv = rms_bwd(xkv_ref[...], gkv_ref[...], _dot_nt(dkv_ref[...], wkv_ref[...]))
        ggq_ref[...] += ggq
        ggkv_ref[...] += ggkv
        dh_ref[:, C_XQ:C_XKV] = dxq.astype(BF16)
        dh_ref[:, C_XKV:C_KR] = dxkv.astype(BF16)
        dh_ref[:, C_GA:C_U] = dga_ref[...]
        dh_ref[:, C_GB:IN_WP] = dgb_ref[...]
        dpc = dpc_ref[...]
        n = ts + POOL_HALO
        ext = jnp.concatenate([dpc, jnp.where(j < nt - 1, dph_ref[...], 0.0)], axis=0)
        r2 = ext + pltpu.roll(ext, n - 1, 0)
        r4 = r2 + pltpu.roll(r2, n - 2, 0)
        r8 = r4 + pltpu.roll(r4, n - 4, 0)
        r16 = r8 + pltpu.roll(r8, n - 8, 0)
        du = _pick_groups(r2, r4, r8, r16, 0, ts) - dpc * _pool_cnt(j * ts, ts)
        dh_ref[:, C_U:C_GB] = du.astype(BF16)
        gx_ref[...] = ALPHA * dz_ref[...] + _dot_nt(dh_ref[...], w1_ref[...])

    def tile(w):
        return pl.BlockSpec((None, ts, w), lambda i: (i // nt, i % nt, 0))

    def whole(a):
        return pl.BlockSpec(a.shape, lambda i: (0, 0))

    halo = pl.BlockSpec((None, POOL_HALO, POOL_W),
                        lambda i: (i // nt, jnp.minimum((i % nt + 1) * hb, ss // POOL_HALO - 1), 0))
    return pl.pallas_call(
        body, name="bwd_mid", grid=(bb * nt,),
        in_specs=[tile(HEADS * HEAD_PAD), tile(HEADS * HEAD_PAD), tile(MLA_W), tile(Q_LORA), tile(KV_LORA),
                  tile(128), tile(128), tile(128), whole(wq), whole(wkv), whole(gq), whole(gkv),
                  tile(MLA_W), tile(POOL_W), tile(POOL_W), halo, tile(D_MODEL), whole(w1)],
        out_specs=[tile(D_MODEL), tile(IN_WP), tile(HEADS * HEAD_PAD), tile(2 * MLA_W),
                   pl.BlockSpec((1, Q_LORA), lambda i: (0, 0)), pl.BlockSpec((1, KV_LORA), lambda i: (0, 0))],
        out_shape=[jax.ShapeDtypeStruct((bb, ss, D_MODEL), F32), jax.ShapeDtypeStruct((bb, ss, IN_WP), BF16),
                   jax.ShapeDtypeStruct((bb, ss, HEADS * HEAD_PAD), BF16), jax.ShapeDtypeStruct((bb, ss, 2 * MLA_W), BF16),
                   jax.ShapeDtypeStruct((1, Q_LORA), F32), jax.ShapeDtypeStruct((1, KV_LORA), F32)],
        compiler_params=_params(48, ("arbitrary",)))(
            dqcat, dkcat, dv, xq, xkv, rc, rsa, rsb, wq, wkv, gq, gkv, dga, dgb, dpc, dpc, dz, w1)


def _grad_w(a, b, bt, name):
    tt, m = a.shape
    n = b.shape[1]

    def body(a_ref, b_ref, out_ref):
        @pl.when(pl.program_id(0) == 0)
        def _():
            out_ref[...] = jnp.zeros(out_ref.shape, F32)

        out_ref[...] += _dot_tn(a_ref[...].astype(BF16), b_ref[...].astype(BF16))

    return pl.pallas_call(
        body, name=name, grid=(tt // bt,),
        in_specs=[pl.BlockSpec((bt, m), lambda i: (i, 0)), pl.BlockSpec((bt, n), lambda i: (i, 0))],
        out_specs=pl.BlockSpec((m, n), lambda i: (0, 0)),
        out_shape=jax.ShapeDtypeStruct((m, n), F32),
        compiler_params=_params(48, ("arbitrary",)))(a, b)


def _adamw(triples):
    n = len(triples)

    def body(*refs):
        ins, outs = refs[:4 * n], refs[4 * n:]
        for i in range(n):
            w, g, m, v = (r[...] for r in ins[4 * i:4 * i + 4])
            m = ADAM_B1 * m + (1.0 - ADAM_B1) * g
            v = ADAM_B2 * v + (1.0 - ADAM_B2) * jnp.square(g)
            m_hat = m / (1.0 - ADAM_B1 ** ADAM_STEP)
            v_hat = v / (1.0 - ADAM_B2 ** ADAM_STEP)
            outs[3 * i][...] = -ADAM_LR * (m_hat / (jnp.sqrt(v_hat) + ADAM_EPS) + ADAM_WD * w)
            outs[3 * i + 1][...] = m
            outs[3 * i + 2][...] = v

    flat = [a for t in triples for a in t]
    vmem = pl.BlockSpec(memory_space=pltpu.VMEM)
    res = pl.pallas_call(
        body, name="adamw", in_specs=[vmem] * len(flat), out_specs=[vmem] * (3 * n),
        out_shape=[jax.ShapeDtypeStruct(t[0].shape, F32) for t in triples for _ in range(3)],
        compiler_params=_params(48))(*flat)
    return [tuple(res[3 * i:3 * i + 3]) for i in range(n)]


def _shard_slab(w_in, w_uq, w_ukv, w_out):
    parts = [w_in.reshape(R_IN, 1024), w_uq.reshape(R_UQ, 1024), w_ukv.reshape(R_UKV, 1024), w_out,
             jnp.zeros((SLAB_ROWS - O_OUT - R_OUT, 1024), F32)]
    return jnp.concatenate(parts, axis=0)


def _unpack_weights(slabs):
    def cols(lo, rows, k, per):
        return slabs[:, lo:lo + rows].reshape(N_DEV, k, per).transpose(1, 0, 2).reshape(k, N_DEV * per)

    w_in = cols(0, R_IN, D_MODEL, IN_W // N_DEV)
    w1 = jnp.concatenate([w_in[:, :768 + ROPE], jnp.zeros((D_MODEL, 128 - ROPE), BF16), w_in[:, 768 + ROPE:]], axis=1)
    w_uq = cols(O_UQ, R_UQ, Q_LORA, HEADS * QK_DIM // N_DEV).reshape(Q_LORA, HEADS, QK_DIM)
    wq = jnp.pad(w_uq, ((0, 0), (0, 0), (0, HEAD_PAD - QK_DIM))).reshape(Q_LORA, HEADS * HEAD_PAD)
    w_ukv = cols(O_UKV, R_UKV, KV_LORA, 1024 // N_DEV)
    wkv = w_ukv.reshape(KV_LORA, HEADS, 2, NOPE).transpose(0, 2, 1, 3).reshape(KV_LORA, 2 * MLA_W)
    w_out = slabs[:, O_OUT:O_OUT + R_OUT].reshape(D_MODEL, D_MODEL)
    return w1, wq, wkv, w_out


def _grad_slab(g_w1, g_wq, g_wkv, g_wout):
    def rows(g, per):
        k = g.shape[0]
        return g.reshape(k, N_DEV, per).transpose(1, 0, 2).reshape(N_DEV, k * per // 1024, 1024)

    g_in = jnp.concatenate([g_w1[:, :768 + ROPE], g_w1[:, C_GA:]], axis=1)
    g_uq = g_wq.reshape(Q_LORA, HEADS, HEAD_PAD)[:, :, :QK_DIM].reshape(Q_LORA, HEADS * QK_DIM)
    g_ukv = g_wkv.reshape(KV_LORA, 2, HEADS, NOPE).transpose(0, 2, 1, 3).reshape(KV_LORA, 1024)
    parts = [rows(g_in, IN_W // N_DEV), rows(g_uq, HEADS * QK_DIM // N_DEV), rows(g_ukv, 1024 // N_DEV),
             g_wout.reshape(N_DEV, R_OUT, 1024), jnp.zeros((N_DEV, SLAB_ROWS - O_OUT - R_OUT, 1024), F32)]
    return jnp.concatenate(parts, axis=1)


def _rope_rows():
    half = ROPE // 2
    inv_freq = ROPE_THETA ** (-jnp.arange(half, dtype=F32) / half)
    zero, one = jnp.zeros((half,), F32), jnp.ones((half,), F32)
    rows = [jnp.concatenate(r) for r in (
        (inv_freq, inv_freq, zero, zero), (one, one, zero, zero), (-one, zero, zero, zero), (zero, one, zero, zero))]
    return jnp.stack(rows + [jnp.zeros((128,), F32)] * 4)


def _pad_rows(a, rows):
    return jnp.pad(a, ((0, rows - a.shape[0]), (0, 0)))


def kernel(x, positions, w_in, q_norm_g, w_uq, kv_norm_g, w_ukv, pool_w, pool_scale, w_out, ln_g, ln_b, loss_target, m_w_in, m_q_norm_g, m_w_uq, m_kv_norm_g, m_w_ukv, m_pool_w, m_pool_scale, m_w_out, m_ln_g, m_ln_b, v_w_in, v_q_norm_g, v_w_uq, v_kv_norm_g, v_w_ukv, v_pool_w, v_pool_scale, v_w_out, v_ln_g, v_ln_b):
    bb, ss, _ = x.shape
    tt = bb * ss
    tile = min(256, ss)
    atile = min(512, ss)
    nblk = ss // atile

    slab = _shard_slab(w_in, w_uq, w_ukv, w_out).astype(BF16)
    slabs = _all_gather(slab, "gather_weights", reduce=False).reshape(N_DEV, SLAB_ROWS, 1024)
    w1, wq, wkv, wo = _unpack_weights(slabs)

    gq, gkv = q_norm_g.reshape(1, Q_LORA), kv_norm_g.reshape(1, KV_LORA)
    ps = pool_scale.reshape(1, POOL_W)
    pos_col = positions.reshape(bb, ss, 1)
    pos_blk = positions.reshape(bb, nblk, 1, atile)
    qmax = jnp.max(positions.reshape(bb, nblk, atile), axis=-1).reshape(-1)
    kmin = jnp.min(positions.reshape(bb, nblk, atile), axis=-1).reshape(-1)

    x2 = x.reshape(tt, D_MODEL)
    xq, xkv, ga, u, gb, qn, kvn, qcat, kcat, v, rc, rsa, rsb = _fwd_in(
        x2, w1, gq, gkv, wq, wkv, positions.reshape(tt, 1), _rope_rows(), tile)
    as3 = lambda a: a.reshape(bb, ss, a.shape[-1])
    qcat, kcat, v = as3(qcat), as3(kcat), as3(v)
    o, lse = _attn_fwd(qcat, kcat, v, pos_col, pos_blk, qmax, kmin, atile, atile)
    dz, ycat, do, dga, dgb, dpc, dl, loss_p, g_lng, g_lnb, g_ps, g_pw = _post(
        x, loss_target, o, as3(ga), as3(u), as3(gb), wo, pool_w.astype(BF16), ps, ln_g, ln_b, tile)

    rows5 = lambda a: a.reshape(bb, HEADS, nblk, 1, atile)
    dqcat, dkcat, dv = _attn_bwd(qcat, kcat, v, do, rows5(lse), rows5(dl), pos_col, pos_blk, qmax, kmin, atile, atile)
    grad_x, dh, dqp, dkv, g_gq, g_gkv = _bwd_mid(
        dqcat, dkcat, dv, as3(xq), as3(xkv), as3(rc), as3(rsa), as3(rsb), wq, wkv, gq, gkv, dga, dgb, dpc, dz, w1, tile)
    bt = min(512, tt)
    g_w1 = _grad_w(x2, dh.reshape(tt, IN_WP), bt, "grad_w_in")
    g_wq = _grad_w(qn, dqp.reshape(tt, HEADS * HEAD_PAD), bt, "grad_w_uq")
    g_wkv = _grad_w(kvn, dkv.reshape(tt, 2 * MLA_W), bt, "grad_w_ukv")
    g_wo = _grad_w(ycat.reshape(tt, D_MODEL), dz.reshape(tt, D_MODEL), bt, "grad_w_out")

    g_big = _reduce_scatter(_grad_slab(g_w1, g_wq, g_wkv, g_wo), "reduce_scatter_grads")
    small = jnp.concatenate(
        [_pad_rows(g_lng.reshape(8, 128), 8), _pad_rows(g_lnb.reshape(8, 128), 8), _pad_rows(g_gq.reshape(4, 128), 8),
         _pad_rows(g_gkv.reshape(2, 128), 8), _pad_rows(g_ps.reshape(4, 128), 8), g_pw.reshape(POOL_G * POOL_C, 128),
         loss_p], axis=0)
    small = _all_gather(small, "all_reduce_small", reduce=True)
    loss = small[40 + POOL_G * POOL_C, 0]
    grads = {
        "w_in": g_big[:R_IN].reshape(D_MODEL, IN_W // N_DEV),
        "q_norm_g": small[16:20].reshape(1, Q_LORA),
        "w_uq": g_big[O_UQ:O_UKV].reshape(Q_LORA, HEADS * QK_DIM // N_DEV),
        "kv_norm_g": small[24:26].reshape(1, KV_LORA),
        "w_ukv": g_big[O_UKV:O_OUT].reshape(KV_LORA, 1024 // N_DEV),
        "pool_w": small[40:40 + POOL_G * POOL_C],
        "pool_scale": small[32:36].reshape(1, POOL_W),
        "w_out": g_big[O_OUT:O_OUT + R_OUT],
        "ln_g": small[0:8].reshape(1, D_MODEL),
        "ln_b": small[8:16].reshape(1, D_MODEL),
    }

    names = ["w_in", "q_norm_g", "w_uq", "kv_norm_g", "w_ukv", "pool_w", "pool_scale", "w_out", "ln_g", "ln_b"]
    weights = dict(w_in=w_in, q_norm_g=q_norm_g, w_uq=w_uq, kv_norm_g=kv_norm_g, w_ukv=w_ukv, pool_w=pool_w,
                   pool_scale=pool_scale, w_out=w_out, ln_g=ln_g, ln_b=ln_b)
    moms = dict(w_in=(m_w_in, v_w_in), q_norm_g=(m_q_norm_g, v_q_norm_g), w_uq=(m_w_uq, v_w_uq),
                kv_norm_g=(m_kv_norm_g, v_kv_norm_g), w_ukv=(m_w_ukv, v_w_ukv), pool_w=(m_pool_w, v_pool_w),
                pool_scale=(m_pool_scale, v_pool_scale), w_out=(m_w_out, v_w_out), ln_g=(m_ln_g, v_ln_g),
                ln_b=(m_ln_b, v_ln_b))
    as2 = lambda a, n: a.reshape(grads[n].shape)
    upd = _adamw([(as2(weights[n], n), grads[n], as2(moms[n][0], n), as2(moms[n][1], n)) for n in names])
    shaped = lambda a, n: a.reshape(weights[n].shape)
    return (loss, grad_x,
            *[shaped(grads[n], n) for n in names],
            *[shaped(upd[i][0], n) for i, n in enumerate(names)],
            *[shaped(upd[i][1], n) for i, n in enumerate(names)],
            *[shaped(upd[i][2], n) for i, n in enumerate(names)])
```

```python
import functools

import jax
import jax.numpy as jnp
from jax import lax
from jax.experimental import pallas as pl
from jax.experimental.pallas import tpu as pltpu

F32 = jnp.float32
BF16 = jnp.bfloat16
MESH = pl.DeviceIdType.MESH

N_DEV = 8
D_MODEL = 1024
HEADS = 4
NOPE = 128
ROPE = 64
V_DIM = 128
QK_DIM = NOPE + ROPE
HEAD_PAD = 256
Q_LORA = 512
KV_LORA = 256
MLA_W = HEADS * V_DIM
POOL_W = 512
POOL_G = 4
POOL_C = 128
POOL_HALO = 16
IN_W = 2368
IN_WP = 2432
C_XQ, C_XKV, C_KR, C_GA, C_U, C_GB = 0, 512, 768, 896, 1408, 1920
ROPE_THETA = 10000.0
RMS_EPS = 1e-6
LN_EPS = 1e-5
ALPHA = 2.0 ** 0.25
SCALE = QK_DIM ** -0.5
SCALE_LOG2 = SCALE * 1.4426950408889634
NEG = float(jnp.finfo(jnp.float32).min)

ADAM_LR = 0.001
ADAM_B1 = 0.9
ADAM_B2 = 0.999
ADAM_EPS = 1e-08
ADAM_WD = 0.01
ADAM_STEP = 10

R_OUT, R_MIX, R_UQ, R_IN = 128, 128, 96, 296
O_MIX, O_IN = R_OUT, R_OUT + R_MIX
SLAB_ROWS = 560

V7X_VMEM_BYTES = 64 * 1024 * 1024
ATTN_FWD_HEADS_PER_STEP = 4
ATTN_BWD_HEADS_PER_STEP = 2


def _params(vmem_mb, semantics=None):
    assert vmem_mb * 1024 * 1024 < V7X_VMEM_BYTES
    return pltpu.CompilerParams(vmem_limit_bytes=vmem_mb * 1024 * 1024, dimension_semantics=semantics)


def _dot(a, b):
    return jnp.dot(a, b, preferred_element_type=F32)


def _dot_nt(a, b):
    return lax.dot_general(a, b, (((1,), (1,)), ((), ())), preferred_element_type=F32)


def _dot_tn(a, b):
    return lax.dot_general(a, b, (((0,), (0,)), ((), ())), preferred_element_type=F32)


def _rope_fwd(t, c, sa, sb):
    return t * c + pltpu.roll(t, 96, 1) * sa + pltpu.roll(t, 32, 1) * sb


def _rope_bwd(d, c, sa, sb):
    return d * c + pltpu.roll(d * sa, 32, 1) + pltpu.roll(d * sb, 96, 1)


def _silu_parts(g):
    sig = jax.nn.sigmoid(g)
    return g * sig, sig * (1.0 + g * (1.0 - sig))


def _pool_cnt(row0, rows):
    t = row0 + lax.broadcasted_iota(jnp.int32, (rows, POOL_W), 0)
    w = 2 << (lax.broadcasted_iota(jnp.int32, (rows, POOL_W), 1) // POOL_C)
    return jnp.minimum(t + 1, w).astype(F32)


def _pick_groups(s2, s4, s8, s16, lo, hi):
    return jnp.concatenate([s2[lo:hi, 0:128], s4[lo:hi, 128:256], s8[lo:hi, 256:384], s16[lo:hi, 384:512]], axis=1)


def _all_gather(shard, name, reduce):
    m_per, n = shard.shape

    def body(x_ref, *rest):
        if reduce:
            sum_ref, out_ref, send_sems, recv_sems, local_sem = rest
        else:
            out_ref, send_sems, recv_sems, local_sem = rest
        x, y, c = lax.axis_index("x"), lax.axis_index("y"), lax.axis_index("c")
        me, sibling = (x, y, c), (x, y, 1 - c)
        chips = [(1 - x, y), (x, 1 - y), (1 - x, 1 - y)]

        def rows(px, py, pc):
            return out_ref.at[pl.ds((4 * px + 2 * py + pc) * m_per, m_per), :]

        def copy(k, block, to, src=None):
            return pltpu.make_async_remote_copy(
                src_ref=rows(*block) if src is None else src, dst_ref=rows(*block),
                send_sem=send_sems.at[k], recv_sem=recv_sems.at[k], device_id=to, device_id_type=MESH)

        mine = pltpu.make_async_copy(x_ref, rows(*me), local_sem)
        mine.start()
        first = [copy(0, me, sibling, src=x_ref)]
        first += [copy(1 + j, me, (*chip, c), src=x_ref) for j, chip in enumerate(chips)]
        for cp in first:
            cp.start()
        passed = [copy(4 + j, (*chip, c), sibling) for j, chip in enumerate(chips)]
        for j, chip in enumerate(chips):
            copy(1 + j, (*chip, c), me).wait_recv()
            passed[j].start()
        copy(0, sibling, me).wait_recv()
        for j, chip in enumerate(chips):
            copy(4 + j, (*chip, 1 - c), me).wait_recv()
        for cp in first + passed:
            cp.wait_send()
        mine.wait()
        if reduce:
            acc = out_ref[pl.ds(0, m_per), :]
            for d in range(1, N_DEV):
                acc = acc + out_ref[pl.ds(d * m_per, m_per), :]
            sum_ref[...] = acc

    gathered = jax.ShapeDtypeStruct((N_DEV * m_per, n), shard.dtype)
    vmem = pl.BlockSpec(memory_space=pltpu.VMEM)
    sems = [pltpu.SemaphoreType.DMA((7,)), pltpu.SemaphoreType.DMA((7,)), pltpu.SemaphoreType.DMA]
    if reduce:
        return pl.pallas_call(
            body, name=name, out_shape=jax.ShapeDtypeStruct((m_per, n), shard.dtype),
            in_specs=[vmem], out_specs=vmem,
            scratch_shapes=[pltpu.VMEM(gathered.shape, gathered.dtype)] + sems,
            compiler_params=_params(32))(shard)
    return pl.pallas_call(
        body, name=name, out_shape=gathered, in_specs=[vmem], out_specs=vmem, scratch_shapes=sems,
        compiler_params=_params(32))(shard)


def _reduce_scatter(gslab, name):
    _, rr, ww = gslab.shape
    chunk = 80
    assert rr % chunk == 0

    def body(g_hbm, out_ref, own_ref, recv1_ref, sendb_ref, recv2_ref, ld_sems, s1_send, s1_recv, s2_send, s2_recv):
        x, y, c = lax.axis_index("x"), lax.axis_index("y"), lax.axis_index("c")
        sibling = (x, y, 1 - c)
        chips = [(1 - x, y), (x, 1 - y), (1 - x, 1 - y)]
        loads = [pltpu.make_async_copy(g_hbm.at[2 * k + c], own_ref.at[k], ld_sems.at[k]) for k in range(4)]
        stage1 = [pltpu.make_async_remote_copy(
            src_ref=g_hbm.at[2 * k + (1 - c)], dst_ref=recv1_ref.at[k], send_sem=s1_send.at[k],
            recv_sem=s1_recv.at[k], device_id=sibling, device_id_type=MESH) for k in range(4)]
        for cp in loads + stage1:
            cp.start()
        for cp in loads:
            cp.wait()
        for cp in stage1:
            cp.wait_recv()

        stage2 = []
        for r, (cx, cy) in enumerate(chips):
            kk = 2 * cx + cy

            def pack(i, carry, r=r, kk=kk):
                rows = pl.ds(pl.multiple_of(i * chunk, chunk), chunk)
                sendb_ref[r, rows, :] = (own_ref[kk, rows, :] + recv1_ref[kk, rows, :]).astype(BF16)
                return carry

            lax.fori_loop(0, rr // chunk, pack, 0)
            cp = pltpu.make_async_remote_copy(
                src_ref=sendb_ref.at[r], dst_ref=recv2_ref.at[r], send_sem=s2_send.at[r],
                recv_sem=s2_recv.at[r], device_id=(cx, cy, c), device_id_type=MESH)
            cp.start()
            stage2.append(cp)
        for cp in stage2:
            cp.wait_recv()
        mine = 2 * x + y

        def total(i, carry):
            rows = pl.ds(pl.multiple_of(i * chunk, chunk), chunk)
            acc = own_ref[mine, rows, :] + recv1_ref[mine, rows, :]
            for r in range(3):
                acc = acc + recv2_ref[r, rows, :].astype(F32)
            out_ref[rows, :] = acc
            return carry

        lax.fori_loop(0, rr // chunk, total, 0)
        for cp in stage1 + stage2:
            cp.wait_send()

    return pl.pallas_call(
        body, name=name, out_shape=jax.ShapeDtypeStruct((rr, ww), F32),
        in_specs=[pl.BlockSpec(memory_space=pl.ANY)], out_specs=pl.BlockSpec(memory_space=pltpu.VMEM),
        scratch_shapes=[
            pltpu.VMEM((4, rr, ww), F32), pltpu.VMEM((4, rr, ww), F32),
            pltpu.VMEM((3, rr, ww), BF16), pltpu.VMEM((3, rr, ww), BF16),
            pltpu.SemaphoreType.DMA((4,)), pltpu.SemaphoreType.DMA((4,)), pltpu.SemaphoreType.DMA((4,)),
            pltpu.SemaphoreType.DMA((3,)), pltpu.SemaphoreType.DMA((3,))],
        compiler_params=_params(48))(gslab)


def _fwd_in(x2, w1, gq, gkv, wq, wkv, pos, rope_rows, tm):
    tt = x2.shape[0]

    def body(x_ref, w1_ref, gq_ref, gkv_ref, wq_ref, wkv_ref, pos_ref, rr_ref,
             xq_ref, xkv_ref, ga_ref, u_ref, gb_ref, qn_ref, kvn_ref, qcat_ref, kcat_ref, v_ref,
             c_ref, sa_ref, sb_ref):
        ang = pos_ref[...].astype(F32) * rr_ref[0:1, :]
        cos, sin = jnp.cos(ang), jnp.sin(ang)
        c, sa, sb = cos * rr_ref[1:2, :], sin * rr_ref[2:3, :], sin * rr_ref[3:4, :]
        c_ref[...] = c
        sa_ref[...] = sa
        sb_ref[...] = sb
        h = _dot_nt(x_ref[...].astype(BF16), w1_ref[...])
        xq = h[:, C_XQ:C_XKV]
        xkv = h[:, C_XKV:C_KR]
        xq_ref[...] = xq
        xkv_ref[...] = xkv
        ga_ref[...] = h[:, C_GA:C_U]
        u_ref[...] = h[:, C_U:C_GB]
        gb_ref[...] = h[:, C_GB:IN_WP]
        qn =(xq * lax.rsqrt(jnp.mean(xq * xq, axis=-1, keepdims=True) + RMS_EPS) * gq_ref[...]).astype(BF16)
        kvn = (xkv * lax.rsqrt(jnp.mean(xkv * xkv, axis=-1, keepdims=True) + RMS_EPS) * gkv_ref[...]).astype(BF16)
        qn_ref[...] = qn
        kvn_ref[...] = kvn
        q = _dot_nt(qn, wq_ref[...])
        kv = _dot_nt(kvn, wkv_ref[...])
        kr = _rope_fwd(h[:, C_KR:C_GA], c, sa, sb).astype(BF16)
        for hd in range(HEADS):
            lo = hd * HEAD_PAD
            qcat_ref[:, lo:lo + NOPE] = q[:, lo:lo + NOPE].astype(BF16)
            qcat_ref[:, lo + NOPE:lo + HEAD_PAD] = _rope_fwd(q[:, lo + NOPE:lo + HEAD_PAD], c, sa, sb).astype(BF16)
            kcat_ref[:, lo:lo + NOPE] = kv[:, hd * NOPE:(hd + 1) * NOPE].astype(BF16)
            kcat_ref[:, lo + NOPE:lo + HEAD_PAD] = kr
        v_ref[...] = kv[:, MLA_W:].astype(BF16)

    def tile(w):
        return pl.BlockSpec((tm, w), lambda i: (i, 0))

    def whole(a):
        return pl.BlockSpec(a.shape, lambda i: (0, 0))

    outs = [(Q_LORA, F32), (KV_LORA, F32), (MLA_W, F32), (POOL_W, F32), (POOL_W, F32),
            (Q_LORA, BF16), (KV_LORA, BF16), (HEADS * HEAD_PAD, BF16), (HEADS * HEAD_PAD, BF16), (MLA_W, BF16),
            (128, F32), (128, F32), (128, F32)]
    return pl.pallas_call(
        body, name="fwd_in", grid=(tt // tm,),
        in_specs=[tile(D_MODEL), whole(w1), whole(gq), whole(gkv), whole(wq), whole(wkv), tile(1), whole(rope_rows)],
        out_specs=[tile(w) for w, _ in outs],
        out_shape=[jax.ShapeDtypeStruct((tt, w), dt) for w, dt in outs],
        compiler_params=_params(48, ("arbitrary",)))(x2, w1, gq, gkv, wq, wkv, pos, rope_rows)


def _attn_fwd(qcat, kcat, v, pos_col, pos_blk, qmax, kmin, tq, tk):
    bb, ss, _ = qcat.shape
    nq, nk = ss // tq, ss // tk
    lanes = 128
    hps = ATTN_FWD_HEADS_PER_STEP

    def body(qmax_ref, kmin_ref, q_ref, k_ref, v_ref, pc_ref, pb_ref, o_ref, lse_ref, m_s, acc_s):
        b, qi = pl.program_id(0), pl.program_id(2)
        m_s[...] = jnp.full(m_s.shape, NEG, F32)
        acc_s[...] = jnp.zeros(acc_s.shape, F32)
        pq = pc_ref[...]
        reach = qmax_ref[b * nq + qi]
        ones = jnp.ones((tk, lanes), BF16)

        def step(ki, carry):
            @pl.when(reach >= kmin_ref[b * nk + ki])
            def _():
                rows = pl.ds(pl.multiple_of(ki * tk, tk), tk)
                mask = pq >= pb_ref[ki]
                for hd in range(hps):
                    qk = slice(hd * HEAD_PAD, (hd + 1) * HEAD_PAD)
                    s = _dot_nt(q_ref[:, qk], k_ref[rows, qk]) * SCALE_LOG2
                    s = jnp.where(mask, s, NEG)
                    m_prev = m_s[hd]
                    m_new = jnp.maximum(m_prev, jnp.max(s, axis=-1, keepdims=True))
                    p = jnp.exp2(s - jnp.tile(m_new, (1, tk // lanes)))
                    a = jnp.exp2(m_prev - m_new)
                    vv = jnp.concatenate([v_ref[rows, hd * V_DIM:(hd + 1) * V_DIM], ones], axis=1)
                    acc_s[hd] = jnp.tile(a, (1, 2)) * acc_s[hd] + _dot(p.astype(BF16), vv)
                    m_s[hd] = m_new
            return carry

        lax.fori_loop(0, nk, step, 0)
        for hd in range(hps):
            acc = acc_s[hd]
            l = acc[:, V_DIM:]
            o_ref[:, hd * V_DIM:(hd + 1) * V_DIM] = acc[:, :V_DIM] / l
            lse_ref[hd] = (m_s[hd] + jnp.log2(l)).T[0:1, :]

    grid_spec = pltpu.PrefetchScalarGridSpec(
        num_scalar_prefetch=2, grid=(bb, HEADS // hps, nq),
        in_specs=[
            pl.BlockSpec((None, tq, hps * HEAD_PAD), lambda b, h, i, *_: (b, i, h)),
            pl.BlockSpec((None, ss, hps * HEAD_PAD), lambda b, h, i, *_: (b, 0, h)),
            pl.BlockSpec((None, ss, hps * V_DIM), lambda b, h, i, *_: (b, 0, h)),
            pl.BlockSpec((None, tq, 1), lambda b, h, i, *_: (b, i, 0)),
            pl.BlockSpec((None, nk, 1, tk), lambda b, h, i, *_: (b, 0, 0, 0)),
        ],
        out_specs=[
            pl.BlockSpec((None, tq, hps * V_DIM), lambda b, h, i, *_: (b, i, h)),
            pl.BlockSpec((None, hps, 1, tq), lambda b, h, i, *_: (b, h, 0, i)),
        ],
        scratch_shapes=[pltpu.VMEM((hps, tq, lanes), F32), pltpu.VMEM((hps, tq, 2 * V_DIM), F32)])
    return pl.pallas_call(
        body, name="attn_fwd", grid_spec=grid_spec,
        out_shape=[jax.ShapeDtypeStruct((bb, ss, MLA_W), F32), jax.ShapeDtypeStruct((bb, HEADS, 1, ss), F32)],
        compiler_params=_params(48, ("arbitrary", "arbitrary", "arbitrary")))(qmax, kmin, qcat, kcat, v, pos_col, pos_blk)


def _post(x, tgt, o, ga, u, gb, w_out, pool_wb, pool_scale, ln_g, ln_b, ts):
    bb, ss, _ = x.shape
    nt = ss // ts
    hb = ts // POOL_HALO

    def body(x_ref, tgt_ref, o_ref, ga_ref, u_ref, uh_ref, gb_ref, wo_ref, pw_ref, ps_ref, lg_ref, lb_ref,
             dz_ref, ycat_ref, do_ref, dga_ref, dgb_ref, dpc_ref, dl_ref, loss_ref, glg_ref, glb_ref, gps_ref, gpw_ref):
        step = pl.program_id(0)
        j = step % nt

        @pl.when(step == 0)
        def _():
            for r in (loss_ref, glg_ref, glb_ref, gps_ref, gpw_ref):
                r[...] = jnp.zeros(r.shape, F32)

        o, ga, u, gb = o_ref[...], ga_ref[...], u_ref[...], gb_ref[...]
        sa, dsa = _silu_parts(ga)
        sb, dsb = _silu_parts(gb)
        ext = jnp.concatenate([jnp.where(j > 0, uh_ref[...], 0.0), u], axis=0)
        s2 = ext + pltpu.roll(ext, 1, 0)
        s4 = s2 + pltpu.roll(s2, 2, 0)
        s8 = s4 + pltpu.roll(s4, 4, 0)
        s16 = s8 + pltpu.roll(s8, 8, 0)
        cnt = _pool_cnt(j * ts, ts)
        pooled = (_pick_groups(s2, s4, s8, s16, POOL_HALO, POOL_HALO + ts) / cnt - u).astype(BF16)
        mixed = jnp.concatenate(
            [_dot(pooled[:, g * POOL_C:(g + 1) * POOL_C], pw_ref[g]) for g in range(POOL_G)], axis=1)
        ps = ps_ref[...]
        scaled = mixed * ps
        ycat = jnp.concatenate([o * sa, scaled * sb], axis=1).astype(BF16)
        ycat_ref[...] = ycat
        z = ALPHA * x_ref[...] + _dot(ycat, wo_ref[...])
        mu = jnp.mean(z, axis=-1, keepdims=True)
        zc = z - mu
        rstd = lax.rsqrt(jnp.mean(zc * zc, axis=-1, keepdims=True) + LN_EPS)
        xhat = zc * rstd
        lg = lg_ref[...]
        diff = xhat * lg + lb_ref[...] - tgt_ref[...]
        loss_ref[...] += jnp.sum(diff * diff) * (0.5 / D_MODEL)
        dy = diff * (1.0 / D_MODEL)
        glb_ref[...] += jnp.sum(dy, axis=0, keepdims=True)
        glg_ref[...] += jnp.sum(dy * xhat, axis=0, keepdims=True)
        dxh = dy * lg
        dz = rstd * (dxh - jnp.mean(dxh, axis=-1, keepdims=True) - xhat * jnp.mean(dxh * xhat, axis=-1, keepdims=True))
        dz_ref[...] = dz
        dycat = _dot_nt(dz.astype(BF16), wo_ref[...])
        dya, dyb = dycat[:, :MLA_W], dycat[:, MLA_W:]
        do = dya * sa
        do_ref[...] = do.astype(BF16)
        doo = do * o
        for hd in range(HEADS):
            dl_ref[hd] = jnp.sum(doo[:, hd * V_DIM:(hd + 1) * V_DIM].T, axis=0, keepdims=True)
        dga_ref[...] = (dya * o * dsa).astype(BF16)
        dgb_ref[...] = (dyb * scaled * dsb).astype(BF16)
        dscaled = dyb * sb
        gps_ref[...] += jnp.sum(dscaled * mixed, axis=0, keepdims=True)
        dmixed = (dscaled * ps).astype(BF16)
        dpooled = []
        for g in range(POOL_G):
            cols = slice(g * POOL_C, (g + 1) * POOL_C)
            gpw_ref[g] += _dot_tn(pooled[:, cols], dmixed[:, cols])
            dpooled.append(_dot_nt(dmixed[:, cols], pw_ref[g]))
        dpc_ref[...] = jnp.concatenate(dpooled, axis=1) / cnt

    def tile(w):
        return pl.BlockSpec((None, ts, w), lambda i: (i // nt, i % nt, 0))

    def whole(a):
        nd = a.ndim
        return pl.BlockSpec(a.shape, lambda i: (0,) * nd)

    halo = pl.BlockSpec((None, POOL_HALO, POOL_W), lambda i: (i // nt, jnp.maximum((i % nt) * hb - 1, 0), 0))
    acc_shapes = [(8, 128), (1, D_MODEL), (1, D_MODEL), (1, POOL_W), (POOL_G, POOL_C, POOL_C)]
    tile_outs = [(D_MODEL, F32), (D_MODEL, BF16), (MLA_W, BF16), (MLA_W, BF16), (POOL_W, BF16), (POOL_W, F32)]
    return pl.pallas_call(
        body, name="post", grid=(bb * nt,),
        in_specs=[tile(D_MODEL), tile(D_MODEL), tile(MLA_W), tile(MLA_W), tile(POOL_W), halo, tile(POOL_W),
                  whole(w_out), whole(pool_wb), whole(pool_scale), whole(ln_g), whole(ln_b)],
        out_specs=[tile(w) for w, _ in tile_outs]
        + [pl.BlockSpec((None, HEADS, 1, ts), lambda i: (i // nt, 0, 0, i % nt))]
        + [pl.BlockSpec(s, lambda i, n=len(s): (0,) * n) for s in acc_shapes],
        out_shape=[jax.ShapeDtypeStruct((bb, ss, w), dt) for w, dt in tile_outs]
        + [jax.ShapeDtypeStruct((bb, HEADS, 1, ss), F32)]
        + [jax.ShapeDtypeStruct(s, F32) for s in acc_shapes],
        compiler_params=_params(48, ("arbitrary",)))(x, tgt, o, ga, u, u, gb, w_out, pool_wb, pool_scale, ln_g, ln_b)


def _attn_bwd(qcat, kcat, v, do, lse, dl, pos_col, pos_blk, qmax, kmin, tq, tk):
    bb, ss, _ = qcat.shape
    nq, nk = ss // tq, ss // tk
    hps = ATTN_BWD_HEADS_PER_STEP

    def body(qmax_ref, kmin_ref, q_ref, k_ref, v_ref, do_ref, lse_ref, dl_ref, pc_ref, pb_ref,
             dq_ref, dk_ref, dv_ref, dqt_s, dk_s, dv_s):
        b = pl.program_id(0)
        dqt_s[...] = jnp.zeros(dqt_s.shape, F32)

        def kv_step(ki, carry):
            krows = pl.ds(pl.multiple_of(ki * tk, tk), tk)
            kts = [k_ref[krows, hd * HEAD_PAD:(hd + 1) * HEAD_PAD].T for hd in range(hps)]
            pk = pc_ref[krows, :]
            first = kmin_ref[b * nk + ki]
            dk_s[...] = jnp.zeros(dk_s.shape, F32)
            dv_s[...] = jnp.zeros(dv_s.shape, F32)

            def q_step(qi, c2):
                @pl.when(qmax_ref[b * nq + qi] >= first)
                def _():
                    qrows = pl.ds(pl.multiple_of(qi * tq, tq), tq)
                    mask = pb_ref[qi] >= pk
                    for hd in range(hps):
                        qk = slice(hd * HEAD_PAD, (hd + 1) * HEAD_PAD)
                        vs = slice(hd * V_DIM, (hd + 1) * V_DIM)
                        q = q_ref[qrows, qk]
                        dd = do_ref[qrows, vs]
                        st = _dot_nt(k_ref[krows, qk], q) * SCALE_LOG2
                        st = jnp.where(mask, st, NEG)
                        pt = jnp.exp2(st - lse_ref[hd, qi])
                        dv_s[:, vs] += _dot(pt.astype(BF16), dd)
                        dpt = _dot_nt(v_ref[krows, vs], dd)
                        dst = (pt * (dpt - dl_ref[hd, qi]) * SCALE).astype(BF16)
                        dk_s[:, qk] += _dot(dst, q)
                        dqt_s[hd, qi] += _dot(kts[hd], dst)
                return c2

            lax.fori_loop(0, nq, q_step, 0)
            dk_ref[krows, :] = dk_s[...]
            dv_ref[krows, :] = dv_s[...].astype(BF16)
            return carry

        lax.fori_loop(0, nk, kv_step, 0)
        for hd in range(hps):
            for qi in range(nq):
                dq_ref[qi * tq:(qi + 1) * tq, hd * HEAD_PAD:(hd + 1) * HEAD_PAD] = dqt_s[hd, qi].T

    def per_head(w):
        return pl.BlockSpec((None, ss, hps * w), lambda b, h, *_: (b, 0, h))

    def rows_of_head():
        return pl.BlockSpec((None, hps, nq, 1, tq), lambda b, h, *_: (b, h, 0, 0, 0))

    grid_spec = pltpu.PrefetchScalarGridSpec(
        num_scalar_prefetch=2, grid=(bb, HEADS // hps),
        in_specs=[per_head(HEAD_PAD), per_head(HEAD_PAD), per_head(V_DIM), per_head(V_DIM),
                  rows_of_head(), rows_of_head(),
                  pl.BlockSpec((None, ss, 1), lambda b, h, *_: (b, 0, 0)),
                  pl.BlockSpec((None, nq, 1, tq), lambda b, h, *_: (b, 0, 0, 0))],
        out_specs=[per_head(HEAD_PAD), per_head(HEAD_PAD), per_head(V_DIM)],
        scratch_shapes=[pltpu.VMEM((hps, nq, HEAD_PAD, tq), F32), pltpu.VMEM((tk, hps * HEAD_PAD), F32),
                        pltpu.VMEM((tk, hps * V_DIM), F32)])
    return pl.pallas_call(
        body, name="attn_bwd", grid_spec=grid_spec,
        out_shape=[jax.ShapeDtypeStruct((bb, ss, HEADS * HEAD_PAD), F32),
                   jax.ShapeDtypeStruct((bb, ss, HEADS * HEAD_PAD), F32),
                   jax.ShapeDtypeStruct((bb, ss, MLA_W), BF16)],
        compiler_params=_params(56, ("arbitrary", "arbitrary")))(qmax, kmin, qcat, kcat, v, do, lse, dl, pos_col, pos_blk)


def _bwd_mid(dqcat, dkcat, dv, xq, xkv, rc, rsa, rsb, wq, wkv, gq, gkv, dga, dgb, dpc, dz, w1, ts):
    bb, ss, _ = dz.shape
    nt = ss // ts
    hb = ts // POOL_HALO

    def body(dq_ref, dk_ref, dv_ref, xq_ref, xkv_ref, c_ref, sa_ref, sb_ref, wq_ref, wkv_ref, gq_ref, gkv_ref,
             dga_ref, dgb_ref, dpc_ref, dph_ref, dz_ref, w1_ref,
             gx_ref, dh_ref, dqp_ref, dkv_ref, ggq_ref, ggkv_ref):
        step = pl.program_id(0)
        j = step % nt

        @pl.when(step == 0)
        def _():
            ggq_ref[...] = jnp.zeros(ggq_ref.shape, F32)
            ggkv_ref[...] = jnp.zeros(ggkv_ref.shape, F32)

        c, sa, sb = c_ref[...], sa_ref[...], sb_ref[...]
        dq, dk = dq_ref[...], dk_ref[...]
        dkr = jnp.zeros((ts, HEAD_PAD - NOPE), F32)
        for hd in range(HEADS):
            lo = hd * HEAD_PAD
            dqp_ref[:, lo:lo + NOPE] = dq[:, lo:lo + NOPE].astype(BF16)
            dqp_ref[:, lo + NOPE:lo + HEAD_PAD] = _rope_bwd(dq[:, lo + NOPE:lo + HEAD_PAD], c, sa, sb).astype(BF16)
            dkv_ref[:, hd * NOPE:(hd + 1) * NOPE] = dk[:, lo:lo + NOPE].astype(BF16)
            dkr = dkr + dk[:, lo + NOPE:lo + HEAD_PAD]
        dkv_ref[:, MLA_W:] = dv_ref[...]
        dh_ref[:, C_KR:C_GA] = _rope_bwd(dkr, c, sa, sb).astype(BF16)

        def rms_bwd(x, g, dn):
            inv = lax.rsqrt(jnp.mean(x * x, axis=-1, keepdims=True) + RMS_EPS)
            xh = x * inv
            dxh = dn * g
            return inv * (dxh - xh * jnp.mean(dxh * xh, axis=-1, keepdims=True)), jnp.sum(dn * xh, axis=0, keepdims=True)

        dxq, ggq = rms_bwd(xq_ref[...], gq_ref[...], _dot(dqp_ref[...], wq_ref[...]))
        dxkv, ggkv = rms_bwd(xkv_ref[...], gkv_ref[...], _dot(dkv_ref[...], wkv_ref[...]))
        ggq_ref[...] += ggq
        ggkv_ref[...] += ggkv
        dh_ref[:, C_XQ:C_XKV] = dxq.astype(BF16)
        dh_ref[:, C_XKV:C_KR] = dxkv.astype(BF16)
        dh_ref[:, C_GA:C_U] = dga_ref[...]
        dh_ref[:, C_GB:IN_WP] = dgb_ref[...]
        dpc = dpc_ref[...]
        n = ts + POOL_HALO
        ext = jnp.concatenate([dpc, jnp.where(j < nt - 1, dph_ref[...], 0.0)], axis=0)
        r2 = ext + pltpu.roll(ext, n - 1, 0)
        r4 = r2 + pltpu.roll(r2, n - 2, 0)
        r8 = r4 + pltpu.roll(r4, n - 4, 0)
        r16 = r8 + pltpu.roll(r8, n - 8, 0)
        du = _pick_groups(r2, r4, r8, r16, 0, ts) - dpc * _pool_cnt(j * ts, ts)
        dh_ref[:, C_U:C_GB] = du.astype(BF16)
        gx_ref[...] = ALPHA * dz_ref[...] + _dot(dh_ref[...], w1_ref[...])

    def tile(w):
        return pl.BlockSpec((None, ts, w), lambda i: (i // nt, i % nt, 0))

    def whole(a):
        return pl.BlockSpec(a.shape, lambda i: (0, 0))

    halo = pl.BlockSpec((None, POOL_HALO, POOL_W),
                        lambda i: (i // nt, jnp.minimum((i % nt + 1) * hb, ss // POOL_HALO - 1), 0))
    return pl.pallas_call(
        body, name="bwd_mid", grid=(bb * nt,),
        in_specs=[tile(HEADS * HEAD_PAD), tile(HEADS * HEAD_PAD), tile(MLA_W), tile(Q_LORA), tile(KV_LORA),
                  tile(128), tile(128), tile(128), whole(wq), whole(wkv), whole(gq), whole(gkv),
                  tile(MLA_W), tile(POOL_W), tile(POOL_W), halo, tile(D_MODEL), whole(w1)],
        out_specs=[tile(D_MODEL), tile(IN_WP), tile(HEADS * HEAD_PAD), tile(2 * MLA_W),
                   pl.BlockSpec((1, Q_LORA), lambda i: (0, 0)), pl.BlockSpec((1, KV_LORA), lambda i: (0, 0))],
        out_shape=[jax.ShapeDtypeStruct((bb, ss, D_MODEL), F32), jax.ShapeDtypeStruct((bb, ss, IN_WP), BF16),
                   jax.ShapeDtypeStruct((bb, ss, HEADS * HEAD_PAD), BF16), jax.ShapeDtypeStruct((bb, ss, 2 * MLA_W), BF16),
                   jax.ShapeDtypeStruct((1, Q_LORA), F32), jax.ShapeDtypeStruct((1, KV_LORA), F32)],
        compiler_params=_params(48, ("arbitrary",)))(
            dqcat, dkcat, dv, xq, xkv, rc, rsa, rsb, wq, wkv, gq, gkv, dga, dgb, dpc, dpc, dz, w1)


def _grad_w(a, b, bt, name):
    tt, m = a.shape
    n = b.shape[1]

    def body(a_ref, b_ref, out_ref):
        @pl.when(pl.program_id(0) == 0)
        def _():
            out_ref[...] = jnp.zeros(out_ref.shape, F32)

        out_ref[...] += _dot_tn(a_ref[...].astype(BF16), b_ref[...].astype(BF16))

    return pl.pallas_call(
        body, name=name, grid=(tt // bt,),
        in_specs=[pl.BlockSpec((bt, m), lambda i: (i, 0)), pl.BlockSpec((bt, n), lambda i: (i, 0))],
        out_specs=pl.BlockSpec((m, n), lambda i: (0, 0)),
        out_shape=jax.ShapeDtypeStruct((m, n), F32),
        compiler_params=_params(48, ("arbitrary",)))(a, b)


def _adamw(triples):
    n = len(triples)

    def body(*refs):
        ins, outs = refs[:4 * n], refs[4 * n:]
        for i in range(n):
            w, g, m, v = (r[...] for r in ins[4 * i:4 * i + 4])
            m = ADAM_B1 * m + (1.0 - ADAM_B1) * g
            v = ADAM_B2 * v + (1.0 - ADAM_B2) * jnp.square(g)
            m_hat = m / (1.0 - ADAM_B1 ** ADAM_STEP)
            v_hat = v / (1.0 - ADAM_B2 ** ADAM_STEP)
            outs[3 * i][...] = -ADAM_LR * (m_hat / (jnp.sqrt(v_hat) + ADAM_EPS) + ADAM_WD * w)
            outs[3 * i + 1][...] = m
            outs[3 * i + 2][...] = v

    flat = [a for t in triples for a in t]
    vmem = pl.BlockSpec(memory_space=pltpu.VMEM)
    res = pl.pallas_call(
        body, name="adamw", in_specs=[vmem] * len(flat), out_specs=[vmem] * (3 * n),
        out_shape=[jax.ShapeDtypeStruct(t[0].shape, F32) for t in triples for _ in range(3)],
        compiler_params=_params(48))(*flat)
    return [tuple(res[3 * i:3 * i + 3]) for i in range(n)]


def _shard_slab(w_in, w_uq, w_ukv, w_out):
    mixed = jnp.concatenate(
        [_pad_rows(w_uq.T, R_MIX), w_ukv.T, jnp.zeros((R_MIX, 1024 - Q_LORA - KV_LORA), F32)], axis=1)
    return jnp.concatenate([w_out, mixed, w_in.T, jnp.zeros((SLAB_ROWS - O_IN - R_IN, 1024), F32)], axis=0)


def _unpack_weights(slabs):
    w_out = slabs[:, :R_OUT].reshape(D_MODEL, D_MODEL)
    uq = slabs[:, O_MIX:O_MIX + R_UQ, :Q_LORA].reshape(HEADS, QK_DIM, Q_LORA)
    wqt = jnp.pad(uq, ((0, 0), (0, HEAD_PAD - QK_DIM), (0, 0))).reshape(HEADS * HEAD_PAD, Q_LORA)
    ukv = slabs[:, O_MIX:O_MIX + R_MIX, Q_LORA:Q_LORA + KV_LORA].reshape(HEADS, 2, NOPE, KV_LORA)
    wkvt = ukv.transpose(1, 0, 2, 3).reshape(2 * MLA_W, KV_LORA)
    raw = slabs[:, O_IN:O_IN + R_IN].reshape(IN_W, D_MODEL)
    w1t = jnp.concatenate([raw[:768 + ROPE], jnp.zeros((128 - ROPE, D_MODEL), BF16), raw[768 + ROPE:]], axis=0)
    return w1t, wqt, wkvt, w_out


def _grad_slab(g_w1t, g_wqt, g_wkvt, g_wout):
    uq = g_wqt.reshape(HEADS, HEAD_PAD, Q_LORA)[:, :QK_DIM].reshape(N_DEV, R_UQ, Q_LORA)
    uq = jnp.pad(uq, ((0, 0), (0, R_MIX - R_UQ), (0, 0)))
    ukv = g_wkvt.reshape(2, HEADS, NOPE, KV_LORA).transpose(1, 0, 2, 3).reshape(N_DEV, R_MIX, KV_LORA)
    mixed = jnp.concatenate([uq, ukv, jnp.zeros((N_DEV, R_MIX, 1024 - Q_LORA - KV_LORA), F32)], axis=2)
    g_in = jnp.concatenate([g_w1t[:768 + ROPE], g_w1t[C_GA:]], axis=0).reshape(N_DEV, R_IN, D_MODEL)
    parts = [g_wout.reshape(N_DEV, R_OUT, D_MODEL), mixed, g_in,
             jnp.zeros((N_DEV, SLAB_ROWS - O_IN - R_IN, D_MODEL), F32)]
    return jnp.concatenate(parts, axis=1)


def _rope_rows():
    half = ROPE // 2
    inv_freq = ROPE_THETA ** (-jnp.arange(half, dtype=F32) / half)
    zero, one = jnp.zeros((half,), F32), jnp.ones((half,), F32)
    rows = [jnp.concatenate(r) for r in (
        (inv_freq, inv_freq, zero, zero), (one, one, zero, zero), (-one, zero, zero, zero), (zero, one, zero, zero))]
    return jnp.stack(rows + [jnp.zeros((128,), F32)] * 4)


def _pad_rows(a, rows):
    return jnp.pad(a, ((0, rows - a.shape[0]), (0, 0)))


def kernel(x, positions, w_in, q_norm_g, w_uq, kv_norm_g, w_ukv, pool_w, pool_scale, w_out, ln_g, ln_b, loss_target, m_w_in, m_q_norm_g, m_w_uq, m_kv_norm_g, m_w_ukv, m_pool_w, m_pool_scale, m_w_out, m_ln_g, m_ln_b, v_w_in, v_q_norm_g, v_w_uq, v_kv_norm_g, v_w_ukv, v_pool_w, v_pool_scale, v_w_out, v_ln_g, v_ln_b):
    bb, ss, _ = x.shape
    tt = bb * ss
    tile = min(256, ss)
    atile = min(512, ss)
    nblk = ss // atile

    slab = _shard_slab(w_in, w_uq, w_ukv, w_out).astype(BF16)
    slabs = _all_gather(slab, "gather_weights", reduce=False).reshape(N_DEV, SLAB_ROWS, 1024)
    w1, wq, wkv, wo = _unpack_weights(slabs)

    gq, gkv = q_norm_g.reshape(1, Q_LORA), kv_norm_g.reshape(1, KV_LORA)
    ps = pool_scale.reshape(1, POOL_W)
    pos_col = positions.reshape(bb, ss, 1)
    pos_blk = positions.reshape(bb, nblk, 1, atile)
    qmax = jnp.max(positions.reshape(bb, nblk, atile), axis=-1).reshape(-1)
    kmin = jnp.min(positions.reshape(bb, nblk, atile), axis=-1).reshape(-1)

    x2 = x.reshape(tt, D_MODEL)
    xq, xkv, ga, u, gb, qn, kvn, qcat, kcat, v, rc, rsa, rsb = _fwd_in(
        x2, w1, gq, gkv, wq, wkv, positions.reshape(tt, 1), _rope_rows(), tile)
    as3 = lambda a: a.reshape(bb, ss, a.shape[-1])
    qcat, kcat, v = as3(qcat), as3(kcat), as3(v)
    o, lse = _attn_fwd(qcat, kcat, v, pos_col, pos_blk, qmax, kmin, atile, atile)
    dz, ycat, do, dga, dgb, dpc, dl, loss_p, g_lng, g_lnb, g_ps, g_pw = _post(
        x, loss_target, o, as3(ga), as3(u), as3(gb), wo, pool_w.astype(BF16), ps, ln_g, ln_b, tile)

    rows5 = lambda a: a.reshape(bb, HEADS, nblk, 1, atile)
    dqcat, dkcat, dv = _attn_bwd(qcat, kcat, v, do, rows5(lse), rows5(dl), pos_col, pos_blk, qmax, kmin, atile, atile)
    grad_x, dh, dqp, dkv, g_gq, g_gkv = _bwd_mid(
        dqcat, dkcat, dv, as3(xq), as3(xkv), as3(rc), as3(rsa), as3(rsb), wq, wkv, gq, gkv, dga, dgb, dpc, dz, w1, tile)
    bt = min(512, tt)
    g_w1 = _grad_w(dh.reshape(tt, IN_WP), x2, bt, "grad_w_in")
    g_wq = _grad_w(dqp.reshape(tt, HEADS * HEAD_PAD), qn, bt, "grad_w_uq")
    g_wkv = _grad_w(dkv.reshape(tt, 2 * MLA_W), kvn, bt, "grad_w_ukv")
    g_wo = _grad_w(ycat.reshape(tt, D_MODEL), dz.reshape(tt, D_MODEL), bt, "grad_w_out")

    g_big = _reduce_scatter(_grad_slab(g_w1, g_wq, g_wkv, g_wo), "reduce_scatter_grads")
    small = jnp.concatenate(
        [_pad_rows(g_lng.reshape(8, 128), 8), _pad_rows(g_lnb.reshape(8, 128), 8), _pad_rows(g_gq.reshape(4, 128), 8),
         _pad_rows(g_gkv.reshape(2, 128), 8), _pad_rows(g_ps.reshape(4, 128), 8), g_pw.reshape(POOL_G * POOL_C, 128),
         loss_p], axis=0)
    small = _all_gather(small, "all_reduce_small", reduce=True)
    loss = small[40 + POOL_G * POOL_C, 0]
    grads = {
        "w_in": g_big[O_IN:O_IN + R_IN],
        "q_norm_g": small[16:20].reshape(1, Q_LORA),
        "w_uq": g_big[O_MIX:O_MIX + R_UQ, :Q_LORA],
        "kv_norm_g": small[24:26].reshape(1, KV_LORA),
        "w_ukv": g_big[O_MIX:O_MIX + R_MIX, Q_LORA:Q_LORA + KV_LORA].T,
        "pool_w": small[40:40 + POOL_G * POOL_C],
        "pool_scale": small[32:36].reshape(1, POOL_W),
        "w_out": g_big[:R_OUT],
        "ln_g": small[0:8].reshape(1, D_MODEL),
        "ln_b": small[8:16].reshape(1, D_MODEL),
    }
    transposed = ("w_in", "w_uq")

    names = ["w_in", "q_norm_g", "w_uq", "kv_norm_g", "w_ukv", "pool_w", "pool_scale", "w_out", "ln_g", "ln_b"]
    weights = dict(w_in=w_in, q_norm_g=q_norm_g, w_uq=w_uq, kv_norm_g=kv_norm_g, w_ukv=w_ukv, pool_w=pool_w,
                   pool_scale=pool_scale, w_out=w_out, ln_g=ln_g, ln_b=ln_b)
    moms = dict(w_in=(m_w_in, v_w_in), q_norm_g=(m_q_norm_g, v_q_norm_g), w_uq=(m_w_uq, v_w_uq),
                kv_norm_g=(m_kv_norm_g, v_kv_norm_g), w_ukv=(m_w_ukv, v_w_ukv), pool_w=(m_pool_w, v_pool_w),
                pool_scale=(m_pool_scale, v_pool_scale), w_out=(m_w_out, v_w_out), ln_g=(m_ln_g, v_ln_g),
                ln_b=(m_ln_b, v_ln_b))
    as2 = lambda a, n: a.T if n in transposed else a.reshape(grads[n].shape)
    upd = _adamw([(as2(weights[n], n), grads[n], as2(moms[n][0], n), as2(moms[n][1], n)) for n in names])
    shaped = lambda a, n: a.T if n in transposed else a.reshape(weights[n].shape)
    return (loss, grad_x,
            *[shaped(grads[n], n) for n in names],
            *[shaped(upd[i][0], n) for i, n in enumerate(names)],
            *[shaped(upd[i][1], n) for i, n in enumerate(names)],
            *[shaped(upd[i][2], n) for i, n in enumerate(names)])
```

```python
import functools

import jax
import jax.numpy as jnp
from jax import lax
from jax.experimental import pallas as pl
from jax.experimental.pallas import tpu as pltpu

F32 = jnp.float32
BF16 = jnp.bfloat16
MESH = pl.DeviceIdType.MESH

N_DEV = 8
D_MODEL = 1024
HEADS = 4
NOPE = 128
ROPE = 64
V_DIM = 128
QK_DIM = NOPE + ROPE
HEAD_PAD = 256
Q_LORA = 512
KV_LORA = 256
MLA_W = HEADS * V_DIM
POOL_W = 512
POOL_G = 4
POOL_C = 128
POOL_HALO = 16
IN_W = 2368
IN_WP = 2432
C_XQ, C_XKV, C_KR, C_GA, C_U, C_GB = 0, 512, 768, 896, 1408, 1920
ROPE_THETA = 10000.0
RMS_EPS = 1e-6
LN_EPS = 1e-5
ALPHA = 2.0 ** 0.25
SCALE = QK_DIM ** -0.5
SCALE_LOG2 = SCALE * 1.4426950408889634
NEG = float(jnp.finfo(jnp.float32).min)

ADAM_LR = 0.001
ADAM_B1 = 0.9
ADAM_B2 = 0.999
ADAM_EPS = 1e-08
ADAM_WD = 0.01
ADAM_STEP = 10

R_OUT, R_MIX, R_UQ, R_IN = 128, 128, 96, 296
O_MIX, O_IN = R_OUT, R_OUT + R_MIX
SLAB_ROWS = 560
R_IN_PAD = 304

V7X_VMEM_BYTES = 64 * 1024 * 1024
ATTN_FWD_HEADS_PER_STEP = 4
ATTN_BWD_HEADS_PER_STEP = 2


def _params(vmem_mb, semantics=None):
    assert vmem_mb * 1024 * 1024 < V7X_VMEM_BYTES
    return pltpu.CompilerParams(vmem_limit_bytes=vmem_mb * 1024 * 1024, dimension_semantics=semantics)


def _dot(a, b):
    return jnp.dot(a, b, preferred_element_type=F32)


def _dot_nt(a, b):
    return lax.dot_general(a, b, (((1,), (1,)), ((), ())), preferred_element_type=F32)


def _dot_tn(a, b):
    return lax.dot_general(a, b, (((0,), (0,)), ((), ())), preferred_element_type=F32)


def _rope_fwd(t, c, sa, sb):
    return t * c + pltpu.roll(t, 96, 1) * sa + pltpu.roll(t, 32, 1) * sb


def _rope_bwd(d, c, sa, sb):
    return d * c + pltpu.roll(d * sa, 32, 1) + pltpu.roll(d * sb, 96, 1)


def _silu_parts(g):
    sig = jax.nn.sigmoid(g)
    return g * sig, sig * (1.0 + g * (1.0 - sig))


def _pool_cnt(row0, rows):
    t = row0 + lax.broadcasted_iota(jnp.int32, (rows, POOL_W), 0)
    w = 2 << (lax.broadcasted_iota(jnp.int32, (rows, POOL_W), 1) // POOL_C)
    return jnp.minimum(t + 1, w).astype(F32)


def _pick_groups(s2, s4, s8, s16, lo, hi):
    return jnp.concatenate([s2[lo:hi, 0:128], s4[lo:hi, 128:256], s8[lo:hi, 256:384], s16[lo:hi, 384:512]], axis=1)


AG_SEMS = [pltpu.SemaphoreType.DMA((7,)), pltpu.SemaphoreType.DMA((7,)), pltpu.SemaphoreType.DMA]


def _ag_run(x_ref, out_ref, send_sems, recv_sems, local_sem):
    m_per = x_ref.shape[0]
    x, y, c = lax.axis_index("x"), lax.axis_index("y"), lax.axis_index("c")
    me, sibling = (x, y, c), (x, y, 1 - c)
    chips = [(1 - x, y), (x, 1 - y), (1 - x, 1 - y)]

    def rows(px, py, pc):
        return out_ref.at[pl.ds((4 * px + 2 * py + pc) * m_per, m_per), :]

    def copy(k, block, to, src=None):
        return pltpu.make_async_remote_copy(
            src_ref=rows(*block) if src is None else src, dst_ref=rows(*block),
            send_sem=send_sems.at[k], recv_sem=recv_sems.at[k], device_id=to, device_id_type=MESH)

    mine = pltpu.make_async_copy(x_ref, rows(*me), local_sem)
    mine.start()
    first = [copy(0, me, sibling, src=x_ref)]
    first += [copy(1 + j, me, (*chip, c), src=x_ref) for j, chip in enumerate(chips)]
    for cp in first:
        cp.start()
    passed = [copy(4 + j, (*chip, c), sibling) for j, chip in enumerate(chips)]
    for j, chip in enumerate(chips):
        copy(1 + j, (*chip, c), me).wait_recv()
        passed[j].start()
    copy(0, sibling, me).wait_recv()
    for j, chip in enumerate(chips):
        copy(4 + j, (*chip, 1 - c), me).wait_recv()
    for cp in first + passed:
        cp.wait_send()
    mine.wait()


def _all_gather(shard, name):
    m_per, n = shard.shape
    vmem = pl.BlockSpec(memory_space=pltpu.VMEM)
    return pl.pallas_call(
        _ag_run, name=name, out_shape=jax.ShapeDtypeStruct((N_DEV * m_per, n), shard.dtype),
        in_specs=[vmem], out_specs=vmem, scratch_shapes=AG_SEMS, compiler_params=_params(32))(shard)


def _tail_exchange(small, gslab, name):
    m_per, n = small.shape
    _, rr, ww = gslab.shape

    def body(x_ref, g_hbm, sum_ref, rs_out, gathered, send_sems, recv_sems, local_sem, *rs_scratch):
        start, forward, finish = _rs_phases(g_hbm, rs_out, *rs_scratch)
        start()
        _ag_run(x_ref, gathered, send_sems, recv_sems, local_sem)
        forward()
        acc = gathered[pl.ds(0, m_per), :]
        for d in range(1, N_DEV):
            acc = acc + gathered[pl.ds(d * m_per, m_per), :]
        sum_ref[...] = acc
        finish()

    vmem = pl.BlockSpec(memory_space=pltpu.VMEM)
    return pl.pallas_call(
        body, name=name, in_specs=[vmem, pl.BlockSpec(memory_space=pl.ANY)], out_specs=[vmem, vmem],
        out_shape=[jax.ShapeDtypeStruct((m_per, n), F32), jax.ShapeDtypeStruct((rr, ww), F32)],
        scratch_shapes=[pltpu.VMEM((N_DEV * m_per, n), F32)] + AG_SEMS + _rs_scratch(rr, ww),
        compiler_params=_params(32))(small, gslab)


def _rs_scratch(rr, ww):
    return [pltpu.VMEM((4, rr, ww), F32), pltpu.VMEM((4, rr, ww), F32),
            pltpu.VMEM((3, rr, ww), BF16), pltpu.VMEM((3, rr, ww), BF16),
            pltpu.SemaphoreType.DMA((4,)), pltpu.SemaphoreType.DMA((4,)), pltpu.SemaphoreType.DMA((4,)),
            pltpu.SemaphoreType.DMA((3,)), pltpu.SemaphoreType.DMA((3,))]


def _rs_phases(g_hbm, out_ref, own_ref, recv1_ref, sendb_ref, recv2_ref, ld_sems, s1_send, s1_recv, s2_send, s2_recv):
    _, rr, ww = g_hbm.shape
    chunk = next(c for c in (128, 80, 64, 48, 32, 16) if rr % c == 0)
    x, y, c = lax.axis_index("x"), lax.axis_index("y"), lax.axis_index("c")
    chips = [(1 - x, y), (x, 1 - y), (1 - x, 1 - y)]

    def loads():
        return [pltpu.make_async_copy(g_hbm.at[2 * k + c], own_ref.at[k], ld_sems.at[k]) for k in range(4)]

    def stage1():
        return [pltpu.make_async_remote_copy(
            src_ref=g_hbm.at[2 * k + (1 - c)], dst_ref=recv1_ref.at[k], send_sem=s1_send.at[k],
            recv_sem=s1_recv.at[k], device_id=(x, y, 1 - c), device_id_type=MESH) for k in range(4)]

    def stage2():
        return [pltpu.make_async_remote_copy(
            src_ref=sendb_ref.at[r], dst_ref=recv2_ref.at[r], send_sem=s2_send.at[r],
            recv_sem=s2_recv.at[r], device_id=(cx, cy, c), device_id_type=MESH) for r, (cx, cy) in enumerate(chips)]

    def start():
        for cp in loads() + stage1():
            cp.start()

    def forward():
        for cp in loads():
            cp.wait()
        for cp in stage1():
            cp.wait_recv()
        sends = stage2()
        for r, (cx, cy) in enumerate(chips):
            kk = 2 * cx + cy

            def pack(i, carry, r=r, kk=kk):
                rows = pl.ds(pl.multiple_of(i * chunk, chunk), chunk)
                sendb_ref[r, rows, :] = (own_ref[kk, rows, :] + recv1_ref[kk, rows, :]).astype(BF16)
                return carry

            lax.fori_loop(0, rr // chunk, pack, 0)
            sends[r].start()

    def finish():
        for cp in stage2():
            cp.wait_recv()
        mine = 2 * x + y

        def total(i, carry):
            rows = pl.ds(pl.multiple_of(i * chunk, chunk), chunk)
            acc = own_ref[mine, rows, :] + recv1_ref[mine, rows, :]
            for r in range(3):
                acc = acc + recv2_ref[r, rows, :].astype(F32)
            out_ref[rows, :] = acc
            return carry

        lax.fori_loop(0, rr // chunk, total, 0)
        for cp in stage1() + stage2():
            cp.wait_send()

    return start, forward, finish


def _fwd_in(x2, w1, gq, gkv, wq, wkv, pos, rope_rows, tm):
    tt = x2.shape[0]

    def body(x_ref, w1_ref, gq_ref, gkv_ref, wq_ref, wkv_ref, pos_ref, rr_ref,
             xq_ref, xkv_ref, ga_ref, u_ref, gb_ref, qn_ref, kvn_ref, qcat_ref, kcat_ref, v_ref,
             c_ref, sa_ref, sb_ref):
        ang = pos_ref[...].astype(F32) * rr_ref[0:1, :]
        cos, sin = jnp.cos(ang), jnp.sin(ang)
        c, sa, sb = cos * rr_ref[1:2, :], sin * rr_ref[2:3, :], sin * rr_ref[3:4, :]
        c_ref[...] = c
        sa_ref[...] = sa
        sb_ref[...] = sb
        h = _dot_nt(x_ref[...].astype(BF16), w1_ref[...])
        xq = h[:, C_XQ:C_XKV]
        xkv = h[:, C_XKV:C_KR]
        xq_ref[...] = xq
        xkv_ref[...] = xkv
        ga_ref[...] = h[:, C_GA:C_U]
        u_ref[...] = h[:, C_U:C_GB]
        gb_ref[...] = h[:, C_GB:IN_WP]
        qn =(xq * lax.rsqrt(jnp.mean(xq * xq, axis=-1, keepdims=True) + RMS_EPS) * gq_ref[...]).astype(BF16)
        kvn = (xkv * lax.rsqrt(jnp.mean(xkv * xkv, axis=-1, keepdims=True) + RMS_EPS) * gkv_ref[...]).astype(BF16)
        qn_ref[...] = qn
        kvn_ref[...] = kvn
        q = _dot_nt(qn, wq_ref[...])
        kv = _dot_nt(kvn, wkv_ref[...])
        kr = _rope_fwd(h[:, C_KR:C_GA], c, sa, sb).astype(BF16)
        for hd in range(HEADS):
            lo = hd * HEAD_PAD
            qcat_ref[:, lo:lo + NOPE] = q[:, lo:lo + NOPE].astype(BF16)
            qcat_ref[:, lo + NOPE:lo + HEAD_PAD] = _rope_fwd(q[:, lo + NOPE:lo + HEAD_PAD], c, sa, sb).astype(BF16)
            kcat_ref[:, lo:lo + NOPE] = kv[:, hd * NOPE:(hd + 1) * NOPE].astype(BF16)
            kcat_ref[:, lo + NOPE:lo + HEAD_PAD] = kr
        v_ref[...] = kv[:, MLA_W:].astype(BF16)

    def tile(w):
        return pl.BlockSpec((tm, w), lambda i: (i, 0))

    def whole(a):
        return pl.BlockSpec(a.shape, lambda i: (0, 0))

    outs = [(Q_LORA, F32), (KV_LORA, F32), (MLA_W, F32), (POOL_W, F32), (POOL_W, F32),
            (Q_LORA, BF16), (KV_LORA, BF16), (HEADS * HEAD_PAD, BF16), (HEADS * HEAD_PAD, BF16), (MLA_W, BF16),
            (128, F32), (128, F32), (128, F32)]
    return pl.pallas_call(
        body, name="fwd_in", grid=(tt // tm,),
        in_specs=[tile(D_MODEL), whole(w1), whole(gq), whole(gkv), whole(wq), whole(wkv), tile(1), whole(rope_rows)],
        out_specs=[tile(w) for w, _ in outs],
        out_shape=[jax.ShapeDtypeStruct((tt, w), dt) for w, dt in outs],
        compiler_params=_params(48, ("arbitrary",)))(x2, w1, gq, gkv, wq, wkv, pos, rope_rows)


def _attn_fwd(qcat, kcat, v, pos_col, pos_blk, qmax, kmin, tq, tk):
    bb, ss, _ = qcat.shape
    nq, nk = ss // tq, ss // tk
    lanes = 128
    hps = ATTN_FWD_HEADS_PER_STEP

    def body(qmax_ref, kmin_ref, q_ref, k_ref, v_ref, pc_ref, pb_ref, o_ref, lse_ref, m_s, acc_s):
        b, qi = pl.program_id(0), pl.program_id(2)
        m_s[...] = jnp.full(m_s.shape, NEG, F32)
        acc_s[...] = jnp.zeros(acc_s.shape, F32)
        pq = pc_ref[...]
        reach = qmax_ref[b * nq + qi]
        ones = jnp.ones((tk, lanes), BF16)

        def step(ki, carry):
            @pl.when(reach >= kmin_ref[b * nk + ki])
            def _():
                rows = pl.ds(pl.multiple_of(ki * tk, tk), tk)
                mask = pq >= pb_ref[ki]
                for hd in range(hps):
                    qk = slice(hd * HEAD_PAD, (hd + 1) * HEAD_PAD)
                    s = _dot_nt(q_ref[:, qk], k_ref[rows, qk]) * SCALE_LOG2
                    s = jnp.where(mask, s, NEG)
                    m_prev = m_s[hd]
                    m_new = jnp.maximum(m_prev, jnp.max(s, axis=-1, keepdims=True))
                    p = jnp.exp2(s - jnp.tile(m_new, (1, tk // lanes)))
                    a = jnp.exp2(m_prev - m_new)
                    vv = jnp.concatenate([v_ref[rows, hd * V_DIM:(hd + 1) * V_DIM], ones], axis=1)
                    acc_s[hd] = jnp.tile(a, (1, 2)) * acc_s[hd] + _dot(p.astype(BF16), vv)
                    m_s[hd] = m_new
            return carry

        lax.fori_loop(0, nk, step, 0)
        for hd in range(hps):
            acc = acc_s[hd]
            l = acc[:, V_DIM:]
            o_ref[:, hd * V_DIM:(hd + 1) * V_DIM] = acc[:, :V_DIM] / l
            lse_ref[hd] = (m_s[hd] + jnp.log2(l)).T[0:1, :]

    grid_spec = pltpu.PrefetchScalarGridSpec(
        num_scalar_prefetch=2, grid=(bb, HEADS // hps, nq),
        in_specs=[
            pl.BlockSpec((None, tq, hps * HEAD_PAD), lambda b, h, i, *_: (b, i, h)),
            pl.BlockSpec((None, ss, hps * HEAD_PAD), lambda b, h, i, *_: (b, 0, h)),
            pl.BlockSpec((None, ss, hps * V_DIM), lambda b, h, i, *_: (b, 0, h)),
            pl.BlockSpec((None, tq, 1), lambda b, h, i, *_: (b, i, 0)),
            pl.BlockSpec((None, nk, 1, tk), lambda b, h, i, *_: (b, 0, 0, 0)),
        ],
        out_specs=[
            pl.BlockSpec((None, tq, hps * V_DIM), lambda b, h, i, *_: (b, i, h)),
            pl.BlockSpec((None, hps, 1, tq), lambda b, h, i, *_: (b, h, 0, i)),
        ],
        scratch_shapes=[pltpu.VMEM((hps, tq, lanes), F32), pltpu.VMEM((hps, tq, 2 * V_DIM), F32)])
    return pl.pallas_call(
        body, name="attn_fwd", grid_spec=grid_spec,
        out_shape=[jax.ShapeDtypeStruct((bb, ss, MLA_W), F32), jax.ShapeDtypeStruct((bb, HEADS, 1, ss), F32)],
        compiler_params=_params(48, ("arbitrary", "arbitrary", "arbitrary")))(qmax, kmin, qcat, kcat, v, pos_col, pos_blk)


def _post(x, tgt, o, ga, u, gb, w_out, pool_wb, pool_scale, ln_g, ln_b, ts):
    bb, ss, _ = x.shape
    nt = ss // ts
    hb = ts // POOL_HALO

    def body(x_ref, tgt_ref, o_ref, ga_ref, u_ref, uh_ref, gb_ref, wo_ref, pw_ref, ps_ref, lg_ref, lb_ref,
             dz_ref, ycat_ref, do_ref, dga_ref, dgb_ref, dpc_ref, dl_ref, loss_ref, glg_ref, glb_ref, gps_ref, gpw_ref):
        step = pl.program_id(0)
        j = step % nt

        @pl.when(step == 0)
        def _():
            for r in (loss_ref, glg_ref, glb_ref, gps_ref, gpw_ref):
                r[...] = jnp.zeros(r.shape, F32)

        o, ga, u, gb = o_ref[...], ga_ref[...], u_ref[...], gb_ref[...]
        sa, dsa = _silu_parts(ga)
        sb, dsb = _silu_parts(gb)
        ext = jnp.concatenate([jnp.where(j > 0, uh_ref[...], 0.0), u], axis=0)
        s2 = ext + pltpu.roll(ext, 1, 0)
        s4 = s2 + pltpu.roll(s2, 2, 0)
        s8 = s4 + pltpu.roll(s4, 4, 0)
        s16 = s8 + pltpu.roll(s8, 8, 0)
        cnt = _pool_cnt(j * ts, ts)
        pooled = (_pick_groups(s2, s4, s8, s16, POOL_HALO, POOL_HALO + ts) / cnt - u).astype(BF16)
        mixed = jnp.concatenate(
            [_dot(pooled[:, g * POOL_C:(g + 1) * POOL_C], pw_ref[g]) for g in range(POOL_G)], axis=1)
        ps = ps_ref[...]
        scaled = mixed * ps
        ycat = jnp.concatenate([o * sa, scaled * sb], axis=1).astype(BF16)
        ycat_ref[...] = ycat
        z = ALPHA * x_ref[...] + _dot(ycat, wo_ref[...])
        mu = jnp.mean(z, axis=-1, keepdims=True)
        zc = z - mu
        rstd = lax.rsqrt(jnp.mean(zc * zc, axis=-1, keepdims=True) + LN_EPS)
        xhat = zc * rstd
        lg = lg_ref[...]
        diff = xhat * lg + lb_ref[...] - tgt_ref[...]
        loss_ref[...] += jnp.sum(diff * diff) * (0.5 / D_MODEL)
        dy = diff * (1.0 / D_MODEL)
        glb_ref[...] += jnp.sum(dy, axis=0, keepdims=True)
        glg_ref[...] += jnp.sum(dy * xhat, axis=0, keepdims=True)
        dxh = dy * lg
        dz = rstd * (dxh - jnp.mean(dxh, axis=-1, keepdims=True) - xhat * jnp.mean(dxh * xhat, axis=-1, keepdims=True))
        dz_ref[...] = dz
        dycat = _dot_nt(dz.astype(BF16), wo_ref[...])
        dya, dyb = dycat[:, :MLA_W], dycat[:, MLA_W:]
        do = dya * sa
        do_ref[...] = do.astype(BF16)
        doo = do * o
        for hd in range(HEADS):
            dl_ref[hd] = jnp.sum(doo[:, hd * V_DIM:(hd + 1) * V_DIM].T, axis=0, keepdims=True)
        dga_ref[...] = (dya * o * dsa).astype(BF16)
        dgb_ref[...] = (dyb * scaled * dsb).astype(BF16)
        dscaled = dyb * sb
        gps_ref[...] += jnp.sum(dscaled * mixed, axis=0, keepdims=True)
        dmixed = (dscaled * ps).astype(BF16)
        dpooled = []
        for g in range(POOL_G):
            cols = slice(g * POOL_C, (g + 1) * POOL_C)
            gpw_ref[g] += _dot_tn(pooled[:, cols], dmixed[:, cols])
            dpooled.append(_dot_nt(dmixed[:, cols], pw_ref[g]))
        dpc_ref[...] = jnp.concatenate(dpooled, axis=1) / cnt

    def tile(w):
        return pl.BlockSpec((None, ts, w), lambda i: (i // nt, i % nt, 0))

    def whole(a):
        nd = a.ndim
        return pl.BlockSpec(a.shape, lambda i: (0,) * nd)

    halo = pl.BlockSpec((None, POOL_HALO, POOL_W), lambda i: (i // nt, jnp.maximum((i % nt) * hb - 1, 0), 0))
    acc_shapes = [(8, 128), (1, D_MODEL), (1, D_MODEL), (1, POOL_W), (POOL_G, POOL_C, POOL_C)]
    tile_outs = [(D_MODEL, F32), (D_MODEL, BF16), (MLA_W, BF16), (MLA_W, BF16), (POOL_W, BF16), (POOL_W, F32)]
    return pl.pallas_call(
        body, name="post", grid=(bb * nt,),
        in_specs=[tile(D_MODEL), tile(D_MODEL), tile(MLA_W), tile(MLA_W), tile(POOL_W), halo, tile(POOL_W),
                  whole(w_out), whole(pool_wb), whole(pool_scale), whole(ln_g), whole(ln_b)],
        out_specs=[tile(w) for w, _ in tile_outs]
        + [pl.BlockSpec((None, HEADS, 1, ts), lambda i: (i // nt, 0, 0, i % nt))]
        + [pl.BlockSpec(s, lambda i, n=len(s): (0,) * n) for s in acc_shapes],
        out_shape=[jax.ShapeDtypeStruct((bb, ss, w), dt) for w, dt in tile_outs]
        + [jax.ShapeDtypeStruct((bb, HEADS, 1, ss), F32)]
        + [jax.ShapeDtypeStruct(s, F32) for s in acc_shapes],
        compiler_params=_params(48, ("arbitrary",)))(x, tgt, o, ga, u, u, gb, w_out, pool_wb, pool_scale, ln_g, ln_b)


def _attn_bwd(qcat, kcat, v, do, lse, dl, pos_col, pos_blk, qmax, kmin, tq, tk):
    bb, ss, _ = qcat.shape
    nq, nk = ss // tq, ss // tk
    hps = ATTN_BWD_HEADS_PER_STEP

    def body(qmax_ref, kmin_ref, q_ref, k_ref, v_ref, do_ref, lse_ref, dl_ref, pc_ref, pb_ref,
             dq_ref, dk_ref, dv_ref, dqt_s, dk_s, dv_s):
        b = pl.program_id(0)
        dqt_s[...] = jnp.zeros(dqt_s.shape, F32)

        def kv_step(ki, carry):
            krows = pl.ds(pl.multiple_of(ki * tk, tk), tk)
            kts = [k_ref[krows, hd * HEAD_PAD:(hd + 1) * HEAD_PAD].T for hd in range(hps)]
            pk = pc_ref[krows, :]
            first = kmin_ref[b * nk + ki]
            dk_s[...] = jnp.zeros(dk_s.shape, F32)
            dv_s[...] = jnp.zeros(dv_s.shape, F32)

            def q_step(qi, c2):
                @pl.when(qmax_ref[b * nq + qi] >= first)
                def _():
                    qrows = pl.ds(pl.multiple_of(qi * tq, tq), tq)
                    mask = pb_ref[qi] >= pk
                    for hd in range(hps):
                        qk = slice(hd * HEAD_PAD, (hd + 1) * HEAD_PAD)
                        vs = slice(hd * V_DIM, (hd + 1) * V_DIM)
                        q = q_ref[qrows, qk]
                        dd = do_ref[qrows, vs]
                        st = _dot_nt(k_ref[krows, qk], q) * SCALE_LOG2
                        st = jnp.where(mask, st, NEG)
                        pt = jnp.exp2(st - lse_ref[hd, qi])
                        dv_s[:, vs] += _dot(pt.astype(BF16), dd)
                        dpt = _dot_nt(v_ref[krows, vs], dd)
                        dst = (pt * (dpt - dl_ref[hd, qi]) * SCALE).astype(BF16)
                        dk_s[:, qk] += _dot(dst, q)
                        dqt_s[hd, qi] += _dot(kts[hd], dst)
                return c2

            lax.fori_loop(0, nq, q_step, 0)
            dk_ref[krows, :] = dk_s[...]
            dv_ref[krows, :] = dv_s[...].astype(BF16)
            return carry

        lax.fori_loop(0, nk, kv_step, 0)
        for hd in range(hps):
            for qi in range(nq):
                dq_ref[qi * tq:(qi + 1) * tq, hd * HEAD_PAD:(hd + 1) * HEAD_PAD] = dqt_s[hd, qi].T

    def per_head(w):
        return pl.BlockSpec((None, ss, hps * w), lambda b, h, *_: (b, 0, h))

    def rows_of_head():
        return pl.BlockSpec((None, hps, nq, 1, tq), lambda b, h, *_: (b, h, 0, 0, 0))

    grid_spec = pltpu.PrefetchScalarGridSpec(
        num_scalar_prefetch=2, grid=(bb, HEADS // hps),
        in_specs=[per_head(HEAD_PAD), per_head(HEAD_PAD), per_head(V_DIM), per_head(V_DIM),
                  rows_of_head(), rows_of_head(),
                  pl.BlockSpec((None, ss, 1), lambda b, h, *_: (b, 0, 0)),
                  pl.BlockSpec((None, nq, 1, tq), lambda b, h, *_: (b, 0, 0, 0))],
        out_specs=[per_head(HEAD_PAD), per_head(HEAD_PAD), per_head(V_DIM)],
        scratch_shapes=[pltpu.VMEM((hps, nq, HEAD_PAD, tq), F32), pltpu.VMEM((tk, hps * HEAD_PAD), F32),
                        pltpu.VMEM((tk, hps * V_DIM), F32)])
    return pl.pallas_call(
        body, name="attn_bwd", grid_spec=grid_spec,
        out_shape=[jax.ShapeDtypeStruct((bb, ss, HEADS * HEAD_PAD), F32),
                   jax.ShapeDtypeStruct((bb, ss, HEADS * HEAD_PAD), F32),
                   jax.ShapeDtypeStruct((bb, ss, MLA_W), BF16)],
        compiler_params=_params(56, ("arbitrary", "arbitrary")))(qmax, kmin, qcat, kcat, v, do, lse, dl, pos_col, pos_blk)


def _bwd_mid(dqcat, dkcat, dv, xq, xkv, rc, rsa, rsb, wq, wkv, gq, gkv, dga, dgb, dpc, dz, w1, ts):
    bb, ss, _ = dz.shape
    nt = ss // ts
    hb = ts // POOL_HALO

    def body(dq_ref, dk_ref, dv_ref, xq_ref, xkv_ref, c_ref, sa_ref, sb_ref, wq_ref, wkv_ref, gq_ref, gkv_ref,
             dga_ref, dgb_ref, dpc_ref, dph_ref, dz_ref, w1_ref,
             gx_ref, dh_ref, dqp_ref, dkv_ref, ggq_ref, ggkv_ref):
        step = pl.program_id(0)
        j = step % nt

        @pl.when(step == 0)
        def _():
            ggq_ref[...] = jnp.zeros(ggq_ref.shape, F32)
            ggkv_ref[...] = jnp.zeros(ggkv_ref.shape, F32)

        c, sa, sb = c_ref[...], sa_ref[...], sb_ref[...]
        dq, dk = dq_ref[...], dk_ref[...]
        dkr = jnp.zeros((ts, HEAD_PAD - NOPE), F32)
        for hd in range(HEADS):
            lo = hd * HEAD_PAD
            dqp_ref[:, lo:lo + NOPE] = dq[:, lo:lo + NOPE].astype(BF16)
            dqp_ref[:, lo + NOPE:lo + HEAD_PAD] = _rope_bwd(dq[:, lo + NOPE:lo + HEAD_PAD], c, sa, sb).astype(BF16)
            dkv_ref[:, hd * NOPE:(hd + 1) * NOPE] = dk[:, lo:lo + NOPE].astype(BF16)
            dkr = dkr + dk[:, lo + NOPE:lo + HEAD_PAD]
        dkv_ref[:, MLA_W:] = dv_ref[...]
        dh_ref[:, C_KR:C_GA] = _rope_bwd(dkr, c, sa, sb).astype(BF16)

        def rms_bwd(x, g, dn):
            inv = lax.rsqrt(jnp.mean(x * x, axis=-1, keepdims=True) + RMS_EPS)
            xh = x * inv
            dxh = dn * g
            return inv * (dxh - xh * jnp.mean(dxh * xh, axis=-1, keepdims=True)), jnp.sum(dn * xh, axis=0, keepdims=True)

        dxq, ggq = rms_bwd(xq_ref[...], gq_ref[...], _dot(dqp_ref[...], wq_ref[...]))
        dxkv, ggkv = rms_bwd(xkv_ref[...], gkv_ref[...], _dot(dkv_ref[...], wkv_ref[...]))
        ggq_ref[...] += ggq
        ggkv_ref[...] += ggkv
        dh_ref[:, C_XQ:C_XKV] = dxq.astype(BF16)
        dh_ref[:, C_XKV:C_KR] = dxkv.astype(BF16)
        dh_ref[:, C_GA:C_U] = dga_ref[...]
        dh_ref[:, C_GB:IN_WP] = dgb_ref[...]
        dpc = dpc_ref[...]
        n = ts + POOL_HALO
        ext = jnp.concatenate([dpc, jnp.where(j < nt - 1, dph_ref[...], 0.0)], axis=0)
        r2 = ext + pltpu.roll(ext, n - 1, 0)
        r4 = r2 + pltpu.roll(r2, n - 2, 0)
        r8 = r4 + pltpu.roll(r4, n - 4, 0)
        r16 = r8 + pltpu.roll(r8, n - 8, 0)
        du = _pick_groups(r2, r4, r8, r16, 0, ts) - dpc * _pool_cnt(j * ts, ts)
        dh_ref[:, C_U:C_GB] = du.astype(BF16)
        gx_ref[...] = ALPHA * dz_ref[...] + _dot(dh_ref[...], w1_ref[...])

    def tile(w):
        return pl.BlockSpec((None, ts, w), lambda i: (i // nt, i % nt, 0))

    def whole(a):
        return pl.BlockSpec(a.shape, lambda i: (0, 0))

    halo = pl.BlockSpec((None, POOL_HALO, POOL_W),
                        lambda i: (i // nt, jnp.minimum((i % nt + 1) * hb, ss // POOL_HALO - 1), 0))
    return pl.pallas_call(
        body, name="bwd_mid", grid=(bb * nt,),
        in_specs=[tile(HEADS * HEAD_PAD), tile(HEADS * HEAD_PAD), tile(MLA_W), tile(Q_LORA), tile(KV_LORA),
                  tile(128), tile(128), tile(128), whole(wq), whole(wkv), whole(gq), whole(gkv),
                  tile(MLA_W), tile(POOL_W), tile(POOL_W), halo, tile(D_MODEL), whole(w1)],
        out_specs=[tile(D_MODEL), tile(IN_WP), tile(HEADS * HEAD_PAD), tile(2 * MLA_W),
                   pl.BlockSpec((1, Q_LORA), lambda i: (0, 0)), pl.BlockSpec((1, KV_LORA), lambda i: (0, 0))],
        out_shape=[jax.ShapeDtypeStruct((bb, ss, D_MODEL), F32), jax.ShapeDtypeStruct((bb, ss, IN_WP), BF16),
                   jax.ShapeDtypeStruct((bb, ss, HEADS * HEAD_PAD), BF16), jax.ShapeDtypeStruct((bb, ss, 2 * MLA_W), BF16),
                   jax.ShapeDtypeStruct((1, Q_LORA), F32), jax.ShapeDtypeStruct((1, KV_LORA), F32)],
        compiler_params=_params(48, ("arbitrary",)))(
            dqcat, dkcat, dv, xq, xkv, rc, rsa, rsb, wq, wkv, gq, gkv, dga, dgb, dpc, dpc, dz, w1)


def _grad_w(pairs, bt, name, carried=None):
    tt = pairs[0][0].shape[0]
    steps = tt // bt
    npairs = len(pairs)

    def body(*refs):
        ab, rest = refs[:2 * npairs], refs[2 * npairs:]
        if carried is None:
            outs = rest
        else:
            g_hbm, outs, rs_out, scratch = rest[0], rest[1:1 + npairs], rest[1 + npairs], rest[2 + npairs:]
            start, forward, finish = _rs_phases(g_hbm, rs_out, *scratch)
        step = pl.program_id(0)

        @pl.when(step == 0)
        def _():
            for o in outs:
                o[...] = jnp.zeros(o.shape, F32)
            if carried is not None:
                start()

        for i in range(npairs):
            outs[i][...] += _dot_tn(ab[2 * i][...].astype(BF16), ab[2 * i + 1][...].astype(BF16))
        if carried is not None:
            pl.when(step == min(2, steps - 1))(forward)
            pl.when(step == steps - 1)(finish)

    in_specs, out_specs, out_shape = [], [], []
    for a, b in pairs:
        m, n = a.shape[1], b.shape[1]
        in_specs += [pl.BlockSpec((bt, m), lambda i: (i, 0)), pl.BlockSpec((bt, n), lambda i: (i, 0))]
        out_specs.append(pl.BlockSpec((m, n), lambda i: (0, 0)))
        out_shape.append(jax.ShapeDtypeStruct((m, n), F32))
    args = [t for p in pairs for t in p]
    scratch = []
    if carried is not None:
        _, rr, ww = carried.shape
        in_specs.append(pl.BlockSpec(memory_space=pl.ANY))
        out_specs.append(pl.BlockSpec((rr, ww), lambda i: (0, 0)))
        out_shape.append(jax.ShapeDtypeStruct((rr, ww), F32))
        args.append(carried)
        scratch = _rs_scratch(rr, ww)
    return pl.pallas_call(
        body, name=name, grid=(steps,), in_specs=in_specs, out_specs=out_specs, out_shape=out_shape,
        scratch_shapes=scratch, compiler_params=_params(56, ("arbitrary",)))(*args)


def _adamw(triples):
    n = len(triples)

    def body(*refs):
        ins, outs = refs[:4 * n], refs[4 * n:]
        for i in range(n):
            w, g, m, v = (r[...] for r in ins[4 * i:4 * i + 4])
            m = ADAM_B1 * m + (1.0 - ADAM_B1) * g
            v = ADAM_B2 * v + (1.0 - ADAM_B2) * jnp.square(g)
            m_hat = m / (1.0 - ADAM_B1 ** ADAM_STEP)
            v_hat = v / (1.0 - ADAM_B2 ** ADAM_STEP)
            outs[3 * i][...] = -ADAM_LR * (m_hat / (jnp.sqrt(v_hat) + ADAM_EPS) + ADAM_WD * w)
            outs[3 * i + 1][...] = m
            outs[3 * i + 2][...] = v

    flat = [a for t in triples for a in t]
    vmem = pl.BlockSpec(memory_space=pltpu.VMEM)
    res = pl.pallas_call(
        body, name="adamw", in_specs=[vmem] * len(flat), out_specs=[vmem] * (3 * n),
        out_shape=[jax.ShapeDtypeStruct(t[0].shape, F32) for t in triples for _ in range(3)],
        compiler_params=_params(48))(*flat)
    return [tuple(res[3 * i:3 * i + 3]) for i in range(n)]


def _shard_slab(w_in, w_uq, w_ukv, w_out):
    mixed = jnp.concatenate(
        [_pad_rows(w_uq.T, R_MIX), w_ukv.T, jnp.zeros((R_MIX, 1024 - Q_LORA - KV_LORA), F32)], axis=1)
    return jnp.concatenate([w_out, mixed, w_in.T, jnp.zeros((SLAB_ROWS - O_IN - R_IN, 1024), F32)], axis=0)


def _unpack_weights(slabs):
    w_out = slabs[:, :R_OUT].reshape(D_MODEL, D_MODEL)
    uq = slabs[:, O_MIX:O_MIX + R_UQ, :Q_LORA].reshape(HEADS, QK_DIM, Q_LORA)
    wqt = jnp.pad(uq, ((0, 0), (0, HEAD_PAD - QK_DIM), (0, 0))).reshape(HEADS * HEAD_PAD, Q_LORA)
    ukv = slabs[:, O_MIX:O_MIX + R_MIX, Q_LORA:Q_LORA + KV_LORA].reshape(HEADS, 2, NOPE, KV_LORA)
    wkvt = ukv.transpose(1, 0, 2, 3).reshape(2 * MLA_W, KV_LORA)
    raw = slabs[:, O_IN:O_IN + R_IN].reshape(IN_W, D_MODEL)
    w1t = jnp.concatenate([raw[:768 + ROPE], jnp.zeros((128 - ROPE, D_MODEL), BF16), raw[768 + ROPE:]], axis=0)
    return w1t, wqt, wkvt, w_out


def _in_band(g_w1t):
    g_in = jnp.concatenate([g_w1t[:768 + ROPE], g_w1t[C_GA:]], axis=0).reshape(N_DEV, R_IN, D_MODEL)
    return jnp.pad(g_in, ((0, 0), (0, R_IN_PAD - R_IN), (0, 0)))


def _mixed_band(g_wqt, g_wkvt):
    uq = g_wqt.reshape(HEADS, HEAD_PAD, Q_LORA)[:, :QK_DIM].reshape(N_DEV, R_UQ, Q_LORA)
    uq = jnp.pad(uq, ((0, 0), (0, R_MIX - R_UQ), (0, 0)))
    ukv = g_wkvt.reshape(2, HEADS, NOPE, KV_LORA).transpose(1, 0, 2, 3).reshape(N_DEV, R_MIX, KV_LORA)
    return jnp.concatenate([uq, ukv, jnp.zeros((N_DEV, R_MIX, 1024 - Q_LORA - KV_LORA), F32)], axis=2)


def _rope_rows():
    half = ROPE // 2
    inv_freq = ROPE_THETA ** (-jnp.arange(half, dtype=F32) / half)
    zero, one = jnp.zeros((half,), F32), jnp.ones((half,), F32)
    rows = [jnp.concatenate(r) for r in (
        (inv_freq, inv_freq, zero, zero), (one, one, zero, zero), (-one, zero, zero, zero), (zero, one, zero, zero))]
    return jnp.stack(rows + [jnp.zeros((128,), F32)] * 4)


def _pad_rows(a, rows):
    return jnp.pad(a, ((0, rows - a.shape[0]), (0, 0)))


def kernel(x, positions, w_in, q_norm_g, w_uq, kv_norm_g, w_ukv, pool_w, pool_scale, w_out, ln_g, ln_b, loss_target, m_w_in, m_q_norm_g, m_w_uq, m_kv_norm_g, m_w_ukv, m_pool_w, m_pool_scale, m_w_out, m_ln_g, m_ln_b, v_w_in, v_q_norm_g, v_w_uq, v_kv_norm_g, v_w_ukv, v_pool_w, v_pool_scale, v_w_out, v_ln_g, v_ln_b):
    bb, ss, _ = x.shape
    tt = bb * ss
    tile = min(256, ss)
    atile = min(512, ss)
    nblk = ss // atile

    slab = _shard_slab(w_in, w_uq, w_ukv, w_out).astype(BF16)
    slabs = _all_gather(slab, "gather_weights").reshape(N_DEV, SLAB_ROWS, 1024)
    w1, wq, wkv, wo = _unpack_weights(slabs)

    gq, gkv = q_norm_g.reshape(1, Q_LORA), kv_norm_g.reshape(1, KV_LORA)
    ps = pool_scale.reshape(1, POOL_W)
    pos_col = positions.reshape(bb, ss, 1)
    pos_blk = positions.reshape(bb, nblk, 1, atile)
    qmax = jnp.max(positions.reshape(bb, nblk, atile), axis=-1).reshape(-1)
    kmin = jnp.min(positions.reshape(bb, nblk, atile), axis=-1).reshape(-1)

    x2 = x.reshape(tt, D_MODEL)
    xq, xkv, ga, u, gb, qn, kvn, qcat, kcat, v, rc, rsa, rsb = _fwd_in(
        x2, w1, gq, gkv, wq, wkv, positions.reshape(tt, 1), _rope_rows(), tile)
    as3 = lambda a: a.reshape(bb, ss, a.shape[-1])
    qcat, kcat, v = as3(qcat), as3(kcat), as3(v)
    o, lse = _attn_fwd(qcat, kcat, v, pos_col, pos_blk, qmax, kmin, atile, atile)
    dz, ycat, do, dga, dgb, dpc, dl, loss_p, g_lng, g_lnb, g_ps, g_pw = _post(
        x, loss_target, o, as3(ga), as3(u), as3(gb), wo, pool_w.astype(BF16), ps, ln_g, ln_b, tile)

    bt = min(512, tt)
    g_wo, = _grad_w([(ycat.reshape(tt, D_MODEL), dz.reshape(tt, D_MODEL))], bt, "grad_w_out")
    rows5 = lambda a: a.reshape(bb, HEADS, nblk, 1, atile)
    dqcat, dkcat, dv = _attn_bwd(qcat, kcat, v, do, rows5(lse), rows5(dl), pos_col, pos_blk, qmax, kmin, atile, atile)
    grad_x, dh, dqp, dkv, g_gq, g_gkv = _bwd_mid(
        dqcat, dkcat, dv, as3(xq), as3(xkv), as3(rc), as3(rsa), as3(rsb), wq, wkv, gq, gkv, dga, dgb, dpc, dz, w1, tile)
    g_w1, rs_out = _grad_w([(dh.reshape(tt, IN_WP), x2)], bt, "grad_w_in",
                           carried=g_wo.reshape(N_DEV, R_OUT, D_MODEL))
    g_wq, g_wkv, rs_in = _grad_w([(dqp.reshape(tt, HEADS * HEAD_PAD), qn), (dkv.reshape(tt, 2 * MLA_W), kvn)], bt,
                                 "grad_w_uqkv", carried=_in_band(g_w1))
    small = jnp.concatenate(
        [_pad_rows(g_lng.reshape(8, 128), 8), _pad_rows(g_lnb.reshape(8, 128), 8), _pad_rows(g_gq.reshape(4, 128), 8),
         _pad_rows(g_gkv.reshape(2, 128), 8), _pad_rows(g_ps.reshape(4, 128), 8), g_pw.reshape(POOL_G * POOL_C, 128),
         loss_p], axis=0)
    small, rs_mix = _tail_exchange(small, _mixed_band(g_wq, g_wkv), "tail_exchange")
    loss = small[40 + POOL_G * POOL_C, 0]
    grads = {
        "w_in": rs_in[:R_IN],
        "q_norm_g": small[16:20].reshape(1, Q_LORA),
        "w_uq": rs_mix[:R_UQ, :Q_LORA],
        "kv_norm_g": small[24:26].reshape(1, KV_LORA),
        "w_ukv": rs_mix[:, Q_LORA:Q_LORA + KV_LORA].T,
        "pool_w": small[40:40 + POOL_G * POOL_C],
        "pool_scale": small[32:36].reshape(1, POOL_W),
        "w_out": rs_out,
        "ln_g": small[0:8].reshape(1, D_MODEL),
        "ln_b": small[8:16].reshape(1, D_MODEL),
    }
    transposed = ("w_in", "w_uq")

    names = ["w_in", "q_norm_g", "w_uq", "kv_norm_g", "w_ukv", "pool_w", "pool_scale", "w_out", "ln_g", "ln_b"]
    weights = dict(w_in=w_in, q_norm_g=q_norm_g, w_uq=w_uq, kv_norm_g=kv_norm_g, w_ukv=w_ukv, pool_w=pool_w,
                   pool_scale=pool_scale, w_out=w_out, ln_g=ln_g, ln_b=ln_b)
    moms = dict(w_in=(m_w_in, v_w_in), q_norm_g=(m_q_norm_g, v_q_norm_g), w_uq=(m_w_uq, v_w_uq),
                kv_norm_g=(m_kv_norm_g, v_kv_norm_g), w_ukv=(m_w_ukv, v_w_ukv), pool_w=(m_pool_w, v_pool_w),
                pool_scale=(m_pool_scale, v_pool_scale), w_out=(m_w_out, v_w_out), ln_g=(m_ln_g, v_ln_g),
                ln_b=(m_ln_b, v_ln_b))
    as2 = lambda a, n: a.T if n in transposed else a.reshape(grads[n].shape)
    upd = _adamw([(as2(weights[n], n), grads[n], as2(moms[n][0], n), as2(moms[n][1], n)) for n in names])
    shaped = lambda a, n: a.T if n in transposed else a.reshape(weights[n].shape)
    return (loss, grad_x,
            *[shaped(grads[n], n) for n in names],
            *[shaped(upd[i][0], n) for i, n in enumerate(names)],
            *[shaped(upd[i][1], n) for i, n in enumerate(names)],
            *[shaped(upd[i][2], n) for i, n in enumerate(names)])
```

```python
import functools

import jax
import jax.numpy as jnp
from jax import lax
from jax.experimental import pallas as pl
from jax.experimental.pallas import tpu as pltpu

F32 = jnp.float32
BF16 = jnp.bfloat16
MESH = pl.DeviceIdType.MESH

N_DEV = 8
D_MODEL = 1024
HEADS = 4
NOPE = 128
ROPE = 64
V_DIM = 128
QK_DIM = NOPE + ROPE
HEAD_PAD = 256
Q_LORA = 512
KV_LORA = 256
MLA_W = HEADS * V_DIM
POOL_W = 512
POOL_G = 4
POOL_C = 128
POOL_HALO = 16
IN_W = 2368
IN_WP = 2432
C_XQ, C_XKV, C_KR, C_GA, C_U, C_GB = 0, 512, 768, 896, 1408, 1920
ROPE_THETA = 10000.0
RMS_EPS = 1e-6
LN_EPS = 1e-5
ALPHA = 2.0 ** 0.25
SCALE = QK_DIM ** -0.5
SCALE_LOG2 = SCALE * 1.4426950408889634
NEG = float(jnp.finfo(jnp.float32).min)

ADAM_LR = 0.001
ADAM_B1 = 0.9
ADAM_B2 = 0.999
ADAM_EPS = 1e-08
ADAM_WD = 0.01
ADAM_STEP = 10

R_OUT, R_MIX, R_UQ, R_IN = 128, 128, 96, 296
O_MIX, O_IN = R_OUT, R_OUT + R_MIX
SLAB_ROWS = 560
R_IN_PAD = 304

V7X_VMEM_BYTES = 64 * 1024 * 1024
ATTN_FWD_HEADS_PER_STEP = 4
ATTN_BWD_HEADS_PER_STEP = 2


def _params(vmem_mb, semantics=None):
    assert vmem_mb * 1024 * 1024 < V7X_VMEM_BYTES
    return pltpu.CompilerParams(vmem_limit_bytes=vmem_mb * 1024 * 1024, dimension_semantics=semantics)


def _dot(a, b):
    return jnp.dot(a, b, preferred_element_type=F32)


def _dot_nt(a, b):
    return lax.dot_general(a, b, (((1,), (1,)), ((), ())), preferred_element_type=F32)


def _dot_tn(a, b):
    return lax.dot_general(a, b, (((0,), (0,)), ((), ())), preferred_element_type=F32)


def _rope_fwd(t, c, sa, sb):
    return t * c + pltpu.roll(t, 96, 1) * sa + pltpu.roll(t, 32, 1) * sb


def _rope_bwd(d, c, sa, sb):
    return d * c + pltpu.roll(d * sa, 32, 1) + pltpu.roll(d * sb, 96, 1)


def _silu_parts(g):
    sig = jax.nn.sigmoid(g)
    return g * sig, sig * (1.0 + g * (1.0 - sig))


def _pool_cnt(row0, rows):
    t = row0 + lax.broadcasted_iota(jnp.int32, (rows, POOL_W), 0)
    w = 2 << (lax.broadcasted_iota(jnp.int32, (rows, POOL_W), 1) // POOL_C)
    return jnp.minimum(t + 1, w).astype(F32)


def _pick_groups(s2, s4, s8, s16, lo, hi):
    return jnp.concatenate([s2[lo:hi, 0:128], s4[lo:hi, 128:256], s8[lo:hi, 256:384], s16[lo:hi, 384:512]], axis=1)


AG_SEMS = [pltpu.SemaphoreType.DMA((7,)), pltpu.SemaphoreType.DMA((7,)), pltpu.SemaphoreType.DMA]


def _ag_run(x_ref, out_ref, send_sems, recv_sems, local_sem):
    m_per = x_ref.shape[0]
    x, y, c = lax.axis_index("x"), lax.axis_index("y"), lax.axis_index("c")
    me, sibling = (x, y, c), (x, y, 1 - c)
    chips = [(1 - x, y), (x, 1 - y), (1 - x, 1 - y)]

    def rows(px, py, pc):
        return out_ref.at[pl.ds((4 * px + 2 * py + pc) * m_per, m_per), :]

    def copy(k, block, to, src=None):
        return pltpu.make_async_remote_copy(
            src_ref=rows(*block) if src is None else src, dst_ref=rows(*block),
            send_sem=send_sems.at[k], recv_sem=recv_sems.at[k], device_id=to, device_id_type=MESH)

    mine = pltpu.make_async_copy(x_ref, rows(*me), local_sem)
    mine.start()
    first = [copy(0, me, sibling, src=x_ref)]
    first += [copy(1 + j, me, (*chip, c), src=x_ref) for j, chip in enumerate(chips)]
    for cp in first:
        cp.start()
    passed = [copy(4 + j, (*chip, c), sibling) for j, chip in enumerate(chips)]
    for j, chip in enumerate(chips):
        copy(1 + j, (*chip, c), me).wait_recv()
        passed[j].start()
    copy(0, sibling, me).wait_recv()
    for j, chip in enumerate(chips):
        copy(4 + j, (*chip, 1 - c), me).wait_recv()
    for cp in first + passed:
        cp.wait_send()
    mine.wait()


def _all_gather(shard, name):
    m_per, n = shard.shape
    vmem = pl.BlockSpec(memory_space=pltpu.VMEM)
    return pl.pallas_call(
        _ag_run, name=name, out_shape=jax.ShapeDtypeStruct((N_DEV * m_per, n), shard.dtype),
        in_specs=[vmem], out_specs=vmem, scratch_shapes=AG_SEMS, compiler_params=_params(32))(shard)


def _tail_exchange(small, gslab, name):
    m_per, n = small.shape
    _, rr, ww = gslab.shape

    def body(x_ref, g_hbm, sum_ref, rs_out, gathered, send_sems, recv_sems, local_sem, *rs_scratch):
        start, forward, finish = _rs_phases(g_hbm, rs_out, *rs_scratch)
        start()
        _ag_run(x_ref, gathered, send_sems, recv_sems, local_sem)
        forward()
        acc = gathered[pl.ds(0, m_per), :]
        for d in range(1, N_DEV):
            acc = acc + gathered[pl.ds(d * m_per, m_per), :]
        sum_ref[...] = acc
        finish()

    vmem = pl.BlockSpec(memory_space=pltpu.VMEM)
    return pl.pallas_call(
        body, name=name, in_specs=[vmem, pl.BlockSpec(memory_space=pl.ANY)], out_specs=[vmem, vmem],
        out_shape=[jax.ShapeDtypeStruct((m_per, n), F32), jax.ShapeDtypeStruct((rr, ww), F32)],
        scratch_shapes=[pltpu.VMEM((N_DEV * m_per, n), F32)] + AG_SEMS + _rs_scratch(rr, ww),
        compiler_params=_params(32))(small, gslab)


def _rs_scratch(rr, ww):
    return [pltpu.VMEM((4, rr, ww), F32), pltpu.VMEM((4, rr, ww), F32),
            pltpu.VMEM((3, rr, ww), BF16), pltpu.VMEM((3, rr, ww), BF16),
            pltpu.SemaphoreType.DMA((4,)), pltpu.SemaphoreType.DMA((4,)), pltpu.SemaphoreType.DMA((4,)),
            pltpu.SemaphoreType.DMA((3,)), pltpu.SemaphoreType.DMA((3,))]


def _rs_phases(g_hbm, out_ref, own_ref, recv1_ref, sendb_ref, recv2_ref, ld_sems, s1_send, s1_recv, s2_send, s2_recv):
    _, rr, ww = g_hbm.shape
    chunk = next(c for c in (128, 80, 64, 48, 32, 16) if rr % c == 0)
    x, y, c = lax.axis_index("x"), lax.axis_index("y"), lax.axis_index("c")
    chips = [(1 - x, y), (x, 1 - y), (1 - x, 1 - y)]

    def loads():
        return [pltpu.make_async_copy(g_hbm.at[2 * k + c], own_ref.at[k], ld_sems.at[k]) for k in range(4)]

    def stage1():
        return [pltpu.make_async_remote_copy(
            src_ref=g_hbm.at[2 * k + (1 - c)], dst_ref=recv1_ref.at[k], send_sem=s1_send.at[k],
            recv_sem=s1_recv.at[k], device_id=(x, y, 1 - c), device_id_type=MESH) for k in range(4)]

    def stage2():
        return [pltpu.make_async_remote_copy(
            src_ref=sendb_ref.at[r], dst_ref=recv2_ref.at[r], send_sem=s2_send.at[r],
            recv_sem=s2_recv.at[r], device_id=(cx, cy, c), device_id_type=MESH) for r, (cx, cy) in enumerate(chips)]

    def start():
        for cp in loads() + stage1():
            cp.start()

    def forward():
        for cp in loads():
            cp.wait()
        for cp in stage1():
            cp.wait_recv()
        sends = stage2()
        for r, (cx, cy) in enumerate(chips):
            kk = 2 * cx + cy

            def pack(i, carry, r=r, kk=kk):
                rows = pl.ds(pl.multiple_of(i * chunk, chunk), chunk)
                sendb_ref[r, rows, :] = (own_ref[kk, rows, :] + recv1_ref[kk, rows, :]).astype(BF16)
                return carry

            lax.fori_loop(0, rr // chunk, pack, 0)
            sends[r].start()

    def finish():
        for cp in stage2():
            cp.wait_recv()
        mine = 2 * x + y

        def total(i, carry):
            rows = pl.ds(pl.multiple_of(i * chunk, chunk), chunk)
            acc = own_ref[mine, rows, :] + recv1_ref[mine, rows, :]
            for r in range(3):
                acc = acc + recv2_ref[r, rows, :].astype(F32)
            out_ref[rows, :] = acc
            return carry

        lax.fori_loop(0, rr // chunk, total, 0)
        for cp in stage1() + stage2():
            cp.wait_send()

    return start, forward, finish


def _fwd_in(x2, w1, gq, gkv, wq, wkv, pos, rope_rows, tm):
    tt = x2.shape[0]

    def body(x_ref, w1_ref, gq_ref, gkv_ref, wq_ref, wkv_ref, pos_ref, rr_ref,
             xq_ref, xkv_ref, ga_ref, u_ref, gb_ref, qn_ref, kvn_ref, qcat_ref, kcat_ref, v_ref,
             c_ref, sa_ref, sb_ref):
        ang = pos_ref[...].astype(F32) * rr_ref[0:1, :]
        cos, sin = jnp.cos(ang), jnp.sin(ang)
        c, sa, sb = cos * rr_ref[1:2, :], sin * rr_ref[2:3, :], sin * rr_ref[3:4, :]
        c_ref[...] = c
        sa_ref[...] = sa
        sb_ref[...] = sb
        h = _dot_nt(x_ref[...].astype(BF16), w1_ref[...])
        xq = h[:, C_XQ:C_XKV]
        xkv = h[:, C_XKV:C_KR]
        xq_ref[...] = xq
        xkv_ref[...] = xkv
        ga_ref[...] = h[:, C_GA:C_U]
        u_ref[...] = h[:, C_U:C_GB]
        gb_ref[...] = h[:, C_GB:IN_WP]
        qn =(xq * lax.rsqrt(jnp.mean(xq * xq, axis=-1, keepdims=True) + RMS_EPS) * gq_ref[...]).astype(BF16)
        kvn = (xkv * lax.rsqrt(jnp.mean(xkv * xkv, axis=-1, keepdims=True) + RMS_EPS) * gkv_ref[...]).astype(BF16)
        qn_ref[...] = qn
        kvn_ref[...] = kvn
        q = _dot_nt(qn, wq_ref[...])
        kv = _dot_nt(kvn, wkv_ref[...])
        kr = _rope_fwd(h[:, C_KR:C_GA], c, sa, sb).astype(BF16)
        for hd in range(HEADS):
            lo = hd * HEAD_PAD
            qcat_ref[:, lo:lo + NOPE] = q[:, lo:lo + NOPE].astype(BF16)
            qcat_ref[:, lo + NOPE:lo + HEAD_PAD] = _rope_fwd(q[:, lo + NOPE:lo + HEAD_PAD], c, sa, sb).astype(BF16)
            kcat_ref[:, lo:lo + NOPE] = kv[:, hd * NOPE:(hd + 1) * NOPE].astype(BF16)
            kcat_ref[:, lo + NOPE:lo + HEAD_PAD] = kr
        v_ref[...] = kv[:, MLA_W:].astype(BF16)

    def tile(w):
        return pl.BlockSpec((tm, w), lambda i: (i, 0))

    def whole(a):
        return pl.BlockSpec(a.shape, lambda i: (0, 0))

    outs = [(Q_LORA, F32), (KV_LORA, F32), (MLA_W, F32), (POOL_W, F32), (POOL_W, F32),
            (Q_LORA, BF16), (KV_LORA, BF16), (HEADS * HEAD_PAD, BF16), (HEADS * HEAD_PAD, BF16), (MLA_W, BF16),
            (128, F32), (128, F32), (128, F32)]
    return pl.pallas_call(
        body, name="fwd_in", grid=(tt // tm,),
        in_specs=[tile(D_MODEL), whole(w1), whole(gq), whole(gkv), whole(wq), whole(wkv), tile(1), whole(rope_rows)],
        out_specs=[tile(w) for w, _ in outs],
        out_shape=[jax.ShapeDtypeStruct((tt, w), dt) for w, dt in outs],
        compiler_params=_params(48, ("arbitrary",)))(x2, w1, gq, gkv, wq, wkv, pos, rope_rows)


def _attn_fwd(qcat, kcat, v, pos_col, pos_blk, qmax, kmin, tq, tk):
    bb, ss, _ = qcat.shape
    nq, nk = ss // tq, ss // tk
    lanes = 128
    hps = ATTN_FWD_HEADS_PER_STEP

    def body(qmax_ref, kmin_ref, q_ref, k_ref, v_ref, pc_ref, pb_ref, o_ref, lse_ref, m_s, acc_s):
        b, qi = pl.program_id(0), pl.program_id(2)
        m_s[...] = jnp.full(m_s.shape, NEG, F32)
        acc_s[...] = jnp.zeros(acc_s.shape, F32)
        pq = pc_ref[...]
        reach = qmax_ref[b * nq + qi]
        ones = jnp.ones((tk, lanes), BF16)

        def step(ki, carry):
            @pl.when(reach >= kmin_ref[b * nk + ki])
            def _():
                rows = pl.ds(pl.multiple_of(ki * tk, tk), tk)
                mask = pq >= pb_ref[ki]
                for hd in range(hps):
                    qk = slice(hd * HEAD_PAD, (hd + 1) * HEAD_PAD)
                    s = _dot_nt(q_ref[:, qk], k_ref[rows, qk]) * SCALE_LOG2
                    s = jnp.where(mask, s, NEG)
                    m_prev = m_s[hd]
                    m_new = jnp.maximum(m_prev, jnp.max(s, axis=-1, keepdims=True))
                    p = jnp.exp2(s - jnp.tile(m_new, (1, tk // lanes)))
                    a = jnp.exp2(m_prev - m_new)
                    vv = jnp.concatenate([v_ref[rows, hd * V_DIM:(hd + 1) * V_DIM], ones], axis=1)
                    acc_s[hd] = jnp.tile(a, (1, 2)) * acc_s[hd] + _dot(p.astype(BF16), vv)
                    m_s[hd] = m_new
            return carry

        lax.fori_loop(0, nk, step, 0)
        for hd in range(hps):
            acc = acc_s[hd]
            l = acc[:, V_DIM:]
            o_ref[:, hd * V_DIM:(hd + 1) * V_DIM] = acc[:, :V_DIM] / l
            lse_ref[hd] = (m_s[hd] + jnp.log2(l)).T[0:1, :]

    grid_spec = pltpu.PrefetchScalarGridSpec(
        num_scalar_prefetch=2, grid=(bb, HEADS // hps, nq),
        in_specs=[
            pl.BlockSpec((None, tq, hps * HEAD_PAD), lambda b, h, i, *_: (b, i, h)),
            pl.BlockSpec((None, ss, hps * HEAD_PAD), lambda b, h, i, *_: (b, 0, h)),
            pl.BlockSpec((None, ss, hps * V_DIM), lambda b, h, i, *_: (b, 0, h)),
            pl.BlockSpec((None, tq, 1), lambda b, h, i, *_: (b, i, 0)),
            pl.BlockSpec((None, nk, 1, tk), lambda b, h, i, *_: (b, 0, 0, 0)),
        ],
        out_specs=[
            pl.BlockSpec((None, tq, hps * V_DIM), lambda b, h, i, *_: (b, i, h)),
            pl.BlockSpec((None, hps, 1, tq), lambda b, h, i, *_: (b, h, 0, i)),
        ],
        scratch_shapes=[pltpu.VMEM((hps, tq, lanes), F32), pltpu.VMEM((hps, tq, 2 * V_DIM), F32)])
    return pl.pallas_call(
        body, name="attn_fwd", grid_spec=grid_spec,
        out_shape=[jax.ShapeDtypeStruct((bb, ss, MLA_W), F32), jax.ShapeDtypeStruct((bb, HEADS, 1, ss), F32)],
        compiler_params=_params(48, ("arbitrary", "arbitrary", "arbitrary")))(qmax, kmin, qcat, kcat, v, pos_col, pos_blk)


def _post(x, tgt, o, ga, u, gb, w_out, pool_wb, pool_scale, ln_g, ln_b, ts):
    bb, ss, _ = x.shape
    nt = ss // ts
    hb = ts // POOL_HALO

    def body(x_ref, tgt_ref, o_ref, ga_ref, u_ref, uh_ref, gb_ref, wo_ref, pw_ref, ps_ref, lg_ref, lb_ref,
             dz_ref, ycat_ref, do_ref, dga_ref, dgb_ref, dpc_ref, dl_ref, loss_ref, glg_ref, glb_ref, gps_ref, gpw_ref):
        step = pl.program_id(0)
        j = step % nt

        @pl.when(step == 0)
        def _():
            for r in (loss_ref, glg_ref, glb_ref, gps_ref, gpw_ref):
                r[...] = jnp.zeros(r.shape, F32)

        o, ga, u, gb = o_ref[...], ga_ref[...], u_ref[...], gb_ref[...]
        sa, dsa = _silu_parts(ga)
        sb, dsb = _silu_parts(gb)
        ext = jnp.concatenate([jnp.where(j > 0, uh_ref[...], 0.0), u], axis=0)
        s2 = ext + pltpu.roll(ext, 1, 0)
        s4 = s2 + pltpu.roll(s2, 2, 0)
        s8 = s4 + pltpu.roll(s4, 4, 0)
        s16 = s8 + pltpu.roll(s8, 8, 0)
        cnt = _pool_cnt(j * ts, ts)
        pooled = (_pick_groups(s2, s4, s8, s16, POOL_HALO, POOL_HALO + ts) / cnt - u).astype(BF16)
        mixed = jnp.concatenate(
            [_dot(pooled[:, g * POOL_C:(g + 1) * POOL_C], pw_ref[g]) for g in range(POOL_G)], axis=1)
        ps = ps_ref[...]
        scaled = mixed * ps
        ycat = jnp.concatenate([o * sa, scaled * sb], axis=1).astype(BF16)
        ycat_ref[...] = ycat
        z = ALPHA * x_ref[...] + _dot(ycat, wo_ref[...])
        mu = jnp.mean(z, axis=-1, keepdims=True)
        zc = z - mu
        rstd = lax.rsqrt(jnp.mean(zc * zc, axis=-1, keepdims=True) + LN_EPS)
        xhat = zc * rstd
        lg = lg_ref[...]
        diff = xhat * lg + lb_ref[...] - tgt_ref[...]
        loss_ref[...] += jnp.sum(diff * diff) * (0.5 / D_MODEL)
        dy = diff * (1.0 / D_MODEL)
        glb_ref[...] += jnp.sum(dy, axis=0, keepdims=True)
        glg_ref[...] += jnp.sum(dy * xhat, axis=0, keepdims=True)
        dxh = dy * lg
        dz = rstd * (dxh - jnp.mean(dxh, axis=-1, keepdims=True) - xhat * jnp.mean(dxh * xhat, axis=-1, keepdims=True))
        dz_ref[...] = dz
        dycat = _dot_nt(dz.astype(BF16), wo_ref[...])
        dya, dyb = dycat[:, :MLA_W], dycat[:, MLA_W:]
        do = dya * sa
        do_ref[...] = do.astype(BF16)
        doo = do * o
        for hd in range(HEADS):
            dl_ref[hd] = jnp.sum(doo[:, hd * V_DIM:(hd + 1) * V_DIM].T, axis=0, keepdims=True)
        dga_ref[...] = (dya * o * dsa).astype(BF16)
        dgb_ref[...] = (dyb * scaled * dsb).astype(BF16)
        dscaled = dyb * sb
        gps_ref[...] += jnp.sum(dscaled * mixed, axis=0, keepdims=True)
        dmixed = (dscaled * ps).astype(BF16)
        dpooled = []
        for g in range(POOL_G):
            cols = slice(g * POOL_C, (g + 1) * POOL_C)
            gpw_ref[g] += _dot_tn(pooled[:, cols], dmixed[:, cols])
            dpooled.append(_dot_nt(dmixed[:, cols], pw_ref[g]))
        dpc_ref[...] = jnp.concatenate(dpooled, axis=1) / cnt

    def tile(w):
        return pl.BlockSpec((None, ts, w), lambda i: (i // nt, i % nt, 0))

    def whole(a):
        nd = a.ndim
        return pl.BlockSpec(a.shape, lambda i: (0,) * nd)

    halo = pl.BlockSpec((None, POOL_HALO, POOL_W), lambda i: (i // nt, jnp.maximum((i % nt) * hb - 1, 0), 0))
    acc_shapes = [(8, 128), (1, D_MODEL), (1, D_MODEL), (1, POOL_W), (POOL_G, POOL_C, POOL_C)]
    tile_outs = [(D_MODEL, F32), (D_MODEL, BF16), (MLA_W, BF16), (MLA_W, BF16), (POOL_W, BF16), (POOL_W, F32)]
    return pl.pallas_call(
        body, name="post", grid=(bb * nt,),
        in_specs=[tile(D_MODEL), tile(D_MODEL), tile(MLA_W), tile(MLA_W), tile(POOL_W), halo, tile(POOL_W),
                  whole(w_out), whole(pool_wb), whole(pool_scale), whole(ln_g), whole(ln_b)],
        out_specs=[tile(w) for w, _ in tile_outs]
        + [pl.BlockSpec((None, HEADS, 1, ts), lambda i: (i // nt, 0, 0, i % nt))]
        + [pl.BlockSpec(s, lambda i, n=len(s): (0,) * n) for s in acc_shapes],
        out_shape=[jax.ShapeDtypeStruct((bb, ss, w), dt) for w, dt in tile_outs]
        + [jax.ShapeDtypeStruct((bb, HEADS, 1, ss), F32)]
        + [jax.ShapeDtypeStruct(s, F32) for s in acc_shapes],
        compiler_params=_params(48, ("arbitrary",)))(x, tgt, o, ga, u, u, gb, w_out, pool_wb, pool_scale, ln_g, ln_b)


def _attn_bwd(qcat, kcat, v, do, lse, dl, rc, rsa, rsb, pos_col, pos_blk, qmax, kmin, tq, tk):
    bb, ss, _ = qcat.shape
    nq, nk = ss // tq, ss // tk
    hps = ATTN_BWD_HEADS_PER_STEP

    def body(qmax_ref, kmin_ref, q_ref, k_ref, v_ref, do_ref, lse_ref, dl_ref, c_ref, sa_ref, sb_ref, pc_ref, pb_ref,
             dqp_ref, dkn_ref, dv_ref, dkr_ref, dq_s, dk_s, dv_s):
        b = pl.program_id(0)
        dq_s[...] = jnp.zeros(dq_s.shape, F32)

        def kv_step(ki, carry):
            krows = pl.ds(pl.multiple_of(ki * tk, tk), tk)
            pk = pc_ref[krows, :]
            first = kmin_ref[b * nk + ki]
            dk_s[...] = jnp.zeros(dk_s.shape, F32)
            dv_s[...] = jnp.zeros(dv_s.shape, F32)

            def q_step(qi, c2):
                @pl.when(qmax_ref[b * nq + qi] >= first)
                def _():
                    qrows = pl.ds(pl.multiple_of(qi * tq, tq), tq)
                    mask = pb_ref[qi] >= pk
                    for hd in range(hps):
                        qk = slice(hd * HEAD_PAD, (hd + 1) * HEAD_PAD)
                        vs = slice(hd * V_DIM, (hd + 1) * V_DIM)
                        q = q_ref[qrows, qk]
                        dd = do_ref[qrows, vs]
                        st = _dot_nt(k_ref[krows, qk], q) * SCALE_LOG2
                        st = jnp.where(mask, st, NEG)
                        pt = jnp.exp2(st - lse_ref[hd, qi])
                        dv_s[:, vs] += _dot(pt.astype(BF16), dd)
                        dpt = _dot_nt(v_ref[krows, vs], dd)
                        dst = (pt * (dpt - dl_ref[hd, qi]) * SCALE).astype(BF16)
                        dk_s[:, qk] += _dot(dst, q)
                        dq_s[qrows, qk] += _dot_tn(dst, k_ref[krows, qk])
                return c2

            lax.fori_loop(0, nq, q_step, 0)
            dkr = jnp.zeros((tk, HEAD_PAD - NOPE), F32)
            for hd in range(hps):
                lo = hd * HEAD_PAD
                dkn_ref[krows, hd * NOPE:(hd + 1) * NOPE] = dk_s[:, lo:lo + NOPE].astype(BF16)
                dkr = dkr + dk_s[:, lo + NOPE:lo + HEAD_PAD]
            dkr_ref[krows, :] = dkr
            dv_ref[krows, :] = dv_s[...].astype(BF16)
            return carry

        lax.fori_loop(0, nk, kv_step, 0)
        c, sa, sb = c_ref[...], sa_ref[...], sb_ref[...]
        for hd in range(hps):
            lo = hd * HEAD_PAD
            dqp_ref[:, lo:lo + NOPE] = dq_s[:, lo:lo + NOPE].astype(BF16)
            dqp_ref[:, lo + NOPE:lo + HEAD_PAD] = _rope_bwd(dq_s[:, lo + NOPE:lo + HEAD_PAD], c, sa, sb).astype(BF16)

    def per_head(w):
        return pl.BlockSpec((None, ss, hps * w), lambda b, h, *_: (b, 0, h))

    def rows_of_head():
        return pl.BlockSpec((None, hps, nq, 1, tq), lambda b, h, *_: (b, h, 0, 0, 0))

    def per_batch(w):
        return pl.BlockSpec((None, ss, w), lambda b, h, *_: (b, 0, 0))

    grid_spec = pltpu.PrefetchScalarGridSpec(
        num_scalar_prefetch=2, grid=(bb, HEADS // hps),
        in_specs=[per_head(HEAD_PAD), per_head(HEAD_PAD), per_head(V_DIM), per_head(V_DIM),
                  rows_of_head(), rows_of_head(), per_batch(128), per_batch(128), per_batch(128), per_batch(1),
                  pl.BlockSpec((None, nq, 1, tq), lambda b, h, *_: (b, 0, 0, 0))],
        out_specs=[per_head(HEAD_PAD), per_head(NOPE), per_head(V_DIM),
                   pl.BlockSpec((None, None, ss, HEAD_PAD - NOPE), lambda b, h, *_: (b, h, 0, 0))],
        scratch_shapes=[pltpu.VMEM((ss, hps * HEAD_PAD), F32), pltpu.VMEM((tk, hps * HEAD_PAD), F32),
                        pltpu.VMEM((tk, hps * V_DIM), F32)])
    return pl.pallas_call(
        body, name="attn_bwd", grid_spec=grid_spec,
        out_shape=[jax.ShapeDtypeStruct((bb, ss, HEADS * HEAD_PAD), BF16),
                   jax.ShapeDtypeStruct((bb, ss, MLA_W), BF16),
                   jax.ShapeDtypeStruct((bb, ss, MLA_W), BF16),
                   jax.ShapeDtypeStruct((bb, HEADS // hps, ss, HEAD_PAD - NOPE), F32)],
        compiler_params=_params(56, ("arbitrary", "arbitrary")))(
            qmax, kmin, qcat, kcat, v, do, lse, dl, rc, rsa, rsb, pos_col, pos_blk)


def _bwd_mid(dqp, dkn, dv, dkr, xq, xkv, rc, rsa, rsb, wq, wkv, gq, gkv, dga, dgb, dpc, dz, w1, ts):
    bb, ss, _ = dz.shape
    nt = ss // ts
    hb = ts // POOL_HALO
    groups = dkr.shape[1]

    def body(dqp_ref, dkn_ref, dv_ref, dkr_ref, xq_ref, xkv_ref, c_ref, sa_ref, sb_ref, wq_ref, wkv_ref, gq_ref,
             gkv_ref, dga_ref, dgb_ref, dpc_ref, dph_ref, dz_ref, w1_ref,
             gx_ref, dh_ref, ggq_ref, ggkv_ref):
        step = pl.program_id(0)
        j = step % nt

        @pl.when(step == 0)
        def _():
            ggq_ref[...] = jnp.zeros(ggq_ref.shape, F32)
            ggkv_ref[...] = jnp.zeros(ggkv_ref.shape, F32)

        dkr = dkr_ref[0]
        for g in range(1, groups):
            dkr = dkr + dkr_ref[g]
        dh_ref[:, C_KR:C_GA] = _rope_bwd(dkr, c_ref[...], sa_ref[...], sb_ref[...]).astype(BF16)

        def rms_bwd(x, g, dn):
            inv = lax.rsqrt(jnp.mean(x * x, axis=-1, keepdims=True) + RMS_EPS)
            xh = x * inv
            dxh = dn * g
            return inv * (dxh - xh * jnp.mean(dxh * xh, axis=-1, keepdims=True)), jnp.sum(dn * xh, axis=0, keepdims=True)

        dxq, ggq = rms_bwd(xq_ref[...], gq_ref[...], _dot(dqp_ref[...], wq_ref[...]))
        dkvn = _dot(dkn_ref[...], wkv_ref[:MLA_W, :]) + _dot(dv_ref[...], wkv_ref[MLA_W:, :])
        dxkv, ggkv = rms_bwd(xkv_ref[...], gkv_ref[...], dkvn)
        ggq_ref[...] += ggq
        ggkv_ref[...] += ggkv
        dh_ref[:, C_XQ:C_XKV] = dxq.astype(BF16)
        dh_ref[:, C_XKV:C_KR] = dxkv.astype(BF16)
        dh_ref[:, C_GA:C_U] = dga_ref[...]
        dh_ref[:, C_GB:IN_WP] = dgb_ref[...]
        dpc = dpc_ref[...]
        n = ts + POOL_HALO
        ext = jnp.concatenate([dpc, jnp.where(j < nt - 1, dph_ref[...], 0.0)], axis=0)
        r2 = ext + pltpu.roll(ext, n - 1, 0)
        r4 = r2 + pltpu.roll(r2, n - 2, 0)
        r8 = r4 + pltpu.roll(r4, n - 4, 0)
        r16 = r8 + pltpu.roll(r8, n - 8, 0)
        du = _pick_groups(r2, r4, r8, r16, 0, ts) - dpc * _pool_cnt(j * ts, ts)
        dh_ref[:, C_U:C_GB] = du.astype(BF16)
        gx_ref[...] = ALPHA * dz_ref[...] + _dot(dh_ref[...], w1_ref[...])

    def tile(w):
        return pl.BlockSpec((None, ts, w), lambda i: (i // nt, i % nt, 0))

    def whole(a):
        return pl.BlockSpec(a.shape, lambda i: (0, 0))

    halo = pl.BlockSpec((None, POOL_HALO, POOL_W),
                        lambda i: (i // nt, jnp.minimum((i % nt + 1) * hb, ss // POOL_HALO - 1), 0))
    return pl.pallas_call(
        body, name="bwd_mid", grid=(bb * nt,),
        in_specs=[tile(HEADS * HEAD_PAD), tile(MLA_W), tile(MLA_W),
                  pl.BlockSpec((None, groups, ts, HEAD_PAD - NOPE), lambda i: (i // nt, 0, i % nt, 0)),
                  tile(Q_LORA), tile(KV_LORA),
                  tile(128), tile(128), tile(128), whole(wq), whole(wkv), whole(gq), whole(gkv),
                  tile(MLA_W), tile(POOL_W), tile(POOL_W), halo, tile(D_MODEL), whole(w1)],
        out_specs=[tile(D_MODEL), tile(IN_WP),
                   pl.BlockSpec((1, Q_LORA), lambda i: (0, 0)), pl.BlockSpec((1, KV_LORA), lambda i: (0, 0))],
        out_shape=[jax.ShapeDtypeStruct((bb, ss, D_MODEL), F32), jax.ShapeDtypeStruct((bb, ss, IN_WP), BF16),
                   jax.ShapeDtypeStruct((1, Q_LORA), F32), jax.ShapeDtypeStruct((1, KV_LORA), F32)],
        compiler_params=_params(48, ("arbitrary",)))(
            dqp, dkn, dv, dkr, xq, xkv, rc, rsa, rsb, wq, wkv, gq, gkv, dga, dgb, dpc, dpc, dz, w1)


def _grad_w(pairs, bt, name, carried=None):
    tt = pairs[0][0].shape[0]
    steps = tt // bt
    npairs = len(pairs)

    def body(*refs):
        ab, rest = refs[:2 * npairs], refs[2 * npairs:]
        if carried is None:
            outs = rest
        else:
            g_hbm, outs, rs_out, scratch = rest[0], rest[1:1 + npairs], rest[1 + npairs], rest[2 + npairs:]
            start, forward, finish = _rs_phases(g_hbm, rs_out, *scratch)
        step = pl.program_id(0)

        @pl.when(step == 0)
        def _():
            for o in outs:
                o[...] = jnp.zeros(o.shape, F32)
            if carried is not None:
                start()

        for i in range(npairs):
            outs[i][...] += _dot_tn(ab[2 * i][...].astype(BF16), ab[2 * i + 1][...].astype(BF16))
        if carried is not None:
            pl.when(step == min(2, steps - 1))(forward)
            pl.when(step == steps - 1)(finish)

    in_specs, out_specs, out_shape = [], [], []
    for a, b in pairs:
        m, n = a.shape[1], b.shape[1]
        in_specs += [pl.BlockSpec((bt, m), lambda i: (i, 0)), pl.BlockSpec((bt, n), lambda i: (i, 0))]
        out_specs.append(pl.BlockSpec((m, n), lambda i: (0, 0)))
        out_shape.append(jax.ShapeDtypeStruct((m, n), F32))
    args = [t for p in pairs for t in p]
    scratch = []
    if carried is not None:
        _, rr, ww = carried.shape
        in_specs.append(pl.BlockSpec(memory_space=pl.ANY))
        out_specs.append(pl.BlockSpec((rr, ww), lambda i: (0, 0)))
        out_shape.append(jax.ShapeDtypeStruct((rr, ww), F32))
        args.append(carried)
        scratch = _rs_scratch(rr, ww)
    return pl.pallas_call(
        body, name=name, grid=(steps,), in_specs=in_specs, out_specs=out_specs, out_shape=out_shape,
        scratch_shapes=scratch, compiler_params=_params(56, ("arbitrary",)))(*args)


def _adamw(triples):
    n = len(triples)

    def body(*refs):
        ins, outs = refs[:4 * n], refs[4 * n:]
        for i in range(n):
            w, g, m, v = (r[...] for r in ins[4 * i:4 * i + 4])
            m = ADAM_B1 * m + (1.0 - ADAM_B1) * g
            v = ADAM_B2 * v + (1.0 - ADAM_B2) * jnp.square(g)
            m_hat = m / (1.0 - ADAM_B1 ** ADAM_STEP)
            v_hat = v / (1.0 - ADAM_B2 ** ADAM_STEP)
            outs[3 * i][...] = -ADAM_LR * (m_hat / (jnp.sqrt(v_hat) + ADAM_EPS) + ADAM_WD * w)
            outs[3 * i + 1][...] = m
            outs[3 * i + 2][...] = v

    flat = [a for t in triples for a in t]
    vmem = pl.BlockSpec(memory_space=pltpu.VMEM)
    res = pl.pallas_call(
        body, name="adamw", in_specs=[vmem] * len(flat), out_specs=[vmem] * (3 * n),
        out_shape=[jax.ShapeDtypeStruct(t[0].shape, F32) for t in triples for _ in range(3)],
        compiler_params=_params(48))(*flat)
    return [tuple(res[3 * i:3 * i + 3]) for i in range(n)]


def _shard_slab(w_in, w_uq, w_ukv, w_out):
    mixed = jnp.concatenate(
        [_pad_rows(w_uq.T, R_MIX), w_ukv.T, jnp.zeros((R_MIX, 1024 - Q_LORA - KV_LORA), F32)], axis=1)
    return jnp.concatenate([w_out, mixed, w_in.T, jnp.zeros((SLAB_ROWS - O_IN - R_IN, 1024), F32)], axis=0)


def _unpack_weights(slabs):
    w_out = slabs[:, :R_OUT].reshape(D_MODEL, D_MODEL)
    uq = slabs[:, O_MIX:O_MIX + R_UQ, :Q_LORA].reshape(HEADS, QK_DIM, Q_LORA)
    wqt = jnp.pad(uq, ((0, 0), (0, HEAD_PAD - QK_DIM), (0, 0))).reshape(HEADS * HEAD_PAD, Q_LORA)
    ukv = slabs[:, O_MIX:O_MIX + R_MIX, Q_LORA:Q_LORA + KV_LORA].reshape(HEADS, 2, NOPE, KV_LORA)
    wkvt = ukv.transpose(1, 0, 2, 3).reshape(2 * MLA_W, KV_LORA)
    raw = slabs[:, O_IN:O_IN + R_IN].reshape(IN_W, D_MODEL)
    w1t = jnp.concatenate([raw[:768 + ROPE], jnp.zeros((128 - ROPE, D_MODEL), BF16), raw[768 + ROPE:]], axis=0)
    return w1t, wqt, wkvt, w_out


def _in_band(g_w1t):
    g_in = jnp.concatenate([g_w1t[:768 + ROPE], g_w1t[C_GA:]], axis=0).reshape(N_DEV, R_IN, D_MODEL)
    return jnp.pad(g_in, ((0, 0), (0, R_IN_PAD - R_IN), (0, 0)))


def _mixed_band(g_wqt, g_wkvt):
    uq = g_wqt.reshape(HEADS, HEAD_PAD, Q_LORA)[:, :QK_DIM].reshape(N_DEV, R_UQ, Q_LORA)
    uq = jnp.pad(uq, ((0, 0), (0, R_MIX - R_UQ), (0, 0)))
    ukv = g_wkvt.reshape(2, HEADS, NOPE, KV_LORA).transpose(1, 0, 2, 3).reshape(N_DEV, R_MIX, KV_LORA)
    return jnp.concatenate([uq, ukv, jnp.zeros((N_DEV, R_MIX, 1024 - Q_LORA - KV_LORA), F32)], axis=2)


def _rope_rows():
    half = ROPE // 2
    inv_freq = ROPE_THETA ** (-jnp.arange(half, dtype=F32) / half)
    zero, one = jnp.zeros((half,), F32), jnp.ones((half,), F32)
    rows = [jnp.concatenate(r) for r in (
        (inv_freq, inv_freq, zero, zero), (one, one, zero, zero), (-one, zero, zero, zero), (zero, one, zero, zero))]
    return jnp.stack(rows + [jnp.zeros((128,), F32)] * 4)


def _pad_rows(a, rows):
    return jnp.pad(a, ((0, rows - a.shape[0]), (0, 0)))


def kernel(x, positions, w_in, q_norm_g, w_uq, kv_norm_g, w_ukv, pool_w, pool_scale, w_out, ln_g, ln_b, loss_target, m_w_in, m_q_norm_g, m_w_uq, m_kv_norm_g, m_w_ukv, m_pool_w, m_pool_scale, m_w_out, m_ln_g, m_ln_b, v_w_in, v_q_norm_g, v_w_uq, v_kv_norm_g, v_w_ukv, v_pool_w, v_pool_scale, v_w_out, v_ln_g, v_ln_b):
    bb, ss, _ = x.shape
    tt = bb * ss
    tile = min(256, ss)
    atile = min(512, ss)
    nblk = ss // atile

    slab = _shard_slab(w_in, w_uq, w_ukv, w_out).astype(BF16)
    slabs = _all_gather(slab, "gather_weights").reshape(N_DEV, SLAB_ROWS, 1024)
    w1, wq, wkv, wo = _unpack_weights(slabs)

    gq, gkv = q_norm_g.reshape(1, Q_LORA), kv_norm_g.reshape(1, KV_LORA)
    ps = pool_scale.reshape(1, POOL_W)
    pos_col = positions.reshape(bb, ss, 1)
    pos_blk = positions.reshape(bb, nblk, 1, atile)
    qmax = jnp.max(positions.reshape(bb, nblk, atile), axis=-1).reshape(-1)
    kmin = jnp.min(positions.reshape(bb, nblk, atile), axis=-1).reshape(-1)

    x2 = x.reshape(tt, D_MODEL)
    xq, xkv, ga, u, gb, qn, kvn, qcat, kcat, v, rc, rsa, rsb = _fwd_in(
        x2, w1, gq, gkv, wq, wkv, positions.reshape(tt, 1), _rope_rows(), tile)
    as3 = lambda a: a.reshape(bb, ss, a.shape[-1])
    qcat, kcat, v = as3(qcat), as3(kcat), as3(v)
    o, lse = _attn_fwd(qcat, kcat, v, pos_col, pos_blk, qmax, kmin, atile, atile)
    dz, ycat, do, dga, dgb, dpc, dl, loss_p, g_lng, g_lnb, g_ps, g_pw = _post(
        x, loss_target, o, as3(ga), as3(u), as3(gb), wo, pool_w.astype(BF16), ps, ln_g, ln_b, tile)

    bt = min(512, tt)
    g_wo, = _grad_w([(ycat.reshape(tt, D_MODEL), dz.reshape(tt, D_MODEL))], bt, "grad_w_out")
    rows5 = lambda a: a.reshape(bb, HEADS, nblk, 1, atile)
    rc, rsa, rsb = as3(rc), as3(rsa), as3(rsb)
    dqp, dkn, dv, dkr = _attn_bwd(
        qcat, kcat, v, do, rows5(lse), rows5(dl), rc, rsa, rsb, pos_col, pos_blk, qmax, kmin, atile, atile)
    grad_x, dh, g_gq, g_gkv = _bwd_mid(
        dqp, dkn, dv, dkr, as3(xq), as3(xkv), rc, rsa, rsb, wq, wkv, gq, gkv, dga, dgb, dpc, dz, w1, tile)
    g_w1, rs_out = _grad_w([(dh.reshape(tt, IN_WP), x2)], bt, "grad_w_in",
                           carried=g_wo.reshape(N_DEV, R_OUT, D_MODEL))
    g_wq, g_wkn, g_wv, rs_in = _grad_w(
        [(dqp.reshape(tt, HEADS * HEAD_PAD), qn), (dkn.reshape(tt, MLA_W), kvn), (dv.reshape(tt, MLA_W), kvn)], bt,
        "grad_w_uqkv", carried=_in_band(g_w1))
    g_wkv = jnp.concatenate([g_wkn, g_wv], axis=0)
    small = jnp.concatenate(
        [_pad_rows(g_lng.reshape(8, 128), 8), _pad_rows(g_lnb.reshape(8, 128), 8), _pad_rows(g_gq.reshape(4, 128), 8),
         _pad_rows(g_gkv.reshape(2, 128), 8), _pad_rows(g_ps.reshape(4, 128), 8), g_pw.reshape(POOL_G * POOL_C, 128),
         loss_p], axis=0)
    small, rs_mix = _tail_exchange(small, _mixed_band(g_wq, g_wkv), "tail_exchange")
    loss = small[40 + POOL_G * POOL_C, 0]
    grads = {
        "w_in": rs_in[:R_IN],
        "q_norm_g": small[16:20].reshape(1, Q_LORA),
        "w_uq": rs_mix[:R_UQ, :Q_LORA],
        "kv_norm_g": small[24:26].reshape(1, KV_LORA),
        "w_ukv": rs_mix[:, Q_LORA:Q_LORA + KV_LORA].T,
        "pool_w": small[40:40 + POOL_G * POOL_C],
        "pool_scale": small[32:36].reshape(1, POOL_W),
        "w_out": rs_out,
        "ln_g": small[0:8].reshape(1, D_MODEL),
        "ln_b": small[8:16].reshape(1, D_MODEL),
    }
    transposed = ("w_in", "w_uq")

    names = ["w_in", "q_norm_g", "w_uq", "kv_norm_g", "w_ukv", "pool_w", "pool_scale", "w_out", "ln_g", "ln_b"]
    weights = dict(w_in=w_in, q_norm_g=q_norm_g, w_uq=w_uq, kv_norm_g=kv_norm_g, w_ukv=w_ukv, pool_w=pool_w,
                   pool_scale=pool_scale, w_out=w_out, ln_g=ln_g, ln_b=ln_b)
    moms = dict(w_in=(m_w_in, v_w_in), q_norm_g=(m_q_norm_g, v_q_norm_g), w_uq=(m_w_uq, v_w_uq),
                kv_norm_g=(m_kv_norm_g, v_kv_norm_g), w_ukv=(m_w_ukv, v_w_ukv), pool_w=(m_pool_w, v_pool_w),
                pool_scale=(m_pool_scale, v_pool_scale), w_out=(m_w_out, v_w_out), ln_g=(m_ln_g, v_ln_g),
                ln_b=(m_ln_b, v_ln_b))
    as2 = lambda a, n: a.T if n in transposed else a.reshape(grads[n].shape)
    upd = _adamw([(as2(weights[n], n), grads[n], as2(moms[n][0], n), as2(moms[n][1], n)) for n in names])
    shaped = lambda a, n: a.T if n in transposed else a.reshape(weights[n].shape)
    return (loss, grad_x,
            *[shaped(grads[n], n) for n in names],
            *[shaped(upd[i][0], n) for i, n in enumerate(names)],
            *[shaped(upd[i][1], n) for i, n in enumerate(names)],
            *[shaped(upd[i][2], n) for i, n in enumerate(names)])
```

```python
import functools

import jax
import jax.numpy as jnp
from jax import lax
from jax.experimental import pallas as pl
from jax.experimental.pallas import tpu as pltpu

F32 = jnp.float32
BF16 = jnp.bfloat16
MESH = pl.DeviceIdType.MESH

N_DEV = 8
D_MODEL = 1024
HEADS = 4
NOPE = 128
ROPE = 64
V_DIM = 128
QK_DIM = NOPE + ROPE
HEAD_PAD = 256
Q_LORA = 512
KV_LORA = 256
MLA_W = HEADS * V_DIM
POOL_W = 512
POOL_G = 4
POOL_C = 128
POOL_HALO = 16
IN_W = 2368
IN_WP = 2432
C_XQ, C_XKV, C_KR, C_GA, C_U, C_GB = 0, 512, 768, 896, 1408, 1920
ROPE_THETA = 10000.0
RMS_EPS = 1e-6
LN_EPS = 1e-5
ALPHA = 2.0 ** 0.25
SCALE = QK_DIM ** -0.5
SCALE_LOG2 = SCALE * 1.4426950408889634
NEG = float(jnp.finfo(jnp.float32).min)

ADAM_LR = 0.001
ADAM_B1 = 0.9
ADAM_B2 = 0.999
ADAM_EPS = 1e-08
ADAM_WD = 0.01
ADAM_STEP = 10

R_OUT, R_MIX, R_UQ, R_IN = 128, 128, 96, 296
O_MIX, O_IN = R_OUT, R_OUT + R_MIX
SLAB_ROWS = 560
R_IN_PAD = 304

V7X_VMEM_BYTES = 64 * 1024 * 1024
ATTN_FWD_HEADS_PER_STEP = 4
ATTN_BWD_HEADS_PER_STEP = 2


def _params(vmem_mb, semantics=None):
    assert vmem_mb * 1024 * 1024 < V7X_VMEM_BYTES
    return pltpu.CompilerParams(vmem_limit_bytes=vmem_mb * 1024 * 1024, dimension_semantics=semantics)


def _dot(a, b):
    return jnp.dot(a, b, preferred_element_type=F32)


def _dot_nt(a, b):
    return lax.dot_general(a, b, (((1,), (1,)), ((), ())), preferred_element_type=F32)


def _dot_tn(a, b):
    return lax.dot_general(a, b, (((0,), (0,)), ((), ())), preferred_element_type=F32)


def _rope_fwd(t, c, sa, sb):
    return t * c + pltpu.roll(t, 96, 1) * sa + pltpu.roll(t, 32, 1) * sb


def _rope_bwd(d, c, sa, sb):
    return d * c + pltpu.roll(d * sa, 32, 1) + pltpu.roll(d * sb, 96, 1)


def _silu_parts(g):
    sig = jax.nn.sigmoid(g)
    return g * sig, sig * (1.0 + g * (1.0 - sig))


def _pool_cnt(row0, rows):
    t = row0 + lax.broadcasted_iota(jnp.int32, (rows, POOL_W), 0)
    w = 2 << (lax.broadcasted_iota(jnp.int32, (rows, POOL_W), 1) // POOL_C)
    return jnp.minimum(t + 1, w).astype(F32)


def _pick_groups(s2, s4, s8, s16, lo, hi):
    return jnp.concatenate([s2[lo:hi, 0:128], s4[lo:hi, 128:256], s8[lo:hi, 256:384], s16[lo:hi, 384:512]], axis=1)


AG_SEMS = [pltpu.SemaphoreType.DMA((7,)), pltpu.SemaphoreType.DMA((7,)), pltpu.SemaphoreType.DMA]


def _ag_run(x_ref, out_ref, send_sems, recv_sems, local_sem):
    m_per = x_ref.shape[0]
    x, y, c = lax.axis_index("x"), lax.axis_index("y"), lax.axis_index("c")
    me, sibling = (x, y, c), (x, y, 1 - c)
    chips = [(1 - x, y), (x, 1 - y), (1 - x, 1 - y)]

    def rows(px, py, pc):
        return out_ref.at[pl.ds((4 * px + 2 * py + pc) * m_per, m_per), :]

    def copy(k, block, to, src=None):
        return pltpu.make_async_remote_copy(
            src_ref=rows(*block) if src is None else src, dst_ref=rows(*block),
            send_sem=send_sems.at[k], recv_sem=recv_sems.at[k], device_id=to, device_id_type=MESH)

    mine = pltpu.make_async_copy(x_ref, rows(*me), local_sem)
    mine.start()
    first = [copy(0, me, sibling, src=x_ref)]
    first += [copy(1 + j, me, (*chip, c), src=x_ref) for j, chip in enumerate(chips)]
    for cp in first:
        cp.start()
    passed = [copy(4 + j, (*chip, c), sibling) for j, chip in enumerate(chips)]
    for j, chip in enumerate(chips):
        copy(1 + j, (*chip, c), me).wait_recv()
        passed[j].start()
    copy(0, sibling, me).wait_recv()
    for j, chip in enumerate(chips):
        copy(4 + j, (*chip, 1 - c), me).wait_recv()
    for cp in first + passed:
        cp.wait_send()
    mine.wait()


def _all_gather(shard, name):
    m_per, n = shard.shape
    vmem = pl.BlockSpec(memory_space=pltpu.VMEM)
    return pl.pallas_call(
        _ag_run, name=name, out_shape=jax.ShapeDtypeStruct((N_DEV * m_per, n), shard.dtype),
        in_specs=[vmem], out_specs=vmem, scratch_shapes=AG_SEMS, compiler_params=_params(32))(shard)


def _tail_exchange(small, gslab, name):
    m_per, n = small.shape
    _, rr, ww = gslab.shape

    def body(x_ref, g_hbm, sum_ref, rs_out, gathered, send_sems, recv_sems, local_sem, *rs_scratch):
        start, forward, finish = _rs_phases(g_hbm, rs_out, *rs_scratch)
        start()
        _ag_run(x_ref, gathered, send_sems, recv_sems, local_sem)
        forward()
        acc = gathered[pl.ds(0, m_per), :]
        for d in range(1, N_DEV):
            acc = acc + gathered[pl.ds(d * m_per, m_per), :]
        sum_ref[...] = acc
        finish()

    vmem = pl.BlockSpec(memory_space=pltpu.VMEM)
    return pl.pallas_call(
        body, name=name, in_specs=[vmem, pl.BlockSpec(memory_space=pl.ANY)], out_specs=[vmem, vmem],
        out_shape=[jax.ShapeDtypeStruct((m_per, n), F32), jax.ShapeDtypeStruct((rr, ww), F32)],
        scratch_shapes=[pltpu.VMEM((N_DEV * m_per, n), F32)] + AG_SEMS + _rs_scratch(rr, ww),
        compiler_params=_params(32))(small, gslab)


def _rs_scratch(rr, ww):
    return [pltpu.VMEM((4, rr, ww), F32), pltpu.VMEM((4, rr, ww), F32),
            pltpu.VMEM((3, rr, ww), BF16), pltpu.VMEM((3, rr, ww), BF16),
            pltpu.SemaphoreType.DMA((4,)), pltpu.SemaphoreType.DMA((4,)), pltpu.SemaphoreType.DMA((4,)),
            pltpu.SemaphoreType.DMA((3,)), pltpu.SemaphoreType.DMA((3,))]


def _rs_phases(g_hbm, out_ref, own_ref, recv1_ref, sendb_ref, recv2_ref, ld_sems, s1_send, s1_recv, s2_send, s2_recv):
    _, rr, ww = g_hbm.shape
    chunk = next(c for c in (128, 80, 64, 48, 32, 16) if rr % c == 0)
    x, y, c = lax.axis_index("x"), lax.axis_index("y"), lax.axis_index("c")
    chips = [(1 - x, y), (x, 1 - y), (1 - x, 1 - y)]

    def loads():
        return [pltpu.make_async_copy(g_hbm.at[2 * k + c], own_ref.at[k], ld_sems.at[k]) for k in range(4)]

    def stage1():
        return [pltpu.make_async_remote_copy(
            src_ref=g_hbm.at[2 * k + (1 - c)], dst_ref=recv1_ref.at[k], send_sem=s1_send.at[k],
            recv_sem=s1_recv.at[k], device_id=(x, y, 1 - c), device_id_type=MESH) for k in range(4)]

    def stage2():
        return [pltpu.make_async_remote_copy(
            src_ref=sendb_ref.at[r], dst_ref=recv2_ref.at[r], send_sem=s2_send.at[r],
            recv_sem=s2_recv.at[r], device_id=(cx, cy, c), device_id_type=MESH) for r, (cx, cy) in enumerate(chips)]

    def start():
        for cp in loads() + stage1():
            cp.start()

    def forward():
        for cp in loads():
            cp.wait()
        for cp in stage1():
            cp.wait_recv()
        sends = stage2()
        for r, (cx, cy) in enumerate(chips):
            kk = 2 * cx + cy

            def pack(i, carry, r=r, kk=kk):
                rows = pl.ds(pl.multiple_of(i * chunk, chunk), chunk)
                sendb_ref[r, rows, :] = (own_ref[kk, rows, :] + recv1_ref[kk, rows, :]).astype(BF16)
                return carry

            lax.fori_loop(0, rr // chunk, pack, 0)
            sends[r].start()

    def finish():
        for cp in stage2():
            cp.wait_recv()
        mine = 2 * x + y

        def total(i, carry):
            rows = pl.ds(pl.multiple_of(i * chunk, chunk), chunk)
            acc = own_ref[mine, rows, :] + recv1_ref[mine, rows, :]
            for r in range(3):
                acc = acc + recv2_ref[r, rows, :].astype(F32)
            out_ref[rows, :] = acc
            return carry

        lax.fori_loop(0, rr // chunk, total, 0)
        for cp in stage1() + stage2():
            cp.wait_send()

    return start, forward, finish


def _fwd_in(x2, w1, gq, gkv, wq, wkv, pos, rope_rows, tm):
    tt = x2.shape[0]

    def body(x_ref, w1_ref, gq_ref, gkv_ref, wq_ref, wkv_ref, pos_ref, rr_ref,
             xq_ref, xkv_ref, ga_ref, u_ref, gb_ref, qn_ref, kvn_ref, qcat_ref, kcat_ref, v_ref,
             c_ref, sa_ref, sb_ref):
        ang = pos_ref[...].astype(F32) * rr_ref[0:1, :]
        cos, sin = jnp.cos(ang), jnp.sin(ang)
        c, sa, sb = cos * rr_ref[1:2, :], sin * rr_ref[2:3, :], sin * rr_ref[3:4, :]
        c_ref[...] = c
        sa_ref[...] = sa
        sb_ref[...] = sb
        h = _dot_nt(x_ref[...].astype(BF16), w1_ref[...])
        xq = h[:, C_XQ:C_XKV]
        xkv = h[:, C_XKV:C_KR]
        xq_ref[...] = xq
        xkv_ref[...] = xkv
        ga_ref[...] = h[:, C_GA:C_U]
        u_ref[...] = h[:, C_U:C_GB]
        gb_ref[...] = h[:, C_GB:IN_WP]
        qn =(xq * lax.rsqrt(jnp.mean(xq * xq, axis=-1, keepdims=True) + RMS_EPS) * gq_ref[...]).astype(BF16)
        kvn = (xkv * lax.rsqrt(jnp.mean(xkv * xkv, axis=-1, keepdims=True) + RMS_EPS) * gkv_ref[...]).astype(BF16)
        qn_ref[...] = qn
        kvn_ref[...] = kvn
        q = _dot_nt(qn, wq_ref[...])
        kv = _dot_nt(kvn, wkv_ref[...])
        kr = _rope_fwd(h[:, C_KR:C_GA], c, sa, sb).astype(BF16)
        for hd in range(HEADS):
            lo = hd * HEAD_PAD
            qcat_ref[:, lo:lo + NOPE] = q[:, lo:lo + NOPE].astype(BF16)
            qcat_ref[:, lo + NOPE:lo + HEAD_PAD] = _rope_fwd(q[:, lo + NOPE:lo + HEAD_PAD], c, sa, sb).astype(BF16)
            kcat_ref[:, lo:lo + NOPE] = kv[:, hd * NOPE:(hd + 1) * NOPE].astype(BF16)
            kcat_ref[:, lo + NOPE:lo + HEAD_PAD] = kr
        v_ref[...] = kv[:, MLA_W:].astype(BF16)

    def tile(w):
        return pl.BlockSpec((tm, w), lambda i: (i, 0))

    def whole(a):
        return pl.BlockSpec(a.shape, lambda i: (0, 0))

    outs = [(Q_LORA, F32), (KV_LORA, F32), (MLA_W, F32), (POOL_W, F32), (POOL_W, F32),
            (Q_LORA, BF16), (KV_LORA, BF16), (HEADS * HEAD_PAD, BF16), (HEADS * HEAD_PAD, BF16), (MLA_W, BF16),
            (128, F32), (128, F32), (128, F32)]
    return pl.pallas_call(
        body, name="fwd_in", grid=(tt // tm,),
        in_specs=[tile(D_MODEL), whole(w1), whole(gq), whole(gkv), whole(wq), whole(wkv), tile(1), whole(rope_rows)],
        out_specs=[tile(w) for w, _ in outs],
        out_shape=[jax.ShapeDtypeStruct((tt, w), dt) for w, dt in outs],
        compiler_params=_params(48, ("arbitrary",)))(x2, w1, gq, gkv, wq, wkv, pos, rope_rows)


def _attn_fwd(qcat, kcat, v, pos_col, pos_blk, qmax, kmin, tq, tk):
    bb, ss, _ = qcat.shape
    nq, nk = ss // tq, ss // tk
    lanes = 128
    hps = ATTN_FWD_HEADS_PER_STEP

    def body(qmax_ref, kmin_ref, q_ref, k_ref, v_ref, pc_ref, pb_ref, o_ref, lse_ref, m_s, acc_s):
        b, qi = pl.program_id(0), pl.program_id(2)
        m_s[...] = jnp.full(m_s.shape, NEG, F32)
        acc_s[...] = jnp.zeros(acc_s.shape, F32)
        pq = pc_ref[...]
        reach = qmax_ref[b * nq + qi]
        ones = jnp.ones((tk, lanes), BF16)

        def step(ki, carry):
            @pl.when(reach >= kmin_ref[b * nk + ki])
            def _():
                rows = pl.ds(pl.multiple_of(ki * tk, tk), tk)
                mask = pq >= pb_ref[ki]
                for hd in range(hps):
                    qk = slice(hd * HEAD_PAD, (hd + 1) * HEAD_PAD)
                    s = _dot_nt(q_ref[:, qk], k_ref[rows, qk]) * SCALE_LOG2
                    s = jnp.where(mask, s, NEG)
                    m_prev = m_s[hd]
                    m_new = jnp.maximum(m_prev, jnp.max(s, axis=-1, keepdims=True))
                    p = jnp.exp2(s - jnp.tile(m_new, (1, tk // lanes)))
                    a = jnp.exp2(m_prev - m_new)
                    vv = jnp.concatenate([v_ref[rows, hd * V_DIM:(hd + 1) * V_DIM], ones], axis=1)
                    acc_s[hd] = jnp.tile(a, (1, 2)) * acc_s[hd] + _dot(p.astype(BF16), vv)
                    m_s[hd] = m_new
            return carry

        lax.fori_loop(0, nk, step, 0)
        for hd in range(hps):
            acc = acc_s[hd]
            l = acc[:, V_DIM:]
            o_ref[:, hd * V_DIM:(hd + 1) * V_DIM] = acc[:, :V_DIM] / l
            lse_ref[hd] = (m_s[hd] + jnp.log2(l)).T[0:1, :]

    grid_spec = pltpu.PrefetchScalarGridSpec(
        num_scalar_prefetch=2, grid=(bb, HEADS // hps, nq),
        in_specs=[
            pl.BlockSpec((None, tq, hps * HEAD_PAD), lambda b, h, i, *_: (b, i, h)),
            pl.BlockSpec((None, ss, hps * HEAD_PAD), lambda b, h, i, *_: (b, 0, h)),
            pl.BlockSpec((None, ss, hps * V_DIM), lambda b, h, i, *_: (b, 0, h)),
            pl.BlockSpec((None, tq, 1), lambda b, h, i, *_: (b, i, 0)),
            pl.BlockSpec((None, nk, 1, tk), lambda b, h, i, *_: (b, 0, 0, 0)),
        ],
        out_specs=[
            pl.BlockSpec((None, tq, hps * V_DIM), lambda b, h, i, *_: (b, i, h)),
            pl.BlockSpec((None, hps, 1, tq), lambda b, h, i, *_: (b, h, 0, i)),
        ],
        scratch_shapes=[pltpu.VMEM((hps, tq, lanes), F32), pltpu.VMEM((hps, tq, 2 * V_DIM), F32)])
    return pl.pallas_call(
        body, name="attn_fwd", grid_spec=grid_spec,
        out_shape=[jax.ShapeDtypeStruct((bb, ss, MLA_W), F32), jax.ShapeDtypeStruct((bb, HEADS, 1, ss), F32)],
        compiler_params=_params(48, ("arbitrary", "arbitrary", "arbitrary")))(qmax, kmin, qcat, kcat, v, pos_col, pos_blk)


def _post(x, tgt, o, ga, u, gb, w_out, pool_wb, pool_scale, ln_g, ln_b, ts):
    bb, ss, _ = x.shape
    nt = ss // ts
    hb = ts // POOL_HALO

    def body(x_ref, tgt_ref, o_ref, ga_ref, u_ref, uh_ref, gb_ref, wo_ref, pw_ref, ps_ref, lg_ref, lb_ref,
             dz_ref, ycat_ref, do_ref, dga_ref, dgb_ref, dpc_ref, dl_ref, loss_ref, glg_ref, glb_ref, gps_ref, gpw_ref):
        step = pl.program_id(0)
        j = step % nt

        @pl.when(step == 0)
        def _():
            for r in (loss_ref, glg_ref, glb_ref, gps_ref, gpw_ref):
                r[...] = jnp.zeros(r.shape, F32)

        o, ga, u, gb = o_ref[...], ga_ref[...], u_ref[...], gb_ref[...]
        sa, dsa = _silu_parts(ga)
        sb, dsb = _silu_parts(gb)
        ext = jnp.concatenate([jnp.where(j > 0, uh_ref[...], 0.0), u], axis=0)
        s2 = ext + pltpu.roll(ext, 1, 0)
        s4 = s2 + pltpu.roll(s2, 2, 0)
        s8 = s4 + pltpu.roll(s4, 4, 0)
        s16 = s8 + pltpu.roll(s8, 8, 0)
        cnt = _pool_cnt(j * ts, ts)
        pooled = (_pick_groups(s2, s4, s8, s16, POOL_HALO, POOL_HALO + ts) / cnt - u).astype(BF16)
        mixed = jnp.concatenate(
            [_dot(pooled[:, g * POOL_C:(g + 1) * POOL_C], pw_ref[g]) for g in range(POOL_G)], axis=1)
        ps = ps_ref[...]
        scaled = mixed * ps
        ycat = jnp.concatenate([o * sa, scaled * sb], axis=1).astype(BF16)
        ycat_ref[...] = ycat
        z = ALPHA * x_ref[...] + _dot(ycat, wo_ref[...])
        mu = jnp.mean(z, axis=-1, keepdims=True)
        zc = z - mu
        rstd = lax.rsqrt(jnp.mean(zc * zc, axis=-1, keepdims=True) + LN_EPS)
        xhat = zc * rstd
        lg = lg_ref[...]
        diff = xhat * lg + lb_ref[...] - tgt_ref[...]
        loss_ref[...] += jnp.sum(diff * diff) * (0.5 / D_MODEL)
        dy = diff * (1.0 / D_MODEL)
        glb_ref[...] += jnp.sum(dy, axis=0, keepdims=True)
        glg_ref[...] += jnp.sum(dy * xhat, axis=0, keepdims=True)
        dxh = dy * lg
        dz = rstd * (dxh - jnp.mean(dxh, axis=-1, keepdims=True) - xhat * jnp.mean(dxh * xhat, axis=-1, keepdims=True))
        dz_ref[...] = dz
        dycat = _dot_nt(dz.astype(BF16), wo_ref[...])
        dya, dyb = dycat[:, :MLA_W], dycat[:, MLA_W:]
        do = dya * sa
        do_ref[...] = do.astype(BF16)
        doo = do * o
        for hd in range(HEADS):
            dl_ref[hd] = jnp.sum(doo[:, hd * V_DIM:(hd + 1) * V_DIM].T, axis=0, keepdims=True)
        dga_ref[...] = (dya * o * dsa).astype(BF16)
        dgb_ref[...] = (dyb * scaled * dsb).astype(BF16)
        dscaled = dyb * sb
        gps_ref[...] += jnp.sum(dscaled * mixed, axis=0, keepdims=True)
        dmixed = (dscaled * ps).astype(BF16)
        dpooled = []
        for g in range(POOL_G):
            cols = slice(g * POOL_C, (g + 1) * POOL_C)
            gpw_ref[g] += _dot_tn(pooled[:, cols], dmixed[:, cols])
            dpooled.append(_dot_nt(dmixed[:, cols], pw_ref[g]))
        dpc_ref[...] = jnp.concatenate(dpooled, axis=1) / cnt

    def tile(w):
        return pl.BlockSpec((None, ts, w), lambda i: (i // nt, i % nt, 0))

    def whole(a):
        nd = a.ndim
        return pl.BlockSpec(a.shape, lambda i: (0,) * nd)

    halo = pl.BlockSpec((None, POOL_HALO, POOL_W), lambda i: (i // nt, jnp.maximum((i % nt) * hb - 1, 0), 0))
    acc_shapes = [(8, 128), (1, D_MODEL), (1, D_MODEL), (1, POOL_W), (POOL_G, POOL_C, POOL_C)]
    tile_outs = [(D_MODEL, F32), (D_MODEL, BF16), (MLA_W, BF16), (MLA_W, BF16), (POOL_W, BF16), (POOL_W, F32)]
    return pl.pallas_call(
        body, name="post", grid=(bb * nt,),
        in_specs=[tile(D_MODEL), tile(D_MODEL), tile(MLA_W), tile(MLA_W), tile(POOL_W), halo, tile(POOL_W),
                  whole(w_out), whole(pool_wb), whole(pool_scale), whole(ln_g), whole(ln_b)],
        out_specs=[tile(w) for w, _ in tile_outs]
        + [pl.BlockSpec((None, HEADS, 1, ts), lambda i: (i // nt, 0, 0, i % nt))]
        + [pl.BlockSpec(s, lambda i, n=len(s): (0,) * n) for s in acc_shapes],
        out_shape=[jax.ShapeDtypeStruct((bb, ss, w), dt) for w, dt in tile_outs]
        + [jax.ShapeDtypeStruct((bb, HEADS, 1, ss), F32)]
        + [jax.ShapeDtypeStruct(s, F32) for s in acc_shapes],
        compiler_params=_params(48, ("arbitrary",)))(x, tgt, o, ga, u, u, gb, w_out, pool_wb, pool_scale, ln_g, ln_b)


def _attn_bwd(qcat, kcat, v, do, lse, dl, rc, rsa, rsb, pos_col, pos_blk, qmax, kmin, tq, tk):
    bb, ss, _ = qcat.shape
    nq, nk = ss // tq, ss // tk
    hps = ATTN_BWD_HEADS_PER_STEP

    def body(qmax_ref, kmin_ref, q_ref, k_ref, v_ref, do_ref, lse_ref, dl_ref, c_ref, sa_ref, sb_ref, pc_ref, pb_ref,
             dqp_ref, dkn_ref, dv_ref, dkr_ref, dq_s, dk_s, dv_s):
        b = pl.program_id(0)
        dq_s[...] = jnp.zeros(dq_s.shape, F32)

        def kv_step(ki, carry):
            krows = pl.ds(pl.multiple_of(ki * tk, tk), tk)
            pk = pc_ref[krows, :]
            first = kmin_ref[b * nk + ki]
            dk_s[...] = jnp.zeros(dk_s.shape, F32)
            dv_s[...] = jnp.zeros(dv_s.shape, F32)

            def q_step(qi, c2):
                @pl.when(qmax_ref[b * nq + qi] >= first)
                def _():
                    qrows = pl.ds(pl.multiple_of(qi * tq, tq), tq)
                    mask = pb_ref[qi] >= pk
                    for hd in range(hps):
                        qk = slice(hd * HEAD_PAD, (hd + 1) * HEAD_PAD)
                        vs = slice(hd * V_DIM, (hd + 1) * V_DIM)
                        q = q_ref[qrows, qk]
                        dd = do_ref[qrows, vs]
                        st = _dot_nt(k_ref[krows, qk], q) * SCALE_LOG2
                        st = jnp.where(mask, st, NEG)
                        pt = jnp.exp2(st - lse_ref[hd, qi])
                        dv_s[:, vs] += _dot(pt.astype(BF16), dd)
                        dpt = _dot_nt(v_ref[krows, vs], dd)
                        dst = (pt * (dpt - dl_ref[hd, qi]) * SCALE).astype(BF16)
                        dk_s[:, qk] += _dot(dst, q)
                        dq_s[qrows, qk] += _dot_tn(dst, k_ref[krows, qk])
                return c2

            lax.fori_loop(0, nq, q_step, 0)
            dkr = jnp.zeros((tk, HEAD_PAD - NOPE), F32)
            for hd in range(hps):
                lo = hd * HEAD_PAD
                dkn_ref[krows, hd * NOPE:(hd + 1) * NOPE] = dk_s[:, lo:lo + NOPE].astype(BF16)
                dkr = dkr + dk_s[:, lo + NOPE:lo + HEAD_PAD]
            dkr_ref[krows, :] = dkr
            dv_ref[krows, :] = dv_s[...].astype(BF16)
            return carry

        lax.fori_loop(0, nk, kv_step, 0)
        c, sa, sb = c_ref[...], sa_ref[...], sb_ref[...]
        for hd in range(hps):
            lo = hd * HEAD_PAD
            dqp_ref[:, lo:lo + NOPE] = dq_s[:, lo:lo + NOPE].astype(BF16)
            dqp_ref[:, lo + NOPE:lo + HEAD_PAD] = _rope_bwd(dq_s[:, lo + NOPE:lo + HEAD_PAD], c, sa, sb).astype(BF16)

    def per_head(w):
        return pl.BlockSpec((None, ss, hps * w), lambda b, h, *_: (b, 0, h))

    def rows_of_head():
        return pl.BlockSpec((None, hps, nq, 1, tq), lambda b, h, *_: (b, h, 0, 0, 0))

    def per_batch(w):
        return pl.BlockSpec((None, ss, w), lambda b, h, *_: (b, 0, 0))

    grid_spec = pltpu.PrefetchScalarGridSpec(
        num_scalar_prefetch=2, grid=(bb, HEADS // hps),
        in_specs=[per_head(HEAD_PAD), per_head(HEAD_PAD), per_head(V_DIM), per_head(V_DIM),
                  rows_of_head(), rows_of_head(), per_batch(128), per_batch(128), per_batch(128), per_batch(1),
                  pl.BlockSpec((None, nq, 1, tq), lambda b, h, *_: (b, 0, 0, 0))],
        out_specs=[per_head(HEAD_PAD), per_head(NOPE), per_head(V_DIM),
                   pl.BlockSpec((None, None, ss, HEAD_PAD - NOPE), lambda b, h, *_: (b, h, 0, 0))],
        scratch_shapes=[pltpu.VMEM((ss, hps * HEAD_PAD), F32), pltpu.VMEM((tk, hps * HEAD_PAD), F32),
                        pltpu.VMEM((tk, hps * V_DIM), F32)])
    return pl.pallas_call(
        body, name="attn_bwd", grid_spec=grid_spec,
        out_shape=[jax.ShapeDtypeStruct((bb, ss, HEADS * HEAD_PAD), BF16),
                   jax.ShapeDtypeStruct((bb, ss, MLA_W), BF16),
                   jax.ShapeDtypeStruct((bb, ss, MLA_W), BF16),
                   jax.ShapeDtypeStruct((bb, HEADS // hps, ss, HEAD_PAD - NOPE), F32)],
        compiler_params=_params(56, ("arbitrary", "arbitrary")))(
            qmax, kmin, qcat, kcat, v, do, lse, dl, rc, rsa, rsb, pos_col, pos_blk)


def _bwd_mid(dqp, dkn, dv, dkr, xq, xkv, rc, rsa, rsb, wq, wkv, gq, gkv, dga, dgb, dpc, dz, w1, ts):
    bb, ss, _ = dz.shape
    nt = ss // ts
    hb = ts // POOL_HALO
    groups = dkr.shape[1]

    def body(dqp_ref, dkn_ref, dv_ref, dkr_ref, xq_ref, xkv_ref, c_ref, sa_ref, sb_ref, wq_ref, wkv_ref, gq_ref,
             gkv_ref, dga_ref, dgb_ref, dpc_ref, dph_ref, dz_ref, w1_ref,
             gx_ref, dh_ref, ggq_ref, ggkv_ref):
        step = pl.program_id(0)
        j = step % nt

        @pl.when(step == 0)
        def _():
            ggq_ref[...] = jnp.zeros(ggq_ref.shape, F32)
            ggkv_ref[...] = jnp.zeros(ggkv_ref.shape, F32)

        dkr = dkr_ref[0]
        for g in range(1, groups):
            dkr = dkr + dkr_ref[g]
        dh_ref[:, C_KR:C_GA] = _rope_bwd(dkr, c_ref[...], sa_ref[...], sb_ref[...]).astype(BF16)

        def rms_bwd(x, g, dn):
            inv = lax.rsqrt(jnp.mean(x * x, axis=-1, keepdims=True) + RMS_EPS)
            xh = x * inv
            dxh = dn * g
            return inv * (dxh - xh * jnp.mean(dxh * xh, axis=-1, keepdims=True)), jnp.sum(dn * xh, axis=0, keepdims=True)

        dxq, ggq = rms_bwd(xq_ref[...], gq_ref[...], _dot(dqp_ref[...], wq_ref[...]))
        dkvn = _dot(dkn_ref[...], wkv_ref[:MLA_W, :]) + _dot(dv_ref[...], wkv_ref[MLA_W:, :])
        dxkv, ggkv = rms_bwd(xkv_ref[...], gkv_ref[...], dkvn)
        ggq_ref[...] += ggq
        ggkv_ref[...] += ggkv
        dh_ref[:, C_XQ:C_XKV] = dxq.astype(BF16)
        dh_ref[:, C_XKV:C_KR] = dxkv.astype(BF16)
        dh_ref[:, C_GA:C_U] = dga_ref[...]
        dh_ref[:, C_GB:IN_WP] = dgb_ref[...]
        dpc = dpc_ref[...]
        n = ts + POOL_HALO
        ext = jnp.concatenate([dpc, jnp.where(j < nt - 1, dph_ref[...], 0.0)], axis=0)
        r2 = ext + pltpu.roll(ext, n - 1, 0)
        r4 = r2 + pltpu.roll(r2, n - 2, 0)
        r8 = r4 + pltpu.roll(r4, n - 4, 0)
        r16 = r8 + pltpu.roll(r8, n - 8, 0)
        du = _pick_groups(r2, r4, r8, r16, 0, ts) - dpc * _pool_cnt(j * ts, ts)
        dh_ref[:, C_U:C_GB] = du.astype(BF16)
        gx_ref[...] = ALPHA * dz_ref[...] + _dot(dh_ref[...], w1_ref[...])

    def tile(w):
        return pl.BlockSpec((None, ts, w), lambda i: (i // nt, i % nt, 0))

    def whole(a):
        return pl.BlockSpec(a.shape, lambda i: (0, 0))

    halo = pl.BlockSpec((None, POOL_HALO, POOL_W),
                        lambda i: (i // nt, jnp.minimum((i % nt + 1) * hb, ss // POOL_HALO - 1), 0))
    return pl.pallas_call(
        body, name="bwd_mid", grid=(bb * nt,),
        in_specs=[tile(HEADS * HEAD_PAD), tile(MLA_W), tile(MLA_W),
                  pl.BlockSpec((None, groups, ts, HEAD_PAD - NOPE), lambda i: (i // nt, 0, i % nt, 0)),
                  tile(Q_LORA), tile(KV_LORA),
                  tile(128), tile(128), tile(128), whole(wq), whole(wkv), whole(gq), whole(gkv),
                  tile(MLA_W), tile(POOL_W), tile(POOL_W), halo, tile(D_MODEL), whole(w1)],
        out_specs=[tile(D_MODEL), tile(IN_WP),
                   pl.BlockSpec((1, Q_LORA), lambda i: (0, 0)), pl.BlockSpec((1, KV_LORA), lambda i: (0, 0))],
        out_shape=[jax.ShapeDtypeStruct((bb, ss, D_MODEL), F32), jax.ShapeDtypeStruct((bb, ss, IN_WP), BF16),
                   jax.ShapeDtypeStruct((1, Q_LORA), F32), jax.ShapeDtypeStruct((1, KV_LORA), F32)],
        compiler_params=_params(48, ("arbitrary",)))(
            dqp, dkn, dv, dkr, xq, xkv, rc, rsa, rsb, wq, wkv, gq, gkv, dga, dgb, dpc, dpc, dz, w1)


def _grad_w(pairs, bt, name, carried=None):
    tt = pairs[0][0].shape[0]
    steps = tt // bt
    npairs = len(pairs)

    def body(*refs):
        ab, rest = refs[:2 * npairs], refs[2 * npairs:]
        if carried is None:
            outs = rest
        else:
            g_hbm, outs, rs_out, scratch = rest[0], rest[1:1 + npairs], rest[1 + npairs], rest[2 + npairs:]
            start, forward, finish = _rs_phases(g_hbm, rs_out, *scratch)
        step = pl.program_id(0)

        @pl.when(step == 0)
        def _():
            for o in outs:
                o[...] = jnp.zeros(o.shape, F32)
            if carried is not None:
                start()

        for i in range(npairs):
            outs[i][...] += _dot_tn(ab[2 * i][...].astype(BF16), ab[2 * i + 1][...].astype(BF16))
        if carried is not None:
            pl.when(step == min(2, steps - 1))(forward)
            pl.when(step == steps - 1)(finish)

    in_specs, out_specs, out_shape = [], [], []
    for a, b in pairs:
        m, n = a.shape[1], b.shape[1]
        in_specs += [pl.BlockSpec((bt, m), lambda i: (i, 0)), pl.BlockSpec((bt, n), lambda i: (i, 0))]
        out_specs.append(pl.BlockSpec((m, n), lambda i: (0, 0)))
        out_shape.append(jax.ShapeDtypeStruct((m, n), F32))
    args = [t for p in pairs for t in p]
    scratch = []
    if carried is not None:
        _, rr, ww = carried.shape
        in_specs.append(pl.BlockSpec(memory_space=pl.ANY))
        out_specs.append(pl.BlockSpec((rr, ww), lambda i: (0, 0)))
        out_shape.append(jax.ShapeDtypeStruct((rr, ww), F32))
        args.append(carried)
        scratch = _rs_scratch(rr, ww)
    return pl.pallas_call(
        body, name=name, grid=(steps,), in_specs=in_specs, out_specs=out_specs, out_shape=out_shape,
        scratch_shapes=scratch, compiler_params=_params(56, ("arbitrary",)))(*args)


def _adamw(triples):
    n = len(triples)

    def body(*refs):
        ins, outs = refs[:4 * n], refs[4 * n:]
        for i in range(n):
            w, g, m, v = (r[...] for r in ins[4 * i:4 * i + 4])
            m = ADAM_B1 * m + (1.0 - ADAM_B1) * g
            v = ADAM_B2 * v + (1.0 - ADAM_B2) * jnp.square(g)
            m_hat = m / (1.0 - ADAM_B1 ** ADAM_STEP)
            v_hat = v / (1.0 - ADAM_B2 ** ADAM_STEP)
            outs[3 * i][...] = -ADAM_LR * (m_hat / (jnp.sqrt(v_hat) + ADAM_EPS) + ADAM_WD * w)
            outs[3 * i + 1][...] = m
            outs[3 * i + 2][...] = v

    flat = [a for t in triples for a in t]
    vmem = pl.BlockSpec(memory_space=pltpu.VMEM)
    res = pl.pallas_call(
        body, name="adamw", in_specs=[vmem] * len(flat), out_specs=[vmem] * (3 * n),
        out_shape=[jax.ShapeDtypeStruct(t[0].shape, F32) for t in triples for _ in range(3)],
        compiler_params=_params(48))(*flat)
    return [tuple(res[3 * i:3 * i + 3]) for i in range(n)]


def _shard_slab(w_in, w_uq, w_ukv, w_out):
    mixed = jnp.concatenate(
        [_pad_rows(w_uq.T, R_MIX), w_ukv.T, jnp.zeros((R_MIX, 1024 - Q_LORA - KV_LORA), F32)], axis=1)
    return jnp.concatenate([w_out, mixed, w_in.T, jnp.zeros((SLAB_ROWS - O_IN - R_IN, 1024), F32)], axis=0)


def _unpack_weights(slabs):
    w_out = slabs[:, :R_OUT].reshape(D_MODEL, D_MODEL)
    uq = slabs[:, O_MIX:O_MIX + R_UQ, :Q_LORA].reshape(HEADS, QK_DIM, Q_LORA)
    wqt = jnp.pad(uq, ((0, 0), (0, HEAD_PAD - QK_DIM), (0, 0))).reshape(HEADS * HEAD_PAD, Q_LORA)
    ukv = slabs[:, O_MIX:O_MIX + R_MIX, Q_LORA:Q_LORA + KV_LORA].reshape(HEADS, 2, NOPE, KV_LORA)
    wkvt = ukv.transpose(1, 0, 2, 3).reshape(2 * MLA_W, KV_LORA)
    raw = slabs[:, O_IN:O_IN + R_IN].reshape(IN_W, D_MODEL)
    w1t = jnp.concatenate([raw[:768 + ROPE], jnp.zeros((128 - ROPE, D_MODEL), BF16), raw[768 + ROPE:]], axis=0)
    return w1t, wqt, wkvt, w_out


def _in_band(g_w1t):
    g_in = jnp.concatenate([g_w1t[:768 + ROPE], g_w1t[C_GA:]], axis=0).reshape(N_DEV, R_IN, D_MODEL)
    return jnp.pad(g_in, ((0, 0), (0, R_IN_PAD - R_IN), (0, 0)))


def _mixed_band(g_wqt, g_wkvt):
    uq = g_wqt.reshape(HEADS, HEAD_PAD, Q_LORA)[:, :QK_DIM].reshape(N_DEV, R_UQ, Q_LORA)
    uq = jnp.pad(uq, ((0, 0), (0, R_MIX - R_UQ), (0, 0)))
    ukv = g_wkvt.reshape(2, HEADS, NOPE, KV_LORA).transpose(1, 0, 2, 3).reshape(N_DEV, R_MIX, KV_LORA)
    return jnp.concatenate([uq, ukv, jnp.zeros((N_DEV, R_MIX, 1024 - Q_LORA - KV_LORA), F32)], axis=2)


def _rope_rows():
    half = ROPE // 2
    inv_freq = ROPE_THETA ** (-jnp.arange(half, dtype=F32) / half)
    zero, one = jnp.zeros((half,), F32), jnp.ones((half,), F32)
    rows = [jnp.concatenate(r) for r in (
        (inv_freq, inv_freq, zero, zero), (one, one, zero, zero), (-one, zero, zero, zero), (zero, one, zero, zero))]
    return jnp.stack(rows + [jnp.zeros((128,), F32)] * 4)


def _pad_rows(a, rows):
    return jnp.pad(a, ((0, rows - a.shape[0]), (0, 0)))


def kernel(x, positions, w_in, q_norm_g, w_uq, kv_norm_g, w_ukv, pool_w, pool_scale, w_out, ln_g, ln_b, loss_target, m_w_in, m_q_norm_g, m_w_uq, m_kv_norm_g, m_w_ukv, m_pool_w, m_pool_scale, m_w_out, m_ln_g, m_ln_b, v_w_in, v_q_norm_g, v_w_uq, v_kv_norm_g, v_w_ukv, v_pool_w, v_pool_scale, v_w_out, v_ln_g, v_ln_b):
    bb, ss, _ = x.shape
    tt = bb * ss
    atile = min(512, ss)
    nblk = ss // atile

    slab = _shard_slab(w_in, w_uq, w_ukv, w_out).astype(BF16)
    slabs = _all_gather(slab, "gather_weights").reshape(N_DEV, SLAB_ROWS, 1024)
    w1, wq, wkv, wo = _unpack_weights(slabs)

    gq, gkv = q_norm_g.reshape(1, Q_LORA), kv_norm_g.reshape(1, KV_LORA)
    ps = pool_scale.reshape(1, POOL_W)
    pos_col = positions.reshape(bb, ss, 1)
    pos_blk = positions.reshape(bb, nblk, 1, atile)
    qmax = jnp.max(positions.reshape(bb, nblk, atile), axis=-1).reshape(-1)
    kmin = jnp.min(positions.reshape(bb, nblk, atile), axis=-1).reshape(-1)

    x2 = x.reshape(tt, D_MODEL)
    xq, xkv, ga, u, gb, qn, kvn, qcat, kcat, v, rc, rsa, rsb = _fwd_in(
        x2, w1, gq, gkv, wq, wkv, positions.reshape(tt, 1), _rope_rows(), atile)
    as3 = lambda a: a.reshape(bb, ss, a.shape[-1])
    qcat, kcat, v = as3(qcat), as3(kcat), as3(v)
    o, lse = _attn_fwd(qcat, kcat, v, pos_col, pos_blk, qmax, kmin, atile, atile)
    dz, ycat, do, dga, dgb, dpc, dl, loss_p, g_lng, g_lnb, g_ps, g_pw = _post(
        x, loss_target, o, as3(ga), as3(u), as3(gb), wo, pool_w.astype(BF16), ps, ln_g, ln_b, atile)

    bt = min(1024, tt)
    g_wo, = _grad_w([(ycat.reshape(tt, D_MODEL), dz.reshape(tt, D_MODEL))], bt, "grad_w_out")
    rows5 = lambda a: a.reshape(bb, HEADS, nblk, 1, atile)
    rc, rsa, rsb = as3(rc), as3(rsa), as3(rsb)
    dqp, dkn, dv, dkr = _attn_bwd(
        qcat, kcat, v, do, rows5(lse), rows5(dl), rc, rsa, rsb, pos_col, pos_blk, qmax, kmin, atile, atile)
    grad_x, dh, g_gq, g_gkv = _bwd_mid(
        dqp, dkn, dv, dkr, as3(xq), as3(xkv), rc, rsa, rsb, wq, wkv, gq, gkv, dga, dgb, dpc, dz, w1, atile)
    g_w1, rs_out = _grad_w([(dh.reshape(tt, IN_WP), x2)], bt, "grad_w_in",
                           carried=g_wo.reshape(N_DEV, R_OUT, D_MODEL))
    g_wq, g_wkn, g_wv, rs_in = _grad_w(
        [(dqp.reshape(tt, HEADS * HEAD_PAD), qn), (dkn.reshape(tt, MLA_W), kvn), (dv.reshape(tt, MLA_W), kvn)], bt,
        "grad_w_uqkv", carried=_in_band(g_w1))
    g_wkv = jnp.concatenate([g_wkn, g_wv], axis=0)
    small = jnp.concatenate(
        [_pad_rows(g_lng.reshape(8, 128), 8), _pad_rows(g_lnb.reshape(8, 128), 8), _pad_rows(g_gq.reshape(4, 128), 8),
         _pad_rows(g_gkv.reshape(2, 128), 8), _pad_rows(g_ps.reshape(4, 128), 8), g_pw.reshape(POOL_G * POOL_C, 128),
         loss_p], axis=0)
    small, rs_mix = _tail_exchange(small, _mixed_band(g_wq, g_wkv), "tail_exchange")
    loss = small[40 + POOL_G * POOL_C, 0]
    grads = {
        "w_in": rs_in[:R_IN],
        "q_norm_g": small[16:20].reshape(1, Q_LORA),
        "w_uq": rs_mix[:R_UQ, :Q_LORA],
        "kv_norm_g": small[24:26].reshape(1, KV_LORA),
        "w_ukv": rs_mix[:, Q_LORA:Q_LORA + KV_LORA].T,
        "pool_w": small[40:40 + POOL_G * POOL_C],
        "pool_scale": small[32:36].reshape(1, POOL_W),
        "w_out": rs_out,
        "ln_g": small[0:8].reshape(1, D_MODEL),
        "ln_b": small[8:16].reshape(1, D_MODEL),
    }
    transposed = ("w_in", "w_uq")

    names = ["w_in", "q_norm_g", "w_uq", "kv_norm_g", "w_ukv", "pool_w", "pool_scale", "w_out", "ln_g", "ln_b"]
    weights = dict(w_in=w_in, q_norm_g=q_norm_g, w_uq=w_uq, kv_norm_g=kv_norm_g, w_ukv=w_ukv, pool_w=pool_w,
                   pool_scale=pool_scale, w_out=w_out, ln_g=ln_g, ln_b=ln_b)
    moms = dict(w_in=(m_w_in, v_w_in), q_norm_g=(m_q_norm_g, v_q_norm_g), w_uq=(m_w_uq, v_w_uq),
                kv_norm_g=(m_kv_norm_g, v_kv_norm_g), w_ukv=(m_w_ukv, v_w_ukv), pool_w=(m_pool_w, v_pool_w),
                pool_scale=(m_pool_scale, v_pool_scale), w_out=(m_w_out, v_w_out), ln_g=(m_ln_g, v_ln_g),
                ln_b=(m_ln_b, v_ln_b))
    as2 = lambda a, n: a.T if n in transposed else a.reshape(grads[n].shape)
    upd = _adamw([(as2(weights[n], n), grads[n], as2(moms[n][0], n), as2(moms[n][1], n)) for n in names])
    shaped = lambda a, n: a.T if n in transposed else a.reshape(weights[n].shape)
    return (loss, grad_x,
            *[shaped(grads[n], n) for n in names],
            *[shaped(upd[i][0], n) for i, n in enumerate(names)],
            *[shaped(upd[i][1], n) for i, n in enumerate(names)],
            *[shaped(upd[i][2], n) for i, n in enumerate(names)])
```

```python
import functools

import jax
import jax.numpy as jnp
from jax import lax
from jax.experimental import pallas as pl
from jax.experimental.pallas import tpu as pltpu

F32 = jnp.float32
BF16 = jnp.bfloat16
MESH = pl.DeviceIdType.MESH

N_DEV = 8
D_MODEL = 1024
HEADS = 4
NOPE = 128
ROPE = 64
V_DIM = 128
QK_DIM = NOPE + ROPE
HEAD_PAD = 256
Q_LORA = 512
KV_LORA = 256
MLA_W = HEADS * V_DIM
POOL_W = 512
POOL_G = 4
POOL_C = 128
POOL_HALO = 16
IN_W = 2368
IN_WP = 2432
C_XQ, C_XKV, C_KR, C_GA, C_U, C_GB = 0, 512, 768, 896, 1408, 1920
ROPE_THETA = 10000.0
RMS_EPS = 1e-6
LN_EPS = 1e-5
ALPHA = 2.0 ** 0.25
SCALE = QK_DIM ** -0.5
SCALE_LOG2 = SCALE * 1.4426950408889634
NEG = float(jnp.finfo(jnp.float32).min)

ADAM_LR = 0.001
ADAM_B1 = 0.9
ADAM_B2 = 0.999
ADAM_EPS = 1e-08
ADAM_WD = 0.01
ADAM_STEP = 10

R_OUT, R_MIX, R_UQ, R_IN = 128, 128, 96, 296
O_MIX, O_IN = R_OUT, R_OUT + R_MIX
SLAB_ROWS = 560
R_IN_PAD = 304

V7X_VMEM_BYTES = 64 * 1024 * 1024
ATTN_FWD_HEADS_PER_STEP = 4
ATTN_BWD_HEADS_PER_STEP = 2


def _params(vmem_mb, semantics=None):
    assert vmem_mb * 1024 * 1024 < V7X_VMEM_BYTES
    return pltpu.CompilerParams(vmem_limit_bytes=vmem_mb * 1024 * 1024, dimension_semantics=semantics)


def _dot(a, b):
    return jnp.dot(a, b, preferred_element_type=F32)


def _dot_nt(a, b):
    return lax.dot_general(a, b, (((1,), (1,)), ((), ())), preferred_element_type=F32)


def _dot_tn(a, b):
    return lax.dot_general(a, b, (((0,), (0,)), ((), ())), preferred_element_type=F32)


def _rope_fwd(t, c, sa, sb):
    return t * c + pltpu.roll(t, 96, 1) * sa + pltpu.roll(t, 32, 1) * sb


def _rope_bwd(d, c, sa, sb):
    return d * c + pltpu.roll(d * sa, 32, 1) + pltpu.roll(d * sb, 96, 1)


def _silu_parts(g):
    sig = jax.nn.sigmoid(g)
    return g * sig, sig * (1.0 + g * (1.0 - sig))


def _pool_cnt(row0, rows):
    t = row0 + lax.broadcasted_iota(jnp.int32, (rows, POOL_W), 0)
    w = 2 << (lax.broadcasted_iota(jnp.int32, (rows, POOL_W), 1) // POOL_C)
    return jnp.minimum(t + 1, w).astype(F32)


def _pick_groups(s2, s4, s8, s16, lo, hi):
    return jnp.concatenate([s2[lo:hi, 0:128], s4[lo:hi, 128:256], s8[lo:hi, 256:384], s16[lo:hi, 384:512]], axis=1)


AG_SEMS = [pltpu.SemaphoreType.DMA((7,)), pltpu.SemaphoreType.DMA((7,)), pltpu.SemaphoreType.DMA]


def _ag_run(x_ref, out_ref, send_sems, recv_sems, local_sem):
    m_per = x_ref.shape[0]
    x, y, c = lax.axis_index("x"), lax.axis_index("y"), lax.axis_index("c")
    me, sibling = (x, y, c), (x, y, 1 - c)
    chips = [(1 - x, y), (x, 1 - y), (1 - x, 1 - y)]

    def rows(px, py, pc):
        return out_ref.at[pl.ds((4 * px + 2 * py + pc) * m_per, m_per), :]

    def copy(k, block, to, src=None):
        return pltpu.make_async_remote_copy(
            src_ref=rows(*block) if src is None else src, dst_ref=rows(*block),
            send_sem=send_sems.at[k], recv_sem=recv_sems.at[k], device_id=to, device_id_type=MESH)

    mine = pltpu.make_async_copy(x_ref, rows(*me), local_sem)
    mine.start()
    first = [copy(0, me, sibling, src=x_ref)]
    first += [copy(1 + j, me, (*chip, c), src=x_ref) for j, chip in enumerate(chips)]
    for cp in first:
        cp.start()
    passed = [copy(4 + j, (*chip, c), sibling) for j, chip in enumerate(chips)]
    for j, chip in enumerate(chips):
        copy(1 + j, (*chip, c), me).wait_recv()
        passed[j].start()
    copy(0, sibling, me).wait_recv()
    for j, chip in enumerate(chips):
        copy(4 + j, (*chip, 1 - c), me).wait_recv()
    for cp in first + passed:
        cp.wait_send()
    mine.wait()


def _all_gather(shard, name):
    m_per, n = shard.shape
    vmem = pl.BlockSpec(memory_space=pltpu.VMEM)
    return pl.pallas_call(
        _ag_run, name=name, out_shape=jax.ShapeDtypeStruct((N_DEV * m_per, n), shard.dtype),
        in_specs=[vmem], out_specs=vmem, scratch_shapes=AG_SEMS, compiler_params=_params(32))(shard)


def _tail_exchange(small, gslab, name):
    m_per, n = small.shape
    sc = _blocks(gslab)
    rr, ww = sc.rows, sc.width

    def body(x_ref, g_hbm, sum_ref, rs_out, gathered, send_sems, recv_sems, local_sem, *rs_scratch):
        start, forward, finish = _rs_phases(sc, g_hbm, rs_out, *rs_scratch)
        start()
        _ag_run(x_ref, gathered, send_sems, recv_sems, local_sem)
        forward()
        acc = gathered[pl.ds(0, m_per), :]
        for d in range(1, N_DEV):
            acc = acc + gathered[pl.ds(d * m_per, m_per), :]
        sum_ref[...] = acc
        finish()

    vmem = pl.BlockSpec(memory_space=pltpu.VMEM)
    return pl.pallas_call(
        body, name=name, in_specs=[vmem, pl.BlockSpec(memory_space=pl.ANY)], out_specs=[vmem, vmem],
        out_shape=[jax.ShapeDtypeStruct((m_per, n), F32), jax.ShapeDtypeStruct((rr, ww), F32)],
        scratch_shapes=[pltpu.VMEM((N_DEV * m_per, n), F32)] + AG_SEMS + _rs_scratch(sc),
        compiler_params=_params(32))(small, gslab)


class _Scattered:
    def __init__(self, array, rows, pieces, locate):
        self.array, self.rows, self.pieces, self.locate = array, rows, pieces, locate
        self.width = array.shape[-1]


def _blocks(g):
    return _Scattered(g, g.shape[1], ((0, g.shape[1]),), lambda ref, d, row, rows: ref.at[d])


def _w_in_blocks(g_w1t):
    def locate(ref, d, row, rows):
        r = R_IN * d + row
        return ref.at[pl.ds(pl.multiple_of(r + jnp.where(r >= C_KR + ROPE, 128 - ROPE, 0), 8), rows), :]

    cut = C_KR + ROPE - 2 * R_IN
    return _Scattered(g_w1t, R_IN_PAD, ((0, cut), (cut, R_IN - cut)), locate)


def _rs_scratch(sc):
    rr, ww, n = sc.rows, sc.width, 4 * len(sc.pieces)
    return [pltpu.VMEM((4, rr, ww), F32), pltpu.VMEM((4, rr, ww), F32),
            pltpu.VMEM((3, rr, ww), BF16), pltpu.VMEM((3, rr, ww), BF16),
            pltpu.SemaphoreType.DMA((n,)), pltpu.SemaphoreType.DMA((n,)), pltpu.SemaphoreType.DMA((n,)),
            pltpu.SemaphoreType.DMA((3,)), pltpu.SemaphoreType.DMA((3,))]


def _rs_phases(sc, g_hbm, out_ref, own_ref, recv1_ref, sendb_ref, recv2_ref, ld_sems, s1_send, s1_recv, s2_send, s2_recv):
    rr, ww = out_ref.shape
    chunk = next(c for c in (128, 80, 64, 48, 32, 16) if rr % c == 0)
    x, y, c = lax.axis_index("x"), lax.axis_index("y"), lax.axis_index("c")
    chips = [(1 - x, y), (x, 1 - y), (1 - x, 1 - y)]
    npieces = len(sc.pieces)
    filled = sum(rows for _, rows in sc.pieces)

    def pieces(d, buf, k):
        for p, (row, rows) in enumerate(sc.pieces):
            dst = buf.at[k] if (row, rows) == (0, rr) else buf.at[k, pl.ds(row, rows), :]
            yield k * npieces + p, sc.locate(g_hbm, d, row, rows), dst

    def loads():
        return [pltpu.make_async_copy(src, dst, ld_sems.at[s])
                for k in range(4) for s, src, dst in pieces(2 * k + c, own_ref, k)]

    def stage1():
        return [pltpu.make_async_remote_copy(
            src_ref=src, dst_ref=dst, send_sem=s1_send.at[s], recv_sem=s1_recv.at[s],
            device_id=(x, y, 1 - c), device_id_type=MESH)
            for k in range(4) for s, src, dst in pieces(2 * k + (1 - c), recv1_ref, k)]

    def stage2():
        return [pltpu.make_async_remote_copy(
            src_ref=sendb_ref.at[r], dst_ref=recv2_ref.at[r], send_sem=s2_send.at[r],
            recv_sem=s2_recv.at[r], device_id=(cx, cy, c), device_id_type=MESH) for r, (cx, cy) in enumerate(chips)]

    def start():
        if filled < rr:
            own_ref[:, filled:rr, :] = jnp.zeros((4, rr - filled, ww), F32)
            recv1_ref[:, filled:rr, :] = jnp.zeros((4, rr - filled, ww), F32)
        for cp in loads() + stage1():
            cp.start()

    def forward():
        for cp in loads():
            cp.wait()
        for cp in stage1():
            cp.wait_recv()
        sends = stage2()
        for r, (cx, cy) in enumerate(chips):
            kk = 2 * cx + cy

            def pack(i, carry, r=r, kk=kk):
                rows = pl.ds(pl.multiple_of(i * chunk, chunk), chunk)
                sendb_ref[r, rows, :] = (own_ref[kk, rows, :] + recv1_ref[kk, rows, :]).astype(BF16)
                return carry

            lax.fori_loop(0, rr // chunk, pack, 0)
            sends[r].start()

    def finish():
        for cp in stage2():
            cp.wait_recv()
        mine = 2 * x + y

        def total(i, carry):
            rows = pl.ds(pl.multiple_of(i * chunk, chunk), chunk)
            acc = own_ref[mine, rows, :] + recv1_ref[mine, rows, :]
            for r in range(3):
                acc = acc + recv2_ref[r, rows, :].astype(F32)
            out_ref[rows, :] = acc
            return carry

        lax.fori_loop(0, rr // chunk, total, 0)
        for cp in stage1() + stage2():
            cp.wait_send()

    return start, forward, finish


def _fwd_in(x2, w1, gq, gkv, wq, wkv, pos, rope_rows, tm):
    tt = x2.shape[0]

    def body(x_ref, w1_ref, gq_ref, gkv_ref, wq_ref, wkv_ref, pos_ref, rr_ref,
             xq_ref, xkv_ref, ga_ref, u_ref, gb_ref, qn_ref, kvn_ref, qcat_ref, kcat_ref, v_ref,
             c_ref, sa_ref, sb_ref):
        ang = pos_ref[...].astype(F32) * rr_ref[0:1, :]
        cos, sin = jnp.cos(ang), jnp.sin(ang)
        c, sa, sb = cos * rr_ref[1:2, :], sin * rr_ref[2:3, :], sin * rr_ref[3:4, :]
        c_ref[...] = c
        sa_ref[...] = sa
        sb_ref[...] = sb
        h = _dot_nt(x_ref[...].astype(BF16), w1_ref[...])
        xq = h[:, C_XQ:C_XKV]
        xkv = h[:, C_XKV:C_KR]
        xq_ref[...] = xq
        xkv_ref[...] = xkv
        ga_ref[...] = h[:, C_GA:C_U]
        u_ref[...] = h[:, C_U:C_GB]
        gb_ref[...] = h[:, C_GB:IN_WP]
        qn =(xq * lax.rsqrt(jnp.mean(xq * xq, axis=-1, keepdims=True) + RMS_EPS) * gq_ref[...]).astype(BF16)
        kvn = (xkv * lax.rsqrt(jnp.mean(xkv * xkv, axis=-1, keepdims=True) + RMS_EPS) * gkv_ref[...]).astype(BF16)
        qn_ref[...] = qn
        kvn_ref[...] = kvn
        q = _dot_nt(qn, wq_ref[...])
        kv = _dot_nt(kvn, wkv_ref[...])
        kr = _rope_fwd(h[:, C_KR:C_GA], c, sa, sb).astype(BF16)
        for hd in range(HEADS):
            lo = hd * HEAD_PAD
            qcat_ref[:, lo:lo + NOPE] = q[:, lo:lo + NOPE].astype(BF16)
            qcat_ref[:, lo + NOPE:lo + HEAD_PAD] = _rope_fwd(q[:, lo + NOPE:lo + HEAD_PAD], c, sa, sb).astype(BF16)
            kcat_ref[:, lo:lo + NOPE] = kv[:, hd * NOPE:(hd + 1) * NOPE].astype(BF16)
            kcat_ref[:, lo + NOPE:lo + HEAD_PAD] = kr
        v_ref[...] = kv[:, MLA_W:].astype(BF16)

    def tile(w):
        return pl.BlockSpec((tm, w), lambda i: (i, 0))

    def whole(a):
        return pl.BlockSpec(a.shape, lambda i: (0, 0))

    outs = [(Q_LORA, F32), (KV_LORA, F32), (MLA_W, F32), (POOL_W, F32), (POOL_W, F32),
            (Q_LORA, BF16), (KV_LORA, BF16), (HEADS * HEAD_PAD, BF16), (HEADS * HEAD_PAD, BF16), (MLA_W, BF16),
            (128, F32), (128, F32), (128, F32)]
    return pl.pallas_call(
        body, name="fwd_in", grid=(tt // tm,),
        in_specs=[tile(D_MODEL), whole(w1), whole(gq), whole(gkv), whole(wq), whole(wkv), tile(1), whole(rope_rows)],
        out_specs=[tile(w) for w, _ in outs],
        out_shape=[jax.ShapeDtypeStruct((tt, w), dt) for w, dt in outs],
        compiler_params=_params(48, ("arbitrary",)))(x2, w1, gq, gkv, wq, wkv, pos, rope_rows)


def _attn_fwd(qcat, kcat, v, pos_col, pos_blk, qmax, kmin, tq, tk):
    bb, ss, _ = qcat.shape
    nq, nk = ss // tq, ss // tk
    lanes = 128
    hps = ATTN_FWD_HEADS_PER_STEP

    def body(qmax_ref, kmin_ref, q_ref, k_ref, v_ref, pc_ref, pb_ref, o_ref, lse_ref, m_s, acc_s):
        b, qi = pl.program_id(0), pl.program_id(2)
        m_s[...] = jnp.full(m_s.shape, NEG, F32)
        acc_s[...] = jnp.zeros(acc_s.shape, F32)
        pq = pc_ref[...]
        reach = qmax_ref[b * nq + qi]
        ones = jnp.ones((tk, lanes), BF16)

        def step(ki, carry):
            @pl.when(reach >= kmin_ref[b * nk + ki])
            def _():
                rows = pl.ds(pl.multiple_of(ki * tk, tk), tk)
                mask = pq >= pb_ref[ki]
                for hd in range(hps):
                    qk = slice(hd * HEAD_PAD, (hd + 1) * HEAD_PAD)
                    s = _dot_nt(q_ref[:, qk], k_ref[rows, qk]) * SCALE_LOG2
                    s = jnp.where(mask, s, NEG)
                    m_prev = m_s[hd]
                    m_new = jnp.maximum(m_prev, jnp.max(s, axis=-1, keepdims=True))
                    p = jnp.exp2(s - jnp.tile(m_new, (1, tk // lanes)))
                    a = jnp.exp2(m_prev - m_new)
                    vv = jnp.concatenate([v_ref[rows, hd * V_DIM:(hd + 1) * V_DIM], ones], axis=1)
                    acc_s[hd] = jnp.tile(a, (1, 2)) * acc_s[hd] + _dot(p.astype(BF16), vv)
                    m_s[hd] = m_new
            return carry

        lax.fori_loop(0, nk, step, 0)
        for hd in range(hps):
            acc = acc_s[hd]
            l = acc[:, V_DIM:]
            o_ref[:, hd * V_DIM:(hd + 1) * V_DIM] = acc[:, :V_DIM] / l
            lse_ref[hd] = (m_s[hd] + jnp.log2(l)).T[0:1, :]

    grid_spec = pltpu.PrefetchScalarGridSpec(
        num_scalar_prefetch=2, grid=(bb, HEADS // hps, nq),
        in_specs=[
            pl.BlockSpec((None, tq, hps * HEAD_PAD), lambda b, h, i, *_: (b, i, h)),
            pl.BlockSpec((None, ss, hps * HEAD_PAD), lambda b, h, i, *_: (b, 0, h)),
            pl.BlockSpec((None, ss, hps * V_DIM), lambda b, h, i, *_: (b, 0, h)),
            pl.BlockSpec((None, tq, 1), lambda b, h, i, *_: (b, i, 0)),
            pl.BlockSpec((None, nk, 1, tk), lambda b, h, i, *_: (b, 0, 0, 0)),
        ],
        out_specs=[
            pl.BlockSpec((None, tq, hps * V_DIM), lambda b, h, i, *_: (b, i, h)),
            pl.BlockSpec((None, hps, 1, tq), lambda b, h, i, *_: (b, h, 0, i)),
        ],
        scratch_shapes=[pltpu.VMEM((hps, tq, lanes), F32), pltpu.VMEM((hps, tq, 2 * V_DIM), F32)])
    return pl.pallas_call(
        body, name="attn_fwd", grid_spec=grid_spec,
        out_shape=[jax.ShapeDtypeStruct((bb, ss, MLA_W), F32), jax.ShapeDtypeStruct((bb, HEADS, 1, ss), F32)],
        compiler_params=_params(48, ("arbitrary", "arbitrary", "arbitrary")))(qmax, kmin, qcat, kcat, v, pos_col, pos_blk)


def _post(x, tgt, o, ga, u, gb, w_out, pool_wb, pool_scale, ln_g, ln_b, ts):
    bb, ss, _ = x.shape
    nt = ss // ts
    hb = ts // POOL_HALO

    def body(x_ref, tgt_ref, o_ref, ga_ref, u_ref, uh_ref, gb_ref, wo_ref, pw_ref, ps_ref, lg_ref, lb_ref,
             dz_ref, ycat_ref, do_ref, dga_ref, dgb_ref, dpc_ref, dl_ref, loss_ref, glg_ref, glb_ref, gps_ref, gpw_ref):
        step = pl.program_id(0)
        j = step % nt

        @pl.when(step == 0)
        def _():
            for r in (loss_ref, glg_ref, glb_ref, gps_ref, gpw_ref):
                r[...] = jnp.zeros(r.shape, F32)

        o, ga, u, gb = o_ref[...], ga_ref[...], u_ref[...], gb_ref[...]
        sa, dsa = _silu_parts(ga)
        sb, dsb = _silu_parts(gb)
        ext = jnp.concatenate([jnp.where(j > 0, uh_ref[...], 0.0), u], axis=0)
        s2 = ext + pltpu.roll(ext, 1, 0)
        s4 = s2 + pltpu.roll(s2, 2, 0)
        s8 = s4 + pltpu.roll(s4, 4, 0)
        s16 = s8 + pltpu.roll(s8, 8, 0)
        cnt = _pool_cnt(j * ts, ts)
        pooled = (_pick_groups(s2, s4, s8, s16, POOL_HALO, POOL_HALO + ts) / cnt - u).astype(BF16)
        mixed = jnp.concatenate(
            [_dot(pooled[:, g * POOL_C:(g + 1) * POOL_C], pw_ref[g]) for g in range(POOL_G)], axis=1)
        ps = ps_ref[...]
        scaled = mixed * ps
        ycat = jnp.concatenate([o * sa, scaled * sb], axis=1).astype(BF16)
        ycat_ref[...] = ycat
        z = ALPHA * x_ref[...] + _dot(ycat, wo_ref[...])
        mu = jnp.mean(z, axis=-1, keepdims=True)
        zc = z - mu
        rstd = lax.rsqrt(jnp.mean(zc * zc, axis=-1, keepdims=True) + LN_EPS)
        xhat = zc * rstd
        lg = lg_ref[...]
        diff = xhat * lg + lb_ref[...] - tgt_ref[...]
        loss_ref[...] += jnp.sum(diff * diff) * (0.5 / D_MODEL)
        dy = diff * (1.0 / D_MODEL)
        glb_ref[...] += jnp.sum(dy, axis=0, keepdims=True)
        glg_ref[...] += jnp.sum(dy * xhat, axis=0, keepdims=True)
        dxh = dy * lg
        dz = rstd * (dxh - jnp.mean(dxh, axis=-1, keepdims=True) - xhat * jnp.mean(dxh * xhat, axis=-1, keepdims=True))
        dz_ref[...] = dz
        dycat = _dot_nt(dz.astype(BF16), wo_ref[...])
        dya, dyb = dycat[:, :MLA_W], dycat[:, MLA_W:]
        do = dya * sa
        do_ref[...] = do.astype(BF16)
        doo = do * o
        for hd in range(HEADS):
            dl_ref[hd] = jnp.sum(doo[:, hd * V_DIM:(hd + 1) * V_DIM].T, axis=0, keepdims=True)
        dga_ref[...] = (dya * o * dsa).astype(BF16)
        dgb_ref[...] = (dyb * scaled * dsb).astype(BF16)
        dscaled = dyb * sb
        gps_ref[...] += jnp.sum(dscaled * mixed, axis=0, keepdims=True)
        dmixed = (dscaled * ps).astype(BF16)
        dpooled = []
        for g in range(POOL_G):
            cols = slice(g * POOL_C, (g + 1) * POOL_C)
            gpw_ref[g] += _dot_tn(pooled[:, cols], dmixed[:, cols])
            dpooled.append(_dot_nt(dmixed[:, cols], pw_ref[g]))
        dpc_ref[...] = jnp.concatenate(dpooled, axis=1) / cnt

    def tile(w):
        return pl.BlockSpec((None, ts, w), lambda i: (i // nt, i % nt, 0))

    def whole(a):
        nd = a.ndim
        return pl.BlockSpec(a.shape, lambda i: (0,) * nd)

    halo = pl.BlockSpec((None, POOL_HALO, POOL_W), lambda i: (i // nt, jnp.maximum((i % nt) * hb - 1, 0), 0))
    acc_shapes = [(8, 128), (1, D_MODEL), (1, D_MODEL), (1, POOL_W), (POOL_G, POOL_C, POOL_C)]
    tile_outs = [(D_MODEL, F32), (D_MODEL, BF16), (MLA_W, BF16), (MLA_W, BF16), (POOL_W, BF16), (POOL_W, F32)]
    return pl.pallas_call(
        body, name="post", grid=(bb * nt,),
        in_specs=[tile(D_MODEL), tile(D_MODEL), tile(MLA_W), tile(MLA_W), tile(POOL_W), halo, tile(POOL_W),
                  whole(w_out), whole(pool_wb), whole(pool_scale), whole(ln_g), whole(ln_b)],
        out_specs=[tile(w) for w, _ in tile_outs]
        + [pl.BlockSpec((None, HEADS, 1, ts), lambda i: (i // nt, 0, 0, i % nt))]
        + [pl.BlockSpec(s, lambda i, n=len(s): (0,) * n) for s in acc_shapes],
        out_shape=[jax.ShapeDtypeStruct((bb, ss, w), dt) for w, dt in tile_outs]
        + [jax.ShapeDtypeStruct((bb, HEADS, 1, ss), F32)]
        + [jax.ShapeDtypeStruct(s, F32) for s in acc_shapes],
        compiler_params=_params(48, ("arbitrary",)))(x, tgt, o, ga, u, u, gb, w_out, pool_wb, pool_scale, ln_g, ln_b)


def _attn_bwd(qcat, kcat, v, do, lse, dl, rc, rsa, rsb, pos_col, pos_blk, qmax, kmin, tq, tk):
    bb, ss, _ = qcat.shape
    nq, nk = ss // tq, ss // tk
    hps = ATTN_BWD_HEADS_PER_STEP

    def body(qmax_ref, kmin_ref, q_ref, k_ref, v_ref, do_ref, lse_ref, dl_ref, c_ref, sa_ref, sb_ref, pc_ref, pb_ref,
             dqp_ref, dkn_ref, dv_ref, dkr_ref, dq_s, dk_s, dv_s):
        b = pl.program_id(0)
        dq_s[...] = jnp.zeros(dq_s.shape, F32)

        def kv_step(ki, carry):
            krows = pl.ds(pl.multiple_of(ki * tk, tk), tk)
            pk = pc_ref[krows, :]
            first = kmin_ref[b * nk + ki]
            dk_s[...] = jnp.zeros(dk_s.shape, F32)
            dv_s[...] = jnp.zeros(dv_s.shape, F32)

            def q_step(qi, c2):
                @pl.when(qmax_ref[b * nq + qi] >= first)
                def _():
                    qrows = pl.ds(pl.multiple_of(qi * tq, tq), tq)
                    mask = pb_ref[qi] >= pk
                    for hd in range(hps):
                        qk = slice(hd * HEAD_PAD, (hd + 1) * HEAD_PAD)
                        vs = slice(hd * V_DIM, (hd + 1) * V_DIM)
                        q = q_ref[qrows, qk]
                        dd = do_ref[qrows, vs]
                        st = _dot_nt(k_ref[krows, qk], q) * SCALE_LOG2
                        st = jnp.where(mask, st, NEG)
                        pt = jnp.exp2(st - lse_ref[hd, qi])
                        dv_s[:, vs] += _dot(pt.astype(BF16), dd)
                        dpt = _dot_nt(v_ref[krows, vs], dd)
                        dst = (pt * (dpt - dl_ref[hd, qi]) * SCALE).astype(BF16)
                        dk_s[:, qk] += _dot(dst, q)
                        dq_s[qrows, qk] += _dot_tn(dst, k_ref[krows, qk])
                return c2

            lax.fori_loop(0, nq, q_step, 0)
            dkr = jnp.zeros((tk, HEAD_PAD - NOPE), F32)
            for hd in range(hps):
                lo = hd * HEAD_PAD
                dkn_ref[krows, hd * NOPE:(hd + 1) * NOPE] = dk_s[:, lo:lo + NOPE].astype(BF16)
                dkr = dkr + dk_s[:, lo + NOPE:lo + HEAD_PAD]
            dkr_ref[krows, :] = dkr
            dv_ref[krows, :] = dv_s[...].astype(BF16)
            return carry

        lax.fori_loop(0, nk, kv_step, 0)
        c, sa, sb = c_ref[...], sa_ref[...], sb_ref[...]
        for hd in range(hps):
            lo = hd * HEAD_PAD
            dqp_ref[:, lo:lo + NOPE] = dq_s[:, lo:lo + NOPE].astype(BF16)
            dqp_ref[:, lo + NOPE:lo + HEAD_PAD] = _rope_bwd(dq_s[:, lo + NOPE:lo + HEAD_PAD], c, sa, sb).astype(BF16)

    def per_head(w):
        return pl.BlockSpec((None, ss, hps * w), lambda b, h, *_: (b, 0, h))

    def rows_of_head():
        return pl.BlockSpec((None, hps, nq, 1, tq), lambda b, h, *_: (b, h, 0, 0, 0))

    def per_batch(w):
        return pl.BlockSpec((None, ss, w), lambda b, h, *_: (b, 0, 0))

    grid_spec = pltpu.PrefetchScalarGridSpec(
        num_scalar_prefetch=2, grid=(bb, HEADS // hps),
        in_specs=[per_head(HEAD_PAD), per_head(HEAD_PAD), per_head(V_DIM), per_head(V_DIM),
                  rows_of_head(), rows_of_head(), per_batch(128), per_batch(128), per_batch(128), per_batch(1),
                  pl.BlockSpec((None, nq, 1, tq), lambda b, h, *_: (b, 0, 0, 0))],
        out_specs=[per_head(HEAD_PAD), per_head(NOPE), per_head(V_DIM),
                   pl.BlockSpec((None, None, ss, HEAD_PAD - NOPE), lambda b, h, *_: (b, h, 0, 0))],
        scratch_shapes=[pltpu.VMEM((ss, hps * HEAD_PAD), F32), pltpu.VMEM((tk, hps * HEAD_PAD), F32),
                        pltpu.VMEM((tk, hps * V_DIM), F32)])
    return pl.pallas_call(
        body, name="attn_bwd", grid_spec=grid_spec,
        out_shape=[jax.ShapeDtypeStruct((bb, ss, HEADS * HEAD_PAD), BF16),
                   jax.ShapeDtypeStruct((bb, ss, MLA_W), BF16),
                   jax.ShapeDtypeStruct((bb, ss, MLA_W), BF16),
                   jax.ShapeDtypeStruct((bb, HEADS // hps, ss, HEAD_PAD - NOPE), F32)],
        compiler_params=_params(56, ("arbitrary", "arbitrary")))(
            qmax, kmin, qcat, kcat, v, do, lse, dl, rc, rsa, rsb, pos_col, pos_blk)


def _bwd_mid(dqp, dkn, dv, dkr, xq, xkv, rc, rsa, rsb, wq, wkv, gq, gkv, dga, dgb, dpc, dz, w1, ts):
    bb, ss, _ = dz.shape
    nt = ss // ts
    hb = ts // POOL_HALO
    groups = dkr.shape[1]

    def body(dqp_ref, dkn_ref, dv_ref, dkr_ref, xq_ref, xkv_ref, c_ref, sa_ref, sb_ref, wq_ref, wkv_ref, gq_ref,
             gkv_ref, dga_ref, dgb_ref, dpc_ref, dph_ref, dz_ref, w1_ref,
             gx_ref, dh_ref, ggq_ref, ggkv_ref):
        step = pl.program_id(0)
        j = step % nt

        @pl.when(step == 0)
        def _():
            ggq_ref[...] = jnp.zeros(ggq_ref.shape, F32)
            ggkv_ref[...] = jnp.zeros(ggkv_ref.shape, F32)

        dkr = dkr_ref[0]
        for g in range(1, groups):
            dkr = dkr + dkr_ref[g]
        dh_ref[:, C_KR:C_GA] = _rope_bwd(dkr, c_ref[...], sa_ref[...], sb_ref[...]).astype(BF16)

        def rms_bwd(x, g, dn):
            inv = lax.rsqrt(jnp.mean(x * x, axis=-1, keepdims=True) + RMS_EPS)
            xh = x * inv
            dxh = dn * g
            return inv * (dxh - xh * jnp.mean(dxh * xh, axis=-1, keepdims=True)), jnp.sum(dn * xh, axis=0, keepdims=True)

        dxq, ggq = rms_bwd(xq_ref[...], gq_ref[...], _dot(dqp_ref[...], wq_ref[...]))
        dkvn = _dot(dkn_ref[...], wkv_ref[:MLA_W, :]) + _dot(dv_ref[...], wkv_ref[MLA_W:, :])
        dxkv, ggkv = rms_bwd(xkv_ref[...], gkv_ref[...], dkvn)
        ggq_ref[...] += ggq
        ggkv_ref[...] += ggkv
        dh_ref[:, C_XQ:C_XKV] = dxq.astype(BF16)
        dh_ref[:, C_XKV:C_KR] = dxkv.astype(BF16)
        dh_ref[:, C_GA:C_U] = dga_ref[...]
        dh_ref[:, C_GB:IN_WP] = dgb_ref[...]
        dpc = dpc_ref[...]
        n = ts + POOL_HALO
        ext = jnp.concatenate([dpc, jnp.where(j < nt - 1, dph_ref[...], 0.0)], axis=0)
        r2 = ext + pltpu.roll(ext, n - 1, 0)
        r4 = r2 + pltpu.roll(r2, n - 2, 0)
        r8 = r4 + pltpu.roll(r4, n - 4, 0)
        r16 = r8 + pltpu.roll(r8, n - 8, 0)
        du = _pick_groups(r2, r4, r8, r16, 0, ts) - dpc * _pool_cnt(j * ts, ts)
        dh_ref[:, C_U:C_GB] = du.astype(BF16)
        gx_ref[...] = ALPHA * dz_ref[...] + _dot(dh_ref[...], w1_ref[...])

    def tile(w):
        return pl.BlockSpec((None, ts, w), lambda i: (i // nt, i % nt, 0))

    def whole(a):
        return pl.BlockSpec(a.shape, lambda i: (0, 0))

    halo = pl.BlockSpec((None, POOL_HALO, POOL_W),
                        lambda i: (i // nt, jnp.minimum((i % nt + 1) * hb, ss // POOL_HALO - 1), 0))
    return pl.pallas_call(
        body, name="bwd_mid", grid=(bb * nt,),
        in_specs=[tile(HEADS * HEAD_PAD), tile(MLA_W), tile(MLA_W),
                  pl.BlockSpec((None, groups, ts, HEAD_PAD - NOPE), lambda i: (i // nt, 0, i % nt, 0)),
                  tile(Q_LORA), tile(KV_LORA),
                  tile(128), tile(128), tile(128), whole(wq), whole(wkv), whole(gq), whole(gkv),
                  tile(MLA_W), tile(POOL_W), tile(POOL_W), halo, tile(D_MODEL), whole(w1)],
        out_specs=[tile(D_MODEL), tile(IN_WP),
                   pl.BlockSpec((1, Q_LORA), lambda i: (0, 0)), pl.BlockSpec((1, KV_LORA), lambda i: (0, 0))],
        out_shape=[jax.ShapeDtypeStruct((bb, ss, D_MODEL), F32), jax.ShapeDtypeStruct((bb, ss, IN_WP), BF16),
                   jax.ShapeDtypeStruct((1, Q_LORA), F32), jax.ShapeDtypeStruct((1, KV_LORA), F32)],
        compiler_params=_params(48, ("arbitrary",)))(
            dqp, dkn, dv, dkr, xq, xkv, rc, rsa, rsb, wq, wkv, gq, gkv, dga, dgb, dpc, dpc, dz, w1)


def _grad_w(pairs, bt, name, carried=None):
    tt = pairs[0][0].shape[0]
    steps = tt // bt
    npairs = len(pairs)

    def body(*refs):
        ab, rest = refs[:2 * npairs], refs[2 * npairs:]
        if carried is None:
            outs = rest
        else:
            g_hbm, outs, rs_out, scratch = rest[0], rest[1:1 + npairs], rest[1 + npairs], rest[2 + npairs:]
            start, forward, finish = _rs_phases(carried, g_hbm, rs_out, *scratch)
        step = pl.program_id(0)

        @pl.when(step == 0)
        def _():
            for o in outs:
                o[...] = jnp.zeros(o.shape, F32)
            if carried is not None:
                start()

        for i in range(npairs):
            outs[i][...] += _dot_tn(ab[2 * i][...].astype(BF16), ab[2 * i + 1][...].astype(BF16))
        if carried is not None:
            pl.when(step == min(2, steps - 1))(forward)
            pl.when(step == steps - 1)(finish)

    in_specs, out_specs, out_shape = [], [], []
    for a, b in pairs:
        m, n = a.shape[1], b.shape[1]
        in_specs += [pl.BlockSpec((bt, m), lambda i: (i, 0)), pl.BlockSpec((bt, n), lambda i: (i, 0))]
        out_specs.append(pl.BlockSpec((m, n), lambda i: (0, 0)))
        out_shape.append(jax.ShapeDtypeStruct((m, n), F32))
    args = [t for p in pairs for t in p]
    scratch = []
    if carried is not None:
        rr, ww = carried.rows, carried.width
        in_specs.append(pl.BlockSpec(memory_space=pl.ANY))
        out_specs.append(pl.BlockSpec((rr, ww), lambda i: (0, 0)))
        out_shape.append(jax.ShapeDtypeStruct((rr, ww), F32))
        args.append(carried.array)
        scratch = _rs_scratch(carried)
    return pl.pallas_call(
        body, name=name, grid=(steps,), in_specs=in_specs, out_specs=out_specs, out_shape=out_shape,
        scratch_shapes=scratch, compiler_params=_params(56, ("arbitrary",)))(*args)


def _adamw(triples):
    n = len(triples)

    def body(*refs):
        ins, outs = refs[:4 * n], refs[4 * n:]
        for i in range(n):
            w, g, m, v = (r[...] for r in ins[4 * i:4 * i + 4])
            m = ADAM_B1 * m + (1.0 - ADAM_B1) * g
            v = ADAM_B2 * v + (1.0 - ADAM_B2) * jnp.square(g)
            m_hat = m / (1.0 - ADAM_B1 ** ADAM_STEP)
            v_hat = v / (1.0 - ADAM_B2 ** ADAM_STEP)
            outs[3 * i][...] = -ADAM_LR * (m_hat / (jnp.sqrt(v_hat) + ADAM_EPS) + ADAM_WD * w)
            outs[3 * i + 1][...] = m
            outs[3 * i + 2][...] = v

    flat = [a for t in triples for a in t]
    vmem = pl.BlockSpec(memory_space=pltpu.VMEM)
    res = pl.pallas_call(
        body, name="adamw", in_specs=[vmem] * len(flat), out_specs=[vmem] * (3 * n),
        out_shape=[jax.ShapeDtypeStruct(t[0].shape, F32) for t in triples for _ in range(3)],
        compiler_params=_params(48))(*flat)
    return [tuple(res[3 * i:3 * i + 3]) for i in range(n)]


def _shard_slab(w_in, w_uq, w_ukv, w_out):
    mixed = jnp.concatenate(
        [_pad_rows(w_uq.T, R_MIX), w_ukv.T, jnp.zeros((R_MIX, 1024 - Q_LORA - KV_LORA), F32)], axis=1)
    return jnp.concatenate([w_out, mixed, w_in.T, jnp.zeros((SLAB_ROWS - O_IN - R_IN, 1024), F32)], axis=0)


def _unpack_weights(slabs):
    w_out = slabs[:, :R_OUT].reshape(D_MODEL, D_MODEL)
    uq = slabs[:, O_MIX:O_MIX + R_UQ, :Q_LORA].reshape(HEADS, QK_DIM, Q_LORA)
    wqt = jnp.pad(uq, ((0, 0), (0, HEAD_PAD - QK_DIM), (0, 0))).reshape(HEADS * HEAD_PAD, Q_LORA)
    ukv = slabs[:, O_MIX:O_MIX + R_MIX, Q_LORA:Q_LORA + KV_LORA].reshape(HEADS, 2, NOPE, KV_LORA)
    wkvt = ukv.transpose(1, 0, 2, 3).reshape(2 * MLA_W, KV_LORA)
    raw = slabs[:, O_IN:O_IN + R_IN].reshape(IN_W, D_MODEL)
    w1t = jnp.concatenate([raw[:768 + ROPE], jnp.zeros((128 - ROPE, D_MODEL), BF16), raw[768 + ROPE:]], axis=0)
    return w1t, wqt, wkvt, w_out


def _mixed_band(g_wqt, g_wkvt):
    uq = g_wqt.reshape(HEADS, HEAD_PAD, Q_LORA)[:, :QK_DIM].reshape(N_DEV, R_UQ, Q_LORA)
    uq = jnp.pad(uq, ((0, 0), (0, R_MIX - R_UQ), (0, 0)))
    ukv = g_wkvt.reshape(2, HEADS, NOPE, KV_LORA).transpose(1, 0, 2, 3).reshape(N_DEV, R_MIX, KV_LORA)
    return jnp.concatenate([uq, ukv, jnp.zeros((N_DEV, R_MIX, 1024 - Q_LORA - KV_LORA), F32)], axis=2)


def _rope_rows():
    half = ROPE // 2
    inv_freq = ROPE_THETA ** (-jnp.arange(half, dtype=F32) / half)
    zero, one = jnp.zeros((half,), F32), jnp.ones((half,), F32)
    rows = [jnp.concatenate(r) for r in (
        (inv_freq, inv_freq, zero, zero), (one, one, zero, zero), (-one, zero, zero, zero), (zero, one, zero, zero))]
    return jnp.stack(rows + [jnp.zeros((128,), F32)] * 4)


def _pad_rows(a, rows):
    return jnp.pad(a, ((0, rows - a.shape[0]), (0, 0)))


def kernel(x, positions, w_in, q_norm_g, w_uq, kv_norm_g, w_ukv, pool_w, pool_scale, w_out, ln_g, ln_b, loss_target, m_w_in, m_q_norm_g, m_w_uq, m_kv_norm_g, m_w_ukv, m_pool_w, m_pool_scale, m_w_out, m_ln_g, m_ln_b, v_w_in, v_q_norm_g, v_w_uq, v_kv_norm_g, v_w_ukv, v_pool_w, v_pool_scale, v_w_out, v_ln_g, v_ln_b):
    bb, ss, _ = x.shape
    tt = bb * ss
    atile = min(512, ss)
    nblk = ss // atile

    slab = _shard_slab(w_in, w_uq, w_ukv, w_out).astype(BF16)
    slabs = _all_gather(slab, "gather_weights").reshape(N_DEV, SLAB_ROWS, 1024)
    w1, wq, wkv, wo = _unpack_weights(slabs)

    gq, gkv = q_norm_g.reshape(1, Q_LORA), kv_norm_g.reshape(1, KV_LORA)
    ps = pool_scale.reshape(1, POOL_W)
    pos_col = positions.reshape(bb, ss, 1)
    pos_blk = positions.reshape(bb, nblk, 1, atile)
    qmax = jnp.max(positions.reshape(bb, nblk, atile), axis=-1).reshape(-1)
    kmin = jnp.min(positions.reshape(bb, nblk, atile), axis=-1).reshape(-1)

    x2 = x.reshape(tt, D_MODEL)
    xq, xkv, ga, u, gb, qn, kvn, qcat, kcat, v, rc, rsa, rsb = _fwd_in(
        x2, w1, gq, gkv, wq, wkv, pos_col.reshape(tt, 1), _rope_rows(), atile)
    as3 = lambda a: a.reshape(bb, ss, a.shape[-1])
    qcat, kcat, v = as3(qcat), as3(kcat), as3(v)
    o, lse = _attn_fwd(qcat, kcat, v, pos_col, pos_blk, qmax, kmin, atile, atile)
    dz, ycat, do, dga, dgb, dpc, dl, loss_p, g_lng, g_lnb, g_ps, g_pw = _post(
        x, loss_target, o, as3(ga), as3(u), as3(gb), wo, pool_w.astype(BF16), ps, ln_g, ln_b, atile)

    bt = min(1024, tt)
    g_wo, = _grad_w([(ycat.reshape(tt, D_MODEL), dz.reshape(tt, D_MODEL))], bt, "grad_w_out")
    rows5 = lambda a: a.reshape(bb, HEADS, nblk, 1, atile)
    rc, rsa, rsb = as3(rc), as3(rsa), as3(rsb)
    dqp, dkn, dv, dkr = _attn_bwd(
        qcat, kcat, v, do, rows5(lse), rows5(dl), rc, rsa, rsb, pos_col, pos_blk, qmax, kmin, atile, atile)
    grad_x, dh, g_gq, g_gkv = _bwd_mid(
        dqp, dkn, dv, dkr, as3(xq), as3(xkv), rc, rsa, rsb, wq, wkv, gq, gkv, dga, dgb, dpc, dz, w1, atile)
    g_w1, rs_out = _grad_w([(dh.reshape(tt, IN_WP), x2)], bt, "grad_w_in",
                           carried=_blocks(g_wo.reshape(N_DEV, R_OUT, D_MODEL)))
    g_wq, g_wkn, g_wv, rs_in = _grad_w(
        [(dqp.reshape(tt, HEADS * HEAD_PAD), qn), (dkn.reshape(tt, MLA_W), kvn), (dv.reshape(tt, MLA_W), kvn)], bt,
        "grad_w_uqkv", carried=_w_in_blocks(g_w1))
    g_wkv = jnp.concatenate([g_wkn, g_wv], axis=0)
    small = jnp.concatenate(
        [_pad_rows(g_lng.reshape(8, 128), 8), _pad_rows(g_lnb.reshape(8, 128), 8), _pad_rows(g_gq.reshape(4, 128), 8),
         _pad_rows(g_gkv.reshape(2, 128), 8), _pad_rows(g_ps.reshape(4, 128), 8), g_pw.reshape(POOL_G * POOL_C, 128),
         loss_p], axis=0)
    small, rs_mix = _tail_exchange(small, _mixed_band(g_wq, g_wkv), "tail_exchange")
    loss = small[40 + POOL_G * POOL_C, 0]
    grads = {
        "w_in": rs_in[:R_IN],
        "q_norm_g": small[16:20].reshape(1, Q_LORA),
        "w_uq": rs_mix[:R_UQ, :Q_LORA],
        "kv_norm_g": small[24:26].reshape(1, KV_LORA),
        "w_ukv": rs_mix[:, Q_LORA:Q_LORA + KV_LORA].T,
        "pool_w": small[40:40 + POOL_G * POOL_C],
        "pool_scale": small[32:36].reshape(1, POOL_W),
        "w_out": rs_out,
        "ln_g": small[0:8].reshape(1, D_MODEL),
        "ln_b": small[8:16].reshape(1, D_MODEL),
    }
    transposed = ("w_in", "w_uq")

    names = ["w_in", "q_norm_g", "w_uq", "kv_norm_g", "w_ukv", "pool_w", "pool_scale", "w_out", "ln_g", "ln_b"]
    weights = dict(w_in=w_in, q_norm_g=q_norm_g, w_uq=w_uq, kv_norm_g=kv_norm_g, w_ukv=w_ukv, pool_w=pool_w,
                   pool_scale=pool_scale, w_out=w_out, ln_g=ln_g, ln_b=ln_b)
    moms = dict(w_in=(m_w_in, v_w_in), q_norm_g=(m_q_norm_g, v_q_norm_g), w_uq=(m_w_uq, v_w_uq),
                kv_norm_g=(m_kv_norm_g, v_kv_norm_g), w_ukv=(m_w_ukv, v_w_ukv), pool_w=(m_pool_w, v_pool_w),
                pool_scale=(m_pool_scale, v_pool_scale), w_out=(m_w_out, v_w_out), ln_g=(m_ln_g, v_ln_g),
                ln_b=(m_ln_b, v_ln_b))
    as2 = lambda a, n: a.T if n in transposed else a.reshape(grads[n].shape)
    upd = _adamw([(as2(weights[n], n), grads[n], as2(moms[n][0], n), as2(moms[n][1], n)) for n in names])
    shaped = lambda a, n: a.T if n in transposed else a.reshape(weights[n].shape)
    return (loss, grad_x,
            *[shaped(grads[n], n) for n in names],
            *[shaped(upd[i][0], n) for i, n in enumerate(names)],
            *[shaped(upd[i][1], n) for i, n in enumerate(names)],
            *[shaped(upd[i][2], n) for i, n in enumerate(names)])
```

```python
import functools

import jax
import jax.numpy as jnp
from jax import lax
from jax.experimental import pallas as pl
from jax.experimental.pallas import tpu as pltpu

F32 = jnp.float32
BF16 = jnp.bfloat16
MESH = pl.DeviceIdType.MESH

N_DEV = 8
D_MODEL = 1024
HEADS = 4
NOPE = 128
ROPE = 64
V_DIM = 128
QK_DIM = NOPE + ROPE
HEAD_PAD = 256
Q_LORA = 512
KV_LORA = 256
MLA_W = HEADS * V_DIM
POOL_W = 512
POOL_G = 4
POOL_C = 128
POOL_HALO = 16
IN_W = 2368
IN_WP = 2432
C_XQ, C_XKV, C_KR, C_GA, C_U, C_GB = 0, 512, 768, 896, 1408, 1920
ROPE_THETA = 10000.0
RMS_EPS = 1e-6
LN_EPS = 1e-5
ALPHA = 2.0 ** 0.25
SCALE = QK_DIM ** -0.5
SCALE_LOG2 = SCALE * 1.4426950408889634
NEG = float(jnp.finfo(jnp.float32).min)

ADAM_LR = 0.001
ADAM_B1 = 0.9
ADAM_B2 = 0.999
ADAM_EPS = 1e-08
ADAM_WD = 0.01
ADAM_STEP = 10

R_OUT, R_MIX, R_UQ, R_IN = 128, 128, 96, 296
O_MIX, O_IN = R_OUT, R_OUT + R_MIX
SLAB_ROWS = 560
R_IN_PAD = 304

V7X_VMEM_BYTES = 64 * 1024 * 1024
ATTN_FWD_HEADS_PER_STEP = 4
ATTN_BWD_HEADS_PER_STEP = 2


def _params(vmem_mb, semantics=None):
    assert vmem_mb * 1024 * 1024 < V7X_VMEM_BYTES
    return pltpu.CompilerParams(vmem_limit_bytes=vmem_mb * 1024 * 1024, dimension_semantics=semantics)


def _dot(a, b):
    return jnp.dot(a, b, preferred_element_type=F32)


def _dot_nt(a, b):
    return lax.dot_general(a, b, (((1,), (1,)), ((), ())), preferred_element_type=F32)


def _dot_tn(a, b):
    return lax.dot_general(a, b, (((0,), (0,)), ((), ())), preferred_element_type=F32)


def _rope_fwd(t, c, sa, sb):
    return t * c + pltpu.roll(t, 96, 1) * sa + pltpu.roll(t, 32, 1) * sb


def _rope_bwd(d, c, sa, sb):
    return d * c + pltpu.roll(d * sa, 32, 1) + pltpu.roll(d * sb, 96, 1)


def _silu_parts(g):
    sig = jax.nn.sigmoid(g)
    return g * sig, sig * (1.0 + g * (1.0 - sig))


def _pool_cnt(row0, rows):
    t = row0 + lax.broadcasted_iota(jnp.int32, (rows, POOL_W), 0)
    w = 2 << (lax.broadcasted_iota(jnp.int32, (rows, POOL_W), 1) // POOL_C)
    return jnp.minimum(t + 1, w).astype(F32)


def _pick_groups(s2, s4, s8, s16, lo, hi):
    return jnp.concatenate([s2[lo:hi, 0:128], s4[lo:hi, 128:256], s8[lo:hi, 256:384], s16[lo:hi, 384:512]], axis=1)


AG_SEMS = [pltpu.SemaphoreType.DMA((7,)), pltpu.SemaphoreType.DMA((7,)), pltpu.SemaphoreType.DMA]


def _ag_phases(x_ref, out_ref, send_sems, recv_sems, local_sem, sum_ref=None):
    m_per = x_ref.shape[0]
    x, y, c = lax.axis_index("x"), lax.axis_index("y"), lax.axis_index("c")
    me, sibling = (x, y, c), (x, y, 1 - c)
    chips = [(1 - x, y), (x, 1 - y), (1 - x, 1 - y)]

    def rows(px, py, pc):
        return out_ref.at[pl.ds((4 * px + 2 * py + pc) * m_per, m_per), :]

    def copy(k, block, to, src=None):
        return pltpu.make_async_remote_copy(
            src_ref=rows(*block) if src is None else src, dst_ref=rows(*block),
            send_sem=send_sems.at[k], recv_sem=recv_sems.at[k], device_id=to, device_id_type=MESH)

    def mine():
        return pltpu.make_async_copy(x_ref, rows(*me), local_sem)

    def first():
        return [copy(0, me, sibling, src=x_ref)] + [copy(1 + j, me, (*chip, c), src=x_ref) for j, chip in enumerate(chips)]

    def passed():
        return [copy(4 + j, (*chip, c), sibling) for j, chip in enumerate(chips)]

    def start():
        for cp in [mine()] + first():
            cp.start()

    def forward():
        for j, (chip, cp) in enumerate(zip(chips, passed())):
            copy(1 + j, (*chip, c), me).wait_recv()
            cp.start()

    def finish():
        copy(0, sibling, me).wait_recv()
        for j, chip in enumerate(chips):
            copy(4 + j, (*chip, 1 - c), me).wait_recv()
        for cp in first() + passed():
            cp.wait_send()
        mine().wait()
        if sum_ref is not None:
            acc = out_ref[pl.ds(0, m_per), :]
            for d in range(1, N_DEV):
                acc = acc + out_ref[pl.ds(d * m_per, m_per), :]
            sum_ref[...] = acc

    return start, forward, finish


def _all_gather(shard, name):
    m_per, n = shard.shape

    def body(x_ref, out_ref, *sems):
        for phase in _ag_phases(x_ref, out_ref, *sems):
            phase()

    vmem = pl.BlockSpec(memory_space=pltpu.VMEM)
    return pl.pallas_call(
        body, name=name, out_shape=jax.ShapeDtypeStruct((N_DEV * m_per, n), shard.dtype),
        in_specs=[vmem], out_specs=vmem, scratch_shapes=AG_SEMS, compiler_params=_params(32))(shard)


def _reduce_scatter(sc, name):
    def body(g_hbm, out_ref, *scratch):
        for phase in _rs_phases(sc, g_hbm, out_ref, *scratch):
            phase()

    return pl.pallas_call(
        body, name=name, out_shape=jax.ShapeDtypeStruct((sc.rows, sc.width), F32),
        in_specs=[pl.BlockSpec(memory_space=pl.ANY)], out_specs=pl.BlockSpec(memory_space=pltpu.VMEM),
        scratch_shapes=_rs_scratch(sc), compiler_params=_params(32))(sc.array)


class _Scattered:
    def __init__(self, array, rows, pieces, locate):
        self.array, self.rows, self.pieces, self.locate = array, rows, pieces, locate
        self.width = array.shape[-1]


def _blocks(g):
    return _Scattered(g, g.shape[1], ((0, g.shape[1]),), lambda ref, d, row, rows: ref.at[d])


def _w_in_blocks(g_w1t):
    def locate(ref, d, row, rows):
        r = R_IN * d + row
        return ref.at[pl.ds(pl.multiple_of(r + jnp.where(r >= C_KR + ROPE, 128 - ROPE, 0), 8), rows), :]

    cut = C_KR + ROPE - 2 * R_IN
    return _Scattered(g_w1t, R_IN_PAD, ((0, cut), (cut, R_IN - cut)), locate)


def _rs_scratch(sc):
    rr, ww, n = sc.rows, sc.width, 4 * len(sc.pieces)
    return [pltpu.VMEM((4, rr, ww), F32), pltpu.VMEM((4, rr, ww), F32),
            pltpu.VMEM((3, rr, ww), BF16), pltpu.VMEM((3, rr, ww), BF16),
            pltpu.SemaphoreType.DMA((n,)), pltpu.SemaphoreType.DMA((n,)), pltpu.SemaphoreType.DMA((n,)),
            pltpu.SemaphoreType.DMA((3,)), pltpu.SemaphoreType.DMA((3,))]


def _rs_phases(sc, g_hbm, out_ref, own_ref, recv1_ref, sendb_ref, recv2_ref, ld_sems, s1_send, s1_recv, s2_send, s2_recv):
    rr, ww = out_ref.shape
    chunk = next(c for c in (128, 80, 64, 48, 32, 16) if rr % c == 0)
    x, y, c = lax.axis_index("x"), lax.axis_index("y"), lax.axis_index("c")
    chips = [(1 - x, y), (x, 1 - y), (1 - x, 1 - y)]
    npieces = len(sc.pieces)
    filled = sum(rows for _, rows in sc.pieces)

    def pieces(d, buf, k):
        for p, (row, rows) in enumerate(sc.pieces):
            dst = buf.at[k] if (row, rows) == (0, rr) else buf.at[k, pl.ds(row, rows), :]
            yield k * npieces + p, sc.locate(g_hbm, d, row, rows), dst

    def loads():
        return [pltpu.make_async_copy(src, dst, ld_sems.at[s])
                for k in range(4) for s, src, dst in pieces(2 * k + c, own_ref, k)]

    def stage1():
        return [pltpu.make_async_remote_copy(
            src_ref=src, dst_ref=dst, send_sem=s1_send.at[s], recv_sem=s1_recv.at[s],
            device_id=(x, y, 1 - c), device_id_type=MESH)
            for k in range(4) for s, src, dst in pieces(2 * k + (1 - c), recv1_ref, k)]

    def stage2():
        return [pltpu.make_async_remote_copy(
            src_ref=sendb_ref.at[r], dst_ref=recv2_ref.at[r], send_sem=s2_send.at[r],
            recv_sem=s2_recv.at[r], device_id=(cx, cy, c), device_id_type=MESH) for r, (cx, cy) in enumerate(chips)]

    def start():
        if filled < rr:
            own_ref[:, filled:rr, :] = jnp.zeros((4, rr - filled, ww), F32)
            recv1_ref[:, filled:rr, :] = jnp.zeros((4, rr - filled, ww), F32)
        for cp in loads() + stage1():
            cp.start()

    def forward():
        for cp in loads():
            cp.wait()
        for cp in stage1():
            cp.wait_recv()
        sends = stage2()
        for r, (cx, cy) in enumerate(chips):
            kk = 2 * cx + cy

            def pack(i, carry, r=r, kk=kk):
                rows = pl.ds(pl.multiple_of(i * chunk, chunk), chunk)
                sendb_ref[r, rows, :] = (own_ref[kk, rows, :] + recv1_ref[kk, rows, :]).astype(BF16)
                return carry

            lax.fori_loop(0, rr // chunk, pack, 0)
            sends[r].start()

    def finish():
        for cp in stage2():
            cp.wait_recv()
        mine = 2 * x + y

        def total(i, carry):
            rows = pl.ds(pl.multiple_of(i * chunk, chunk), chunk)
            acc = own_ref[mine, rows, :] + recv1_ref[mine, rows, :]
            for r in range(3):
                acc = acc + recv2_ref[r, rows, :].astype(F32)
            out_ref[rows, :] = acc
            return carry

        lax.fori_loop(0, rr // chunk, total, 0)
        for cp in stage1() + stage2():
            cp.wait_send()

    return start, forward, finish


def _fwd_in(x2, w1, gq, gkv, wq, wkv, pos, rope_rows, tm):
    tt = x2.shape[0]

    def body(x_ref, w1_ref, gq_ref, gkv_ref, wq_ref, wkv_ref, pos_ref, rr_ref,
             xq_ref, xkv_ref, ga_ref, u_ref, gb_ref, qn_ref, kvn_ref, qcat_ref, kcat_ref, v_ref,
             c_ref, sa_ref, sb_ref):
        ang = pos_ref[...].astype(F32) * rr_ref[0:1, :]
        cos, sin = jnp.cos(ang), jnp.sin(ang)
        c, sa, sb = cos * rr_ref[1:2, :], sin * rr_ref[2:3, :], sin * rr_ref[3:4, :]
        c_ref[...] = c
        sa_ref[...] = sa
        sb_ref[...] = sb
        h = _dot_nt(x_ref[...].astype(BF16), w1_ref[...])
        xq = h[:, C_XQ:C_XKV]
        xkv = h[:, C_XKV:C_KR]
        xq_ref[...] = xq
        xkv_ref[...] = xkv
        ga_ref[...] = h[:, C_GA:C_U]
        u_ref[...] = h[:, C_U:C_GB]
        gb_ref[...] = h[:, C_GB:IN_WP]
        qn =(xq * lax.rsqrt(jnp.mean(xq * xq, axis=-1, keepdims=True) + RMS_EPS) * gq_ref[...]).astype(BF16)
        kvn = (xkv * lax.rsqrt(jnp.mean(xkv * xkv, axis=-1, keepdims=True) + RMS_EPS) * gkv_ref[...]).astype(BF16)
        qn_ref[...] = qn
        kvn_ref[...] = kvn
        q = _dot_nt(qn, wq_ref[...])
        kv = _dot_nt(kvn, wkv_ref[...])
        kr = _rope_fwd(h[:, C_KR:C_GA], c, sa, sb).astype(BF16)
        for hd in range(HEADS):
            lo = hd * HEAD_PAD
            qcat_ref[:, lo:lo + NOPE] = q[:, lo:lo + NOPE].astype(BF16)
            qcat_ref[:, lo + NOPE:lo + HEAD_PAD] = _rope_fwd(q[:, lo + NOPE:lo + HEAD_PAD], c, sa, sb).astype(BF16)
            kcat_ref[:, lo:lo + NOPE] = kv[:, hd * NOPE:(hd + 1) * NOPE].astype(BF16)
            kcat_ref[:, lo + NOPE:lo + HEAD_PAD] = kr
        v_ref[...] = kv[:, MLA_W:].astype(BF16)

    def tile(w):
        return pl.BlockSpec((tm, w), lambda i: (i, 0))

    def whole(a):
        return pl.BlockSpec(a.shape, lambda i: (0, 0))

    outs = [(Q_LORA, F32), (KV_LORA, F32), (MLA_W, F32), (POOL_W, F32), (POOL_W, F32),
            (Q_LORA, BF16), (KV_LORA, BF16), (HEADS * HEAD_PAD, BF16), (HEADS * HEAD_PAD, BF16), (MLA_W, BF16),
            (128, F32), (128, F32), (128, F32)]
    return pl.pallas_call(
        body, name="fwd_in", grid=(tt // tm,),
        in_specs=[tile(D_MODEL), whole(w1), whole(gq), whole(gkv), whole(wq), whole(wkv), tile(1), whole(rope_rows)],
        out_specs=[tile(w) for w, _ in outs],
        out_shape=[jax.ShapeDtypeStruct((tt, w), dt) for w, dt in outs],
        compiler_params=_params(48, ("arbitrary",)))(x2, w1, gq, gkv, wq, wkv, pos, rope_rows)


def _attn_fwd(qcat, kcat, v, pos_col, pos_blk, qmax, kmin, tq, tk):
    bb, ss, _ = qcat.shape
    nq, nk = ss // tq, ss // tk
    lanes = 128
    hps = ATTN_FWD_HEADS_PER_STEP

    def body(qmax_ref, kmin_ref, q_ref, k_ref, v_ref, pc_ref, pb_ref, o_ref, lse_ref, m_s, acc_s):
        b, qi = pl.program_id(0), pl.program_id(2)
        m_s[...] = jnp.full(m_s.shape, NEG, F32)
        acc_s[...] = jnp.zeros(acc_s.shape, F32)
        pq = pc_ref[...]
        reach = qmax_ref[b * nq + qi]
        ones = jnp.ones((tk, lanes), BF16)

        def step(ki, carry):
            @pl.when(reach >= kmin_ref[b * nk + ki])
            def _():
                rows = pl.ds(pl.multiple_of(ki * tk, tk), tk)
                mask = pq >= pb_ref[ki]
                for hd in range(hps):
                    qk = slice(hd * HEAD_PAD, (hd + 1) * HEAD_PAD)
                    s = _dot_nt(q_ref[:, qk], k_ref[rows, qk]) * SCALE_LOG2
                    s = jnp.where(mask, s, NEG)
                    m_prev = m_s[hd]
                    m_new = jnp.maximum(m_prev, jnp.max(s, axis=-1, keepdims=True))
                    p = jnp.exp2(s - jnp.tile(m_new, (1, tk // lanes)))
                    a = jnp.exp2(m_prev - m_new)
                    vv = jnp.concatenate([v_ref[rows, hd * V_DIM:(hd + 1) * V_DIM], ones], axis=1)
                    acc_s[hd] = jnp.tile(a, (1, 2)) * acc_s[hd] + _dot(p.astype(BF16), vv)
                    m_s[hd] = m_new
            return carry

        lax.fori_loop(0, nk, step, 0)
        for hd in range(hps):
            acc = acc_s[hd]
            l = acc[:, V_DIM:]
            o_ref[:, hd * V_DIM:(hd + 1) * V_DIM] = acc[:, :V_DIM] / l
            lse_ref[hd] = (m_s[hd] + jnp.log2(l)).T[0:1, :]

    grid_spec = pltpu.PrefetchScalarGridSpec(
        num_scalar_prefetch=2, grid=(bb, HEADS // hps, nq),
        in_specs=[
            pl.BlockSpec((None, tq, hps * HEAD_PAD), lambda b, h, i, *_: (b, i, h)),
            pl.BlockSpec((None, ss, hps * HEAD_PAD), lambda b, h, i, *_: (b, 0, h)),
            pl.BlockSpec((None, ss, hps * V_DIM), lambda b, h, i, *_: (b, 0, h)),
            pl.BlockSpec((None, tq, 1), lambda b, h, i, *_: (b, i, 0)),
            pl.BlockSpec((None, nk, 1, tk), lambda b, h, i, *_: (b, 0, 0, 0)),
        ],
        out_specs=[
            pl.BlockSpec((None, tq, hps * V_DIM), lambda b, h, i, *_: (b, i, h)),
            pl.BlockSpec((None, hps, 1, tq), lambda b, h, i, *_: (b, h, 0, i)),
        ],
        scratch_shapes=[pltpu.VMEM((hps, tq, lanes), F32), pltpu.VMEM((hps, tq, 2 * V_DIM), F32)])
    return pl.pallas_call(
        body, name="attn_fwd", grid_spec=grid_spec,
        out_shape=[jax.ShapeDtypeStruct((bb, ss, MLA_W), F32), jax.ShapeDtypeStruct((bb, HEADS, 1, ss), F32)],
        compiler_params=_params(48, ("arbitrary", "arbitrary", "arbitrary")))(qmax, kmin, qcat, kcat, v, pos_col, pos_blk)


def _post(x, tgt, o, ga, u, gb, w_out, pool_wb, pool_scale, ln_g, ln_b, ts):
    bb, ss, _ = x.shape
    nt = ss // ts
    hb = ts // POOL_HALO

    def body(x_ref, tgt_ref, o_ref, ga_ref, u_ref, uh_ref, gb_ref, wo_ref, pw_ref, ps_ref, lg_ref, lb_ref,
             dz_ref, ycat_ref, do_ref, dga_ref, dgb_ref, dpc_ref, dl_ref, loss_ref, glg_ref, glb_ref, gps_ref, gpw_ref):
        step = pl.program_id(0)
        j = step % nt

        @pl.when(step == 0)
        def _():
            for r in (loss_ref, glg_ref, glb_ref, gps_ref, gpw_ref):
                r[...] = jnp.zeros(r.shape, F32)

        o, ga, u, gb = o_ref[...], ga_ref[...], u_ref[...], gb_ref[...]
        sa, dsa = _silu_parts(ga)
        sb, dsb = _silu_parts(gb)
        ext = jnp.concatenate([jnp.where(j > 0, uh_ref[...], 0.0), u], axis=0)
        s2 = ext + pltpu.roll(ext, 1, 0)
        s4 = s2 + pltpu.roll(s2, 2, 0)
        s8 = s4 + pltpu.roll(s4, 4, 0)
        s16 = s8 + pltpu.roll(s8, 8, 0)
        cnt = _pool_cnt(j * ts, ts)
        pooled = (_pick_groups(s2, s4, s8, s16, POOL_HALO, POOL_HALO + ts) / cnt - u).astype(BF16)
        mixed = jnp.concatenate(
            [_dot(pooled[:, g * POOL_C:(g + 1) * POOL_C], pw_ref[g]) for g in range(POOL_G)], axis=1)
        ps = ps_ref[...]
        scaled = mixed * ps
        ycat = jnp.concatenate([o * sa, scaled * sb], axis=1).astype(BF16)
        ycat_ref[...] = ycat
        z = ALPHA * x_ref[...] + _dot(ycat, wo_ref[...])
        mu = jnp.mean(z, axis=-1, keepdims=True)
        zc = z - mu
        rstd = lax.rsqrt(jnp.mean(zc * zc, axis=-1, keepdims=True) + LN_EPS)
        xhat = zc * rstd
        lg = lg_ref[...]
        diff = xhat * lg + lb_ref[...] - tgt_ref[...]
        loss_ref[...] += jnp.sum(diff * diff) * (0.5 / D_MODEL)
        dy = diff * (1.0 / D_MODEL)
        glb_ref[...] += jnp.sum(dy, axis=0, keepdims=True)
        glg_ref[...] += jnp.sum(dy * xhat, axis=0, keepdims=True)
        dxh = dy * lg
        dz = rstd * (dxh - jnp.mean(dxh, axis=-1, keepdims=True) - xhat * jnp.mean(dxh * xhat, axis=-1, keepdims=True))
        dz_ref[...] = dz
        dycat = _dot_nt(dz.astype(BF16), wo_ref[...])
        dya, dyb = dycat[:, :MLA_W], dycat[:, MLA_W:]
        do = dya * sa
        do_ref[...] = do.astype(BF16)
        doo = do * o
        for hd in range(HEADS):
            dl_ref[hd] = jnp.sum(doo[:, hd * V_DIM:(hd + 1) * V_DIM].T, axis=0, keepdims=True)
        dga_ref[...] = (dya * o * dsa).astype(BF16)
        dgb_ref[...] = (dyb * scaled * dsb).astype(BF16)
        dscaled = dyb * sb
        gps_ref[...] += jnp.sum(dscaled * mixed, axis=0, keepdims=True)
        dmixed = (dscaled * ps).astype(BF16)
        dpooled = []
        for g in range(POOL_G):
            cols = slice(g * POOL_C, (g + 1) * POOL_C)
            gpw_ref[g] += _dot_tn(pooled[:, cols], dmixed[:, cols])
            dpooled.append(_dot_nt(dmixed[:, cols], pw_ref[g]))
        dpc_ref[...] = jnp.concatenate(dpooled, axis=1) / cnt

    def tile(w):
        return pl.BlockSpec((None, ts, w), lambda i: (i // nt, i % nt, 0))

    def whole(a):
        nd = a.ndim
        return pl.BlockSpec(a.shape, lambda i: (0,) * nd)

    halo = pl.BlockSpec((None, POOL_HALO, POOL_W), lambda i: (i // nt, jnp.maximum((i % nt) * hb - 1, 0), 0))
    acc_shapes = [(8, 128), (1, D_MODEL), (1, D_MODEL), (1, POOL_W), (POOL_G, POOL_C, POOL_C)]
    tile_outs = [(D_MODEL, F32), (D_MODEL, BF16), (MLA_W, BF16), (MLA_W, BF16), (POOL_W, BF16), (POOL_W, F32)]
    return pl.pallas_call(
        body, name="post", grid=(bb * nt,),
        in_specs=[tile(D_MODEL), tile(D_MODEL), tile(MLA_W), tile(MLA_W), tile(POOL_W), halo, tile(POOL_W),
                  whole(w_out), whole(pool_wb), whole(pool_scale), whole(ln_g), whole(ln_b)],
        out_specs=[tile(w) for w, _ in tile_outs]
        + [pl.BlockSpec((None, HEADS, 1, ts), lambda i: (i // nt, 0, 0, i % nt))]
        + [pl.BlockSpec(s, lambda i, n=len(s): (0,) * n) for s in acc_shapes],
        out_shape=[jax.ShapeDtypeStruct((bb, ss, w), dt) for w, dt in tile_outs]
        + [jax.ShapeDtypeStruct((bb, HEADS, 1, ss), F32)]
        + [jax.ShapeDtypeStruct(s, F32) for s in acc_shapes],
        compiler_params=_params(48, ("arbitrary",)))(x, tgt, o, ga, u, u, gb, w_out, pool_wb, pool_scale, ln_g, ln_b)


def _attn_bwd(qcat, kcat, v, do, lse, dl, rc, rsa, rsb, pos_col, pos_blk, qmax, kmin, tq, tk):
    bb, ss, _ = qcat.shape
    nq, nk = ss // tq, ss // tk
    hps = ATTN_BWD_HEADS_PER_STEP

    def body(qmax_ref, kmin_ref, q_ref, k_ref, v_ref, do_ref, lse_ref, dl_ref, c_ref, sa_ref, sb_ref, pc_ref, pb_ref,
             dqp_ref, dkn_ref, dv_ref, dkr_ref, dq_s, dk_s, dv_s):
        b = pl.program_id(0)
        dq_s[...] = jnp.zeros(dq_s.shape, F32)

        def kv_step(ki, carry):
            krows = pl.ds(pl.multiple_of(ki * tk, tk), tk)
            pk = pc_ref[krows, :]
            first = kmin_ref[b * nk + ki]
            dk_s[...] = jnp.zeros(dk_s.shape, F32)
            dv_s[...] = jnp.zeros(dv_s.shape, F32)

            def q_step(qi, c2):
                @pl.when(qmax_ref[b * nq + qi] >= first)
                def _():
                    qrows = pl.ds(pl.multiple_of(qi * tq, tq), tq)
                    mask = pb_ref[qi] >= pk
                    for hd in range(hps):
                        qk = slice(hd * HEAD_PAD, (hd + 1) * HEAD_PAD)
                        vs = slice(hd * V_DIM, (hd + 1) * V_DIM)
                        q = q_ref[qrows, qk]
                        dd = do_ref[qrows, vs]
                        st = _dot_nt(k_ref[krows, qk], q) * SCALE_LOG2
                        st = jnp.where(mask, st, NEG)
                        pt = jnp.exp2(st - lse_ref[hd, qi])
                        dv_s[:, vs] += _dot(pt.astype(BF16), dd)
                        dpt = _dot_nt(v_ref[krows, vs], dd)
                        dst = (pt * (dpt - dl_ref[hd, qi]) * SCALE).astype(BF16)
                        dk_s[:, qk] += _dot(dst, q)
                        dq_s[qrows, qk] += _dot_tn(dst, k_ref[krows, qk])
                return c2

            lax.fori_loop(0, nq, q_step, 0)
            dkr = jnp.zeros((tk, HEAD_PAD - NOPE), F32)
            for hd in range(hps):
                lo = hd * HEAD_PAD
                dkn_ref[krows, hd * NOPE:(hd + 1) * NOPE] = dk_s[:, lo:lo + NOPE].astype(BF16)
                dkr = dkr + dk_s[:, lo + NOPE:lo + HEAD_PAD]
            dkr_ref[krows, :] = dkr
            dv_ref[krows, :] = dv_s[...].astype(BF16)
            return carry

        lax.fori_loop(0, nk, kv_step, 0)
        c, sa, sb = c_ref[...], sa_ref[...], sb_ref[...]
        for hd in range(hps):
            lo = hd * HEAD_PAD
            dqp_ref[:, lo:lo + NOPE] = dq_s[:, lo:lo + NOPE].astype(BF16)
            dqp_ref[:, lo + NOPE:lo + HEAD_PAD] = _rope_bwd(dq_s[:, lo + NOPE:lo + HEAD_PAD], c, sa, sb).astype(BF16)

    def per_head(w):
        return pl.BlockSpec((None, ss, hps * w), lambda b, h, *_: (b, 0, h))

    def rows_of_head():
        return pl.BlockSpec((None, hps, nq, 1, tq), lambda b, h, *_: (b, h, 0, 0, 0))

    def per_batch(w):
        return pl.BlockSpec((None, ss, w), lambda b, h, *_: (b, 0, 0))

    grid_spec = pltpu.PrefetchScalarGridSpec(
        num_scalar_prefetch=2, grid=(bb, HEADS // hps),
        in_specs=[per_head(HEAD_PAD), per_head(HEAD_PAD), per_head(V_DIM), per_head(V_DIM),
                  rows_of_head(), rows_of_head(), per_batch(128), per_batch(128), per_batch(128), per_batch(1),
                  pl.BlockSpec((None, nq, 1, tq), lambda b, h, *_: (b, 0, 0, 0))],
        out_specs=[per_head(HEAD_PAD), per_head(NOPE), per_head(V_DIM),
                   pl.BlockSpec((None, None, ss, HEAD_PAD - NOPE), lambda b, h, *_: (b, h, 0, 0))],
        scratch_shapes=[pltpu.VMEM((ss, hps * HEAD_PAD), F32), pltpu.VMEM((tk, hps * HEAD_PAD), F32),
                        pltpu.VMEM((tk, hps * V_DIM), F32)])
    return pl.pallas_call(
        body, name="attn_bwd", grid_spec=grid_spec,
        out_shape=[jax.ShapeDtypeStruct((bb, ss, HEADS * HEAD_PAD), BF16),
                   jax.ShapeDtypeStruct((bb, ss, MLA_W), BF16),
                   jax.ShapeDtypeStruct((bb, ss, MLA_W), BF16),
                   jax.ShapeDtypeStruct((bb, HEADS // hps, ss, HEAD_PAD - NOPE), F32)],
        compiler_params=_params(56, ("arbitrary", "arbitrary")))(
            qmax, kmin, qcat, kcat, v, do, lse, dl, rc, rsa, rsb, pos_col, pos_blk)


def _bwd_mid(dqp, dkn, dv, dkr, xq, xkv, rc, rsa, rsb, wq, wkv, gq, gkv, dga, dgb, dpc, dz, w1, ts):
    bb, ss, _ = dz.shape
    nt = ss // ts
    hb = ts // POOL_HALO
    groups = dkr.shape[1]

    def body(dqp_ref, dkn_ref, dv_ref, dkr_ref, xq_ref, xkv_ref, c_ref, sa_ref, sb_ref, wq_ref, wkv_ref, gq_ref,
             gkv_ref, dga_ref, dgb_ref, dpc_ref, dph_ref, dz_ref, w1_ref,
             gx_ref, dh_ref, ggq_ref, ggkv_ref):
        step = pl.program_id(0)
        j = step % nt

        @pl.when(step == 0)
        def _():
            ggq_ref[...] = jnp.zeros(ggq_ref.shape, F32)
            ggkv_ref[...] = jnp.zeros(ggkv_ref.shape, F32)

        dkr = dkr_ref[0]
        for g in range(1, groups):
            dkr = dkr + dkr_ref[g]
        dh_ref[:, C_KR:C_GA] = _rope_bwd(dkr, c_ref[...], sa_ref[...], sb_ref[...]).astype(BF16)

        def rms_bwd(x, g, dn):
            inv = lax.rsqrt(jnp.mean(x * x, axis=-1, keepdims=True) + RMS_EPS)
            xh = x * inv
            dxh = dn * g
            return inv * (dxh - xh * jnp.mean(dxh * xh, axis=-1, keepdims=True)), jnp.sum(dn * xh, axis=0, keepdims=True)

        dxq, ggq = rms_bwd(xq_ref[...], gq_ref[...], _dot(dqp_ref[...], wq_ref[...]))
        dkvn = _dot(dkn_ref[...], wkv_ref[:MLA_W, :]) + _dot(dv_ref[...], wkv_ref[MLA_W:, :])
        dxkv, ggkv = rms_bwd(xkv_ref[...], gkv_ref[...], dkvn)
        ggq_ref[...] += ggq
        ggkv_ref[...] += ggkv
        dh_ref[:, C_XQ:C_XKV] = dxq.astype(BF16)
        dh_ref[:, C_XKV:C_KR] = dxkv.astype(BF16)
        dh_ref[:, C_GA:C_U] = dga_ref[...]
        dh_ref[:, C_GB:IN_WP] = dgb_ref[...]
        dpc = dpc_ref[...]
        n = ts + POOL_HALO
        ext = jnp.concatenate([dpc, jnp.where(j < nt - 1, dph_ref[...], 0.0)], axis=0)
        r2 = ext + pltpu.roll(ext, n - 1, 0)
        r4 = r2 + pltpu.roll(r2, n - 2, 0)
        r8 = r4 + pltpu.roll(r4, n - 4, 0)
        r16 = r8 + pltpu.roll(r8, n - 8, 0)
        du = _pick_groups(r2, r4, r8, r16, 0, ts) - dpc * _pool_cnt(j * ts, ts)
        dh_ref[:, C_U:C_GB] = du.astype(BF16)
        gx_ref[...] = ALPHA * dz_ref[...] + _dot(dh_ref[...], w1_ref[...])

    def tile(w):
        return pl.BlockSpec((None, ts, w), lambda i: (i // nt, i % nt, 0))

    def whole(a):
        return pl.BlockSpec(a.shape, lambda i: (0, 0))

    halo = pl.BlockSpec((None, POOL_HALO, POOL_W),
                        lambda i: (i // nt, jnp.minimum((i % nt + 1) * hb, ss // POOL_HALO - 1), 0))
    return pl.pallas_call(
        body, name="bwd_mid", grid=(bb * nt,),
        in_specs=[tile(HEADS * HEAD_PAD), tile(MLA_W), tile(MLA_W),
                  pl.BlockSpec((None, groups, ts, HEAD_PAD - NOPE), lambda i: (i // nt, 0, i % nt, 0)),
                  tile(Q_LORA), tile(KV_LORA),
                  tile(128), tile(128), tile(128), whole(wq), whole(wkv), whole(gq), whole(gkv),
                  tile(MLA_W), tile(POOL_W), tile(POOL_W), halo, tile(D_MODEL), whole(w1)],
        out_specs=[tile(D_MODEL), tile(IN_WP),
                   pl.BlockSpec((1, Q_LORA), lambda i: (0, 0)), pl.BlockSpec((1, KV_LORA), lambda i: (0, 0))],
        out_shape=[jax.ShapeDtypeStruct((bb, ss, D_MODEL), F32), jax.ShapeDtypeStruct((bb, ss, IN_WP), BF16),
                   jax.ShapeDtypeStruct((1, Q_LORA), F32), jax.ShapeDtypeStruct((1, KV_LORA), F32)],
        compiler_params=_params(48, ("arbitrary",)))(
            dqp, dkn, dv, dkr, xq, xkv, rc, rsa, rsb, wq, wkv, gq, gkv, dga, dgb, dpc, dpc, dz, w1)


def _grad_w(pairs, bt, name, b_cols=None, carried=None, reduced=None):
    tt = pairs[0][0].shape[0]
    steps = tt // bt
    npairs = len(pairs)
    n_rs = len(_rs_scratch(carried)) if carried is not None else 0

    def body(*refs):
        ab, rest = refs[:2 * npairs], list(refs[2 * npairs:])
        g_hbm = rest.pop(0) if carried is not None else None
        part = rest.pop(0) if reduced is not None else None
        outs = [rest.pop(0) for _ in range(npairs)]
        phases = []
        if carried is not None:
            rs_out = rest.pop(0)
        if reduced is not None:
            sum_ref = rest.pop(0)
        if carried is not None:
            phases.append(_rs_phases(carried, g_hbm, rs_out, *rest[:n_rs]))
        if reduced is not None:
            gathered, *sems = rest[n_rs:]
            phases.append(_ag_phases(part, gathered, *sems, sum_ref=sum_ref))
        step = pl.program_id(0)

        @pl.when(step == 0)
        def _():
            for o in outs:
                o[...] = jnp.zeros(o.shape, F32)
            for start, _, _ in phases:
                start()

        for i in range(npairs):
            outs[i][...] += _dot_tn(ab[2 * i][...].astype(BF16), ab[2 * i + 1][...].astype(BF16))

        @pl.when(step == min(2, steps - 1))
        def _():
            for _, forward, _ in phases:
                forward()

        @pl.when(step == steps - 1)
        def _():
            for _, _, finish in phases:
                finish()

    in_specs, out_specs, out_shape = [], [], []
    for a, b in pairs:
        m, (n, col) = a.shape[1], b_cols or (b.shape[1], 0)
        in_specs += [pl.BlockSpec((bt, m), lambda i: (i, 0)), pl.BlockSpec((bt, n), lambda i, col=col: (i, col))]
        out_specs.append(pl.BlockSpec((m, n), lambda i: (0, 0)))
        out_shape.append(jax.ShapeDtypeStruct((m, n), F32))
    args = [t for p in pairs for t in p]
    scratch = []
    if carried is not None:
        in_specs.append(pl.BlockSpec(memory_space=pl.ANY))
        out_specs.append(pl.BlockSpec((carried.rows, carried.width), lambda i: (0, 0)))
        out_shape.append(jax.ShapeDtypeStruct((carried.rows, carried.width), F32))
        args.append(carried.array)
        scratch += _rs_scratch(carried)
    if reduced is not None:
        in_specs.append(pl.BlockSpec(memory_space=pl.ANY))
        out_specs.append(pl.BlockSpec(reduced.shape, lambda i: (0, 0)))
        out_shape.append(jax.ShapeDtypeStruct(reduced.shape, F32))
        args.append(reduced)
        scratch += [pltpu.VMEM((N_DEV * reduced.shape[0], reduced.shape[1]), F32)] + AG_SEMS
    return pl.pallas_call(
        body, name=name, grid=(steps,), in_specs=in_specs, out_specs=out_specs, out_shape=out_shape,
        scratch_shapes=scratch, compiler_params=_params(56, ("arbitrary",)))(*args)


def _adamw(triples):
    n = len(triples)

    def body(*refs):
        ins, outs = refs[:4 * n], refs[4 * n:]
        for i in range(n):
            w, g, m, v = (r[...] for r in ins[4 * i:4 * i + 4])
            m = ADAM_B1 * m + (1.0 - ADAM_B1) * g
            v = ADAM_B2 * v + (1.0 - ADAM_B2) * jnp.square(g)
            m_hat = m / (1.0 - ADAM_B1 ** ADAM_STEP)
            v_hat = v / (1.0 - ADAM_B2 ** ADAM_STEP)
            outs[3 * i][...] = -ADAM_LR * (m_hat / (jnp.sqrt(v_hat) + ADAM_EPS) + ADAM_WD * w)
            outs[3 * i + 1][...] = m
            outs[3 * i + 2][...] = v

    flat = [a for t in triples for a in t]
    vmem = pl.BlockSpec(memory_space=pltpu.VMEM)
    res = pl.pallas_call(
        body, name="adamw", in_specs=[vmem] * len(flat), out_specs=[vmem] * (3 * n),
        out_shape=[jax.ShapeDtypeStruct(t[0].shape, F32) for t in triples for _ in range(3)],
        compiler_params=_params(48))(*flat)
    return [tuple(res[3 * i:3 * i + 3]) for i in range(n)]


def _shard_slab(w_in, w_uq, w_ukv, w_out):
    mixed = jnp.concatenate(
        [_pad_rows(w_uq.T, R_MIX), w_ukv.T, jnp.zeros((R_MIX, 1024 - Q_LORA - KV_LORA), F32)], axis=1)
    return jnp.concatenate([w_out, mixed, w_in.T, jnp.zeros((SLAB_ROWS - O_IN - R_IN, 1024), F32)], axis=0)


def _unpack_weights(slabs):
    w_out = slabs[:, :R_OUT].reshape(D_MODEL, D_MODEL)
    uq = slabs[:, O_MIX:O_MIX + R_UQ, :Q_LORA].reshape(HEADS, QK_DIM, Q_LORA)
    wqt = jnp.pad(uq, ((0, 0), (0, HEAD_PAD - QK_DIM), (0, 0))).reshape(HEADS * HEAD_PAD, Q_LORA)
    ukv = slabs[:, O_MIX:O_MIX + R_MIX, Q_LORA:Q_LORA + KV_LORA].reshape(HEADS, 2, NOPE, KV_LORA)
    wkvt = ukv.transpose(1, 0, 2, 3).reshape(2 * MLA_W, KV_LORA)
    raw = slabs[:, O_IN:O_IN + R_IN].reshape(IN_W, D_MODEL)
    w1t = jnp.concatenate([raw[:768 + ROPE], jnp.zeros((128 - ROPE, D_MODEL), BF16), raw[768 + ROPE:]], axis=0)
    return w1t, wqt, wkvt, w_out


def _mixed_band(g_wqt, g_wkvt):
    uq = g_wqt.reshape(HEADS, HEAD_PAD, Q_LORA)[:, :QK_DIM].reshape(N_DEV, R_UQ, Q_LORA)
    uq = jnp.pad(uq, ((0, 0), (0, R_MIX - R_UQ), (0, 0)))
    ukv = g_wkvt.reshape(2, HEADS, NOPE, KV_LORA).transpose(1, 0, 2, 3).reshape(N_DEV, R_MIX, KV_LORA)
    return jnp.concatenate([uq, ukv, jnp.zeros((N_DEV, R_MIX, 1024 - Q_LORA - KV_LORA), F32)], axis=2)


def _rope_rows():
    half = ROPE // 2
    inv_freq = ROPE_THETA ** (-jnp.arange(half, dtype=F32) / half)
    zero, one = jnp.zeros((half,), F32), jnp.ones((half,), F32)
    rows = [jnp.concatenate(r) for r in (
        (inv_freq, inv_freq, zero, zero), (one, one, zero, zero), (-one, zero, zero, zero), (zero, one, zero, zero))]
    return jnp.stack(rows + [jnp.zeros((128,), F32)] * 4)


def _pad_rows(a, rows):
    return jnp.pad(a, ((0, rows - a.shape[0]), (0, 0)))


def kernel(x, positions, w_in, q_norm_g, w_uq, kv_norm_g, w_ukv, pool_w, pool_scale, w_out, ln_g, ln_b, loss_target, m_w_in, m_q_norm_g, m_w_uq, m_kv_norm_g, m_w_ukv, m_pool_w, m_pool_scale, m_w_out, m_ln_g, m_ln_b, v_w_in, v_q_norm_g, v_w_uq, v_kv_norm_g, v_w_ukv, v_pool_w, v_pool_scale, v_w_out, v_ln_g, v_ln_b):
    bb, ss, _ = x.shape
    tt = bb * ss
    atile = min(512, ss)
    nblk = ss // atile

    slab = _shard_slab(w_in, w_uq, w_ukv, w_out).astype(BF16)
    slabs = _all_gather(slab, "gather_weights").reshape(N_DEV, SLAB_ROWS, 1024)
    w1, wq, wkv, wo = _unpack_weights(slabs)

    gq, gkv = q_norm_g.reshape(1, Q_LORA), kv_norm_g.reshape(1, KV_LORA)
    ps = pool_scale.reshape(1, POOL_W)
    pos_col = positions.reshape(bb, ss, 1)
    pos_blk = positions.reshape(bb, nblk, 1, atile)
    qmax = jnp.max(positions.reshape(bb, nblk, atile), axis=-1).reshape(-1)
    kmin = jnp.min(positions.reshape(bb, nblk, atile), axis=-1).reshape(-1)

    x2 = x.reshape(tt, D_MODEL)
    xq, xkv, ga, u, gb, qn, kvn, qcat, kcat, v, rc, rsa, rsb = _fwd_in(
        x2, w1, gq, gkv, wq, wkv, pos_col.reshape(tt, 1), _rope_rows(), atile)
    as3 = lambda a: a.reshape(bb, ss, a.shape[-1])
    qcat, kcat, v = as3(qcat), as3(kcat), as3(v)
    o, lse = _attn_fwd(qcat, kcat, v, pos_col, pos_blk, qmax, kmin, atile, atile)
    dz, ycat, do, dga, dgb, dpc, dl, loss_p, g_lng, g_lnb, g_ps, g_pw = _post(
        x, loss_target, o, as3(ga), as3(u), as3(gb), wo, pool_w.astype(BF16), ps, ln_g, ln_b, atile)

    bt = min(1024, tt)
    g_wo, = _grad_w([(ycat.reshape(tt, D_MODEL), dz.reshape(tt, D_MODEL))], bt, "grad_w_out")
    rows5 = lambda a: a.reshape(bb, HEADS, nblk, 1, atile)
    rc, rsa, rsb = as3(rc), as3(rsa), as3(rsb)
    dqp, dkn, dv, dkr = _attn_bwd(
        qcat, kcat, v, do, rows5(lse), rows5(dl), rc, rsa, rsb, pos_col, pos_blk, qmax, kmin, atile, atile)
    g_wq, g_wkn, g_wv, rs_out = _grad_w(
        [(dqp.reshape(tt, HEADS * HEAD_PAD), qn), (dkn.reshape(tt, MLA_W), kvn), (dv.reshape(tt, MLA_W), kvn)], bt,
        "grad_w_uqkv", carried=_blocks(g_wo.reshape(N_DEV, R_OUT, D_MODEL)))
    g_wkv = jnp.concatenate([g_wkn, g_wv], axis=0)
    grad_x, dh, g_gq, g_gkv = _bwd_mid(
        dqp, dkn, dv, dkr, as3(xq), as3(xkv), rc, rsa, rsb, wq, wkv, gq, gkv, dga, dgb, dpc, dz, w1, atile)
    small = jnp.concatenate(
        [_pad_rows(g_lng.reshape(8, 128), 8), _pad_rows(g_lnb.reshape(8, 128), 8), _pad_rows(g_gq.reshape(4, 128), 8),
         _pad_rows(g_gkv.reshape(2, 128), 8), _pad_rows(g_ps.reshape(4, 128), 8), g_pw.reshape(POOL_G * POOL_C, 128),
         loss_p], axis=0)
    dh2, half = dh.reshape(tt, IN_WP), D_MODEL // 2
    g_w1a, rs_mix, small = _grad_w([(dh2, x2)], bt, "grad_w_in_a", b_cols=(half, 0),
                                   carried=_blocks(_mixed_band(g_wq, g_wkv)), reduced=small)
    g_w1b, rs_in_a = _grad_w([(dh2, x2)], bt, "grad_w_in_b", b_cols=(half, 1), carried=_w_in_blocks(g_w1a))
    rs_in = jnp.concatenate([rs_in_a, _reduce_scatter(_w_in_blocks(g_w1b), "reduce_scatter_w_in")], axis=1)
    loss = small[40 + POOL_G * POOL_C, 0]
    grads = {
        "w_in": rs_in[:R_IN],
        "q_norm_g": small[16:20].reshape(1, Q_LORA),
        "w_uq": rs_mix[:R_UQ, :Q_LORA],
        "kv_norm_g": small[24:26].reshape(1, KV_LORA),
        "w_ukv": rs_mix[:, Q_LORA:Q_LORA + KV_LORA].T,
        "pool_w": small[40:40 + POOL_G * POOL_C],
        "pool_scale": small[32:36].reshape(1, POOL_W),
        "w_out": rs_out,
        "ln_g": small[0:8].reshape(1, D_MODEL),
        "ln_b": small[8:16].reshape(1, D_MODEL),
    }
    transposed = ("w_in", "w_uq")

    names = ["w_in", "q_norm_g", "w_uq", "kv_norm_g", "w_ukv", "pool_w", "pool_scale", "w_out", "ln_g", "ln_b"]
    weights = dict(w_in=w_in, q_norm_g=q_norm_g, w_uq=w_uq, kv_norm_g=kv_norm_g, w_ukv=w_ukv, pool_w=pool_w,
                   pool_scale=pool_scale, w_out=w_out, ln_g=ln_g, ln_b=ln_b)
    moms = dict(w_in=(m_w_in, v_w_in), q_norm_g=(m_q_norm_g, v_q_norm_g), w_uq=(m_w_uq, v_w_uq),
                kv_norm_g=(m_kv_norm_g, v_kv_norm_g), w_ukv=(m_w_ukv, v_w_ukv), pool_w=(m_pool_w, v_pool_w),
                pool_scale=(m_pool_scale, v_pool_scale), w_out=(m_w_out, v_w_out), ln_g=(m_ln_g, v_ln_g),
                ln_b=(m_ln_b, v_ln_b))
    as2 = lambda a, n: a.T if n in transposed else a.reshape(grads[n].shape)
    upd = _adamw([(as2(weights[n], n), grads[n], as2(moms[n][0], n), as2(moms[n][1], n)) for n in names])
    shaped = lambda a, n: a.T if n in transposed else a.reshape(weights[n].shape)
    return (loss, grad_x,
            *[shaped(grads[n], n) for n in names],
            *[shaped(upd[i][0], n) for i, n in enumerate(names)],
            *[shaped(upd[i][1], n) for i, n in enumerate(names)],
            *[shaped(upd[i][2], n) for i, n in enumerate(names)])
```

```python
import functools

import jax
import jax.numpy as jnp
from jax import lax
from jax.experimental import pallas as pl
from jax.experimental.pallas import tpu as pltpu

F32 = jnp.float32
BF16 = jnp.bfloat16
MESH = pl.DeviceIdType.MESH

N_DEV = 8
D_MODEL = 1024
HEADS = 4
NOPE = 128
ROPE = 64
V_DIM = 128
QK_DIM = NOPE + ROPE
HEAD_PAD = 256
Q_LORA = 512
KV_LORA = 256
MLA_W = HEADS * V_DIM
POOL_W = 512
POOL_G = 4
POOL_C = 128
POOL_HALO = 16
IN_W = 2368
IN_WP = 2432
C_XQ, C_XKV, C_KR, C_GA, C_U, C_GB = 0, 512, 768, 896, 1408, 1920
ROPE_THETA = 10000.0
RMS_EPS = 1e-6
LN_EPS = 1e-5
ALPHA = 2.0 ** 0.25
SCALE = QK_DIM ** -0.5
SCALE_LOG2 = SCALE * 1.4426950408889634
NEG = float(jnp.finfo(jnp.float32).min)

ADAM_LR = 0.001
ADAM_B1 = 0.9
ADAM_B2 = 0.999
ADAM_EPS = 1e-08
ADAM_WD = 0.01
ADAM_STEP = 10

R_OUT, R_MIX, R_UQ, R_IN = 128, 128, 96, 296
O_MIX, O_IN = R_OUT, R_OUT + R_MIX
SLAB_ROWS = 560
R_IN_PAD = 304

V7X_VMEM_BYTES = 64 * 1024 * 1024
ATTN_FWD_HEADS_PER_STEP = 4
ATTN_BWD_HEADS_PER_STEP = 2


def _params(vmem_mb, semantics=None):
    assert vmem_mb * 1024 * 1024 < V7X_VMEM_BYTES
    return pltpu.CompilerParams(vmem_limit_bytes=vmem_mb * 1024 * 1024, dimension_semantics=semantics)


def _dot(a, b):
    return jnp.dot(a, b, preferred_element_type=F32)


def _dot_nt(a, b):
    return lax.dot_general(a, b, (((1,), (1,)), ((), ())), preferred_element_type=F32)


def _dot_tn(a, b):
    return lax.dot_general(a, b, (((0,), (0,)), ((), ())), preferred_element_type=F32)


def _rope_fwd(t, c, sa, sb):
    return t * c + pltpu.roll(t, 96, 1) * sa + pltpu.roll(t, 32, 1) * sb


def _rope_bwd(d, c, sa, sb):
    return d * c + pltpu.roll(d * sa, 32, 1) + pltpu.roll(d * sb, 96, 1)


def _silu_parts(g):
    sig = jax.nn.sigmoid(g)
    return g * sig, sig * (1.0 + g * (1.0 - sig))


def _pool_cnt(row0, rows):
    t = row0 + lax.broadcasted_iota(jnp.int32, (rows, POOL_W), 0)
    w = 2 << (lax.broadcasted_iota(jnp.int32, (rows, POOL_W), 1) // POOL_C)
    return jnp.minimum(t + 1, w).astype(F32)


def _pick_groups(s2, s4, s8, s16, lo, hi):
    return jnp.concatenate([s2[lo:hi, 0:128], s4[lo:hi, 128:256], s8[lo:hi, 256:384], s16[lo:hi, 384:512]], axis=1)


AG_SEMS = [pltpu.SemaphoreType.DMA((7,)), pltpu.SemaphoreType.DMA((7,)), pltpu.SemaphoreType.DMA]


def _ag_phases(x_ref, out_ref, send_sems, recv_sems, local_sem, sum_ref=None):
    m_per = x_ref.shape[0]
    x, y, c = lax.axis_index("x"), lax.axis_index("y"), lax.axis_index("c")
    me, sibling = (x, y, c), (x, y, 1 - c)
    chips = [(1 - x, y), (x, 1 - y), (1 - x, 1 - y)]

    def rows(px, py, pc):
        return out_ref.at[pl.ds((4 * px + 2 * py + pc) * m_per, m_per), :]

    def copy(k, block, to, src=None):
        return pltpu.make_async_remote_copy(
            src_ref=rows(*block) if src is None else src, dst_ref=rows(*block),
            send_sem=send_sems.at[k], recv_sem=recv_sems.at[k], device_id=to, device_id_type=MESH)

    def mine():
        return pltpu.make_async_copy(x_ref, rows(*me), local_sem)

    def first():
        return [copy(0, me, sibling, src=x_ref)] + [copy(1 + j, me, (*chip, c), src=x_ref) for j, chip in enumerate(chips)]

    def passed():
        return [copy(4 + j, (*chip, c), sibling) for j, chip in enumerate(chips)]

    def start():
        for cp in [mine()] + first():
            cp.start()

    def forward():
        for j, (chip, cp) in enumerate(zip(chips, passed())):
            copy(1 + j, (*chip, c), me).wait_recv()
            cp.start()

    def finish():
        copy(0, sibling, me).wait_recv()
        for j, chip in enumerate(chips):
            copy(4 + j, (*chip, 1 - c), me).wait_recv()
        for cp in first() + passed():
            cp.wait_send()
        mine().wait()
        if sum_ref is not None:
            acc = out_ref[pl.ds(0, m_per), :]
            for d in range(1, N_DEV):
                acc = acc + out_ref[pl.ds(d * m_per, m_per), :]
            sum_ref[...] = acc

    return start, forward, finish


def _all_gather(shard, name):
    m_per, n = shard.shape

    def body(x_ref, out_ref, *sems):
        for phase in _ag_phases(x_ref, out_ref, *sems):
            phase()

    vmem = pl.BlockSpec(memory_space=pltpu.VMEM)
    return pl.pallas_call(
        body, name=name, out_shape=jax.ShapeDtypeStruct((N_DEV * m_per, n), shard.dtype),
        in_specs=[vmem], out_specs=vmem, scratch_shapes=AG_SEMS, compiler_params=_params(32))(shard)


def _reduce_scatter(sc, name):
    def body(g_hbm, out_ref, *scratch):
        for phase in _rs_phases(sc, g_hbm, out_ref, *scratch):
            phase()

    return pl.pallas_call(
        body, name=name, out_shape=jax.ShapeDtypeStruct((sc.rows, sc.width), F32),
        in_specs=[pl.BlockSpec(memory_space=pl.ANY)], out_specs=pl.BlockSpec(memory_space=pltpu.VMEM),
        scratch_shapes=_rs_scratch(sc), compiler_params=_params(32))(sc.array)


class _Scattered:
    def __init__(self, array, rows, pieces, locate):
        self.array, self.rows, self.pieces, self.locate = array, rows, pieces, locate
        self.width = array.shape[-1]


def _blocks(g):
    return _Scattered(g, g.shape[1], ((0, g.shape[1]),), lambda ref, d, row, rows: ref.at[d])


def _w_in_blocks(g_w1t):
    def locate(ref, d, row, rows):
        r = R_IN * d + row
        return ref.at[pl.ds(pl.multiple_of(r + jnp.where(r >= C_KR + ROPE, 128 - ROPE, 0), 8), rows), :]

    cut = C_KR + ROPE - 2 * R_IN
    return _Scattered(g_w1t, R_IN_PAD, ((0, cut), (cut, R_IN - cut)), locate)


def _rs_scratch(sc):
    rr, ww, n = sc.rows, sc.width, 4 * len(sc.pieces)
    return [pltpu.VMEM((4, rr, ww), F32), pltpu.VMEM((4, rr, ww), F32),
            pltpu.VMEM((3, rr, ww), BF16), pltpu.VMEM((3, rr, ww), BF16),
            pltpu.SemaphoreType.DMA((n,)), pltpu.SemaphoreType.DMA((n,)), pltpu.SemaphoreType.DMA((n,)),
            pltpu.SemaphoreType.DMA((3,)), pltpu.SemaphoreType.DMA((3,))]


def _rs_phases(sc, g_hbm, out_ref, own_ref, recv1_ref, sendb_ref, recv2_ref, ld_sems, s1_send, s1_recv, s2_send, s2_recv):
    rr, ww = out_ref.shape
    chunk = next(c for c in (128, 80, 64, 48, 32, 16) if rr % c == 0)
    x, y, c = lax.axis_index("x"), lax.axis_index("y"), lax.axis_index("c")
    chips = [(1 - x, y), (x, 1 - y), (1 - x, 1 - y)]
    npieces = len(sc.pieces)
    filled = sum(rows for _, rows in sc.pieces)

    def pieces(d, buf, k):
        for p, (row, rows) in enumerate(sc.pieces):
            dst = buf.at[k] if (row, rows) == (0, rr) else buf.at[k, pl.ds(row, rows), :]
            yield k * npieces + p, sc.locate(g_hbm, d, row, rows), dst

    def loads():
        return [pltpu.make_async_copy(src, dst, ld_sems.at[s])
                for k in range(4) for s, src, dst in pieces(2 * k + c, own_ref, k)]

    def stage1():
        return [pltpu.make_async_remote_copy(
            src_ref=src, dst_ref=dst, send_sem=s1_send.at[s], recv_sem=s1_recv.at[s],
            device_id=(x, y, 1 - c), device_id_type=MESH)
            for k in range(4) for s, src, dst in pieces(2 * k + (1 - c), recv1_ref, k)]

    def stage2():
        return [pltpu.make_async_remote_copy(
            src_ref=sendb_ref.at[r], dst_ref=recv2_ref.at[r], send_sem=s2_send.at[r],
            recv_sem=s2_recv.at[r], device_id=(cx, cy, c), device_id_type=MESH) for r, (cx, cy) in enumerate(chips)]

    def start():
        if filled < rr:
            own_ref[:, filled:rr, :] = jnp.zeros((4, rr - filled, ww), F32)
            recv1_ref[:, filled:rr, :] = jnp.zeros((4, rr - filled, ww), F32)
        for cp in loads() + stage1():
            cp.start()

    def forward():
        for cp in loads():
            cp.wait()
        for cp in stage1():
            cp.wait_recv()
        sends = stage2()
        for r, (cx, cy) in enumerate(chips):
            kk = 2 * cx + cy

            def pack(i, carry, r=r, kk=kk):
                rows = pl.ds(pl.multiple_of(i * chunk, chunk), chunk)
                sendb_ref[r, rows, :] = (own_ref[kk, rows, :] + recv1_ref[kk, rows, :]).astype(BF16)
                return carry

            lax.fori_loop(0, rr // chunk, pack, 0)
            sends[r].start()

    def finish():
        for cp in stage2():
            cp.wait_recv()
        mine = 2 * x + y

        def total(i, carry):
            rows = pl.ds(pl.multiple_of(i * chunk, chunk), chunk)
            acc = own_ref[mine, rows, :] + recv1_ref[mine, rows, :]
            for r in range(3):
                acc = acc + recv2_ref[r, rows, :].astype(F32)
            out_ref[rows, :] = acc
            return carry

        lax.fori_loop(0, rr // chunk, total, 0)
        for cp in stage1() + stage2():
            cp.wait_send()

    return start, forward, finish


def _fwd_in(x2, w1, gq, gkv, wq, wkv, pos, rope_rows, tm):
    tt = x2.shape[0]

    def body(x_ref, w1_ref, gq_ref, gkv_ref, wq_ref, wkv_ref, pos_ref, rr_ref,
             xq_ref, xkv_ref, ga_ref, u_ref, gb_ref, qn_ref, kvn_ref, qcat_ref, kcat_ref, v_ref,
             c_ref, sa_ref, sb_ref):
        ang = pos_ref[...].astype(F32) * rr_ref[0:1, :]
        cos, sin = jnp.cos(ang), jnp.sin(ang)
        c, sa, sb = cos * rr_ref[1:2, :], sin * rr_ref[2:3, :], sin * rr_ref[3:4, :]
        c_ref[...] = c
        sa_ref[...] = sa
        sb_ref[...] = sb
        h = _dot_nt(x_ref[...].astype(BF16), w1_ref[...])
        xq = h[:, C_XQ:C_XKV]
        xkv = h[:, C_XKV:C_KR]
        xq_ref[...] = xq
        xkv_ref[...] = xkv
        ga_ref[...] = h[:, C_GA:C_U]
        u_ref[...] = h[:, C_U:C_GB]
        gb_ref[...] = h[:, C_GB:IN_WP]
        qn =(xq * lax.rsqrt(jnp.mean(xq * xq, axis=-1, keepdims=True) + RMS_EPS) * gq_ref[...]).astype(BF16)
        kvn = (xkv * lax.rsqrt(jnp.mean(xkv * xkv, axis=-1, keepdims=True) + RMS_EPS) * gkv_ref[...]).astype(BF16)
        qn_ref[...] = qn
        kvn_ref[...] = kvn
        q = _dot_nt(qn, wq_ref[...])
        kv = _dot_nt(kvn, wkv_ref[...])
        kr = _rope_fwd(h[:, C_KR:C_GA], c, sa, sb).astype(BF16)
        for hd in range(HEADS):
            lo = hd * HEAD_PAD
            qcat_ref[:, lo:lo + NOPE] = q[:, lo:lo + NOPE].astype(BF16)
            qcat_ref[:, lo + NOPE:lo + HEAD_PAD] = _rope_fwd(q[:, lo + NOPE:lo + HEAD_PAD], c, sa, sb).astype(BF16)
            kcat_ref[:, lo:lo + NOPE] = kv[:, hd * NOPE:(hd + 1) * NOPE].astype(BF16)
            kcat_ref[:, lo + NOPE:lo + HEAD_PAD] = kr
        v_ref[...] = kv[:, MLA_W:].astype(BF16)

    def tile(w):
        return pl.BlockSpec((tm, w), lambda i: (i, 0))

    def whole(a):
        return pl.BlockSpec(a.shape, lambda i: (0, 0))

    outs = [(Q_LORA, F32), (KV_LORA, F32), (MLA_W, F32), (POOL_W, F32), (POOL_W, F32),
            (Q_LORA, BF16), (KV_LORA, BF16), (HEADS * HEAD_PAD, BF16), (HEADS * HEAD_PAD, BF16), (MLA_W, BF16),
            (128, F32), (128, F32), (128, F32)]
    return pl.pallas_call(
        body, name="fwd_in", grid=(tt // tm,),
        in_specs=[tile(D_MODEL), whole(w1), whole(gq), whole(gkv), whole(wq), whole(wkv), tile(1), whole(rope_rows)],
        out_specs=[tile(w) for w, _ in outs],
        out_shape=[jax.ShapeDtypeStruct((tt, w), dt) for w, dt in outs],
        compiler_params=_params(48, ("arbitrary",)))(x2, w1, gq, gkv, wq, wkv, pos, rope_rows)


def _tile_cases(pmax_ref, pmin_ref, b, nhalf, qi, ki):
    q0, q1 = b * nhalf + 2 * qi, b * nhalf + 2 * qi + 1
    k0, k1 = b * nhalf + 2 * ki, b * nhalf + 2 * ki + 1
    needed = jnp.maximum(pmax_ref[q0], pmax_ref[q1]) >= jnp.minimum(pmin_ref[k0], pmin_ref[k1])
    visible = jnp.minimum(pmin_ref[q0], pmin_ref[q1]) >= jnp.maximum(pmax_ref[k0], pmax_ref[k1])
    stepped = pmax_ref[q0] < pmin_ref[k1]
    return needed, visible, stepped


def _attn_fwd(qcat, kcat, v, pos_col, pos_blk, pmax, pmin, tq, tk):
    bb, ss, _ = qcat.shape
    nq, nk = ss // tq, ss // tk
    lanes = 128
    hps = ATTN_FWD_HEADS_PER_STEP

    def body(pmax_ref, pmin_ref, q_ref, k_ref, v_ref, pc_ref, pb_ref, o_ref, lse_ref, m_s, acc_s):
        b, qi = pl.program_id(0), pl.program_id(2)
        m_s[...] = jnp.full(m_s.shape, NEG, F32)
        acc_s[...] = jnp.zeros(acc_s.shape, F32)

        def part(ki, masked, q_lo, q_n, k_n):
            qrows = slice(q_lo, q_lo + q_n)
            krows = pl.ds(pl.multiple_of(ki * tk, tk), k_n)
            if masked:
                mask = pc_ref[qrows, :] >= pb_ref[ki][:, :k_n]
            ones = jnp.ones((k_n, lanes), BF16)
            for hd in range(hps):
                qk = slice(hd * HEAD_PAD, (hd + 1) * HEAD_PAD)
                s = _dot_nt(q_ref[qrows, qk], k_ref[krows, qk]) * SCALE_LOG2
                if masked:
                    s = jnp.where(mask, s, NEG)
                m_prev = m_s[hd, qrows, :]
                m_new = jnp.maximum(m_prev, jnp.max(s, axis=-1, keepdims=True))
                p = jnp.exp2(s - jnp.tile(m_new, (1, k_n // lanes)))
                a = jnp.exp2(m_prev - m_new)
                vv = jnp.concatenate([v_ref[krows, hd * V_DIM:(hd + 1) * V_DIM], ones], axis=1)
                acc_s[hd, qrows, :] = jnp.tile(a, (1, 2)) * acc_s[hd, qrows, :] + _dot(p.astype(BF16), vv)
                m_s[hd, qrows, :] = m_new

        def step(ki, carry):
            needed, visible, stepped = _tile_cases(pmax_ref, pmin_ref, b, 2 * nq, qi, ki)

            @pl.when(needed & visible)
            def _():
                part(ki, False, 0, tq, tk)

            @pl.when(needed & ~visible & stepped)
            def _():
                part(ki, True, 0, tq // 2, tk // 2)
                part(ki, True, tq // 2, tq // 2, tk)

            @pl.when(needed & ~visible & ~stepped)
            def _():
                part(ki, True, 0, tq, tk)

            return carry

        lax.fori_loop(0, nk, step, 0)
        for hd in range(hps):
            acc = acc_s[hd]
            l = acc[:, V_DIM:]
            o_ref[:, hd * V_DIM:(hd + 1) * V_DIM] = acc[:, :V_DIM] / l
            lse_ref[hd] = (m_s[hd] + jnp.log2(l)).T[0:1, :]

    grid_spec = pltpu.PrefetchScalarGridSpec(
        num_scalar_prefetch=2, grid=(bb, HEADS // hps, nq),
        in_specs=[
            pl.BlockSpec((None, tq, hps * HEAD_PAD), lambda b, h, i, *_: (b, i, h)),
            pl.BlockSpec((None, ss, hps * HEAD_PAD), lambda b, h, i, *_: (b, 0, h)),
            pl.BlockSpec((None, ss, hps * V_DIM), lambda b, h, i, *_: (b, 0, h)),
            pl.BlockSpec((None, tq, 1), lambda b, h, i, *_: (b, i, 0)),
            pl.BlockSpec((None, nk, 1, tk), lambda b, h, i, *_: (b, 0, 0, 0)),
        ],
        out_specs=[
            pl.BlockSpec((None, tq, hps * V_DIM), lambda b, h, i, *_: (b, i, h)),
            pl.BlockSpec((None, hps, 1, tq), lambda b, h, i, *_: (b, h, 0, i)),
        ],
        scratch_shapes=[pltpu.VMEM((hps, tq, lanes), F32), pltpu.VMEM((hps, tq, 2 * V_DIM), F32)])
    return pl.pallas_call(
        body, name="attn_fwd", grid_spec=grid_spec,
        out_shape=[jax.ShapeDtypeStruct((bb, ss, MLA_W), F32), jax.ShapeDtypeStruct((bb, HEADS, 1, ss), F32)],
        compiler_params=_params(48, ("arbitrary", "arbitrary", "arbitrary")))(pmax, pmin, qcat, kcat, v, pos_col, pos_blk)


def _post(x, tgt, o, ga, u, gb, w_out, pool_wb, pool_scale, ln_g, ln_b, ts):
    bb, ss, _ = x.shape
    nt = ss // ts
    hb = ts // POOL_HALO

    def body(x_ref, tgt_ref, o_ref, ga_ref, u_ref, uh_ref, gb_ref, wo_ref, pw_ref, ps_ref, lg_ref, lb_ref,
             dz_ref, ycat_ref, do_ref, dga_ref, dgb_ref, dpc_ref, dl_ref, loss_ref, glg_ref, glb_ref, gps_ref, gpw_ref):
        step = pl.program_id(0)
        j = step % nt

        @pl.when(step == 0)
        def _():
            for r in (loss_ref, glg_ref, glb_ref, gps_ref, gpw_ref):
                r[...] = jnp.zeros(r.shape, F32)

        o, ga, u, gb = o_ref[...], ga_ref[...], u_ref[...], gb_ref[...]
        sa, dsa = _silu_parts(ga)
        sb, dsb = _silu_parts(gb)
        ext = jnp.concatenate([jnp.where(j > 0, uh_ref[...], 0.0), u], axis=0)
        s2 = ext + pltpu.roll(ext, 1, 0)
        s4 = s2 + pltpu.roll(s2, 2, 0)
        s8 = s4 + pltpu.roll(s4, 4, 0)
        s16 = s8 + pltpu.roll(s8, 8, 0)
        cnt = _pool_cnt(j * ts, ts)
        pooled = (_pick_groups(s2, s4, s8, s16, POOL_HALO, POOL_HALO + ts) / cnt - u).astype(BF16)
        mixed = jnp.concatenate(
            [_dot(pooled[:, g * POOL_C:(g + 1) * POOL_C], pw_ref[g]) for g in range(POOL_G)], axis=1)
        ps = ps_ref[...]
        scaled = mixed * ps
        ycat = jnp.concatenate([o * sa, scaled * sb], axis=1).astype(BF16)
        ycat_ref[...] = ycat
        z = ALPHA * x_ref[...] + _dot(ycat, wo_ref[...])
        mu = jnp.mean(z, axis=-1, keepdims=True)
        zc = z - mu
        rstd = lax.rsqrt(jnp.mean(zc * zc, axis=-1, keepdims=True) + LN_EPS)
        xhat = zc * rstd
        lg = lg_ref[...]
        diff = xhat * lg + lb_ref[...] - tgt_ref[...]
        loss_ref[...] += jnp.sum(diff * diff) * (0.5 / D_MODEL)
        dy = diff * (1.0 / D_MODEL)
        glb_ref[...] += jnp.sum(dy, axis=0, keepdims=True)
        glg_ref[...] += jnp.sum(dy * xhat, axis=0, keepdims=True)
        dxh = dy * lg
        dz = rstd * (dxh - jnp.mean(dxh, axis=-1, keepdims=True) - xhat * jnp.mean(dxh * xhat, axis=-1, keepdims=True))
        dz_ref[...] = dz
        dycat = _dot_nt(dz.astype(BF16), wo_ref[...])
        dya, dyb = dycat[:, :MLA_W], dycat[:, MLA_W:]
        do = dya * sa
        do_ref[...] = do.astype(BF16)
        doo = do * o
        for hd in range(HEADS):
            dl_ref[hd] = jnp.sum(doo[:, hd * V_DIM:(hd + 1) * V_DIM].T, axis=0, keepdims=True)
        dga_ref[...] = (dya * o * dsa).astype(BF16)
        dgb_ref[...] = (dyb * scaled * dsb).astype(BF16)
        dscaled = dyb * sb
        gps_ref[...] += jnp.sum(dscaled * mixed, axis=0, keepdims=True)
        dmixed = (dscaled * ps).astype(BF16)
        dpooled = []
        for g in range(POOL_G):
            cols = slice(g * POOL_C, (g + 1) * POOL_C)
            gpw_ref[g] += _dot_tn(pooled[:, cols], dmixed[:, cols])
            dpooled.append(_dot_nt(dmixed[:, cols], pw_ref[g]))
        dpc_ref[...] = jnp.concatenate(dpooled, axis=1) / cnt

    def tile(w):
        return pl.BlockSpec((None, ts, w), lambda i: (i // nt, i % nt, 0))

    def whole(a):
        nd = a.ndim
        return pl.BlockSpec(a.shape, lambda i: (0,) * nd)

    halo = pl.BlockSpec((None, POOL_HALO, POOL_W), lambda i: (i // nt, jnp.maximum((i % nt) * hb - 1, 0), 0))
    acc_shapes = [(8, 128), (1, D_MODEL), (1, D_MODEL), (1, POOL_W), (POOL_G, POOL_C, POOL_C)]
    tile_outs = [(D_MODEL, F32), (D_MODEL, BF16), (MLA_W, BF16), (MLA_W, BF16), (POOL_W, BF16), (POOL_W, F32)]
    return pl.pallas_call(
        body, name="post", grid=(bb * nt,),
        in_specs=[tile(D_MODEL), tile(D_MODEL), tile(MLA_W), tile(MLA_W), tile(POOL_W), halo, tile(POOL_W),
                  whole(w_out), whole(pool_wb), whole(pool_scale), whole(ln_g), whole(ln_b)],
        out_specs=[tile(w) for w, _ in tile_outs]
        + [pl.BlockSpec((None, HEADS, 1, ts), lambda i: (i // nt, 0, 0, i % nt))]
        + [pl.BlockSpec(s, lambda i, n=len(s): (0,) * n) for s in acc_shapes],
        out_shape=[jax.ShapeDtypeStruct((bb, ss, w), dt) for w, dt in tile_outs]
        + [jax.ShapeDtypeStruct((bb, HEADS, 1, ss), F32)]
        + [jax.ShapeDtypeStruct(s, F32) for s in acc_shapes],
        compiler_params=_params(48, ("arbitrary",)))(x, tgt, o, ga, u, u, gb, w_out, pool_wb, pool_scale, ln_g, ln_b)


def _attn_bwd(qcat, kcat, v, do, lse, dl, rc, rsa, rsb, pos_col, pos_blk, pmax, pmin, tq, tk):
    bb, ss, _ = qcat.shape
    nq, nk = ss // tq, ss // tk
    hps = ATTN_BWD_HEADS_PER_STEP

    def body(pmax_ref, pmin_ref, q_ref, k_ref, v_ref, do_ref, lse_ref, dl_ref, c_ref, sa_ref, sb_ref, pc_ref, pb_ref,
             dqp_ref, dkn_ref, dv_ref, dkr_ref, dq_s, dk_s, dv_s):
        b = pl.program_id(0)
        dq_s[...] = jnp.zeros(dq_s.shape, F32)

        def part(qi, ki, masked):
            krows = pl.ds(pl.multiple_of(ki * tk, tk), tk)
            qrows = pl.ds(pl.multiple_of(qi * tq, tq), tq)
            if masked:
                mask = pb_ref[qi] >= pc_ref[krows, :]
            for hd in range(hps):
                qk = slice(hd * HEAD_PAD, (hd + 1) * HEAD_PAD)
                vs = slice(hd * V_DIM, (hd + 1) * V_DIM)
                q = q_ref[qrows, qk]
                dd = do_ref[qrows, vs]
                st = _dot_nt(k_ref[krows, qk], q) * SCALE_LOG2
                if masked:
                    st = jnp.where(mask, st, NEG)
                pt = jnp.exp2(st - lse_ref[hd, qi])
                dv_s[:, vs] += _dot(pt.astype(BF16), dd)
                dpt = _dot_nt(v_ref[krows, vs], dd)
                dst = (pt * (dpt - dl_ref[hd, qi]) * SCALE).astype(BF16)
                dk_s[:, qk] += _dot(dst, q)
                dq_s[qrows, qk] += _dot_tn(dst, k_ref[krows, qk])

        def kv_step(ki, carry):
            krows = pl.ds(pl.multiple_of(ki * tk, tk), tk)
            dk_s[...] = jnp.zeros(dk_s.shape, F32)
            dv_s[...] = jnp.zeros(dv_s.shape, F32)

            def q_step(qi, c2):
                needed, visible, _ = _tile_cases(pmax_ref, pmin_ref, b, 2 * nq, qi, ki)

                @pl.when(needed & visible)
                def _():
                    part(qi, ki, False)

                @pl.when(needed & ~visible)
                def _():
                    part(qi, ki, True)

                return c2

            lax.fori_loop(0, nq, q_step, 0)
            dkr = jnp.zeros((tk, HEAD_PAD - NOPE), F32)
            for hd in range(hps):
                lo = hd * HEAD_PAD
                dkn_ref[krows, hd * NOPE:(hd + 1) * NOPE] = dk_s[:, lo:lo + NOPE].astype(BF16)
                dkr = dkr + dk_s[:, lo + NOPE:lo + HEAD_PAD]
            dkr_ref[krows, :] = dkr
            dv_ref[krows, :] = dv_s[...].astype(BF16)
            return carry

        lax.fori_loop(0, nk, kv_step, 0)
        c, sa, sb = c_ref[...], sa_ref[...], sb_ref[...]
        for hd in range(hps):
            lo = hd * HEAD_PAD
            dqp_ref[:, lo:lo + NOPE] = dq_s[:, lo:lo + NOPE].astype(BF16)
            dqp_ref[:, lo + NOPE:lo + HEAD_PAD] = _rope_bwd(dq_s[:, lo + NOPE:lo + HEAD_PAD], c, sa, sb).astype(BF16)

    def per_head(w):
        return pl.BlockSpec((None, ss, hps * w), lambda b, h, *_: (b, 0, h))

    def rows_of_head():
        return pl.BlockSpec((None, hps, nq, 1, tq), lambda b, h, *_: (b, h, 0, 0, 0))

    def per_batch(w):
        return pl.BlockSpec((None, ss, w), lambda b, h, *_: (b, 0, 0))

    grid_spec = pltpu.PrefetchScalarGridSpec(
        num_scalar_prefetch=2, grid=(bb, HEADS // hps),
        in_specs=[per_head(HEAD_PAD), per_head(HEAD_PAD), per_head(V_DIM), per_head(V_DIM),
                  rows_of_head(), rows_of_head(), per_batch(128), per_batch(128), per_batch(128), per_batch(1),
                  pl.BlockSpec((None, nq, 1, tq), lambda b, h, *_: (b, 0, 0, 0))],
        out_specs=[per_head(HEAD_PAD), per_head(NOPE), per_head(V_DIM),
                   pl.BlockSpec((None, None, ss, HEAD_PAD - NOPE), lambda b, h, *_: (b, h, 0, 0))],
        scratch_shapes=[pltpu.VMEM((ss, hps * HEAD_PAD), F32), pltpu.VMEM((tk, hps * HEAD_PAD), F32),
                        pltpu.VMEM((tk, hps * V_DIM), F32)])
    return pl.pallas_call(
        body, name="attn_bwd", grid_spec=grid_spec,
        out_shape=[jax.ShapeDtypeStruct((bb, ss, HEADS * HEAD_PAD), BF16),
                   jax.ShapeDtypeStruct((bb, ss, MLA_W), BF16),
                   jax.ShapeDtypeStruct((bb, ss, MLA_W), BF16),
                   jax.ShapeDtypeStruct((bb, HEADS // hps, ss, HEAD_PAD - NOPE), F32)],
        compiler_params=_params(56, ("arbitrary", "arbitrary")))(
            pmax, pmin, qcat, kcat, v, do, lse, dl, rc, rsa, rsb, pos_col, pos_blk)


def _bwd_mid(dqp, dkn, dv, dkr, xq, xkv, rc, rsa, rsb, wq, wkv, gq, gkv, dga, dgb, dpc, dz, w1, ts):
    bb, ss, _ = dz.shape
    nt = ss // ts
    hb = ts // POOL_HALO
    groups = dkr.shape[1]

    def body(dqp_ref, dkn_ref, dv_ref, dkr_ref, xq_ref, xkv_ref, c_ref, sa_ref, sb_ref, wq_ref, wkv_ref, gq_ref,
             gkv_ref, dga_ref, dgb_ref, dpc_ref, dph_ref, dz_ref, w1_ref,
             gx_ref, dh_ref, ggq_ref, ggkv_ref):
        step = pl.program_id(0)
        j = step % nt

        @pl.when(step == 0)
        def _():
            ggq_ref[...] = jnp.zeros(ggq_ref.shape, F32)
            ggkv_ref[...] = jnp.zeros(ggkv_ref.shape, F32)

        dkr = dkr_ref[0]
        for g in range(1, groups):
            dkr = dkr + dkr_ref[g]
        dh_ref[:, C_KR:C_GA] = _rope_bwd(dkr, c_ref[...], sa_ref[...], sb_ref[...]).astype(BF16)

        def rms_bwd(x, g, dn):
            inv = lax.rsqrt(jnp.mean(x * x, axis=-1, keepdims=True) + RMS_EPS)
            xh = x * inv
            dxh = dn * g
            return inv * (dxh - xh * jnp.mean(dxh * xh, axis=-1, keepdims=True)), jnp.sum(dn * xh, axis=0, keepdims=True)

        dxq, ggq = rms_bwd(xq_ref[...], gq_ref[...], _dot(dqp_ref[...], wq_ref[...]))
        dkvn = _dot(dkn_ref[...], wkv_ref[:MLA_W, :]) + _dot(dv_ref[...], wkv_ref[MLA_W:, :])
        dxkv, ggkv = rms_bwd(xkv_ref[...], gkv_ref[...], dkvn)
        ggq_ref[...] += ggq
        ggkv_ref[...] += ggkv
        dh_ref[:, C_XQ:C_XKV] = dxq.astype(BF16)
        dh_ref[:, C_XKV:C_KR] = dxkv.astype(BF16)
        dh_ref[:, C_GA:C_U] = dga_ref[...]
        dh_ref[:, C_GB:IN_WP] = dgb_ref[...]
        dpc = dpc_ref[...]
        n = ts + POOL_HALO
        ext = jnp.concatenate([dpc, jnp.where(j < nt - 1, dph_ref[...], 0.0)], axis=0)
        r2 = ext + pltpu.roll(ext, n - 1, 0)
        r4 = r2 + pltpu.roll(r2, n - 2, 0)
        r8 = r4 + pltpu.roll(r4, n - 4, 0)
        r16 = r8 + pltpu.roll(r8, n - 8, 0)
        du = _pick_groups(r2, r4, r8, r16, 0, ts) - dpc * _pool_cnt(j * ts, ts)
        dh_ref[:, C_U:C_GB] = du.astype(BF16)
        gx_ref[...] = ALPHA * dz_ref[...] + _dot(dh_ref[...], w1_ref[...])

    def tile(w):
        return pl.BlockSpec((None, ts, w), lambda i: (i // nt, i % nt, 0))

    def whole(a):
        return pl.BlockSpec(a.shape, lambda i: (0, 0))

    halo = pl.BlockSpec((None, POOL_HALO, POOL_W),
                        lambda i: (i // nt, jnp.minimum((i % nt + 1) * hb, ss // POOL_HALO - 1), 0))
    return pl.pallas_call(
        body, name="bwd_mid", grid=(bb * nt,),
        in_specs=[tile(HEADS * HEAD_PAD), tile(MLA_W), tile(MLA_W),
                  pl.BlockSpec((None, groups, ts, HEAD_PAD - NOPE), lambda i: (i // nt, 0, i % nt, 0)),
                  tile(Q_LORA), tile(KV_LORA),
                  tile(128), tile(128), tile(128), whole(wq), whole(wkv), whole(gq), whole(gkv),
                  tile(MLA_W), tile(POOL_W), tile(POOL_W), halo, tile(D_MODEL), whole(w1)],
        out_specs=[tile(D_MODEL), tile(IN_WP),
                   pl.BlockSpec((1, Q_LORA), lambda i: (0, 0)), pl.BlockSpec((1, KV_LORA), lambda i: (0, 0))],
        out_shape=[jax.ShapeDtypeStruct((bb, ss, D_MODEL), F32), jax.ShapeDtypeStruct((bb, ss, IN_WP), BF16),
                   jax.ShapeDtypeStruct((1, Q_LORA), F32), jax.ShapeDtypeStruct((1, KV_LORA), F32)],
        compiler_params=_params(48, ("arbitrary",)))(
            dqp, dkn, dv, dkr, xq, xkv, rc, rsa, rsb, wq, wkv, gq, gkv, dga, dgb, dpc, dpc, dz, w1)


def _grad_w(pairs, bt, name, b_cols=None, carried=None, reduced=None):
    tt = pairs[0][0].shape[0]
    steps = tt // bt
    npairs = len(pairs)
    n_rs = len(_rs_scratch(carried)) if carried is not None else 0

    def body(*refs):
        ab, rest = refs[:2 * npairs], list(refs[2 * npairs:])
        g_hbm = rest.pop(0) if carried is not None else None
        part = rest.pop(0) if reduced is not None else None
        outs = [rest.pop(0) for _ in range(npairs)]
        phases = []
        if carried is not None:
            rs_out = rest.pop(0)
        if reduced is not None:
            sum_ref = rest.pop(0)
        if carried is not None:
            phases.append(_rs_phases(carried, g_hbm, rs_out, *rest[:n_rs]))
        if reduced is not None:
            gathered, *sems = rest[n_rs:]
            phases.append(_ag_phases(part, gathered, *sems, sum_ref=sum_ref))
        step = pl.program_id(0)

        @pl.when(step == 0)
        def _():
            for o in outs:
                o[...] = jnp.zeros(o.shape, F32)
            for start, _, _ in phases:
                start()

        for i in range(npairs):
            outs[i][...] += _dot_tn(ab[2 * i][...].astype(BF16), ab[2 * i + 1][...].astype(BF16))

        @pl.when(step == min(2, steps - 1))
        def _():
            for _, forward, _ in phases:
                forward()

        @pl.when(step == steps - 1)
        def _():
            for _, _, finish in phases:
                finish()

    in_specs, out_specs, out_shape = [], [], []
    for a, b in pairs:
        m, (n, col) = a.shape[1], b_cols or (b.shape[1], 0)
        in_specs += [pl.BlockSpec((bt, m), lambda i: (i, 0)), pl.BlockSpec((bt, n), lambda i, col=col: (i, col))]
        out_specs.append(pl.BlockSpec((m, n), lambda i: (0, 0)))
        out_shape.append(jax.ShapeDtypeStruct((m, n), F32))
    args = [t for p in pairs for t in p]
    scratch = []
    if carried is not None:
        in_specs.append(pl.BlockSpec(memory_space=pl.ANY))
        out_specs.append(pl.BlockSpec((carried.rows, carried.width), lambda i: (0, 0)))
        out_shape.append(jax.ShapeDtypeStruct((carried.rows, carried.width), F32))
        args.append(carried.array)
        scratch += _rs_scratch(carried)
    if reduced is not None:
        in_specs.append(pl.BlockSpec(memory_space=pl.ANY))
        out_specs.append(pl.BlockSpec(reduced.shape, lambda i: (0, 0)))
        out_shape.append(jax.ShapeDtypeStruct(reduced.shape, F32))
        args.append(reduced)
        scratch += [pltpu.VMEM((N_DEV * reduced.shape[0], reduced.shape[1]), F32)] + AG_SEMS
    return pl.pallas_call(
        body, name=name, grid=(steps,), in_specs=in_specs, out_specs=out_specs, out_shape=out_shape,
        scratch_shapes=scratch, compiler_params=_params(56, ("arbitrary",)))(*args)


def _adamw(triples):
    n = len(triples)

    def body(*refs):
        ins, outs = refs[:4 * n], refs[4 * n:]
        for i in range(n):
            w, g, m, v = (r[...] for r in ins[4 * i:4 * i + 4])
            m = ADAM_B1 * m + (1.0 - ADAM_B1) * g
            v = ADAM_B2 * v + (1.0 - ADAM_B2) * jnp.square(g)
            m_hat = m / (1.0 - ADAM_B1 ** ADAM_STEP)
            v_hat = v / (1.0 - ADAM_B2 ** ADAM_STEP)
            outs[3 * i][...] = -ADAM_LR * (m_hat / (jnp.sqrt(v_hat) + ADAM_EPS) + ADAM_WD * w)
            outs[3 * i + 1][...] = m
            outs[3 * i + 2][...] = v

    flat = [a for t in triples for a in t]
    vmem = pl.BlockSpec(memory_space=pltpu.VMEM)
    res = pl.pallas_call(
        body, name="adamw", in_specs=[vmem] * len(flat), out_specs=[vmem] * (3 * n),
        out_shape=[jax.ShapeDtypeStruct(t[0].shape, F32) for t in triples for _ in range(3)],
        compiler_params=_params(48))(*flat)
    return [tuple(res[3 * i:3 * i + 3]) for i in range(n)]


def _shard_slab(w_in, w_uq, w_ukv, w_out):
    mixed = jnp.concatenate(
        [_pad_rows(w_uq.T, R_MIX), w_ukv.T, jnp.zeros((R_MIX, 1024 - Q_LORA - KV_LORA), F32)], axis=1)
    return jnp.concatenate([w_out, mixed, w_in.T, jnp.zeros((SLAB_ROWS - O_IN - R_IN, 1024), F32)], axis=0)


def _unpack_weights(slabs):
    w_out = slabs[:, :R_OUT].reshape(D_MODEL, D_MODEL)
    uq = slabs[:, O_MIX:O_MIX + R_UQ, :Q_LORA].reshape(HEADS, QK_DIM, Q_LORA)
    wqt = jnp.pad(uq, ((0, 0), (0, HEAD_PAD - QK_DIM), (0, 0))).reshape(HEADS * HEAD_PAD, Q_LORA)
    ukv = slabs[:, O_MIX:O_MIX + R_MIX, Q_LORA:Q_LORA + KV_LORA].reshape(HEADS, 2, NOPE, KV_LORA)
    wkvt = ukv.transpose(1, 0, 2, 3).reshape(2 * MLA_W, KV_LORA)
    raw = slabs[:, O_IN:O_IN + R_IN].reshape(IN_W, D_MODEL)
    w1t = jnp.concatenate([raw[:768 + ROPE], jnp.zeros((128 - ROPE, D_MODEL), BF16), raw[768 + ROPE:]], axis=0)
    return w1t, wqt, wkvt, w_out


def _mixed_band(g_wqt, g_wkvt):
    uq = g_wqt.reshape(HEADS, HEAD_PAD, Q_LORA)[:, :QK_DIM].reshape(N_DEV, R_UQ, Q_LORA)
    uq = jnp.pad(uq, ((0, 0), (0, R_MIX - R_UQ), (0, 0)))
    ukv = g_wkvt.reshape(2, HEADS, NOPE, KV_LORA).transpose(1, 0, 2, 3).reshape(N_DEV, R_MIX, KV_LORA)
    return jnp.concatenate([uq, ukv, jnp.zeros((N_DEV, R_MIX, 1024 - Q_LORA - KV_LORA), F32)], axis=2)


def _rope_rows():
    half = ROPE // 2
    inv_freq = ROPE_THETA ** (-jnp.arange(half, dtype=F32) / half)
    zero, one = jnp.zeros((half,), F32), jnp.ones((half,), F32)
    rows = [jnp.concatenate(r) for r in (
        (inv_freq, inv_freq, zero, zero), (one, one, zero, zero), (-one, zero, zero, zero), (zero, one, zero, zero))]
    return jnp.stack(rows + [jnp.zeros((128,), F32)] * 4)


def _pad_rows(a, rows):
    return jnp.pad(a, ((0, rows - a.shape[0]), (0, 0)))


def kernel(x, positions, w_in, q_norm_g, w_uq, kv_norm_g, w_ukv, pool_w, pool_scale, w_out, ln_g, ln_b, loss_target, m_w_in, m_q_norm_g, m_w_uq, m_kv_norm_g, m_w_ukv, m_pool_w, m_pool_scale, m_w_out, m_ln_g, m_ln_b, v_w_in, v_q_norm_g, v_w_uq, v_kv_norm_g, v_w_ukv, v_pool_w, v_pool_scale, v_w_out, v_ln_g, v_ln_b):
    bb, ss, _ = x.shape
    tt = bb * ss
    atile = min(512, ss)
    nblk = ss // atile

    slab = _shard_slab(w_in, w_uq, w_ukv, w_out).astype(BF16)
    slabs = _all_gather(slab, "gather_weights").reshape(N_DEV, SLAB_ROWS, 1024)
    w1, wq, wkv, wo = _unpack_weights(slabs)

    gq, gkv = q_norm_g.reshape(1, Q_LORA), kv_norm_g.reshape(1, KV_LORA)
    ps = pool_scale.reshape(1, POOL_W)
    pos_col = positions.reshape(bb, ss, 1)
    pos_blk = positions.reshape(bb, nblk, 1, atile)
    pmax = jnp.max(positions.reshape(bb, 2 * nblk, atile // 2), axis=-1).reshape(-1)
    pmin = jnp.min(positions.reshape(bb, 2 * nblk, atile // 2), axis=-1).reshape(-1)

    x2 = x.reshape(tt, D_MODEL)
    xq, xkv, ga, u, gb, qn, kvn, qcat, kcat, v, rc, rsa, rsb = _fwd_in(
        x2, w1, gq, gkv, wq, wkv, pos_col.reshape(tt, 1), _rope_rows(), atile)
    as3 = lambda a: a.reshape(bb, ss, a.shape[-1])
    qcat, kcat, v = as3(qcat), as3(kcat), as3(v)
    o, lse = _attn_fwd(qcat, kcat, v, pos_col, pos_blk, pmax, pmin, atile, atile)
    dz, ycat, do, dga, dgb, dpc, dl, loss_p, g_lng, g_lnb, g_ps, g_pw = _post(
        x, loss_target, o, as3(ga), as3(u), as3(gb), wo, pool_w.astype(BF16), ps, ln_g, ln_b, atile)

    bt = min(1024, tt)
    g_wo, = _grad_w([(ycat.reshape(tt, D_MODEL), dz.reshape(tt, D_MODEL))], bt, "grad_w_out")
    rows5 = lambda a: a.reshape(bb, HEADS, nblk, 1, atile)
    rc, rsa, rsb = as3(rc), as3(rsa), as3(rsb)
    dqp, dkn, dv, dkr = _attn_bwd(
        qcat, kcat, v, do, rows5(lse), rows5(dl), rc, rsa, rsb, pos_col, pos_blk, pmax, pmin, atile, atile)
    g_wq, g_wkn, g_wv, rs_out = _grad_w(
        [(dqp.reshape(tt, HEADS * HEAD_PAD), qn), (dkn.reshape(tt, MLA_W), kvn), (dv.reshape(tt, MLA_W), kvn)], bt,
        "grad_w_uqkv", carried=_blocks(g_wo.reshape(N_DEV, R_OUT, D_MODEL)))
    g_wkv = jnp.concatenate([g_wkn, g_wv], axis=0)
    grad_x, dh, g_gq, g_gkv = _bwd_mid(
        dqp, dkn, dv, dkr, as3(xq), as3(xkv), rc, rsa, rsb, wq, wkv, gq, gkv, dga, dgb, dpc, dz, w1, atile)
    small = jnp.concatenate(
        [_pad_rows(g_lng.reshape(8, 128), 8), _pad_rows(g_lnb.reshape(8, 128), 8), _pad_rows(g_gq.reshape(4, 128), 8),
         _pad_rows(g_gkv.reshape(2, 128), 8), _pad_rows(g_ps.reshape(4, 128), 8), g_pw.reshape(POOL_G * POOL_C, 128),
         loss_p], axis=0)
    dh2, half = dh.reshape(tt, IN_WP), D_MODEL // 2
    g_w1a, rs_mix, small = _grad_w([(dh2, x2)], bt, "grad_w_in_a", b_cols=(half, 0),
                                   carried=_blocks(_mixed_band(g_wq, g_wkv)), reduced=small)
    g_w1b, rs_in_a = _grad_w([(dh2, x2)], bt, "grad_w_in_b", b_cols=(half, 1), carried=_w_in_blocks(g_w1a))
    rs_in = jnp.concatenate([rs_in_a, _reduce_scatter(_w_in_blocks(g_w1b), "reduce_scatter_w_in")], axis=1)
    loss = small[40 + POOL_G * POOL_C, 0]
    grads = {
        "w_in": rs_in[:R_IN],
        "q_norm_g": small[16:20].reshape(1, Q_LORA),
        "w_uq": rs_mix[:R_UQ, :Q_LORA],
        "kv_norm_g": small[24:26].reshape(1, KV_LORA),
        "w_ukv": rs_mix[:, Q_LORA:Q_LORA + KV_LORA].T,
        "pool_w": small[40:40 + POOL_G * POOL_C],
        "pool_scale": small[32:36].reshape(1, POOL_W),
        "w_out": rs_out,
        "ln_g": small[0:8].reshape(1, D_MODEL),
        "ln_b": small[8:16].reshape(1, D_MODEL),
    }
    transposed = ("w_in", "w_uq")

    names = ["w_in", "q_norm_g", "w_uq", "kv_norm_g", "w_ukv", "pool_w", "pool_scale", "w_out", "ln_g", "ln_b"]
    weights = dict(w_in=w_in, q_norm_g=q_norm_g, w_uq=w_uq, kv_norm_g=kv_norm_g, w_ukv=w_ukv, pool_w=pool_w,
                   pool_scale=pool_scale, w_out=w_out, ln_g=ln_g, ln_b=ln_b)
    moms = dict(w_in=(m_w_in, v_w_in), q_norm_g=(m_q_norm_g, v_q_norm_g), w_uq=(m_w_uq, v_w_uq),
                kv_norm_g=(m_kv_norm_g, v_kv_norm_g), w_ukv=(m_w_ukv, v_w_ukv), pool_w=(m_pool_w, v_pool_w),
                pool_scale=(m_pool_scale, v_pool_scale), w_out=(m_w_out, v_w_out), ln_g=(m_ln_g, v_ln_g),
                ln_b=(m_ln_b, v_ln_b))
    as2 = lambda a, n: a.T if n in transposed else a.reshape(grads[n].shape)
    upd = _adamw([(as2(weights[n], n), grads[n], as2(moms[n][0], n), as2(moms[n][1], n)) for n in names])
    shaped = lambda a, n: a.T if n in transposed else a.reshape(weights[n].shape)
    return (loss, grad_x,
            *[shaped(grads[n], n) for n in names],
            *[shaped(upd[i][0], n) for i, n in enumerate(names)],
            *[shaped(upd[i][1], n) for i, n in enumerate(names)],
            *[shaped(upd[i][2], n) for i, n in enumerate(names)])
```

```python
import functools

import jax
import jax.numpy as jnp
from jax import lax
from jax.experimental import pallas as pl
from jax.experimental.pallas import tpu as pltpu

F32 = jnp.float32
BF16 = jnp.bfloat16
MESH = pl.DeviceIdType.MESH

N_DEV = 8
D_MODEL = 1024
HEADS = 4
NOPE = 128
ROPE = 64
V_DIM = 128
QK_DIM = NOPE + ROPE
HEAD_PAD = 256
Q_LORA = 512
KV_LORA = 256
MLA_W = HEADS * V_DIM
POOL_W = 512
POOL_G = 4
POOL_C = 128
POOL_HALO = 16
IN_W = 2368
IN_WP = 2432
C_XQ, C_XKV, C_KR, C_GA, C_U, C_GB = 0, 512, 768, 896, 1408, 1920
ROPE_THETA = 10000.0
RMS_EPS = 1e-6
LN_EPS = 1e-5
ALPHA = 2.0 ** 0.25
SCALE = QK_DIM ** -0.5
SCALE_LOG2 = SCALE * 1.4426950408889634
NEG = float(jnp.finfo(jnp.float32).min)

ADAM_LR = 0.001
ADAM_B1 = 0.9
ADAM_B2 = 0.999
ADAM_EPS = 1e-08
ADAM_WD = 0.01
ADAM_STEP = 10

R_OUT, R_MIX, R_UQ, R_IN = 128, 128, 96, 296
O_MIX, O_IN = 0, R_MIX
SLAB_ROWS = 432
R_IN_PAD = 304

V7X_VMEM_BYTES = 64 * 1024 * 1024
ATTN_FWD_HEADS_PER_STEP = 4
ATTN_BWD_HEADS_PER_STEP = 2


def _params(vmem_mb, semantics=None):
    assert vmem_mb * 1024 * 1024 < V7X_VMEM_BYTES
    return pltpu.CompilerParams(vmem_limit_bytes=vmem_mb * 1024 * 1024, dimension_semantics=semantics)


def _dot(a, b):
    return jnp.dot(a, b, preferred_element_type=F32)


def _dot_nt(a, b):
    return lax.dot_general(a, b, (((1,), (1,)), ((), ())), preferred_element_type=F32)


def _dot_tn(a, b):
    return lax.dot_general(a, b, (((0,), (0,)), ((), ())), preferred_element_type=F32)


def _rope_fwd(t, c, sa, sb):
    return t * c + pltpu.roll(t, 96, 1) * sa + pltpu.roll(t, 32, 1) * sb


def _rope_bwd(d, c, sa, sb):
    return d * c + pltpu.roll(d * sa, 32, 1) + pltpu.roll(d * sb, 96, 1)


def _silu_parts(g):
    sig = jax.nn.sigmoid(g)
    return g * sig, sig * (1.0 + g * (1.0 - sig))


def _pool_cnt(row0, rows):
    t = row0 + lax.broadcasted_iota(jnp.int32, (rows, POOL_W), 0)
    w = 2 << (lax.broadcasted_iota(jnp.int32, (rows, POOL_W), 1) // POOL_C)
    return jnp.minimum(t + 1, w).astype(F32)


def _pick_groups(s2, s4, s8, s16, lo, hi):
    return jnp.concatenate([s2[lo:hi, 0:128], s4[lo:hi, 128:256], s8[lo:hi, 256:384], s16[lo:hi, 384:512]], axis=1)


AG_SEMS = [pltpu.SemaphoreType.DMA((7,)), pltpu.SemaphoreType.DMA((7,)), pltpu.SemaphoreType.DMA]


def _ag_phases(x_ref, out_ref, send_sems, recv_sems, local_sem, sum_ref=None):
    m_per = x_ref.shape[0]
    x, y, c = lax.axis_index("x"), lax.axis_index("y"), lax.axis_index("c")
    me, sibling = (x, y, c), (x, y, 1 - c)
    chips = [(1 - x, y), (x, 1 - y), (1 - x, 1 - y)]

    def rows(px, py, pc):
        return out_ref.at[pl.ds((4 * px + 2 * py + pc) * m_per, m_per), :]

    def copy(k, block, to, src=None):
        return pltpu.make_async_remote_copy(
            src_ref=rows(*block) if src is None else src, dst_ref=rows(*block),
            send_sem=send_sems.at[k], recv_sem=recv_sems.at[k], device_id=to, device_id_type=MESH)

    def mine():
        return pltpu.make_async_copy(x_ref, rows(*me), local_sem)

    def first():
        return [copy(0, me, sibling, src=x_ref)] + [copy(1 + j, me, (*chip, c), src=x_ref) for j, chip in enumerate(chips)]

    def passed():
        return [copy(4 + j, (*chip, c), sibling) for j, chip in enumerate(chips)]

    def start():
        for cp in [mine()] + first():
            cp.start()

    def forward():
        for j, (chip, cp) in enumerate(zip(chips, passed())):
            copy(1 + j, (*chip, c), me).wait_recv()
            cp.start()

    def finish():
        copy(0, sibling, me).wait_recv()
        for j, chip in enumerate(chips):
            copy(4 + j, (*chip, 1 - c), me).wait_recv()
        for cp in first() + passed():
            cp.wait_send()
        mine().wait()
        if sum_ref is not None:
            acc = out_ref[pl.ds(0, m_per), :]
            for d in range(1, N_DEV):
                acc = acc + out_ref[pl.ds(d * m_per, m_per), :]
            sum_ref[...] = acc

    return start, forward, finish


def _all_gather(shard, name):
    m_per, n = shard.shape

    def body(x_ref, out_ref, *sems):
        for phase in _ag_phases(x_ref, out_ref, *sems):
            phase()

    vmem = pl.BlockSpec(memory_space=pltpu.VMEM)
    return pl.pallas_call(
        body, name=name, out_shape=jax.ShapeDtypeStruct((N_DEV * m_per, n), shard.dtype),
        in_specs=[vmem], out_specs=vmem, scratch_shapes=AG_SEMS, compiler_params=_params(32))(shard)


def _reduce_scatter(sc, name):
    def body(g_hbm, out_ref, *scratch):
        for phase in _rs_phases(sc, g_hbm, out_ref, *scratch):
            phase()

    return pl.pallas_call(
        body, name=name, out_shape=jax.ShapeDtypeStruct((sc.rows, sc.width), F32),
        in_specs=[pl.BlockSpec(memory_space=pl.ANY)], out_specs=pl.BlockSpec(memory_space=pltpu.VMEM),
        scratch_shapes=_rs_scratch(sc), compiler_params=_params(32))(sc.array)


class _Scattered:
    def __init__(self, array, rows, pieces, locate):
        self.array, self.rows, self.pieces, self.locate = array, rows, pieces, locate
        self.width = array.shape[-1]


def _blocks(g):
    return _Scattered(g, g.shape[1], ((0, g.shape[1]),), lambda ref, d, row, rows: ref.at[d])


def _w_in_blocks(g_w1t):
    def locate(ref, d, row, rows):
        r = R_IN * d + row
        return ref.at[pl.ds(pl.multiple_of(r + jnp.where(r >= C_KR + ROPE, 128 - ROPE, 0), 8), rows), :]

    cut = C_KR + ROPE - 2 * R_IN
    return _Scattered(g_w1t, R_IN_PAD, ((0, cut), (cut, R_IN - cut)), locate)


def _rs_scratch(sc):
    rr, ww, n = sc.rows, sc.width, 4 * len(sc.pieces)
    return [pltpu.VMEM((4, rr, ww), F32), pltpu.VMEM((4, rr, ww), F32),
            pltpu.VMEM((3, rr, ww), BF16), pltpu.VMEM((3, rr, ww), BF16),
            pltpu.SemaphoreType.DMA((n,)), pltpu.SemaphoreType.DMA((n,)), pltpu.SemaphoreType.DMA((n,)),
            pltpu.SemaphoreType.DMA((3,)), pltpu.SemaphoreType.DMA((3,))]


def _rs_phases(sc, g_hbm, out_ref, own_ref, recv1_ref, sendb_ref, recv2_ref, ld_sems, s1_send, s1_recv, s2_send, s2_recv):
    rr, ww = out_ref.shape
    chunk = next(c for c in (128, 80, 64, 48, 32, 16) if rr % c == 0)
    x, y, c = lax.axis_index("x"), lax.axis_index("y"), lax.axis_index("c")
    chips = [(1 - x, y), (x, 1 - y), (1 - x, 1 - y)]
    npieces = len(sc.pieces)
    filled = sum(rows for _, rows in sc.pieces)

    def pieces(d, buf, k):
        for p, (row, rows) in enumerate(sc.pieces):
            dst = buf.at[k] if (row, rows) == (0, rr) else buf.at[k, pl.ds(row, rows), :]
            yield k * npieces + p, sc.locate(g_hbm, d, row, rows), dst

    def loads():
        return [pltpu.make_async_copy(src, dst, ld_sems.at[s])
                for k in range(4) for s, src, dst in pieces(2 * k + c, own_ref, k)]

    def stage1():
        return [pltpu.make_async_remote_copy(
            src_ref=src, dst_ref=dst, send_sem=s1_send.at[s], recv_sem=s1_recv.at[s],
            device_id=(x, y, 1 - c), device_id_type=MESH)
            for k in range(4) for s, src, dst in pieces(2 * k + (1 - c), recv1_ref, k)]

    def stage2():
        return [pltpu.make_async_remote_copy(
            src_ref=sendb_ref.at[r], dst_ref=recv2_ref.at[r], send_sem=s2_send.at[r],
            recv_sem=s2_recv.at[r], device_id=(cx, cy, c), device_id_type=MESH) for r, (cx, cy) in enumerate(chips)]

    def start():
        if filled < rr:
            own_ref[:, filled:rr, :] = jnp.zeros((4, rr - filled, ww), F32)
            recv1_ref[:, filled:rr, :] = jnp.zeros((4, rr - filled, ww), F32)
        for cp in loads() + stage1():
            cp.start()

    def forward():
        for cp in loads():
            cp.wait()
        for cp in stage1():
            cp.wait_recv()
        sends = stage2()
        for r, (cx, cy) in enumerate(chips):
            kk = 2 * cx + cy

            def pack(i, carry, r=r, kk=kk):
                rows = pl.ds(pl.multiple_of(i * chunk, chunk), chunk)
                sendb_ref[r, rows, :] = (own_ref[kk, rows, :] + recv1_ref[kk, rows, :]).astype(BF16)
                return carry

            lax.fori_loop(0, rr // chunk, pack, 0)
            sends[r].start()

    def finish():
        for cp in stage2():
            cp.wait_recv()
        mine = 2 * x + y

        def total(i, carry):
            rows = pl.ds(pl.multiple_of(i * chunk, chunk), chunk)
            acc = own_ref[mine, rows, :] + recv1_ref[mine, rows, :]
            for r in range(3):
                acc = acc + recv2_ref[r, rows, :].astype(F32)
            out_ref[rows, :] = acc
            return carry

        lax.fori_loop(0, rr // chunk, total, 0)
        for cp in stage1() + stage2():
            cp.wait_send()

    return start, forward, finish


def _fwd_in(x2, w1, gq, gkv, wq, wkv, pos, rope_rows, shard, tm):
    tt = x2.shape[0]
    steps = tt // tm
    gathered_shape = (N_DEV * shard.shape[0], shard.shape[1])

    def body(x_ref, w1_ref, gq_ref, gkv_ref, wq_ref, wkv_ref, pos_ref, rr_ref, shard_ref,
             xq_ref, xkv_ref, ga_ref, u_ref, gb_ref, qn_ref, kvn_ref, qcat_ref, kcat_ref, v_ref,
             c_ref, sa_ref, sb_ref, all_ref, gathered, *sems):
        step = pl.program_id(0)
        start, forward, finish = _ag_phases(shard_ref, gathered, *sems)
        pl.when(step == 0)(start)
        pl.when(step == min(2, steps - 1))(forward)

        @pl.when(step == steps - 1)
        def _():
            finish()
            all_ref[...] = gathered[...]

        ang = pos_ref[...].astype(F32) * rr_ref[0:1, :]
        cos, sin = jnp.cos(ang), jnp.sin(ang)
        c, sa, sb = cos * rr_ref[1:2, :], sin * rr_ref[2:3, :], sin * rr_ref[3:4, :]
        c_ref[...] = c
        sa_ref[...] = sa
        sb_ref[...] = sb
        h = _dot_nt(x_ref[...].astype(BF16), w1_ref[...])
        xq = h[:, C_XQ:C_XKV]
        xkv = h[:, C_XKV:C_KR]
        xq_ref[...] = xq
        xkv_ref[...] = xkv
        ga_ref[...] = h[:, C_GA:C_U]
        u_ref[...] = h[:, C_U:C_GB]
        gb_ref[...] = h[:, C_GB:IN_WP]
        qn =(xq * lax.rsqrt(jnp.mean(xq * xq, axis=-1, keepdims=True) + RMS_EPS) * gq_ref[...]).astype(BF16)
        kvn = (xkv * lax.rsqrt(jnp.mean(xkv * xkv, axis=-1, keepdims=True) + RMS_EPS) * gkv_ref[...]).astype(BF16)
        qn_ref[...] = qn
        kvn_ref[...] = kvn
        q = _dot_nt(qn, wq_ref[...])
        kv = _dot_nt(kvn, wkv_ref[...])
        kr = _rope_fwd(h[:, C_KR:C_GA], c, sa, sb).astype(BF16)
        for hd in range(HEADS):
            lo = hd * HEAD_PAD
            qcat_ref[:, lo:lo + NOPE] = q[:, lo:lo + NOPE].astype(BF16)
            qcat_ref[:, lo + NOPE:lo + HEAD_PAD] = _rope_fwd(q[:, lo + NOPE:lo + HEAD_PAD], c, sa, sb).astype(BF16)
            kcat_ref[:, lo:lo + NOPE] = kv[:, hd * NOPE:(hd + 1) * NOPE].astype(BF16)
            kcat_ref[:, lo + NOPE:lo + HEAD_PAD] = kr
        v_ref[...] = kv[:, MLA_W:].astype(BF16)

    def tile(w):
        return pl.BlockSpec((tm, w), lambda i: (i, 0))

    def whole(a):
        return pl.BlockSpec(a.shape, lambda i: (0, 0))

    outs = [(Q_LORA, F32), (KV_LORA, F32), (MLA_W, F32), (POOL_W, F32), (POOL_W, F32),
            (Q_LORA, BF16), (KV_LORA, BF16), (HEADS * HEAD_PAD, BF16), (HEADS * HEAD_PAD, BF16), (MLA_W, BF16),
            (128, F32), (128, F32), (128, F32)]
    return pl.pallas_call(
        body, name="fwd_in", grid=(steps,),
        in_specs=[tile(D_MODEL), whole(w1), whole(gq), whole(gkv), whole(wq), whole(wkv), tile(1), whole(rope_rows),
                  pl.BlockSpec(memory_space=pl.ANY)],
        out_specs=[tile(w) for w, _ in outs] + [pl.BlockSpec(gathered_shape, lambda i: (0, 0))],
        out_shape=[jax.ShapeDtypeStruct((tt, w), dt) for w, dt in outs]
        + [jax.ShapeDtypeStruct(gathered_shape, shard.dtype)],
        scratch_shapes=[pltpu.VMEM(gathered_shape, shard.dtype)] + AG_SEMS,
        compiler_params=_params(56, ("arbitrary",)))(x2, w1, gq, gkv, wq, wkv, pos, rope_rows, shard)


def _tile_cases(pmax_ref, pmin_ref, b, nhalf, qi, ki):
    q0, q1 = b * nhalf + 2 * qi, b * nhalf + 2 * qi + 1
    k0, k1 = b * nhalf + 2 * ki, b * nhalf + 2 * ki + 1
    needed = jnp.maximum(pmax_ref[q0], pmax_ref[q1]) >= jnp.minimum(pmin_ref[k0], pmin_ref[k1])
    visible = jnp.minimum(pmin_ref[q0], pmin_ref[q1]) >= jnp.maximum(pmax_ref[k0], pmax_ref[k1])
    stepped = pmax_ref[q0] < pmin_ref[k1]
    return needed, visible, stepped


def _attn_fwd(qcat, kcat, v, pos_col, pos_blk, pmax, pmin, tq, tk):
    bb, ss, _ = qcat.shape
    nq, nk = ss // tq, ss // tk
    lanes = 128
    hps = ATTN_FWD_HEADS_PER_STEP

    def body(pmax_ref, pmin_ref, q_ref, k_ref, v_ref, pc_ref, pb_ref, o_ref, lse_ref, m_s, acc_s):
        b, qi = pl.program_id(0), pl.program_id(2)
        m_s[...] = jnp.full(m_s.shape, NEG, F32)
        acc_s[...] = jnp.zeros(acc_s.shape, F32)

        def part(ki, masked, q_lo, q_n, k_n):
            qrows = slice(q_lo, q_lo + q_n)
            krows = pl.ds(pl.multiple_of(ki * tk, tk), k_n)
            if masked:
                mask = pc_ref[qrows, :] >= pb_ref[ki][:, :k_n]
            ones = jnp.ones((k_n, lanes), BF16)
            for hd in range(hps):
                qk = slice(hd * HEAD_PAD, (hd + 1) * HEAD_PAD)
                s = _dot_nt(q_ref[qrows, qk], k_ref[krows, qk]) * SCALE_LOG2
                if masked:
                    s = jnp.where(mask, s, NEG)
                m_prev = m_s[hd, qrows, :]
                m_new = jnp.maximum(m_prev, jnp.max(s, axis=-1, keepdims=True))
                p = jnp.exp2(s - jnp.tile(m_new, (1, k_n // lanes)))
                a = jnp.exp2(m_prev - m_new)
                vv = jnp.concatenate([v_ref[krows, hd * V_DIM:(hd + 1) * V_DIM], ones], axis=1)
                acc_s[hd, qrows, :] = jnp.tile(a, (1, 2)) * acc_s[hd, qrows, :] + _dot(p.astype(BF16), vv)
                m_s[hd, qrows, :] = m_new

        def step(ki, carry):
            needed, visible, stepped = _tile_cases(pmax_ref, pmin_ref, b, 2 * nq, qi, ki)

            @pl.when(needed & visible)
            def _():
                part(ki, False, 0, tq, tk)

            @pl.when(needed & ~visible & stepped)
            def _():
                part(ki, True, 0, tq // 2, tk // 2)
                part(ki, True, tq // 2, tq // 2, tk)

            @pl.when(needed & ~visible & ~stepped)
            def _():
                part(ki, True, 0, tq, tk)

            return carry

        lax.fori_loop(0, nk, step, 0)
        for hd in range(hps):
            acc = acc_s[hd]
            l = acc[:, V_DIM:]
            o_ref[:, hd * V_DIM:(hd + 1) * V_DIM] = acc[:, :V_DIM] / l
            lse_ref[hd] = (m_s[hd] + jnp.log2(l)).T[0:1, :]

    grid_spec = pltpu.PrefetchScalarGridSpec(
        num_scalar_prefetch=2, grid=(bb, HEADS // hps, nq),
        in_specs=[
            pl.BlockSpec((None, tq, hps * HEAD_PAD), lambda b, h, i, *_: (b, i, h)),
            pl.BlockSpec((None, ss, hps * HEAD_PAD), lambda b, h, i, *_: (b, 0, h)),
            pl.BlockSpec((None, ss, hps * V_DIM), lambda b, h, i, *_: (b, 0, h)),
            pl.BlockSpec((None, tq, 1), lambda b, h, i, *_: (b, i, 0)),
            pl.BlockSpec((None, nk, 1, tk), lambda b, h, i, *_: (b, 0, 0, 0)),
        ],
        out_specs=[
            pl.BlockSpec((None, tq, hps * V_DIM), lambda b, h, i, *_: (b, i, h)),
            pl.BlockSpec((None, hps, 1, tq), lambda b, h, i, *_: (b, h, 0, i)),
        ],
        scratch_shapes=[pltpu.VMEM((hps, tq, lanes), F32), pltpu.VMEM((hps, tq, 2 * V_DIM), F32)])
    return pl.pallas_call(
        body, name="attn_fwd", grid_spec=grid_spec,
        out_shape=[jax.ShapeDtypeStruct((bb, ss, MLA_W), F32), jax.ShapeDtypeStruct((bb, HEADS, 1, ss), F32)],
        compiler_params=_params(48, ("arbitrary", "arbitrary", "arbitrary")))(pmax, pmin, qcat, kcat, v, pos_col, pos_blk)


def _post(x, tgt, o, ga, u, gb, w_out, pool_wb, pool_scale, ln_g, ln_b, ts):
    bb, ss, _ = x.shape
    nt = ss // ts
    hb = ts // POOL_HALO

    def body(x_ref, tgt_ref, o_ref, ga_ref, u_ref, uh_ref, gb_ref, wo_ref, pw_ref, ps_ref, lg_ref, lb_ref,
             dz_ref, ycat_ref, do_ref, dga_ref, dgb_ref, dpc_ref, dl_ref, loss_ref, glg_ref, glb_ref, gps_ref, gpw_ref):
        step = pl.program_id(0)
        j = step % nt

        @pl.when(step == 0)
        def _():
            for r in (loss_ref, glg_ref, glb_ref, gps_ref, gpw_ref):
                r[...] = jnp.zeros(r.shape, F32)

        o, ga, u, gb = o_ref[...], ga_ref[...], u_ref[...], gb_ref[...]
        sa, dsa = _silu_parts(ga)
        sb, dsb = _silu_parts(gb)
        ext = jnp.concatenate([jnp.where(j > 0, uh_ref[...], 0.0), u], axis=0)
        s2 = ext + pltpu.roll(ext, 1, 0)
        s4 = s2 + pltpu.roll(s2, 2, 0)
        s8 = s4 + pltpu.roll(s4, 4, 0)
        s16 = s8 + pltpu.roll(s8, 8, 0)
        cnt = _pool_cnt(j * ts, ts)
        pooled = (_pick_groups(s2, s4, s8, s16, POOL_HALO, POOL_HALO + ts) / cnt - u).astype(BF16)
        mixed = jnp.concatenate(
            [_dot(pooled[:, g * POOL_C:(g + 1) * POOL_C], pw_ref[g]) for g in range(POOL_G)], axis=1)
        ps = ps_ref[...]
        scaled = mixed * ps
        ycat = jnp.concatenate([o * sa, scaled * sb], axis=1).astype(BF16)
        ycat_ref[...] = ycat
        z = ALPHA * x_ref[...] + _dot(ycat, wo_ref[...])
        mu = jnp.mean(z, axis=-1, keepdims=True)
        zc = z - mu
        rstd = lax.rsqrt(jnp.mean(zc * zc, axis=-1, keepdims=True) + LN_EPS)
        xhat = zc * rstd
        lg = lg_ref[...]
        diff = xhat * lg + lb_ref[...] - tgt_ref[...]
        loss_ref[...] += jnp.sum(diff * diff) * (0.5 / D_MODEL)
        dy = diff * (1.0 / D_MODEL)
        glb_ref[...] += jnp.sum(dy, axis=0, keepdims=True)
        glg_ref[...] += jnp.sum(dy * xhat, axis=0, keepdims=True)
        dxh = dy * lg
        dz = rstd * (dxh - jnp.mean(dxh, axis=-1, keepdims=True) - xhat * jnp.mean(dxh * xhat, axis=-1, keepdims=True))
        dz_ref[...] = dz
        dycat = _dot_nt(dz.astype(BF16), wo_ref[...])
        dya, dyb = dycat[:, :MLA_W], dycat[:, MLA_W:]
        do = dya * sa
        do_ref[...] = do.astype(BF16)
        doo = do * o
        for hd in range(HEADS):
            dl_ref[hd] = jnp.sum(doo[:, hd * V_DIM:(hd + 1) * V_DIM].T, axis=0, keepdims=True)
        dga_ref[...] = (dya * o * dsa).astype(BF16)
        dgb_ref[...] = (dyb * scaled * dsb).astype(BF16)
        dscaled = dyb * sb
        gps_ref[...] += jnp.sum(dscaled * mixed, axis=0, keepdims=True)
        dmixed = (dscaled * ps).astype(BF16)
        dpooled = []
        for g in range(POOL_G):
            cols = slice(g * POOL_C, (g + 1) * POOL_C)
            gpw_ref[g] += _dot_tn(pooled[:, cols], dmixed[:, cols])
            dpooled.append(_dot_nt(dmixed[:, cols], pw_ref[g]))
        dpc_ref[...] = jnp.concatenate(dpooled, axis=1) / cnt

    def tile(w):
        return pl.BlockSpec((None, ts, w), lambda i: (i // nt, i % nt, 0))

    def whole(a):
        nd = a.ndim
        return pl.BlockSpec(a.shape, lambda i: (0,) * nd)

    halo = pl.BlockSpec((None, POOL_HALO, POOL_W), lambda i: (i // nt, jnp.maximum((i % nt) * hb - 1, 0), 0))
    acc_shapes = [(8, 128), (1, D_MODEL), (1, D_MODEL), (1, POOL_W), (POOL_G, POOL_C, POOL_C)]
    tile_outs = [(D_MODEL, F32), (D_MODEL, BF16), (MLA_W, BF16), (MLA_W, BF16), (POOL_W, BF16), (POOL_W, F32)]
    return pl.pallas_call(
        body, name="post", grid=(bb * nt,),
        in_specs=[tile(D_MODEL), tile(D_MODEL), tile(MLA_W), tile(MLA_W), tile(POOL_W), halo, tile(POOL_W),
                  whole(w_out), whole(pool_wb), whole(pool_scale), whole(ln_g), whole(ln_b)],
        out_specs=[tile(w) for w, _ in tile_outs]
        + [pl.BlockSpec((None, HEADS, 1, ts), lambda i: (i // nt, 0, 0, i % nt))]
        + [pl.BlockSpec(s, lambda i, n=len(s): (0,) * n) for s in acc_shapes],
        out_shape=[jax.ShapeDtypeStruct((bb, ss, w), dt) for w, dt in tile_outs]
        + [jax.ShapeDtypeStruct((bb, HEADS, 1, ss), F32)]
        + [jax.ShapeDtypeStruct(s, F32) for s in acc_shapes],
        compiler_params=_params(48, ("arbitrary",)))(x, tgt, o, ga, u, u, gb, w_out, pool_wb, pool_scale, ln_g, ln_b)


def _attn_bwd(qcat, kcat, v, do, lse, dl, rc, rsa, rsb, pos_col, pos_blk, pmax, pmin, tq, tk):
    bb, ss, _ = qcat.shape
    nq, nk = ss // tq, ss // tk
    hps = ATTN_BWD_HEADS_PER_STEP

    def body(pmax_ref, pmin_ref, q_ref, k_ref, v_ref, do_ref, lse_ref, dl_ref, c_ref, sa_ref, sb_ref, pc_ref, pb_ref,
             dqp_ref, dkn_ref, dv_ref, dkr_ref, dq_s, dk_s, dv_s):
        b = pl.program_id(0)
        dq_s[...] = jnp.zeros(dq_s.shape, F32)

        def part(qi, ki, masked):
            krows = pl.ds(pl.multiple_of(ki * tk, tk), tk)
            qrows = pl.ds(pl.multiple_of(qi * tq, tq), tq)
            if masked:
                mask = pb_ref[qi] >= pc_ref[krows, :]
            for hd in range(hps):
                qk = slice(hd * HEAD_PAD, (hd + 1) * HEAD_PAD)
                vs = slice(hd * V_DIM, (hd + 1) * V_DIM)
                q = q_ref[qrows, qk]
                dd = do_ref[qrows, vs]
                st = _dot_nt(k_ref[krows, qk], q) * SCALE_LOG2
                if masked:
                    st = jnp.where(mask, st, NEG)
                pt = jnp.exp2(st - lse_ref[hd, qi])
                dv_s[:, vs] += _dot(pt.astype(BF16), dd)
                dpt = _dot_nt(v_ref[krows, vs], dd)
                dst = (pt * (dpt - dl_ref[hd, qi]) * SCALE).astype(BF16)
                dk_s[:, qk] += _dot(dst, q)
                dq_s[qrows, qk] += _dot_tn(dst, k_ref[krows, qk])

        def kv_step(ki, carry):
            krows = pl.ds(pl.multiple_of(ki * tk, tk), tk)
            dk_s[...] = jnp.zeros(dk_s.shape, F32)
            dv_s[...] = jnp.zeros(dv_s.shape, F32)

            def q_step(qi, c2):
                needed, visible, _ = _tile_cases(pmax_ref, pmin_ref, b, 2 * nq, qi, ki)

                @pl.when(needed & visible)
                def _():
                    part(qi, ki, False)

                @pl.when(needed & ~visible)
                def _():
                    part(qi, ki, True)

                return c2

            lax.fori_loop(0, nq, q_step, 0)
            dkr = jnp.zeros((tk, HEAD_PAD - NOPE), F32)
            for hd in range(hps):
                lo = hd * HEAD_PAD
                dkn_ref[krows, hd * NOPE:(hd + 1) * NOPE] = dk_s[:, lo:lo + NOPE].astype(BF16)
                dkr = dkr + dk_s[:, lo + NOPE:lo + HEAD_PAD]
            dkr_ref[krows, :] = dkr
            dv_ref[krows, :] = dv_s[...].astype(BF16)
            return carry

        lax.fori_loop(0, nk, kv_step, 0)
        c, sa, sb = c_ref[...], sa_ref[...], sb_ref[...]
        for hd in range(hps):
            lo = hd * HEAD_PAD
            dqp_ref[:, lo:lo + NOPE] = dq_s[:, lo:lo + NOPE].astype(BF16)
            dqp_ref[:, lo + NOPE:lo + HEAD_PAD] = _rope_bwd(dq_s[:, lo + NOPE:lo + HEAD_PAD], c, sa, sb).astype(BF16)

    def per_head(w):
        return pl.BlockSpec((None, ss, hps * w), lambda b, h, *_: (b, 0, h))

    def rows_of_head():
        return pl.BlockSpec((None, hps, nq, 1, tq), lambda b, h, *_: (b, h, 0, 0, 0))

    def per_batch(w):
        return pl.BlockSpec((None, ss, w), lambda b, h, *_: (b, 0, 0))

    grid_spec = pltpu.PrefetchScalarGridSpec(
        num_scalar_prefetch=2, grid=(bb, HEADS // hps),
        in_specs=[per_head(HEAD_PAD), per_head(HEAD_PAD), per_head(V_DIM), per_head(V_DIM),
                  rows_of_head(), rows_of_head(), per_batch(128), per_batch(128), per_batch(128), per_batch(1),
                  pl.BlockSpec((None, nq, 1, tq), lambda b, h, *_: (b, 0, 0, 0))],
        out_specs=[per_head(HEAD_PAD), per_head(NOPE), per_head(V_DIM),
                   pl.BlockSpec((None, None, ss, HEAD_PAD - NOPE), lambda b, h, *_: (b, h, 0, 0))],
        scratch_shapes=[pltpu.VMEM((ss, hps * HEAD_PAD), F32), pltpu.VMEM((tk, hps * HEAD_PAD), F32),
                        pltpu.VMEM((tk, hps * V_DIM), F32)])
    return pl.pallas_call(
        body, name="attn_bwd", grid_spec=grid_spec,
        out_shape=[jax.ShapeDtypeStruct((bb, ss, HEADS * HEAD_PAD), BF16),
                   jax.ShapeDtypeStruct((bb, ss, MLA_W), BF16),
                   jax.ShapeDtypeStruct((bb, ss, MLA_W), BF16),
                   jax.ShapeDtypeStruct((bb, HEADS // hps, ss, HEAD_PAD - NOPE), F32)],
        compiler_params=_params(56, ("arbitrary", "arbitrary")))(
            pmax, pmin, qcat, kcat, v, do, lse, dl, rc, rsa, rsb, pos_col, pos_blk)


def _bwd_mid(dqp, dkn, dv, dkr, xq, xkv, rc, rsa, rsb, wq, wkv, gq, gkv, dga, dgb, dpc, dz, w1, ts):
    bb, ss, _ = dz.shape
    nt = ss // ts
    hb = ts // POOL_HALO
    groups = dkr.shape[1]

    def body(dqp_ref, dkn_ref, dv_ref, dkr_ref, xq_ref, xkv_ref, c_ref, sa_ref, sb_ref, wq_ref, wkv_ref, gq_ref,
             gkv_ref, dga_ref, dgb_ref, dpc_ref, dph_ref, dz_ref, w1_ref,
             gx_ref, dh_ref, ggq_ref, ggkv_ref):
        step = pl.program_id(0)
        j = step % nt

        @pl.when(step == 0)
        def _():
            ggq_ref[...] = jnp.zeros(ggq_ref.shape, F32)
            ggkv_ref[...] = jnp.zeros(ggkv_ref.shape, F32)

        dkr = dkr_ref[0]
        for g in range(1, groups):
            dkr = dkr + dkr_ref[g]
        dh_ref[:, C_KR:C_GA] = _rope_bwd(dkr, c_ref[...], sa_ref[...], sb_ref[...]).astype(BF16)

        def rms_bwd(x, g, dn):
            inv = lax.rsqrt(jnp.mean(x * x, axis=-1, keepdims=True) + RMS_EPS)
            xh = x * inv
            dxh = dn * g
            return inv * (dxh - xh * jnp.mean(dxh * xh, axis=-1, keepdims=True)), jnp.sum(dn * xh, axis=0, keepdims=True)

        dxq, ggq = rms_bwd(xq_ref[...], gq_ref[...], _dot(dqp_ref[...], wq_ref[...]))
        dkvn = _dot(dkn_ref[...], wkv_ref[:MLA_W, :]) + _dot(dv_ref[...], wkv_ref[MLA_W:, :])
        dxkv, ggkv = rms_bwd(xkv_ref[...], gkv_ref[...], dkvn)
        ggq_ref[...] += ggq
        ggkv_ref[...] += ggkv
        dh_ref[:, C_XQ:C_XKV] = dxq.astype(BF16)
        dh_ref[:, C_XKV:C_KR] = dxkv.astype(BF16)
        dh_ref[:, C_GA:C_U] = dga_ref[...]
        dh_ref[:, C_GB:IN_WP] = dgb_ref[...]
        dpc = dpc_ref[...]
        n = ts + POOL_HALO
        ext = jnp.concatenate([dpc, jnp.where(j < nt - 1, dph_ref[...], 0.0)], axis=0)
        r2 = ext + pltpu.roll(ext, n - 1, 0)
        r4 = r2 + pltpu.roll(r2, n - 2, 0)
        r8 = r4 + pltpu.roll(r4, n - 4, 0)
        r16 = r8 + pltpu.roll(r8, n - 8, 0)
        du = _pick_groups(r2, r4, r8, r16, 0, ts) - dpc * _pool_cnt(j * ts, ts)
        dh_ref[:, C_U:C_GB] = du.astype(BF16)
        gx_ref[...] = ALPHA * dz_ref[...] + _dot(dh_ref[...], w1_ref[...])

    def tile(w):
        return pl.BlockSpec((None, ts, w), lambda i: (i // nt, i % nt, 0))

    def whole(a):
        return pl.BlockSpec(a.shape, lambda i: (0, 0))

    halo = pl.BlockSpec((None, POOL_HALO, POOL_W),
                        lambda i: (i // nt, jnp.minimum((i % nt + 1) * hb, ss // POOL_HALO - 1), 0))
    return pl.pallas_call(
        body, name="bwd_mid", grid=(bb * nt,),
        in_specs=[tile(HEADS * HEAD_PAD), tile(MLA_W), tile(MLA_W),
                  pl.BlockSpec((None, groups, ts, HEAD_PAD - NOPE), lambda i: (i // nt, 0, i % nt, 0)),
                  tile(Q_LORA), tile(KV_LORA),
                  tile(128), tile(128), tile(128), whole(wq), whole(wkv), whole(gq), whole(gkv),
                  tile(MLA_W), tile(POOL_W), tile(POOL_W), halo, tile(D_MODEL), whole(w1)],
        out_specs=[tile(D_MODEL), tile(IN_WP),
                   pl.BlockSpec((1, Q_LORA), lambda i: (0, 0)), pl.BlockSpec((1, KV_LORA), lambda i: (0, 0))],
        out_shape=[jax.ShapeDtypeStruct((bb, ss, D_MODEL), F32), jax.ShapeDtypeStruct((bb, ss, IN_WP), BF16),
                   jax.ShapeDtypeStruct((1, Q_LORA), F32), jax.ShapeDtypeStruct((1, KV_LORA), F32)],
        compiler_params=_params(48, ("arbitrary",)))(
            dqp, dkn, dv, dkr, xq, xkv, rc, rsa, rsb, wq, wkv, gq, gkv, dga, dgb, dpc, dpc, dz, w1)


def _grad_w(pairs, bt, name, b_cols=None, carried=None, reduced=None):
    tt = pairs[0][0].shape[0]
    steps = tt // bt
    npairs = len(pairs)
    n_rs = len(_rs_scratch(carried)) if carried is not None else 0

    def body(*refs):
        ab, rest = refs[:2 * npairs], list(refs[2 * npairs:])
        g_hbm = rest.pop(0) if carried is not None else None
        part = rest.pop(0) if reduced is not None else None
        outs = [rest.pop(0) for _ in range(npairs)]
        phases = []
        if carried is not None:
            rs_out = rest.pop(0)
        if reduced is not None:
            sum_ref = rest.pop(0)
        if carried is not None:
            phases.append(_rs_phases(carried, g_hbm, rs_out, *rest[:n_rs]))
        if reduced is not None:
            gathered, *sems = rest[n_rs:]
            phases.append(_ag_phases(part, gathered, *sems, sum_ref=sum_ref))
        step = pl.program_id(0)

        @pl.when(step == 0)
        def _():
            for o in outs:
                o[...] = jnp.zeros(o.shape, F32)
            for start, _, _ in phases:
                start()

        for i in range(npairs):
            outs[i][...] += _dot_tn(ab[2 * i][...].astype(BF16), ab[2 * i + 1][...].astype(BF16))

        @pl.when(step == min(2, steps - 1))
        def _():
            for _, forward, _ in phases:
                forward()

        @pl.when(step == steps - 1)
        def _():
            for _, _, finish in phases:
                finish()

    in_specs, out_specs, out_shape = [], [], []
    for a, b in pairs:
        m, (n, col) = a.shape[1], b_cols or (b.shape[1], 0)
        in_specs += [pl.BlockSpec((bt, m), lambda i: (i, 0)), pl.BlockSpec((bt, n), lambda i, col=col: (i, col))]
        out_specs.append(pl.BlockSpec((m, n), lambda i: (0, 0)))
        out_shape.append(jax.ShapeDtypeStruct((m, n), F32))
    args = [t for p in pairs for t in p]
    scratch = []
    if carried is not None:
        in_specs.append(pl.BlockSpec(memory_space=pl.ANY))
        out_specs.append(pl.BlockSpec((carried.rows, carried.width), lambda i: (0, 0)))
        out_shape.append(jax.ShapeDtypeStruct((carried.rows, carried.width), F32))
        args.append(carried.array)
        scratch += _rs_scratch(carried)
    if reduced is not None:
        in_specs.append(pl.BlockSpec(memory_space=pl.ANY))
        out_specs.append(pl.BlockSpec(reduced.shape, lambda i: (0, 0)))
        out_shape.append(jax.ShapeDtypeStruct(reduced.shape, F32))
        args.append(reduced)
        scratch += [pltpu.VMEM((N_DEV * reduced.shape[0], reduced.shape[1]), F32)] + AG_SEMS
    return pl.pallas_call(
        body, name=name, grid=(steps,), in_specs=in_specs, out_specs=out_specs, out_shape=out_shape,
        scratch_shapes=scratch, compiler_params=_params(56, ("arbitrary",)))(*args)


def _adamw(triples):
    n = len(triples)

    def body(*refs):
        ins, outs = refs[:4 * n], refs[4 * n:]
        for i in range(n):
            w, g, m, v = (r[...] for r in ins[4 * i:4 * i + 4])
            m = ADAM_B1 * m + (1.0 - ADAM_B1) * g
            v = ADAM_B2 * v + (1.0 - ADAM_B2) * jnp.square(g)
            m_hat = m / (1.0 - ADAM_B1 ** ADAM_STEP)
            v_hat = v / (1.0 - ADAM_B2 ** ADAM_STEP)
            outs[3 * i][...] = -ADAM_LR * (m_hat / (jnp.sqrt(v_hat) + ADAM_EPS) + ADAM_WD * w)
            outs[3 * i + 1][...] = m
            outs[3 * i + 2][...] = v

    flat = [a for t in triples for a in t]
    vmem = pl.BlockSpec(memory_space=pltpu.VMEM)
    res = pl.pallas_call(
        body, name="adamw", in_specs=[vmem] * len(flat), out_specs=[vmem] * (3 * n),
        out_shape=[jax.ShapeDtypeStruct(t[0].shape, F32) for t in triples for _ in range(3)],
        compiler_params=_params(48))(*flat)
    return [tuple(res[3 * i:3 * i + 3]) for i in range(n)]


def _shard_slab(w_in, w_uq, w_ukv):
    mixed = jnp.concatenate(
        [_pad_rows(w_uq.T, R_MIX), w_ukv.T, jnp.zeros((R_MIX, 1024 - Q_LORA - KV_LORA), F32)], axis=1)
    return jnp.concatenate([mixed, w_in.T, jnp.zeros((SLAB_ROWS - O_IN - R_IN, 1024), F32)], axis=0)


def _unpack_weights(slabs):
    uq = slabs[:, O_MIX:O_MIX + R_UQ, :Q_LORA].reshape(HEADS, QK_DIM, Q_LORA)
    wqt = jnp.pad(uq, ((0, 0), (0, HEAD_PAD - QK_DIM), (0, 0))).reshape(HEADS * HEAD_PAD, Q_LORA)
    ukv = slabs[:, O_MIX:O_MIX + R_MIX, Q_LORA:Q_LORA + KV_LORA].reshape(HEADS, 2, NOPE, KV_LORA)
    wkvt = ukv.transpose(1, 0, 2, 3).reshape(2 * MLA_W, KV_LORA)
    raw = slabs[:, O_IN:O_IN + R_IN].reshape(IN_W, D_MODEL)
    w1t = jnp.concatenate([raw[:768 + ROPE], jnp.zeros((128 - ROPE, D_MODEL), BF16), raw[768 + ROPE:]], axis=0)
    return w1t, wqt, wkvt


def _mixed_band(g_wqt, g_wkvt):
    uq = g_wqt.reshape(HEADS, HEAD_PAD, Q_LORA)[:, :QK_DIM].reshape(N_DEV, R_UQ, Q_LORA)
    uq = jnp.pad(uq, ((0, 0), (0, R_MIX - R_UQ), (0, 0)))
    ukv = g_wkvt.reshape(2, HEADS, NOPE, KV_LORA).transpose(1, 0, 2, 3).reshape(N_DEV, R_MIX, KV_LORA)
    return jnp.concatenate([uq, ukv, jnp.zeros((N_DEV, R_MIX, 1024 - Q_LORA - KV_LORA), F32)], axis=2)


def _rope_rows():
    half = ROPE // 2
    inv_freq = ROPE_THETA ** (-jnp.arange(half, dtype=F32) / half)
    zero, one = jnp.zeros((half,), F32), jnp.ones((half,), F32)
    rows = [jnp.concatenate(r) for r in (
        (inv_freq, inv_freq, zero, zero), (one, one, zero, zero), (-one, zero, zero, zero), (zero, one, zero, zero))]
    return jnp.stack(rows + [jnp.zeros((128,), F32)] * 4)


def _pad_rows(a, rows):
    return jnp.pad(a, ((0, rows - a.shape[0]), (0, 0)))


def kernel(x, positions, w_in, q_norm_g, w_uq, kv_norm_g, w_ukv, pool_w, pool_scale, w_out, ln_g, ln_b, loss_target, m_w_in, m_q_norm_g, m_w_uq, m_kv_norm_g, m_w_ukv, m_pool_w, m_pool_scale, m_w_out, m_ln_g, m_ln_b, v_w_in, v_q_norm_g, v_w_uq, v_kv_norm_g, v_w_ukv, v_pool_w, v_pool_scale, v_w_out, v_ln_g, v_ln_b):
    bb, ss, _ = x.shape
    tt = bb * ss
    atile = min(512, ss)
    nblk = ss // atile

    slab = _shard_slab(w_in, w_uq, w_ukv).astype(BF16)
    slabs = _all_gather(slab, "gather_weights").reshape(N_DEV, SLAB_ROWS, 1024)
    w1, wq, wkv = _unpack_weights(slabs)

    gq, gkv = q_norm_g.reshape(1, Q_LORA), kv_norm_g.reshape(1, KV_LORA)
    ps = pool_scale.reshape(1, POOL_W)
    pos_col = positions.reshape(bb, ss, 1)
    pos_blk = positions.reshape(bb, nblk, 1, atile)
    pmax = jnp.max(positions.reshape(bb, 2 * nblk, atile // 2), axis=-1).reshape(-1)
    pmin = jnp.min(positions.reshape(bb, 2 * nblk, atile // 2), axis=-1).reshape(-1)

    x2 = x.reshape(tt, D_MODEL)
    xq, xkv, ga, u, gb, qn, kvn, qcat, kcat, v, rc, rsa, rsb, wo = _fwd_in(
        x2, w1, gq, gkv, wq, wkv, pos_col.reshape(tt, 1), _rope_rows(), w_out.astype(BF16), atile)
    as3 = lambda a: a.reshape(bb, ss, a.shape[-1])
    qcat, kcat, v = as3(qcat), as3(kcat), as3(v)
    o, lse = _attn_fwd(qcat, kcat, v, pos_col, pos_blk, pmax, pmin, atile, atile)
    dz, ycat, do, dga, dgb, dpc, dl, loss_p, g_lng, g_lnb, g_ps, g_pw = _post(
        x, loss_target, o, as3(ga), as3(u), as3(gb), wo, pool_w.astype(BF16), ps, ln_g, ln_b, atile)

    bt = min(1024, tt)
    g_wo, = _grad_w([(ycat.reshape(tt, D_MODEL), dz.reshape(tt, D_MODEL))], bt, "grad_w_out")
    rows5 = lambda a: a.reshape(bb, HEADS, nblk, 1, atile)
    rc, rsa, rsb = as3(rc), as3(rsa), as3(rsb)
    dqp, dkn, dv, dkr = _attn_bwd(
        qcat, kcat, v, do, rows5(lse), rows5(dl), rc, rsa, rsb, pos_col, pos_blk, pmax, pmin, atile, atile)
    g_wq, g_wkn, g_wv, rs_out = _grad_w(
        [(dqp.reshape(tt, HEADS * HEAD_PAD), qn), (dkn.reshape(tt, MLA_W), kvn), (dv.reshape(tt, MLA_W), kvn)], bt,
        "grad_w_uqkv", carried=_blocks(g_wo.reshape(N_DEV, R_OUT, D_MODEL)))
    g_wkv = jnp.concatenate([g_wkn, g_wv], axis=0)
    grad_x, dh, g_gq, g_gkv = _bwd_mid(
        dqp, dkn, dv, dkr, as3(xq), as3(xkv), rc, rsa, rsb, wq, wkv, gq, gkv, dga, dgb, dpc, dz, w1, atile)
    small = jnp.concatenate(
        [_pad_rows(g_lng.reshape(8, 128), 8), _pad_rows(g_lnb.reshape(8, 128), 8), _pad_rows(g_gq.reshape(4, 128), 8),
         _pad_rows(g_gkv.reshape(2, 128), 8), _pad_rows(g_ps.reshape(4, 128), 8), g_pw.reshape(POOL_G * POOL_C, 128),
         loss_p], axis=0)
    dh2, half = dh.reshape(tt, IN_WP), D_MODEL // 2
    g_w1a, rs_mix, small = _grad_w([(dh2, x2)], bt, "grad_w_in_a", b_cols=(half, 0),
                                   carried=_blocks(_mixed_band(g_wq, g_wkv)), reduced=small)
    g_w1b, rs_in_a = _grad_w([(dh2, x2)], bt, "grad_w_in_b", b_cols=(half, 1), carried=_w_in_blocks(g_w1a))
    rs_in = jnp.concatenate([rs_in_a, _reduce_scatter(_w_in_blocks(g_w1b), "reduce_scatter_w_in")], axis=1)
    loss = small[40 + POOL_G * POOL_C, 0]
    grads = {
        "w_in": rs_in[:R_IN],
        "q_norm_g": small[16:20].reshape(1, Q_LORA),
        "w_uq": rs_mix[:R_UQ, :Q_LORA],
        "kv_norm_g": small[24:26].reshape(1, KV_LORA),
        "w_ukv": rs_mix[:, Q_LORA:Q_LORA + KV_LORA].T,
        "pool_w": small[40:40 + POOL_G * POOL_C],
        "pool_scale": small[32:36].reshape(1, POOL_W),
        "w_out": rs_out,
        "ln_g": small[0:8].reshape(1, D_MODEL),
        "ln_b": small[8:16].reshape(1, D_MODEL),
    }
    transposed = ("w_in", "w_uq")

    names = ["w_in", "q_norm_g", "w_uq", "kv_norm_g", "w_ukv", "pool_w", "pool_scale", "w_out", "ln_g", "ln_b"]
    weights = dict(w_in=w_in, q_norm_g=q_norm_g, w_uq=w_uq, kv_norm_g=kv_norm_g, w_ukv=w_ukv, pool_w=pool_w,
                   pool_scale=pool_scale, w_out=w_out, ln_g=ln_g, ln_b=ln_b)
    moms = dict(w_in=(m_w_in, v_w_in), q_norm_g=(m_q_norm_g, v_q_norm_g), w_uq=(m_w_uq, v_w_uq),
                kv_norm_g=(m_kv_norm_g, v_kv_norm_g), w_ukv=(m_w_ukv, v_w_ukv), pool_w=(m_pool_w, v_pool_w),
                pool_scale=(m_pool_scale, v_pool_scale), w_out=(m_w_out, v_w_out), ln_g=(m_ln_g, v_ln_g),
                ln_b=(m_ln_b, v_ln_b))
    as2 = lambda a, n: a.T if n in transposed else a.reshape(grads[n].shape)
    upd = _adamw([(as2(weights[n], n), grads[n], as2(moms[n][0], n), as2(moms[n][1], n)) for n in names])
    shaped = lambda a, n: a.T if n in transposed else a.reshape(weights[n].shape)
    return (loss, grad_x,
            *[shaped(grads[n], n) for n in names],
            *[shaped(upd[i][0], n) for i, n in enumerate(names)],
            *[shaped(upd[i][1], n) for i, n in enumerate(names)],
            *[shaped(upd[i][2], n) for i, n in enumerate(names)])
```

```python
import functools

import jax
import jax.numpy as jnp
from jax import lax
from jax.experimental import pallas as pl
from jax.experimental.pallas import tpu as pltpu

F32 = jnp.float32
BF16 = jnp.bfloat16
MESH = pl.DeviceIdType.MESH

N_DEV = 8
D_MODEL = 1024
HEADS = 4
NOPE = 128
ROPE = 64
V_DIM = 128
QK_DIM = NOPE + ROPE
HEAD_PAD = 256
Q_LORA = 512
KV_LORA = 256
MLA_W = HEADS * V_DIM
POOL_W = 512
POOL_G = 4
POOL_C = 128
POOL_HALO = 16
IN_W = 2368
IN_WP = 2432
C_XQ, C_XKV, C_KR, C_GA, C_U, C_GB = 0, 512, 768, 896, 1408, 1920
ROPE_THETA = 10000.0
RMS_EPS = 1e-6
LN_EPS = 1e-5
ALPHA = 2.0 ** 0.25
SCALE = QK_DIM ** -0.5
LN2 = 0.6931471805599453
SCALE_LOG2 = SCALE / LN2
NEG = float(jnp.finfo(jnp.float32).min)

ADAM_LR = 0.001
ADAM_B1 = 0.9
ADAM_B2 = 0.999
ADAM_EPS = 1e-08
ADAM_WD = 0.01
ADAM_STEP = 10

R_OUT, R_MIX, R_UQ, R_IN = 128, 128, 96, 296
O_MIX, O_IN = 0, R_MIX
SLAB_ROWS = 432
R_IN_PAD = 304

V7X_VMEM_BYTES = 64 * 1024 * 1024
ATTN_FWD_HEADS_PER_STEP = 4
ATTN_BWD_HEADS_PER_STEP = 2


def _params(vmem_mb, semantics=None):
    assert vmem_mb * 1024 * 1024 < V7X_VMEM_BYTES
    return pltpu.CompilerParams(vmem_limit_bytes=vmem_mb * 1024 * 1024, dimension_semantics=semantics)


def _dot(a, b):
    return jnp.dot(a, b, preferred_element_type=F32)


def _dot_nt(a, b):
    return lax.dot_general(a, b, (((1,), (1,)), ((), ())), preferred_element_type=F32)


def _dot_tn(a, b):
    return lax.dot_general(a, b, (((0,), (0,)), ((), ())), preferred_element_type=F32)


def _rope_fwd(t, c, sa, sb):
    return t * c + pltpu.roll(t, 96, 1) * sa + pltpu.roll(t, 32, 1) * sb


def _rope_bwd(d, c, sa, sb):
    return d * c + pltpu.roll(d * sa, 32, 1) + pltpu.roll(d * sb, 96, 1)


def _silu_parts(g):
    sig = jax.nn.sigmoid(g)
    return g * sig, sig * (1.0 + g * (1.0 - sig))


def _pool_cnt(row0, rows):
    t = row0 + lax.broadcasted_iota(jnp.int32, (rows, POOL_W), 0)
    w = 2 << (lax.broadcasted_iota(jnp.int32, (rows, POOL_W), 1) // POOL_C)
    return jnp.minimum(t + 1, w).astype(F32)


def _pick_groups(s2, s4, s8, s16, lo, hi):
    return jnp.concatenate([s2[lo:hi, 0:128], s4[lo:hi, 128:256], s8[lo:hi, 256:384], s16[lo:hi, 384:512]], axis=1)


AG_SEMS = [pltpu.SemaphoreType.DMA((7,)), pltpu.SemaphoreType.DMA((7,)), pltpu.SemaphoreType.DMA]


def _ag_phases(x_ref, out_ref, send_sems, recv_sems, local_sem, sum_ref=None):
    m_per = x_ref.shape[0]
    x, y, c = lax.axis_index("x"), lax.axis_index("y"), lax.axis_index("c")
    me, sibling = (x, y, c), (x, y, 1 - c)
    chips = [(1 - x, y), (x, 1 - y), (1 - x, 1 - y)]

    def rows(px, py, pc):
        return out_ref.at[pl.ds((4 * px + 2 * py + pc) * m_per, m_per), :]

    def copy(k, block, to, src=None):
        return pltpu.make_async_remote_copy(
            src_ref=rows(*block) if src is None else src, dst_ref=rows(*block),
            send_sem=send_sems.at[k], recv_sem=recv_sems.at[k], device_id=to, device_id_type=MESH)

    def mine():
        return pltpu.make_async_copy(x_ref, rows(*me), local_sem)

    def first():
        return [copy(0, me, sibling, src=x_ref)] + [copy(1 + j, me, (*chip, c), src=x_ref) for j, chip in enumerate(chips)]

    def passed():
        return [copy(4 + j, (*chip, c), sibling) for j, chip in enumerate(chips)]

    def start():
        for cp in [mine()] + first():
            cp.start()

    def forward():
        for j, (chip, cp) in enumerate(zip(chips, passed())):
            copy(1 + j, (*chip, c), me).wait_recv()
            cp.start()

    def finish():
        copy(0, sibling, me).wait_recv()
        for j, chip in enumerate(chips):
            copy(4 + j, (*chip, 1 - c), me).wait_recv()
        for cp in first() + passed():
            cp.wait_send()
        mine().wait()
        if sum_ref is not None:
            acc = out_ref[pl.ds(0, m_per), :]
            for d in range(1, N_DEV):
                acc = acc + out_ref[pl.ds(d * m_per, m_per), :]
            sum_ref[...] = acc

    return start, forward, finish


def _all_gather(shard, name):
    m_per, n = shard.shape

    def body(x_ref, out_ref, *sems):
        for phase in _ag_phases(x_ref, out_ref, *sems):
            phase()

    vmem = pl.BlockSpec(memory_space=pltpu.VMEM)
    return pl.pallas_call(
        body, name=name, out_shape=jax.ShapeDtypeStruct((N_DEV * m_per, n), shard.dtype),
        in_specs=[vmem], out_specs=vmem, scratch_shapes=AG_SEMS, compiler_params=_params(32))(shard)


def _reduce_scatter(sc, name):
    def body(g_hbm, out_ref, *scratch):
        for phase in _rs_phases(sc, g_hbm, out_ref, *scratch):
            phase()

    return pl.pallas_call(
        body, name=name, out_shape=jax.ShapeDtypeStruct((sc.rows, sc.width), F32),
        in_specs=[pl.BlockSpec(memory_space=pl.ANY)], out_specs=pl.BlockSpec(memory_space=pltpu.VMEM),
        scratch_shapes=_rs_scratch(sc), compiler_params=_params(32))(sc.array)


class _Scattered:
    def __init__(self, array, rows, pieces, locate):
        self.array, self.rows, self.pieces, self.locate = array, rows, pieces, locate
        self.width = array.shape[-1]


def _blocks(g):
    return _Scattered(g, g.shape[1], ((0, g.shape[1]),), lambda ref, d, row, rows: ref.at[d])


def _w_in_blocks(g_w1t):
    def locate(ref, d, row, rows):
        r = R_IN * d + row
        return ref.at[pl.ds(pl.multiple_of(r + jnp.where(r >= C_KR + ROPE, 128 - ROPE, 0), 8), rows), :]

    cut = C_KR + ROPE - 2 * R_IN
    return _Scattered(g_w1t, R_IN_PAD, ((0, cut), (cut, R_IN - cut)), locate)


def _rs_scratch(sc):
    rr, ww, n = sc.rows, sc.width, 4 * len(sc.pieces)
    return [pltpu.VMEM((4, rr, ww), F32), pltpu.VMEM((4, rr, ww), F32),
            pltpu.VMEM((3, rr, ww), BF16), pltpu.VMEM((3, rr, ww), BF16),
            pltpu.SemaphoreType.DMA((n,)), pltpu.SemaphoreType.DMA((n,)), pltpu.SemaphoreType.DMA((n,)),
            pltpu.SemaphoreType.DMA((3,)), pltpu.SemaphoreType.DMA((3,))]


def _rs_phases(sc, g_hbm, out_ref, own_ref, recv1_ref, sendb_ref, recv2_ref, ld_sems, s1_send, s1_recv, s2_send, s2_recv):
    rr, ww = out_ref.shape
    chunk = next(c for c in (128, 80, 64, 48, 32, 16) if rr % c == 0)
    x, y, c = lax.axis_index("x"), lax.axis_index("y"), lax.axis_index("c")
    chips = [(1 - x, y), (x, 1 - y), (1 - x, 1 - y)]
    npieces = len(sc.pieces)
    filled = sum(rows for _, rows in sc.pieces)

    def pieces(d, buf, k):
        for p, (row, rows) in enumerate(sc.pieces):
            dst = buf.at[k] if (row, rows) == (0, rr) else buf.at[k, pl.ds(row, rows), :]
            yield k * npieces + p, sc.locate(g_hbm, d, row, rows), dst

    def loads():
        return [pltpu.make_async_copy(src, dst, ld_sems.at[s])
                for k in range(4) for s, src, dst in pieces(2 * k + c, own_ref, k)]

    def stage1():
        return [pltpu.make_async_remote_copy(
            src_ref=src, dst_ref=dst, send_sem=s1_send.at[s], recv_sem=s1_recv.at[s],
            device_id=(x, y, 1 - c), device_id_type=MESH)
            for k in range(4) for s, src, dst in pieces(2 * k + (1 - c), recv1_ref, k)]

    def stage2():
        return [pltpu.make_async_remote_copy(
            src_ref=sendb_ref.at[r], dst_ref=recv2_ref.at[r], send_sem=s2_send.at[r],
            recv_sem=s2_recv.at[r], device_id=(cx, cy, c), device_id_type=MESH) for r, (cx, cy) in enumerate(chips)]

    def start():
        if filled < rr:
            own_ref[:, filled:rr, :] = jnp.zeros((4, rr - filled, ww), F32)
            recv1_ref[:, filled:rr, :] = jnp.zeros((4, rr - filled, ww), F32)
        for cp in loads() + stage1():
            cp.start()

    def forward():
        for cp in loads():
            cp.wait()
        for cp in stage1():
            cp.wait_recv()
        sends = stage2()
        for r, (cx, cy) in enumerate(chips):
            kk = 2 * cx + cy

            def pack(i, carry, r=r, kk=kk):
                rows = pl.ds(pl.multiple_of(i * chunk, chunk), chunk)
                sendb_ref[r, rows, :] = (own_ref[kk, rows, :] + recv1_ref[kk, rows, :]).astype(BF16)
                return carry

            lax.fori_loop(0, rr // chunk, pack, 0)
            sends[r].start()

    def finish():
        for cp in stage2():
            cp.wait_recv()
        mine = 2 * x + y

        def total(i, carry):
            rows = pl.ds(pl.multiple_of(i * chunk, chunk), chunk)
            acc = own_ref[mine, rows, :] + recv1_ref[mine, rows, :]
            for r in range(3):
                acc = acc + recv2_ref[r, rows, :].astype(F32)
            out_ref[rows, :] = acc
            return carry

        lax.fori_loop(0, rr // chunk, total, 0)
        for cp in stage1() + stage2():
            cp.wait_send()

    return start, forward, finish


def _fwd_in(x2, w1, gq, gkv, wq, wkv, pos, rope_rows, shard, tm):
    tt = x2.shape[0]
    steps = tt // tm
    gathered_shape = (N_DEV * shard.shape[0], shard.shape[1])

    def body(x_ref, w1_ref, gq_ref, gkv_ref, wq_ref, wkv_ref, pos_ref, rr_ref, shard_ref,
             xq_ref, xkv_ref, ga_ref, u_ref, gb_ref, qn_ref, kvn_ref, qcat_ref, kcat_ref, v_ref,
             c_ref, sa_ref, sb_ref, all_ref, gathered, *sems):
        step = pl.program_id(0)
        start, forward, finish = _ag_phases(shard_ref, gathered, *sems)
        pl.when(step == 0)(start)
        pl.when(step == min(2, steps - 1))(forward)

        @pl.when(step == steps - 1)
        def _():
            finish()
            all_ref[...] = gathered[...]

        ang = pos_ref[...].astype(F32) * rr_ref[0:1, :]
        cos, sin = jnp.cos(ang), jnp.sin(ang)
        c, sa, sb = cos * rr_ref[1:2, :], sin * rr_ref[2:3, :], sin * rr_ref[3:4, :]
        c_ref[...] = c
        sa_ref[...] = sa
        sb_ref[...] = sb
        h = _dot_nt(x_ref[...].astype(BF16), w1_ref[...])
        xq = h[:, C_XQ:C_XKV]
        xkv = h[:, C_XKV:C_KR]
        xq_ref[...] = xq
        xkv_ref[...] = xkv
        ga_ref[...] = h[:, C_GA:C_U]
        u_ref[...] = h[:, C_U:C_GB]
        gb_ref[...] = h[:, C_GB:IN_WP]
        qn =(xq * lax.rsqrt(jnp.mean(xq * xq, axis=-1, keepdims=True) + RMS_EPS) * gq_ref[...]).astype(BF16)
        kvn = (xkv * lax.rsqrt(jnp.mean(xkv * xkv, axis=-1, keepdims=True) + RMS_EPS) * gkv_ref[...]).astype(BF16)
        qn_ref[...] = qn
        kvn_ref[...] = kvn
        q = _dot_nt(qn, wq_ref[...]) * SCALE_LOG2
        kv = _dot_nt(kvn, wkv_ref[...])
        kr = _rope_fwd(h[:, C_KR:C_GA], c, sa, sb).astype(BF16)
        for hd in range(HEADS):
            lo = hd * HEAD_PAD
            qcat_ref[:, lo:lo + NOPE] = q[:, lo:lo + NOPE].astype(BF16)
            qcat_ref[:, lo + NOPE:lo + HEAD_PAD] = _rope_fwd(q[:, lo + NOPE:lo + HEAD_PAD], c, sa, sb).astype(BF16)
            kcat_ref[:, lo:lo + NOPE] = kv[:, hd * NOPE:(hd + 1) * NOPE].astype(BF16)
            kcat_ref[:, lo + NOPE:lo + HEAD_PAD] = kr
        v_ref[...] = kv[:, MLA_W:].astype(BF16)

    def tile(w):
        return pl.BlockSpec((tm, w), lambda i: (i, 0))

    def whole(a):
        return pl.BlockSpec(a.shape, lambda i: (0, 0))

    outs = [(Q_LORA, F32), (KV_LORA, F32), (MLA_W, F32), (POOL_W, F32), (POOL_W, F32),
            (Q_LORA, BF16), (KV_LORA, BF16), (HEADS * HEAD_PAD, BF16), (HEADS * HEAD_PAD, BF16), (MLA_W, BF16),
            (128, F32), (128, F32), (128, F32)]
    return pl.pallas_call(
        body, name="fwd_in", grid=(steps,),
        in_specs=[tile(D_MODEL), whole(w1), whole(gq), whole(gkv), whole(wq), whole(wkv), tile(1), whole(rope_rows),
                  pl.BlockSpec(memory_space=pl.ANY)],
        out_specs=[tile(w) for w, _ in outs] + [pl.BlockSpec(gathered_shape, lambda i: (0, 0))],
        out_shape=[jax.ShapeDtypeStruct((tt, w), dt) for w, dt in outs]
        + [jax.ShapeDtypeStruct(gathered_shape, shard.dtype)],
        scratch_shapes=[pltpu.VMEM(gathered_shape, shard.dtype)] + AG_SEMS,
        compiler_params=_params(56, ("arbitrary",)))(x2, w1, gq, gkv, wq, wkv, pos, rope_rows, shard)


def _tile_cases(pmax_ref, pmin_ref, b, nhalf, qi, ki):
    q0, q1 = b * nhalf + 2 * qi, b * nhalf + 2 * qi + 1
    k0, k1 = b * nhalf + 2 * ki, b * nhalf + 2 * ki + 1
    needed = jnp.maximum(pmax_ref[q0], pmax_ref[q1]) >= jnp.minimum(pmin_ref[k0], pmin_ref[k1])
    visible = jnp.minimum(pmin_ref[q0], pmin_ref[q1]) >= jnp.maximum(pmax_ref[k0], pmax_ref[k1])
    stepped = pmax_ref[q0] < pmin_ref[k1]
    return needed, visible, stepped


def _attn_fwd(qcat, kcat, v, pos_col, pos_blk, pmax, pmin, tq, tk):
    bb, ss, _ = qcat.shape
    nq, nk = ss // tq, ss // tk
    lanes = 128
    hps = ATTN_FWD_HEADS_PER_STEP

    def body(pmax_ref, pmin_ref, q_ref, k_ref, v_ref, pc_ref, pb_ref, o_ref, lse_ref, m_s, acc_s):
        b, qi = pl.program_id(0), pl.program_id(2)
        m_s[...] = jnp.full(m_s.shape, NEG, F32)
        acc_s[...] = jnp.zeros(acc_s.shape, F32)

        def part(ki, masked, q_lo, q_n, k_n):
            qrows = slice(q_lo, q_lo + q_n)
            krows = pl.ds(pl.multiple_of(ki * tk, tk), k_n)
            if masked:
                mask = pc_ref[qrows, :] >= pb_ref[ki][:, :k_n]
            ones = jnp.ones((k_n, lanes), BF16)
            for hd in range(hps):
                qk = slice(hd * HEAD_PAD, (hd + 1) * HEAD_PAD)
                s = _dot_nt(q_ref[qrows, qk], k_ref[krows, qk])
                if masked:
                    s = jnp.where(mask, s, NEG)
                m_prev = m_s[hd, qrows, :]
                m_new = jnp.maximum(m_prev, jnp.max(s, axis=-1, keepdims=True))
                p = jnp.exp2(s - jnp.tile(m_new, (1, k_n // lanes)))
                a = jnp.exp2(m_prev - m_new)
                vv = jnp.concatenate([v_ref[krows, hd * V_DIM:(hd + 1) * V_DIM], ones], axis=1)
                acc_s[hd, qrows, :] = jnp.tile(a, (1, 2)) * acc_s[hd, qrows, :] + _dot(p.astype(BF16), vv)
                m_s[hd, qrows, :] = m_new

        def step(ki, carry):
            needed, visible, stepped = _tile_cases(pmax_ref, pmin_ref, b, 2 * nq, qi, ki)

            @pl.when(needed & visible)
            def _():
                part(ki, False, 0, tq, tk)

            @pl.when(needed & ~visible & stepped)
            def _():
                part(ki, True, 0, tq // 2, tk // 2)
                part(ki, True, tq // 2, tq // 2, tk)

            @pl.when(needed & ~visible & ~stepped)
            def _():
                part(ki, True, 0, tq, tk)

            return carry

        lax.fori_loop(0, nk, step, 0)
        for hd in range(hps):
            acc = acc_s[hd]
            l = acc[:, V_DIM:]
            o_ref[:, hd * V_DIM:(hd + 1) * V_DIM] = acc[:, :V_DIM] / l
            lse_ref[hd] = (m_s[hd] + jnp.log2(l)).T[0:1, :]

    grid_spec = pltpu.PrefetchScalarGridSpec(
        num_scalar_prefetch=2, grid=(bb, HEADS // hps, nq),
        in_specs=[
            pl.BlockSpec((None, tq, hps * HEAD_PAD), lambda b, h, i, *_: (b, i, h)),
            pl.BlockSpec((None, ss, hps * HEAD_PAD), lambda b, h, i, *_: (b, 0, h)),
            pl.BlockSpec((None, ss, hps * V_DIM), lambda b, h, i, *_: (b, 0, h)),
            pl.BlockSpec((None, tq, 1), lambda b, h, i, *_: (b, i, 0)),
            pl.BlockSpec((None, nk, 1, tk), lambda b, h, i, *_: (b, 0, 0, 0)),
        ],
        out_specs=[
            pl.BlockSpec((None, tq, hps * V_DIM), lambda b, h, i, *_: (b, i, h)),
            pl.BlockSpec((None, hps, 1, tq), lambda b, h, i, *_: (b, h, 0, i)),
        ],
        scratch_shapes=[pltpu.VMEM((hps, tq, lanes), F32), pltpu.VMEM((hps, tq, 2 * V_DIM), F32)])
    return pl.pallas_call(
        body, name="attn_fwd", grid_spec=grid_spec,
        out_shape=[jax.ShapeDtypeStruct((bb, ss, MLA_W), F32), jax.ShapeDtypeStruct((bb, HEADS, 1, ss), F32)],
        compiler_params=_params(48, ("arbitrary", "arbitrary", "arbitrary")))(pmax, pmin, qcat, kcat, v, pos_col, pos_blk)


def _post(x, tgt, o, ga, u, gb, w_out, pool_wb, pool_scale, ln_g, ln_b, ts):
    bb, ss, _ = x.shape
    nt = ss // ts
    hb = ts // POOL_HALO

    def body(x_ref, tgt_ref, o_ref, ga_ref, u_ref, uh_ref, gb_ref, wo_ref, pw_ref, ps_ref, lg_ref, lb_ref,
             dz_ref, ycat_ref, do_ref, dga_ref, dgb_ref, dpc_ref, dl_ref, loss_ref, glg_ref, glb_ref, gps_ref, gpw_ref):
        step = pl.program_id(0)
        j = step % nt

        @pl.when(step == 0)
        def _():
            for r in (loss_ref, glg_ref, glb_ref, gps_ref, gpw_ref):
                r[...] = jnp.zeros(r.shape, F32)

        o, ga, u, gb = o_ref[...], ga_ref[...], u_ref[...], gb_ref[...]
        sa, dsa = _silu_parts(ga)
        sb, dsb = _silu_parts(gb)
        ext = jnp.concatenate([jnp.where(j > 0, uh_ref[...], 0.0), u], axis=0)
        s2 = ext + pltpu.roll(ext, 1, 0)
        s4 = s2 + pltpu.roll(s2, 2, 0)
        s8 = s4 + pltpu.roll(s4, 4, 0)
        s16 = s8 + pltpu.roll(s8, 8, 0)
        cnt = _pool_cnt(j * ts, ts)
        pooled = (_pick_groups(s2, s4, s8, s16, POOL_HALO, POOL_HALO + ts) / cnt - u).astype(BF16)
        mixed = jnp.concatenate(
            [_dot(pooled[:, g * POOL_C:(g + 1) * POOL_C], pw_ref[g]) for g in range(POOL_G)], axis=1)
        ps = ps_ref[...]
        scaled = mixed * ps
        ycat = jnp.concatenate([o * sa, scaled * sb], axis=1).astype(BF16)
        ycat_ref[...] = ycat
        z = ALPHA * x_ref[...] + _dot(ycat, wo_ref[...])
        mu = jnp.mean(z, axis=-1, keepdims=True)
        zc = z - mu
        rstd = lax.rsqrt(jnp.mean(zc * zc, axis=-1, keepdims=True) + LN_EPS)
        xhat = zc * rstd
        lg = lg_ref[...]
        diff = xhat * lg + lb_ref[...] - tgt_ref[...]
        loss_ref[...] += jnp.sum(diff * diff) * (0.5 / D_MODEL)
        glb_ref[...] += jnp.sum(diff, axis=0, keepdims=True) * (1.0 / D_MODEL)
        glg_ref[...] += jnp.sum(diff * xhat, axis=0, keepdims=True) * (1.0 / D_MODEL)
        dxh = diff * (lg * (1.0 / D_MODEL))
        dz = rstd * (dxh - jnp.mean(dxh, axis=-1, keepdims=True) - xhat * jnp.mean(dxh * xhat, axis=-1, keepdims=True))
        dz_ref[...] = dz
        dycat = _dot_nt(dz.astype(BF16), wo_ref[...])
        dya, dyb = dycat[:, :MLA_W], dycat[:, MLA_W:]
        do = dya * sa
        do_ref[...] = do.astype(BF16)
        doo = do * o
        for hd in range(HEADS):
            dl_ref[hd] = jnp.sum(doo[:, hd * V_DIM:(hd + 1) * V_DIM].T, axis=0, keepdims=True)
        dga_ref[...] = (dya * o * dsa).astype(BF16)
        dgb_ref[...] = (dyb * scaled * dsb).astype(BF16)
        dscaled = dyb * sb
        gps_ref[...] += jnp.sum(dscaled * mixed, axis=0, keepdims=True)
        dmixed = (dscaled * ps).astype(BF16)
        dpooled = []
        for g in range(POOL_G):
            cols = slice(g * POOL_C, (g + 1) * POOL_C)
            gpw_ref[g] += _dot_tn(pooled[:, cols], dmixed[:, cols])
            dpooled.append(_dot_nt(dmixed[:, cols], pw_ref[g]))
        dpc_ref[...] = jnp.concatenate(dpooled, axis=1) / cnt

    def tile(w):
        return pl.BlockSpec((None, ts, w), lambda i: (i // nt, i % nt, 0))

    def whole(a):
        nd = a.ndim
        return pl.BlockSpec(a.shape, lambda i: (0,) * nd)

    halo = pl.BlockSpec((None, POOL_HALO, POOL_W), lambda i: (i // nt, jnp.maximum((i % nt) * hb - 1, 0), 0))
    acc_shapes = [(8, 128), (1, D_MODEL), (1, D_MODEL), (1, POOL_W), (POOL_G, POOL_C, POOL_C)]
    tile_outs = [(D_MODEL, F32), (D_MODEL, BF16), (MLA_W, BF16), (MLA_W, BF16), (POOL_W, BF16), (POOL_W, F32)]
    return pl.pallas_call(
        body, name="post", grid=(bb * nt,),
        in_specs=[tile(D_MODEL), tile(D_MODEL), tile(MLA_W), tile(MLA_W), tile(POOL_W), halo, tile(POOL_W),
                  whole(w_out), whole(pool_wb), whole(pool_scale), whole(ln_g), whole(ln_b)],
        out_specs=[tile(w) for w, _ in tile_outs]
        + [pl.BlockSpec((None, HEADS, 1, ts), lambda i: (i // nt, 0, 0, i % nt))]
        + [pl.BlockSpec(s, lambda i, n=len(s): (0,) * n) for s in acc_shapes],
        out_shape=[jax.ShapeDtypeStruct((bb, ss, w), dt) for w, dt in tile_outs]
        + [jax.ShapeDtypeStruct((bb, HEADS, 1, ss), F32)]
        + [jax.ShapeDtypeStruct(s, F32) for s in acc_shapes],
        compiler_params=_params(48, ("arbitrary",)))(x, tgt, o, ga, u, u, gb, w_out, pool_wb, pool_scale, ln_g, ln_b)


def _attn_bwd(qcat, kcat, v, do, lse, dl, rc, rsa, rsb, pos_col, pos_blk, pmax, pmin, tq, tk):
    bb, ss, _ = qcat.shape
    nq, nk = ss // tq, ss // tk
    hps = ATTN_BWD_HEADS_PER_STEP

    def body(pmax_ref, pmin_ref, q_ref, k_ref, v_ref, do_ref, lse_ref, dl_ref, c_ref, sa_ref, sb_ref, pc_ref, pb_ref,
             dqp_ref, dkn_ref, dv_ref, dkr_ref, dq_s, dk_s, dv_s):
        b = pl.program_id(0)
        dq_s[...] = jnp.zeros(dq_s.shape, F32)

        def part(qi, ki, masked):
            krows = pl.ds(pl.multiple_of(ki * tk, tk), tk)
            qrows = pl.ds(pl.multiple_of(qi * tq, tq), tq)
            if masked:
                mask = pb_ref[qi] >= pc_ref[krows, :]
            for hd in range(hps):
                qk = slice(hd * HEAD_PAD, (hd + 1) * HEAD_PAD)
                vs = slice(hd * V_DIM, (hd + 1) * V_DIM)
                q = q_ref[qrows, qk]
                dd = do_ref[qrows, vs]
                st = _dot_nt(k_ref[krows, qk], q)
                if masked:
                    st = jnp.where(mask, st, NEG)
                pt = jnp.exp2(st - lse_ref[hd, qi])
                dv_s[:, vs] += _dot(pt.astype(BF16), dd)
                dpt = _dot_nt(v_ref[krows, vs], dd)
                dst = (pt * (dpt - dl_ref[hd, qi])).astype(BF16)
                dk_s[:, qk] += _dot(dst, q)
                dq_s[qrows, qk] += _dot_tn(dst, k_ref[krows, qk])

        def kv_step(ki, carry):
            krows = pl.ds(pl.multiple_of(ki * tk, tk), tk)
            dk_s[...] = jnp.zeros(dk_s.shape, F32)
            dv_s[...] = jnp.zeros(dv_s.shape, F32)

            def q_step(qi, c2):
                needed, visible, _ = _tile_cases(pmax_ref, pmin_ref, b, 2 * nq, qi, ki)

                @pl.when(needed & visible)
                def _():
                    part(qi, ki, False)

                @pl.when(needed & ~visible)
                def _():
                    part(qi, ki, True)

                return c2

            lax.fori_loop(0, nq, q_step, 0)
            dkr = jnp.zeros((tk, HEAD_PAD - NOPE), F32)
            for hd in range(hps):
                lo = hd * HEAD_PAD
                dkn_ref[krows, hd * NOPE:(hd + 1) * NOPE] = (dk_s[:, lo:lo + NOPE] * LN2).astype(BF16)
                dkr = dkr + dk_s[:, lo + NOPE:lo + HEAD_PAD]
            dkr_ref[krows, :] = dkr * LN2
            dv_ref[krows, :] = dv_s[...].astype(BF16)
            return carry

        lax.fori_loop(0, nk, kv_step, 0)
        c, sa, sb = c_ref[...], sa_ref[...], sb_ref[...]
        for hd in range(hps):
            lo = hd * HEAD_PAD
            dqp_ref[:, lo:lo + NOPE] = (dq_s[:, lo:lo + NOPE] * SCALE).astype(BF16)
            dqp_ref[:, lo + NOPE:lo + HEAD_PAD] = _rope_bwd(
                dq_s[:, lo + NOPE:lo + HEAD_PAD] * SCALE, c, sa, sb).astype(BF16)

    def per_head(w):
        return pl.BlockSpec((None, ss, hps * w), lambda b, h, *_: (b, 0, h))

    def rows_of_head():
        return pl.BlockSpec((None, hps, nq, 1, tq), lambda b, h, *_: (b, h, 0, 0, 0))

    def per_batch(w):
        return pl.BlockSpec((None, ss, w), lambda b, h, *_: (b, 0, 0))

    grid_spec = pltpu.PrefetchScalarGridSpec(
        num_scalar_prefetch=2, grid=(bb, HEADS // hps),
        in_specs=[per_head(HEAD_PAD), per_head(HEAD_PAD), per_head(V_DIM), per_head(V_DIM),
                  rows_of_head(), rows_of_head(), per_batch(128), per_batch(128), per_batch(128), per_batch(1),
                  pl.BlockSpec((None, nq, 1, tq), lambda b, h, *_: (b, 0, 0, 0))],
        out_specs=[per_head(HEAD_PAD), per_head(NOPE), per_head(V_DIM),
                   pl.BlockSpec((None, None, ss, HEAD_PAD - NOPE), lambda b, h, *_: (b, h, 0, 0))],
        scratch_shapes=[pltpu.VMEM((ss, hps * HEAD_PAD), F32), pltpu.VMEM((tk, hps * HEAD_PAD), F32),
                        pltpu.VMEM((tk, hps * V_DIM), F32)])
    return pl.pallas_call(
        body, name="attn_bwd", grid_spec=grid_spec,
        out_shape=[jax.ShapeDtypeStruct((bb, ss, HEADS * HEAD_PAD), BF16),
                   jax.ShapeDtypeStruct((bb, ss, MLA_W), BF16),
                   jax.ShapeDtypeStruct((bb, ss, MLA_W), BF16),
                   jax.ShapeDtypeStruct((bb, HEADS // hps, ss, HEAD_PAD - NOPE), F32)],
        compiler_params=_params(56, ("arbitrary", "arbitrary")))(
            pmax, pmin, qcat, kcat, v, do, lse, dl, rc, rsa, rsb, pos_col, pos_blk)


def _bwd_mid(dqp, dkn, dv, dkr, xq, xkv, rc, rsa, rsb, wq, wkv, gq, gkv, dga, dgb, dpc, dz, w1, ts):
    bb, ss, _ = dz.shape
    nt = ss // ts
    hb = ts // POOL_HALO
    groups = dkr.shape[1]

    def body(dqp_ref, dkn_ref, dv_ref, dkr_ref, xq_ref, xkv_ref, c_ref, sa_ref, sb_ref, wq_ref, wkv_ref, gq_ref,
             gkv_ref, dga_ref, dgb_ref, dpc_ref, dph_ref, dz_ref, w1_ref,
             gx_ref, dh_ref, ggq_ref, ggkv_ref):
        step = pl.program_id(0)
        j = step % nt

        @pl.when(step == 0)
        def _():
            ggq_ref[...] = jnp.zeros(ggq_ref.shape, F32)
            ggkv_ref[...] = jnp.zeros(ggkv_ref.shape, F32)

        dkr = dkr_ref[0]
        for g in range(1, groups):
            dkr = dkr + dkr_ref[g]
        dh_ref[:, C_KR:C_GA] = _rope_bwd(dkr, c_ref[...], sa_ref[...], sb_ref[...]).astype(BF16)

        def rms_bwd(x, g, dn):
            inv = lax.rsqrt(jnp.mean(x * x, axis=-1, keepdims=True) + RMS_EPS)
            xh = x * inv
            dxh = dn * g
            return inv * (dxh - xh * jnp.mean(dxh * xh, axis=-1, keepdims=True)), jnp.sum(dn * xh, axis=0, keepdims=True)

        dxq, ggq = rms_bwd(xq_ref[...], gq_ref[...], _dot(dqp_ref[...], wq_ref[...]))
        dkvn = _dot(dkn_ref[...], wkv_ref[:MLA_W, :]) + _dot(dv_ref[...], wkv_ref[MLA_W:, :])
        dxkv, ggkv = rms_bwd(xkv_ref[...], gkv_ref[...], dkvn)
        ggq_ref[...] += ggq
        ggkv_ref[...] += ggkv
        dh_ref[:, C_XQ:C_XKV] = dxq.astype(BF16)
        dh_ref[:, C_XKV:C_KR] = dxkv.astype(BF16)
        dh_ref[:, C_GA:C_U] = dga_ref[...]
        dh_ref[:, C_GB:IN_WP] = dgb_ref[...]
        dpc = dpc_ref[...]
        n = ts + POOL_HALO
        ext = jnp.concatenate([dpc, jnp.where(j < nt - 1, dph_ref[...], 0.0)], axis=0)
        r2 = ext + pltpu.roll(ext, n - 1, 0)
        r4 = r2 + pltpu.roll(r2, n - 2, 0)
        r8 = r4 + pltpu.roll(r4, n - 4, 0)
        r16 = r8 + pltpu.roll(r8, n - 8, 0)
        du = _pick_groups(r2, r4, r8, r16, 0, ts) - dpc * _pool_cnt(j * ts, ts)
        dh_ref[:, C_U:C_GB] = du.astype(BF16)
        gx_ref[...] = ALPHA * dz_ref[...] + _dot(dh_ref[...], w1_ref[...])

    def tile(w):
        return pl.BlockSpec((None, ts, w), lambda i: (i // nt, i % nt, 0))

    def whole(a):
        return pl.BlockSpec(a.shape, lambda i: (0, 0))

    halo = pl.BlockSpec((None, POOL_HALO, POOL_W),
                        lambda i: (i // nt, jnp.minimum((i % nt + 1) * hb, ss // POOL_HALO - 1), 0))
    return pl.pallas_call(
        body, name="bwd_mid", grid=(bb * nt,),
        in_specs=[tile(HEADS * HEAD_PAD), tile(MLA_W), tile(MLA_W),
                  pl.BlockSpec((None, groups, ts, HEAD_PAD - NOPE), lambda i: (i // nt, 0, i % nt, 0)),
                  tile(Q_LORA), tile(KV_LORA),
                  tile(128), tile(128), tile(128), whole(wq), whole(wkv), whole(gq), whole(gkv),
                  tile(MLA_W), tile(POOL_W), tile(POOL_W), halo, tile(D_MODEL), whole(w1)],
        out_specs=[tile(D_MODEL), tile(IN_WP),
                   pl.BlockSpec((1, Q_LORA), lambda i: (0, 0)), pl.BlockSpec((1, KV_LORA), lambda i: (0, 0))],
        out_shape=[jax.ShapeDtypeStruct((bb, ss, D_MODEL), F32), jax.ShapeDtypeStruct((bb, ss, IN_WP), BF16),
                   jax.ShapeDtypeStruct((1, Q_LORA), F32), jax.ShapeDtypeStruct((1, KV_LORA), F32)],
        compiler_params=_params(48, ("arbitrary",)))(
            dqp, dkn, dv, dkr, xq, xkv, rc, rsa, rsb, wq, wkv, gq, gkv, dga, dgb, dpc, dpc, dz, w1)


def _grad_w(pairs, bt, name, b_cols=None, carried=None, reduced=None):
    tt = pairs[0][0].shape[0]
    steps = tt // bt
    npairs = len(pairs)
    n_rs = len(_rs_scratch(carried)) if carried is not None else 0

    def body(*refs):
        ab, rest = refs[:2 * npairs], list(refs[2 * npairs:])
        g_hbm = rest.pop(0) if carried is not None else None
        part = rest.pop(0) if reduced is not None else None
        outs = [rest.pop(0) for _ in range(npairs)]
        phases = []
        if carried is not None:
            rs_out = rest.pop(0)
        if reduced is not None:
            sum_ref = rest.pop(0)
        if carried is not None:
            phases.append(_rs_phases(carried, g_hbm, rs_out, *rest[:n_rs]))
        if reduced is not None:
            gathered, *sems = rest[n_rs:]
            phases.append(_ag_phases(part, gathered, *sems, sum_ref=sum_ref))
        step = pl.program_id(0)

        @pl.when(step == 0)
        def _():
            for o in outs:
                o[...] = jnp.zeros(o.shape, F32)
            for start, _, _ in phases:
                start()

        for i in range(npairs):
            outs[i][...] += _dot_tn(ab[2 * i][...].astype(BF16), ab[2 * i + 1][...].astype(BF16))

        @pl.when(step == min(2, steps - 1))
        def _():
            for _, forward, _ in phases:
                forward()

        @pl.when(step == steps - 1)
        def _():
            for _, _, finish in phases:
                finish()

    in_specs, out_specs, out_shape = [], [], []
    for a, b in pairs:
        m, (n, col) = a.shape[1], b_cols or (b.shape[1], 0)
        in_specs += [pl.BlockSpec((bt, m), lambda i: (i, 0)), pl.BlockSpec((bt, n), lambda i, col=col: (i, col))]
        out_specs.append(pl.BlockSpec((m, n), lambda i: (0, 0)))
        out_shape.append(jax.ShapeDtypeStruct((m, n), F32))
    args = [t for p in pairs for t in p]
    scratch = []
    if carried is not None:
        in_specs.append(pl.BlockSpec(memory_space=pl.ANY))
        out_specs.append(pl.BlockSpec((carried.rows, carried.width), lambda i: (0, 0)))
        out_shape.append(jax.ShapeDtypeStruct((carried.rows, carried.width), F32))
        args.append(carried.array)
        scratch += _rs_scratch(carried)
    if reduced is not None:
        in_specs.append(pl.BlockSpec(memory_space=pl.ANY))
        out_specs.append(pl.BlockSpec(reduced.shape, lambda i: (0, 0)))
        out_shape.append(jax.ShapeDtypeStruct(reduced.shape, F32))
        args.append(reduced)
        scratch += [pltpu.VMEM((N_DEV * reduced.shape[0], reduced.shape[1]), F32)] + AG_SEMS
    return pl.pallas_call(
        body, name=name, grid=(steps,), in_specs=in_specs, out_specs=out_specs, out_shape=out_shape,
        scratch_shapes=scratch, compiler_params=_params(56, ("arbitrary",)))(*args)


def _adamw(triples):
    n = len(triples)

    def body(*refs):
        ins, outs = refs[:4 * n], refs[4 * n:]
        for i in range(n):
            w, g, m, v = (r[...] for r in ins[4 * i:4 * i + 4])
            m = ADAM_B1 * m + (1.0 - ADAM_B1) * g
            v = ADAM_B2 * v + (1.0 - ADAM_B2) * jnp.square(g)
            m_hat = m / (1.0 - ADAM_B1 ** ADAM_STEP)
            v_hat = v / (1.0 - ADAM_B2 ** ADAM_STEP)
            outs[3 * i][...] = -ADAM_LR * (m_hat / (jnp.sqrt(v_hat) + ADAM_EPS) + ADAM_WD * w)
            outs[3 * i + 1][...] = m
            outs[3 * i + 2][...] = v

    flat = [a for t in triples for a in t]
    vmem = pl.BlockSpec(memory_space=pltpu.VMEM)
    res = pl.pallas_call(
        body, name="adamw", in_specs=[vmem] * len(flat), out_specs=[vmem] * (3 * n),
        out_shape=[jax.ShapeDtypeStruct(t[0].shape, F32) for t in triples for _ in range(3)],
        compiler_params=_params(48))(*flat)
    return [tuple(res[3 * i:3 * i + 3]) for i in range(n)]


def _shard_slab(w_in, w_uq, w_ukv):
    mixed = jnp.concatenate(
        [_pad_rows(w_uq.T, R_MIX), w_ukv.T, jnp.zeros((R_MIX, 1024 - Q_LORA - KV_LORA), F32)], axis=1)
    return jnp.concatenate([mixed, w_in.T, jnp.zeros((SLAB_ROWS - O_IN - R_IN, 1024), F32)], axis=0)


def _unpack_weights(slabs):
    uq = slabs[:, O_MIX:O_MIX + R_UQ, :Q_LORA].reshape(HEADS, QK_DIM, Q_LORA)
    wqt = jnp.pad(uq, ((0, 0), (0, HEAD_PAD - QK_DIM), (0, 0))).reshape(HEADS * HEAD_PAD, Q_LORA)
    ukv = slabs[:, O_MIX:O_MIX + R_MIX, Q_LORA:Q_LORA + KV_LORA].reshape(HEADS, 2, NOPE, KV_LORA)
    wkvt = ukv.transpose(1, 0, 2, 3).reshape(2 * MLA_W, KV_LORA)
    raw = slabs[:, O_IN:O_IN + R_IN].reshape(IN_W, D_MODEL)
    w1t = jnp.concatenate([raw[:768 + ROPE], jnp.zeros((128 - ROPE, D_MODEL), BF16), raw[768 + ROPE:]], axis=0)
    return w1t, wqt, wkvt


def _mixed_band(g_wqt, g_wkvt):
    uq = g_wqt.reshape(HEADS, HEAD_PAD, Q_LORA)[:, :QK_DIM].reshape(N_DEV, R_UQ, Q_LORA)
    uq = jnp.pad(uq, ((0, 0), (0, R_MIX - R_UQ), (0, 0)))
    ukv = g_wkvt.reshape(2, HEADS, NOPE, KV_LORA).transpose(1, 0, 2, 3).reshape(N_DEV, R_MIX, KV_LORA)
    return jnp.concatenate([uq, ukv, jnp.zeros((N_DEV, R_MIX, 1024 - Q_LORA - KV_LORA), F32)], axis=2)


def _rope_rows():
    half = ROPE // 2
    inv_freq = ROPE_THETA ** (-jnp.arange(half, dtype=F32) / half)
    zero, one = jnp.zeros((half,), F32), jnp.ones((half,), F32)
    rows = [jnp.concatenate(r) for r in (
        (inv_freq, inv_freq, zero, zero), (one, one, zero, zero), (-one, zero, zero, zero), (zero, one, zero, zero))]
    return jnp.stack(rows + [jnp.zeros((128,), F32)] * 4)


def _pad_rows(a, rows):
    return jnp.pad(a, ((0, rows - a.shape[0]), (0, 0)))


def kernel(x, positions, w_in, q_norm_g, w_uq, kv_norm_g, w_ukv, pool_w, pool_scale, w_out, ln_g, ln_b, loss_target, m_w_in, m_q_norm_g, m_w_uq, m_kv_norm_g, m_w_ukv, m_pool_w, m_pool_scale, m_w_out, m_ln_g, m_ln_b, v_w_in, v_q_norm_g, v_w_uq, v_kv_norm_g, v_w_ukv, v_pool_w, v_pool_scale, v_w_out, v_ln_g, v_ln_b):
    bb, ss, _ = x.shape
    tt = bb * ss
    atile = min(512, ss)
    nblk = ss // atile

    slab = _shard_slab(w_in, w_uq, w_ukv).astype(BF16)
    slabs = _all_gather(slab, "gather_weights").reshape(N_DEV, SLAB_ROWS, 1024)
    w1, wq, wkv = _unpack_weights(slabs)

    gq, gkv = q_norm_g.reshape(1, Q_LORA), kv_norm_g.reshape(1, KV_LORA)
    ps = pool_scale.reshape(1, POOL_W)
    pos_col = positions.reshape(bb, ss, 1)
    pos_blk = positions.reshape(bb, nblk, 1, atile)
    pmax = jnp.max(positions.reshape(bb, 2 * nblk, atile // 2), axis=-1).reshape(-1)
    pmin = jnp.min(positions.reshape(bb, 2 * nblk, atile // 2), axis=-1).reshape(-1)

    x2 = x.reshape(tt, D_MODEL)
    xq, xkv, ga, u, gb, qn, kvn, qcat, kcat, v, rc, rsa, rsb, wo = _fwd_in(
        x2, w1, gq, gkv, wq, wkv, pos_col.reshape(tt, 1), _rope_rows(), w_out.astype(BF16), atile)
    as3 = lambda a: a.reshape(bb, ss, a.shape[-1])
    qcat, kcat, v = as3(qcat), as3(kcat), as3(v)
    o, lse = _attn_fwd(qcat, kcat, v, pos_col, pos_blk, pmax, pmin, atile, atile)
    dz, ycat, do, dga, dgb, dpc, dl, loss_p, g_lng, g_lnb, g_ps, g_pw = _post(
        x, loss_target, o, as3(ga), as3(u), as3(gb), wo, pool_w.astype(BF16), ps, ln_g, ln_b, atile)

    bt = min(1024, tt)
    g_wo, = _grad_w([(ycat.reshape(tt, D_MODEL), dz.reshape(tt, D_MODEL))], bt, "grad_w_out")
    rows5 = lambda a: a.reshape(bb, HEADS, nblk, 1, atile)
    rc, rsa, rsb = as3(rc), as3(rsa), as3(rsb)
    dqp, dkn, dv, dkr = _attn_bwd(
        qcat, kcat, v, do, rows5(lse), rows5(dl), rc, rsa, rsb, pos_col, pos_blk, pmax, pmin, atile, atile)
    g_wq, g_wkn, g_wv, rs_out = _grad_w(
        [(dqp.reshape(tt, HEADS * HEAD_PAD), qn), (dkn.reshape(tt, MLA_W), kvn), (dv.reshape(tt, MLA_W), kvn)], bt,
        "grad_w_uqkv", carried=_blocks(g_wo.reshape(N_DEV, R_OUT, D_MODEL)))
    g_wkv = jnp.concatenate([g_wkn, g_wv], axis=0)
    grad_x, dh, g_gq, g_gkv = _bwd_mid(
        dqp, dkn, dv, dkr, as3(xq), as3(xkv), rc, rsa, rsb, wq, wkv, gq, gkv, dga, dgb, dpc, dz, w1, atile)
    small = jnp.concatenate(
        [_pad_rows(g_lng.reshape(8, 128), 8), _pad_rows(g_lnb.reshape(8, 128), 8), _pad_rows(g_gq.reshape(4, 128), 8),
         _pad_rows(g_gkv.reshape(2, 128), 8), _pad_rows(g_ps.reshape(4, 128), 8), g_pw.reshape(POOL_G * POOL_C, 128),
         loss_p], axis=0)
    dh2, half = dh.reshape(tt, IN_WP), D_MODEL // 2
    g_w1a, rs_mix, small = _grad_w([(dh2, x2)], bt, "grad_w_in_a", b_cols=(half, 0),
                                   carried=_blocks(_mixed_band(g_wq, g_wkv)), reduced=small)
    g_w1b, rs_in_a = _grad_w([(dh2, x2)], bt, "grad_w_in_b", b_cols=(half, 1), carried=_w_in_blocks(g_w1a))
    rs_in = jnp.concatenate([rs_in_a, _reduce_scatter(_w_in_blocks(g_w1b), "reduce_scatter_w_in")], axis=1)
    loss = small[40 + POOL_G * POOL_C, 0]
    grads = {
        "w_in": rs_in[:R_IN],
        "q_norm_g": small[16:20].reshape(1, Q_LORA),
        "w_uq": rs_mix[:R_UQ, :Q_LORA],
        "kv_norm_g": small[24:26].reshape(1, KV_LORA),
        "w_ukv": rs_mix[:, Q_LORA:Q_LORA + KV_LORA].T,
        "pool_w": small[40:40 + POOL_G * POOL_C],
        "pool_scale": small[32:36].reshape(1, POOL_W),
        "w_out": rs_out,
        "ln_g": small[0:8].reshape(1, D_MODEL),
        "ln_b": small[8:16].reshape(1, D_MODEL),
    }
    transposed = ("w_in", "w_uq")

    names = ["w_in", "q_norm_g", "w_uq", "kv_norm_g", "w_ukv", "pool_w", "pool_scale", "w_out", "ln_g", "ln_b"]
    weights = dict(w_in=w_in, q_norm_g=q_norm_g, w_uq=w_uq, kv_norm_g=kv_norm_g, w_ukv=w_ukv, pool_w=pool_w,
                   pool_scale=pool_scale, w_out=w_out, ln_g=ln_g, ln_b=ln_b)
    moms = dict(w_in=(m_w_in, v_w_in), q_norm_g=(m_q_norm_g, v_q_norm_g), w_uq=(m_w_uq, v_w_uq),
                kv_norm_g=(m_kv_norm_g, v_kv_norm_g), w_ukv=(m_w_ukv, v_w_ukv), pool_w=(m_pool_w, v_pool_w),
                pool_scale=(m_pool_scale, v_pool_scale), w_out=(m_w_out, v_w_out), ln_g=(m_ln_g, v_ln_g),
                ln_b=(m_ln_b, v_ln_b))
    as2 = lambda a, n: a.T if n in transposed else a.reshape(grads[n].shape)
    upd = _adamw([(as2(weights[n], n), grads[n], as2(moms[n][0], n), as2(moms[n][1], n)) for n in names])
    shaped = lambda a, n: a.T if n in transposed else a.reshape(weights[n].shape)
    return (loss, grad_x,
            *[shaped(grads[n], n) for n in names],
            *[shaped(upd[i][0], n) for i, n in enumerate(names)],
            *[shaped(upd[i][1], n) for i, n in enumerate(names)],
            *[shaped(upd[i][2], n) for i, n in enumerate(names)])
```

```python
import functools

import jax
import jax.numpy as jnp
from jax import lax
from jax.experimental import pallas as pl
from jax.experimental.pallas import tpu as pltpu

F32 = jnp.float32
BF16 = jnp.bfloat16
MESH = pl.DeviceIdType.MESH

N_DEV = 8
D_MODEL = 1024
HEADS = 4
NOPE = 128
ROPE = 64
V_DIM = 128
QK_DIM = NOPE + ROPE
HEAD_PAD = 256
Q_LORA = 512
KV_LORA = 256
MLA_W = HEADS * V_DIM
POOL_W = 512
POOL_G = 4
POOL_C = 128
POOL_HALO = 16
IN_W = 2368
IN_WP = 2432
C_XQ, C_XKV, C_KR, C_GA, C_U, C_GB = 0, 512, 768, 896, 1408, 1920
ROPE_THETA = 10000.0
RMS_EPS = 1e-6
LN_EPS = 1e-5
ALPHA = 2.0 ** 0.25
SCALE = QK_DIM ** -0.5
LN2 = 0.6931471805599453
SCALE_LOG2 = SCALE / LN2
NEG = float(jnp.finfo(jnp.float32).min)

ADAM_LR = 0.001
ADAM_B1 = 0.9
ADAM_B2 = 0.999
ADAM_EPS = 1e-08
ADAM_WD = 0.01
ADAM_STEP = 10

R_OUT, R_MIX, R_UQ, R_IN = 128, 128, 96, 296
O_MIX, O_IN = 0, R_MIX
SLAB_ROWS = 432
R_IN_PAD = 304

V7X_VMEM_BYTES = 64 * 1024 * 1024
ATTN_FWD_HEADS_PER_STEP = 4
ATTN_BWD_HEADS_PER_STEP = 2


def _params(vmem_mb, semantics=None):
    assert vmem_mb * 1024 * 1024 < V7X_VMEM_BYTES
    return pltpu.CompilerParams(vmem_limit_bytes=vmem_mb * 1024 * 1024, dimension_semantics=semantics)


def _dot(a, b):
    return jnp.dot(a, b, preferred_element_type=F32)


def _dot_nt(a, b):
    return lax.dot_general(a, b, (((1,), (1,)), ((), ())), preferred_element_type=F32)


def _dot_tn(a, b):
    return lax.dot_general(a, b, (((0,), (0,)), ((), ())), preferred_element_type=F32)


def _rope_fwd(t, c, sa, sb):
    return t * c + pltpu.roll(t, 96, 1) * sa + pltpu.roll(t, 32, 1) * sb


def _rope_bwd(d, c, sa, sb):
    return d * c + pltpu.roll(d * sa, 32, 1) + pltpu.roll(d * sb, 96, 1)


def _silu_parts(g):
    sig = 0.5 * jnp.tanh(0.5 * g) + 0.5
    silu = g * sig
    return silu, sig + silu - silu * sig


def _pool_cnt(row0, rows):
    t = row0 + lax.broadcasted_iota(jnp.int32, (rows, POOL_W), 0)
    w = 2 << (lax.broadcasted_iota(jnp.int32, (rows, POOL_W), 1) // POOL_C)
    return jnp.minimum(t + 1, w).astype(F32)


def _pick_groups(s2, s4, s8, s16, lo, hi):
    return jnp.concatenate([s2[lo:hi, 0:128], s4[lo:hi, 128:256], s8[lo:hi, 256:384], s16[lo:hi, 384:512]], axis=1)


AG_SEMS = [pltpu.SemaphoreType.DMA((7,)), pltpu.SemaphoreType.DMA((7,)), pltpu.SemaphoreType.DMA]


def _ag_phases(x_ref, out_ref, send_sems, recv_sems, local_sem, sum_ref=None):
    m_per = x_ref.shape[0]
    x, y, c = lax.axis_index("x"), lax.axis_index("y"), lax.axis_index("c")
    me, sibling = (x, y, c), (x, y, 1 - c)
    chips = [(1 - x, y), (x, 1 - y), (1 - x, 1 - y)]

    def rows(px, py, pc):
        return out_ref.at[pl.ds((4 * px + 2 * py + pc) * m_per, m_per), :]

    def copy(k, block, to, src=None):
        return pltpu.make_async_remote_copy(
            src_ref=rows(*block) if src is None else src, dst_ref=rows(*block),
            send_sem=send_sems.at[k], recv_sem=recv_sems.at[k], device_id=to, device_id_type=MESH)

    def mine():
        return pltpu.make_async_copy(x_ref, rows(*me), local_sem)

    def first():
        return [copy(0, me, sibling, src=x_ref)] + [copy(1 + j, me, (*chip, c), src=x_ref) for j, chip in enumerate(chips)]

    def passed():
        return [copy(4 + j, (*chip, c), sibling) for j, chip in enumerate(chips)]

    def start():
        for cp in [mine()] + first():
            cp.start()

    def forward():
        for j, (chip, cp) in enumerate(zip(chips, passed())):
            copy(1 + j, (*chip, c), me).wait_recv()
            cp.start()

    def finish():
        copy(0, sibling, me).wait_recv()
        for j, chip in enumerate(chips):
            copy(4 + j, (*chip, 1 - c), me).wait_recv()
        for cp in first() + passed():
            cp.wait_send()
        mine().wait()
        if sum_ref is not None:
            acc = out_ref[pl.ds(0, m_per), :]
            for d in range(1, N_DEV):
                acc = acc + out_ref[pl.ds(d * m_per, m_per), :]
            sum_ref[...] = acc

    return start, forward, finish


def _all_gather(shard, name):
    m_per, n = shard.shape

    def body(x_ref, out_ref, *sems):
        for phase in _ag_phases(x_ref, out_ref, *sems):
            phase()

    vmem = pl.BlockSpec(memory_space=pltpu.VMEM)
    return pl.pallas_call(
        body, name=name, out_shape=jax.ShapeDtypeStruct((N_DEV * m_per, n), shard.dtype),
        in_specs=[vmem], out_specs=vmem, scratch_shapes=AG_SEMS, compiler_params=_params(32))(shard)


def _reduce_scatter(sc, name):
    def body(g_hbm, out_ref, *scratch):
        for phase in _rs_phases(sc, g_hbm, out_ref, *scratch):
            phase()

    return pl.pallas_call(
        body, name=name, out_shape=jax.ShapeDtypeStruct((sc.rows, sc.width), F32),
        in_specs=[pl.BlockSpec(memory_space=pl.ANY)], out_specs=pl.BlockSpec(memory_space=pltpu.VMEM),
        scratch_shapes=_rs_scratch(sc), compiler_params=_params(32))(sc.array)


class _Scattered:
    def __init__(self, array, rows, pieces, locate):
        self.array, self.rows, self.pieces, self.locate = array, rows, pieces, locate
        self.width = array.shape[-1]


def _blocks(g):
    return _Scattered(g, g.shape[1], ((0, g.shape[1]),), lambda ref, d, row, rows: ref.at[d])


def _w_in_blocks(g_w1t):
    def locate(ref, d, row, rows):
        r = R_IN * d + row
        return ref.at[pl.ds(pl.multiple_of(r + jnp.where(r >= C_KR + ROPE, 128 - ROPE, 0), 8), rows), :]

    cut = C_KR + ROPE - 2 * R_IN
    return _Scattered(g_w1t, R_IN_PAD, ((0, cut), (cut, R_IN - cut)), locate)


def _rs_scratch(sc):
    rr, ww, n = sc.rows, sc.width, 4 * len(sc.pieces)
    return [pltpu.VMEM((4, rr, ww), F32), pltpu.VMEM((4, rr, ww), F32),
            pltpu.VMEM((3, rr, ww), BF16), pltpu.VMEM((3, rr, ww), BF16),
            pltpu.SemaphoreType.DMA((n,)), pltpu.SemaphoreType.DMA((n,)), pltpu.SemaphoreType.DMA((n,)),
            pltpu.SemaphoreType.DMA((3,)), pltpu.SemaphoreType.DMA((3,))]


def _rs_phases(sc, g_hbm, out_ref, own_ref, recv1_ref, sendb_ref, recv2_ref, ld_sems, s1_send, s1_recv, s2_send, s2_recv):
    rr, ww = out_ref.shape
    chunk = next(c for c in (128, 80, 64, 48, 32, 16) if rr % c == 0)
    x, y, c = lax.axis_index("x"), lax.axis_index("y"), lax.axis_index("c")
    chips = [(1 - x, y), (x, 1 - y), (1 - x, 1 - y)]
    npieces = len(sc.pieces)
    filled = sum(rows for _, rows in sc.pieces)

    def pieces(d, buf, k):
        for p, (row, rows) in enumerate(sc.pieces):
            dst = buf.at[k] if (row, rows) == (0, rr) else buf.at[k, pl.ds(row, rows), :]
            yield k * npieces + p, sc.locate(g_hbm, d, row, rows), dst

    def loads():
        return [pltpu.make_async_copy(src, dst, ld_sems.at[s])
                for k in range(4) for s, src, dst in pieces(2 * k + c, own_ref, k)]

    def stage1():
        return [pltpu.make_async_remote_copy(
            src_ref=src, dst_ref=dst, send_sem=s1_send.at[s], recv_sem=s1_recv.at[s],
            device_id=(x, y, 1 - c), device_id_type=MESH)
            for k in range(4) for s, src, dst in pieces(2 * k + (1 - c), recv1_ref, k)]

    def stage2():
        return [pltpu.make_async_remote_copy(
            src_ref=sendb_ref.at[r], dst_ref=recv2_ref.at[r], send_sem=s2_send.at[r],
            recv_sem=s2_recv.at[r], device_id=(cx, cy, c), device_id_type=MESH) for r, (cx, cy) in enumerate(chips)]

    def start():
        if filled < rr:
            own_ref[:, filled:rr, :] = jnp.zeros((4, rr - filled, ww), F32)
            recv1_ref[:, filled:rr, :] = jnp.zeros((4, rr - filled, ww), F32)
        for cp in loads() + stage1():
            cp.start()

    def forward():
        for cp in loads():
            cp.wait()
        for cp in stage1():
            cp.wait_recv()
        sends = stage2()
        for r, (cx, cy) in enumerate(chips):
            kk = 2 * cx + cy

            def pack(i, carry, r=r, kk=kk):
                rows = pl.ds(pl.multiple_of(i * chunk, chunk), chunk)
                sendb_ref[r, rows, :] = (own_ref[kk, rows, :] + recv1_ref[kk, rows, :]).astype(BF16)
                return carry

            lax.fori_loop(0, rr // chunk, pack, 0)
            sends[r].start()

    def finish():
        for cp in stage2():
            cp.wait_recv()
        mine = 2 * x + y

        def total(i, carry):
            rows = pl.ds(pl.multiple_of(i * chunk, chunk), chunk)
            acc = own_ref[mine, rows, :] + recv1_ref[mine, rows, :]
            for r in range(3):
                acc = acc + recv2_ref[r, rows, :].astype(F32)
            out_ref[rows, :] = acc
            return carry

        lax.fori_loop(0, rr // chunk, total, 0)
        for cp in stage1() + stage2():
            cp.wait_send()

    return start, forward, finish


def _fwd_in(x2, w1, gq, gkv, wq, wkv, pos, rope_rows, shard, tm):
    tt = x2.shape[0]
    steps = tt // tm
    gathered_shape = (N_DEV * shard.shape[0], shard.shape[1])

    def body(x_ref, w1_ref, gq_ref, gkv_ref, wq_ref, wkv_ref, pos_ref, rr_ref, shard_ref,
             xq_ref, xkv_ref, ga_ref, u_ref, gb_ref, qn_ref, kvn_ref, qcat_ref, kcat_ref, v_ref,
             c_ref, sa_ref, sb_ref, all_ref, gathered, *sems):
        step = pl.program_id(0)
        start, forward, finish = _ag_phases(shard_ref, gathered, *sems)
        pl.when(step == 0)(start)
        pl.when(step == min(2, steps - 1))(forward)

        @pl.when(step == steps - 1)
        def _():
            finish()
            all_ref[...] = gathered[...]

        ang = pos_ref[...].astype(F32) * rr_ref[0:1, :]
        cos, sin = jnp.cos(ang), jnp.sin(ang)
        c, sa, sb = cos * rr_ref[1:2, :], sin * rr_ref[2:3, :], sin * rr_ref[3:4, :]
        c_ref[...] = c
        sa_ref[...] = sa
        sb_ref[...] = sb
        h = _dot_nt(x_ref[...].astype(BF16), w1_ref[...])
        xq = h[:, C_XQ:C_XKV]
        xkv = h[:, C_XKV:C_KR]
        xq_ref[...] = xq
        xkv_ref[...] = xkv
        ga_ref[...] = h[:, C_GA:C_U]
        u_ref[...] = h[:, C_U:C_GB]
        gb_ref[...] = h[:, C_GB:IN_WP]
        qn =(xq * lax.rsqrt(jnp.mean(xq * xq, axis=-1, keepdims=True) + RMS_EPS) * gq_ref[...]).astype(BF16)
        kvn = (xkv * lax.rsqrt(jnp.mean(xkv * xkv, axis=-1, keepdims=True) + RMS_EPS) * gkv_ref[...]).astype(BF16)
        qn_ref[...] = qn
        kvn_ref[...] = kvn
        q = _dot_nt(qn, wq_ref[...]) * SCALE_LOG2
        kv = _dot_nt(kvn, wkv_ref[...])
        kr = _rope_fwd(h[:, C_KR:C_GA], c, sa, sb).astype(BF16)
        for hd in range(HEADS):
            lo = hd * HEAD_PAD
            qcat_ref[:, lo:lo + NOPE] = q[:, lo:lo + NOPE].astype(BF16)
            qcat_ref[:, lo + NOPE:lo + HEAD_PAD] = _rope_fwd(q[:, lo + NOPE:lo + HEAD_PAD], c, sa, sb).astype(BF16)
            kcat_ref[:, lo:lo + NOPE] = kv[:, hd * NOPE:(hd + 1) * NOPE].astype(BF16)
            kcat_ref[:, lo + NOPE:lo + HEAD_PAD] = kr
        v_ref[...] = kv[:, MLA_W:].astype(BF16)

    def tile(w):
        return pl.BlockSpec((tm, w), lambda i: (i, 0))

    def whole(a):
        return pl.BlockSpec(a.shape, lambda i: (0, 0))

    outs = [(Q_LORA, F32), (KV_LORA, F32), (MLA_W, F32), (POOL_W, F32), (POOL_W, F32),
            (Q_LORA, BF16), (KV_LORA, BF16), (HEADS * HEAD_PAD, BF16), (HEADS * HEAD_PAD, BF16), (MLA_W, BF16),
            (128, F32), (128, F32), (128, F32)]
    return pl.pallas_call(
        body, name="fwd_in", grid=(steps,),
        in_specs=[tile(D_MODEL), whole(w1), whole(gq), whole(gkv), whole(wq), whole(wkv), tile(1), whole(rope_rows),
                  pl.BlockSpec(memory_space=pl.ANY)],
        out_specs=[tile(w) for w, _ in outs] + [pl.BlockSpec(gathered_shape, lambda i: (0, 0))],
        out_shape=[jax.ShapeDtypeStruct((tt, w), dt) for w, dt in outs]
        + [jax.ShapeDtypeStruct(gathered_shape, shard.dtype)],
        scratch_shapes=[pltpu.VMEM(gathered_shape, shard.dtype)] + AG_SEMS,
        compiler_params=_params(56, ("arbitrary",)))(x2, w1, gq, gkv, wq, wkv, pos, rope_rows, shard)


def _tile_cases(pmax_ref, pmin_ref, b, nhalf, qi, ki):
    q0, q1 = b * nhalf + 2 * qi, b * nhalf + 2 * qi + 1
    k0, k1 = b * nhalf + 2 * ki, b * nhalf + 2 * ki + 1
    needed = jnp.maximum(pmax_ref[q0], pmax_ref[q1]) >= jnp.minimum(pmin_ref[k0], pmin_ref[k1])
    visible = jnp.minimum(pmin_ref[q0], pmin_ref[q1]) >= jnp.maximum(pmax_ref[k0], pmax_ref[k1])
    stepped = pmax_ref[q0] < pmin_ref[k1]
    return needed, visible, stepped


def _attn_fwd(qcat, kcat, v, pos_col, pos_blk, pmax, pmin, tq, tk):
    bb, ss, _ = qcat.shape
    nq, nk = ss // tq, ss // tk
    lanes = 128
    hps = ATTN_FWD_HEADS_PER_STEP

    def body(pmax_ref, pmin_ref, q_ref, k_ref, v_ref, pc_ref, pb_ref, o_ref, lse_ref, m_s, acc_s):
        b, qi = pl.program_id(0), pl.program_id(2)
        m_s[...] = jnp.full(m_s.shape, NEG, F32)
        acc_s[...] = jnp.zeros(acc_s.shape, F32)

        def part(ki, masked, q_lo, q_n, k_n):
            qrows = slice(q_lo, q_lo + q_n)
            krows = pl.ds(pl.multiple_of(ki * tk, tk), k_n)
            if masked:
                mask = pc_ref[qrows, :] >= pb_ref[ki][:, :k_n]
            ones = jnp.ones((k_n, lanes), BF16)
            for hd in range(hps):
                qk = slice(hd * HEAD_PAD, (hd + 1) * HEAD_PAD)
                s = _dot_nt(q_ref[qrows, qk], k_ref[krows, qk])
                if masked:
                    s = jnp.where(mask, s, NEG)
                m_prev = m_s[hd, qrows, :]
                m_new = jnp.maximum(m_prev, jnp.max(s, axis=-1, keepdims=True))
                p = jnp.exp2(s - jnp.tile(m_new, (1, k_n // lanes)))
                a = jnp.exp2(m_prev - m_new)
                vv = jnp.concatenate([v_ref[krows, hd * V_DIM:(hd + 1) * V_DIM], ones], axis=1)
                acc_s[hd, qrows, :] = jnp.tile(a, (1, 2)) * acc_s[hd, qrows, :] + _dot(p.astype(BF16), vv)
                m_s[hd, qrows, :] = m_new

        def step(ki, carry):
            needed, visible, stepped = _tile_cases(pmax_ref, pmin_ref, b, 2 * nq, qi, ki)

            @pl.when(needed & visible)
            def _():
                part(ki, False, 0, tq, tk)

            @pl.when(needed & ~visible & stepped)
            def _():
                part(ki, True, 0, tq // 2, tk // 2)
                part(ki, True, tq // 2, tq // 2, tk)

            @pl.when(needed & ~visible & ~stepped)
            def _():
                part(ki, True, 0, tq, tk)

            return carry

        lax.fori_loop(0, nk, step, 0)
        for hd in range(hps):
            acc = acc_s[hd]
            l = acc[:, V_DIM:]
            o_ref[:, hd * V_DIM:(hd + 1) * V_DIM] = acc[:, :V_DIM] / l
            lse_ref[hd] = (m_s[hd] + jnp.log2(l)).T[0:1, :]

    grid_spec = pltpu.PrefetchScalarGridSpec(
        num_scalar_prefetch=2, grid=(bb, HEADS // hps, nq),
        in_specs=[
            pl.BlockSpec((None, tq, hps * HEAD_PAD), lambda b, h, i, *_: (b, i, h)),
            pl.BlockSpec((None, ss, hps * HEAD_PAD), lambda b, h, i, *_: (b, 0, h)),
            pl.BlockSpec((None, ss, hps * V_DIM), lambda b, h, i, *_: (b, 0, h)),
            pl.BlockSpec((None, tq, 1), lambda b, h, i, *_: (b, i, 0)),
            pl.BlockSpec((None, nk, 1, tk), lambda b, h, i, *_: (b, 0, 0, 0)),
        ],
        out_specs=[
            pl.BlockSpec((None, tq, hps * V_DIM), lambda b, h, i, *_: (b, i, h)),
            pl.BlockSpec((None, hps, 1, tq), lambda b, h, i, *_: (b, h, 0, i)),
        ],
        scratch_shapes=[pltpu.VMEM((hps, tq, lanes), F32), pltpu.VMEM((hps, tq, 2 * V_DIM), F32)])
    return pl.pallas_call(
        body, name="attn_fwd", grid_spec=grid_spec,
        out_shape=[jax.ShapeDtypeStruct((bb, ss, MLA_W), F32), jax.ShapeDtypeStruct((bb, HEADS, 1, ss), F32)],
        compiler_params=_params(48, ("arbitrary", "arbitrary", "arbitrary")))(pmax, pmin, qcat, kcat, v, pos_col, pos_blk)


def _post(x, tgt, o, ga, u, gb, w_out, pool_wb, pool_scale, ln_g, ln_b, ts):
    bb, ss, _ = x.shape
    nt = ss // ts
    hb = ts // POOL_HALO

    def body(x_ref, tgt_ref, o_ref, ga_ref, u_ref, uh_ref, gb_ref, wo_ref, pw_ref, ps_ref, lg_ref, lb_ref,
             dz_ref, ycat_ref, do_ref, dga_ref, dgb_ref, dpc_ref, dl_ref, loss_ref, glg_ref, glb_ref, gps_ref, gpw_ref):
        step = pl.program_id(0)
        j = step % nt

        @pl.when(step == 0)
        def _():
            for r in (loss_ref, glg_ref, glb_ref, gps_ref, gpw_ref):
                r[...] = jnp.zeros(r.shape, F32)

        o, ga, u, gb = o_ref[...], ga_ref[...], u_ref[...], gb_ref[...]
        sa, dsa = _silu_parts(ga)
        sb, dsb = _silu_parts(gb)
        ext = jnp.concatenate([jnp.where(j > 0, uh_ref[...], 0.0), u], axis=0)
        s2 = ext + pltpu.roll(ext, 1, 0)
        s4 = s2 + pltpu.roll(s2, 2, 0)
        s8 = s4 + pltpu.roll(s4, 4, 0)
        s16 = s8 + pltpu.roll(s8, 8, 0)
        cnt = _pool_cnt(j * ts, ts)
        rcnt = 1.0 / cnt
        pooled = (_pick_groups(s2, s4, s8, s16, POOL_HALO, POOL_HALO + ts) * rcnt - u).astype(BF16)
        mixed = jnp.concatenate(
            [_dot(pooled[:, g * POOL_C:(g + 1) * POOL_C], pw_ref[g]) for g in range(POOL_G)], axis=1)
        ps = ps_ref[...]
        scaled = mixed * ps
        ycat = jnp.concatenate([o * sa, scaled * sb], axis=1).astype(BF16)
        ycat_ref[...] = ycat
        z = ALPHA * x_ref[...] + _dot(ycat, wo_ref[...])
        mu = jnp.mean(z, axis=-1, keepdims=True)
        zc = z - mu
        rstd = lax.rsqrt(jnp.mean(zc * zc, axis=-1, keepdims=True) + LN_EPS)
        xhat = zc * rstd
        lg = lg_ref[...]
        diff = xhat * lg + lb_ref[...] - tgt_ref[...]
        loss_ref[...] += jnp.sum(diff * diff) * (0.5 / D_MODEL)
        glb_ref[...] += jnp.sum(diff, axis=0, keepdims=True) * (1.0 / D_MODEL)
        glg_ref[...] += jnp.sum(diff * xhat, axis=0, keepdims=True) * (1.0 / D_MODEL)
        dxh = diff * (lg * (1.0 / D_MODEL))
        dz = rstd * (dxh - jnp.mean(dxh, axis=-1, keepdims=True) - xhat * jnp.mean(dxh * xhat, axis=-1, keepdims=True))
        dz_ref[...] = dz
        dycat = _dot_nt(dz.astype(BF16), wo_ref[...])
        dya, dyb = dycat[:, :MLA_W], dycat[:, MLA_W:]
        do = dya * sa
        do_ref[...] = do.astype(BF16)
        doo = do * o
        for hd in range(HEADS):
            dl_ref[hd] = jnp.sum(doo[:, hd * V_DIM:(hd + 1) * V_DIM].T, axis=0, keepdims=True)
        dga_ref[...] = (dya * o * dsa).astype(BF16)
        dgb_ref[...] = (dyb * scaled * dsb).astype(BF16)
        dscaled = dyb * sb
        gps_ref[...] += jnp.sum(dscaled * mixed, axis=0, keepdims=True)
        dmixed = (dscaled * ps).astype(BF16)
        dpooled = []
        for g in range(POOL_G):
            cols = slice(g * POOL_C, (g + 1) * POOL_C)
            gpw_ref[g] += _dot_tn(pooled[:, cols], dmixed[:, cols])
            dpooled.append(_dot_nt(dmixed[:, cols], pw_ref[g]))
        dpc_ref[...] = jnp.concatenate(dpooled, axis=1) * rcnt

    def tile(w):
        return pl.BlockSpec((None, ts, w), lambda i: (i // nt, i % nt, 0))

    def whole(a):
        nd = a.ndim
        return pl.BlockSpec(a.shape, lambda i: (0,) * nd)

    halo = pl.BlockSpec((None, POOL_HALO, POOL_W), lambda i: (i // nt, jnp.maximum((i % nt) * hb - 1, 0), 0))
    acc_shapes = [(8, 128), (1, D_MODEL), (1, D_MODEL), (1, POOL_W), (POOL_G, POOL_C, POOL_C)]
    tile_outs = [(D_MODEL, F32), (D_MODEL, BF16), (MLA_W, BF16), (MLA_W, BF16), (POOL_W, BF16), (POOL_W, F32)]
    return pl.pallas_call(
        body, name="post", grid=(bb * nt,),
        in_specs=[tile(D_MODEL), tile(D_MODEL), tile(MLA_W), tile(MLA_W), tile(POOL_W), halo, tile(POOL_W),
                  whole(w_out), whole(pool_wb), whole(pool_scale), whole(ln_g), whole(ln_b)],
        out_specs=[tile(w) for w, _ in tile_outs]
        + [pl.BlockSpec((None, HEADS, 1, ts), lambda i: (i // nt, 0, 0, i % nt))]
        + [pl.BlockSpec(s, lambda i, n=len(s): (0,) * n) for s in acc_shapes],
        out_shape=[jax.ShapeDtypeStruct((bb, ss, w), dt) for w, dt in tile_outs]
        + [jax.ShapeDtypeStruct((bb, HEADS, 1, ss), F32)]
        + [jax.ShapeDtypeStruct(s, F32) for s in acc_shapes],
        compiler_params=_params(48, ("arbitrary",)))(x, tgt, o, ga, u, u, gb, w_out, pool_wb, pool_scale, ln_g, ln_b)


def _attn_bwd(qcat, kcat, v, do, lse, dl, rc, rsa, rsb, pos_col, pos_blk, pmax, pmin, tq, tk):
    bb, ss, _ = qcat.shape
    nq, nk = ss // tq, ss // tk
    hps = ATTN_BWD_HEADS_PER_STEP

    def body(pmax_ref, pmin_ref, q_ref, k_ref, v_ref, do_ref, lse_ref, dl_ref, c_ref, sa_ref, sb_ref, pc_ref, pb_ref,
             dqp_ref, dkn_ref, dv_ref, dkr_ref, dq_s, dk_s, dv_s):
        b = pl.program_id(0)
        dq_s[...] = jnp.zeros(dq_s.shape, F32)

        def part(qi, ki, masked):
            krows = pl.ds(pl.multiple_of(ki * tk, tk), tk)
            qrows = pl.ds(pl.multiple_of(qi * tq, tq), tq)
            if masked:
                mask = pb_ref[qi] >= pc_ref[krows, :]
            for hd in range(hps):
                qk = slice(hd * HEAD_PAD, (hd + 1) * HEAD_PAD)
                vs = slice(hd * V_DIM, (hd + 1) * V_DIM)
                q = q_ref[qrows, qk]
                dd = do_ref[qrows, vs]
                st = _dot_nt(k_ref[krows, qk], q)
                if masked:
                    st = jnp.where(mask, st, NEG)
                pt = jnp.exp2(st - lse_ref[hd, qi])
                dv_s[:, vs] += _dot(pt.astype(BF16), dd)
                dpt = _dot_nt(v_ref[krows, vs], dd)
                dst = (pt * (dpt - dl_ref[hd, qi])).astype(BF16)
                dk_s[:, qk] += _dot(dst, q)
                dq_s[qrows, qk] += _dot_tn(dst, k_ref[krows, qk])

        def kv_step(ki, carry):
            krows = pl.ds(pl.multiple_of(ki * tk, tk), tk)
            dk_s[...] = jnp.zeros(dk_s.shape, F32)
            dv_s[...] = jnp.zeros(dv_s.shape, F32)

            def q_step(qi, c2):
                needed, visible, _ = _tile_cases(pmax_ref, pmin_ref, b, 2 * nq, qi, ki)

                @pl.when(needed & visible)
                def _():
                    part(qi, ki, False)

                @pl.when(needed & ~visible)
                def _():
                    part(qi, ki, True)

                return c2

            lax.fori_loop(0, nq, q_step, 0)
            dkr = jnp.zeros((tk, HEAD_PAD - NOPE), F32)
            for hd in range(hps):
                lo = hd * HEAD_PAD
                dkn_ref[krows, hd * NOPE:(hd + 1) * NOPE] = (dk_s[:, lo:lo + NOPE] * LN2).astype(BF16)
                dkr = dkr + dk_s[:, lo + NOPE:lo + HEAD_PAD]
            dkr_ref[krows, :] = dkr * LN2
            dv_ref[krows, :] = dv_s[...].astype(BF16)
            return carry

        lax.fori_loop(0, nk, kv_step, 0)
        c, sa, sb = c_ref[...], sa_ref[...], sb_ref[...]
        for hd in range(hps):
            lo = hd * HEAD_PAD
            dqp_ref[:, lo:lo + NOPE] = (dq_s[:, lo:lo + NOPE] * SCALE).astype(BF16)
            dqp_ref[:, lo + NOPE:lo + HEAD_PAD] = _rope_bwd(
                dq_s[:, lo + NOPE:lo + HEAD_PAD] * SCALE, c, sa, sb).astype(BF16)

    def per_head(w):
        return pl.BlockSpec((None, ss, hps * w), lambda b, h, *_: (b, 0, h))

    def rows_of_head():
        return pl.BlockSpec((None, hps, nq, 1, tq), lambda b, h, *_: (b, h, 0, 0, 0))

    def per_batch(w):
        return pl.BlockSpec((None, ss, w), lambda b, h, *_: (b, 0, 0))

    grid_spec = pltpu.PrefetchScalarGridSpec(
        num_scalar_prefetch=2, grid=(bb, HEADS // hps),
        in_specs=[per_head(HEAD_PAD), per_head(HEAD_PAD), per_head(V_DIM), per_head(V_DIM),
                  rows_of_head(), rows_of_head(), per_batch(128), per_batch(128), per_batch(128), per_batch(1),
                  pl.BlockSpec((None, nq, 1, tq), lambda b, h, *_: (b, 0, 0, 0))],
        out_specs=[per_head(HEAD_PAD), per_head(NOPE), per_head(V_DIM),
                   pl.BlockSpec((None, None, ss, HEAD_PAD - NOPE), lambda b, h, *_: (b, h, 0, 0))],
        scratch_shapes=[pltpu.VMEM((ss, hps * HEAD_PAD), F32), pltpu.VMEM((tk, hps * HEAD_PAD), F32),
                        pltpu.VMEM((tk, hps * V_DIM), F32)])
    return pl.pallas_call(
        body, name="attn_bwd", grid_spec=grid_spec,
        out_shape=[jax.ShapeDtypeStruct((bb, ss, HEADS * HEAD_PAD), BF16),
                   jax.ShapeDtypeStruct((bb, ss, MLA_W), BF16),
                   jax.ShapeDtypeStruct((bb, ss, MLA_W), BF16),
                   jax.ShapeDtypeStruct((bb, HEADS // hps, ss, HEAD_PAD - NOPE), F32)],
        compiler_params=_params(56, ("arbitrary", "arbitrary")))(
            pmax, pmin, qcat, kcat, v, do, lse, dl, rc, rsa, rsb, pos_col, pos_blk)


def _bwd_mid(dqp, dkn, dv, dkr, xq, xkv, rc, rsa, rsb, wq, wkv, gq, gkv, dga, dgb, dpc, dz, w1, ts):
    bb, ss, _ = dz.shape
    nt = ss // ts
    hb = ts // POOL_HALO
    groups = dkr.shape[1]

    def body(dqp_ref, dkn_ref, dv_ref, dkr_ref, xq_ref, xkv_ref, c_ref, sa_ref, sb_ref, wq_ref, wkv_ref, gq_ref,
             gkv_ref, dga_ref, dgb_ref, dpc_ref, dph_ref, dz_ref, w1_ref,
             gx_ref, dh_ref, ggq_ref, ggkv_ref):
        step = pl.program_id(0)
        j = step % nt

        @pl.when(step == 0)
        def _():
            ggq_ref[...] = jnp.zeros(ggq_ref.shape, F32)
            ggkv_ref[...] = jnp.zeros(ggkv_ref.shape, F32)

        dkr = dkr_ref[0]
        for g in range(1, groups):
            dkr = dkr + dkr_ref[g]
        dh_ref[:, C_KR:C_GA] = _rope_bwd(dkr, c_ref[...], sa_ref[...], sb_ref[...]).astype(BF16)

        def rms_bwd(x, g, dn):
            inv = lax.rsqrt(jnp.mean(x * x, axis=-1, keepdims=True) + RMS_EPS)
            xh = x * inv
            dxh = dn * g
            return inv * (dxh - xh * jnp.mean(dxh * xh, axis=-1, keepdims=True)), jnp.sum(dn * xh, axis=0, keepdims=True)

        dxq, ggq = rms_bwd(xq_ref[...], gq_ref[...], _dot(dqp_ref[...], wq_ref[...]))
        dkvn = _dot(dkn_ref[...], wkv_ref[:MLA_W, :]) + _dot(dv_ref[...], wkv_ref[MLA_W:, :])
        dxkv, ggkv = rms_bwd(xkv_ref[...], gkv_ref[...], dkvn)
        ggq_ref[...] += ggq
        ggkv_ref[...] += ggkv
        dh_ref[:, C_XQ:C_XKV] = dxq.astype(BF16)
        dh_ref[:, C_XKV:C_KR] = dxkv.astype(BF16)
        dh_ref[:, C_GA:C_U] = dga_ref[...]
        dh_ref[:, C_GB:IN_WP] = dgb_ref[...]
        dpc = dpc_ref[...]
        n = ts + POOL_HALO
        ext = jnp.concatenate([dpc, jnp.where(j < nt - 1, dph_ref[...], 0.0)], axis=0)
        r2 = ext + pltpu.roll(ext, n - 1, 0)
        r4 = r2 + pltpu.roll(r2, n - 2, 0)
        r8 = r4 + pltpu.roll(r4, n - 4, 0)
        r16 = r8 + pltpu.roll(r8, n - 8, 0)
        du = _pick_groups(r2, r4, r8, r16, 0, ts) - dpc * _pool_cnt(j * ts, ts)
        dh_ref[:, C_U:C_GB] = du.astype(BF16)
        gx_ref[...] = ALPHA * dz_ref[...] + _dot(dh_ref[...], w1_ref[...])

    def tile(w):
        return pl.BlockSpec((None, ts, w), lambda i: (i // nt, i % nt, 0))

    def whole(a):
        return pl.BlockSpec(a.shape, lambda i: (0, 0))

    halo = pl.BlockSpec((None, POOL_HALO, POOL_W),
                        lambda i: (i // nt, jnp.minimum((i % nt + 1) * hb, ss // POOL_HALO - 1), 0))
    return pl.pallas_call(
        body, name="bwd_mid", grid=(bb * nt,),
        in_specs=[tile(HEADS * HEAD_PAD), tile(MLA_W), tile(MLA_W),
                  pl.BlockSpec((None, groups, ts, HEAD_PAD - NOPE), lambda i: (i // nt, 0, i % nt, 0)),
                  tile(Q_LORA), tile(KV_LORA),
                  tile(128), tile(128), tile(128), whole(wq), whole(wkv), whole(gq), whole(gkv),
                  tile(MLA_W), tile(POOL_W), tile(POOL_W), halo, tile(D_MODEL), whole(w1)],
        out_specs=[tile(D_MODEL), tile(IN_WP),
                   pl.BlockSpec((1, Q_LORA), lambda i: (0, 0)), pl.BlockSpec((1, KV_LORA), lambda i: (0, 0))],
        out_shape=[jax.ShapeDtypeStruct((bb, ss, D_MODEL), F32), jax.ShapeDtypeStruct((bb, ss, IN_WP), BF16),
                   jax.ShapeDtypeStruct((1, Q_LORA), F32), jax.ShapeDtypeStruct((1, KV_LORA), F32)],
        compiler_params=_params(48, ("arbitrary",)))(
            dqp, dkn, dv, dkr, xq, xkv, rc, rsa, rsb, wq, wkv, gq, gkv, dga, dgb, dpc, dpc, dz, w1)


def _grad_w(pairs, bt, name, b_cols=None, carried=None, reduced=None):
    tt = pairs[0][0].shape[0]
    steps = tt // bt
    npairs = len(pairs)
    n_rs = len(_rs_scratch(carried)) if carried is not None else 0

    def body(*refs):
        ab, rest = refs[:2 * npairs], list(refs[2 * npairs:])
        g_hbm = rest.pop(0) if carried is not None else None
        part = rest.pop(0) if reduced is not None else None
        outs = [rest.pop(0) for _ in range(npairs)]
        phases = []
        if carried is not None:
            rs_out = rest.pop(0)
        if reduced is not None:
            sum_ref = rest.pop(0)
        if carried is not None:
            phases.append(_rs_phases(carried, g_hbm, rs_out, *rest[:n_rs]))
        if reduced is not None:
            gathered, *sems = rest[n_rs:]
            phases.append(_ag_phases(part, gathered, *sems, sum_ref=sum_ref))
        step = pl.program_id(0)

        @pl.when(step == 0)
        def _():
            for o in outs:
                o[...] = jnp.zeros(o.shape, F32)
            for start, _, _ in phases:
                start()

        for i in range(npairs):
            outs[i][...] += _dot_tn(ab[2 * i][...].astype(BF16), ab[2 * i + 1][...].astype(BF16))

        @pl.when(step == min(2, steps - 1))
        def _():
            for _, forward, _ in phases:
                forward()

        @pl.when(step == steps - 1)
        def _():
            for _, _, finish in phases:
                finish()

    in_specs, out_specs, out_shape = [], [], []
    for a, b in pairs:
        m, (n, col) = a.shape[1], b_cols or (b.shape[1], 0)
        in_specs += [pl.BlockSpec((bt, m), lambda i: (i, 0)), pl.BlockSpec((bt, n), lambda i, col=col: (i, col))]
        out_specs.append(pl.BlockSpec((m, n), lambda i: (0, 0)))
        out_shape.append(jax.ShapeDtypeStruct((m, n), F32))
    args = [t for p in pairs for t in p]
    scratch = []
    if carried is not None:
        in_specs.append(pl.BlockSpec(memory_space=pl.ANY))
        out_specs.append(pl.BlockSpec((carried.rows, carried.width), lambda i: (0, 0)))
        out_shape.append(jax.ShapeDtypeStruct((carried.rows, carried.width), F32))
        args.append(carried.array)
        scratch += _rs_scratch(carried)
    if reduced is not None:
        in_specs.append(pl.BlockSpec(memory_space=pl.ANY))
        out_specs.append(pl.BlockSpec(reduced.shape, lambda i: (0, 0)))
        out_shape.append(jax.ShapeDtypeStruct(reduced.shape, F32))
        args.append(reduced)
        scratch += [pltpu.VMEM((N_DEV * reduced.shape[0], reduced.shape[1]), F32)] + AG_SEMS
    return pl.pallas_call(
        body, name=name, grid=(steps,), in_specs=in_specs, out_specs=out_specs, out_shape=out_shape,
        scratch_shapes=scratch, compiler_params=_params(56, ("arbitrary",)))(*args)


def _adamw(triples):
    n = len(triples)

    def body(*refs):
        ins, outs = refs[:4 * n], refs[4 * n:]
        for i in range(n):
            w, g, m, v = (r[...] for r in ins[4 * i:4 * i + 4])
            m = ADAM_B1 * m + (1.0 - ADAM_B1) * g
            v = ADAM_B2 * v + (1.0 - ADAM_B2) * jnp.square(g)
            m_hat = m / (1.0 - ADAM_B1 ** ADAM_STEP)
            v_hat = v / (1.0 - ADAM_B2 ** ADAM_STEP)
            outs[3 * i][...] = -ADAM_LR * (m_hat / (jnp.sqrt(v_hat) + ADAM_EPS) + ADAM_WD * w)
            outs[3 * i + 1][...] = m
            outs[3 * i + 2][...] = v

    flat = [a for t in triples for a in t]
    vmem = pl.BlockSpec(memory_space=pltpu.VMEM)
    res = pl.pallas_call(
        body, name="adamw", in_specs=[vmem] * len(flat), out_specs=[vmem] * (3 * n),
        out_shape=[jax.ShapeDtypeStruct(t[0].shape, F32) for t in triples for _ in range(3)],
        compiler_params=_params(48))(*flat)
    return [tuple(res[3 * i:3 * i + 3]) for i in range(n)]


def _shard_slab(w_in, w_uq, w_ukv):
    mixed = jnp.concatenate(
        [_pad_rows(w_uq.T, R_MIX), w_ukv.T, jnp.zeros((R_MIX, 1024 - Q_LORA - KV_LORA), F32)], axis=1)
    return jnp.concatenate([mixed, w_in.T, jnp.zeros((SLAB_ROWS - O_IN - R_IN, 1024), F32)], axis=0)


def _unpack_weights(slabs):
    uq = slabs[:, O_MIX:O_MIX + R_UQ, :Q_LORA].reshape(HEADS, QK_DIM, Q_LORA)
    wqt = jnp.pad(uq, ((0, 0), (0, HEAD_PAD - QK_DIM), (0, 0))).reshape(HEADS * HEAD_PAD, Q_LORA)
    ukv = slabs[:, O_MIX:O_MIX + R_MIX, Q_LORA:Q_LORA + KV_LORA].reshape(HEADS, 2, NOPE, KV_LORA)
    wkvt = ukv.transpose(1, 0, 2, 3).reshape(2 * MLA_W, KV_LORA)
    raw = slabs[:, O_IN:O_IN + R_IN].reshape(IN_W, D_MODEL)
    w1t = jnp.concatenate([raw[:768 + ROPE], jnp.zeros((128 - ROPE, D_MODEL), BF16), raw[768 + ROPE:]], axis=0)
    return w1t, wqt, wkvt


def _mixed_band(g_wqt, g_wkvt):
    uq = g_wqt.reshape(HEADS, HEAD_PAD, Q_LORA)[:, :QK_DIM].reshape(N_DEV, R_UQ, Q_LORA)
    uq = jnp.pad(uq, ((0, 0), (0, R_MIX - R_UQ), (0, 0)))
    ukv = g_wkvt.reshape(2, HEADS, NOPE, KV_LORA).transpose(1, 0, 2, 3).reshape(N_DEV, R_MIX, KV_LORA)
    return jnp.concatenate([uq, ukv, jnp.zeros((N_DEV, R_MIX, 1024 - Q_LORA - KV_LORA), F32)], axis=2)


def _rope_rows():
    half = ROPE // 2
    inv_freq = ROPE_THETA ** (-jnp.arange(half, dtype=F32) / half)
    zero, one = jnp.zeros((half,), F32), jnp.ones((half,), F32)
    rows = [jnp.concatenate(r) for r in (
        (inv_freq, inv_freq, zero, zero), (one, one, zero, zero), (-one, zero, zero, zero), (zero, one, zero, zero))]
    return jnp.stack(rows + [jnp.zeros((128,), F32)] * 4)


def _pad_rows(a, rows):
    return jnp.pad(a, ((0, rows - a.shape[0]), (0, 0)))


def kernel(x, positions, w_in, q_norm_g, w_uq, kv_norm_g, w_ukv, pool_w, pool_scale, w_out, ln_g, ln_b, loss_target, m_w_in, m_q_norm_g, m_w_uq, m_kv_norm_g, m_w_ukv, m_pool_w, m_pool_scale, m_w_out, m_ln_g, m_ln_b, v_w_in, v_q_norm_g, v_w_uq, v_kv_norm_g, v_w_ukv, v_pool_w, v_pool_scale, v_w_out, v_ln_g, v_ln_b):
    bb, ss, _ = x.shape
    tt = bb * ss
    atile = min(512, ss)
    nblk = ss // atile

    slab = _shard_slab(w_in, w_uq, w_ukv).astype(BF16)
    slabs = _all_gather(slab, "gather_weights").reshape(N_DEV, SLAB_ROWS, 1024)
    w1, wq, wkv = _unpack_weights(slabs)

    gq, gkv = q_norm_g.reshape(1, Q_LORA), kv_norm_g.reshape(1, KV_LORA)
    ps = pool_scale.reshape(1, POOL_W)
    pos_col = positions.reshape(bb, ss, 1)
    pos_blk = positions.reshape(bb, nblk, 1, atile)
    pmax = jnp.max(positions.reshape(bb, 2 * nblk, atile // 2), axis=-1).reshape(-1)
    pmin = jnp.min(positions.reshape(bb, 2 * nblk, atile // 2), axis=-1).reshape(-1)

    x2 = x.reshape(tt, D_MODEL)
    xq, xkv, ga, u, gb, qn, kvn, qcat, kcat, v, rc, rsa, rsb, wo = _fwd_in(
        x2, w1, gq, gkv, wq, wkv, pos_col.reshape(tt, 1), _rope_rows(), w_out.astype(BF16), atile)
    as3 = lambda a: a.reshape(bb, ss, a.shape[-1])
    qcat, kcat, v = as3(qcat), as3(kcat), as3(v)
    o, lse = _attn_fwd(qcat, kcat, v, pos_col, pos_blk, pmax, pmin, atile, atile)
    dz, ycat, do, dga, dgb, dpc, dl, loss_p, g_lng, g_lnb, g_ps, g_pw = _post(
        x, loss_target, o, as3(ga), as3(u), as3(gb), wo, pool_w.astype(BF16), ps, ln_g, ln_b, atile)

    bt = min(1024, tt)
    rows5 = lambda a: a.reshape(bb, HEADS, nblk, 1, atile)
    rc, rsa, rsb = as3(rc), as3(rsa), as3(rsb)
    dqp, dkn, dv, dkr = _attn_bwd(
        qcat, kcat, v, do, rows5(lse), rows5(dl), rc, rsa, rsb, pos_col, pos_blk, pmax, pmin, atile, atile)
    g_wq, g_wkn, g_wv = _grad_w(
        [(dqp.reshape(tt, HEADS * HEAD_PAD), qn), (dkn.reshape(tt, MLA_W), kvn), (dv.reshape(tt, MLA_W), kvn)], bt,
        "grad_w_uqkv")
    g_wkv = jnp.concatenate([g_wkn, g_wv], axis=0)
    grad_x, dh, g_gq, g_gkv = _bwd_mid(
        dqp, dkn, dv, dkr, as3(xq), as3(xkv), rc, rsa, rsb, wq, wkv, gq, gkv, dga, dgb, dpc, dz, w1, atile)
    small = jnp.concatenate(
        [_pad_rows(g_lng.reshape(8, 128), 8), _pad_rows(g_lnb.reshape(8, 128), 8), _pad_rows(g_gq.reshape(4, 128), 8),
         _pad_rows(g_gkv.reshape(2, 128), 8), _pad_rows(g_ps.reshape(4, 128), 8), g_pw.reshape(POOL_G * POOL_C, 128),
         loss_p], axis=0)
    dh2, half = dh.reshape(tt, IN_WP), D_MODEL // 2
    g_w1a, rs_mix, small = _grad_w([(dh2, x2)], bt, "grad_w_in_a", b_cols=(half, 0),
                                   carried=_blocks(_mixed_band(g_wq, g_wkv)), reduced=small)
    g_w1b, rs_in_a = _grad_w([(dh2, x2)], bt, "grad_w_in_b", b_cols=(half, 1), carried=_w_in_blocks(g_w1a))
    g_wo, rs_in_b = _grad_w([(ycat.reshape(tt, D_MODEL), dz.reshape(tt, D_MODEL))], bt, "grad_w_out",
                            carried=_w_in_blocks(g_w1b))
    rs_in = jnp.concatenate([rs_in_a, rs_in_b], axis=1)
    rs_out = _reduce_scatter(_blocks(g_wo.reshape(N_DEV, R_OUT, D_MODEL)), "reduce_scatter_w_out")
    loss = small[40 + POOL_G * POOL_C, 0]
    grads = {
        "w_in": rs_in[:R_IN],
        "q_norm_g": small[16:20].reshape(1, Q_LORA),
        "w_uq": rs_mix[:R_UQ, :Q_LORA],
        "kv_norm_g": small[24:26].reshape(1, KV_LORA),
        "w_ukv": rs_mix[:, Q_LORA:Q_LORA + KV_LORA].T,
        "pool_w": small[40:40 + POOL_G * POOL_C],
        "pool_scale": small[32:36].reshape(1, POOL_W),
        "w_out": rs_out,
        "ln_g": small[0:8].reshape(1, D_MODEL),
        "ln_b": small[8:16].reshape(1, D_MODEL),
    }
    transposed = ("w_in", "w_uq")

    names = ["w_in", "q_norm_g", "w_uq", "kv_norm_g", "w_ukv", "pool_w", "pool_scale", "w_out", "ln_g", "ln_b"]
    weights = dict(w_in=w_in, q_norm_g=q_norm_g, w_uq=w_uq, kv_norm_g=kv_norm_g, w_ukv=w_ukv, pool_w=pool_w,
                   pool_scale=pool_scale, w_out=w_out, ln_g=ln_g, ln_b=ln_b)
    moms = dict(w_in=(m_w_in, v_w_in), q_norm_g=(m_q_norm_g, v_q_norm_g), w_uq=(m_w_uq, v_w_uq),
                kv_norm_g=(m_kv_norm_g, v_kv_norm_g), w_ukv=(m_w_ukv, v_w_ukv), pool_w=(m_pool_w, v_pool_w),
                pool_scale=(m_pool_scale, v_pool_scale), w_out=(m_w_out, v_w_out), ln_g=(m_ln_g, v_ln_g),
                ln_b=(m_ln_b, v_ln_b))
    as2 = lambda a, n: a.T if n in transposed else a.reshape(grads[n].shape)
    upd = _adamw([(as2(weights[n], n), grads[n], as2(moms[n][0], n), as2(moms[n][1], n)) for n in names])
    shaped = lambda a, n: a.T if n in transposed else a.reshape(weights[n].shape)
    return (loss, grad_x,
            *[shaped(grads[n], n) for n in names],
            *[shaped(upd[i][0], n) for i, n in enumerate(names)],
            *[shaped(upd[i][1], n) for i, n in enumerate(names)],
            *[shaped(upd[i][2], n) for i, n in enumerate(names)])
```

```python
import jax
import jax.numpy as jnp
from jax import lax
from jax.experimental import pallas as pl
from jax.experimental.pallas import tpu as pltpu

F32 = jnp.float32
BF16 = jnp.bfloat16
MESH = pl.DeviceIdType.MESH

N_DEV = 8
D_MODEL = 1024
HEADS = 4
NOPE = 128
ROPE = 64
V_DIM = 128
QK_DIM = NOPE + ROPE
HEAD_PAD = 256
Q_LORA = 512
KV_LORA = 256
MLA_W = HEADS * V_DIM
POOL_W = 512
POOL_G = 4
POOL_C = 128
POOL_HALO = 16
IN_W = 2368
IN_WP = 2432
C_XQ, C_XKV, C_KR, C_GA, C_U, C_GB = 0, 512, 768, 896, 1408, 1920
ROPE_THETA = 10000.0
RMS_EPS = 1e-6
LN_EPS = 1e-5
ALPHA = 2.0 ** 0.25
SCALE = QK_DIM ** -0.5
LN2 = 0.6931471805599453
SCALE_LOG2 = SCALE / LN2
NEG = float(jnp.finfo(jnp.float32).min)

ADAM_LR = 0.001
ADAM_B1 = 0.9
ADAM_B2 = 0.999
ADAM_EPS = 1e-08
ADAM_WD = 0.01
ADAM_STEP = 10

R_OUT, R_MIX, R_UQ, R_IN = 128, 128, 96, 296
O_MIX, O_IN = 0, R_MIX
SLAB_ROWS = 432
R_IN_PAD = 304

V7X_VMEM_BYTES = 64 * 1024 * 1024
ATTN_FWD_HEADS_PER_STEP = 4
ATTN_BWD_HEADS_PER_STEP = 2


def _params(vmem_mb, semantics=None):
    assert vmem_mb * 1024 * 1024 < V7X_VMEM_BYTES
    return pltpu.CompilerParams(vmem_limit_bytes=vmem_mb * 1024 * 1024, dimension_semantics=semantics)


def _dot(a, b):
    return jnp.dot(a, b, preferred_element_type=F32)


def _dot_nt(a, b):
    return lax.dot_general(a, b, (((1,), (1,)), ((), ())), preferred_element_type=F32)


def _dot_tn(a, b):
    return lax.dot_general(a, b, (((0,), (0,)), ((), ())), preferred_element_type=F32)


def _rope_fwd(t, c, sa, sb):
    return t * c + pltpu.roll(t, 96, 1) * sa + pltpu.roll(t, 32, 1) * sb


def _rope_bwd(d, c, sa, sb):
    return d * c + pltpu.roll(d * sa, 32, 1) + pltpu.roll(d * sb, 96, 1)


def _silu_parts(g):
    sig = 0.5 * jnp.tanh(0.5 * g) + 0.5
    silu = g * sig
    return silu, sig + silu - silu * sig


def _pool_cnt(row0, rows):
    t = row0 + lax.broadcasted_iota(jnp.int32, (rows, POOL_W), 0)
    w = 2 << (lax.broadcasted_iota(jnp.int32, (rows, POOL_W), 1) // POOL_C)
    return jnp.minimum(t + 1, w).astype(F32)


def _pick_groups(s2, s4, s8, s16, lo, hi):
    return jnp.concatenate([s2[lo:hi, 0:128], s4[lo:hi, 128:256], s8[lo:hi, 256:384], s16[lo:hi, 384:512]], axis=1)


AG_SEMS = [pltpu.SemaphoreType.DMA((7,)), pltpu.SemaphoreType.DMA((7,)), pltpu.SemaphoreType.DMA]


def _ag_phases(x_ref, out_ref, send_sems, recv_sems, local_sem, sum_ref=None):
    m_per = x_ref.shape[0]
    x, y, c = lax.axis_index("x"), lax.axis_index("y"), lax.axis_index("c")
    me, sibling = (x, y, c), (x, y, 1 - c)
    chips = [(1 - x, y), (x, 1 - y), (1 - x, 1 - y)]

    def rows(px, py, pc):
        return out_ref.at[pl.ds((4 * px + 2 * py + pc) * m_per, m_per), :]

    def copy(k, block, to, src=None):
        return pltpu.make_async_remote_copy(
            src_ref=rows(*block) if src is None else src, dst_ref=rows(*block),
            send_sem=send_sems.at[k], recv_sem=recv_sems.at[k], device_id=to, device_id_type=MESH)

    def mine():
        return pltpu.make_async_copy(x_ref, rows(*me), local_sem)

    def first():
        return [copy(0, me, sibling, src=x_ref)] + [copy(1 + j, me, (*chip, c), src=x_ref) for j, chip in enumerate(chips)]

    def passed():
        return [copy(4 + j, (*chip, c), sibling) for j, chip in enumerate(chips)]

    def start():
        for cp in [mine()] + first():
            cp.start()

    def forward():
        for j, (chip, cp) in enumerate(zip(chips, passed())):
            copy(1 + j, (*chip, c), me).wait_recv()
            cp.start()

    def finish():
        copy(0, sibling, me).wait_recv()
        for j, chip in enumerate(chips):
            copy(4 + j, (*chip, 1 - c), me).wait_recv()
        for cp in first() + passed():
            cp.wait_send()
        mine().wait()
        if sum_ref is not None:
            acc = out_ref[pl.ds(0, m_per), :]
            for d in range(1, N_DEV):
                acc = acc + out_ref[pl.ds(d * m_per, m_per), :]
            sum_ref[...] = acc

    return start, forward, finish


def _all_gather(shard, name):
    m_per, n = shard.shape

    def body(x_ref, out_ref, *sems):
        for phase in _ag_phases(x_ref, out_ref, *sems):
            phase()

    vmem = pl.BlockSpec(memory_space=pltpu.VMEM)
    return pl.pallas_call(
        body, name=name, out_shape=jax.ShapeDtypeStruct((N_DEV * m_per, n), shard.dtype),
        in_specs=[vmem], out_specs=vmem, scratch_shapes=AG_SEMS, compiler_params=_params(32))(shard)


def _reduce_scatter(sc, name):
    def body(g_hbm, out_ref, *scratch):
        for phase in _rs_phases(sc, g_hbm, out_ref, *scratch):
            phase()

    return pl.pallas_call(
        body, name=name, out_shape=jax.ShapeDtypeStruct((sc.rows, sc.width), F32),
        in_specs=[pl.BlockSpec(memory_space=pl.ANY)], out_specs=pl.BlockSpec(memory_space=pltpu.VMEM),
        scratch_shapes=_rs_scratch(sc), compiler_params=_params(32))(sc.array)


class _Scattered:
    def __init__(self, array, rows, pieces, locate):
        self.array, self.rows, self.pieces, self.locate = array, rows, pieces, locate
        self.width = array.shape[-1]


def _blocks(g):
    return _Scattered(g, g.shape[1], ((0, g.shape[1]),), lambda ref, d, row, rows: ref.at[d])


def _w_in_blocks(g_w1t):
    def locate(ref, d, row, rows):
        r = R_IN * d + row
        return ref.at[pl.ds(pl.multiple_of(r + jnp.where(r >= C_KR + ROPE, 128 - ROPE, 0), 8), rows), :]

    cut = C_KR + ROPE - 2 * R_IN
    return _Scattered(g_w1t, R_IN_PAD, ((0, cut), (cut, R_IN - cut)), locate)


def _rs_scratch(sc):
    rr, ww, n = sc.rows, sc.width, 4 * len(sc.pieces)
    return [pltpu.VMEM((4, rr, ww), F32), pltpu.VMEM((4, rr, ww), F32),
            pltpu.VMEM((3, rr, ww), BF16), pltpu.VMEM((3, rr, ww), BF16),
            pltpu.SemaphoreType.DMA((n,)), pltpu.SemaphoreType.DMA((n,)), pltpu.SemaphoreType.DMA((n,)),
            pltpu.SemaphoreType.DMA((3,)), pltpu.SemaphoreType.DMA((3,))]


def _rs_phases(sc, g_hbm, out_ref, own_ref, recv1_ref, sendb_ref, recv2_ref, ld_sems, s1_send, s1_recv, s2_send, s2_recv):
    rr, ww = out_ref.shape
    chunk = next(c for c in (128, 80, 64, 48, 32, 16) if rr % c == 0)
    x, y, c = lax.axis_index("x"), lax.axis_index("y"), lax.axis_index("c")
    chips = [(1 - x, y), (x, 1 - y), (1 - x, 1 - y)]
    npieces = len(sc.pieces)
    filled = sum(rows for _, rows in sc.pieces)

    def pieces(d, buf, k):
        for p, (row, rows) in enumerate(sc.pieces):
            dst = buf.at[k] if (row, rows) == (0, rr) else buf.at[k, pl.ds(row, rows), :]
            yield k * npieces + p, sc.locate(g_hbm, d, row, rows), dst

    def loads():
        return [pltpu.make_async_copy(src, dst, ld_sems.at[s])
                for k in range(4) for s, src, dst in pieces(2 * k + c, own_ref, k)]

    def stage1():
        return [pltpu.make_async_remote_copy(
            src_ref=src, dst_ref=dst, send_sem=s1_send.at[s], recv_sem=s1_recv.at[s],
            device_id=(x, y, 1 - c), device_id_type=MESH)
            for k in range(4) for s, src, dst in pieces(2 * k + (1 - c), recv1_ref, k)]

    def stage2():
        return [pltpu.make_async_remote_copy(
            src_ref=sendb_ref.at[r], dst_ref=recv2_ref.at[r], send_sem=s2_send.at[r],
            recv_sem=s2_recv.at[r], device_id=(cx, cy, c), device_id_type=MESH) for r, (cx, cy) in enumerate(chips)]

    def start():
        if filled < rr:
            own_ref[:, filled:rr, :] = jnp.zeros((4, rr - filled, ww), F32)
            recv1_ref[:, filled:rr, :] = jnp.zeros((4, rr - filled, ww), F32)
        for cp in loads() + stage1():
            cp.start()

    def forward():
        for cp in loads():
            cp.wait()
        for cp in stage1():
            cp.wait_recv()
        sends = stage2()
        for r, (cx, cy) in enumerate(chips):
            kk = 2 * cx + cy

            def pack(i, carry, r=r, kk=kk):
                rows = pl.ds(pl.multiple_of(i * chunk, chunk), chunk)
                sendb_ref[r, rows, :] = (own_ref[kk, rows, :] + recv1_ref[kk, rows, :]).astype(BF16)
                return carry

            lax.fori_loop(0, rr // chunk, pack, 0)
            sends[r].start()

    def finish():
        for cp in stage2():
            cp.wait_recv()
        mine = 2 * x + y

        def total(i, carry):
            rows = pl.ds(pl.multiple_of(i * chunk, chunk), chunk)
            acc = own_ref[mine, rows, :] + recv1_ref[mine, rows, :]
            for r in range(3):
                acc = acc + recv2_ref[r, rows, :].astype(F32)
            out_ref[rows, :] = acc
            return carry

        lax.fori_loop(0, rr // chunk, total, 0)
        for cp in stage1() + stage2():
            cp.wait_send()

    return start, forward, finish


def _fwd_in(x2, w1, gq, gkv, wq, wkv, pos, rope_rows, shard, tm):
    tt = x2.shape[0]
    steps = tt // tm
    gathered_shape = (N_DEV * shard.shape[0], shard.shape[1])

    def body(x_ref, w1_ref, gq_ref, gkv_ref, wq_ref, wkv_ref, pos_ref, rr_ref, shard_ref,
             xq_ref, xkv_ref, ga_ref, u_ref, gb_ref, qn_ref, kvn_ref, qcat_ref, kcat_ref, v_ref,
             c_ref, sa_ref, sb_ref, all_ref, gathered, *sems):
        step = pl.program_id(0)
        start, forward, finish = _ag_phases(shard_ref, gathered, *sems)
        pl.when(step == 0)(start)
        pl.when(step == min(2, steps - 1))(forward)

        @pl.when(step == steps - 1)
        def _():
            finish()
            all_ref[...] = gathered[...]

        ang = pos_ref[...].astype(F32) * rr_ref[0:1, :]
        cos, sin = jnp.cos(ang), jnp.sin(ang)
        c, sa, sb = cos * rr_ref[1:2, :], sin * rr_ref[2:3, :], sin * rr_ref[3:4, :]
        c_ref[...] = c
        sa_ref[...] = sa
        sb_ref[...] = sb
        h = _dot_nt(x_ref[...].astype(BF16), w1_ref[...])
        xq = h[:, C_XQ:C_XKV]
        xkv = h[:, C_XKV:C_KR]
        xq_ref[...] = xq
        xkv_ref[...] = xkv
        ga_ref[...] = h[:, C_GA:C_U]
        u_ref[...] = h[:, C_U:C_GB]
        gb_ref[...] = h[:, C_GB:IN_WP]
        qn =(xq * lax.rsqrt(jnp.mean(xq * xq, axis=-1, keepdims=True) + RMS_EPS) * gq_ref[...]).astype(BF16)
        kvn = (xkv * lax.rsqrt(jnp.mean(xkv * xkv, axis=-1, keepdims=True) + RMS_EPS) * gkv_ref[...]).astype(BF16)
        qn_ref[...] = qn
        kvn_ref[...] = kvn
        q = _dot_nt(qn, wq_ref[...]) * SCALE_LOG2
        kv = _dot_nt(kvn, wkv_ref[...])
        kr = _rope_fwd(h[:, C_KR:C_GA], c, sa, sb).astype(BF16)
        for hd in range(HEADS):
            lo = hd * HEAD_PAD
            qcat_ref[:, lo:lo + NOPE] = q[:, lo:lo + NOPE].astype(BF16)
            qcat_ref[:, lo + NOPE:lo + HEAD_PAD] = _rope_fwd(q[:, lo + NOPE:lo + HEAD_PAD], c, sa, sb).astype(BF16)
            kcat_ref[:, lo:lo + NOPE] = kv[:, hd * NOPE:(hd + 1) * NOPE].astype(BF16)
            kcat_ref[:, lo + NOPE:lo + HEAD_PAD] = kr
        v_ref[...] = kv[:, MLA_W:].astype(BF16)

    def tile(w):
        return pl.BlockSpec((tm, w), lambda i: (i, 0))

    def whole(a):
        return pl.BlockSpec(a.shape, lambda i: (0, 0))

    outs = [(Q_LORA, F32), (KV_LORA, F32), (MLA_W, F32), (POOL_W, F32), (POOL_W, F32),
            (Q_LORA, BF16), (KV_LORA, BF16), (HEADS * HEAD_PAD, BF16), (HEADS * HEAD_PAD, BF16), (MLA_W, BF16),
            (128, F32), (128, F32), (128, F32)]
    return pl.pallas_call(
        body, name="fwd_in", grid=(steps,),
        in_specs=[tile(D_MODEL), whole(w1), whole(gq), whole(gkv), whole(wq), whole(wkv), tile(1), whole(rope_rows),
                  pl.BlockSpec(memory_space=pl.ANY)],
        out_specs=[tile(w) for w, _ in outs] + [pl.BlockSpec(gathered_shape, lambda i: (0, 0))],
        out_shape=[jax.ShapeDtypeStruct((tt, w), dt) for w, dt in outs]
        + [jax.ShapeDtypeStruct(gathered_shape, shard.dtype)],
        scratch_shapes=[pltpu.VMEM(gathered_shape, shard.dtype)] + AG_SEMS,
        compiler_params=_params(56, ("arbitrary",)))(x2, w1, gq, gkv, wq, wkv, pos, rope_rows, shard)


def _tile_cases(pmax_ref, pmin_ref, b, nhalf, qi, ki):
    q0, q1 = b * nhalf + 2 * qi, b * nhalf + 2 * qi + 1
    k0, k1 = b * nhalf + 2 * ki, b * nhalf + 2 * ki + 1
    needed = jnp.maximum(pmax_ref[q0], pmax_ref[q1]) >= jnp.minimum(pmin_ref[k0], pmin_ref[k1])
    visible = jnp.minimum(pmin_ref[q0], pmin_ref[q1]) >= jnp.maximum(pmax_ref[k0], pmax_ref[k1])
    stepped = pmax_ref[q0] < pmin_ref[k1]
    return needed, visible, stepped


def _attn_fwd(qcat, kcat, v, pos_col, pos_blk, pmax, pmin, tq, tk):
    bb, ss, _ = qcat.shape
    nq, nk = ss // tq, ss // tk
    lanes = 128
    hps = ATTN_FWD_HEADS_PER_STEP

    def body(pmax_ref, pmin_ref, q_ref, k_ref, v_ref, pc_ref, pb_ref, o_ref, lse_ref, m_s, acc_s):
        b, qi = pl.program_id(0), pl.program_id(2)
        m_s[...] = jnp.full(m_s.shape, NEG, F32)
        acc_s[...] = jnp.zeros(acc_s.shape, F32)

        def part(ki, masked, q_lo, q_n, k_n):
            qrows = slice(q_lo, q_lo + q_n)
            krows = pl.ds(pl.multiple_of(ki * tk, tk), k_n)
            if masked:
                mask = pc_ref[qrows, :] >= pb_ref[ki][:, :k_n]
            ones = jnp.ones((k_n, lanes), BF16)
            for hd in range(hps):
                qk = slice(hd * HEAD_PAD, (hd + 1) * HEAD_PAD)
                s = _dot_nt(q_ref[qrows, qk], k_ref[krows, qk])
                if masked:
                    s = jnp.where(mask, s, NEG)
                m_prev = m_s[hd, qrows, :]
                m_new = jnp.maximum(m_prev, jnp.max(s, axis=-1, keepdims=True))
                p = jnp.exp2(s - jnp.tile(m_new, (1, k_n // lanes)))
                a = jnp.exp2(m_prev - m_new)
                vv = jnp.concatenate([v_ref[krows, hd * V_DIM:(hd + 1) * V_DIM], ones], axis=1)
                acc_s[hd, qrows, :] = jnp.tile(a, (1, 2)) * acc_s[hd, qrows, :] + _dot(p.astype(BF16), vv)
                m_s[hd, qrows, :] = m_new

        def step(ki, carry):
            needed, visible, stepped = _tile_cases(pmax_ref, pmin_ref, b, 2 * nq, qi, ki)

            @pl.when(needed & visible)
            def _():
                part(ki, False, 0, tq, tk)

            @pl.when(needed & ~visible & stepped)
            def _():
                part(ki, True, 0, tq // 2, tk // 2)
                part(ki, True, tq // 2, tq // 2, tk)

            @pl.when(needed & ~visible & ~stepped)
            def _():
                part(ki, True, 0, tq, tk)

            return carry

        lax.fori_loop(0, nk, step, 0)
        for hd in range(hps):
            acc = acc_s[hd]
            l = acc[:, V_DIM:]
            o_ref[:, hd * V_DIM:(hd + 1) * V_DIM] = acc[:, :V_DIM] / l
            lse_ref[hd] = (m_s[hd] + jnp.log2(l)).T[0:1, :]

    grid_spec = pltpu.PrefetchScalarGridSpec(
        num_scalar_prefetch=2, grid=(bb, HEADS // hps, nq),
        in_specs=[
            pl.BlockSpec((None, tq, hps * HEAD_PAD), lambda b, h, i, *_: (b, i, h)),
            pl.BlockSpec((None, ss, hps * HEAD_PAD), lambda b, h, i, *_: (b, 0, h)),
            pl.BlockSpec((None, ss, hps * V_DIM), lambda b, h, i, *_: (b, 0, h)),
            pl.BlockSpec((None, tq, 1), lambda b, h, i, *_: (b, i, 0)),
            pl.BlockSpec((None, nk, 1, tk), lambda b, h, i, *_: (b, 0, 0, 0)),
        ],
        out_specs=[
            pl.BlockSpec((None, tq, hps * V_DIM), lambda b, h, i, *_: (b, i, h)),
            pl.BlockSpec((None, hps, 1, tq), lambda b, h, i, *_: (b, h, 0, i)),
        ],
        scratch_shapes=[pltpu.VMEM((hps, tq, lanes), F32), pltpu.VMEM((hps, tq, 2 * V_DIM), F32)])
    return pl.pallas_call(
        body, name="attn_fwd", grid_spec=grid_spec,
        out_shape=[jax.ShapeDtypeStruct((bb, ss, MLA_W), F32), jax.ShapeDtypeStruct((bb, HEADS, 1, ss), F32)],
        compiler_params=_params(48, ("arbitrary", "arbitrary", "arbitrary")))(pmax, pmin, qcat, kcat, v, pos_col, pos_blk)


def _post(x, tgt, o, ga, u, gb, w_out, pool_wb, pool_scale, ln_g, ln_b, ts):
    bb, ss, _ = x.shape
    nt = ss // ts
    hb = ts // POOL_HALO

    def body(x_ref, tgt_ref, o_ref, ga_ref, u_ref, uh_ref, gb_ref, wo_ref, pw_ref, ps_ref, lg_ref, lb_ref,
             dz_ref, ycat_ref, do_ref, dga_ref, dgb_ref, dpc_ref, dl_ref, loss_ref, glg_ref, glb_ref, gps_ref, gpw_ref):
        step = pl.program_id(0)
        j = step % nt

        @pl.when(step == 0)
        def _():
            for r in (loss_ref, glg_ref, glb_ref, gps_ref, gpw_ref):
                r[...] = jnp.zeros(r.shape, F32)

        o, ga, u, gb = o_ref[...], ga_ref[...], u_ref[...], gb_ref[...]
        sa, dsa = _silu_parts(ga)
        sb, dsb = _silu_parts(gb)
        ext = jnp.concatenate([jnp.where(j > 0, uh_ref[...], 0.0), u], axis=0)
        s2 = ext + pltpu.roll(ext, 1, 0)
        s4 = s2 + pltpu.roll(s2, 2, 0)
        s8 = s4 + pltpu.roll(s4, 4, 0)
        s16 = s8 + pltpu.roll(s8, 8, 0)
        cnt = _pool_cnt(j * ts, ts)
        rcnt = 1.0 / cnt
        pooled = (_pick_groups(s2, s4, s8, s16, POOL_HALO, POOL_HALO + ts) * rcnt - u).astype(BF16)
        mixed = jnp.concatenate(
            [_dot(pooled[:, g * POOL_C:(g + 1) * POOL_C], pw_ref[g]) for g in range(POOL_G)], axis=1)
        ps = ps_ref[...]
        scaled = mixed * ps
        ycat = jnp.concatenate([o * sa, scaled * sb], axis=1).astype(BF16)
        ycat_ref[...] = ycat
        z = ALPHA * x_ref[...] + _dot(ycat, wo_ref[...])
        mu = jnp.mean(z, axis=-1, keepdims=True)
        zc = z - mu
        rstd = lax.rsqrt(jnp.mean(zc * zc, axis=-1, keepdims=True) + LN_EPS)
        xhat = zc * rstd
        lg = lg_ref[...]
        diff = xhat * lg + lb_ref[...] - tgt_ref[...]
        loss_ref[...] += jnp.sum(diff * diff) * (0.5 / D_MODEL)
        glb_ref[...] += jnp.sum(diff, axis=0, keepdims=True) * (1.0 / D_MODEL)
        glg_ref[...] += jnp.sum(diff * xhat, axis=0, keepdims=True) * (1.0 / D_MODEL)
        dxh = diff * (lg * (1.0 / D_MODEL))
        dz = rstd * (dxh - jnp.mean(dxh, axis=-1, keepdims=True) - xhat * jnp.mean(dxh * xhat, axis=-1, keepdims=True))
        dz_ref[...] = dz
        dycat = _dot_nt(dz.astype(BF16), wo_ref[...])
        dya, dyb = dycat[:, :MLA_W], dycat[:, MLA_W:]
        do = dya * sa
        do_ref[...] = do.astype(BF16)
        doo = do * o
        for hd in range(HEADS):
            dl_ref[hd] = jnp.sum(doo[:, hd * V_DIM:(hd + 1) * V_DIM].T, axis=0, keepdims=True)
        dga_ref[...] = (dya * o * dsa).astype(BF16)
        dgb_ref[...] = (dyb * scaled * dsb).astype(BF16)
        dscaled = dyb * sb
        gps_ref[...] += jnp.sum(dscaled * mixed, axis=0, keepdims=True)
        dmixed = (dscaled * ps).astype(BF16)
        dpooled = []
        for g in range(POOL_G):
            cols = slice(g * POOL_C, (g + 1) * POOL_C)
            gpw_ref[g] += _dot_tn(pooled[:, cols], dmixed[:, cols])
            dpooled.append(_dot_nt(dmixed[:, cols], pw_ref[g]))
        dpc_ref[...] = jnp.concatenate(dpooled, axis=1) * rcnt

    def tile(w):
        return pl.BlockSpec((None, ts, w), lambda i: (i // nt, i % nt, 0))

    def whole(a):
        nd = a.ndim
        return pl.BlockSpec(a.shape, lambda i: (0,) * nd)

    halo = pl.BlockSpec((None, POOL_HALO, POOL_W), lambda i: (i // nt, jnp.maximum((i % nt) * hb - 1, 0), 0))
    acc_shapes = [(8, 128), (1, D_MODEL), (1, D_MODEL), (1, POOL_W), (POOL_G, POOL_C, POOL_C)]
    tile_outs = [(D_MODEL, F32), (D_MODEL, BF16), (MLA_W, BF16), (MLA_W, BF16), (POOL_W, BF16), (POOL_W, F32)]
    return pl.pallas_call(
        body, name="post", grid=(bb * nt,),
        in_specs=[tile(D_MODEL), tile(D_MODEL), tile(MLA_W), tile(MLA_W), tile(POOL_W), halo, tile(POOL_W),
                  whole(w_out), whole(pool_wb), whole(pool_scale), whole(ln_g), whole(ln_b)],
        out_specs=[tile(w) for w, _ in tile_outs]
        + [pl.BlockSpec((None, HEADS, 1, ts), lambda i: (i // nt, 0, 0, i % nt))]
        + [pl.BlockSpec(s, lambda i, n=len(s): (0,) * n) for s in acc_shapes],
        out_shape=[jax.ShapeDtypeStruct((bb, ss, w), dt) for w, dt in tile_outs]
        + [jax.ShapeDtypeStruct((bb, HEADS, 1, ss), F32)]
        + [jax.ShapeDtypeStruct(s, F32) for s in acc_shapes],
        compiler_params=_params(48, ("arbitrary",)))(x, tgt, o, ga, u, u, gb, w_out, pool_wb, pool_scale, ln_g, ln_b)


def _attn_bwd(qcat, kcat, v, do, lse, dl, rc, rsa, rsb, pos_col, pos_blk, pmax, pmin, tq, tk):
    bb, ss, _ = qcat.shape
    nq, nk = ss // tq, ss // tk
    hps = ATTN_BWD_HEADS_PER_STEP

    def body(pmax_ref, pmin_ref, q_ref, k_ref, v_ref, do_ref, lse_ref, dl_ref, c_ref, sa_ref, sb_ref, pc_ref, pb_ref,
             dqp_ref, dkn_ref, dv_ref, dkr_ref, dq_s, dk_s, dv_s):
        b = pl.program_id(0)
        dq_s[...] = jnp.zeros(dq_s.shape, F32)

        def part(qi, ki, masked):
            krows = pl.ds(pl.multiple_of(ki * tk, tk), tk)
            qrows = pl.ds(pl.multiple_of(qi * tq, tq), tq)
            if masked:
                mask = pb_ref[qi] >= pc_ref[krows, :]
            for hd in range(hps):
                qk = slice(hd * HEAD_PAD, (hd + 1) * HEAD_PAD)
                vs = slice(hd * V_DIM, (hd + 1) * V_DIM)
                q = q_ref[qrows, qk]
                dd = do_ref[qrows, vs]
                st = _dot_nt(k_ref[krows, qk], q)
                if masked:
                    st = jnp.where(mask, st, NEG)
                pt = jnp.exp2(st - lse_ref[hd, qi])
                dv_s[:, vs] += _dot(pt.astype(BF16), dd)
                dpt = _dot_nt(v_ref[krows, vs], dd)
                dst = (pt * (dpt - dl_ref[hd, qi])).astype(BF16)
                dk_s[:, qk] += _dot(dst, q)
                dq_s[qrows, qk] += _dot_tn(dst, k_ref[krows, qk])

        def kv_step(ki, carry):
            krows = pl.ds(pl.multiple_of(ki * tk, tk), tk)
            dk_s[...] = jnp.zeros(dk_s.shape, F32)
            dv_s[...] = jnp.zeros(dv_s.shape, F32)

            def q_step(qi, c2):
                needed, visible, _ = _tile_cases(pmax_ref, pmin_ref, b, 2 * nq, qi, ki)

                @pl.when(needed & visible)
                def _():
                    part(qi, ki, False)

                @pl.when(needed & ~visible)
                def _():
                    part(qi, ki, True)

                return c2

            lax.fori_loop(0, nq, q_step, 0)
            dkr = jnp.zeros((tk, HEAD_PAD - NOPE), F32)
            for hd in range(hps):
                lo = hd * HEAD_PAD
                dkn_ref[krows, hd * NOPE:(hd + 1) * NOPE] = (dk_s[:, lo:lo + NOPE] * LN2).astype(BF16)
                dkr = dkr + dk_s[:, lo + NOPE:lo + HEAD_PAD]
            dkr_ref[krows, :] = dkr * LN2
            dv_ref[krows, :] = dv_s[...].astype(BF16)
            return carry

        lax.fori_loop(0, nk, kv_step, 0)
        c, sa, sb = c_ref[...], sa_ref[...], sb_ref[...]
        for hd in range(hps):
            lo = hd * HEAD_PAD
            dqp_ref[:, lo:lo + NOPE] = (dq_s[:, lo:lo + NOPE] * SCALE).astype(BF16)
            dqp_ref[:, lo + NOPE:lo + HEAD_PAD] = _rope_bwd(
                dq_s[:, lo + NOPE:lo + HEAD_PAD] * SCALE, c, sa, sb).astype(BF16)

    def per_head(w):
        return pl.BlockSpec((None, ss, hps * w), lambda b, h, *_: (b, 0, h))

    def rows_of_head():
        return pl.BlockSpec((None, hps, nq, 1, tq), lambda b, h, *_: (b, h, 0, 0, 0))

    def per_batch(w):
        return pl.BlockSpec((None, ss, w), lambda b, h, *_: (b, 0, 0))

    grid_spec = pltpu.PrefetchScalarGridSpec(
        num_scalar_prefetch=2, grid=(bb, HEADS // hps),
        in_specs=[per_head(HEAD_PAD), per_head(HEAD_PAD), per_head(V_DIM), per_head(V_DIM),
                  rows_of_head(), rows_of_head(), per_batch(128), per_batch(128), per_batch(128), per_batch(1),
                  pl.BlockSpec((None, nq, 1, tq), lambda b, h, *_: (b, 0, 0, 0))],
        out_specs=[per_head(HEAD_PAD), per_head(NOPE), per_head(V_DIM),
                   pl.BlockSpec((None, None, ss, HEAD_PAD - NOPE), lambda b, h, *_: (b, h, 0, 0))],
        scratch_shapes=[pltpu.VMEM((ss, hps * HEAD_PAD), F32), pltpu.VMEM((tk, hps * HEAD_PAD), F32),
                        pltpu.VMEM((tk, hps * V_DIM), F32)])
    return pl.pallas_call(
        body, name="attn_bwd", grid_spec=grid_spec,
        out_shape=[jax.ShapeDtypeStruct((bb, ss, HEADS * HEAD_PAD), BF16),
                   jax.ShapeDtypeStruct((bb, ss, MLA_W), BF16),
                   jax.ShapeDtypeStruct((bb, ss, MLA_W), BF16),
                   jax.ShapeDtypeStruct((bb, HEADS // hps, ss, HEAD_PAD - NOPE), F32)],
        compiler_params=_params(56, ("arbitrary", "arbitrary")))(
            pmax, pmin, qcat, kcat, v, do, lse, dl, rc, rsa, rsb, pos_col, pos_blk)


def _bwd_mid(dqp, dkn, dv, dkr, xq, xkv, rc, rsa, rsb, wq, wkv, gq, gkv, dga, dgb, dpc, dz, w1, ts):
    bb, ss, _ = dz.shape
    nt = ss // ts
    hb = ts // POOL_HALO
    groups = dkr.shape[1]

    def body(dqp_ref, dkn_ref, dv_ref, dkr_ref, xq_ref, xkv_ref, c_ref, sa_ref, sb_ref, wq_ref, wkv_ref, gq_ref,
             gkv_ref, dga_ref, dgb_ref, dpc_ref, dph_ref, dz_ref, w1_ref,
             gx_ref, dh_ref, ggq_ref, ggkv_ref):
        step = pl.program_id(0)
        j = step % nt

        @pl.when(step == 0)
        def _():
            ggq_ref[...] = jnp.zeros(ggq_ref.shape, F32)
            ggkv_ref[...] = jnp.zeros(ggkv_ref.shape, F32)

        dkr = dkr_ref[0]
        for g in range(1, groups):
            dkr = dkr + dkr_ref[g]
        dh_ref[:, C_KR:C_GA] = _rope_bwd(dkr, c_ref[...], sa_ref[...], sb_ref[...]).astype(BF16)

        def rms_bwd(x, g, dn):
            inv = lax.rsqrt(jnp.mean(x * x, axis=-1, keepdims=True) + RMS_EPS)
            xh = x * inv
            dxh = dn * g
            return inv * (dxh - xh * jnp.mean(dxh * xh, axis=-1, keepdims=True)), jnp.sum(dn * xh, axis=0, keepdims=True)

        dxq, ggq = rms_bwd(xq_ref[...], gq_ref[...], _dot(dqp_ref[...], wq_ref[...]))
        dkvn = _dot(dkn_ref[...], wkv_ref[:MLA_W, :]) + _dot(dv_ref[...], wkv_ref[MLA_W:, :])
        dxkv, ggkv = rms_bwd(xkv_ref[...], gkv_ref[...], dkvn)
        ggq_ref[...] += ggq
        ggkv_ref[...] += ggkv
        dh_ref[:, C_XQ:C_XKV] = dxq.astype(BF16)
        dh_ref[:, C_XKV:C_KR] = dxkv.astype(BF16)
        dh_ref[:, C_GA:C_U] = dga_ref[...]
        dh_ref[:, C_GB:IN_WP] = dgb_ref[...]
        dpc = dpc_ref[...]
        n = ts + POOL_HALO
        ext = jnp.concatenate([dpc, jnp.where(j < nt - 1, dph_ref[...], 0.0)], axis=0)
        r2 = ext + pltpu.roll(ext, n - 1, 0)
        r4 = r2 + pltpu.roll(r2, n - 2, 0)
        r8 = r4 + pltpu.roll(r4, n - 4, 0)
        r16 = r8 + pltpu.roll(r8, n - 8, 0)
        du = _pick_groups(r2, r4, r8, r16, 0, ts) - dpc * _pool_cnt(j * ts, ts)
        dh_ref[:, C_U:C_GB] = du.astype(BF16)
        gx_ref[...] = ALPHA * dz_ref[...] + _dot(dh_ref[...], w1_ref[...])

    def tile(w):
        return pl.BlockSpec((None, ts, w), lambda i: (i // nt, i % nt, 0))

    def whole(a):
        return pl.BlockSpec(a.shape, lambda i: (0, 0))

    halo = pl.BlockSpec((None, POOL_HALO, POOL_W),
                        lambda i: (i // nt, jnp.minimum((i % nt + 1) * hb, ss // POOL_HALO - 1), 0))
    return pl.pallas_call(
        body, name="bwd_mid", grid=(bb * nt,),
        in_specs=[tile(HEADS * HEAD_PAD), tile(MLA_W), tile(MLA_W),
                  pl.BlockSpec((None, groups, ts, HEAD_PAD - NOPE), lambda i: (i // nt, 0, i % nt, 0)),
                  tile(Q_LORA), tile(KV_LORA),
                  tile(128), tile(128), tile(128), whole(wq), whole(wkv), whole(gq), whole(gkv),
                  tile(MLA_W), tile(POOL_W), tile(POOL_W), halo, tile(D_MODEL), whole(w1)],
        out_specs=[tile(D_MODEL), tile(IN_WP),
                   pl.BlockSpec((1, Q_LORA), lambda i: (0, 0)), pl.BlockSpec((1, KV_LORA), lambda i: (0, 0))],
        out_shape=[jax.ShapeDtypeStruct((bb, ss, D_MODEL), F32), jax.ShapeDtypeStruct((bb, ss, IN_WP), BF16),
                   jax.ShapeDtypeStruct((1, Q_LORA), F32), jax.ShapeDtypeStruct((1, KV_LORA), F32)],
        compiler_params=_params(48, ("arbitrary",)))(
            dqp, dkn, dv, dkr, xq, xkv, rc, rsa, rsb, wq, wkv, gq, gkv, dga, dgb, dpc, dpc, dz, w1)


def _grad_w(pairs, bt, name, b_cols=None, carried=None, reduced=None):
    tt = pairs[0][0].shape[0]
    steps = tt // bt
    npairs = len(pairs)
    n_rs = len(_rs_scratch(carried)) if carried is not None else 0

    def body(*refs):
        ab, rest = refs[:2 * npairs], list(refs[2 * npairs:])
        g_hbm = rest.pop(0) if carried is not None else None
        part = rest.pop(0) if reduced is not None else None
        outs = [rest.pop(0) for _ in range(npairs)]
        phases = []
        if carried is not None:
            rs_out = rest.pop(0)
        if reduced is not None:
            sum_ref = rest.pop(0)
        if carried is not None:
            phases.append(_rs_phases(carried, g_hbm, rs_out, *rest[:n_rs]))
        if reduced is not None:
            gathered, *sems = rest[n_rs:]
            phases.append(_ag_phases(part, gathered, *sems, sum_ref=sum_ref))
        step = pl.program_id(0)

        @pl.when(step == 0)
        def _():
            for o in outs:
                o[...] = jnp.zeros(o.shape, F32)
            for start, _, _ in phases:
                start()

        for i in range(npairs):
            outs[i][...] += _dot_tn(ab[2 * i][...].astype(BF16), ab[2 * i + 1][...].astype(BF16))

        @pl.when(step == min(2, steps - 1))
        def _():
            for _, forward, _ in phases:
                forward()

        @pl.when(step == steps - 1)
        def _():
            for _, _, finish in phases:
                finish()

    in_specs, out_specs, out_shape = [], [], []
    for a, b in pairs:
        m, (n, col) = a.shape[1], b_cols or (b.shape[1], 0)
        in_specs += [pl.BlockSpec((bt, m), lambda i: (i, 0)), pl.BlockSpec((bt, n), lambda i, col=col: (i, col))]
        out_specs.append(pl.BlockSpec((m, n), lambda i: (0, 0)))
        out_shape.append(jax.ShapeDtypeStruct((m, n), F32))
    args = [t for p in pairs for t in p]
    scratch = []
    if carried is not None:
        in_specs.append(pl.BlockSpec(memory_space=pl.ANY))
        out_specs.append(pl.BlockSpec((carried.rows, carried.width), lambda i: (0, 0)))
        out_shape.append(jax.ShapeDtypeStruct((carried.rows, carried.width), F32))
        args.append(carried.array)
        scratch += _rs_scratch(carried)
    if reduced is not None:
        in_specs.append(pl.BlockSpec(memory_space=pl.ANY))
        out_specs.append(pl.BlockSpec(reduced.shape, lambda i: (0, 0)))
        out_shape.append(jax.ShapeDtypeStruct(reduced.shape, F32))
        args.append(reduced)
        scratch += [pltpu.VMEM((N_DEV * reduced.shape[0], reduced.shape[1]), F32)] + AG_SEMS
    return pl.pallas_call(
        body, name=name, grid=(steps,), in_specs=in_specs, out_specs=out_specs, out_shape=out_shape,
        scratch_shapes=scratch, compiler_params=_params(56, ("arbitrary",)))(*args)


def _adamw(triples):
    n = len(triples)

    def body(*refs):
        ins, outs = refs[:4 * n], refs[4 * n:]
        for i in range(n):
            w, g, m, v = (r[...] for r in ins[4 * i:4 * i + 4])
            m = ADAM_B1 * m + (1.0 - ADAM_B1) * g
            v = ADAM_B2 * v + (1.0 - ADAM_B2) * jnp.square(g)
            m_hat = m / (1.0 - ADAM_B1 ** ADAM_STEP)
            v_hat = v / (1.0 - ADAM_B2 ** ADAM_STEP)
            outs[3 * i][...] = -ADAM_LR * (m_hat / (jnp.sqrt(v_hat) + ADAM_EPS) + ADAM_WD * w)
            outs[3 * i + 1][...] = m
            outs[3 * i + 2][...] = v

    flat = [a for t in triples for a in t]
    vmem = pl.BlockSpec(memory_space=pltpu.VMEM)
    res = pl.pallas_call(
        body, name="adamw", in_specs=[vmem] * len(flat), out_specs=[vmem] * (3 * n),
        out_shape=[jax.ShapeDtypeStruct(t[0].shape, F32) for t in triples for _ in range(3)],
        compiler_params=_params(48))(*flat)
    return [tuple(res[3 * i:3 * i + 3]) for i in range(n)]


def _shard_slab(w_in, w_uq, w_ukv):
    mixed = jnp.concatenate(
        [_pad_rows(w_uq.T, R_MIX), w_ukv.T, jnp.zeros((R_MIX, 1024 - Q_LORA - KV_LORA), F32)], axis=1)
    return jnp.concatenate([mixed, w_in.T, jnp.zeros((SLAB_ROWS - O_IN - R_IN, 1024), F32)], axis=0)


def _unpack_weights(slabs):
    uq = slabs[:, O_MIX:O_MIX + R_UQ, :Q_LORA].reshape(HEADS, QK_DIM, Q_LORA)
    wqt = jnp.pad(uq, ((0, 0), (0, HEAD_PAD - QK_DIM), (0, 0))).reshape(HEADS * HEAD_PAD, Q_LORA)
    ukv = slabs[:, O_MIX:O_MIX + R_MIX, Q_LORA:Q_LORA + KV_LORA].reshape(HEADS, 2, NOPE, KV_LORA)
    wkvt = ukv.transpose(1, 0, 2, 3).reshape(2 * MLA_W, KV_LORA)
    raw = slabs[:, O_IN:O_IN + R_IN].reshape(IN_W, D_MODEL)
    w1t = jnp.concatenate([raw[:768 + ROPE], jnp.zeros((128 - ROPE, D_MODEL), BF16), raw[768 + ROPE:]], axis=0)
    return w1t, wqt, wkvt


def _mixed_band(g_wqt, g_wkvt):
    uq = g_wqt.reshape(HEADS, HEAD_PAD, Q_LORA)[:, :QK_DIM].reshape(N_DEV, R_UQ, Q_LORA)
    uq = jnp.pad(uq, ((0, 0), (0, R_MIX - R_UQ), (0, 0)))
    ukv = g_wkvt.reshape(2, HEADS, NOPE, KV_LORA).transpose(1, 0, 2, 3).reshape(N_DEV, R_MIX, KV_LORA)
    return jnp.concatenate([uq, ukv, jnp.zeros((N_DEV, R_MIX, 1024 - Q_LORA - KV_LORA), F32)], axis=2)


def _rope_rows():
    half = ROPE // 2
    inv_freq = ROPE_THETA ** (-jnp.arange(half, dtype=F32) / half)
    zero, one = jnp.zeros((half,), F32), jnp.ones((half,), F32)
    rows = [jnp.concatenate(r) for r in (
        (inv_freq, inv_freq, zero, zero), (one, one, zero, zero), (-one, zero, zero, zero), (zero, one, zero, zero))]
    return jnp.stack(rows + [jnp.zeros((128,), F32)] * 4)


def _pad_rows(a, rows):
    return jnp.pad(a, ((0, rows - a.shape[0]), (0, 0)))


def kernel(x, positions, w_in, q_norm_g, w_uq, kv_norm_g, w_ukv, pool_w, pool_scale, w_out, ln_g, ln_b, loss_target, m_w_in, m_q_norm_g, m_w_uq, m_kv_norm_g, m_w_ukv, m_pool_w, m_pool_scale, m_w_out, m_ln_g, m_ln_b, v_w_in, v_q_norm_g, v_w_uq, v_kv_norm_g, v_w_ukv, v_pool_w, v_pool_scale, v_w_out, v_ln_g, v_ln_b):
    bb, ss, _ = x.shape
    tt = bb * ss
    atile = min(512, ss)
    nblk = ss // atile

    slab = _shard_slab(w_in, w_uq, w_ukv).astype(BF16)
    slabs = _all_gather(slab, "gather_weights").reshape(N_DEV, SLAB_ROWS, 1024)
    w1, wq, wkv = _unpack_weights(slabs)

    gq, gkv = q_norm_g.reshape(1, Q_LORA), kv_norm_g.reshape(1, KV_LORA)
    ps = pool_scale.reshape(1, POOL_W)
    pos_col = positions.reshape(bb, ss, 1)
    pos_blk = positions.reshape(bb, nblk, 1, atile)
    pmax = jnp.max(positions.reshape(bb, 2 * nblk, atile // 2), axis=-1).reshape(-1)
    pmin = jnp.min(positions.reshape(bb, 2 * nblk, atile // 2), axis=-1).reshape(-1)

    x2 = x.reshape(tt, D_MODEL)
    xq, xkv, ga, u, gb, qn, kvn, qcat, kcat, v, rc, rsa, rsb, wo = _fwd_in(
        x2, w1, gq, gkv, wq, wkv, pos_col.reshape(tt, 1), _rope_rows(), w_out.astype(BF16), atile)
    as3 = lambda a: a.reshape(bb, ss, a.shape[-1])
    qcat, kcat, v = as3(qcat), as3(kcat), as3(v)
    o, lse = _attn_fwd(qcat, kcat, v, pos_col, pos_blk, pmax, pmin, atile, atile)
    dz, ycat, do, dga, dgb, dpc, dl, loss_p, g_lng, g_lnb, g_ps, g_pw = _post(
        x, loss_target, o, as3(ga), as3(u), as3(gb), wo, pool_w.astype(BF16), ps, ln_g, ln_b, atile)

    bt = min(1024, tt)
    rows5 = lambda a: a.reshape(bb, HEADS, nblk, 1, atile)
    rc, rsa, rsb = as3(rc), as3(rsa), as3(rsb)
    dqp, dkn, dv, dkr = _attn_bwd(
        qcat, kcat, v, do, rows5(lse), rows5(dl), rc, rsa, rsb, pos_col, pos_blk, pmax, pmin, atile, atile)
    g_wq, g_wkn, g_wv = _grad_w(
        [(dqp.reshape(tt, HEADS * HEAD_PAD), qn), (dkn.reshape(tt, MLA_W), kvn), (dv.reshape(tt, MLA_W), kvn)],
        min(2 * bt, tt), "grad_w_uqkv")
    g_wkv = jnp.concatenate([g_wkn, g_wv], axis=0)
    grad_x, dh, g_gq, g_gkv = _bwd_mid(
        dqp, dkn, dv, dkr, as3(xq), as3(xkv), rc, rsa, rsb, wq, wkv, gq, gkv, dga, dgb, dpc, dz, w1, atile)
    small = jnp.concatenate(
        [_pad_rows(g_lng.reshape(8, 128), 8), _pad_rows(g_lnb.reshape(8, 128), 8), _pad_rows(g_gq.reshape(4, 128), 8),
         _pad_rows(g_gkv.reshape(2, 128), 8), _pad_rows(g_ps.reshape(4, 128), 8), g_pw.reshape(POOL_G * POOL_C, 128),
         loss_p], axis=0)
    dh2, half = dh.reshape(tt, IN_WP), D_MODEL // 2
    g_w1a, rs_mix, small = _grad_w([(dh2, x2)], bt, "grad_w_in_a", b_cols=(half, 0),
                                   carried=_blocks(_mixed_band(g_wq, g_wkv)), reduced=small)
    g_w1b, rs_in_a = _grad_w([(dh2, x2)], bt, "grad_w_in_b", b_cols=(half, 1), carried=_w_in_blocks(g_w1a))
    g_wo, rs_in_b = _grad_w([(ycat.reshape(tt, D_MODEL), dz.reshape(tt, D_MODEL))], bt, "grad_w_out",
                            carried=_w_in_blocks(g_w1b))
    rs_in = jnp.concatenate([rs_in_a, rs_in_b], axis=1)
    rs_out = _reduce_scatter(_blocks(g_wo.reshape(N_DEV, R_OUT, D_MODEL)), "reduce_scatter_w_out")
    loss = small[40 + POOL_G * POOL_C, 0]
    grads = {
        "w_in": rs_in[:R_IN],
        "q_norm_g": small[16:20].reshape(1, Q_LORA),
        "w_uq": rs_mix[:R_UQ, :Q_LORA],
        "kv_norm_g": small[24:26].reshape(1, KV_LORA),
        "w_ukv": rs_mix[:, Q_LORA:Q_LORA + KV_LORA].T,
        "pool_w": small[40:40 + POOL_G * POOL_C],
        "pool_scale": small[32:36].reshape(1, POOL_W),
        "w_out": rs_out,
        "ln_g": small[0:8].reshape(1, D_MODEL),
        "ln_b": small[8:16].reshape(1, D_MODEL),
    }
    transposed = ("w_in", "w_uq")

    names = ["w_in", "q_norm_g", "w_uq", "kv_norm_g", "w_ukv", "pool_w", "pool_scale", "w_out", "ln_g", "ln_b"]
    weights = dict(w_in=w_in, q_norm_g=q_norm_g, w_uq=w_uq, kv_norm_g=kv_norm_g, w_ukv=w_ukv, pool_w=pool_w,
                   pool_scale=pool_scale, w_out=w_out, ln_g=ln_g, ln_b=ln_b)
    moms = dict(w_in=(m_w_in, v_w_in), q_norm_g=(m_q_norm_g, v_q_norm_g), w_uq=(m_w_uq, v_w_uq),
                kv_norm_g=(m_kv_norm_g, v_kv_norm_g), w_ukv=(m_w_ukv, v_w_ukv), pool_w=(m_pool_w, v_pool_w),
                pool_scale=(m_pool_scale, v_pool_scale), w_out=(m_w_out, v_w_out), ln_g=(m_ln_g, v_ln_g),
                ln_b=(m_ln_b, v_ln_b))
    as2 = lambda a, n: a.T if n in transposed else a.reshape(grads[n].shape)
    upd = _adamw([(as2(weights[n], n), grads[n], as2(moms[n][0], n), as2(moms[n][1], n)) for n in names])
    shaped = lambda a, n: a.T if n in transposed else a.reshape(weights[n].shape)
    return (loss, grad_x,
            *[shaped(grads[n], n) for n in names],
            *[shaped(upd[i][0], n) for i, n in enumerate(names)],
            *[shaped(upd[i][1], n) for i, n in enumerate(names)],
            *[shaped(upd[i][2], n) for i, n in enumerate(names)])
```

```python
import jax
import jax.numpy as jnp
from jax import lax
from jax.experimental import pallas as pl
from jax.experimental.pallas import tpu as pltpu

F32 = jnp.float32
BF16 = jnp.bfloat16
MESH = pl.DeviceIdType.MESH

N_DEV = 8
D_MODEL = 1024
HEADS = 4
NOPE = 128
ROPE = 64
V_DIM = 128
QK_DIM = NOPE + ROPE
HEAD_PAD = 256
Q_LORA = 512
KV_LORA = 256
MLA_W = HEADS * V_DIM
POOL_W = 512
POOL_G = 4
POOL_C = 128
POOL_HALO = 16
IN_W = 2368
IN_WP = 2432
C_XQ, C_XKV, C_KR, C_GA, C_U, C_GB = 0, 512, 768, 896, 1408, 1920
ROPE_THETA = 10000.0
RMS_EPS = 1e-6
LN_EPS = 1e-5
ALPHA = 2.0 ** 0.25
SCALE = QK_DIM ** -0.5
LN2 = 0.6931471805599453
SCALE_LOG2 = SCALE / LN2
NEG = float(jnp.finfo(jnp.float32).min)

ADAM_LR = 0.001
ADAM_B1 = 0.9
ADAM_B2 = 0.999
ADAM_EPS = 1e-08
ADAM_WD = 0.01
ADAM_STEP = 10

R_OUT, R_MIX, R_UQ, R_IN = 128, 128, 96, 296
O_MIX, O_IN = 0, R_MIX
SLAB_ROWS = 432
R_IN_PAD = 304

V7X_VMEM_BYTES = 64 * 1024 * 1024
ATTN_FWD_HEADS_PER_STEP = 4
ATTN_BWD_HEADS_PER_STEP = 2


def _params(vmem_mb, semantics=None):
    assert vmem_mb * 1024 * 1024 < V7X_VMEM_BYTES
    return pltpu.CompilerParams(vmem_limit_bytes=vmem_mb * 1024 * 1024, dimension_semantics=semantics)


def _dot(a, b):
    return jnp.dot(a, b, preferred_element_type=F32)


def _dot_nt(a, b):
    return lax.dot_general(a, b, (((1,), (1,)), ((), ())), preferred_element_type=F32)


def _dot_tn(a, b):
    return lax.dot_general(a, b, (((0,), (0,)), ((), ())), preferred_element_type=F32)


def _rope_fwd(t, c, sa, sb):
    return t * c + pltpu.roll(t, 96, 1) * sa + pltpu.roll(t, 32, 1) * sb


def _rope_bwd(d, c, sa, sb):
    return d * c + pltpu.roll(d * sa, 32, 1) + pltpu.roll(d * sb, 96, 1)


def _silu_parts(g):
    sig = 0.5 * jnp.tanh(0.5 * g) + 0.5
    silu = g * sig
    return silu, sig + silu - silu * sig


def _pool_cnt(row0, rows):
    t = row0 + lax.broadcasted_iota(jnp.int32, (rows, POOL_W), 0)
    w = 2 << (lax.broadcasted_iota(jnp.int32, (rows, POOL_W), 1) // POOL_C)
    return jnp.minimum(t + 1, w).astype(F32)


def _pick_groups(s2, s4, s8, s16, lo, hi):
    return jnp.concatenate([s2[lo:hi, 0:128], s4[lo:hi, 128:256], s8[lo:hi, 256:384], s16[lo:hi, 384:512]], axis=1)


AG_SEMS = [pltpu.SemaphoreType.DMA((7,)), pltpu.SemaphoreType.DMA((7,)), pltpu.SemaphoreType.DMA]


def _ag_phases(x_ref, out_ref, send_sems, recv_sems, local_sem, sum_ref=None):
    m_per = x_ref.shape[0]
    x, y, c = lax.axis_index("x"), lax.axis_index("y"), lax.axis_index("c")
    me, sibling = (x, y, c), (x, y, 1 - c)
    chips = [(1 - x, y), (x, 1 - y), (1 - x, 1 - y)]

    def rows(px, py, pc):
        return out_ref.at[pl.ds((4 * px + 2 * py + pc) * m_per, m_per), :]

    def copy(k, block, to, src=None):
        return pltpu.make_async_remote_copy(
            src_ref=rows(*block) if src is None else src, dst_ref=rows(*block),
            send_sem=send_sems.at[k], recv_sem=recv_sems.at[k], device_id=to, device_id_type=MESH)

    def mine():
        return pltpu.make_async_copy(x_ref, rows(*me), local_sem)

    def first():
        return [copy(0, me, sibling, src=x_ref)] + [copy(1 + j, me, (*chip, c), src=x_ref) for j, chip in enumerate(chips)]

    def passed():
        return [copy(4 + j, (*chip, c), sibling) for j, chip in enumerate(chips)]

    def start():
        for cp in [mine()] + first():
            cp.start()

    def forward():
        for j, (chip, cp) in enumerate(zip(chips, passed())):
            copy(1 + j, (*chip, c), me).wait_recv()
            cp.start()

    def finish():
        copy(0, sibling, me).wait_recv()
        for j, chip in enumerate(chips):
            copy(4 + j, (*chip, 1 - c), me).wait_recv()
        for cp in first() + passed():
            cp.wait_send()
        mine().wait()
        if sum_ref is not None:
            acc = out_ref[pl.ds(0, m_per), :]
            for d in range(1, N_DEV):
                acc = acc + out_ref[pl.ds(d * m_per, m_per), :]
            sum_ref[...] = acc

    return start, forward, finish


def _all_gather(shard, name):
    m_per, n = shard.shape

    def body(x_ref, out_ref, *sems):
        for phase in _ag_phases(x_ref, out_ref, *sems):
            phase()

    vmem = pl.BlockSpec(memory_space=pltpu.VMEM)
    return pl.pallas_call(
        body, name=name, out_shape=jax.ShapeDtypeStruct((N_DEV * m_per, n), shard.dtype),
        in_specs=[vmem], out_specs=vmem, scratch_shapes=AG_SEMS, compiler_params=_params(32))(shard)


def _reduce_scatter(sc, name):
    def body(g_hbm, out_ref, *scratch):
        for phase in _rs_phases(sc, g_hbm, out_ref, *scratch):
            phase()

    return pl.pallas_call(
        body, name=name, out_shape=jax.ShapeDtypeStruct((sc.rows, sc.width), F32),
        in_specs=[pl.BlockSpec(memory_space=pl.ANY)], out_specs=pl.BlockSpec(memory_space=pltpu.VMEM),
        scratch_shapes=_rs_scratch(sc), compiler_params=_params(32))(sc.array)


class _Scattered:
    def __init__(self, array, rows, pieces, locate):
        self.array, self.rows, self.pieces, self.locate = array, rows, pieces, locate
        self.width = array.shape[-1]


def _blocks(g):
    return _Scattered(g, g.shape[1], ((0, g.shape[1]),), lambda ref, d, row, rows: ref.at[d])


def _w_in_blocks(g_w1t):
    def locate(ref, d, row, rows):
        r = R_IN * d + row
        return ref.at[pl.ds(pl.multiple_of(r + jnp.where(r >= C_KR + ROPE, 128 - ROPE, 0), 8), rows), :]

    cut = C_KR + ROPE - 2 * R_IN
    return _Scattered(g_w1t, R_IN_PAD, ((0, cut), (cut, R_IN - cut)), locate)


def _rs_scratch(sc):
    rr, ww, n = sc.rows, sc.width, 4 * len(sc.pieces)
    return [pltpu.VMEM((4, rr, ww), F32), pltpu.VMEM((4, rr, ww), F32),
            pltpu.VMEM((3, rr, ww), BF16), pltpu.VMEM((3, rr, ww), BF16),
            pltpu.SemaphoreType.DMA((n,)), pltpu.SemaphoreType.DMA((n,)), pltpu.SemaphoreType.DMA((n,)),
            pltpu.SemaphoreType.DMA((3,)), pltpu.SemaphoreType.DMA((3,))]


def _rs_phases(sc, g_hbm, out_ref, own_ref, recv1_ref, sendb_ref, recv2_ref, ld_sems, s1_send, s1_recv, s2_send, s2_recv):
    rr, ww = out_ref.shape
    chunk = next(c for c in (128, 80, 64, 48, 32, 16) if rr % c == 0)
    x, y, c = lax.axis_index("x"), lax.axis_index("y"), lax.axis_index("c")
    chips = [(1 - x, y), (x, 1 - y), (1 - x, 1 - y)]
    npieces = len(sc.pieces)
    filled = sum(rows for _, rows in sc.pieces)

    def pieces(d, buf, k):
        for p, (row, rows) in enumerate(sc.pieces):
            dst = buf.at[k] if (row, rows) == (0, rr) else buf.at[k, pl.ds(row, rows), :]
            yield k * npieces + p, sc.locate(g_hbm, d, row, rows), dst

    def loads():
        return [pltpu.make_async_copy(src, dst, ld_sems.at[s])
                for k in range(4) for s, src, dst in pieces(2 * k + c, own_ref, k)]

    def stage1():
        return [pltpu.make_async_remote_copy(
            src_ref=src, dst_ref=dst, send_sem=s1_send.at[s], recv_sem=s1_recv.at[s],
            device_id=(x, y, 1 - c), device_id_type=MESH)
            for k in range(4) for s, src, dst in pieces(2 * k + (1 - c), recv1_ref, k)]

    def stage2():
        return [pltpu.make_async_remote_copy(
            src_ref=sendb_ref.at[r], dst_ref=recv2_ref.at[r], send_sem=s2_send.at[r],
            recv_sem=s2_recv.at[r], device_id=(cx, cy, c), device_id_type=MESH) for r, (cx, cy) in enumerate(chips)]

    def start():
        if filled < rr:
            own_ref[:, filled:rr, :] = jnp.zeros((4, rr - filled, ww), F32)
            recv1_ref[:, filled:rr, :] = jnp.zeros((4, rr - filled, ww), F32)
        for cp in loads() + stage1():
            cp.start()

    def forward():
        for cp in loads():
            cp.wait()
        for cp in stage1():
            cp.wait_recv()
        sends = stage2()
        for r, (cx, cy) in enumerate(chips):
            kk = 2 * cx + cy

            def pack(i, carry, r=r, kk=kk):
                rows = pl.ds(pl.multiple_of(i * chunk, chunk), chunk)
                sendb_ref[r, rows, :] = (own_ref[kk, rows, :] + recv1_ref[kk, rows, :]).astype(BF16)
                return carry

            lax.fori_loop(0, rr // chunk, pack, 0)
            sends[r].start()

    def finish():
        for cp in stage2():
            cp.wait_recv()
        mine = 2 * x + y

        def total(i, carry):
            rows = pl.ds(pl.multiple_of(i * chunk, chunk), chunk)
            acc = own_ref[mine, rows, :] + recv1_ref[mine, rows, :]
            for r in range(3):
                acc = acc + recv2_ref[r, rows, :].astype(F32)
            out_ref[rows, :] = acc
            return carry

        lax.fori_loop(0, rr // chunk, total, 0)
        for cp in stage1() + stage2():
            cp.wait_send()

    return start, forward, finish


def _fwd_in(x2, w1, gq, gkv, wq, wkv, pos, rope_rows, shard, tm):
    tt = x2.shape[0]
    steps = tt // tm
    gathered_shape = (N_DEV * shard.shape[0], shard.shape[1])

    def body(x_ref, w1_ref, gq_ref, gkv_ref, wq_ref, wkv_ref, pos_ref, rr_ref, shard_ref,
             xq_ref, xkv_ref, ga_ref, u_ref, gb_ref, qn_ref, kvn_ref, qcat_ref, kcat_ref, v_ref,
             c_ref, sa_ref, sb_ref, all_ref, gathered, *sems):
        step = pl.program_id(0)
        start, forward, finish = _ag_phases(shard_ref, gathered, *sems)
        pl.when(step == 0)(start)
        pl.when(step == min(2, steps - 1))(forward)

        @pl.when(step == steps - 1)
        def _():
            finish()
            all_ref[...] = gathered[...]

        ang = pos_ref[...].astype(F32) * rr_ref[0:1, :]
        cos, sin = jnp.cos(ang), jnp.sin(ang)
        c, sa, sb = cos * rr_ref[1:2, :], sin * rr_ref[2:3, :], sin * rr_ref[3:4, :]
        c_ref[...] = c
        sa_ref[...] = sa
        sb_ref[...] = sb
        h = _dot_nt(x_ref[...].astype(BF16), w1_ref[...])
        xq = h[:, C_XQ:C_XKV]
        xkv = h[:, C_XKV:C_KR]
        xq_ref[...] = xq
        xkv_ref[...] = xkv
        ga_ref[...] = h[:, C_GA:C_U]
        u_ref[...] = h[:, C_U:C_GB]
        gb_ref[...] = h[:, C_GB:IN_WP]
        qn =(xq * lax.rsqrt(jnp.mean(xq * xq, axis=-1, keepdims=True) + RMS_EPS) * gq_ref[...]).astype(BF16)
        kvn = (xkv * lax.rsqrt(jnp.mean(xkv * xkv, axis=-1, keepdims=True) + RMS_EPS) * gkv_ref[...]).astype(BF16)
        qn_ref[...] = qn
        kvn_ref[...] = kvn
        q = _dot_nt(qn, wq_ref[...]) * SCALE_LOG2
        kv = _dot_nt(kvn, wkv_ref[...])
        kr = _rope_fwd(h[:, C_KR:C_GA], c, sa, sb).astype(BF16)
        for hd in range(HEADS):
            lo = hd * HEAD_PAD
            qcat_ref[:, lo:lo + NOPE] = q[:, lo:lo + NOPE].astype(BF16)
            qcat_ref[:, lo + NOPE:lo + HEAD_PAD] = _rope_fwd(q[:, lo + NOPE:lo + HEAD_PAD], c, sa, sb).astype(BF16)
            kcat_ref[:, lo:lo + NOPE] = kv[:, hd * NOPE:(hd + 1) * NOPE].astype(BF16)
            kcat_ref[:, lo + NOPE:lo + HEAD_PAD] = kr
        v_ref[...] = kv[:, MLA_W:].astype(BF16)

    def tile(w):
        return pl.BlockSpec((tm, w), lambda i: (i, 0))

    def whole(a):
        return pl.BlockSpec(a.shape, lambda i: (0, 0))

    outs = [(Q_LORA, F32), (KV_LORA, F32), (MLA_W, F32), (POOL_W, F32), (POOL_W, F32),
            (Q_LORA, BF16), (KV_LORA, BF16), (HEADS * HEAD_PAD, BF16), (HEADS * HEAD_PAD, BF16), (MLA_W, BF16),
            (128, F32), (128, F32), (128, F32)]
    return pl.pallas_call(
        body, name="fwd_in", grid=(steps,),
        in_specs=[tile(D_MODEL), whole(w1), whole(gq), whole(gkv), whole(wq), whole(wkv), tile(1), whole(rope_rows),
                  pl.BlockSpec(memory_space=pl.ANY)],
        out_specs=[tile(w) for w, _ in outs] + [pl.BlockSpec(gathered_shape, lambda i: (0, 0))],
        out_shape=[jax.ShapeDtypeStruct((tt, w), dt) for w, dt in outs]
        + [jax.ShapeDtypeStruct(gathered_shape, shard.dtype)],
        scratch_shapes=[pltpu.VMEM(gathered_shape, shard.dtype)] + AG_SEMS,
        compiler_params=_params(56, ("arbitrary",)))(x2, w1, gq, gkv, wq, wkv, pos, rope_rows, shard)


def _tile_cases(pmax_ref, pmin_ref, b, nhalf, qi, ki):
    q0, q1 = b * nhalf + 2 * qi, b * nhalf + 2 * qi + 1
    k0, k1 = b * nhalf + 2 * ki, b * nhalf + 2 * ki + 1
    needed = jnp.maximum(pmax_ref[q0], pmax_ref[q1]) >= jnp.minimum(pmin_ref[k0], pmin_ref[k1])
    visible = jnp.minimum(pmin_ref[q0], pmin_ref[q1]) >= jnp.maximum(pmax_ref[k0], pmax_ref[k1])
    stepped = pmax_ref[q0] < pmin_ref[k1]
    return needed, visible, stepped


def _attn_fwd(qcat, kcat, v, pos_col, pos_blk, pmax, pmin, tq, tk):
    bb, ss, _ = qcat.shape
    nq, nk = ss // tq, ss // tk
    lanes = 128
    hps = ATTN_FWD_HEADS_PER_STEP

    def body(pmax_ref, pmin_ref, q_ref, k_ref, v_ref, pc_ref, pb_ref, o_ref, lse_ref, m_s, acc_s):
        b, qi = pl.program_id(0), pl.program_id(2)
        m_s[...] = jnp.full(m_s.shape, NEG, F32)
        acc_s[...] = jnp.zeros(acc_s.shape, F32)

        def part(ki, masked, q_lo, q_n, k_n):
            qrows = slice(q_lo, q_lo + q_n)
            krows = pl.ds(pl.multiple_of(ki * tk, tk), k_n)
            if masked:
                mask = pc_ref[qrows, :] >= pb_ref[ki][:, :k_n]
            ones = jnp.ones((k_n, lanes), BF16)
            parts = []
            for hd in range(hps):
                qk = slice(hd * HEAD_PAD, (hd + 1) * HEAD_PAD)
                s = _dot_nt(q_ref[qrows, qk], k_ref[krows, qk])
                if masked:
                    s = jnp.where(mask, s, NEG)
                m_prev = m_s[hd, qrows, :]
                m_new = jnp.maximum(m_prev, jnp.max(s, axis=-1, keepdims=True))
                p = jnp.exp2(s - jnp.tile(m_new, (1, k_n // lanes)))
                a = jnp.exp2(m_prev - m_new)
                vv = jnp.concatenate([v_ref[krows, hd * V_DIM:(hd + 1) * V_DIM], ones], axis=1)
                parts.append((m_new, jnp.tile(a, (1, 2)) * acc_s[hd, qrows, :] + _dot(p.astype(BF16), vv)))
            for hd, (m_new, acc) in enumerate(parts):
                acc_s[hd, qrows, :] = acc
                m_s[hd, qrows, :] = m_new

        def step(ki, carry):
            needed, visible, stepped = _tile_cases(pmax_ref, pmin_ref, b, 2 * nq, qi, ki)

            @pl.when(needed & visible)
            def _():
                part(ki, False, 0, tq, tk)

            @pl.when(needed & ~visible & stepped)
            def _():
                part(ki, True, 0, tq // 2, tk // 2)
                part(ki, True, tq // 2, tq // 2, tk)

            @pl.when(needed & ~visible & ~stepped)
            def _():
                part(ki, True, 0, tq, tk)

            return carry

        lax.fori_loop(0, nk, step, 0)
        for hd in range(hps):
            acc = acc_s[hd]
            l = acc[:, V_DIM:]
            o_ref[:, hd * V_DIM:(hd + 1) * V_DIM] = acc[:, :V_DIM] / l
            lse_ref[hd] = (m_s[hd] + jnp.log2(l)).T[0:1, :]

    grid_spec = pltpu.PrefetchScalarGridSpec(
        num_scalar_prefetch=2, grid=(bb, HEADS // hps, nq),
        in_specs=[
            pl.BlockSpec((None, tq, hps * HEAD_PAD), lambda b, h, i, *_: (b, i, h)),
            pl.BlockSpec((None, ss, hps * HEAD_PAD), lambda b, h, i, *_: (b, 0, h)),
            pl.BlockSpec((None, ss, hps * V_DIM), lambda b, h, i, *_: (b, 0, h)),
            pl.BlockSpec((None, tq, 1), lambda b, h, i, *_: (b, i, 0)),
            pl.BlockSpec((None, nk, 1, tk), lambda b, h, i, *_: (b, 0, 0, 0)),
        ],
        out_specs=[
            pl.BlockSpec((None, tq, hps * V_DIM), lambda b, h, i, *_: (b, i, h)),
            pl.BlockSpec((None, hps, 1, tq), lambda b, h, i, *_: (b, h, 0, i)),
        ],
        scratch_shapes=[pltpu.VMEM((hps, tq, lanes), F32), pltpu.VMEM((hps, tq, 2 * V_DIM), F32)])
    return pl.pallas_call(
        body, name="attn_fwd", grid_spec=grid_spec,
        out_shape=[jax.ShapeDtypeStruct((bb, ss, MLA_W), F32), jax.ShapeDtypeStruct((bb, HEADS, 1, ss), F32)],
        compiler_params=_params(48, ("arbitrary", "arbitrary", "arbitrary")))(pmax, pmin, qcat, kcat, v, pos_col, pos_blk)


def _post(x, tgt, o, ga, u, gb, w_out, pool_wb, pool_scale, ln_g, ln_b, ts):
    bb, ss, _ = x.shape
    nt = ss // ts
    hb = ts // POOL_HALO

    def body(x_ref, tgt_ref, o_ref, ga_ref, u_ref, uh_ref, gb_ref, wo_ref, pw_ref, ps_ref, lg_ref, lb_ref,
             dz_ref, ycat_ref, do_ref, dga_ref, dgb_ref, dpc_ref, dl_ref, loss_ref, glg_ref, glb_ref, gps_ref, gpw_ref):
        step = pl.program_id(0)
        j = step % nt

        @pl.when(step == 0)
        def _():
            for r in (loss_ref, glg_ref, glb_ref, gps_ref, gpw_ref):
                r[...] = jnp.zeros(r.shape, F32)

        o, ga, u, gb = o_ref[...], ga_ref[...], u_ref[...], gb_ref[...]
        sa, dsa = _silu_parts(ga)
        sb, dsb = _silu_parts(gb)
        ext = jnp.concatenate([jnp.where(j > 0, uh_ref[...], 0.0), u], axis=0)
        s2 = ext + pltpu.roll(ext, 1, 0)
        s4 = s2 + pltpu.roll(s2, 2, 0)
        s8 = s4 + pltpu.roll(s4, 4, 0)
        s16 = s8 + pltpu.roll(s8, 8, 0)
        cnt = _pool_cnt(j * ts, ts)
        rcnt = 1.0 / cnt
        pooled = (_pick_groups(s2, s4, s8, s16, POOL_HALO, POOL_HALO + ts) * rcnt - u).astype(BF16)
        mixed = jnp.concatenate(
            [_dot(pooled[:, g * POOL_C:(g + 1) * POOL_C], pw_ref[g]) for g in range(POOL_G)], axis=1)
        ps = ps_ref[...]
        scaled = mixed * ps
        ycat = jnp.concatenate([o * sa, scaled * sb], axis=1).astype(BF16)
        ycat_ref[...] = ycat
        z = ALPHA * x_ref[...] + _dot(ycat, wo_ref[...])
        mu = jnp.mean(z, axis=-1, keepdims=True)
        zc = z - mu
        rstd = lax.rsqrt(jnp.mean(zc * zc, axis=-1, keepdims=True) + LN_EPS)
        xhat = zc * rstd
        lg = lg_ref[...]
        diff = xhat * lg + lb_ref[...] - tgt_ref[...]
        loss_ref[...] += jnp.sum(diff * diff) * (0.5 / D_MODEL)
        glb_ref[...] += jnp.sum(diff, axis=0, keepdims=True) * (1.0 / D_MODEL)
        glg_ref[...] += jnp.sum(diff * xhat, axis=0, keepdims=True) * (1.0 / D_MODEL)
        dxh = diff * (lg * (1.0 / D_MODEL))
        dz = rstd * (dxh - jnp.mean(dxh, axis=-1, keepdims=True) - xhat * jnp.mean(dxh * xhat, axis=-1, keepdims=True))
        dz_ref[...] = dz
        dycat = _dot_nt(dz.astype(BF16), wo_ref[...])
        dya, dyb = dycat[:, :MLA_W], dycat[:, MLA_W:]
        do = dya * sa
        do_ref[...] = do.astype(BF16)
        doo = do * o
        for hd in range(HEADS):
            dl_ref[hd] = jnp.sum(doo[:, hd * V_DIM:(hd + 1) * V_DIM].T, axis=0, keepdims=True)
        dga_ref[...] = (dya * o * dsa).astype(BF16)
        dgb_ref[...] = (dyb * scaled * dsb).astype(BF16)
        dscaled = dyb * sb
        gps_ref[...] += jnp.sum(dscaled * mixed, axis=0, keepdims=True)
        dmixed = (dscaled * ps).astype(BF16)
        dpooled = []
        for g in range(POOL_G):
            cols = slice(g * POOL_C, (g + 1) * POOL_C)
            gpw_ref[g] += _dot_tn(pooled[:, cols], dmixed[:, cols])
            dpooled.append(_dot_nt(dmixed[:, cols], pw_ref[g]))
        dpc_ref[...] = jnp.concatenate(dpooled, axis=1) * rcnt

    def tile(w):
        return pl.BlockSpec((None, ts, w), lambda i: (i // nt, i % nt, 0))

    def whole(a):
        nd = a.ndim
        return pl.BlockSpec(a.shape, lambda i: (0,) * nd)

    halo = pl.BlockSpec((None, POOL_HALO, POOL_W), lambda i: (i // nt, jnp.maximum((i % nt) * hb - 1, 0), 0))
    acc_shapes = [(8, 128), (1, D_MODEL), (1, D_MODEL), (1, POOL_W), (POOL_G, POOL_C, POOL_C)]
    tile_outs = [(D_MODEL, F32), (D_MODEL, BF16), (MLA_W, BF16), (MLA_W, BF16), (POOL_W, BF16), (POOL_W, F32)]
    return pl.pallas_call(
        body, name="post", grid=(bb * nt,),
        in_specs=[tile(D_MODEL), tile(D_MODEL), tile(MLA_W), tile(MLA_W), tile(POOL_W), halo, tile(POOL_W),
                  whole(w_out), whole(pool_wb), whole(pool_scale), whole(ln_g), whole(ln_b)],
        out_specs=[tile(w) for w, _ in tile_outs]
        + [pl.BlockSpec((None, HEADS, 1, ts), lambda i: (i // nt, 0, 0, i % nt))]
        + [pl.BlockSpec(s, lambda i, n=len(s): (0,) * n) for s in acc_shapes],
        out_shape=[jax.ShapeDtypeStruct((bb, ss, w), dt) for w, dt in tile_outs]
        + [jax.ShapeDtypeStruct((bb, HEADS, 1, ss), F32)]
        + [jax.ShapeDtypeStruct(s, F32) for s in acc_shapes],
        compiler_params=_params(48, ("arbitrary",)))(x, tgt, o, ga, u, u, gb, w_out, pool_wb, pool_scale, ln_g, ln_b)


def _attn_bwd(qcat, kcat, v, do, lse, dl, rc, rsa, rsb, pos_col, pos_blk, pmax, pmin, tq, tk):
    bb, ss, _ = qcat.shape
    nq, nk = ss // tq, ss // tk
    hps = ATTN_BWD_HEADS_PER_STEP

    def body(pmax_ref, pmin_ref, q_ref, k_ref, v_ref, do_ref, lse_ref, dl_ref, c_ref, sa_ref, sb_ref, pc_ref, pb_ref,
             dqp_ref, dkn_ref, dv_ref, dkr_ref, dq_s, dk_s, dv_s):
        b = pl.program_id(0)
        dq_s[...] = jnp.zeros(dq_s.shape, F32)

        def part(qi, ki, masked):
            krows = pl.ds(pl.multiple_of(ki * tk, tk), tk)
            qrows = pl.ds(pl.multiple_of(qi * tq, tq), tq)
            if masked:
                mask = pb_ref[qi] >= pc_ref[krows, :]
            parts = []
            for hd in range(hps):
                qk = slice(hd * HEAD_PAD, (hd + 1) * HEAD_PAD)
                vs = slice(hd * V_DIM, (hd + 1) * V_DIM)
                q = q_ref[qrows, qk]
                dd = do_ref[qrows, vs]
                st = _dot_nt(k_ref[krows, qk], q)
                if masked:
                    st = jnp.where(mask, st, NEG)
                pt = jnp.exp2(st - lse_ref[hd, qi])
                dpt = _dot_nt(v_ref[krows, vs], dd)
                dst = (pt * (dpt - dl_ref[hd, qi])).astype(BF16)
                parts.append((_dot(pt.astype(BF16), dd), _dot(dst, q), _dot_tn(dst, k_ref[krows, qk])))
            for hd, (dv_part, dk_part, dq_part) in enumerate(parts):
                dv_s[:, hd * V_DIM:(hd + 1) * V_DIM] += dv_part
                dk_s[:, hd * HEAD_PAD:(hd + 1) * HEAD_PAD] += dk_part
                dq_s[qrows, hd * HEAD_PAD:(hd + 1) * HEAD_PAD] += dq_part

        def kv_step(ki, carry):
            krows = pl.ds(pl.multiple_of(ki * tk, tk), tk)
            dk_s[...] = jnp.zeros(dk_s.shape, F32)
            dv_s[...] = jnp.zeros(dv_s.shape, F32)

            def q_step(qi, c2):
                needed, visible, _ = _tile_cases(pmax_ref, pmin_ref, b, 2 * nq, qi, ki)

                @pl.when(needed & visible)
                def _():
                    part(qi, ki, False)

                @pl.when(needed & ~visible)
                def _():
                    part(qi, ki, True)

                return c2

            lax.fori_loop(0, nq, q_step, 0)
            dkr = jnp.zeros((tk, HEAD_PAD - NOPE), F32)
            for hd in range(hps):
                lo = hd * HEAD_PAD
                dkn_ref[krows, hd * NOPE:(hd + 1) * NOPE] = (dk_s[:, lo:lo + NOPE] * LN2).astype(BF16)
                dkr = dkr + dk_s[:, lo + NOPE:lo + HEAD_PAD]
            dkr_ref[krows, :] = dkr * LN2
            dv_ref[krows, :] = dv_s[...].astype(BF16)
            return carry

        lax.fori_loop(0, nk, kv_step, 0)
        c, sa, sb = c_ref[...], sa_ref[...], sb_ref[...]
        for hd in range(hps):
            lo = hd * HEAD_PAD
            dqp_ref[:, lo:lo + NOPE] = (dq_s[:, lo:lo + NOPE] * SCALE).astype(BF16)
            dqp_ref[:, lo + NOPE:lo + HEAD_PAD] = _rope_bwd(
                dq_s[:, lo + NOPE:lo + HEAD_PAD] * SCALE, c, sa, sb).astype(BF16)

    def per_head(w):
        return pl.BlockSpec((None, ss, hps * w), lambda b, h, *_: (b, 0, h))

    def rows_of_head():
        return pl.BlockSpec((None, hps, nq, 1, tq), lambda b, h, *_: (b, h, 0, 0, 0))

    def per_batch(w):
        return pl.BlockSpec((None, ss, w), lambda b, h, *_: (b, 0, 0))

    grid_spec = pltpu.PrefetchScalarGridSpec(
        num_scalar_prefetch=2, grid=(bb, HEADS // hps),
        in_specs=[per_head(HEAD_PAD), per_head(HEAD_PAD), per_head(V_DIM), per_head(V_DIM),
                  rows_of_head(), rows_of_head(), per_batch(128), per_batch(128), per_batch(128), per_batch(1),
                  pl.BlockSpec((None, nq, 1, tq), lambda b, h, *_: (b, 0, 0, 0))],
        out_specs=[per_head(HEAD_PAD), per_head(NOPE), per_head(V_DIM),
                   pl.BlockSpec((None, None, ss, HEAD_PAD - NOPE), lambda b, h, *_: (b, h, 0, 0))],
        scratch_shapes=[pltpu.VMEM((ss, hps * HEAD_PAD), F32), pltpu.VMEM((tk, hps * HEAD_PAD), F32),
                        pltpu.VMEM((tk, hps * V_DIM), F32)])
    return pl.pallas_call(
        body, name="attn_bwd", grid_spec=grid_spec,
        out_shape=[jax.ShapeDtypeStruct((bb, ss, HEADS * HEAD_PAD), BF16),
                   jax.ShapeDtypeStruct((bb, ss, MLA_W), BF16),
                   jax.ShapeDtypeStruct((bb, ss, MLA_W), BF16),
                   jax.ShapeDtypeStruct((bb, HEADS // hps, ss, HEAD_PAD - NOPE), F32)],
        compiler_params=_params(56, ("arbitrary", "arbitrary")))(
            pmax, pmin, qcat, kcat, v, do, lse, dl, rc, rsa, rsb, pos_col, pos_blk)


def _bwd_mid(dqp, dkn, dv, dkr, xq, xkv, rc, rsa, rsb, wq, wkv, gq, gkv, dga, dgb, dpc, dz, w1, ts):
    bb, ss, _ = dz.shape
    nt = ss // ts
    hb = ts // POOL_HALO
    groups = dkr.shape[1]

    def body(dqp_ref, dkn_ref, dv_ref, dkr_ref, xq_ref, xkv_ref, c_ref, sa_ref, sb_ref, wq_ref, wkv_ref, gq_ref,
             gkv_ref, dga_ref, dgb_ref, dpc_ref, dph_ref, dz_ref, w1_ref,
             gx_ref, dh_ref, ggq_ref, ggkv_ref):
        step = pl.program_id(0)
        j = step % nt

        @pl.when(step == 0)
        def _():
            ggq_ref[...] = jnp.zeros(ggq_ref.shape, F32)
            ggkv_ref[...] = jnp.zeros(ggkv_ref.shape, F32)

        dkr = dkr_ref[0]
        for g in range(1, groups):
            dkr = dkr + dkr_ref[g]
        dkr = _rope_bwd(dkr, c_ref[...], sa_ref[...], sb_ref[...]).astype(BF16)

        def rms_bwd(x, g, dn):
            inv = lax.rsqrt(jnp.mean(x * x, axis=-1, keepdims=True) + RMS_EPS)
            xh = x * inv
            dxh = dn * g
            return inv * (dxh - xh * jnp.mean(dxh * xh, axis=-1, keepdims=True)), jnp.sum(dn * xh, axis=0, keepdims=True)

        dxq, ggq = rms_bwd(xq_ref[...], gq_ref[...], _dot(dqp_ref[...], wq_ref[...]))
        dkvn = _dot(dkn_ref[...], wkv_ref[:MLA_W, :]) + _dot(dv_ref[...], wkv_ref[MLA_W:, :])
        dxkv, ggkv = rms_bwd(xkv_ref[...], gkv_ref[...], dkvn)
        ggq_ref[...] += ggq
        ggkv_ref[...] += ggkv
        dpc = dpc_ref[...]
        n = ts + POOL_HALO
        ext = jnp.concatenate([dpc, jnp.where(j < nt - 1, dph_ref[...], 0.0)], axis=0)
        r2 = ext + pltpu.roll(ext, n - 1, 0)
        r4 = r2 + pltpu.roll(r2, n - 2, 0)
        r8 = r4 + pltpu.roll(r4, n - 4, 0)
        r16 = r8 + pltpu.roll(r8, n - 8, 0)
        du = _pick_groups(r2, r4, r8, r16, 0, ts) - dpc * _pool_cnt(j * ts, ts)
        dh = jnp.concatenate(
            [dxq.astype(BF16), dxkv.astype(BF16), dkr, dga_ref[...], du.astype(BF16), dgb_ref[...]], axis=1)
        dh_ref[...] = dh
        gx_ref[...] = ALPHA * dz_ref[...] + _dot(dh, w1_ref[...])

    def tile(w):
        return pl.BlockSpec((None, ts, w), lambda i: (i // nt, i % nt, 0))

    def whole(a):
        return pl.BlockSpec(a.shape, lambda i: (0, 0))

    halo = pl.BlockSpec((None, POOL_HALO, POOL_W),
                        lambda i: (i // nt, jnp.minimum((i % nt + 1) * hb, ss // POOL_HALO - 1), 0))
    return pl.pallas_call(
        body, name="bwd_mid", grid=(bb * nt,),
        in_specs=[tile(HEADS * HEAD_PAD), tile(MLA_W), tile(MLA_W),
                  pl.BlockSpec((None, groups, ts, HEAD_PAD - NOPE), lambda i: (i // nt, 0, i % nt, 0)),
                  tile(Q_LORA), tile(KV_LORA),
                  tile(128), tile(128), tile(128), whole(wq), whole(wkv), whole(gq), whole(gkv),
                  tile(MLA_W), tile(POOL_W), tile(POOL_W), halo, tile(D_MODEL), whole(w1)],
        out_specs=[tile(D_MODEL), tile(IN_WP),
                   pl.BlockSpec((1, Q_LORA), lambda i: (0, 0)), pl.BlockSpec((1, KV_LORA), lambda i: (0, 0))],
        out_shape=[jax.ShapeDtypeStruct((bb, ss, D_MODEL), F32), jax.ShapeDtypeStruct((bb, ss, IN_WP), BF16),
                   jax.ShapeDtypeStruct((1, Q_LORA), F32), jax.ShapeDtypeStruct((1, KV_LORA), F32)],
        compiler_params=_params(48, ("arbitrary",)))(
            dqp, dkn, dv, dkr, xq, xkv, rc, rsa, rsb, wq, wkv, gq, gkv, dga, dgb, dpc, dpc, dz, w1)


def _grad_w(pairs, bt, name, b_cols=None, carried=None, reduced=None):
    tt = pairs[0][0].shape[0]
    steps = tt // bt
    npairs = len(pairs)
    n_rs = len(_rs_scratch(carried)) if carried is not None else 0

    def body(*refs):
        ab, rest = refs[:2 * npairs], list(refs[2 * npairs:])
        g_hbm = rest.pop(0) if carried is not None else None
        part = rest.pop(0) if reduced is not None else None
        outs = [rest.pop(0) for _ in range(npairs)]
        phases = []
        if carried is not None:
            rs_out = rest.pop(0)
        if reduced is not None:
            sum_ref = rest.pop(0)
        if carried is not None:
            phases.append(_rs_phases(carried, g_hbm, rs_out, *rest[:n_rs]))
        if reduced is not None:
            gathered, *sems = rest[n_rs:]
            phases.append(_ag_phases(part, gathered, *sems, sum_ref=sum_ref))
        step = pl.program_id(0)

        @pl.when(step == 0)
        def _():
            for o in outs:
                o[...] = jnp.zeros(o.shape, F32)
            for start, _, _ in phases:
                start()

        parts = [_dot_tn(ab[2 * i][...].astype(BF16), ab[2 * i + 1][...].astype(BF16)) for i in range(npairs)]
        for out, part in zip(outs, parts):
            out[...] += part

        @pl.when(step == min(2, steps - 1))
        def _():
            for _, forward, _ in phases:
                forward()

        @pl.when(step == steps - 1)
        def _():
            for _, _, finish in phases:
                finish()

    in_specs, out_specs, out_shape = [], [], []
    for a, b in pairs:
        m, (n, col) = a.shape[1], b_cols or (b.shape[1], 0)
        in_specs += [pl.BlockSpec((bt, m), lambda i: (i, 0)), pl.BlockSpec((bt, n), lambda i, col=col: (i, col))]
        out_specs.append(pl.BlockSpec((m, n), lambda i: (0, 0)))
        out_shape.append(jax.ShapeDtypeStruct((m, n), F32))
    args = [t for p in pairs for t in p]
    scratch = []
    if carried is not None:
        in_specs.append(pl.BlockSpec(memory_space=pl.ANY))
        out_specs.append(pl.BlockSpec((carried.rows, carried.width), lambda i: (0, 0)))
        out_shape.append(jax.ShapeDtypeStruct((carried.rows, carried.width), F32))
        args.append(carried.array)
        scratch += _rs_scratch(carried)
    if reduced is not None:
        in_specs.append(pl.BlockSpec(memory_space=pl.ANY))
        out_specs.append(pl.BlockSpec(reduced.shape, lambda i: (0, 0)))
        out_shape.append(jax.ShapeDtypeStruct(reduced.shape, F32))
        args.append(reduced)
        scratch += [pltpu.VMEM((N_DEV * reduced.shape[0], reduced.shape[1]), F32)] + AG_SEMS
    return pl.pallas_call(
        body, name=name, grid=(steps,), in_specs=in_specs, out_specs=out_specs, out_shape=out_shape,
        scratch_shapes=scratch, compiler_params=_params(56, ("arbitrary",)))(*args)


def _adamw(triples):
    n = len(triples)

    def body(*refs):
        ins, outs = refs[:4 * n], refs[4 * n:]
        for i in range(n):
            w, g, m, v = (r[...] for r in ins[4 * i:4 * i + 4])
            m = ADAM_B1 * m + (1.0 - ADAM_B1) * g
            v = ADAM_B2 * v + (1.0 - ADAM_B2) * jnp.square(g)
            m_hat = m / (1.0 - ADAM_B1 ** ADAM_STEP)
            v_hat = v / (1.0 - ADAM_B2 ** ADAM_STEP)
            outs[3 * i][...] = -ADAM_LR * (m_hat / (jnp.sqrt(v_hat) + ADAM_EPS) + ADAM_WD * w)
            outs[3 * i + 1][...] = m
            outs[3 * i + 2][...] = v

    flat = [a for t in triples for a in t]
    vmem = pl.BlockSpec(memory_space=pltpu.VMEM)
    res = pl.pallas_call(
        body, name="adamw", in_specs=[vmem] * len(flat), out_specs=[vmem] * (3 * n),
        out_shape=[jax.ShapeDtypeStruct(t[0].shape, F32) for t in triples for _ in range(3)],
        compiler_params=_params(48))(*flat)
    return [tuple(res[3 * i:3 * i + 3]) for i in range(n)]


def _shard_slab(w_in, w_uq, w_ukv):
    mixed = jnp.concatenate(
        [_pad_rows(w_uq.T, R_MIX), w_ukv.T, jnp.zeros((R_MIX, 1024 - Q_LORA - KV_LORA), F32)], axis=1)
    return jnp.concatenate([mixed, w_in.T, jnp.zeros((SLAB_ROWS - O_IN - R_IN, 1024), F32)], axis=0)


def _unpack_weights(slabs):
    uq = slabs[:, O_MIX:O_MIX + R_UQ, :Q_LORA].reshape(HEADS, QK_DIM, Q_LORA)
    wqt = jnp.pad(uq, ((0, 0), (0, HEAD_PAD - QK_DIM), (0, 0))).reshape(HEADS * HEAD_PAD, Q_LORA)
    ukv = slabs[:, O_MIX:O_MIX + R_MIX, Q_LORA:Q_LORA + KV_LORA].reshape(HEADS, 2, NOPE, KV_LORA)
    wkvt = ukv.transpose(1, 0, 2, 3).reshape(2 * MLA_W, KV_LORA)
    raw = slabs[:, O_IN:O_IN + R_IN].reshape(IN_W, D_MODEL)
    w1t = jnp.concatenate([raw[:768 + ROPE], jnp.zeros((128 - ROPE, D_MODEL), BF16), raw[768 + ROPE:]], axis=0)
    return w1t, wqt, wkvt


def _mixed_band(g_wqt, g_wkvt):
    uq = g_wqt.reshape(HEADS, HEAD_PAD, Q_LORA)[:, :QK_DIM].reshape(N_DEV, R_UQ, Q_LORA)
    uq = jnp.pad(uq, ((0, 0), (0, R_MIX - R_UQ), (0, 0)))
    ukv = g_wkvt.reshape(2, HEADS, NOPE, KV_LORA).transpose(1, 0, 2, 3).reshape(N_DEV, R_MIX, KV_LORA)
    return jnp.concatenate([uq, ukv, jnp.zeros((N_DEV, R_MIX, 1024 - Q_LORA - KV_LORA), F32)], axis=2)


def _rope_rows():
    half = ROPE // 2
    inv_freq = ROPE_THETA ** (-jnp.arange(half, dtype=F32) / half)
    zero, one = jnp.zeros((half,), F32), jnp.ones((half,), F32)
    rows = [jnp.concatenate(r) for r in (
        (inv_freq, inv_freq, zero, zero), (one, one, zero, zero), (-one, zero, zero, zero), (zero, one, zero, zero))]
    return jnp.stack(rows + [jnp.zeros((128,), F32)] * 4)


def _pad_rows(a, rows):
    return jnp.pad(a, ((0, rows - a.shape[0]), (0, 0)))


def kernel(x, positions, w_in, q_norm_g, w_uq, kv_norm_g, w_ukv, pool_w, pool_scale, w_out, ln_g, ln_b, loss_target, m_w_in, m_q_norm_g, m_w_uq, m_kv_norm_g, m_w_ukv, m_pool_w, m_pool_scale, m_w_out, m_ln_g, m_ln_b, v_w_in, v_q_norm_g, v_w_uq, v_kv_norm_g, v_w_ukv, v_pool_w, v_pool_scale, v_w_out, v_ln_g, v_ln_b):
    bb, ss, _ = x.shape
    tt = bb * ss
    atile = min(512, ss)
    nblk = ss // atile

    slab = _shard_slab(w_in, w_uq, w_ukv).astype(BF16)
    slabs = _all_gather(slab, "gather_weights").reshape(N_DEV, SLAB_ROWS, 1024)
    w1, wq, wkv = _unpack_weights(slabs)

    gq, gkv = q_norm_g.reshape(1, Q_LORA), kv_norm_g.reshape(1, KV_LORA)
    ps = pool_scale.reshape(1, POOL_W)
    pos_col = positions.reshape(bb, ss, 1)
    pos_blk = positions.reshape(bb, nblk, 1, atile)
    pmax = jnp.max(positions.reshape(bb, 2 * nblk, atile // 2), axis=-1).reshape(-1)
    pmin = jnp.min(positions.reshape(bb, 2 * nblk, atile // 2), axis=-1).reshape(-1)

    x2 = x.reshape(tt, D_MODEL)
    xq, xkv, ga, u, gb, qn, kvn, qcat, kcat, v, rc, rsa, rsb, wo = _fwd_in(
        x2, w1, gq, gkv, wq, wkv, pos_col.reshape(tt, 1), _rope_rows(), w_out.astype(BF16), atile)
    as3 = lambda a: a.reshape(bb, ss, a.shape[-1])
    qcat, kcat, v = as3(qcat), as3(kcat), as3(v)
    o, lse = _attn_fwd(qcat, kcat, v, pos_col, pos_blk, pmax, pmin, atile, atile)
    dz, ycat, do, dga, dgb, dpc, dl, loss_p, g_lng, g_lnb, g_ps, g_pw = _post(
        x, loss_target, o, as3(ga), as3(u), as3(gb), wo, pool_w.astype(BF16), ps, ln_g, ln_b, atile)

    bt = min(1024, tt)
    rows5 = lambda a: a.reshape(bb, HEADS, nblk, 1, atile)
    rc, rsa, rsb = as3(rc), as3(rsa), as3(rsb)
    dqp, dkn, dv, dkr = _attn_bwd(
        qcat, kcat, v, do, rows5(lse), rows5(dl), rc, rsa, rsb, pos_col, pos_blk, pmax, pmin, atile, atile)
    g_wq, g_wkn, g_wv = _grad_w(
        [(dqp.reshape(tt, HEADS * HEAD_PAD), qn), (dkn.reshape(tt, MLA_W), kvn), (dv.reshape(tt, MLA_W), kvn)],
        min(2 * bt, tt), "grad_w_uqkv")
    g_wkv = jnp.concatenate([g_wkn, g_wv], axis=0)
    grad_x, dh, g_gq, g_gkv = _bwd_mid(
        dqp, dkn, dv, dkr, as3(xq), as3(xkv), rc, rsa, rsb, wq, wkv, gq, gkv, dga, dgb, dpc, dz, w1, atile)
    small = jnp.concatenate(
        [_pad_rows(g_lng.reshape(8, 128), 8), _pad_rows(g_lnb.reshape(8, 128), 8), _pad_rows(g_gq.reshape(4, 128), 8),
         _pad_rows(g_gkv.reshape(2, 128), 8), _pad_rows(g_ps.reshape(4, 128), 8), g_pw.reshape(POOL_G * POOL_C, 128),
         loss_p], axis=0)
    dh2, half = dh.reshape(tt, IN_WP), D_MODEL // 2
    g_w1a, rs_mix, small = _grad_w([(dh2, x2)], bt, "grad_w_in_a", b_cols=(half, 0),
                                   carried=_blocks(_mixed_band(g_wq, g_wkv)), reduced=small)
    g_w1b, rs_in_a = _grad_w([(dh2, x2)], bt, "grad_w_in_b", b_cols=(half, 1), carried=_w_in_blocks(g_w1a))
    g_wo, rs_in_b = _grad_w([(ycat.reshape(tt, D_MODEL), dz.reshape(tt, D_MODEL))], bt, "grad_w_out",
                            carried=_w_in_blocks(g_w1b))
    rs_in = jnp.concatenate([rs_in_a, rs_in_b], axis=1)
    rs_out = _reduce_scatter(_blocks(g_wo.reshape(N_DEV, R_OUT, D_MODEL)), "reduce_scatter_w_out")
    loss = small[40 + POOL_G * POOL_C, 0]
    grads = {
        "w_in": rs_in[:R_IN],
        "q_norm_g": small[16:20].reshape(1, Q_LORA),
        "w_uq": rs_mix[:R_UQ, :Q_LORA],
        "kv_norm_g": small[24:26].reshape(1, KV_LORA),
        "w_ukv": rs_mix[:, Q_LORA:Q_LORA + KV_LORA].T,
        "pool_w": small[40:40 + POOL_G * POOL_C],
        "pool_scale": small[32:36].reshape(1, POOL_W),
        "w_out": rs_out,
        "ln_g": small[0:8].reshape(1, D_MODEL),
        "ln_b": small[8:16].reshape(1, D_MODEL),
    }
    transposed = ("w_in", "w_uq")

    names = ["w_in", "q_norm_g", "w_uq", "kv_norm_g", "w_ukv", "pool_w", "pool_scale", "w_out", "ln_g", "ln_b"]
    weights = dict(w_in=w_in, q_norm_g=q_norm_g, w_uq=w_uq, kv_norm_g=kv_norm_g, w_ukv=w_ukv, pool_w=pool_w,
                   pool_scale=pool_scale, w_out=w_out, ln_g=ln_g, ln_b=ln_b)
    moms = dict(w_in=(m_w_in, v_w_in), q_norm_g=(m_q_norm_g, v_q_norm_g), w_uq=(m_w_uq, v_w_uq),
                kv_norm_g=(m_kv_norm_g, v_kv_norm_g), w_ukv=(m_w_ukv, v_w_ukv), pool_w=(m_pool_w, v_pool_w),
                pool_scale=(m_pool_scale, v_pool_scale), w_out=(m_w_out, v_w_out), ln_g=(m_ln_g, v_ln_g),
                ln_b=(m_ln_b, v_ln_b))
    as2 = lambda a, n: a.T if n in transposed else a.reshape(grads[n].shape)
    upd = _adamw([(as2(weights[n], n), grads[n], as2(moms[n][0], n), as2(moms[n][1], n)) for n in names])
    shaped = lambda a, n: a.T if n in transposed else a.reshape(weights[n].shape)
    return (loss, grad_x,
            *[shaped(grads[n], n) for n in names],
            *[shaped(upd[i][0], n) for i, n in enumerate(names)],
            *[shaped(upd[i][1], n) for i, n in enumerate(names)],
            *[shaped(upd[i][2], n) for i, n in enumerate(names)])
```

```python
import jax
import jax.numpy as jnp
from jax import lax
from jax.experimental import pallas as pl
from jax.experimental.pallas import tpu as pltpu

F32 = jnp.float32
BF16 = jnp.bfloat16
MESH = pl.DeviceIdType.MESH

N_DEV = 8
D_MODEL = 1024
HEADS = 4
NOPE = 128
ROPE = 64
V_DIM = 128
QK_DIM = NOPE + ROPE
HEAD_PAD = 256
Q_LORA = 512
KV_LORA = 256
MLA_W = HEADS * V_DIM
POOL_W = 512
POOL_G = 4
POOL_C = 128
POOL_HALO = 16
IN_W = 2368
IN_WP = 2432
C_XQ, C_XKV, C_KR, C_GA, C_U, C_GB = 0, 512, 768, 896, 1408, 1920
ROPE_THETA = 10000.0
RMS_EPS = 1e-6
LN_EPS = 1e-5
ALPHA = 2.0 ** 0.25
SCALE = QK_DIM ** -0.5
LN2 = 0.6931471805599453
SCALE_LOG2 = SCALE / LN2
NEG = float(jnp.finfo(jnp.float32).min)

ADAM_LR = 0.001
ADAM_B1 = 0.9
ADAM_B2 = 0.999
ADAM_EPS = 1e-08
ADAM_WD = 0.01
ADAM_STEP = 10

R_OUT, R_MIX, R_UQ, R_IN = 128, 128, 96, 296
O_MIX, O_IN = 0, R_MIX
SLAB_ROWS = 432
R_IN_PAD = 304

V7X_VMEM_BYTES = 64 * 1024 * 1024
ATTN_FWD_HEADS_PER_STEP = 4
ATTN_BWD_HEADS_PER_STEP = 2


def _params(vmem_mb, semantics=None):
    assert vmem_mb * 1024 * 1024 < V7X_VMEM_BYTES
    return pltpu.CompilerParams(vmem_limit_bytes=vmem_mb * 1024 * 1024, dimension_semantics=semantics)


def _dot(a, b):
    return jnp.dot(a, b, preferred_element_type=F32)


def _dot_nt(a, b):
    return lax.dot_general(a, b, (((1,), (1,)), ((), ())), preferred_element_type=F32)


def _dot_tn(a, b):
    return lax.dot_general(a, b, (((0,), (0,)), ((), ())), preferred_element_type=F32)


def _rope_fwd(t, c, sa, sb):
    return t * c + pltpu.roll(t, 96, 1) * sa + pltpu.roll(t, 32, 1) * sb


def _rope_bwd(d, c, sa, sb):
    return d * c + pltpu.roll(d * sa, 32, 1) + pltpu.roll(d * sb, 96, 1)


def _silu_parts(g):
    sig = 0.5 * jnp.tanh(0.5 * g) + 0.5
    silu = g * sig
    return silu, sig + silu - silu * sig


def _pool_cnt(row0, rows):
    t = row0 + lax.broadcasted_iota(jnp.int32, (rows, POOL_W), 0)
    w = 2 << (lax.broadcasted_iota(jnp.int32, (rows, POOL_W), 1) // POOL_C)
    return jnp.minimum(t + 1, w).astype(F32)


def _pick_groups(s2, s4, s8, s16, lo, hi):
    return jnp.concatenate([s2[lo:hi, 0:128], s4[lo:hi, 128:256], s8[lo:hi, 256:384], s16[lo:hi, 384:512]], axis=1)


AG_SEMS = [pltpu.SemaphoreType.DMA((7,)), pltpu.SemaphoreType.DMA((7,)), pltpu.SemaphoreType.DMA]


def _ag_phases(x_ref, out_ref, send_sems, recv_sems, local_sem, sum_ref=None):
    m_per = x_ref.shape[0]
    x, y, c = lax.axis_index("x"), lax.axis_index("y"), lax.axis_index("c")
    me, sibling = (x, y, c), (x, y, 1 - c)
    chips = [(1 - x, y), (x, 1 - y), (1 - x, 1 - y)]

    def rows(px, py, pc):
        return out_ref.at[pl.ds((4 * px + 2 * py + pc) * m_per, m_per), :]

    def copy(k, block, to, src=None):
        return pltpu.make_async_remote_copy(
            src_ref=rows(*block) if src is None else src, dst_ref=rows(*block),
            send_sem=send_sems.at[k], recv_sem=recv_sems.at[k], device_id=to, device_id_type=MESH)

    def mine():
        return pltpu.make_async_copy(x_ref, rows(*me), local_sem)

    def first():
        return [copy(0, me, sibling, src=x_ref)] + [copy(1 + j, me, (*chip, c), src=x_ref) for j, chip in enumerate(chips)]

    def passed():
        return [copy(4 + j, (*chip, c), sibling) for j, chip in enumerate(chips)]

    def start():
        for cp in [mine()] + first():
            cp.start()

    def forward():
        for j, (chip, cp) in enumerate(zip(chips, passed())):
            copy(1 + j, (*chip, c), me).wait_recv()
            cp.start()

    def finish():
        copy(0, sibling, me).wait_recv()
        for j, chip in enumerate(chips):
            copy(4 + j, (*chip, 1 - c), me).wait_recv()
        for cp in first() + passed():
            cp.wait_send()
        mine().wait()
        if sum_ref is not None:
            acc = out_ref[pl.ds(0, m_per), :]
            for d in range(1, N_DEV):
                acc = acc + out_ref[pl.ds(d * m_per, m_per), :]
            sum_ref[...] = acc

    return start, forward, finish


def _all_gather(shard, name):
    m_per, n = shard.shape

    def body(x_ref, out_ref, *sems):
        for phase in _ag_phases(x_ref, out_ref, *sems):
            phase()

    vmem = pl.BlockSpec(memory_space=pltpu.VMEM)
    return pl.pallas_call(
        body, name=name, out_shape=jax.ShapeDtypeStruct((N_DEV * m_per, n), shard.dtype),
        in_specs=[vmem], out_specs=vmem, scratch_shapes=AG_SEMS, compiler_params=_params(32))(shard)


def _reduce_scatter(sc, name):
    def body(g_hbm, out_ref, *scratch):
        for phase in _rs_phases(sc, g_hbm, out_ref, *scratch):
            phase()

    return pl.pallas_call(
        body, name=name, out_shape=jax.ShapeDtypeStruct((sc.rows, sc.width), F32),
        in_specs=[pl.BlockSpec(memory_space=pl.ANY)], out_specs=pl.BlockSpec(memory_space=pltpu.VMEM),
        scratch_shapes=_rs_scratch(sc), compiler_params=_params(32))(sc.array)


class _Scattered:
    def __init__(self, array, rows, pieces, locate):
        self.array, self.rows, self.pieces, self.locate = array, rows, pieces, locate
        self.width = array.shape[-1]


def _blocks(g):
    return _Scattered(g, g.shape[1], ((0, g.shape[1]),), lambda ref, d, row, rows: ref.at[d])


def _w_in_blocks(g_w1t):
    def locate(ref, d, row, rows):
        r = R_IN * d + row
        return ref.at[pl.ds(pl.multiple_of(r + jnp.where(r >= C_KR + ROPE, 128 - ROPE, 0), 8), rows), :]

    cut = C_KR + ROPE - 2 * R_IN
    return _Scattered(g_w1t, R_IN_PAD, ((0, cut), (cut, R_IN - cut)), locate)


def _rs_scratch(sc):
    rr, ww, n = sc.rows, sc.width, 4 * len(sc.pieces)
    return [pltpu.VMEM((4, rr, ww), F32), pltpu.VMEM((4, rr, ww), F32),
            pltpu.VMEM((3, rr, ww), BF16), pltpu.VMEM((3, rr, ww), BF16),
            pltpu.SemaphoreType.DMA((n,)), pltpu.SemaphoreType.DMA((n,)), pltpu.SemaphoreType.DMA((n,)),
            pltpu.SemaphoreType.DMA((3,)), pltpu.SemaphoreType.DMA((3,))]


def _rs_phases(sc, g_hbm, out_ref, own_ref, recv1_ref, sendb_ref, recv2_ref, ld_sems, s1_send, s1_recv, s2_send, s2_recv):
    rr, ww = out_ref.shape
    chunk = next(c for c in (128, 80, 64, 48, 32, 16) if rr % c == 0)
    x, y, c = lax.axis_index("x"), lax.axis_index("y"), lax.axis_index("c")
    chips = [(1 - x, y), (x, 1 - y), (1 - x, 1 - y)]
    npieces = len(sc.pieces)
    filled = sum(rows for _, rows in sc.pieces)

    def pieces(d, buf, k):
        for p, (row, rows) in enumerate(sc.pieces):
            dst = buf.at[k] if (row, rows) == (0, rr) else buf.at[k, pl.ds(row, rows), :]
            yield k * npieces + p, sc.locate(g_hbm, d, row, rows), dst

    def loads():
        return [pltpu.make_async_copy(src, dst, ld_sems.at[s])
                for k in range(4) for s, src, dst in pieces(2 * k + c, own_ref, k)]

    def stage1():
        return [pltpu.make_async_remote_copy(
            src_ref=src, dst_ref=dst, send_sem=s1_send.at[s], recv_sem=s1_recv.at[s],
            device_id=(x, y, 1 - c), device_id_type=MESH)
            for k in range(4) for s, src, dst in pieces(2 * k + (1 - c), recv1_ref, k)]

    def stage2():
        return [pltpu.make_async_remote_copy(
            src_ref=sendb_ref.at[r], dst_ref=recv2_ref.at[r], send_sem=s2_send.at[r],
            recv_sem=s2_recv.at[r], device_id=(cx, cy, c), device_id_type=MESH) for r, (cx, cy) in enumerate(chips)]

    def start():
        if filled < rr:
            own_ref[:, filled:rr, :] = jnp.zeros((4, rr - filled, ww), F32)
            recv1_ref[:, filled:rr, :] = jnp.zeros((4, rr - filled, ww), F32)
        for cp in loads() + stage1():
            cp.start()

    def forward():
        for cp in loads():
            cp.wait()
        for cp in stage1():
            cp.wait_recv()
        sends = stage2()
        for r, (cx, cy) in enumerate(chips):
            kk = 2 * cx + cy

            def pack(i, carry, r=r, kk=kk):
                rows = pl.ds(pl.multiple_of(i * chunk, chunk), chunk)
                sendb_ref[r, rows, :] = (own_ref[kk, rows, :] + recv1_ref[kk, rows, :]).astype(BF16)
                return carry

            lax.fori_loop(0, rr // chunk, pack, 0)
            sends[r].start()

    def finish():
        for cp in stage2():
            cp.wait_recv()
        mine = 2 * x + y

        def total(i, carry):
            rows = pl.ds(pl.multiple_of(i * chunk, chunk), chunk)
            acc = own_ref[mine, rows, :] + recv1_ref[mine, rows, :]
            for r in range(3):
                acc = acc + recv2_ref[r, rows, :].astype(F32)
            out_ref[rows, :] = acc
            return carry

        lax.fori_loop(0, rr // chunk, total, 0)
        for cp in stage1() + stage2():
            cp.wait_send()

    return start, forward, finish


def _fwd_in(x2, w1, gq, gkv, wq, wkv, pos, rope_rows, shard, tm):
    tt = x2.shape[0]
    steps = tt // tm
    gathered_shape = (N_DEV * shard.shape[0], shard.shape[1])

    def body(x_ref, w1_ref, gq_ref, gkv_ref, wq_ref, wkv_ref, pos_ref, rr_ref, shard_ref,
             xq_ref, xkv_ref, ga_ref, u_ref, gb_ref, qn_ref, kvn_ref, qcat_ref, kcat_ref, v_ref,
             c_ref, sa_ref, sb_ref, all_ref, gathered, *sems):
        step = pl.program_id(0)
        start, forward, finish = _ag_phases(shard_ref, gathered, *sems)
        pl.when(step == 0)(start)
        pl.when(step == min(2, steps - 1))(forward)

        @pl.when(step == steps - 1)
        def _():
            finish()
            all_ref[...] = gathered[...]

        ang = pos_ref[...].astype(F32) * rr_ref[0:1, :]
        cos, sin = jnp.cos(ang), jnp.sin(ang)
        c, sa, sb = cos * rr_ref[1:2, :], sin * rr_ref[2:3, :], sin * rr_ref[3:4, :]
        h = _dot_nt(x_ref[...].astype(BF16), w1_ref[...])
        xq = h[:, C_XQ:C_XKV]
        xkv = h[:, C_XKV:C_KR]
        qn = (xq * lax.rsqrt(jnp.mean(xq * xq, axis=-1, keepdims=True) + RMS_EPS) * gq_ref[...]).astype(BF16)
        kvn = (xkv * lax.rsqrt(jnp.mean(xkv * xkv, axis=-1, keepdims=True) + RMS_EPS) * gkv_ref[...]).astype(BF16)
        q = _dot_nt(qn, wq_ref[...]) * SCALE_LOG2
        kv = _dot_nt(kvn, wkv_ref[...])
        kr = _rope_fwd(h[:, C_KR:C_GA], c, sa, sb).astype(BF16)
        c_ref[...] = c
        sa_ref[...] = sa
        sb_ref[...] = sb
        xq_ref[...] = xq
        xkv_ref[...] = xkv
        ga_ref[...] = h[:, C_GA:C_U]
        u_ref[...] = h[:, C_U:C_GB]
        gb_ref[...] = h[:, C_GB:IN_WP]
        qn_ref[...] = qn
        kvn_ref[...] = kvn
        for hd in range(HEADS):
            lo = hd * HEAD_PAD
            qcat_ref[:, lo:lo + NOPE] = q[:, lo:lo + NOPE].astype(BF16)
            qcat_ref[:, lo + NOPE:lo + HEAD_PAD] = _rope_fwd(q[:, lo + NOPE:lo + HEAD_PAD], c, sa, sb).astype(BF16)
            kcat_ref[:, lo:lo + NOPE] = kv[:, hd * NOPE:(hd + 1) * NOPE].astype(BF16)
            kcat_ref[:, lo + NOPE:lo + HEAD_PAD] = kr
        v_ref[...] = kv[:, MLA_W:].astype(BF16)

    def tile(w):
        return pl.BlockSpec((tm, w), lambda i: (i, 0))

    def whole(a):
        return pl.BlockSpec(a.shape, lambda i: (0, 0))

    outs = [(Q_LORA, F32), (KV_LORA, F32), (MLA_W, F32), (POOL_W, F32), (POOL_W, F32),
            (Q_LORA, BF16), (KV_LORA, BF16), (HEADS * HEAD_PAD, BF16), (HEADS * HEAD_PAD, BF16), (MLA_W, BF16),
            (128, F32), (128, F32), (128, F32)]
    return pl.pallas_call(
        body, name="fwd_in", grid=(steps,),
        in_specs=[tile(D_MODEL), whole(w1), whole(gq), whole(gkv), whole(wq), whole(wkv), tile(1), whole(rope_rows),
                  pl.BlockSpec(memory_space=pl.ANY)],
        out_specs=[tile(w) for w, _ in outs] + [pl.BlockSpec(gathered_shape, lambda i: (0, 0))],
        out_shape=[jax.ShapeDtypeStruct((tt, w), dt) for w, dt in outs]
        + [jax.ShapeDtypeStruct(gathered_shape, shard.dtype)],
        scratch_shapes=[pltpu.VMEM(gathered_shape, shard.dtype)] + AG_SEMS,
        compiler_params=_params(56, ("arbitrary",)))(x2, w1, gq, gkv, wq, wkv, pos, rope_rows, shard)


def _tile_cases(pmax_ref, pmin_ref, b, nhalf, qi, ki):
    q0, q1 = b * nhalf + 2 * qi, b * nhalf + 2 * qi + 1
    k0, k1 = b * nhalf + 2 * ki, b * nhalf + 2 * ki + 1
    needed = jnp.maximum(pmax_ref[q0], pmax_ref[q1]) >= jnp.minimum(pmin_ref[k0], pmin_ref[k1])
    visible = jnp.minimum(pmin_ref[q0], pmin_ref[q1]) >= jnp.maximum(pmax_ref[k0], pmax_ref[k1])
    stepped = pmax_ref[q0] < pmin_ref[k1]
    return needed, visible, stepped


def _attn_fwd(qcat, kcat, v, pos_col, pos_blk, pmax, pmin, tq, tk):
    bb, ss, _ = qcat.shape
    nq, nk = ss // tq, ss // tk
    lanes = 128
    hps = ATTN_FWD_HEADS_PER_STEP

    def body(pmax_ref, pmin_ref, q_ref, k_ref, v_ref, pc_ref, pb_ref, o_ref, lse_ref, m_s, acc_s):
        b, qi = pl.program_id(0), pl.program_id(2)
        m_s[...] = jnp.full(m_s.shape, NEG, F32)
        acc_s[...] = jnp.zeros(acc_s.shape, F32)

        def part(ki, masked, q_lo, q_n, k_n):
            qrows = slice(q_lo, q_lo + q_n)
            krows = pl.ds(pl.multiple_of(ki * tk, tk), k_n)
            if masked:
                mask = pc_ref[qrows, :] >= pb_ref[ki][:, :k_n]
            ones = jnp.ones((k_n, lanes), BF16)
            parts = []
            for hd in range(hps):
                qk = slice(hd * HEAD_PAD, (hd + 1) * HEAD_PAD)
                s = _dot_nt(q_ref[qrows, qk], k_ref[krows, qk])
                if masked:
                    s = jnp.where(mask, s, NEG)
                m_prev = m_s[hd, qrows, :]
                m_new = jnp.maximum(m_prev, jnp.max(s, axis=-1, keepdims=True))
                p = jnp.exp2(s - jnp.tile(m_new, (1, k_n // lanes)))
                a = jnp.exp2(m_prev - m_new)
                vv = jnp.concatenate([v_ref[krows, hd * V_DIM:(hd + 1) * V_DIM], ones], axis=1)
                parts.append((m_new, jnp.tile(a, (1, 2)) * acc_s[hd, qrows, :] + _dot(p.astype(BF16), vv)))
            for hd, (m_new, acc) in enumerate(parts):
                acc_s[hd, qrows, :] = acc
                m_s[hd, qrows, :] = m_new

        def step(ki, carry):
            needed, visible, stepped = _tile_cases(pmax_ref, pmin_ref, b, 2 * nq, qi, ki)

            @pl.when(needed & visible)
            def _():
                part(ki, False, 0, tq, tk)

            @pl.when(needed & ~visible & stepped)
            def _():
                part(ki, True, 0, tq // 2, tk // 2)
                part(ki, True, tq // 2, tq // 2, tk)

            @pl.when(needed & ~visible & ~stepped)
            def _():
                part(ki, True, 0, tq, tk)

            return carry

        lax.fori_loop(0, nk, step, 0)
        for hd in range(hps):
            acc = acc_s[hd]
            l = acc[:, V_DIM:]
            o_ref[:, hd * V_DIM:(hd + 1) * V_DIM] = acc[:, :V_DIM] / l
            lse_ref[hd] = (m_s[hd] + jnp.log2(l)).T[0:1, :]

    grid_spec = pltpu.PrefetchScalarGridSpec(
        num_scalar_prefetch=2, grid=(bb, HEADS // hps, nq),
        in_specs=[
            pl.BlockSpec((None, tq, hps * HEAD_PAD), lambda b, h, i, *_: (b, i, h)),
            pl.BlockSpec((None, ss, hps * HEAD_PAD), lambda b, h, i, *_: (b, 0, h)),
            pl.BlockSpec((None, ss, hps * V_DIM), lambda b, h, i, *_: (b, 0, h)),
            pl.BlockSpec((None, tq, 1), lambda b, h, i, *_: (b, i, 0)),
            pl.BlockSpec((None, nk, 1, tk), lambda b, h, i, *_: (b, 0, 0, 0)),
        ],
        out_specs=[
            pl.BlockSpec((None, tq, hps * V_DIM), lambda b, h, i, *_: (b, i, h)),
            pl.BlockSpec((None, hps, 1, tq), lambda b, h, i, *_: (b, h, 0, i)),
        ],
        scratch_shapes=[pltpu.VMEM((hps, tq, lanes), F32), pltpu.VMEM((hps, tq, 2 * V_DIM), F32)])
    return pl.pallas_call(
        body, name="attn_fwd", grid_spec=grid_spec,
        out_shape=[jax.ShapeDtypeStruct((bb, ss, MLA_W), F32), jax.ShapeDtypeStruct((bb, HEADS, 1, ss), F32)],
        compiler_params=_params(48, ("arbitrary", "arbitrary", "arbitrary")))(pmax, pmin, qcat, kcat, v, pos_col, pos_blk)


def _post(x, tgt, o, ga, u, gb, w_out, pool_wb, pool_scale, ln_g, ln_b, ts):
    bb, ss, _ = x.shape
    nt = ss // ts
    hb = ts // POOL_HALO

    def body(x_ref, tgt_ref, o_ref, ga_ref, u_ref, uh_ref, gb_ref, wo_ref, pw_ref, ps_ref, lg_ref, lb_ref,
             dz_ref, ycat_ref, do_ref, dga_ref, dgb_ref, dpc_ref, dl_ref, loss_ref, glg_ref, glb_ref, gps_ref, gpw_ref):
        step = pl.program_id(0)
        j = step % nt

        @pl.when(step == 0)
        def _():
            for r in (loss_ref, glg_ref, glb_ref, gps_ref, gpw_ref):
                r[...] = jnp.zeros(r.shape, F32)

        o, ga, u, gb = o_ref[...], ga_ref[...], u_ref[...], gb_ref[...]
        sa, dsa = _silu_parts(ga)
        sb, dsb = _silu_parts(gb)
        ext = jnp.concatenate([jnp.where(j > 0, uh_ref[...], 0.0), u], axis=0)
        s2 = ext + pltpu.roll(ext, 1, 0)
        s4 = s2 + pltpu.roll(s2, 2, 0)
        s8 = s4 + pltpu.roll(s4, 4, 0)
        s16 = s8 + pltpu.roll(s8, 8, 0)
        cnt = _pool_cnt(j * ts, ts)
        rcnt = 1.0 / cnt
        pooled = (_pick_groups(s2, s4, s8, s16, POOL_HALO, POOL_HALO + ts) * rcnt - u).astype(BF16)
        mixed = jnp.concatenate(
            [_dot(pooled[:, g * POOL_C:(g + 1) * POOL_C], pw_ref[g]) for g in range(POOL_G)], axis=1)
        ps = ps_ref[...]
        scaled = mixed * ps
        ycat = jnp.concatenate([o * sa, scaled * sb], axis=1).astype(BF16)
        z = ALPHA * x_ref[...] + _dot(ycat, wo_ref[...])
        mu = jnp.mean(z, axis=-1, keepdims=True)
        zc = z - mu
        rstd = lax.rsqrt(jnp.mean(zc * zc, axis=-1, keepdims=True) + LN_EPS)
        xhat = zc * rstd
        lg = lg_ref[...]
        diff = xhat * lg + lb_ref[...] - tgt_ref[...]
        loss = jnp.sum(diff * diff) * (0.5 / D_MODEL)
        glb = jnp.sum(diff, axis=0, keepdims=True) * (1.0 / D_MODEL)
        glg = jnp.sum(diff * xhat, axis=0, keepdims=True) * (1.0 / D_MODEL)
        dxh = diff * (lg * (1.0 / D_MODEL))
        dz = rstd * (dxh - jnp.mean(dxh, axis=-1, keepdims=True) - xhat * jnp.mean(dxh * xhat, axis=-1, keepdims=True))
        dycat = _dot_nt(dz.astype(BF16), wo_ref[...])
        dya, dyb = dycat[:, :MLA_W], dycat[:, MLA_W:]
        do = dya * sa
        doo = do * o
        dl = [jnp.sum(doo[:, hd * V_DIM:(hd + 1) * V_DIM].T, axis=0, keepdims=True) for hd in range(HEADS)]
        dga = (dya * o * dsa).astype(BF16)
        dgb = (dyb * scaled * dsb).astype(BF16)
        dscaled = dyb * sb
        gps = jnp.sum(dscaled * mixed, axis=0, keepdims=True)
        dmixed = (dscaled * ps).astype(BF16)
        groups = [slice(g * POOL_C, (g + 1) * POOL_C) for g in range(POOL_G)]
        gpw = [_dot_tn(pooled[:, cols], dmixed[:, cols]) for cols in groups]
        dpooled = [_dot_nt(dmixed[:, cols], pw_ref[g]) for g, cols in enumerate(groups)]
        ycat_ref[...] = ycat
        dz_ref[...] = dz
        do_ref[...] = do.astype(BF16)
        dga_ref[...] = dga
        dgb_ref[...] = dgb
        dpc_ref[...] = jnp.concatenate(dpooled, axis=1) * rcnt
        for hd in range(HEADS):
            dl_ref[hd] = dl[hd]
        for g in range(POOL_G):
            gpw_ref[g] += gpw[g]
        loss_ref[...] += loss
        glb_ref[...] += glb
        glg_ref[...] += glg
        gps_ref[...] += gps

    def tile(w):
        return pl.BlockSpec((None, ts, w), lambda i: (i // nt, i % nt, 0))

    def whole(a):
        nd = a.ndim
        return pl.BlockSpec(a.shape, lambda i: (0,) * nd)

    halo = pl.BlockSpec((None, POOL_HALO, POOL_W), lambda i: (i // nt, jnp.maximum((i % nt) * hb - 1, 0), 0))
    acc_shapes = [(8, 128), (1, D_MODEL), (1, D_MODEL), (1, POOL_W), (POOL_G, POOL_C, POOL_C)]
    tile_outs = [(D_MODEL, F32), (D_MODEL, BF16), (MLA_W, BF16), (MLA_W, BF16), (POOL_W, BF16), (POOL_W, F32)]
    return pl.pallas_call(
        body, name="post", grid=(bb * nt,),
        in_specs=[tile(D_MODEL), tile(D_MODEL), tile(MLA_W), tile(MLA_W), tile(POOL_W), halo, tile(POOL_W),
                  whole(w_out), whole(pool_wb), whole(pool_scale), whole(ln_g), whole(ln_b)],
        out_specs=[tile(w) for w, _ in tile_outs]
        + [pl.BlockSpec((None, HEADS, 1, ts), lambda i: (i // nt, 0, 0, i % nt))]
        + [pl.BlockSpec(s, lambda i, n=len(s): (0,) * n) for s in acc_shapes],
        out_shape=[jax.ShapeDtypeStruct((bb, ss, w), dt) for w, dt in tile_outs]
        + [jax.ShapeDtypeStruct((bb, HEADS, 1, ss), F32)]
        + [jax.ShapeDtypeStruct(s, F32) for s in acc_shapes],
        compiler_params=_params(48, ("arbitrary",)))(x, tgt, o, ga, u, u, gb, w_out, pool_wb, pool_scale, ln_g, ln_b)


def _attn_bwd(qcat, kcat, v, do, lse, dl, rc, rsa, rsb, pos_col, pos_blk, pmax, pmin, tq, tk):
    bb, ss, _ = qcat.shape
    nq, nk = ss // tq, ss // tk
    hps = ATTN_BWD_HEADS_PER_STEP

    def body(pmax_ref, pmin_ref, q_ref, k_ref, v_ref, do_ref, lse_ref, dl_ref, c_ref, sa_ref, sb_ref, pc_ref, pb_ref,
             dqp_ref, dkn_ref, dv_ref, dkr_ref, dq_s, dk_s, dv_s):
        b = pl.program_id(0)
        dq_s[...] = jnp.zeros(dq_s.shape, F32)

        def part(qi, ki, masked):
            krows = pl.ds(pl.multiple_of(ki * tk, tk), tk)
            qrows = pl.ds(pl.multiple_of(qi * tq, tq), tq)
            if masked:
                mask = pb_ref[qi] >= pc_ref[krows, :]
            parts = []
            for hd in range(hps):
                qk = slice(hd * HEAD_PAD, (hd + 1) * HEAD_PAD)
                vs = slice(hd * V_DIM, (hd + 1) * V_DIM)
                q = q_ref[qrows, qk]
                dd = do_ref[qrows, vs]
                st = _dot_nt(k_ref[krows, qk], q)
                if masked:
                    st = jnp.where(mask, st, NEG)
                pt = jnp.exp2(st - lse_ref[hd, qi])
                dpt = _dot_nt(v_ref[krows, vs], dd)
                dst = (pt * (dpt - dl_ref[hd, qi])).astype(BF16)
                parts.append((_dot(pt.astype(BF16), dd), _dot(dst, q), _dot_tn(dst, k_ref[krows, qk])))
            for hd, (dv_part, dk_part, dq_part) in enumerate(parts):
                dv_s[:, hd * V_DIM:(hd + 1) * V_DIM] += dv_part
                dk_s[:, hd * HEAD_PAD:(hd + 1) * HEAD_PAD] += dk_part
                dq_s[qrows, hd * HEAD_PAD:(hd + 1) * HEAD_PAD] += dq_part

        def kv_step(ki, carry):
            krows = pl.ds(pl.multiple_of(ki * tk, tk), tk)
            dk_s[...] = jnp.zeros(dk_s.shape, F32)
            dv_s[...] = jnp.zeros(dv_s.shape, F32)

            def q_step(qi, c2):
                needed, visible, _ = _tile_cases(pmax_ref, pmin_ref, b, 2 * nq, qi, ki)

                @pl.when(needed & visible)
                def _():
                    part(qi, ki, False)

                @pl.when(needed & ~visible)
                def _():
                    part(qi, ki, True)

                return c2

            lax.fori_loop(0, nq, q_step, 0)
            dkr = jnp.zeros((tk, HEAD_PAD - NOPE), F32)
            for hd in range(hps):
                lo = hd * HEAD_PAD
                dkn_ref[krows, hd * NOPE:(hd + 1) * NOPE] = (dk_s[:, lo:lo + NOPE] * LN2).astype(BF16)
                dkr = dkr + dk_s[:, lo + NOPE:lo + HEAD_PAD]
            dkr_ref[krows, :] = dkr * LN2
            dv_ref[krows, :] = dv_s[...].astype(BF16)
            return carry

        lax.fori_loop(0, nk, kv_step, 0)
        c, sa, sb = c_ref[...], sa_ref[...], sb_ref[...]
        for hd in range(hps):
            lo = hd * HEAD_PAD
            dqp_ref[:, lo:lo + NOPE] = (dq_s[:, lo:lo + NOPE] * SCALE).astype(BF16)
            dqp_ref[:, lo + NOPE:lo + HEAD_PAD] = _rope_bwd(
                dq_s[:, lo + NOPE:lo + HEAD_PAD] * SCALE, c, sa, sb).astype(BF16)

    def per_head(w):
        return pl.BlockSpec((None, ss, hps * w), lambda b, h, *_: (b, 0, h))

    def rows_of_head():
        return pl.BlockSpec((None, hps, nq, 1, tq), lambda b, h, *_: (b, h, 0, 0, 0))

    def per_batch(w):
        return pl.BlockSpec((None, ss, w), lambda b, h, *_: (b, 0, 0))

    grid_spec = pltpu.PrefetchScalarGridSpec(
        num_scalar_prefetch=2, grid=(bb, HEADS // hps),
        in_specs=[per_head(HEAD_PAD), per_head(HEAD_PAD), per_head(V_DIM), per_head(V_DIM),
                  rows_of_head(), rows_of_head(), per_batch(128), per_batch(128), per_batch(128), per_batch(1),
                  pl.BlockSpec((None, nq, 1, tq), lambda b, h, *_: (b, 0, 0, 0))],
        out_specs=[per_head(HEAD_PAD), per_head(NOPE), per_head(V_DIM),
                   pl.BlockSpec((None, None, ss, HEAD_PAD - NOPE), lambda b, h, *_: (b, h, 0, 0))],
        scratch_shapes=[pltpu.VMEM((ss, hps * HEAD_PAD), F32), pltpu.VMEM((tk, hps * HEAD_PAD), F32),
                        pltpu.VMEM((tk, hps * V_DIM), F32)])
    return pl.pallas_call(
        body, name="attn_bwd", grid_spec=grid_spec,
        out_shape=[jax.ShapeDtypeStruct((bb, ss, HEADS * HEAD_PAD), BF16),
                   jax.ShapeDtypeStruct((bb, ss, MLA_W), BF16),
                   jax.ShapeDtypeStruct((bb, ss, MLA_W), BF16),
                   jax.ShapeDtypeStruct((bb, HEADS // hps, ss, HEAD_PAD - NOPE), F32)],
        compiler_params=_params(56, ("arbitrary", "arbitrary")))(
            pmax, pmin, qcat, kcat, v, do, lse, dl, rc, rsa, rsb, pos_col, pos_blk)


def _bwd_mid(dqp, dkn, dv, dkr, xq, xkv, rc, rsa, rsb, wq, wkv, gq, gkv, dga, dgb, dpc, dz, w1, ts):
    bb, ss, _ = dz.shape
    nt = ss // ts
    hb = ts // POOL_HALO
    groups = dkr.shape[1]

    def body(dqp_ref, dkn_ref, dv_ref, dkr_ref, xq_ref, xkv_ref, c_ref, sa_ref, sb_ref, wq_ref, wkv_ref, gq_ref,
             gkv_ref, dga_ref, dgb_ref, dpc_ref, dph_ref, dz_ref, w1_ref,
             gx_ref, dh_ref, ggq_ref, ggkv_ref):
        step = pl.program_id(0)
        j = step % nt

        @pl.when(step == 0)
        def _():
            ggq_ref[...] = jnp.zeros(ggq_ref.shape, F32)
            ggkv_ref[...] = jnp.zeros(ggkv_ref.shape, F32)

        dkr = dkr_ref[0]
        for g in range(1, groups):
            dkr = dkr + dkr_ref[g]
        dkr = _rope_bwd(dkr, c_ref[...], sa_ref[...], sb_ref[...]).astype(BF16)

        def rms_bwd(x, g, dn):
            inv = lax.rsqrt(jnp.mean(x * x, axis=-1, keepdims=True) + RMS_EPS)
            xh = x * inv
            dxh = dn * g
            return inv * (dxh - xh * jnp.mean(dxh * xh, axis=-1, keepdims=True)), jnp.sum(dn * xh, axis=0, keepdims=True)

        dxq, ggq = rms_bwd(xq_ref[...], gq_ref[...], _dot(dqp_ref[...], wq_ref[...]))
        dkvn = _dot(dkn_ref[...], wkv_ref[:MLA_W, :]) + _dot(dv_ref[...], wkv_ref[MLA_W:, :])
        dxkv, ggkv = rms_bwd(xkv_ref[...], gkv_ref[...], dkvn)
        ggq_ref[...] += ggq
        ggkv_ref[...] += ggkv
        dpc = dpc_ref[...]
        n = ts + POOL_HALO
        ext = jnp.concatenate([dpc, jnp.where(j < nt - 1, dph_ref[...], 0.0)], axis=0)
        r2 = ext + pltpu.roll(ext, n - 1, 0)
        r4 = r2 + pltpu.roll(r2, n - 2, 0)
        r8 = r4 + pltpu.roll(r4, n - 4, 0)
        r16 = r8 + pltpu.roll(r8, n - 8, 0)
        du = _pick_groups(r2, r4, r8, r16, 0, ts) - dpc * _pool_cnt(j * ts, ts)
        dh = jnp.concatenate(
            [dxq.astype(BF16), dxkv.astype(BF16), dkr, dga_ref[...], du.astype(BF16), dgb_ref[...]], axis=1)
        dh_ref[...] = dh
        gx_ref[...] = ALPHA * dz_ref[...] + _dot(dh, w1_ref[...])

    def tile(w):
        return pl.BlockSpec((None, ts, w), lambda i: (i // nt, i % nt, 0))

    def whole(a):
        return pl.BlockSpec(a.shape, lambda i: (0, 0))

    halo = pl.BlockSpec((None, POOL_HALO, POOL_W),
                        lambda i: (i // nt, jnp.minimum((i % nt + 1) * hb, ss // POOL_HALO - 1), 0))
    return pl.pallas_call(
        body, name="bwd_mid", grid=(bb * nt,),
        in_specs=[tile(HEADS * HEAD_PAD), tile(MLA_W), tile(MLA_W),
                  pl.BlockSpec((None, groups, ts, HEAD_PAD - NOPE), lambda i: (i // nt, 0, i % nt, 0)),
                  tile(Q_LORA), tile(KV_LORA),
                  tile(128), tile(128), tile(128), whole(wq), whole(wkv), whole(gq), whole(gkv),
                  tile(MLA_W), tile(POOL_W), tile(POOL_W), halo, tile(D_MODEL), whole(w1)],
        out_specs=[tile(D_MODEL), tile(IN_WP),
                   pl.BlockSpec((1, Q_LORA), lambda i: (0, 0)), pl.BlockSpec((1, KV_LORA), lambda i: (0, 0))],
        out_shape=[jax.ShapeDtypeStruct((bb, ss, D_MODEL), F32), jax.ShapeDtypeStruct((bb, ss, IN_WP), BF16),
                   jax.ShapeDtypeStruct((1, Q_LORA), F32), jax.ShapeDtypeStruct((1, KV_LORA), F32)],
        compiler_params=_params(48, ("arbitrary",)))(
            dqp, dkn, dv, dkr, xq, xkv, rc, rsa, rsb, wq, wkv, gq, gkv, dga, dgb, dpc, dpc, dz, w1)


def _grad_w(pairs, bt, name, b_cols=None, carried=None, reduced=None):
    tt = pairs[0][0].shape[0]
    steps = tt // bt
    npairs = len(pairs)
    n_rs = len(_rs_scratch(carried)) if carried is not None else 0

    def body(*refs):
        ab, rest = refs[:2 * npairs], list(refs[2 * npairs:])
        g_hbm = rest.pop(0) if carried is not None else None
        part = rest.pop(0) if reduced is not None else None
        outs = [rest.pop(0) for _ in range(npairs)]
        phases = []
        if carried is not None:
            rs_out = rest.pop(0)
        if reduced is not None:
            sum_ref = rest.pop(0)
        if carried is not None:
            phases.append(_rs_phases(carried, g_hbm, rs_out, *rest[:n_rs]))
        if reduced is not None:
            gathered, *sems = rest[n_rs:]
            phases.append(_ag_phases(part, gathered, *sems, sum_ref=sum_ref))
        step = pl.program_id(0)

        @pl.when(step == 0)
        def _():
            for o in outs:
                o[...] = jnp.zeros(o.shape, F32)
            for start, _, _ in phases:
                start()

        parts = [_dot_tn(ab[2 * i][...].astype(BF16), ab[2 * i + 1][...].astype(BF16)) for i in range(npairs)]
        for out, part in zip(outs, parts):
            out[...] += part

        @pl.when(step == min(2, steps - 1))
        def _():
            for _, forward, _ in phases:
                forward()

        @pl.when(step == steps - 1)
        def _():
            for _, _, finish in phases:
                finish()

    in_specs, out_specs, out_shape = [], [], []
    for a, b in pairs:
        m, (n, col) = a.shape[1], b_cols or (b.shape[1], 0)
        in_specs += [pl.BlockSpec((bt, m), lambda i: (i, 0)), pl.BlockSpec((bt, n), lambda i, col=col: (i, col))]
        out_specs.append(pl.BlockSpec((m, n), lambda i: (0, 0)))
        out_shape.append(jax.ShapeDtypeStruct((m, n), F32))
    args = [t for p in pairs for t in p]
    scratch = []
    if carried is not None:
        in_specs.append(pl.BlockSpec(memory_space=pl.ANY))
        out_specs.append(pl.BlockSpec((carried.rows, carried.width), lambda i: (0, 0)))
        out_shape.append(jax.ShapeDtypeStruct((carried.rows, carried.width), F32))
        args.append(carried.array)
        scratch += _rs_scratch(carried)
    if reduced is not None:
        in_specs.append(pl.BlockSpec(memory_space=pl.ANY))
        out_specs.append(pl.BlockSpec(reduced.shape, lambda i: (0, 0)))
        out_shape.append(jax.ShapeDtypeStruct(reduced.shape, F32))
        args.append(reduced)
        scratch += [pltpu.VMEM((N_DEV * reduced.shape[0], reduced.shape[1]), F32)] + AG_SEMS
    return pl.pallas_call(
        body, name=name, grid=(steps,), in_specs=in_specs, out_specs=out_specs, out_shape=out_shape,
        scratch_shapes=scratch, compiler_params=_params(56, ("arbitrary",)))(*args)


def _adamw(triples):
    n = len(triples)

    def body(*refs):
        ins, outs = refs[:4 * n], refs[4 * n:]
        for i in range(n):
            w, g, m, v = (r[...] for r in ins[4 * i:4 * i + 4])
            m = ADAM_B1 * m + (1.0 - ADAM_B1) * g
            v = ADAM_B2 * v + (1.0 - ADAM_B2) * jnp.square(g)
            m_hat = m / (1.0 - ADAM_B1 ** ADAM_STEP)
            v_hat = v / (1.0 - ADAM_B2 ** ADAM_STEP)
            outs[3 * i][...] = -ADAM_LR * (m_hat / (jnp.sqrt(v_hat) + ADAM_EPS) + ADAM_WD * w)
            outs[3 * i + 1][...] = m
            outs[3 * i + 2][...] = v

    flat = [a for t in triples for a in t]
    vmem = pl.BlockSpec(memory_space=pltpu.VMEM)
    res = pl.pallas_call(
        body, name="adamw", in_specs=[vmem] * len(flat), out_specs=[vmem] * (3 * n),
        out_shape=[jax.ShapeDtypeStruct(t[0].shape, F32) for t in triples for _ in range(3)],
        compiler_params=_params(48))(*flat)
    return [tuple(res[3 * i:3 * i + 3]) for i in range(n)]


def _shard_slab(w_in, w_uq, w_ukv):
    mixed = jnp.concatenate(
        [_pad_rows(w_uq.T, R_MIX), w_ukv.T, jnp.zeros((R_MIX, 1024 - Q_LORA - KV_LORA), F32)], axis=1)
    return jnp.concatenate([mixed, w_in.T, jnp.zeros((SLAB_ROWS - O_IN - R_IN, 1024), F32)], axis=0)


def _unpack_weights(slabs):
    uq = slabs[:, O_MIX:O_MIX + R_UQ, :Q_LORA].reshape(HEADS, QK_DIM, Q_LORA)
    wqt = jnp.pad(uq, ((0, 0), (0, HEAD_PAD - QK_DIM), (0, 0))).reshape(HEADS * HEAD_PAD, Q_LORA)
    ukv = slabs[:, O_MIX:O_MIX + R_MIX, Q_LORA:Q_LORA + KV_LORA].reshape(HEADS, 2, NOPE, KV_LORA)
    wkvt = ukv.transpose(1, 0, 2, 3).reshape(2 * MLA_W, KV_LORA)
    raw = slabs[:, O_IN:O_IN + R_IN].reshape(IN_W, D_MODEL)
    w1t = jnp.concatenate([raw[:768 + ROPE], jnp.zeros((128 - ROPE, D_MODEL), BF16), raw[768 + ROPE:]], axis=0)
    return w1t, wqt, wkvt


def _mixed_band(g_wqt, g_wkvt):
    uq = g_wqt.reshape(HEADS, HEAD_PAD, Q_LORA)[:, :QK_DIM].reshape(N_DEV, R_UQ, Q_LORA)
    uq = jnp.pad(uq, ((0, 0), (0, R_MIX - R_UQ), (0, 0)))
    ukv = g_wkvt.reshape(2, HEADS, NOPE, KV_LORA).transpose(1, 0, 2, 3).reshape(N_DEV, R_MIX, KV_LORA)
    return jnp.concatenate([uq, ukv, jnp.zeros((N_DEV, R_MIX, 1024 - Q_LORA - KV_LORA), F32)], axis=2)


def _rope_rows():
    half = ROPE // 2
    inv_freq = ROPE_THETA ** (-jnp.arange(half, dtype=F32) / half)
    zero, one = jnp.zeros((half,), F32), jnp.ones((half,), F32)
    rows = [jnp.concatenate(r) for r in (
        (inv_freq, inv_freq, zero, zero), (one, one, zero, zero), (-one, zero, zero, zero), (zero, one, zero, zero))]
    return jnp.stack(rows + [jnp.zeros((128,), F32)] * 4)


def _pad_rows(a, rows):
    return jnp.pad(a, ((0, rows - a.shape[0]), (0, 0)))


def kernel(x, positions, w_in, q_norm_g, w_uq, kv_norm_g, w_ukv, pool_w, pool_scale, w_out, ln_g, ln_b, loss_target, m_w_in, m_q_norm_g, m_w_uq, m_kv_norm_g, m_w_ukv, m_pool_w, m_pool_scale, m_w_out, m_ln_g, m_ln_b, v_w_in, v_q_norm_g, v_w_uq, v_kv_norm_g, v_w_ukv, v_pool_w, v_pool_scale, v_w_out, v_ln_g, v_ln_b):
    bb, ss, _ = x.shape
    tt = bb * ss
    atile = min(512, ss)
    nblk = ss // atile

    slab = _shard_slab(w_in, w_uq, w_ukv).astype(BF16)
    slabs = _all_gather(slab, "gather_weights").reshape(N_DEV, SLAB_ROWS, 1024)
    w1, wq, wkv = _unpack_weights(slabs)

    gq, gkv = q_norm_g.reshape(1, Q_LORA), kv_norm_g.reshape(1, KV_LORA)
    ps = pool_scale.reshape(1, POOL_W)
    pos_col = positions.reshape(bb, ss, 1)
    pos_blk = positions.reshape(bb, nblk, 1, atile)
    pmax = jnp.max(positions.reshape(bb, 2 * nblk, atile // 2), axis=-1).reshape(-1)
    pmin = jnp.min(positions.reshape(bb, 2 * nblk, atile // 2), axis=-1).reshape(-1)

    x2 = x.reshape(tt, D_MODEL)
    xq, xkv, ga, u, gb, qn, kvn, qcat, kcat, v, rc, rsa, rsb, wo = _fwd_in(
        x2, w1, gq, gkv, wq, wkv, pos_col.reshape(tt, 1), _rope_rows(), w_out.astype(BF16), atile)
    as3 = lambda a: a.reshape(bb, ss, a.shape[-1])
    qcat, kcat, v = as3(qcat), as3(kcat), as3(v)
    o, lse = _attn_fwd(qcat, kcat, v, pos_col, pos_blk, pmax, pmin, atile, atile)
    dz, ycat, do, dga, dgb, dpc, dl, loss_p, g_lng, g_lnb, g_ps, g_pw = _post(
        x, loss_target, o, as3(ga), as3(u), as3(gb), wo, pool_w.astype(BF16), ps, ln_g, ln_b, atile)

    bt = min(1024, tt)
    rows5 = lambda a: a.reshape(bb, HEADS, nblk, 1, atile)
    rc, rsa, rsb = as3(rc), as3(rsa), as3(rsb)
    dqp, dkn, dv, dkr = _attn_bwd(
        qcat, kcat, v, do, rows5(lse), rows5(dl), rc, rsa, rsb, pos_col, pos_blk, pmax, pmin, atile, atile)
    g_wq, g_wkn, g_wv = _grad_w(
        [(dqp.reshape(tt, HEADS * HEAD_PAD), qn), (dkn.reshape(tt, MLA_W), kvn), (dv.reshape(tt, MLA_W), kvn)],
        min(2 * bt, tt), "grad_w_uqkv")
    g_wkv = jnp.concatenate([g_wkn, g_wv], axis=0)
    grad_x, dh, g_gq, g_gkv = _bwd_mid(
        dqp, dkn, dv, dkr, as3(xq), as3(xkv), rc, rsa, rsb, wq, wkv, gq, gkv, dga, dgb, dpc, dz, w1, atile)
    small = jnp.concatenate(
        [_pad_rows(g_lng.reshape(8, 128), 8), _pad_rows(g_lnb.reshape(8, 128), 8), _pad_rows(g_gq.reshape(4, 128), 8),
         _pad_rows(g_gkv.reshape(2, 128), 8), _pad_rows(g_ps.reshape(4, 128), 8), g_pw.reshape(POOL_G * POOL_C, 128),
         loss_p], axis=0)
    dh2, half = dh.reshape(tt, IN_WP), D_MODEL // 2
    g_w1a, rs_mix, small = _grad_w([(dh2, x2)], bt, "grad_w_in_a", b_cols=(half, 0),
                                   carried=_blocks(_mixed_band(g_wq, g_wkv)), reduced=small)
    g_w1b, rs_in_a = _grad_w([(dh2, x2)], bt, "grad_w_in_b", b_cols=(half, 1), carried=_w_in_blocks(g_w1a))
    g_wo, rs_in_b = _grad_w([(ycat.reshape(tt, D_MODEL), dz.reshape(tt, D_MODEL))], bt, "grad_w_out",
                            carried=_w_in_blocks(g_w1b))
    rs_in = jnp.concatenate([rs_in_a, rs_in_b], axis=1)
    rs_out = _reduce_scatter(_blocks(g_wo.reshape(N_DEV, R_OUT, D_MODEL)), "reduce_scatter_w_out")
    loss = small[40 + POOL_G * POOL_C, 0]
    grads = {
        "w_in": rs_in[:R_IN],
        "q_norm_g": small[16:20].reshape(1, Q_LORA),
        "w_uq": rs_mix[:R_UQ, :Q_LORA],
        "kv_norm_g": small[24:26].reshape(1, KV_LORA),
        "w_ukv": rs_mix[:, Q_LORA:Q_LORA + KV_LORA].T,
        "pool_w": small[40:40 + POOL_G * POOL_C],
        "pool_scale": small[32:36].reshape(1, POOL_W),
        "w_out": rs_out,
        "ln_g": small[0:8].reshape(1, D_MODEL),
        "ln_b": small[8:16].reshape(1, D_MODEL),
    }
    transposed = ("w_in", "w_uq")

    names = ["w_in", "q_norm_g", "w_uq", "kv_norm_g", "w_ukv", "pool_w", "pool_scale", "w_out", "ln_g", "ln_b"]
    weights = dict(w_in=w_in, q_norm_g=q_norm_g, w_uq=w_uq, kv_norm_g=kv_norm_g, w_ukv=w_ukv, pool_w=pool_w,
                   pool_scale=pool_scale, w_out=w_out, ln_g=ln_g, ln_b=ln_b)
    moms = dict(w_in=(m_w_in, v_w_in), q_norm_g=(m_q_norm_g, v_q_norm_g), w_uq=(m_w_uq, v_w_uq),
                kv_norm_g=(m_kv_norm_g, v_kv_norm_g), w_ukv=(m_w_ukv, v_w_ukv), pool_w=(m_pool_w, v_pool_w),
                pool_scale=(m_pool_scale, v_pool_scale), w_out=(m_w_out, v_w_out), ln_g=(m_ln_g, v_ln_g),
                ln_b=(m_ln_b, v_ln_b))
    as2 = lambda a, n: a.T if n in transposed else a.reshape(grads[n].shape)
    upd = _adamw([(as2(weights[n], n), grads[n], as2(moms[n][0], n), as2(moms[n][1], n)) for n in names])
    shaped = lambda a, n: a.T if n in transposed else a.reshape(weights[n].shape)
    return (loss, grad_x,
            *[shaped(grads[n], n) for n in names],
            *[shaped(upd[i][0], n) for i, n in enumerate(names)],
            *[shaped(upd[i][1], n) for i, n in enumerate(names)],
            *[shaped(upd[i][2], n) for i, n in enumerate(names)])
```

```python
import jax
import jax.numpy as jnp
from jax import lax
from jax.experimental import pallas as pl
from jax.experimental.pallas import tpu as pltpu

F32 = jnp.float32
BF16 = jnp.bfloat16
MESH = pl.DeviceIdType.MESH

N_DEV = 8
D_MODEL = 1024
HEADS = 4
NOPE = 128
ROPE = 64
V_DIM = 128
QK_DIM = NOPE + ROPE
HEAD_PAD = 256
Q_LORA = 512
KV_LORA = 256
MLA_W = HEADS * V_DIM
POOL_W = 512
POOL_G = 4
POOL_C = 128
POOL_HALO = 16
IN_W = 2368
IN_WP = 2432
C_XQ, C_XKV, C_KR, C_GA, C_U, C_GB = 0, 512, 768, 896, 1408, 1920
ROPE_THETA = 10000.0
RMS_EPS = 1e-6
LN_EPS = 1e-5
ALPHA = 2.0 ** 0.25
SCALE = QK_DIM ** -0.5
LN2 = 0.6931471805599453
SCALE_LOG2 = SCALE / LN2
NEG = float(jnp.finfo(jnp.float32).min)

ADAM_LR = 0.001
ADAM_B1 = 0.9
ADAM_B2 = 0.999
ADAM_EPS = 1e-08
ADAM_WD = 0.01
ADAM_STEP = 10

R_OUT, R_MIX, R_UQ, R_IN = 128, 128, 96, 296
O_MIX, O_IN = 0, R_MIX
SLAB_ROWS = 432
R_IN_PAD = 304

V7X_VMEM_BYTES = 64 * 1024 * 1024
ATTN_FWD_HEADS_PER_STEP = 4
ATTN_BWD_HEADS_PER_STEP = 2


def _params(vmem_mb, semantics=None):
    assert vmem_mb * 1024 * 1024 < V7X_VMEM_BYTES
    return pltpu.CompilerParams(vmem_limit_bytes=vmem_mb * 1024 * 1024, dimension_semantics=semantics)


def _dot(a, b):
    return jnp.dot(a, b, preferred_element_type=F32)


def _dot_nt(a, b):
    return lax.dot_general(a, b, (((1,), (1,)), ((), ())), preferred_element_type=F32)


def _dot_tn(a, b):
    return lax.dot_general(a, b, (((0,), (0,)), ((), ())), preferred_element_type=F32)


def _rope_fwd(t, c, sa, sb):
    return t * c + pltpu.roll(t, 96, 1) * sa + pltpu.roll(t, 32, 1) * sb


def _rope_bwd(d, c, sa, sb):
    return d * c + pltpu.roll(d * sa, 32, 1) + pltpu.roll(d * sb, 96, 1)


def _silu_parts(g):
    sig = 0.5 * jnp.tanh(0.5 * g) + 0.5
    silu = g * sig
    return silu, sig + silu - silu * sig


def _pool_cnt(row0, rows):
    t = row0 + lax.broadcasted_iota(jnp.int32, (rows, POOL_W), 0)
    w = 2 << (lax.broadcasted_iota(jnp.int32, (rows, POOL_W), 1) // POOL_C)
    return jnp.minimum(t + 1, w).astype(F32)


def _pick_groups(s2, s4, s8, s16, lo, hi):
    return jnp.concatenate([s2[lo:hi, 0:128], s4[lo:hi, 128:256], s8[lo:hi, 256:384], s16[lo:hi, 384:512]], axis=1)


AG_SEMS = [pltpu.SemaphoreType.DMA((7,)), pltpu.SemaphoreType.DMA((7,)), pltpu.SemaphoreType.DMA]


def _ag_phases(x_ref, out_ref, send_sems, recv_sems, local_sem, sum_ref=None):
    m_per = x_ref.shape[0]
    x, y, c = lax.axis_index("x"), lax.axis_index("y"), lax.axis_index("c")
    me, sibling = (x, y, c), (x, y, 1 - c)
    chips = [(1 - x, y), (x, 1 - y), (1 - x, 1 - y)]

    def rows(px, py, pc):
        return out_ref.at[pl.ds((4 * px + 2 * py + pc) * m_per, m_per), :]

    def copy(k, block, to, src=None):
        return pltpu.make_async_remote_copy(
            src_ref=rows(*block) if src is None else src, dst_ref=rows(*block),
            send_sem=send_sems.at[k], recv_sem=recv_sems.at[k], device_id=to, device_id_type=MESH)

    def mine():
        return pltpu.make_async_copy(x_ref, rows(*me), local_sem)

    def first():
        return [copy(0, me, sibling, src=x_ref)] + [copy(1 + j, me, (*chip, c), src=x_ref) for j, chip in enumerate(chips)]

    def passed():
        return [copy(4 + j, (*chip, c), sibling) for j, chip in enumerate(chips)]

    def start():
        for cp in [mine()] + first():
            cp.start()

    def forward():
        for j, (chip, cp) in enumerate(zip(chips, passed())):
            copy(1 + j, (*chip, c), me).wait_recv()
            cp.start()

    def finish():
        copy(0, sibling, me).wait_recv()
        for j, chip in enumerate(chips):
            copy(4 + j, (*chip, 1 - c), me).wait_recv()
        for cp in first() + passed():
            cp.wait_send()
        mine().wait()
        if sum_ref is not None:
            acc = out_ref[pl.ds(0, m_per), :]
            for d in range(1, N_DEV):
                acc = acc + out_ref[pl.ds(d * m_per, m_per), :]
            sum_ref[...] = acc

    return start, forward, finish


def _all_gather(shard, name):
    m_per, n = shard.shape

    def body(x_ref, out_ref, *sems):
        for phase in _ag_phases(x_ref, out_ref, *sems):
            phase()

    vmem = pl.BlockSpec(memory_space=pltpu.VMEM)
    return pl.pallas_call(
        body, name=name, out_shape=jax.ShapeDtypeStruct((N_DEV * m_per, n), shard.dtype),
        in_specs=[vmem], out_specs=vmem, scratch_shapes=AG_SEMS, compiler_params=_params(32))(shard)


def _reduce_scatter(sc, name):
    def body(g_hbm, out_ref, *scratch):
        for phase in _rs_phases(sc, g_hbm, out_ref, *scratch):
            phase()

    return pl.pallas_call(
        body, name=name, out_shape=jax.ShapeDtypeStruct((sc.rows, sc.width), F32),
        in_specs=[pl.BlockSpec(memory_space=pl.ANY)], out_specs=pl.BlockSpec(memory_space=pltpu.VMEM),
        scratch_shapes=_rs_scratch(sc), compiler_params=_params(32))(sc.array)


class _Scattered:
    def __init__(self, array, rows, pieces, locate):
        self.array, self.rows, self.pieces, self.locate = array, rows, pieces, locate
        self.width = array.shape[-1]


def _blocks(g):
    return _Scattered(g, g.shape[1], ((0, g.shape[1]),), lambda ref, d, row, rows: ref.at[d])


def _w_in_blocks(g_w1t):
    def locate(ref, d, row, rows):
        r = R_IN * d + row
        return ref.at[pl.ds(pl.multiple_of(r + jnp.where(r >= C_KR + ROPE, 128 - ROPE, 0), 8), rows), :]

    cut = C_KR + ROPE - 2 * R_IN
    return _Scattered(g_w1t, R_IN_PAD, ((0, cut), (cut, R_IN - cut)), locate)


def _rs_scratch(sc):
    rr, ww, n = sc.rows, sc.width, 4 * len(sc.pieces)
    return [pltpu.VMEM((4, rr, ww), F32), pltpu.VMEM((4, rr, ww), F32),
            pltpu.VMEM((3, rr, ww), BF16), pltpu.VMEM((3, rr, ww), BF16),
            pltpu.SemaphoreType.DMA((n,)), pltpu.SemaphoreType.DMA((n,)), pltpu.SemaphoreType.DMA((n,)),
            pltpu.SemaphoreType.DMA((3,)), pltpu.SemaphoreType.DMA((3,))]


def _rs_phases(sc, g_hbm, out_ref, own_ref, recv1_ref, sendb_ref, recv2_ref, ld_sems, s1_send, s1_recv, s2_send, s2_recv):
    rr, ww = out_ref.shape
    chunk = next(c for c in (128, 80, 64, 48, 32, 16) if rr % c == 0)
    x, y, c = lax.axis_index("x"), lax.axis_index("y"), lax.axis_index("c")
    chips = [(1 - x, y), (x, 1 - y), (1 - x, 1 - y)]
    npieces = len(sc.pieces)
    filled = sum(rows for _, rows in sc.pieces)

    def pieces(d, buf, k):
        for p, (row, rows) in enumerate(sc.pieces):
            dst = buf.at[k] if (row, rows) == (0, rr) else buf.at[k, pl.ds(row, rows), :]
            yield k * npieces + p, sc.locate(g_hbm, d, row, rows), dst

    def loads():
        return [pltpu.make_async_copy(src, dst, ld_sems.at[s])
                for k in range(4) for s, src, dst in pieces(2 * k + c, own_ref, k)]

    def stage1():
        return [pltpu.make_async_remote_copy(
            src_ref=src, dst_ref=dst, send_sem=s1_send.at[s], recv_sem=s1_recv.at[s],
            device_id=(x, y, 1 - c), device_id_type=MESH)
            for k in range(4) for s, src, dst in pieces(2 * k + (1 - c), recv1_ref, k)]

    def stage2():
        return [pltpu.make_async_remote_copy(
            src_ref=sendb_ref.at[r], dst_ref=recv2_ref.at[r], send_sem=s2_send.at[r],
            recv_sem=s2_recv.at[r], device_id=(cx, cy, c), device_id_type=MESH) for r, (cx, cy) in enumerate(chips)]

    def start():
        if filled < rr:
            own_ref[:, filled:rr, :] = jnp.zeros((4, rr - filled, ww), F32)
            recv1_ref[:, filled:rr, :] = jnp.zeros((4, rr - filled, ww), F32)
        for cp in loads() + stage1():
            cp.start()

    def forward():
        for cp in loads():
            cp.wait()
        for cp in stage1():
            cp.wait_recv()
        sends = stage2()
        for r, (cx, cy) in enumerate(chips):
            kk = 2 * cx + cy

            def pack(i, carry, r=r, kk=kk):
                rows = pl.ds(pl.multiple_of(i * chunk, chunk), chunk)
                sendb_ref[r, rows, :] = (own_ref[kk, rows, :] + recv1_ref[kk, rows, :]).astype(BF16)
                return carry

            lax.fori_loop(0, rr // chunk, pack, 0)
            sends[r].start()

    def finish():
        for cp in stage2():
            cp.wait_recv()
        mine = 2 * x + y

        def total(i, carry):
            rows = pl.ds(pl.multiple_of(i * chunk, chunk), chunk)
            acc = own_ref[mine, rows, :] + recv1_ref[mine, rows, :]
            for r in range(3):
                acc = acc + recv2_ref[r, rows, :].astype(F32)
            out_ref[rows, :] = acc
            return carry

        lax.fori_loop(0, rr // chunk, total, 0)
        for cp in stage1() + stage2():
            cp.wait_send()

    return start, forward, finish


def _fwd_in(x2, w1, gq, gkv, wq, wkv, pos, rope_rows, shard, tm):
    tt = x2.shape[0]
    steps = tt // tm
    gathered_shape = (N_DEV * shard.shape[0], shard.shape[1])

    def body(x_ref, w1_ref, gq_ref, gkv_ref, wq_ref, wkv_ref, pos_ref, rr_ref, shard_ref,
             xq_ref, xkv_ref, ga_ref, u_ref, gb_ref, qn_ref, kvn_ref, qcat_ref, kcat_ref, v_ref,
             c_ref, sa_ref, sb_ref, all_ref, gathered, *sems):
        step = pl.program_id(0)
        start, forward, finish = _ag_phases(shard_ref, gathered, *sems)
        pl.when(step == 0)(start)
        pl.when(step == min(2, steps - 1))(forward)

        @pl.when(step == steps - 1)
        def _():
            finish()
            all_ref[...] = gathered[...]

        ang = pos_ref[...].astype(F32) * rr_ref[0:1, :]
        cos, sin = jnp.cos(ang), jnp.sin(ang)
        c, sa, sb = cos * rr_ref[1:2, :], sin * rr_ref[2:3, :], sin * rr_ref[3:4, :]
        h = _dot_nt(x_ref[...].astype(BF16), w1_ref[...])
        xq = h[:, C_XQ:C_XKV]
        xkv = h[:, C_XKV:C_KR]
        qn = (xq * lax.rsqrt(jnp.mean(xq * xq, axis=-1, keepdims=True) + RMS_EPS) * gq_ref[...]).astype(BF16)
        kvn = (xkv * lax.rsqrt(jnp.mean(xkv * xkv, axis=-1, keepdims=True) + RMS_EPS) * gkv_ref[...]).astype(BF16)
        q = _dot_nt(qn, wq_ref[...]) * SCALE_LOG2
        kv = _dot_nt(kvn, wkv_ref[...])
        kr = _rope_fwd(h[:, C_KR:C_GA], c, sa, sb).astype(BF16)
        c_ref[...] = c
        sa_ref[...] = sa
        sb_ref[...] = sb
        xq_ref[...] = xq
        xkv_ref[...] = xkv
        ga_ref[...] = h[:, C_GA:C_U]
        u_ref[...] = h[:, C_U:C_GB]
        gb_ref[...] = h[:, C_GB:IN_WP]
        qn_ref[...] = qn
        kvn_ref[...] = kvn
        for hd in range(HEADS):
            lo = hd * HEAD_PAD
            qcat_ref[:, lo:lo + NOPE] = q[:, lo:lo + NOPE].astype(BF16)
            qcat_ref[:, lo + NOPE:lo + HEAD_PAD] = _rope_fwd(q[:, lo + NOPE:lo + HEAD_PAD], c, sa, sb).astype(BF16)
            kcat_ref[:, lo:lo + NOPE] = kv[:, hd * NOPE:(hd + 1) * NOPE].astype(BF16)
            kcat_ref[:, lo + NOPE:lo + HEAD_PAD] = kr
        v_ref[...] = kv[:, MLA_W:].astype(BF16)

    def tile(w):
        return pl.BlockSpec((tm, w), lambda i: (i, 0))

    def whole(a):
        return pl.BlockSpec(a.shape, lambda i: (0, 0))

    outs = [(Q_LORA, F32), (KV_LORA, F32), (MLA_W, F32), (POOL_W, F32), (POOL_W, F32),
            (Q_LORA, BF16), (KV_LORA, BF16), (HEADS * HEAD_PAD, BF16), (HEADS * HEAD_PAD, BF16), (MLA_W, BF16),
            (128, F32), (128, F32), (128, F32)]
    return pl.pallas_call(
        body, name="fwd_in", grid=(steps,),
        in_specs=[tile(D_MODEL), whole(w1), whole(gq), whole(gkv), whole(wq), whole(wkv), tile(1), whole(rope_rows),
                  pl.BlockSpec(memory_space=pl.ANY)],
        out_specs=[tile(w) for w, _ in outs] + [pl.BlockSpec(gathered_shape, lambda i: (0, 0))],
        out_shape=[jax.ShapeDtypeStruct((tt, w), dt) for w, dt in outs]
        + [jax.ShapeDtypeStruct(gathered_shape, shard.dtype)],
        scratch_shapes=[pltpu.VMEM(gathered_shape, shard.dtype)] + AG_SEMS,
        compiler_params=_params(56, ("arbitrary",)))(x2, w1, gq, gkv, wq, wkv, pos, rope_rows, shard)


def _tile_cases(pmax_ref, pmin_ref, b, nhalf, qi, ki):
    q0, q1 = b * nhalf + 2 * qi, b * nhalf + 2 * qi + 1
    k0, k1 = b * nhalf + 2 * ki, b * nhalf + 2 * ki + 1
    needed = jnp.maximum(pmax_ref[q0], pmax_ref[q1]) >= jnp.minimum(pmin_ref[k0], pmin_ref[k1])
    visible = jnp.minimum(pmin_ref[q0], pmin_ref[q1]) >= jnp.maximum(pmax_ref[k0], pmax_ref[k1])
    stepped = pmax_ref[q0] < pmin_ref[k1]
    return needed, visible, stepped


def _attn_fwd(qcat, kcat, v, pos_col, pos_blk, pmax, pmin, tq, tk):
    bb, ss, _ = qcat.shape
    nq, nk = ss // tq, ss // tk
    lanes = 128
    hps = ATTN_FWD_HEADS_PER_STEP

    def body(pmax_ref, pmin_ref, q_ref, k_ref, v_ref, pc_ref, pb_ref, o_ref, lse_ref, m_s, acc_s):
        b, qi = pl.program_id(0), pl.program_id(2)
        m_s[...] = jnp.full(m_s.shape, NEG, F32)
        acc_s[...] = jnp.zeros(acc_s.shape, F32)

        def part(ki, masked, q_lo, q_n, k_n):
            qrows = slice(q_lo, q_lo + q_n)
            krows = pl.ds(pl.multiple_of(ki * tk, tk), k_n)
            if masked:
                mask = pc_ref[qrows, :] >= pb_ref[ki][:, :k_n]
            ones = jnp.ones((k_n, lanes), BF16)
            parts = []
            for hd in range(hps):
                qk = slice(hd * HEAD_PAD, (hd + 1) * HEAD_PAD)
                s = _dot_nt(q_ref[qrows, qk], k_ref[krows, qk])
                if masked:
                    s = jnp.where(mask, s, NEG)
                m_prev = m_s[hd, qrows, :]
                m_new = jnp.maximum(m_prev, jnp.max(s, axis=-1, keepdims=True))
                p = jnp.exp2(s - jnp.tile(m_new, (1, k_n // lanes)))
                a = jnp.exp2(m_prev - m_new)
                vv = jnp.concatenate([v_ref[krows, hd * V_DIM:(hd + 1) * V_DIM], ones], axis=1)
                parts.append((m_new, jnp.tile(a, (1, 2)) * acc_s[hd, qrows, :] + _dot(p.astype(BF16), vv)))
            for hd, (m_new, acc) in enumerate(parts):
                acc_s[hd, qrows, :] = acc
                m_s[hd, qrows, :] = m_new

        def step(ki, carry):
            needed, visible, stepped = _tile_cases(pmax_ref, pmin_ref, b, 2 * nq, qi, ki)

            @pl.when(needed & visible)
            def _():
                part(ki, False, 0, tq, tk)

            @pl.when(needed & ~visible & stepped)
            def _():
                part(ki, True, 0, tq // 2, tk // 2)
                part(ki, True, tq // 2, tq // 2, tk)

            @pl.when(needed & ~visible & ~stepped)
            def _():
                part(ki, True, 0, tq, tk)

            return carry

        lax.fori_loop(0, nk, step, 0)
        for hd in range(hps):
            acc = acc_s[hd]
            l = acc[:, V_DIM:]
            o_ref[:, hd * V_DIM:(hd + 1) * V_DIM] = acc[:, :V_DIM] / l
            lse_ref[hd] = (m_s[hd] + jnp.log2(l)).T[0:1, :]

    grid_spec = pltpu.PrefetchScalarGridSpec(
        num_scalar_prefetch=2, grid=(bb, HEADS // hps, nq),
        in_specs=[
            pl.BlockSpec((None, tq, hps * HEAD_PAD), lambda b, h, i, *_: (b, i, h)),
            pl.BlockSpec((None, ss, hps * HEAD_PAD), lambda b, h, i, *_: (b, 0, h)),
            pl.BlockSpec((None, ss, hps * V_DIM), lambda b, h, i, *_: (b, 0, h)),
            pl.BlockSpec((None, tq, 1), lambda b, h, i, *_: (b, i, 0)),
            pl.BlockSpec((None, nk, 1, tk), lambda b, h, i, *_: (b, 0, 0, 0)),
        ],
        out_specs=[
            pl.BlockSpec((None, tq, hps * V_DIM), lambda b, h, i, *_: (b, i, h)),
            pl.BlockSpec((None, hps, 1, tq), lambda b, h, i, *_: (b, h, 0, i)),
        ],
        scratch_shapes=[pltpu.VMEM((hps, tq, lanes), F32), pltpu.VMEM((hps, tq, 2 * V_DIM), F32)])
    return pl.pallas_call(
        body, name="attn_fwd", grid_spec=grid_spec,
        out_shape=[jax.ShapeDtypeStruct((bb, ss, MLA_W), F32), jax.ShapeDtypeStruct((bb, HEADS, 1, ss), F32)],
        compiler_params=_params(48, ("arbitrary", "arbitrary", "arbitrary")))(pmax, pmin, qcat, kcat, v, pos_col, pos_blk)


def _post(x, tgt, o, ga, u, gb, w_out, pool_wb, pool_scale, ln_g, ln_b, ts):
    bb, ss, _ = x.shape
    nt = ss // ts
    hb = ts // POOL_HALO

    def body(x_ref, tgt_ref, o_ref, ga_ref, u_ref, uh_ref, gb_ref, wo_ref, pw_ref, ps_ref, lg_ref, lb_ref,
             dz_ref, ycat_ref, do_ref, dga_ref, dgb_ref, dpc_ref, dl_ref, loss_ref, glg_ref, glb_ref, gps_ref, gpw_ref):
        step = pl.program_id(0)
        j = step % nt

        @pl.when(step == 0)
        def _():
            for r in (loss_ref, glg_ref, glb_ref, gps_ref, gpw_ref):
                r[...] = jnp.zeros(r.shape, F32)

        o, ga, u, gb = o_ref[...], ga_ref[...], u_ref[...], gb_ref[...]
        sa, dsa = _silu_parts(ga)
        sb, dsb = _silu_parts(gb)
        ext = jnp.concatenate([jnp.where(j > 0, uh_ref[...], 0.0), u], axis=0)
        s2 = ext + pltpu.roll(ext, 1, 0)
        s4 = s2 + pltpu.roll(s2, 2, 0)
        s8 = s4 + pltpu.roll(s4, 4, 0)
        s16 = s8 + pltpu.roll(s8, 8, 0)
        cnt = _pool_cnt(j * ts, ts)
        rcnt = 1.0 / cnt
        pooled = (_pick_groups(s2, s4, s8, s16, POOL_HALO, POOL_HALO + ts) * rcnt - u).astype(BF16)
        mixed = jnp.concatenate(
            [_dot(pooled[:, g * POOL_C:(g + 1) * POOL_C], pw_ref[g]) for g in range(POOL_G)], axis=1)
        ps = ps_ref[...]
        scaled = mixed * ps
        ycat = jnp.concatenate([o * sa, scaled * sb], axis=1).astype(BF16)
        z = ALPHA * x_ref[...] + _dot(ycat, wo_ref[...])
        mu = jnp.mean(z, axis=-1, keepdims=True)
        zc = z - mu
        rstd = lax.rsqrt(jnp.mean(zc * zc, axis=-1, keepdims=True) + LN_EPS)
        xhat = zc * rstd
        lg = lg_ref[...]
        diff = xhat * lg + lb_ref[...] - tgt_ref[...]
        loss = jnp.sum(diff * diff) * (0.5 / D_MODEL)
        glb = jnp.sum(diff, axis=0, keepdims=True) * (1.0 / D_MODEL)
        glg = jnp.sum(diff * xhat, axis=0, keepdims=True) * (1.0 / D_MODEL)
        dxh = diff * (lg * (1.0 / D_MODEL))
        dz = rstd * (dxh - jnp.mean(dxh, axis=-1, keepdims=True) - xhat * jnp.mean(dxh * xhat, axis=-1, keepdims=True))
        dycat = _dot_nt(dz.astype(BF16), wo_ref[...])
        dya, dyb = dycat[:, :MLA_W], dycat[:, MLA_W:]
        do = dya * sa
        doo = do * o
        dl = [jnp.sum(doo[:, hd * V_DIM:(hd + 1) * V_DIM].T, axis=0, keepdims=True) for hd in range(HEADS)]
        dga = (dya * o * dsa).astype(BF16)
        dgb = (dyb * scaled * dsb).astype(BF16)
        dscaled = dyb * sb
        gps = jnp.sum(dscaled * mixed, axis=0, keepdims=True)
        dmixed = (dscaled * ps).astype(BF16)
        groups = [slice(g * POOL_C, (g + 1) * POOL_C) for g in range(POOL_G)]
        gpw = [_dot_tn(pooled[:, cols], dmixed[:, cols]) for cols in groups]
        dpooled = [_dot_nt(dmixed[:, cols], pw_ref[g]) for g, cols in enumerate(groups)]
        ycat_ref[...] = ycat
        dz_ref[...] = dz
        do_ref[...] = do.astype(BF16)
        dga_ref[...] = dga
        dgb_ref[...] = dgb
        dpc_ref[...] = jnp.concatenate(dpooled, axis=1) * rcnt
        for hd in range(HEADS):
            dl_ref[hd] = dl[hd]
        for g in range(POOL_G):
            gpw_ref[g] += gpw[g]
        loss_ref[...] += loss
        glb_ref[...] += glb
        glg_ref[...] += glg
        gps_ref[...] += gps

    def tile(w):
        return pl.BlockSpec((None, ts, w), lambda i: (i // nt, i % nt, 0))

    def whole(a):
        nd = a.ndim
        return pl.BlockSpec(a.shape, lambda i: (0,) * nd)

    halo = pl.BlockSpec((None, POOL_HALO, POOL_W), lambda i: (i // nt, jnp.maximum((i % nt) * hb - 1, 0), 0))
    acc_shapes = [(8, 128), (1, D_MODEL), (1, D_MODEL), (1, POOL_W), (POOL_G, POOL_C, POOL_C)]
    tile_outs = [(D_MODEL, F32), (D_MODEL, BF16), (MLA_W, BF16), (MLA_W, BF16), (POOL_W, BF16), (POOL_W, F32)]
    return pl.pallas_call(
        body, name="post", grid=(bb * nt,),
        in_specs=[tile(D_MODEL), tile(D_MODEL), tile(MLA_W), tile(MLA_W), tile(POOL_W), halo, tile(POOL_W),
                  whole(w_out), whole(pool_wb), whole(pool_scale), whole(ln_g), whole(ln_b)],
        out_specs=[tile(w) for w, _ in tile_outs]
        + [pl.BlockSpec((None, HEADS, 1, ts), lambda i: (i // nt, 0, 0, i % nt))]
        + [pl.BlockSpec(s, lambda i, n=len(s): (0,) * n) for s in acc_shapes],
        out_shape=[jax.ShapeDtypeStruct((bb, ss, w), dt) for w, dt in tile_outs]
        + [jax.ShapeDtypeStruct((bb, HEADS, 1, ss), F32)]
        + [jax.ShapeDtypeStruct(s, F32) for s in acc_shapes],
        compiler_params=_params(48, ("arbitrary",)))(x, tgt, o, ga, u, u, gb, w_out, pool_wb, pool_scale, ln_g, ln_b)


def _attn_bwd(qcat, kcat, v, do, lse, dl, rc, rsa, rsb, pos_col, pos_blk, pmax, pmin, tq, tk):
    bb, ss, _ = qcat.shape
    nq, nk = ss // tq, ss // tk
    hps = ATTN_BWD_HEADS_PER_STEP

    def body(pmax_ref, pmin_ref, q_ref, k_ref, v_ref, do_ref, lse_ref, dl_ref, c_ref, sa_ref, sb_ref, pc_ref, pb_ref,
             dqp_ref, dkn_ref, dv_ref, dkr_ref, dq_s, dk_s, dv_s):
        b = pl.program_id(0)
        dq_s[...] = jnp.zeros(dq_s.shape, F32)

        def part(qi, ki, masked):
            krows = pl.ds(pl.multiple_of(ki * tk, tk), tk)
            qrows = pl.ds(pl.multiple_of(qi * tq, tq), tq)
            if masked:
                mask = pb_ref[qi] >= pc_ref[krows, :]
            parts = []
            for hd in range(hps):
                qk = slice(hd * HEAD_PAD, (hd + 1) * HEAD_PAD)
                vs = slice(hd * V_DIM, (hd + 1) * V_DIM)
                q = q_ref[qrows, qk]
                dd = do_ref[qrows, vs]
                st = _dot_nt(k_ref[krows, qk], q)
                if masked:
                    st = jnp.where(mask, st, NEG)
                pt = jnp.exp2(st - lse_ref[hd, qi])
                dpt = _dot_nt(v_ref[krows, vs], dd)
                dst = (pt * (dpt - dl_ref[hd, qi])).astype(BF16)
                parts.append((_dot(pt.astype(BF16), dd), _dot(dst, q), _dot_tn(dst, k_ref[krows, qk])))
            for hd, (dv_part, dk_part, dq_part) in enumerate(parts):
                dv_s[:, hd * V_DIM:(hd + 1) * V_DIM] += dv_part
                dk_s[:, hd * HEAD_PAD:(hd + 1) * HEAD_PAD] += dk_part
                dq_s[qrows, hd * HEAD_PAD:(hd + 1) * HEAD_PAD] += dq_part

        def kv_step(ki, carry):
            krows = pl.ds(pl.multiple_of(ki * tk, tk), tk)
            dk_s[...] = jnp.zeros(dk_s.shape, F32)
            dv_s[...] = jnp.zeros(dv_s.shape, F32)

            def q_step(qi, c2):
                needed, visible, _ = _tile_cases(pmax_ref, pmin_ref, b, 2 * nq, qi, ki)

                @pl.when(needed & visible)
                def _():
                    part(qi, ki, False)

                @pl.when(needed & ~visible)
                def _():
                    part(qi, ki, True)

                return c2

            lax.fori_loop(0, nq, q_step, 0)
            dkr = jnp.zeros((tk, HEAD_PAD - NOPE), F32)
            for hd in range(hps):
                lo = hd * HEAD_PAD
                dkn_ref[krows, hd * NOPE:(hd + 1) * NOPE] = (dk_s[:, lo:lo + NOPE] * LN2).astype(BF16)
                dkr = dkr + dk_s[:, lo + NOPE:lo + HEAD_PAD]
            dkr_ref[krows, :] = dkr * LN2
            dv_ref[krows, :] = dv_s[...].astype(BF16)
            return carry

        lax.fori_loop(0, nk, kv_step, 0)
        c, sa, sb = c_ref[...], sa_ref[...], sb_ref[...]
        for hd in range(hps):
            lo = hd * HEAD_PAD
            dqp_ref[:, lo:lo + NOPE] = (dq_s[:, lo:lo + NOPE] * SCALE).astype(BF16)
            dqp_ref[:, lo + NOPE:lo + HEAD_PAD] = _rope_bwd(
                dq_s[:, lo + NOPE:lo + HEAD_PAD] * SCALE, c, sa, sb).astype(BF16)

    def per_head(w):
        return pl.BlockSpec((None, ss, hps * w), lambda b, h, *_: (b, 0, h))

    def rows_of_head():
        return pl.BlockSpec((None, hps, nq, 1, tq), lambda b, h, *_: (b, h, 0, 0, 0))

    def per_batch(w):
        return pl.BlockSpec((None, ss, w), lambda b, h, *_: (b, 0, 0))

    grid_spec = pltpu.PrefetchScalarGridSpec(
        num_scalar_prefetch=2, grid=(bb, HEADS // hps),
        in_specs=[per_head(HEAD_PAD), per_head(HEAD_PAD), per_head(V_DIM), per_head(V_DIM),
                  rows_of_head(), rows_of_head(), per_batch(128), per_batch(128), per_batch(128), per_batch(1),
                  pl.BlockSpec((None, nq, 1, tq), lambda b, h, *_: (b, 0, 0, 0))],
        out_specs=[per_head(HEAD_PAD), per_head(NOPE), per_head(V_DIM),
                   pl.BlockSpec((None, None, ss, HEAD_PAD - NOPE), lambda b, h, *_: (b, h, 0, 0))],
        scratch_shapes=[pltpu.VMEM((ss, hps * HEAD_PAD), F32), pltpu.VMEM((tk, hps * HEAD_PAD), F32),
                        pltpu.VMEM((tk, hps * V_DIM), F32)])
    return pl.pallas_call(
        body, name="attn_bwd", grid_spec=grid_spec,
        out_shape=[jax.ShapeDtypeStruct((bb, ss, HEADS * HEAD_PAD), BF16),
                   jax.ShapeDtypeStruct((bb, ss, MLA_W), BF16),
                   jax.ShapeDtypeStruct((bb, ss, MLA_W), BF16),
                   jax.ShapeDtypeStruct((bb, HEADS // hps, ss, HEAD_PAD - NOPE), F32)],
        compiler_params=_params(56, ("arbitrary", "arbitrary")))(
            pmax, pmin, qcat, kcat, v, do, lse, dl, rc, rsa, rsb, pos_col, pos_blk)


def _bwd_mid(dqp, dkn, dv, dkr, xq, xkv, rc, rsa, rsb, wq, wkv, gq, gkv, dga, dgb, dpc, dz, w1, ts):
    bb, ss, _ = dz.shape
    nt = ss // ts
    hb = ts // POOL_HALO
    groups = dkr.shape[1]

    def body(dqp_ref, dkn_ref, dv_ref, dkr_ref, xq_ref, xkv_ref, c_ref, sa_ref, sb_ref, wq_ref, wkv_ref, gq_ref,
             gkv_ref, dga_ref, dgb_ref, dpc_ref, dph_ref, dz_ref, w1_ref,
             gx_ref, dh_ref, ggq_ref, ggkv_ref):
        step = pl.program_id(0)
        j = step % nt

        @pl.when(step == 0)
        def _():
            ggq_ref[...] = jnp.zeros(ggq_ref.shape, F32)
            ggkv_ref[...] = jnp.zeros(ggkv_ref.shape, F32)

        dkr = dkr_ref[0]
        for g in range(1, groups):
            dkr = dkr + dkr_ref[g]
        dkr = _rope_bwd(dkr, c_ref[...], sa_ref[...], sb_ref[...]).astype(BF16)

        def rms_bwd(x, g, dn):
            inv = lax.rsqrt(jnp.mean(x * x, axis=-1, keepdims=True) + RMS_EPS)
            xh = x * inv
            dxh = dn * g
            return inv * (dxh - xh * jnp.mean(dxh * xh, axis=-1, keepdims=True)), jnp.sum(dn * xh, axis=0, keepdims=True)

        dxq, ggq = rms_bwd(xq_ref[...], gq_ref[...], _dot(dqp_ref[...], wq_ref[...]))
        dkvn = _dot(dkn_ref[...], wkv_ref[:MLA_W, :]) + _dot(dv_ref[...], wkv_ref[MLA_W:, :])
        dxkv, ggkv = rms_bwd(xkv_ref[...], gkv_ref[...], dkvn)
        ggq_ref[...] += ggq
        ggkv_ref[...] += ggkv
        dpc = dpc_ref[...]
        n = ts + POOL_HALO
        ext = jnp.concatenate([dpc, jnp.where(j < nt - 1, dph_ref[...], 0.0)], axis=0)
        r2 = ext + pltpu.roll(ext, n - 1, 0)
        r4 = r2 + pltpu.roll(r2, n - 2, 0)
        r8 = r4 + pltpu.roll(r4, n - 4, 0)
        r16 = r8 + pltpu.roll(r8, n - 8, 0)
        du = _pick_groups(r2, r4, r8, r16, 0, ts) - dpc * _pool_cnt(j * ts, ts)
        dh = jnp.concatenate(
            [dxq.astype(BF16), dxkv.astype(BF16), dkr, dga_ref[...], du.astype(BF16), dgb_ref[...]], axis=1)
        dh_ref[...] = dh
        gx_ref[...] = ALPHA * dz_ref[...] + _dot(dh, w1_ref[...])

    def tile(w):
        return pl.BlockSpec((None, ts, w), lambda i: (i // nt, i % nt, 0))

    def whole(a):
        return pl.BlockSpec(a.shape, lambda i: (0, 0))

    halo = pl.BlockSpec((None, POOL_HALO, POOL_W),
                        lambda i: (i // nt, jnp.minimum((i % nt + 1) * hb, ss // POOL_HALO - 1), 0))
    return pl.pallas_call(
        body, name="bwd_mid", grid=(bb * nt,),
        in_specs=[tile(HEADS * HEAD_PAD), tile(MLA_W), tile(MLA_W),
                  pl.BlockSpec((None, groups, ts, HEAD_PAD - NOPE), lambda i: (i // nt, 0, i % nt, 0)),
                  tile(Q_LORA), tile(KV_LORA),
                  tile(128), tile(128), tile(128), whole(wq), whole(wkv), whole(gq), whole(gkv),
                  tile(MLA_W), tile(POOL_W), tile(POOL_W), halo, tile(D_MODEL), whole(w1)],
        out_specs=[tile(D_MODEL), tile(IN_WP),
                   pl.BlockSpec((1, Q_LORA), lambda i: (0, 0)), pl.BlockSpec((1, KV_LORA), lambda i: (0, 0))],
        out_shape=[jax.ShapeDtypeStruct((bb, ss, D_MODEL), F32), jax.ShapeDtypeStruct((bb, ss, IN_WP), BF16),
                   jax.ShapeDtypeStruct((1, Q_LORA), F32), jax.ShapeDtypeStruct((1, KV_LORA), F32)],
        compiler_params=_params(48, ("arbitrary",)))(
            dqp, dkn, dv, dkr, xq, xkv, rc, rsa, rsb, wq, wkv, gq, gkv, dga, dgb, dpc, dpc, dz, w1)


def _grad_w(pairs, bt, name, b_cols=None, carried=None, reduced=None):
    tt = pairs[0][0].shape[0]
    steps = tt // bt
    npairs = len(pairs)
    n_rs = len(_rs_scratch(carried)) if carried is not None else 0

    def body(*refs):
        ab, rest = refs[:2 * npairs], list(refs[2 * npairs:])
        g_hbm = rest.pop(0) if carried is not None else None
        part = rest.pop(0) if reduced is not None else None
        outs = [rest.pop(0) for _ in range(npairs)]
        phases = []
        if carried is not None:
            rs_out = rest.pop(0)
        if reduced is not None:
            sum_ref = rest.pop(0)
        if carried is not None:
            phases.append(_rs_phases(carried, g_hbm, rs_out, *rest[:n_rs]))
        if reduced is not None:
            gathered, *sems = rest[n_rs:]
            phases.append(_ag_phases(part, gathered, *sems, sum_ref=sum_ref))
        step = pl.program_id(0)

        @pl.when(step == 0)
        def _():
            for o in outs:
                o[...] = jnp.zeros(o.shape, F32)
            for start, _, _ in phases:
                start()

        parts = [_dot_tn(ab[2 * i][...].astype(BF16), ab[2 * i + 1][...].astype(BF16)) for i in range(npairs)]
        for out, part in zip(outs, parts):
            out[...] += part

        @pl.when(step == min(1, steps - 1))
        def _():
            for _, forward, _ in phases:
                forward()

        @pl.when(step == steps - 1)
        def _():
            for _, _, finish in phases:
                finish()

    in_specs, out_specs, out_shape = [], [], []
    for a, b in pairs:
        m, (n, col) = a.shape[1], b_cols or (b.shape[1], 0)
        in_specs += [pl.BlockSpec((bt, m), lambda i: (i, 0)), pl.BlockSpec((bt, n), lambda i, col=col: (i, col))]
        out_specs.append(pl.BlockSpec((m, n), lambda i: (0, 0)))
        out_shape.append(jax.ShapeDtypeStruct((m, n), F32))
    args = [t for p in pairs for t in p]
    scratch = []
    if carried is not None:
        in_specs.append(pl.BlockSpec(memory_space=pl.ANY))
        out_specs.append(pl.BlockSpec((carried.rows, carried.width), lambda i: (0, 0)))
        out_shape.append(jax.ShapeDtypeStruct((carried.rows, carried.width), F32))
        args.append(carried.array)
        scratch += _rs_scratch(carried)
    if reduced is not None:
        in_specs.append(pl.BlockSpec(memory_space=pl.ANY))
        out_specs.append(pl.BlockSpec(reduced.shape, lambda i: (0, 0)))
        out_shape.append(jax.ShapeDtypeStruct(reduced.shape, F32))
        args.append(reduced)
        scratch += [pltpu.VMEM((N_DEV * reduced.shape[0], reduced.shape[1]), F32)] + AG_SEMS
    return pl.pallas_call(
        body, name=name, grid=(steps,), in_specs=in_specs, out_specs=out_specs, out_shape=out_shape,
        scratch_shapes=scratch, compiler_params=_params(56, ("arbitrary",)))(*args)


def _adamw(triples):
    n = len(triples)

    def body(*refs):
        ins, outs = refs[:4 * n], refs[4 * n:]
        for i in range(n):
            w, g, m, v = (r[...] for r in ins[4 * i:4 * i + 4])
            m = ADAM_B1 * m + (1.0 - ADAM_B1) * g
            v = ADAM_B2 * v + (1.0 - ADAM_B2) * jnp.square(g)
            m_hat = m / (1.0 - ADAM_B1 ** ADAM_STEP)
            v_hat = v / (1.0 - ADAM_B2 ** ADAM_STEP)
            outs[3 * i][...] = -ADAM_LR * (m_hat / (jnp.sqrt(v_hat) + ADAM_EPS) + ADAM_WD * w)
            outs[3 * i + 1][...] = m
            outs[3 * i + 2][...] = v

    flat = [a for t in triples for a in t]
    vmem = pl.BlockSpec(memory_space=pltpu.VMEM)
    res = pl.pallas_call(
        body, name="adamw", in_specs=[vmem] * len(flat), out_specs=[vmem] * (3 * n),
        out_shape=[jax.ShapeDtypeStruct(t[0].shape, F32) for t in triples for _ in range(3)],
        compiler_params=_params(48))(*flat)
    return [tuple(res[3 * i:3 * i + 3]) for i in range(n)]


def _shard_slab(w_in, w_uq, w_ukv):
    mixed = jnp.concatenate(
        [_pad_rows(w_uq.T, R_MIX), w_ukv.T, jnp.zeros((R_MIX, 1024 - Q_LORA - KV_LORA), F32)], axis=1)
    return jnp.concatenate([mixed, w_in.T, jnp.zeros((SLAB_ROWS - O_IN - R_IN, 1024), F32)], axis=0)


def _unpack_weights(slabs):
    uq = slabs[:, O_MIX:O_MIX + R_UQ, :Q_LORA].reshape(HEADS, QK_DIM, Q_LORA)
    wqt = jnp.pad(uq, ((0, 0), (0, HEAD_PAD - QK_DIM), (0, 0))).reshape(HEADS * HEAD_PAD, Q_LORA)
    ukv = slabs[:, O_MIX:O_MIX + R_MIX, Q_LORA:Q_LORA + KV_LORA].reshape(HEADS, 2, NOPE, KV_LORA)
    wkvt = ukv.transpose(1, 0, 2, 3).reshape(2 * MLA_W, KV_LORA)
    raw = slabs[:, O_IN:O_IN + R_IN].reshape(IN_W, D_MODEL)
    w1t = jnp.concatenate([raw[:768 + ROPE], jnp.zeros((128 - ROPE, D_MODEL), BF16), raw[768 + ROPE:]], axis=0)
    return w1t, wqt, wkvt


def _mixed_band(g_wqt, g_wkvt):
    uq = g_wqt.reshape(HEADS, HEAD_PAD, Q_LORA)[:, :QK_DIM].reshape(N_DEV, R_UQ, Q_LORA)
    uq = jnp.pad(uq, ((0, 0), (0, R_MIX - R_UQ), (0, 0)))
    ukv = g_wkvt.reshape(2, HEADS, NOPE, KV_LORA).transpose(1, 0, 2, 3).reshape(N_DEV, R_MIX, KV_LORA)
    return jnp.concatenate([uq, ukv, jnp.zeros((N_DEV, R_MIX, 1024 - Q_LORA - KV_LORA), F32)], axis=2)


def _rope_rows():
    half = ROPE // 2
    inv_freq = ROPE_THETA ** (-jnp.arange(half, dtype=F32) / half)
    zero, one = jnp.zeros((half,), F32), jnp.ones((half,), F32)
    rows = [jnp.concatenate(r) for r in (
        (inv_freq, inv_freq, zero, zero), (one, one, zero, zero), (-one, zero, zero, zero), (zero, one, zero, zero))]
    return jnp.stack(rows + [jnp.zeros((128,), F32)] * 4)


def _pad_rows(a, rows):
    return jnp.pad(a, ((0, rows - a.shape[0]), (0, 0)))


def kernel(x, positions, w_in, q_norm_g, w_uq, kv_norm_g, w_ukv, pool_w, pool_scale, w_out, ln_g, ln_b, loss_target, m_w_in, m_q_norm_g, m_w_uq, m_kv_norm_g, m_w_ukv, m_pool_w, m_pool_scale, m_w_out, m_ln_g, m_ln_b, v_w_in, v_q_norm_g, v_w_uq, v_kv_norm_g, v_w_ukv, v_pool_w, v_pool_scale, v_w_out, v_ln_g, v_ln_b):
    bb, ss, _ = x.shape
    tt = bb * ss
    atile = min(512, ss)
    nblk = ss // atile

    slab = _shard_slab(w_in, w_uq, w_ukv).astype(BF16)
    slabs = _all_gather(slab, "gather_weights").reshape(N_DEV, SLAB_ROWS, 1024)
    w1, wq, wkv = _unpack_weights(slabs)

    gq, gkv = q_norm_g.reshape(1, Q_LORA), kv_norm_g.reshape(1, KV_LORA)
    ps = pool_scale.reshape(1, POOL_W)
    pos_col = positions.reshape(bb, ss, 1)
    pos_blk = positions.reshape(bb, nblk, 1, atile)
    pmax = jnp.max(positions.reshape(bb, 2 * nblk, atile // 2), axis=-1).reshape(-1)
    pmin = jnp.min(positions.reshape(bb, 2 * nblk, atile // 2), axis=-1).reshape(-1)

    x2 = x.reshape(tt, D_MODEL)
    xq, xkv, ga, u, gb, qn, kvn, qcat, kcat, v, rc, rsa, rsb, wo = _fwd_in(
        x2, w1, gq, gkv, wq, wkv, pos_col.reshape(tt, 1), _rope_rows(), w_out.astype(BF16), atile)
    as3 = lambda a: a.reshape(bb, ss, a.shape[-1])
    qcat, kcat, v = as3(qcat), as3(kcat), as3(v)
    o, lse = _attn_fwd(qcat, kcat, v, pos_col, pos_blk, pmax, pmin, atile, atile)
    dz, ycat, do, dga, dgb, dpc, dl, loss_p, g_lng, g_lnb, g_ps, g_pw = _post(
        x, loss_target, o, as3(ga), as3(u), as3(gb), wo, pool_w.astype(BF16), ps, ln_g, ln_b, atile)

    bt = min(1024, tt)
    rows5 = lambda a: a.reshape(bb, HEADS, nblk, 1, atile)
    rc, rsa, rsb = as3(rc), as3(rsa), as3(rsb)
    dqp, dkn, dv, dkr = _attn_bwd(
        qcat, kcat, v, do, rows5(lse), rows5(dl), rc, rsa, rsb, pos_col, pos_blk, pmax, pmin, atile, atile)
    g_wq, g_wkn, g_wv = _grad_w(
        [(dqp.reshape(tt, HEADS * HEAD_PAD), qn), (dkn.reshape(tt, MLA_W), kvn), (dv.reshape(tt, MLA_W), kvn)],
        min(2 * bt, tt), "grad_w_uqkv")
    g_wkv = jnp.concatenate([g_wkn, g_wv], axis=0)
    grad_x, dh, g_gq, g_gkv = _bwd_mid(
        dqp, dkn, dv, dkr, as3(xq), as3(xkv), rc, rsa, rsb, wq, wkv, gq, gkv, dga, dgb, dpc, dz, w1, atile)
    small = jnp.concatenate(
        [_pad_rows(g_lng.reshape(8, 128), 8), _pad_rows(g_lnb.reshape(8, 128), 8), _pad_rows(g_gq.reshape(4, 128), 8),
         _pad_rows(g_gkv.reshape(2, 128), 8), _pad_rows(g_ps.reshape(4, 128), 8), g_pw.reshape(POOL_G * POOL_C, 128),
         loss_p], axis=0)
    dh2, half = dh.reshape(tt, IN_WP), D_MODEL // 2
    g_w1a, rs_mix, small = _grad_w([(dh2, x2)], bt, "grad_w_in_a", b_cols=(half, 0),
                                   carried=_blocks(_mixed_band(g_wq, g_wkv)), reduced=small)
    g_w1b, rs_in_a = _grad_w([(dh2, x2)], bt, "grad_w_in_b", b_cols=(half, 1), carried=_w_in_blocks(g_w1a))
    g_wo, rs_in_b = _grad_w([(ycat.reshape(tt, D_MODEL), dz.reshape(tt, D_MODEL))], bt, "grad_w_out",
                            carried=_w_in_blocks(g_w1b))
    rs_in = jnp.concatenate([rs_in_a, rs_in_b], axis=1)
    rs_out = _reduce_scatter(_blocks(g_wo.reshape(N_DEV, R_OUT, D_MODEL)), "reduce_scatter_w_out")
    loss = small[40 + POOL_G * POOL_C, 0]
    grads = {
        "w_in": rs_in[:R_IN],
        "q_norm_g": small[16:20].reshape(1, Q_LORA),
        "w_uq": rs_mix[:R_UQ, :Q_LORA],
        "kv_norm_g": small[24:26].reshape(1, KV_LORA),
        "w_ukv": rs_mix[:, Q_LORA:Q_LORA + KV_LORA].T,
        "pool_w": small[40:40 + POOL_G * POOL_C],
        "pool_scale": small[32:36].reshape(1, POOL_W),
        "w_out": rs_out,
        "ln_g": small[0:8].reshape(1, D_MODEL),
        "ln_b": small[8:16].reshape(1, D_MODEL),
    }
    transposed = ("w_in", "w_uq")

    names = ["w_in", "q_norm_g", "w_uq", "kv_norm_g", "w_ukv", "pool_w", "pool_scale", "w_out", "ln_g", "ln_b"]
    weights = dict(w_in=w_in, q_norm_g=q_norm_g, w_uq=w_uq, kv_norm_g=kv_norm_g, w_ukv=w_ukv, pool_w=pool_w,
                   pool_scale=pool_scale, w_out=w_out, ln_g=ln_g, ln_b=ln_b)
    moms = dict(w_in=(m_w_in, v_w_in), q_norm_g=(m_q_norm_g, v_q_norm_g), w_uq=(m_w_uq, v_w_uq),
                kv_norm_g=(m_kv_norm_g, v_kv_norm_g), w_ukv=(m_w_ukv, v_w_ukv), pool_w=(m_pool_w, v_pool_w),
                pool_scale=(m_pool_scale, v_pool_scale), w_out=(m_w_out, v_w_out), ln_g=(m_ln_g, v_ln_g),
                ln_b=(m_ln_b, v_ln_b))
    as2 = lambda a, n: a.T if n in transposed else a.reshape(grads[n].shape)
    upd = _adamw([(as2(weights[n], n), grads[n], as2(moms[n][0], n), as2(moms[n][1], n)) for n in names])
    shaped = lambda a, n: a.T if n in transposed else a.reshape(weights[n].shape)
    return (loss, grad_x,
            *[shaped(grads[n], n) for n in names],
            *[shaped(upd[i][0], n) for i, n in enumerate(names)],
            *[shaped(upd[i][1], n) for i, n in enumerate(names)],
            *[shaped(upd[i][2], n) for i, n in enumerate(names)])
```

```python
import jax
import jax.numpy as jnp
from jax import lax
from jax.experimental import pallas as pl
from jax.experimental.pallas import tpu as pltpu

F32 = jnp.float32
BF16 = jnp.bfloat16
MESH = pl.DeviceIdType.MESH

N_DEV = 8
D_MODEL = 1024
HEADS = 4
NOPE = 128
ROPE = 64
V_DIM = 128
QK_DIM = NOPE + ROPE
HEAD_PAD = 256
Q_LORA = 512
KV_LORA = 256
MLA_W = HEADS * V_DIM
POOL_W = 512
POOL_G = 4
POOL_C = 128
POOL_HALO = 16
IN_W = 2368
IN_WP = 2432
C_XQ, C_XKV, C_KR, C_GA, C_U, C_GB = 0, 512, 768, 896, 1408, 1920
ROPE_THETA = 10000.0
RMS_EPS = 1e-6
LN_EPS = 1e-5
ALPHA = 2.0 ** 0.25
SCALE = QK_DIM ** -0.5
LN2 = 0.6931471805599453
SCALE_LOG2 = SCALE / LN2
NEG = float(jnp.finfo(jnp.float32).min)

ADAM_LR = 0.001
ADAM_B1 = 0.9
ADAM_B2 = 0.999
ADAM_EPS = 1e-08
ADAM_WD = 0.01
ADAM_STEP = 10

R_OUT, R_MIX, R_UQ, R_IN = 128, 128, 96, 296
O_MIX, O_IN = 0, R_MIX
SLAB_ROWS = 432
R_IN_PAD = 304

V7X_VMEM_BYTES = 64 * 1024 * 1024
ATTN_FWD_HEADS_PER_STEP = 4
ATTN_BWD_HEADS_PER_STEP = 2


def _params(vmem_mb, semantics=None):
    assert vmem_mb * 1024 * 1024 < V7X_VMEM_BYTES
    return pltpu.CompilerParams(vmem_limit_bytes=vmem_mb * 1024 * 1024, dimension_semantics=semantics)


def _dot(a, b):
    return jnp.dot(a, b, preferred_element_type=F32)


def _dot_nt(a, b):
    return lax.dot_general(a, b, (((1,), (1,)), ((), ())), preferred_element_type=F32)


def _dot_tn(a, b):
    return lax.dot_general(a, b, (((0,), (0,)), ((), ())), preferred_element_type=F32)


def _rope_fwd(t, c, sa, sb):
    return t * c + pltpu.roll(t, 96, 1) * sa + pltpu.roll(t, 32, 1) * sb


def _rope_bwd(d, c, sa, sb):
    return d * c + pltpu.roll(d * sa, 32, 1) + pltpu.roll(d * sb, 96, 1)


def _silu_parts(g):
    sig = 0.5 * jnp.tanh(0.5 * g) + 0.5
    silu = g * sig
    return silu, sig + silu - silu * sig


def _pool_cnt(row0, rows):
    t = row0 + lax.broadcasted_iota(jnp.int32, (rows, POOL_W), 0)
    w = 2 << (lax.broadcasted_iota(jnp.int32, (rows, POOL_W), 1) // POOL_C)
    return jnp.minimum(t + 1, w).astype(F32)


def _pick_groups(s2, s4, s8, s16, lo, hi):
    return jnp.concatenate([s2[lo:hi, 0:128], s4[lo:hi, 128:256], s8[lo:hi, 256:384], s16[lo:hi, 384:512]], axis=1)


AG_SEMS = [pltpu.SemaphoreType.DMA((7,)), pltpu.SemaphoreType.DMA((7,)), pltpu.SemaphoreType.DMA]


def _ag_phases(x_ref, out_ref, send_sems, recv_sems, local_sem, sum_ref=None):
    m_per = x_ref.shape[0]
    x, y, c = lax.axis_index("x"), lax.axis_index("y"), lax.axis_index("c")
    me, sibling = (x, y, c), (x, y, 1 - c)
    chips = [(1 - x, y), (x, 1 - y), (1 - x, 1 - y)]

    def rows(px, py, pc):
        return out_ref.at[pl.ds((4 * px + 2 * py + pc) * m_per, m_per), :]

    def copy(k, block, to, src=None):
        return pltpu.make_async_remote_copy(
            src_ref=rows(*block) if src is None else src, dst_ref=rows(*block),
            send_sem=send_sems.at[k], recv_sem=recv_sems.at[k], device_id=to, device_id_type=MESH)

    def mine():
        return pltpu.make_async_copy(x_ref, rows(*me), local_sem)

    def first():
        return [copy(0, me, sibling, src=x_ref)] + [copy(1 + j, me, (*chip, c), src=x_ref) for j, chip in enumerate(chips)]

    def passed():
        return [copy(4 + j, (*chip, c), sibling) for j, chip in enumerate(chips)]

    def start():
        for cp in [mine()] + first():
            cp.start()

    def forward():
        for j, (chip, cp) in enumerate(zip(chips, passed())):
            copy(1 + j, (*chip, c), me).wait_recv()
            cp.start()

    def finish():
        copy(0, sibling, me).wait_recv()
        for j, chip in enumerate(chips):
            copy(4 + j, (*chip, 1 - c), me).wait_recv()
        for cp in first() + passed():
            cp.wait_send()
        mine().wait()
        if sum_ref is not None:
            acc = out_ref[pl.ds(0, m_per), :]
            for d in range(1, N_DEV):
                acc = acc + out_ref[pl.ds(d * m_per, m_per), :]
            sum_ref[...] = acc

    return start, forward, finish


def _all_gather(shard, name):
    m_per, n = shard.shape

    def body(x_ref, out_ref, *sems):
        for phase in _ag_phases(x_ref, out_ref, *sems):
            phase()

    vmem = pl.BlockSpec(memory_space=pltpu.VMEM)
    return pl.pallas_call(
        body, name=name, out_shape=jax.ShapeDtypeStruct((N_DEV * m_per, n), shard.dtype),
        in_specs=[vmem], out_specs=vmem, scratch_shapes=AG_SEMS, compiler_params=_params(32))(shard)


def _reduce_scatter(sc, name):
    def body(g_hbm, out_ref, *scratch):
        for phase in _rs_phases(sc, g_hbm, out_ref, *scratch):
            phase()

    return pl.pallas_call(
        body, name=name, out_shape=jax.ShapeDtypeStruct((sc.rows, sc.width), F32),
        in_specs=[pl.BlockSpec(memory_space=pl.ANY)], out_specs=pl.BlockSpec(memory_space=pltpu.VMEM),
        scratch_shapes=_rs_scratch(sc), compiler_params=_params(32))(sc.array)


class _Scattered:
    def __init__(self, array, rows, pieces, locate):
        self.array, self.rows, self.pieces, self.locate = array, rows, pieces, locate
        self.width = array.shape[-1]


def _blocks(g):
    return _Scattered(g, g.shape[1], ((0, g.shape[1]),), lambda ref, d, row, rows: ref.at[d])


def _w_in_blocks(g_w1t):
    def locate(ref, d, row, rows):
        r = R_IN * d + row
        return ref.at[pl.ds(pl.multiple_of(r + jnp.where(r >= C_KR + ROPE, 128 - ROPE, 0), 8), rows), :]

    cut = C_KR + ROPE - 2 * R_IN
    return _Scattered(g_w1t, R_IN_PAD, ((0, cut), (cut, R_IN - cut)), locate)


def _rs_scratch(sc):
    rr, ww, n = sc.rows, sc.width, 4 * len(sc.pieces)
    return [pltpu.VMEM((4, rr, ww), F32), pltpu.VMEM((4, rr, ww), F32),
            pltpu.VMEM((3, rr, ww), BF16), pltpu.VMEM((3, rr, ww), BF16),
            pltpu.SemaphoreType.DMA((n,)), pltpu.SemaphoreType.DMA((n,)), pltpu.SemaphoreType.DMA((n,)),
            pltpu.SemaphoreType.DMA((3,)), pltpu.SemaphoreType.DMA((3,))]


def _rs_phases(sc, g_hbm, out_ref, own_ref, recv1_ref, sendb_ref, recv2_ref, ld_sems, s1_send, s1_recv, s2_send, s2_recv):
    rr, ww = out_ref.shape
    chunk = next(c for c in (128, 80, 64, 48, 32, 16) if rr % c == 0)
    x, y, c = lax.axis_index("x"), lax.axis_index("y"), lax.axis_index("c")
    chips = [(1 - x, y), (x, 1 - y), (1 - x, 1 - y)]
    npieces = len(sc.pieces)
    filled = sum(rows for _, rows in sc.pieces)

    def pieces(d, buf, k):
        for p, (row, rows) in enumerate(sc.pieces):
            dst = buf.at[k] if (row, rows) == (0, rr) else buf.at[k, pl.ds(row, rows), :]
            yield k * npieces + p, sc.locate(g_hbm, d, row, rows), dst

    def loads():
        return [pltpu.make_async_copy(src, dst, ld_sems.at[s])
                for k in range(4) for s, src, dst in pieces(2 * k + c, own_ref, k)]

    def stage1():
        return [pltpu.make_async_remote_copy(
            src_ref=src, dst_ref=dst, send_sem=s1_send.at[s], recv_sem=s1_recv.at[s],
            device_id=(x, y, 1 - c), device_id_type=MESH)
            for k in range(4) for s, src, dst in pieces(2 * k + (1 - c), recv1_ref, k)]

    def stage2():
        return [pltpu.make_async_remote_copy(
            src_ref=sendb_ref.at[r], dst_ref=recv2_ref.at[r], send_sem=s2_send.at[r],
            recv_sem=s2_recv.at[r], device_id=(cx, cy, c), device_id_type=MESH) for r, (cx, cy) in enumerate(chips)]

    def start():
        if filled < rr:
            own_ref[:, filled:rr, :] = jnp.zeros((4, rr - filled, ww), F32)
            recv1_ref[:, filled:rr, :] = jnp.zeros((4, rr - filled, ww), F32)
        for cp in loads() + stage1():
            cp.start()

    def forward():
        for cp in loads():
            cp.wait()
        for cp in stage1():
            cp.wait_recv()
        sends = stage2()
        for r, (cx, cy) in enumerate(chips):
            kk = 2 * cx + cy

            def pack(i, carry, r=r, kk=kk):
                rows = pl.ds(pl.multiple_of(i * chunk, chunk), chunk)
                sendb_ref[r, rows, :] = (own_ref[kk, rows, :] + recv1_ref[kk, rows, :]).astype(BF16)
                return carry

            lax.fori_loop(0, rr // chunk, pack, 0)
            sends[r].start()

    def finish():
        for cp in stage2():
            cp.wait_recv()
        mine = 2 * x + y

        def total(i, carry):
            rows = pl.ds(pl.multiple_of(i * chunk, chunk), chunk)
            acc = own_ref[mine, rows, :] + recv1_ref[mine, rows, :]
            for r in range(3):
                acc = acc + recv2_ref[r, rows, :].astype(F32)
            out_ref[rows, :] = acc
            return carry

        lax.fori_loop(0, rr // chunk, total, 0)
        for cp in stage1() + stage2():
            cp.wait_send()

    return start, forward, finish


def _fwd_in(x2, w1, gq, gkv, wq, wkv, pos, rope_rows, shard, tm):
    tt = x2.shape[0]
    steps = tt // tm
    gathered_shape = (N_DEV * shard.shape[0], shard.shape[1])

    def body(x_ref, w1_ref, gq_ref, gkv_ref, wq_ref, wkv_ref, pos_ref, rr_ref, shard_ref,
             xq_ref, xkv_ref, ga_ref, u_ref, gb_ref, qn_ref, kvn_ref, qcat_ref, kcat_ref, v_ref,
             c_ref, sa_ref, sb_ref, all_ref, gathered, *sems):
        step = pl.program_id(0)
        start, forward, finish = _ag_phases(shard_ref, gathered, *sems)
        pl.when(step == 0)(start)
        pl.when(step == min(2, steps - 1))(forward)

        @pl.when(step == steps - 1)
        def _():
            finish()
            all_ref[...] = gathered[...]

        ang = pos_ref[...].astype(F32) * rr_ref[0:1, :]
        cos, sin = jnp.cos(ang), jnp.sin(ang)
        c, sa, sb = cos * rr_ref[1:2, :], sin * rr_ref[2:3, :], sin * rr_ref[3:4, :]
        h = _dot_nt(x_ref[...].astype(BF16), w1_ref[...])
        xq = h[:, C_XQ:C_XKV]
        xkv = h[:, C_XKV:C_KR]
        qn = (xq * lax.rsqrt(jnp.mean(xq * xq, axis=-1, keepdims=True) + RMS_EPS) * gq_ref[...]).astype(BF16)
        kvn = (xkv * lax.rsqrt(jnp.mean(xkv * xkv, axis=-1, keepdims=True) + RMS_EPS) * gkv_ref[...]).astype(BF16)
        q = _dot_nt(qn, wq_ref[...]) * SCALE_LOG2
        kv = _dot_nt(kvn, wkv_ref[...])
        kr = _rope_fwd(h[:, C_KR:C_GA], c, sa, sb).astype(BF16)
        c_ref[...] = c
        sa_ref[...] = sa
        sb_ref[...] = sb
        xq_ref[...] = xq
        xkv_ref[...] = xkv
        ga_ref[...] = h[:, C_GA:C_U]
        u_ref[...] = h[:, C_U:C_GB]
        gb_ref[...] = h[:, C_GB:IN_WP]
        qn_ref[...] = qn
        kvn_ref[...] = kvn
        for hd in range(HEADS):
            lo = hd * HEAD_PAD
            qcat_ref[:, lo:lo + NOPE] = q[:, lo:lo + NOPE].astype(BF16)
            qcat_ref[:, lo + NOPE:lo + HEAD_PAD] = _rope_fwd(q[:, lo + NOPE:lo + HEAD_PAD], c, sa, sb).astype(BF16)
            kcat_ref[:, lo:lo + NOPE] = kv[:, hd * NOPE:(hd + 1) * NOPE].astype(BF16)
            kcat_ref[:, lo + NOPE:lo + HEAD_PAD] = kr
        v_ref[...] = kv[:, MLA_W:].astype(BF16)

    def tile(w):
        return pl.BlockSpec((tm, w), lambda i: (i, 0))

    def whole(a):
        return pl.BlockSpec(a.shape, lambda i: (0, 0))

    outs = [(Q_LORA, F32), (KV_LORA, F32), (MLA_W, F32), (POOL_W, F32), (POOL_W, F32),
            (Q_LORA, BF16), (KV_LORA, BF16), (HEADS * HEAD_PAD, BF16), (HEADS * HEAD_PAD, BF16), (MLA_W, BF16),
            (128, F32), (128, F32), (128, F32)]
    return pl.pallas_call(
        body, name="fwd_in", grid=(steps,),
        in_specs=[tile(D_MODEL), whole(w1), whole(gq), whole(gkv), whole(wq), whole(wkv), tile(1), whole(rope_rows),
                  pl.BlockSpec(memory_space=pl.ANY)],
        out_specs=[tile(w) for w, _ in outs] + [pl.BlockSpec(gathered_shape, lambda i: (0, 0))],
        out_shape=[jax.ShapeDtypeStruct((tt, w), dt) for w, dt in outs]
        + [jax.ShapeDtypeStruct(gathered_shape, shard.dtype)],
        scratch_shapes=[pltpu.VMEM(gathered_shape, shard.dtype)] + AG_SEMS,
        compiler_params=_params(56, ("arbitrary",)))(x2, w1, gq, gkv, wq, wkv, pos, rope_rows, shard)


def _tile_cases(pmax_ref, pmin_ref, b, nhalf, qi, ki):
    q0, q1 = b * nhalf + 2 * qi, b * nhalf + 2 * qi + 1
    k0, k1 = b * nhalf + 2 * ki, b * nhalf + 2 * ki + 1
    needed = jnp.maximum(pmax_ref[q0], pmax_ref[q1]) >= jnp.minimum(pmin_ref[k0], pmin_ref[k1])
    visible = jnp.minimum(pmin_ref[q0], pmin_ref[q1]) >= jnp.maximum(pmax_ref[k0], pmax_ref[k1])
    stepped = pmax_ref[q0] < pmin_ref[k1]
    return needed, visible, stepped


def _attn_fwd(qcat, kcat, v, pos_col, pos_blk, pmax, pmin, tq, tk):
    bb, ss, _ = qcat.shape
    nq, nk = ss // tq, ss // tk
    lanes = 128
    hps = ATTN_FWD_HEADS_PER_STEP

    def body(pmax_ref, pmin_ref, q_ref, k_ref, v_ref, pc_ref, pb_ref, o_ref, lse_ref, m_s, acc_s):
        b, qi = pl.program_id(0), pl.program_id(2)
        m_s[...] = jnp.full(m_s.shape, NEG, F32)
        acc_s[...] = jnp.zeros(acc_s.shape, F32)

        def part(ki, masked, q_lo, q_n, k_n):
            qrows = slice(q_lo, q_lo + q_n)
            krows = pl.ds(pl.multiple_of(ki * tk, tk), k_n)
            if masked:
                mask = pc_ref[qrows, :] >= pb_ref[ki][:, :k_n]
            ones = jnp.ones((k_n, lanes), BF16)
            parts = []
            for hd in range(hps):
                qk = slice(hd * HEAD_PAD, (hd + 1) * HEAD_PAD)
                s = _dot_nt(q_ref[qrows, qk], k_ref[krows, qk])
                if masked:
                    s = jnp.where(mask, s, NEG)
                m_prev = m_s[hd, qrows, :]
                m_new = jnp.maximum(m_prev, jnp.max(s, axis=-1, keepdims=True))
                p = jnp.exp2(s - jnp.tile(m_new, (1, k_n // lanes)))
                a = jnp.exp2(m_prev - m_new)
                vv = jnp.concatenate([v_ref[krows, hd * V_DIM:(hd + 1) * V_DIM], ones], axis=1)
                parts.append((m_new, jnp.tile(a, (1, 2)) * acc_s[hd, qrows, :] + _dot(p.astype(BF16), vv)))
            for hd, (m_new, acc) in enumerate(parts):
                acc_s[hd, qrows, :] = acc
                m_s[hd, qrows, :] = m_new

        def step(ki, carry):
            needed, visible, stepped = _tile_cases(pmax_ref, pmin_ref, b, 2 * nq, qi, ki)

            @pl.when(needed & visible)
            def _():
                part(ki, False, 0, tq, tk)

            @pl.when(needed & ~visible & stepped)
            def _():
                part(ki, True, 0, tq // 2, tk // 2)
                part(ki, True, tq // 2, tq // 2, tk)

            @pl.when(needed & ~visible & ~stepped)
            def _():
                part(ki, True, 0, tq, tk)

            return carry

        lax.fori_loop(0, nk, step, 0)
        for hd in range(hps):
            acc = acc_s[hd]
            l = acc[:, V_DIM:]
            o_ref[:, hd * V_DIM:(hd + 1) * V_DIM] = acc[:, :V_DIM] / l
            lse_ref[hd] = (m_s[hd] + jnp.log2(l)).T[0:1, :]

    grid_spec = pltpu.PrefetchScalarGridSpec(
        num_scalar_prefetch=2, grid=(bb, HEADS // hps, nq),
        in_specs=[
            pl.BlockSpec((None, tq, hps * HEAD_PAD), lambda b, h, i, *_: (b, i, h)),
            pl.BlockSpec((None, ss, hps * HEAD_PAD), lambda b, h, i, *_: (b, 0, h)),
            pl.BlockSpec((None, ss, hps * V_DIM), lambda b, h, i, *_: (b, 0, h)),
            pl.BlockSpec((None, tq, 1), lambda b, h, i, *_: (b, i, 0)),
            pl.BlockSpec((None, nk, 1, tk), lambda b, h, i, *_: (b, 0, 0, 0)),
        ],
        out_specs=[
            pl.BlockSpec((None, tq, hps * V_DIM), lambda b, h, i, *_: (b, i, h)),
            pl.BlockSpec((None, hps, 1, tq), lambda b, h, i, *_: (b, h, 0, i)),
        ],
        scratch_shapes=[pltpu.VMEM((hps, tq, lanes), F32), pltpu.VMEM((hps, tq, 2 * V_DIM), F32)])
    return pl.pallas_call(
        body, name="attn_fwd", grid_spec=grid_spec,
        out_shape=[jax.ShapeDtypeStruct((bb, ss, MLA_W), F32), jax.ShapeDtypeStruct((bb, HEADS, 1, ss), F32)],
        compiler_params=_params(48, ("arbitrary", "arbitrary", "arbitrary")))(pmax, pmin, qcat, kcat, v, pos_col, pos_blk)


def _post(x, tgt, o, ga, u, gb, w_out, pool_wb, pool_scale, ln_g, ln_b, ts):
    bb, ss, _ = x.shape
    nt = ss // ts
    hb = ts // POOL_HALO

    def body(x_ref, tgt_ref, o_ref, ga_ref, u_ref, uh_ref, gb_ref, wo_ref, pw_ref, ps_ref, lg_ref, lb_ref,
             dz_ref, ycat_ref, do_ref, dga_ref, dgb_ref, dpc_ref, dl_ref, loss_ref, glg_ref, glb_ref, gps_ref, gpw_ref):
        step = pl.program_id(0)
        j = step % nt

        @pl.when(step == 0)
        def _():
            for r in (loss_ref, glg_ref, glb_ref, gps_ref, gpw_ref):
                r[...] = jnp.zeros(r.shape, F32)

        o, ga, u, gb = o_ref[...], ga_ref[...], u_ref[...], gb_ref[...]
        sa, dsa = _silu_parts(ga)
        sb, dsb = _silu_parts(gb)
        ext = jnp.concatenate([jnp.where(j > 0, uh_ref[...], 0.0), u], axis=0)
        s2 = ext + pltpu.roll(ext, 1, 0)
        s4 = s2 + pltpu.roll(s2, 2, 0)
        s8 = s4 + pltpu.roll(s4, 4, 0)
        s16 = s8 + pltpu.roll(s8, 8, 0)
        cnt = _pool_cnt(j * ts, ts)
        rcnt = 1.0 / cnt
        pooled = (_pick_groups(s2, s4, s8, s16, POOL_HALO, POOL_HALO + ts) * rcnt - u).astype(BF16)
        mixed = jnp.concatenate(
            [_dot(pooled[:, g * POOL_C:(g + 1) * POOL_C], pw_ref[g]) for g in range(POOL_G)], axis=1)
        ps = ps_ref[...]
        scaled = mixed * ps
        ycat = jnp.concatenate([o * sa, scaled * sb], axis=1).astype(BF16)
        z = ALPHA * x_ref[...] + _dot(ycat, wo_ref[...])
        mu = jnp.mean(z, axis=-1, keepdims=True)
        zc = z - mu
        rstd = lax.rsqrt(jnp.mean(zc * zc, axis=-1, keepdims=True) + LN_EPS)
        xhat = zc * rstd
        lg = lg_ref[...]
        diff = xhat * lg + lb_ref[...] - tgt_ref[...]
        loss = jnp.sum(diff * diff) * (0.5 / D_MODEL)
        glb = jnp.sum(diff, axis=0, keepdims=True) * (1.0 / D_MODEL)
        glg = jnp.sum(diff * xhat, axis=0, keepdims=True) * (1.0 / D_MODEL)
        dxh = diff * (lg * (1.0 / D_MODEL))
        dz = rstd * (dxh - jnp.mean(dxh, axis=-1, keepdims=True) - xhat * jnp.mean(dxh * xhat, axis=-1, keepdims=True))
        dycat = _dot_nt(dz.astype(BF16), wo_ref[...])
        dya, dyb = dycat[:, :MLA_W], dycat[:, MLA_W:]
        do = dya * sa
        doo = do * o
        dl = [jnp.sum(doo[:, hd * V_DIM:(hd + 1) * V_DIM].T, axis=0, keepdims=True) for hd in range(HEADS)]
        dga = (dya * o * dsa).astype(BF16)
        dgb = (dyb * scaled * dsb).astype(BF16)
        dscaled = dyb * sb
        gps = jnp.sum(dscaled * mixed, axis=0, keepdims=True)
        dmixed = (dscaled * ps).astype(BF16)
        groups = [slice(g * POOL_C, (g + 1) * POOL_C) for g in range(POOL_G)]
        gpw = [_dot_tn(pooled[:, cols], dmixed[:, cols]) for cols in groups]
        dpooled = [_dot_nt(dmixed[:, cols], pw_ref[g]) for g, cols in enumerate(groups)]
        ycat_ref[...] = ycat
        dz_ref[...] = dz
        do_ref[...] = do.astype(BF16)
        dga_ref[...] = dga
        dgb_ref[...] = dgb
        dpc_ref[...] = jnp.concatenate(dpooled, axis=1) * rcnt
        for hd in range(HEADS):
            dl_ref[hd] = dl[hd]
        for g in range(POOL_G):
            gpw_ref[g] += gpw[g]
        loss_ref[...] += loss
        glb_ref[...] += glb
        glg_ref[...] += glg
        gps_ref[...] += gps

    def tile(w):
        return pl.BlockSpec((None, ts, w), lambda i: (i // nt, i % nt, 0))

    def whole(a):
        nd = a.ndim
        return pl.BlockSpec(a.shape, lambda i: (0,) * nd)

    halo = pl.BlockSpec((None, POOL_HALO, POOL_W), lambda i: (i // nt, jnp.maximum((i % nt) * hb - 1, 0), 0))
    acc_shapes = [(8, 128), (1, D_MODEL), (1, D_MODEL), (1, POOL_W), (POOL_G, POOL_C, POOL_C)]
    tile_outs = [(D_MODEL, F32), (D_MODEL, BF16), (MLA_W, BF16), (MLA_W, BF16), (POOL_W, BF16), (POOL_W, F32)]
    return pl.pallas_call(
        body, name="post", grid=(bb * nt,),
        in_specs=[tile(D_MODEL), tile(D_MODEL), tile(MLA_W), tile(MLA_W), tile(POOL_W), halo, tile(POOL_W),
                  whole(w_out), whole(pool_wb), whole(pool_scale), whole(ln_g), whole(ln_b)],
        out_specs=[tile(w) for w, _ in tile_outs]
        + [pl.BlockSpec((None, HEADS, 1, ts), lambda i: (i // nt, 0, 0, i % nt))]
        + [pl.BlockSpec(s, lambda i, n=len(s): (0,) * n) for s in acc_shapes],
        out_shape=[jax.ShapeDtypeStruct((bb, ss, w), dt) for w, dt in tile_outs]
        + [jax.ShapeDtypeStruct((bb, HEADS, 1, ss), F32)]
        + [jax.ShapeDtypeStruct(s, F32) for s in acc_shapes],
        compiler_params=_params(48, ("arbitrary",)))(x, tgt, o, ga, u, u, gb, w_out, pool_wb, pool_scale, ln_g, ln_b)


def _attn_bwd(qcat, kcat, v, do, lse, dl, rc, rsa, rsb, pos_col, pos_blk, pmax, pmin, tq, tk):
    bb, ss, _ = qcat.shape
    nq, nk = ss // tq, ss // tk
    hps = ATTN_BWD_HEADS_PER_STEP

    def body(pmax_ref, pmin_ref, q_ref, k_ref, v_ref, do_ref, lse_ref, dl_ref, c_ref, sa_ref, sb_ref, pc_ref, pb_ref,
             dqp_ref, dkn_ref, dv_ref, dkr_ref, dq_s, dk_s, dv_s):
        b = pl.program_id(0)
        dq_s[...] = jnp.zeros(dq_s.shape, F32)

        def part(qi, ki, masked):
            krows = pl.ds(pl.multiple_of(ki * tk, tk), tk)
            qrows = pl.ds(pl.multiple_of(qi * tq, tq), tq)
            if masked:
                mask = pb_ref[qi] >= pc_ref[krows, :]
            parts = []
            for hd in range(hps):
                qk = slice(hd * HEAD_PAD, (hd + 1) * HEAD_PAD)
                vs = slice(hd * V_DIM, (hd + 1) * V_DIM)
                q = q_ref[qrows, qk]
                dd = do_ref[qrows, vs]
                st = _dot_nt(k_ref[krows, qk], q)
                if masked:
                    st = jnp.where(mask, st, NEG)
                pt = jnp.exp2(st - lse_ref[hd, qi])
                dpt = _dot_nt(v_ref[krows, vs], dd)
                dst = (pt * (dpt - dl_ref[hd, qi])).astype(BF16)
                parts.append((_dot(pt.astype(BF16), dd), _dot(dst, q), _dot_tn(dst, k_ref[krows, qk])))
            for hd, (dv_part, dk_part, dq_part) in enumerate(parts):
                dv_s[:, hd * V_DIM:(hd + 1) * V_DIM] += dv_part
                dk_s[:, hd * HEAD_PAD:(hd + 1) * HEAD_PAD] += dk_part
                dq_s[qrows, hd * HEAD_PAD:(hd + 1) * HEAD_PAD] += dq_part

        def kv_step(ki, carry):
            krows = pl.ds(pl.multiple_of(ki * tk, tk), tk)
            dk_s[...] = jnp.zeros(dk_s.shape, F32)
            dv_s[...] = jnp.zeros(dv_s.shape, F32)

            def q_step(qi, c2):
                needed, visible, _ = _tile_cases(pmax_ref, pmin_ref, b, 2 * nq, qi, ki)

                @pl.when(needed & visible)
                def _():
                    part(qi, ki, False)

                @pl.when(needed & ~visible)
                def _():
                    part(qi, ki, True)

                return c2

            lax.fori_loop(0, nq, q_step, 0)
            dkr = jnp.zeros((tk, HEAD_PAD - NOPE), F32)
            for hd in range(hps):
                lo = hd * HEAD_PAD
                dkn_ref[krows, hd * NOPE:(hd + 1) * NOPE] = (dk_s[:, lo:lo + NOPE] * LN2).astype(BF16)
                dkr = dkr + dk_s[:, lo + NOPE:lo + HEAD_PAD]
            dkr_ref[krows, :] = dkr * LN2
            dv_ref[krows, :] = dv_s[...].astype(BF16)
            return carry

        lax.fori_loop(0, nk, kv_step, 0)
        c, sa, sb = c_ref[...], sa_ref[...], sb_ref[...]
        for hd in range(hps):
            lo = hd * HEAD_PAD
            dqp_ref[:, lo:lo + NOPE] = (dq_s[:, lo:lo + NOPE] * SCALE).astype(BF16)
            dqp_ref[:, lo + NOPE:lo + HEAD_PAD] = _rope_bwd(
                dq_s[:, lo + NOPE:lo + HEAD_PAD] * SCALE, c, sa, sb).astype(BF16)

    def per_head(w):
        return pl.BlockSpec((None, ss, hps * w), lambda b, h, *_: (b, 0, h))

    def rows_of_head():
        return pl.BlockSpec((None, hps, nq, 1, tq), lambda b, h, *_: (b, h, 0, 0, 0))

    def per_batch(w):
        return pl.BlockSpec((None, ss, w), lambda b, h, *_: (b, 0, 0))

    grid_spec = pltpu.PrefetchScalarGridSpec(
        num_scalar_prefetch=2, grid=(bb, HEADS // hps),
        in_specs=[per_head(HEAD_PAD), per_head(HEAD_PAD), per_head(V_DIM), per_head(V_DIM),
                  rows_of_head(), rows_of_head(), per_batch(128), per_batch(128), per_batch(128), per_batch(1),
                  pl.BlockSpec((None, nq, 1, tq), lambda b, h, *_: (b, 0, 0, 0))],
        out_specs=[per_head(HEAD_PAD), per_head(NOPE), per_head(V_DIM),
                   pl.BlockSpec((None, None, ss, HEAD_PAD - NOPE), lambda b, h, *_: (b, h, 0, 0))],
        scratch_shapes=[pltpu.VMEM((ss, hps * HEAD_PAD), F32), pltpu.VMEM((tk, hps * HEAD_PAD), F32),
                        pltpu.VMEM((tk, hps * V_DIM), F32)])
    return pl.pallas_call(
        body, name="attn_bwd", grid_spec=grid_spec,
        out_shape=[jax.ShapeDtypeStruct((bb, ss, HEADS * HEAD_PAD), BF16),
                   jax.ShapeDtypeStruct((bb, ss, MLA_W), BF16),
                   jax.ShapeDtypeStruct((bb, ss, MLA_W), BF16),
                   jax.ShapeDtypeStruct((bb, HEADS // hps, ss, HEAD_PAD - NOPE), F32)],
        compiler_params=_params(56, ("arbitrary", "arbitrary")))(
            pmax, pmin, qcat, kcat, v, do, lse, dl, rc, rsa, rsb, pos_col, pos_blk)


def _bwd_mid(dqp, dkn, dv, dkr, xq, xkv, rc, rsa, rsb, wq, wkv, gq, gkv, dga, dgb, dpc, dz, w1, ts):
    bb, ss, _ = dz.shape
    nt = ss // ts
    hb = ts // POOL_HALO
    groups = dkr.shape[1]

    def body(dqp_ref, dkn_ref, dv_ref, dkr_ref, xq_ref, xkv_ref, c_ref, sa_ref, sb_ref, wq_ref, wkv_ref, gq_ref,
             gkv_ref, dga_ref, dgb_ref, dpc_ref, dph_ref, dz_ref, w1_ref,
             gx_ref, dh_ref, ggq_ref, ggkv_ref):
        step = pl.program_id(0)
        j = step % nt

        @pl.when(step == 0)
        def _():
            ggq_ref[...] = jnp.zeros(ggq_ref.shape, F32)
            ggkv_ref[...] = jnp.zeros(ggkv_ref.shape, F32)

        dkr = dkr_ref[0]
        for g in range(1, groups):
            dkr = dkr + dkr_ref[g]
        dkr = _rope_bwd(dkr, c_ref[...], sa_ref[...], sb_ref[...]).astype(BF16)

        def rms_bwd(x, g, dn):
            inv = lax.rsqrt(jnp.mean(x * x, axis=-1, keepdims=True) + RMS_EPS)
            xh = x * inv
            dxh = dn * g
            return inv * (dxh - xh * jnp.mean(dxh * xh, axis=-1, keepdims=True)), jnp.sum(dn * xh, axis=0, keepdims=True)

        dxq, ggq = rms_bwd(xq_ref[...], gq_ref[...], _dot(dqp_ref[...], wq_ref[...]))
        dkvn = _dot(dkn_ref[...], wkv_ref[:MLA_W, :]) + _dot(dv_ref[...], wkv_ref[MLA_W:, :])
        dxkv, ggkv = rms_bwd(xkv_ref[...], gkv_ref[...], dkvn)
        ggq_ref[...] += ggq
        ggkv_ref[...] += ggkv
        dpc = dpc_ref[...]
        n = ts + POOL_HALO
        ext = jnp.concatenate([dpc, jnp.where(j < nt - 1, dph_ref[...], 0.0)], axis=0)
        r2 = ext + pltpu.roll(ext, n - 1, 0)
        r4 = r2 + pltpu.roll(r2, n - 2, 0)
        r8 = r4 + pltpu.roll(r4, n - 4, 0)
        r16 = r8 + pltpu.roll(r8, n - 8, 0)
        du = _pick_groups(r2, r4, r8, r16, 0, ts) - dpc * _pool_cnt(j * ts, ts)
        dh = jnp.concatenate(
            [dxq.astype(BF16), dxkv.astype(BF16), dkr, dga_ref[...], du.astype(BF16), dgb_ref[...]], axis=1)
        dh_ref[...] = dh
        gx_ref[...] = ALPHA * dz_ref[...] + _dot(dh, w1_ref[...])

    def tile(w):
        return pl.BlockSpec((None, ts, w), lambda i: (i // nt, i % nt, 0))

    def whole(a):
        return pl.BlockSpec(a.shape, lambda i: (0, 0))

    halo = pl.BlockSpec((None, POOL_HALO, POOL_W),
                        lambda i: (i // nt, jnp.minimum((i % nt + 1) * hb, ss // POOL_HALO - 1), 0))
    return pl.pallas_call(
        body, name="bwd_mid", grid=(bb * nt,),
        in_specs=[tile(HEADS * HEAD_PAD), tile(MLA_W), tile(MLA_W),
                  pl.BlockSpec((None, groups, ts, HEAD_PAD - NOPE), lambda i: (i // nt, 0, i % nt, 0)),
                  tile(Q_LORA), tile(KV_LORA),
                  tile(128), tile(128), tile(128), whole(wq), whole(wkv), whole(gq), whole(gkv),
                  tile(MLA_W), tile(POOL_W), tile(POOL_W), halo, tile(D_MODEL), whole(w1)],
        out_specs=[tile(D_MODEL), tile(IN_WP),
                   pl.BlockSpec((1, Q_LORA), lambda i: (0, 0)), pl.BlockSpec((1, KV_LORA), lambda i: (0, 0))],
        out_shape=[jax.ShapeDtypeStruct((bb, ss, D_MODEL), F32), jax.ShapeDtypeStruct((bb, ss, IN_WP), BF16),
                   jax.ShapeDtypeStruct((1, Q_LORA), F32), jax.ShapeDtypeStruct((1, KV_LORA), F32)],
        compiler_params=_params(48, ("arbitrary",)))(
            dqp, dkn, dv, dkr, xq, xkv, rc, rsa, rsb, wq, wkv, gq, gkv, dga, dgb, dpc, dpc, dz, w1)


def _grad_w(pairs, bt, name, b_cols=None, carried=None, reduced=None, forward_at=2):
    tt = pairs[0][0].shape[0]
    steps = tt // bt
    npairs = len(pairs)
    n_rs = len(_rs_scratch(carried)) if carried is not None else 0

    def body(*refs):
        ab, rest = refs[:2 * npairs], list(refs[2 * npairs:])
        g_hbm = rest.pop(0) if carried is not None else None
        part = rest.pop(0) if reduced is not None else None
        outs = [rest.pop(0) for _ in range(npairs)]
        phases = []
        if carried is not None:
            rs_out = rest.pop(0)
        if reduced is not None:
            sum_ref = rest.pop(0)
        if carried is not None:
            phases.append(_rs_phases(carried, g_hbm, rs_out, *rest[:n_rs]))
        if reduced is not None:
            gathered, *sems = rest[n_rs:]
            phases.append(_ag_phases(part, gathered, *sems, sum_ref=sum_ref))
        step = pl.program_id(0)

        @pl.when(step == 0)
        def _():
            for o in outs:
                o[...] = jnp.zeros(o.shape, F32)
            for start, _, _ in phases:
                start()

        parts = [_dot_tn(ab[2 * i][...].astype(BF16), ab[2 * i + 1][...].astype(BF16)) for i in range(npairs)]
        for out, part in zip(outs, parts):
            out[...] += part

        @pl.when(step == min(forward_at, steps - 1))
        def _():
            for _, forward, _ in phases:
                forward()

        @pl.when(step == steps - 1)
        def _():
            for _, _, finish in phases:
                finish()

    in_specs, out_specs, out_shape = [], [], []
    for a, b in pairs:
        m, (n, col) = a.shape[1], b_cols or (b.shape[1], 0)
        in_specs += [pl.BlockSpec((bt, m), lambda i: (i, 0)), pl.BlockSpec((bt, n), lambda i, col=col: (i, col))]
        out_specs.append(pl.BlockSpec((m, n), lambda i: (0, 0)))
        out_shape.append(jax.ShapeDtypeStruct((m, n), F32))
    args = [t for p in pairs for t in p]
    scratch = []
    if carried is not None:
        in_specs.append(pl.BlockSpec(memory_space=pl.ANY))
        out_specs.append(pl.BlockSpec((carried.rows, carried.width), lambda i: (0, 0)))
        out_shape.append(jax.ShapeDtypeStruct((carried.rows, carried.width), F32))
        args.append(carried.array)
        scratch += _rs_scratch(carried)
    if reduced is not None:
        in_specs.append(pl.BlockSpec(memory_space=pl.ANY))
        out_specs.append(pl.BlockSpec(reduced.shape, lambda i: (0, 0)))
        out_shape.append(jax.ShapeDtypeStruct(reduced.shape, F32))
        args.append(reduced)
        scratch += [pltpu.VMEM((N_DEV * reduced.shape[0], reduced.shape[1]), F32)] + AG_SEMS
    return pl.pallas_call(
        body, name=name, grid=(steps,), in_specs=in_specs, out_specs=out_specs, out_shape=out_shape,
        scratch_shapes=scratch, compiler_params=_params(56, ("arbitrary",)))(*args)


def _adamw(triples):
    n = len(triples)

    def body(*refs):
        ins, outs = refs[:4 * n], refs[4 * n:]
        for i in range(n):
            w, g, m, v = (r[...] for r in ins[4 * i:4 * i + 4])
            m = ADAM_B1 * m + (1.0 - ADAM_B1) * g
            v = ADAM_B2 * v + (1.0 - ADAM_B2) * jnp.square(g)
            m_hat = m / (1.0 - ADAM_B1 ** ADAM_STEP)
            v_hat = v / (1.0 - ADAM_B2 ** ADAM_STEP)
            outs[3 * i][...] = -ADAM_LR * (m_hat / (jnp.sqrt(v_hat) + ADAM_EPS) + ADAM_WD * w)
            outs[3 * i + 1][...] = m
            outs[3 * i + 2][...] = v

    flat = [a for t in triples for a in t]
    vmem = pl.BlockSpec(memory_space=pltpu.VMEM)
    res = pl.pallas_call(
        body, name="adamw", in_specs=[vmem] * len(flat), out_specs=[vmem] * (3 * n),
        out_shape=[jax.ShapeDtypeStruct(t[0].shape, F32) for t in triples for _ in range(3)],
        compiler_params=_params(48))(*flat)
    return [tuple(res[3 * i:3 * i + 3]) for i in range(n)]


def _shard_slab(w_in, w_uq, w_ukv):
    mixed = jnp.concatenate(
        [_pad_rows(w_uq.T, R_MIX), w_ukv.T, jnp.zeros((R_MIX, 1024 - Q_LORA - KV_LORA), F32)], axis=1)
    return jnp.concatenate([mixed, w_in.T, jnp.zeros((SLAB_ROWS - O_IN - R_IN, 1024), F32)], axis=0)


def _unpack_weights(slabs):
    uq = slabs[:, O_MIX:O_MIX + R_UQ, :Q_LORA].reshape(HEADS, QK_DIM, Q_LORA)
    wqt = jnp.pad(uq, ((0, 0), (0, HEAD_PAD - QK_DIM), (0, 0))).reshape(HEADS * HEAD_PAD, Q_LORA)
    ukv = slabs[:, O_MIX:O_MIX + R_MIX, Q_LORA:Q_LORA + KV_LORA].reshape(HEADS, 2, NOPE, KV_LORA)
    wkvt = ukv.transpose(1, 0, 2, 3).reshape(2 * MLA_W, KV_LORA)
    raw = slabs[:, O_IN:O_IN + R_IN].reshape(IN_W, D_MODEL)
    w1t = jnp.concatenate([raw[:768 + ROPE], jnp.zeros((128 - ROPE, D_MODEL), BF16), raw[768 + ROPE:]], axis=0)
    return w1t, wqt, wkvt


def _mixed_band(g_wqt, g_wkvt):
    uq = g_wqt.reshape(HEADS, HEAD_PAD, Q_LORA)[:, :QK_DIM].reshape(N_DEV, R_UQ, Q_LORA)
    uq = jnp.pad(uq, ((0, 0), (0, R_MIX - R_UQ), (0, 0)))
    ukv = g_wkvt.reshape(2, HEADS, NOPE, KV_LORA).transpose(1, 0, 2, 3).reshape(N_DEV, R_MIX, KV_LORA)
    return jnp.concatenate([uq, ukv, jnp.zeros((N_DEV, R_MIX, 1024 - Q_LORA - KV_LORA), F32)], axis=2)


def _rope_rows():
    half = ROPE // 2
    inv_freq = ROPE_THETA ** (-jnp.arange(half, dtype=F32) / half)
    zero, one = jnp.zeros((half,), F32), jnp.ones((half,), F32)
    rows = [jnp.concatenate(r) for r in (
        (inv_freq, inv_freq, zero, zero), (one, one, zero, zero), (-one, zero, zero, zero), (zero, one, zero, zero))]
    return jnp.stack(rows + [jnp.zeros((128,), F32)] * 4)


def _pad_rows(a, rows):
    return jnp.pad(a, ((0, rows - a.shape[0]), (0, 0)))


def kernel(x, positions, w_in, q_norm_g, w_uq, kv_norm_g, w_ukv, pool_w, pool_scale, w_out, ln_g, ln_b, loss_target, m_w_in, m_q_norm_g, m_w_uq, m_kv_norm_g, m_w_ukv, m_pool_w, m_pool_scale, m_w_out, m_ln_g, m_ln_b, v_w_in, v_q_norm_g, v_w_uq, v_kv_norm_g, v_w_ukv, v_pool_w, v_pool_scale, v_w_out, v_ln_g, v_ln_b):
    bb, ss, _ = x.shape
    tt = bb * ss
    atile = min(512, ss)
    nblk = ss // atile

    slab = _shard_slab(w_in, w_uq, w_ukv).astype(BF16)
    slabs = _all_gather(slab, "gather_weights").reshape(N_DEV, SLAB_ROWS, 1024)
    w1, wq, wkv = _unpack_weights(slabs)

    gq, gkv = q_norm_g.reshape(1, Q_LORA), kv_norm_g.reshape(1, KV_LORA)
    ps = pool_scale.reshape(1, POOL_W)
    pos_col = positions.reshape(bb, ss, 1)
    pos_blk = positions.reshape(bb, nblk, 1, atile)
    pmax = jnp.max(positions.reshape(bb, 2 * nblk, atile // 2), axis=-1).reshape(-1)
    pmin = jnp.min(positions.reshape(bb, 2 * nblk, atile // 2), axis=-1).reshape(-1)

    x2 = x.reshape(tt, D_MODEL)
    xq, xkv, ga, u, gb, qn, kvn, qcat, kcat, v, rc, rsa, rsb, wo = _fwd_in(
        x2, w1, gq, gkv, wq, wkv, pos_col.reshape(tt, 1), _rope_rows(), w_out.astype(BF16), atile)
    as3 = lambda a: a.reshape(bb, ss, a.shape[-1])
    qcat, kcat, v = as3(qcat), as3(kcat), as3(v)
    o, lse = _attn_fwd(qcat, kcat, v, pos_col, pos_blk, pmax, pmin, atile, atile)
    dz, ycat, do, dga, dgb, dpc, dl, loss_p, g_lng, g_lnb, g_ps, g_pw = _post(
        x, loss_target, o, as3(ga), as3(u), as3(gb), wo, pool_w.astype(BF16), ps, ln_g, ln_b, atile)

    bt = min(1024, tt)
    rows5 = lambda a: a.reshape(bb, HEADS, nblk, 1, atile)
    rc, rsa, rsb = as3(rc), as3(rsa), as3(rsb)
    dqp, dkn, dv, dkr = _attn_bwd(
        qcat, kcat, v, do, rows5(lse), rows5(dl), rc, rsa, rsb, pos_col, pos_blk, pmax, pmin, atile, atile)
    g_wq, g_wkn, g_wv = _grad_w(
        [(dqp.reshape(tt, HEADS * HEAD_PAD), qn), (dkn.reshape(tt, MLA_W), kvn), (dv.reshape(tt, MLA_W), kvn)],
        min(2 * bt, tt), "grad_w_uqkv")
    g_wkv = jnp.concatenate([g_wkn, g_wv], axis=0)
    grad_x, dh, g_gq, g_gkv = _bwd_mid(
        dqp, dkn, dv, dkr, as3(xq), as3(xkv), rc, rsa, rsb, wq, wkv, gq, gkv, dga, dgb, dpc, dz, w1, atile)
    small = jnp.concatenate(
        [_pad_rows(g_lng.reshape(8, 128), 8), _pad_rows(g_lnb.reshape(8, 128), 8), _pad_rows(g_gq.reshape(4, 128), 8),
         _pad_rows(g_gkv.reshape(2, 128), 8), _pad_rows(g_ps.reshape(4, 128), 8), g_pw.reshape(POOL_G * POOL_C, 128),
         loss_p], axis=0)
    dh2, half = dh.reshape(tt, IN_WP), D_MODEL // 2
    g_w1a, rs_mix, small = _grad_w([(dh2, x2)], bt, "grad_w_in_a", b_cols=(half, 0),
                                   carried=_blocks(_mixed_band(g_wq, g_wkv)), reduced=small)
    g_w1b, rs_in_a = _grad_w([(dh2, x2)], bt, "grad_w_in_b", b_cols=(half, 1), carried=_w_in_blocks(g_w1a),
                             forward_at=1)
    g_wo, rs_in_b = _grad_w([(ycat.reshape(tt, D_MODEL), dz.reshape(tt, D_MODEL))], bt, "grad_w_out",
                            carried=_w_in_blocks(g_w1b), forward_at=1)
    rs_in = jnp.concatenate([rs_in_a, rs_in_b], axis=1)
    rs_out = _reduce_scatter(_blocks(g_wo.reshape(N_DEV, R_OUT, D_MODEL)), "reduce_scatter_w_out")
    loss = small[40 + POOL_G * POOL_C, 0]
    grads = {
        "w_in": rs_in[:R_IN],
        "q_norm_g": small[16:20].reshape(1, Q_LORA),
        "w_uq": rs_mix[:R_UQ, :Q_LORA],
        "kv_norm_g": small[24:26].reshape(1, KV_LORA),
        "w_ukv": rs_mix[:, Q_LORA:Q_LORA + KV_LORA].T,
        "pool_w": small[40:40 + POOL_G * POOL_C],
        "pool_scale": small[32:36].reshape(1, POOL_W),
        "w_out": rs_out,
        "ln_g": small[0:8].reshape(1, D_MODEL),
        "ln_b": small[8:16].reshape(1, D_MODEL),
    }
    transposed = ("w_in", "w_uq")

    names = ["w_in", "q_norm_g", "w_uq", "kv_norm_g", "w_ukv", "pool_w", "pool_scale", "w_out", "ln_g", "ln_b"]
    weights = dict(w_in=w_in, q_norm_g=q_norm_g, w_uq=w_uq, kv_norm_g=kv_norm_g, w_ukv=w_ukv, pool_w=pool_w,
                   pool_scale=pool_scale, w_out=w_out, ln_g=ln_g, ln_b=ln_b)
    moms = dict(w_in=(m_w_in, v_w_in), q_norm_g=(m_q_norm_g, v_q_norm_g), w_uq=(m_w_uq, v_w_uq),
                kv_norm_g=(m_kv_norm_g, v_kv_norm_g), w_ukv=(m_w_ukv, v_w_ukv), pool_w=(m_pool_w, v_pool_w),
                pool_scale=(m_pool_scale, v_pool_scale), w_out=(m_w_out, v_w_out), ln_g=(m_ln_g, v_ln_g),
                ln_b=(m_ln_b, v_ln_b))
    as2 = lambda a, n: a.T if n in transposed else a.reshape(grads[n].shape)
    upd = _adamw([(as2(weights[n], n), grads[n], as2(moms[n][0], n), as2(moms[n][1], n)) for n in names])
    shaped = lambda a, n: a.T if n in transposed else a.reshape(weights[n].shape)
    return (loss, grad_x,
            *[shaped(grads[n], n) for n in names],
            *[shaped(upd[i][0], n) for i, n in enumerate(names)],
            *[shaped(upd[i][1], n) for i, n in enumerate(names)],
            *[shaped(upd[i][2], n) for i, n in enumerate(names)])
```

```python
import jax
import jax.numpy as jnp
from jax import lax
from jax.experimental import pallas as pl
from jax.experimental.pallas import tpu as pltpu

F32 = jnp.float32
BF16 = jnp.bfloat16
MESH = pl.DeviceIdType.MESH

N_DEV = 8
D_MODEL = 1024
HEADS = 4
NOPE = 128
ROPE = 64
V_DIM = 128
QK_DIM = NOPE + ROPE
HEAD_PAD = 256
Q_LORA = 512
KV_LORA = 256
MLA_W = HEADS * V_DIM
POOL_W = 512
POOL_G = 4
POOL_C = 128
POOL_HALO = 16
IN_W = 2368
IN_WP = 2432
C_XQ, C_XKV, C_KR, C_GA, C_U, C_GB = 0, 512, 768, 896, 1408, 1920
ROPE_THETA = 10000.0
RMS_EPS = 1e-6
LN_EPS = 1e-5
ALPHA = 2.0 ** 0.25
SCALE = QK_DIM ** -0.5
LN2 = 0.6931471805599453
SCALE_LOG2 = SCALE / LN2
NEG = float(jnp.finfo(jnp.float32).min)

ADAM_LR = 0.001
ADAM_B1 = 0.9
ADAM_B2 = 0.999
ADAM_EPS = 1e-08
ADAM_WD = 0.01
ADAM_STEP = 10

R_OUT, R_MIX, R_UQ, R_IN = 128, 128, 96, 296
O_MIX, O_IN = 0, R_MIX
SLAB_ROWS = 432
R_IN_PAD = 304

V7X_VMEM_BYTES = 64 * 1024 * 1024
ATTN_FWD_HEADS_PER_STEP = 4
ATTN_BWD_HEADS_PER_STEP = 2


def _params(vmem_mb, semantics=None):
    assert vmem_mb * 1024 * 1024 < V7X_VMEM_BYTES
    return pltpu.CompilerParams(vmem_limit_bytes=vmem_mb * 1024 * 1024, dimension_semantics=semantics)


def _dot(a, b):
    return jnp.dot(a, b, preferred_element_type=F32)


def _dot_nt(a, b):
    return lax.dot_general(a, b, (((1,), (1,)), ((), ())), preferred_element_type=F32)


def _dot_tn(a, b):
    return lax.dot_general(a, b, (((0,), (0,)), ((), ())), preferred_element_type=F32)


def _rope_fwd(t, c, sa, sb):
    return t * c + pltpu.roll(t, 96, 1) * sa + pltpu.roll(t, 32, 1) * sb


def _rope_bwd(d, c, sa, sb):
    return d * c + pltpu.roll(d * sa, 32, 1) + pltpu.roll(d * sb, 96, 1)


def _silu_parts(g):
    sig = 0.5 * jnp.tanh(0.5 * g) + 0.5
    silu = g * sig
    return silu, sig + silu - silu * sig


def _pool_cnt(row0, rows):
    t = row0 + lax.broadcasted_iota(jnp.int32, (rows, POOL_W), 0)
    w = 2 << (lax.broadcasted_iota(jnp.int32, (rows, POOL_W), 1) // POOL_C)
    return jnp.minimum(t + 1, w).astype(F32)


def _pick_groups(s2, s4, s8, s16, lo, hi):
    return jnp.concatenate([s2[lo:hi, 0:128], s4[lo:hi, 128:256], s8[lo:hi, 256:384], s16[lo:hi, 384:512]], axis=1)


AG_SEMS = [pltpu.SemaphoreType.DMA((7,)), pltpu.SemaphoreType.DMA((7,)), pltpu.SemaphoreType.DMA]


def _ag_phases(x_ref, out_ref, send_sems, recv_sems, local_sem, sum_ref=None):
    m_per = x_ref.shape[0]
    x, y, c = lax.axis_index("x"), lax.axis_index("y"), lax.axis_index("c")
    me, sibling = (x, y, c), (x, y, 1 - c)
    chips = [(1 - x, y), (x, 1 - y), (1 - x, 1 - y)]

    def rows(px, py, pc):
        return out_ref.at[pl.ds((4 * px + 2 * py + pc) * m_per, m_per), :]

    def copy(k, block, to, src=None):
        return pltpu.make_async_remote_copy(
            src_ref=rows(*block) if src is None else src, dst_ref=rows(*block),
            send_sem=send_sems.at[k], recv_sem=recv_sems.at[k], device_id=to, device_id_type=MESH)

    def mine():
        return pltpu.make_async_copy(x_ref, rows(*me), local_sem)

    def first():
        return [copy(0, me, sibling, src=x_ref)] + [copy(1 + j, me, (*chip, c), src=x_ref) for j, chip in enumerate(chips)]

    def passed():
        return [copy(4 + j, (*chip, c), sibling) for j, chip in enumerate(chips)]

    def start():
        for cp in [mine()] + first():
            cp.start()

    def forward():
        for j, (chip, cp) in enumerate(zip(chips, passed())):
            copy(1 + j, (*chip, c), me).wait_recv()
            cp.start()

    def finish():
        copy(0, sibling, me).wait_recv()
        for j, chip in enumerate(chips):
            copy(4 + j, (*chip, 1 - c), me).wait_recv()
        for cp in first() + passed():
            cp.wait_send()
        mine().wait()
        if sum_ref is not None:
            acc = out_ref[pl.ds(0, m_per), :]
            for d in range(1, N_DEV):
                acc = acc + out_ref[pl.ds(d * m_per, m_per), :]
            sum_ref[...] = acc

    return start, forward, finish


def _all_gather(shard, name):
    m_per, n = shard.shape

    def body(x_ref, out_ref, *sems):
        for phase in _ag_phases(x_ref, out_ref, *sems):
            phase()

    vmem = pl.BlockSpec(memory_space=pltpu.VMEM)
    return pl.pallas_call(
        body, name=name, out_shape=jax.ShapeDtypeStruct((N_DEV * m_per, n), shard.dtype),
        in_specs=[vmem], out_specs=vmem, scratch_shapes=AG_SEMS, compiler_params=_params(32))(shard)


def _reduce_scatter(sc, name):
    def body(g_hbm, out_ref, *scratch):
        for phase in _rs_phases(sc, g_hbm, out_ref, *scratch):
            phase()

    return pl.pallas_call(
        body, name=name, out_shape=jax.ShapeDtypeStruct((sc.rows, sc.width), F32),
        in_specs=[pl.BlockSpec(memory_space=pl.ANY)], out_specs=pl.BlockSpec(memory_space=pltpu.VMEM),
        scratch_shapes=_rs_scratch(sc), compiler_params=_params(32))(sc.array)


class _Scattered:
    def __init__(self, array, rows, pieces, locate):
        self.array, self.rows, self.pieces, self.locate = array, rows, pieces, locate
        self.width = array.shape[-1]


def _blocks(g):
    return _Scattered(g, g.shape[1], ((0, g.shape[1]),), lambda ref, d, row, rows: ref.at[d])


def _w_in_blocks(g_w1t):
    def locate(ref, d, row, rows):
        r = R_IN * d + row
        return ref.at[pl.ds(pl.multiple_of(r + jnp.where(r >= C_KR + ROPE, 128 - ROPE, 0), 8), rows), :]

    cut = C_KR + ROPE - 2 * R_IN
    return _Scattered(g_w1t, R_IN_PAD, ((0, cut), (cut, R_IN - cut)), locate)


def _rs_scratch(sc):
    rr, ww, n = sc.rows, sc.width, 4 * len(sc.pieces)
    return [pltpu.VMEM((4, rr, ww), F32), pltpu.VMEM((4, rr, ww), F32),
            pltpu.VMEM((3, rr, ww), BF16), pltpu.VMEM((3, rr, ww), BF16),
            pltpu.SemaphoreType.DMA((n,)), pltpu.SemaphoreType.DMA((n,)), pltpu.SemaphoreType.DMA((n,)),
            pltpu.SemaphoreType.DMA((3,)), pltpu.SemaphoreType.DMA((3,))]


def _rs_phases(sc, g_hbm, out_ref, own_ref, recv1_ref, sendb_ref, recv2_ref, ld_sems, s1_send, s1_recv, s2_send, s2_recv):
    rr, ww = out_ref.shape
    chunk = next(c for c in (128, 80, 64, 48, 32, 16) if rr % c == 0)
    x, y, c = lax.axis_index("x"), lax.axis_index("y"), lax.axis_index("c")
    chips = [(1 - x, y), (x, 1 - y), (1 - x, 1 - y)]
    npieces = len(sc.pieces)
    filled = sum(rows for _, rows in sc.pieces)

    def pieces(d, buf, k):
        for p, (row, rows) in enumerate(sc.pieces):
            dst = buf.at[k] if (row, rows) == (0, rr) else buf.at[k, pl.ds(row, rows), :]
            yield k * npieces + p, sc.locate(g_hbm, d, row, rows), dst

    def loads():
        return [pltpu.make_async_copy(src, dst, ld_sems.at[s])
                for k in range(4) for s, src, dst in pieces(2 * k + c, own_ref, k)]

    def stage1():
        return [pltpu.make_async_remote_copy(
            src_ref=src, dst_ref=dst, send_sem=s1_send.at[s], recv_sem=s1_recv.at[s],
            device_id=(x, y, 1 - c), device_id_type=MESH)
            for k in range(4) for s, src, dst in pieces(2 * k + (1 - c), recv1_ref, k)]

    def stage2():
        return [pltpu.make_async_remote_copy(
            src_ref=sendb_ref.at[r], dst_ref=recv2_ref.at[r], send_sem=s2_send.at[r],
            recv_sem=s2_recv.at[r], device_id=(cx, cy, c), device_id_type=MESH) for r, (cx, cy) in enumerate(chips)]

    def start():
        if filled < rr:
            own_ref[:, filled:rr, :] = jnp.zeros((4, rr - filled, ww), F32)
            recv1_ref[:, filled:rr, :] = jnp.zeros((4, rr - filled, ww), F32)
        for cp in loads() + stage1():
            cp.start()

    def forward():
        for cp in loads():
            cp.wait()
        for cp in stage1():
            cp.wait_recv()
        sends = stage2()
        for r, (cx, cy) in enumerate(chips):
            kk = 2 * cx + cy

            def pack(i, carry, r=r, kk=kk):
                rows = pl.ds(pl.multiple_of(i * chunk, chunk), chunk)
                sendb_ref[r, rows, :] = (own_ref[kk, rows, :] + recv1_ref[kk, rows, :]).astype(BF16)
                return carry

            lax.fori_loop(0, rr // chunk, pack, 0)
            sends[r].start()

    def finish():
        for cp in stage2():
            cp.wait_recv()
        mine = 2 * x + y

        def total(i, carry):
            rows = pl.ds(pl.multiple_of(i * chunk, chunk), chunk)
            acc = own_ref[mine, rows, :] + recv1_ref[mine, rows, :]
            for r in range(3):
                acc = acc + recv2_ref[r, rows, :].astype(F32)
            out_ref[rows, :] = acc
            return carry

        lax.fori_loop(0, rr // chunk, total, 0)
        for cp in stage1() + stage2():
            cp.wait_send()

    return start, forward, finish


def _fwd_in(x2, w1, gq, gkv, wq, wkv, pos, rope_rows, shard, tm):
    tt = x2.shape[0]
    steps = tt // tm
    gathered_shape = (N_DEV * shard.shape[0], shard.shape[1])

    def body(x_ref, w1_ref, gq_ref, gkv_ref, wq_ref, wkv_ref, pos_ref, rr_ref, shard_ref,
             xq_ref, xkv_ref, ga_ref, u_ref, gb_ref, qn_ref, kvn_ref, qcat_ref, kcat_ref, v_ref,
             c_ref, sa_ref, sb_ref, all_ref, gathered, *sems):
        step = pl.program_id(0)
        start, forward, finish = _ag_phases(shard_ref, gathered, *sems)
        pl.when(step == 0)(start)
        pl.when(step == min(2, steps - 1))(forward)

        @pl.when(step == steps - 1)
        def _():
            finish()
            all_ref[...] = gathered[...]

        ang = pos_ref[...].astype(F32) * rr_ref[0:1, :]
        cos, sin = jnp.cos(ang), jnp.sin(ang)
        c, sa, sb = cos * rr_ref[1:2, :], sin * rr_ref[2:3, :], sin * rr_ref[3:4, :]
        h = _dot_nt(x_ref[...].astype(BF16), w1_ref[...])
        xq = h[:, C_XQ:C_XKV]
        xkv = h[:, C_XKV:C_KR]
        qn = (xq * lax.rsqrt(jnp.mean(xq * xq, axis=-1, keepdims=True) + RMS_EPS) * gq_ref[...]).astype(BF16)
        kvn = (xkv * lax.rsqrt(jnp.mean(xkv * xkv, axis=-1, keepdims=True) + RMS_EPS) * gkv_ref[...]).astype(BF16)
        q = _dot_nt(qn, wq_ref[...]) * SCALE_LOG2
        kv = _dot_nt(kvn, wkv_ref[...])
        kr = _rope_fwd(h[:, C_KR:C_GA], c, sa, sb).astype(BF16)
        c_ref[...] = c
        sa_ref[...] = sa
        sb_ref[...] = sb
        xq_ref[...] = xq
        xkv_ref[...] = xkv
        ga_ref[...] = h[:, C_GA:C_U]
        u_ref[...] = h[:, C_U:C_GB]
        gb_ref[...] = h[:, C_GB:IN_WP]
        qn_ref[...] = qn
        kvn_ref[...] = kvn
        for hd in range(HEADS):
            lo = hd * HEAD_PAD
            qcat_ref[:, lo:lo + NOPE] = q[:, lo:lo + NOPE].astype(BF16)
            qcat_ref[:, lo + NOPE:lo + HEAD_PAD] = _rope_fwd(q[:, lo + NOPE:lo + HEAD_PAD], c, sa, sb).astype(BF16)
            kcat_ref[:, lo:lo + NOPE] = kv[:, hd * NOPE:(hd + 1) * NOPE].astype(BF16)
            kcat_ref[:, lo + NOPE:lo + HEAD_PAD] = kr
        v_ref[...] = kv[:, MLA_W:].astype(BF16)

    def tile(w):
        return pl.BlockSpec((tm, w), lambda i: (i, 0))

    def whole(a):
        return pl.BlockSpec(a.shape, lambda i: (0, 0))

    outs = [(Q_LORA, F32), (KV_LORA, F32), (MLA_W, F32), (POOL_W, F32), (POOL_W, F32),
            (Q_LORA, BF16), (KV_LORA, BF16), (HEADS * HEAD_PAD, BF16), (HEADS * HEAD_PAD, BF16), (MLA_W, BF16),
            (128, F32), (128, F32), (128, F32)]
    return pl.pallas_call(
        body, name="fwd_in", grid=(steps,),
        in_specs=[tile(D_MODEL), whole(w1), whole(gq), whole(gkv), whole(wq), whole(wkv), tile(1), whole(rope_rows),
                  pl.BlockSpec(memory_space=pl.ANY)],
        out_specs=[tile(w) for w, _ in outs] + [pl.BlockSpec(gathered_shape, lambda i: (0, 0))],
        out_shape=[jax.ShapeDtypeStruct((tt, w), dt) for w, dt in outs]
        + [jax.ShapeDtypeStruct(gathered_shape, shard.dtype)],
        scratch_shapes=[pltpu.VMEM(gathered_shape, shard.dtype)] + AG_SEMS,
        compiler_params=_params(56, ("arbitrary",)))(x2, w1, gq, gkv, wq, wkv, pos, rope_rows, shard)


def _tile_cases(pmax_ref, pmin_ref, b, nhalf, qi, ki):
    q0, q1 = b * nhalf + 2 * qi, b * nhalf + 2 * qi + 1
    k0, k1 = b * nhalf + 2 * ki, b * nhalf + 2 * ki + 1
    needed = jnp.maximum(pmax_ref[q0], pmax_ref[q1]) >= jnp.minimum(pmin_ref[k0], pmin_ref[k1])
    visible = jnp.minimum(pmin_ref[q0], pmin_ref[q1]) >= jnp.maximum(pmax_ref[k0], pmax_ref[k1])
    stepped = pmax_ref[q0] < pmin_ref[k1]
    return needed, visible, stepped


def _attn_fwd(qcat, kcat, v, pos_col, pos_blk, pmax, pmin, tq, tk):
    bb, ss, _ = qcat.shape
    nq, nk = ss // tq, ss // tk
    lanes = 128
    hps = ATTN_FWD_HEADS_PER_STEP

    def body(pmax_ref, pmin_ref, q_ref, k_ref, v_ref, pc_ref, pb_ref, o_ref, lse_ref, m_s, acc_s):
        b, qi = pl.program_id(0), pl.program_id(2)
        m_s[...] = jnp.full(m_s.shape, NEG, F32)
        acc_s[...] = jnp.zeros(acc_s.shape, F32)

        def part(ki, masked, q_lo, q_n, k_n, k_lo=0):
            qrows = slice(q_lo, q_lo + q_n)
            krows = pl.ds(pl.multiple_of(ki * tk + k_lo, k_n), k_n)
            if masked:
                mask = pc_ref[qrows, :] >= pb_ref[ki][:, k_lo:k_lo + k_n]
            ones = jnp.ones((k_n, lanes), BF16)
            parts = []
            for hd in range(hps):
                qk = slice(hd * HEAD_PAD, (hd + 1) * HEAD_PAD)
                s = _dot_nt(q_ref[qrows, qk], k_ref[krows, qk])
                if masked:
                    s = jnp.where(mask, s, NEG)
                m_prev = m_s[hd, qrows, :]
                m_new = jnp.maximum(m_prev, jnp.max(s, axis=-1, keepdims=True))
                p = jnp.exp2(s - jnp.tile(m_new, (1, k_n // lanes)))
                a = jnp.exp2(m_prev - m_new)
                vv = jnp.concatenate([v_ref[krows, hd * V_DIM:(hd + 1) * V_DIM], ones], axis=1)
                parts.append((m_new, jnp.tile(a, (1, 2)) * acc_s[hd, qrows, :] + _dot(p.astype(BF16), vv)))
            for hd, (m_new, acc) in enumerate(parts):
                acc_s[hd, qrows, :] = acc
                m_s[hd, qrows, :] = m_new

        def step(ki, carry):
            needed, visible, stepped = _tile_cases(pmax_ref, pmin_ref, b, 2 * nq, qi, ki)

            @pl.when(needed & visible)
            def _():
                part(ki, False, 0, tq, tk // 2)
                part(ki, False, 0, tq, tk // 2, tk // 2)

            @pl.when(needed & ~visible & stepped)
            def _():
                part(ki, True, 0, tq // 2, tk // 2)
                part(ki, True, tq // 2, tq // 2, tk)

            @pl.when(needed & ~visible & ~stepped)
            def _():
                part(ki, True, 0, tq, tk)

            return carry

        lax.fori_loop(0, nk, step, 0)
        for hd in range(hps):
            acc = acc_s[hd]
            l = acc[:, V_DIM:]
            o_ref[:, hd * V_DIM:(hd + 1) * V_DIM] = acc[:, :V_DIM] / l
            lse_ref[hd] = (m_s[hd] + jnp.log2(l)).T[0:1, :]

    grid_spec = pltpu.PrefetchScalarGridSpec(
        num_scalar_prefetch=2, grid=(bb, HEADS // hps, nq),
        in_specs=[
            pl.BlockSpec((None, tq, hps * HEAD_PAD), lambda b, h, i, *_: (b, i, h)),
            pl.BlockSpec((None, ss, hps * HEAD_PAD), lambda b, h, i, *_: (b, 0, h)),
            pl.BlockSpec((None, ss, hps * V_DIM), lambda b, h, i, *_: (b, 0, h)),
            pl.BlockSpec((None, tq, 1), lambda b, h, i, *_: (b, i, 0)),
            pl.BlockSpec((None, nk, 1, tk), lambda b, h, i, *_: (b, 0, 0, 0)),
        ],
        out_specs=[
            pl.BlockSpec((None, tq, hps * V_DIM), lambda b, h, i, *_: (b, i, h)),
            pl.BlockSpec((None, hps, 1, tq), lambda b, h, i, *_: (b, h, 0, i)),
        ],
        scratch_shapes=[pltpu.VMEM((hps, tq, lanes), F32), pltpu.VMEM((hps, tq, 2 * V_DIM), F32)])
    return pl.pallas_call(
        body, name="attn_fwd", grid_spec=grid_spec,
        out_shape=[jax.ShapeDtypeStruct((bb, ss, MLA_W), F32), jax.ShapeDtypeStruct((bb, HEADS, 1, ss), F32)],
        compiler_params=_params(48, ("arbitrary", "arbitrary", "arbitrary")))(pmax, pmin, qcat, kcat, v, pos_col, pos_blk)


def _post(x, tgt, o, ga, u, gb, w_out, pool_wb, pool_scale, ln_g, ln_b, ts):
    bb, ss, _ = x.shape
    nt = ss // ts
    hb = ts // POOL_HALO

    def body(x_ref, tgt_ref, o_ref, ga_ref, u_ref, uh_ref, gb_ref, wo_ref, pw_ref, ps_ref, lg_ref, lb_ref,
             dz_ref, ycat_ref, do_ref, dga_ref, dgb_ref, dpc_ref, dl_ref, loss_ref, glg_ref, glb_ref, gps_ref, gpw_ref):
        step = pl.program_id(0)
        j = step % nt

        @pl.when(step == 0)
        def _():
            for r in (loss_ref, glg_ref, glb_ref, gps_ref, gpw_ref):
                r[...] = jnp.zeros(r.shape, F32)

        o, ga, u, gb = o_ref[...], ga_ref[...], u_ref[...], gb_ref[...]
        sa, dsa = _silu_parts(ga)
        sb, dsb = _silu_parts(gb)
        ext = jnp.concatenate([jnp.where(j > 0, uh_ref[...], 0.0), u], axis=0)
        s2 = ext + pltpu.roll(ext, 1, 0)
        s4 = s2 + pltpu.roll(s2, 2, 0)
        s8 = s4 + pltpu.roll(s4, 4, 0)
        s16 = s8 + pltpu.roll(s8, 8, 0)
        cnt = _pool_cnt(j * ts, ts)
        rcnt = 1.0 / cnt
        pooled = (_pick_groups(s2, s4, s8, s16, POOL_HALO, POOL_HALO + ts) * rcnt - u).astype(BF16)
        mixed = jnp.concatenate(
            [_dot(pooled[:, g * POOL_C:(g + 1) * POOL_C], pw_ref[g]) for g in range(POOL_G)], axis=1)
        ps = ps_ref[...]
        scaled = mixed * ps
        ycat = jnp.concatenate([o * sa, scaled * sb], axis=1).astype(BF16)
        z = ALPHA * x_ref[...] + _dot(ycat, wo_ref[...])
        mu = jnp.mean(z, axis=-1, keepdims=True)
        zc = z - mu
        rstd = lax.rsqrt(jnp.mean(zc * zc, axis=-1, keepdims=True) + LN_EPS)
        xhat = zc * rstd
        lg = lg_ref[...]
        diff = xhat * lg + lb_ref[...] - tgt_ref[...]
        loss = jnp.sum(diff * diff) * (0.5 / D_MODEL)
        glb = jnp.sum(diff, axis=0, keepdims=True) * (1.0 / D_MODEL)
        glg = jnp.sum(diff * xhat, axis=0, keepdims=True) * (1.0 / D_MODEL)
        dxh = diff * (lg * (1.0 / D_MODEL))
        dz = rstd * (dxh - jnp.mean(dxh, axis=-1, keepdims=True) - xhat * jnp.mean(dxh * xhat, axis=-1, keepdims=True))
        dycat = _dot_nt(dz.astype(BF16), wo_ref[...])
        dya, dyb = dycat[:, :MLA_W], dycat[:, MLA_W:]
        do = dya * sa
        doo = do * o
        dl = [jnp.sum(doo[:, hd * V_DIM:(hd + 1) * V_DIM].T, axis=0, keepdims=True) for hd in range(HEADS)]
        dga = (dya * o * dsa).astype(BF16)
        dgb = (dyb * scaled * dsb).astype(BF16)
        dscaled = dyb * sb
        gps = jnp.sum(dscaled * mixed, axis=0, keepdims=True)
        dmixed = (dscaled * ps).astype(BF16)
        groups = [slice(g * POOL_C, (g + 1) * POOL_C) for g in range(POOL_G)]
        gpw = [_dot_tn(pooled[:, cols], dmixed[:, cols]) for cols in groups]
        dpooled = [_dot_nt(dmixed[:, cols], pw_ref[g]) for g, cols in enumerate(groups)]
        ycat_ref[...] = ycat
        dz_ref[...] = dz
        do_ref[...] = do.astype(BF16)
        dga_ref[...] = dga
        dgb_ref[...] = dgb
        dpc_ref[...] = jnp.concatenate(dpooled, axis=1) * rcnt
        for hd in range(HEADS):
            dl_ref[hd] = dl[hd]
        for g in range(POOL_G):
            gpw_ref[g] += gpw[g]
        loss_ref[...] += loss
        glb_ref[...] += glb
        glg_ref[...] += glg
        gps_ref[...] += gps

    def tile(w):
        return pl.BlockSpec((None, ts, w), lambda i: (i // nt, i % nt, 0))

    def whole(a):
        nd = a.ndim
        return pl.BlockSpec(a.shape, lambda i: (0,) * nd)

    halo = pl.BlockSpec((None, POOL_HALO, POOL_W), lambda i: (i // nt, jnp.maximum((i % nt) * hb - 1, 0), 0))
    acc_shapes = [(8, 128), (1, D_MODEL), (1, D_MODEL), (1, POOL_W), (POOL_G, POOL_C, POOL_C)]
    tile_outs = [(D_MODEL, F32), (D_MODEL, BF16), (MLA_W, BF16), (MLA_W, BF16), (POOL_W, BF16), (POOL_W, F32)]
    return pl.pallas_call(
        body, name="post", grid=(bb * nt,),
        in_specs=[tile(D_MODEL), tile(D_MODEL), tile(MLA_W), tile(MLA_W), tile(POOL_W), halo, tile(POOL_W),
                  whole(w_out), whole(pool_wb), whole(pool_scale), whole(ln_g), whole(ln_b)],
        out_specs=[tile(w) for w, _ in tile_outs]
        + [pl.BlockSpec((None, HEADS, 1, ts), lambda i: (i // nt, 0, 0, i % nt))]
        + [pl.BlockSpec(s, lambda i, n=len(s): (0,) * n) for s in acc_shapes],
        out_shape=[jax.ShapeDtypeStruct((bb, ss, w), dt) for w, dt in tile_outs]
        + [jax.ShapeDtypeStruct((bb, HEADS, 1, ss), F32)]
        + [jax.ShapeDtypeStruct(s, F32) for s in acc_shapes],
        compiler_params=_params(48, ("arbitrary",)))(x, tgt, o, ga, u, u, gb, w_out, pool_wb, pool_scale, ln_g, ln_b)


def _attn_bwd(qcat, kcat, v, do, lse, dl, rc, rsa, rsb, pos_col, pos_blk, pmax, pmin, tq, tk):
    bb, ss, _ = qcat.shape
    nq, nk = ss // tq, ss // tk
    hps = ATTN_BWD_HEADS_PER_STEP

    def body(pmax_ref, pmin_ref, q_ref, k_ref, v_ref, do_ref, lse_ref, dl_ref, c_ref, sa_ref, sb_ref, pc_ref, pb_ref,
             dqp_ref, dkn_ref, dv_ref, dkr_ref, dq_s, dk_s, dv_s):
        b = pl.program_id(0)
        dq_s[...] = jnp.zeros(dq_s.shape, F32)

        def part(qi, ki, masked):
            krows = pl.ds(pl.multiple_of(ki * tk, tk), tk)
            qrows = pl.ds(pl.multiple_of(qi * tq, tq), tq)
            if masked:
                mask = pb_ref[qi] >= pc_ref[krows, :]
            parts = []
            for hd in range(hps):
                qk = slice(hd * HEAD_PAD, (hd + 1) * HEAD_PAD)
                vs = slice(hd * V_DIM, (hd + 1) * V_DIM)
                q = q_ref[qrows, qk]
                dd = do_ref[qrows, vs]
                st = _dot_nt(k_ref[krows, qk], q)
                if masked:
                    st = jnp.where(mask, st, NEG)
                pt = jnp.exp2(st - lse_ref[hd, qi])
                dpt = _dot_nt(v_ref[krows, vs], dd)
                dst = (pt * (dpt - dl_ref[hd, qi])).astype(BF16)
                parts.append((_dot(pt.astype(BF16), dd), _dot(dst, q), _dot_tn(dst, k_ref[krows, qk])))
            for hd, (dv_part, dk_part, dq_part) in enumerate(parts):
                dv_s[:, hd * V_DIM:(hd + 1) * V_DIM] += dv_part
                dk_s[:, hd * HEAD_PAD:(hd + 1) * HEAD_PAD] += dk_part
                dq_s[qrows, hd * HEAD_PAD:(hd + 1) * HEAD_PAD] += dq_part

        def kv_step(ki, carry):
            krows = pl.ds(pl.multiple_of(ki * tk, tk), tk)
            dk_s[...] = jnp.zeros(dk_s.shape, F32)
            dv_s[...] = jnp.zeros(dv_s.shape, F32)

            def q_step(qi, c2):
                needed, visible, _ = _tile_cases(pmax_ref, pmin_ref, b, 2 * nq, qi, ki)

                @pl.when(needed & visible)
                def _():
                    part(qi, ki, False)

                @pl.when(needed & ~visible)
                def _():
                    part(qi, ki, True)

                return c2

            lax.fori_loop(0, nq, q_step, 0)
            dkr = jnp.zeros((tk, HEAD_PAD - NOPE), F32)
            for hd in range(hps):
                lo = hd * HEAD_PAD
                dkn_ref[krows, hd * NOPE:(hd + 1) * NOPE] = (dk_s[:, lo:lo + NOPE] * LN2).astype(BF16)
                dkr = dkr + dk_s[:, lo + NOPE:lo + HEAD_PAD]
            dkr_ref[krows, :] = dkr * LN2
            dv_ref[krows, :] = dv_s[...].astype(BF16)
            return carry

        lax.fori_loop(0, nk, kv_step, 0)
        c, sa, sb = c_ref[...], sa_ref[...], sb_ref[...]
        for hd in range(hps):
            lo = hd * HEAD_PAD
            dqp_ref[:, lo:lo + NOPE] = (dq_s[:, lo:lo + NOPE] * SCALE).astype(BF16)
            dqp_ref[:, lo + NOPE:lo + HEAD_PAD] = _rope_bwd(
                dq_s[:, lo + NOPE:lo + HEAD_PAD] * SCALE, c, sa, sb).astype(BF16)

    def per_head(w):
        return pl.BlockSpec((None, ss, hps * w), lambda b, h, *_: (b, 0, h))

    def rows_of_head():
        return pl.BlockSpec((None, hps, nq, 1, tq), lambda b, h, *_: (b, h, 0, 0, 0))

    def per_batch(w):
        return pl.BlockSpec((None, ss, w), lambda b, h, *_: (b, 0, 0))

    grid_spec = pltpu.PrefetchScalarGridSpec(
        num_scalar_prefetch=2, grid=(bb, HEADS // hps),
        in_specs=[per_head(HEAD_PAD), per_head(HEAD_PAD), per_head(V_DIM), per_head(V_DIM),
                  rows_of_head(), rows_of_head(), per_batch(128), per_batch(128), per_batch(128), per_batch(1),
                  pl.BlockSpec((None, nq, 1, tq), lambda b, h, *_: (b, 0, 0, 0))],
        out_specs=[per_head(HEAD_PAD), per_head(NOPE), per_head(V_DIM),
                   pl.BlockSpec((None, None, ss, HEAD_PAD - NOPE), lambda b, h, *_: (b, h, 0, 0))],
        scratch_shapes=[pltpu.VMEM((ss, hps * HEAD_PAD), F32), pltpu.VMEM((tk, hps * HEAD_PAD), F32),
                        pltpu.VMEM((tk, hps * V_DIM), F32)])
    return pl.pallas_call(
        body, name="attn_bwd", grid_spec=grid_spec,
        out_shape=[jax.ShapeDtypeStruct((bb, ss, HEADS * HEAD_PAD), BF16),
                   jax.ShapeDtypeStruct((bb, ss, MLA_W), BF16),
                   jax.ShapeDtypeStruct((bb, ss, MLA_W), BF16),
                   jax.ShapeDtypeStruct((bb, HEADS // hps, ss, HEAD_PAD - NOPE), F32)],
        compiler_params=_params(56, ("arbitrary", "arbitrary")))(
            pmax, pmin, qcat, kcat, v, do, lse, dl, rc, rsa, rsb, pos_col, pos_blk)


def _bwd_mid(dqp, dkn, dv, dkr, xq, xkv, rc, rsa, rsb, wq, wkv, gq, gkv, dga, dgb, dpc, dz, w1, ts):
    bb, ss, _ = dz.shape
    nt = ss // ts
    hb = ts // POOL_HALO
    groups = dkr.shape[1]

    def body(dqp_ref, dkn_ref, dv_ref, dkr_ref, xq_ref, xkv_ref, c_ref, sa_ref, sb_ref, wq_ref, wkv_ref, gq_ref,
             gkv_ref, dga_ref, dgb_ref, dpc_ref, dph_ref, dz_ref, w1_ref,
             gx_ref, dh_ref, ggq_ref, ggkv_ref):
        step = pl.program_id(0)
        j = step % nt

        @pl.when(step == 0)
        def _():
            ggq_ref[...] = jnp.zeros(ggq_ref.shape, F32)
            ggkv_ref[...] = jnp.zeros(ggkv_ref.shape, F32)

        dkr = dkr_ref[0]
        for g in range(1, groups):
            dkr = dkr + dkr_ref[g]
        dkr = _rope_bwd(dkr, c_ref[...], sa_ref[...], sb_ref[...]).astype(BF16)

        def rms_bwd(x, g, dn):
            inv = lax.rsqrt(jnp.mean(x * x, axis=-1, keepdims=True) + RMS_EPS)
            xh = x * inv
            dxh = dn * g
            return inv * (dxh - xh * jnp.mean(dxh * xh, axis=-1, keepdims=True)), jnp.sum(dn * xh, axis=0, keepdims=True)

        dxq, ggq = rms_bwd(xq_ref[...], gq_ref[...], _dot(dqp_ref[...], wq_ref[...]))
        dkvn = _dot(dkn_ref[...], wkv_ref[:MLA_W, :]) + _dot(dv_ref[...], wkv_ref[MLA_W:, :])
        dxkv, ggkv = rms_bwd(xkv_ref[...], gkv_ref[...], dkvn)
        ggq_ref[...] += ggq
        ggkv_ref[...] += ggkv
        dpc = dpc_ref[...]
        n = ts + POOL_HALO
        ext = jnp.concatenate([dpc, jnp.where(j < nt - 1, dph_ref[...], 0.0)], axis=0)
        r2 = ext + pltpu.roll(ext, n - 1, 0)
        r4 = r2 + pltpu.roll(r2, n - 2, 0)
        r8 = r4 + pltpu.roll(r4, n - 4, 0)
        r16 = r8 + pltpu.roll(r8, n - 8, 0)
        du = _pick_groups(r2, r4, r8, r16, 0, ts) - dpc * _pool_cnt(j * ts, ts)
        dh = jnp.concatenate(
            [dxq.astype(BF16), dxkv.astype(BF16), dkr, dga_ref[...], du.astype(BF16), dgb_ref[...]], axis=1)
        dh_ref[...] = dh
        gx_ref[...] = ALPHA * dz_ref[...] + _dot(dh, w1_ref[...])

    def tile(w):
        return pl.BlockSpec((None, ts, w), lambda i: (i // nt, i % nt, 0))

    def whole(a):
        return pl.BlockSpec(a.shape, lambda i: (0, 0))

    halo = pl.BlockSpec((None, POOL_HALO, POOL_W),
                        lambda i: (i // nt, jnp.minimum((i % nt + 1) * hb, ss // POOL_HALO - 1), 0))
    return pl.pallas_call(
        body, name="bwd_mid", grid=(bb * nt,),
        in_specs=[tile(HEADS * HEAD_PAD), tile(MLA_W), tile(MLA_W),
                  pl.BlockSpec((None, groups, ts, HEAD_PAD - NOPE), lambda i: (i // nt, 0, i % nt, 0)),
                  tile(Q_LORA), tile(KV_LORA),
                  tile(128), tile(128), tile(128), whole(wq), whole(wkv), whole(gq), whole(gkv),
                  tile(MLA_W), tile(POOL_W), tile(POOL_W), halo, tile(D_MODEL), whole(w1)],
        out_specs=[tile(D_MODEL), tile(IN_WP),
                   pl.BlockSpec((1, Q_LORA), lambda i: (0, 0)), pl.BlockSpec((1, KV_LORA), lambda i: (0, 0))],
        out_shape=[jax.ShapeDtypeStruct((bb, ss, D_MODEL), F32), jax.ShapeDtypeStruct((bb, ss, IN_WP), BF16),
                   jax.ShapeDtypeStruct((1, Q_LORA), F32), jax.ShapeDtypeStruct((1, KV_LORA), F32)],
        compiler_params=_params(48, ("arbitrary",)))(
            dqp, dkn, dv, dkr, xq, xkv, rc, rsa, rsb, wq, wkv, gq, gkv, dga, dgb, dpc, dpc, dz, w1)


def _grad_w(pairs, bt, name, b_cols=None, carried=None, reduced=None):
    tt = pairs[0][0].shape[0]
    steps = tt // bt
    npairs = len(pairs)
    n_rs = len(_rs_scratch(carried)) if carried is not None else 0

    def body(*refs):
        ab, rest = refs[:2 * npairs], list(refs[2 * npairs:])
        g_hbm = rest.pop(0) if carried is not None else None
        part = rest.pop(0) if reduced is not None else None
        outs = [rest.pop(0) for _ in range(npairs)]
        phases = []
        if carried is not None:
            rs_out = rest.pop(0)
        if reduced is not None:
            sum_ref = rest.pop(0)
        if carried is not None:
            phases.append(_rs_phases(carried, g_hbm, rs_out, *rest[:n_rs]))
        if reduced is not None:
            gathered, *sems = rest[n_rs:]
            phases.append(_ag_phases(part, gathered, *sems, sum_ref=sum_ref))
        step = pl.program_id(0)

        @pl.when(step == 0)
        def _():
            for o in outs:
                o[...] = jnp.zeros(o.shape, F32)
            for start, _, _ in phases:
                start()

        parts = [_dot_tn(ab[2 * i][...].astype(BF16), ab[2 * i + 1][...].astype(BF16)) for i in range(npairs)]
        for out, part in zip(outs, parts):
            out[...] += part

        @pl.when(step == min(2, steps - 1))
        def _():
            for _, forward, _ in phases:
                forward()

        @pl.when(step == steps - 1)
        def _():
            for _, _, finish in phases:
                finish()

    in_specs, out_specs, out_shape = [], [], []
    for a, b in pairs:
        m, (n, col) = a.shape[1], b_cols or (b.shape[1], 0)
        in_specs += [pl.BlockSpec((bt, m), lambda i: (i, 0)), pl.BlockSpec((bt, n), lambda i, col=col: (i, col))]
        out_specs.append(pl.BlockSpec((m, n), lambda i: (0, 0)))
        out_shape.append(jax.ShapeDtypeStruct((m, n), F32))
    args = [t for p in pairs for t in p]
    scratch = []
    if carried is not None:
        in_specs.append(pl.BlockSpec(memory_space=pl.ANY))
        out_specs.append(pl.BlockSpec((carried.rows, carried.width), lambda i: (0, 0)))
        out_shape.append(jax.ShapeDtypeStruct((carried.rows, carried.width), F32))
        args.append(carried.array)
        scratch += _rs_scratch(carried)
    if reduced is not None:
        in_specs.append(pl.BlockSpec(memory_space=pl.ANY))
        out_specs.append(pl.BlockSpec(reduced.shape, lambda i: (0, 0)))
        out_shape.append(jax.ShapeDtypeStruct(reduced.shape, F32))
        args.append(reduced)
        scratch += [pltpu.VMEM((N_DEV * reduced.shape[0], reduced.shape[1]), F32)] + AG_SEMS
    return pl.pallas_call(
        body, name=name, grid=(steps,), in_specs=in_specs, out_specs=out_specs, out_shape=out_shape,
        scratch_shapes=scratch, compiler_params=_params(56, ("arbitrary",)))(*args)


def _adamw(triples):
    n = len(triples)

    def body(*refs):
        ins, outs = refs[:4 * n], refs[4 * n:]
        for i in range(n):
            w, g, m, v = (r[...] for r in ins[4 * i:4 * i + 4])
            m = ADAM_B1 * m + (1.0 - ADAM_B1) * g
            v = ADAM_B2 * v + (1.0 - ADAM_B2) * jnp.square(g)
            m_hat = m / (1.0 - ADAM_B1 ** ADAM_STEP)
            v_hat = v / (1.0 - ADAM_B2 ** ADAM_STEP)
            outs[3 * i][...] = -ADAM_LR * (m_hat / (jnp.sqrt(v_hat) + ADAM_EPS) + ADAM_WD * w)
            outs[3 * i + 1][...] = m
            outs[3 * i + 2][...] = v

    flat = [a for t in triples for a in t]
    vmem = pl.BlockSpec(memory_space=pltpu.VMEM)
    res = pl.pallas_call(
        body, name="adamw", in_specs=[vmem] * len(flat), out_specs=[vmem] * (3 * n),
        out_shape=[jax.ShapeDtypeStruct(t[0].shape, F32) for t in triples for _ in range(3)],
        compiler_params=_params(48))(*flat)
    return [tuple(res[3 * i:3 * i + 3]) for i in range(n)]


def _shard_slab(w_in, w_uq, w_ukv):
    mixed = jnp.concatenate(
        [_pad_rows(w_uq.T, R_MIX), w_ukv.T, jnp.zeros((R_MIX, 1024 - Q_LORA - KV_LORA), F32)], axis=1)
    return jnp.concatenate([mixed, w_in.T, jnp.zeros((SLAB_ROWS - O_IN - R_IN, 1024), F32)], axis=0)


def _unpack_weights(slabs):
    uq = slabs[:, O_MIX:O_MIX + R_UQ, :Q_LORA].reshape(HEADS, QK_DIM, Q_LORA)
    wqt = jnp.pad(uq, ((0, 0), (0, HEAD_PAD - QK_DIM), (0, 0))).reshape(HEADS * HEAD_PAD, Q_LORA)
    ukv = slabs[:, O_MIX:O_MIX + R_MIX, Q_LORA:Q_LORA + KV_LORA].reshape(HEADS, 2, NOPE, KV_LORA)
    wkvt = ukv.transpose(1, 0, 2, 3).reshape(2 * MLA_W, KV_LORA)
    raw = slabs[:, O_IN:O_IN + R_IN].reshape(IN_W, D_MODEL)
    w1t = jnp.concatenate([raw[:768 + ROPE], jnp.zeros((128 - ROPE, D_MODEL), BF16), raw[768 + ROPE:]], axis=0)
    return w1t, wqt, wkvt


def _mixed_band(g_wqt, g_wkvt):
    uq = g_wqt.reshape(HEADS, HEAD_PAD, Q_LORA)[:, :QK_DIM].reshape(N_DEV, R_UQ, Q_LORA)
    uq = jnp.pad(uq, ((0, 0), (0, R_MIX - R_UQ), (0, 0)))
    ukv = g_wkvt.reshape(2, HEADS, NOPE, KV_LORA).transpose(1, 0, 2, 3).reshape(N_DEV, R_MIX, KV_LORA)
    return jnp.concatenate([uq, ukv, jnp.zeros((N_DEV, R_MIX, 1024 - Q_LORA - KV_LORA), F32)], axis=2)


def _rope_rows():
    half = ROPE // 2
    inv_freq = ROPE_THETA ** (-jnp.arange(half, dtype=F32) / half)
    zero, one = jnp.zeros((half,), F32), jnp.ones((half,), F32)
    rows = [jnp.concatenate(r) for r in (
        (inv_freq, inv_freq, zero, zero), (one, one, zero, zero), (-one, zero, zero, zero), (zero, one, zero, zero))]
    return jnp.stack(rows + [jnp.zeros((128,), F32)] * 4)


def _pad_rows(a, rows):
    return jnp.pad(a, ((0, rows - a.shape[0]), (0, 0)))


def kernel(x, positions, w_in, q_norm_g, w_uq, kv_norm_g, w_ukv, pool_w, pool_scale, w_out, ln_g, ln_b, loss_target, m_w_in, m_q_norm_g, m_w_uq, m_kv_norm_g, m_w_ukv, m_pool_w, m_pool_scale, m_w_out, m_ln_g, m_ln_b, v_w_in, v_q_norm_g, v_w_uq, v_kv_norm_g, v_w_ukv, v_pool_w, v_pool_scale, v_w_out, v_ln_g, v_ln_b):
    bb, ss, _ = x.shape
    tt = bb * ss
    atile = min(512, ss)
    nblk = ss // atile

    slab = _shard_slab(w_in, w_uq, w_ukv).astype(BF16)
    slabs = _all_gather(slab, "gather_weights").reshape(N_DEV, SLAB_ROWS, 1024)
    w1, wq, wkv = _unpack_weights(slabs)

    gq, gkv = q_norm_g.reshape(1, Q_LORA), kv_norm_g.reshape(1, KV_LORA)
    ps = pool_scale.reshape(1, POOL_W)
    pos_col = positions.reshape(bb, ss, 1)
    pos_blk = positions.reshape(bb, nblk, 1, atile)
    pmax = jnp.max(positions.reshape(bb, 2 * nblk, atile // 2), axis=-1).reshape(-1)
    pmin = jnp.min(positions.reshape(bb, 2 * nblk, atile // 2), axis=-1).reshape(-1)

    x2 = x.reshape(tt, D_MODEL)
    xq, xkv, ga, u, gb, qn, kvn, qcat, kcat, v, rc, rsa, rsb, wo = _fwd_in(
        x2, w1, gq, gkv, wq, wkv, pos_col.reshape(tt, 1), _rope_rows(), w_out.astype(BF16), atile)
    as3 = lambda a: a.reshape(bb, ss, a.shape[-1])
    qcat, kcat, v = as3(qcat), as3(kcat), as3(v)
    o, lse = _attn_fwd(qcat, kcat, v, pos_col, pos_blk, pmax, pmin, atile, atile)
    dz, ycat, do, dga, dgb, dpc, dl, loss_p, g_lng, g_lnb, g_ps, g_pw = _post(
        x, loss_target, o, as3(ga), as3(u), as3(gb), wo, pool_w.astype(BF16), ps, ln_g, ln_b, atile)

    bt = min(1024, tt)
    rows5 = lambda a: a.reshape(bb, HEADS, nblk, 1, atile)
    rc, rsa, rsb = as3(rc), as3(rsa), as3(rsb)
    dqp, dkn, dv, dkr = _attn_bwd(
        qcat, kcat, v, do, rows5(lse), rows5(dl), rc, rsa, rsb, pos_col, pos_blk, pmax, pmin, atile, atile)
    g_wq, g_wkn, g_wv = _grad_w(
        [(dqp.reshape(tt, HEADS * HEAD_PAD), qn), (dkn.reshape(tt, MLA_W), kvn), (dv.reshape(tt, MLA_W), kvn)],
        min(2 * bt, tt), "grad_w_uqkv")
    g_wkv = jnp.concatenate([g_wkn, g_wv], axis=0)
    grad_x, dh, g_gq, g_gkv = _bwd_mid(
        dqp, dkn, dv, dkr, as3(xq), as3(xkv), rc, rsa, rsb, wq, wkv, gq, gkv, dga, dgb, dpc, dz, w1, atile)
    small = jnp.concatenate(
        [_pad_rows(g_lng.reshape(8, 128), 8), _pad_rows(g_lnb.reshape(8, 128), 8), _pad_rows(g_gq.reshape(4, 128), 8),
         _pad_rows(g_gkv.reshape(2, 128), 8), _pad_rows(g_ps.reshape(4, 128), 8), g_pw.reshape(POOL_G * POOL_C, 128),
         loss_p], axis=0)
    dh2, half = dh.reshape(tt, IN_WP), D_MODEL // 2
    g_w1a, rs_mix, small = _grad_w([(dh2, x2)], bt, "grad_w_in_a", b_cols=(half, 0),
                                   carried=_blocks(_mixed_band(g_wq, g_wkv)), reduced=small)
    g_w1b, rs_in_a = _grad_w([(dh2, x2)], bt, "grad_w_in_b", b_cols=(half, 1), carried=_w_in_blocks(g_w1a))
    g_wo, rs_in_b = _grad_w([(ycat.reshape(tt, D_MODEL), dz.reshape(tt, D_MODEL))], bt, "grad_w_out",
                            carried=_w_in_blocks(g_w1b))
    rs_in = jnp.concatenate([rs_in_a, rs_in_b], axis=1)
    rs_out = _reduce_scatter(_blocks(g_wo.reshape(N_DEV, R_OUT, D_MODEL)), "reduce_scatter_w_out")
    loss = small[40 + POOL_G * POOL_C, 0]
    grads = {
        "w_in": rs_in[:R_IN],
        "q_norm_g": small[16:20].reshape(1, Q_LORA),
        "w_uq": rs_mix[:R_UQ, :Q_LORA],
        "kv_norm_g": small[24:26].reshape(1, KV_LORA),
        "w_ukv": rs_mix[:, Q_LORA:Q_LORA + KV_LORA].T,
        "pool_w": small[40:40 + POOL_G * POOL_C],
        "pool_scale": small[32:36].reshape(1, POOL_W),
        "w_out": rs_out,
        "ln_g": small[0:8].reshape(1, D_MODEL),
        "ln_b": small[8:16].reshape(1, D_MODEL),
    }
    transposed = ("w_in", "w_uq")

    names = ["w_in", "q_norm_g", "w_uq", "kv_norm_g", "w_ukv", "pool_w", "pool_scale", "w_out", "ln_g", "ln_b"]
    weights = dict(w_in=w_in, q_norm_g=q_norm_g, w_uq=w_uq, kv_norm_g=kv_norm_g, w_ukv=w_ukv, pool_w=pool_w,
                   pool_scale=pool_scale, w_out=w_out, ln_g=ln_g, ln_b=ln_b)
    moms = dict(w_in=(m_w_in, v_w_in), q_norm_g=(m_q_norm_g, v_q_norm_g), w_uq=(m_w_uq, v_w_uq),
                kv_norm_g=(m_kv_norm_g, v_kv_norm_g), w_ukv=(m_w_ukv, v_w_ukv), pool_w=(m_pool_w, v_pool_w),
                pool_scale=(m_pool_scale, v_pool_scale), w_out=(m_w_out, v_w_out), ln_g=(m_ln_g, v_ln_g),
                ln_b=(m_ln_b, v_ln_b))
    as2 = lambda a, n: a.T if n in transposed else a.reshape(grads[n].shape)
    upd = _adamw([(as2(weights[n], n), grads[n], as2(moms[n][0], n), as2(moms[n][1], n)) for n in names])
    shaped = lambda a, n: a.T if n in transposed else a.reshape(weights[n].shape)
    return (loss, grad_x,
            *[shaped(grads[n], n) for n in names],
            *[shaped(upd[i][0], n) for i, n in enumerate(names)],
            *[shaped(upd[i][1], n) for i, n in enumerate(names)],
            *[shaped(upd[i][2], n) for i, n in enumerate(names)])
```
